```python
import math
import jax, jax.numpy as jnp
from jax import lax
import numpy as np

D_MODEL = 1024
BATCH = 16
SEQ = 2048
DEPTH = 1

E_A = D_MODEL
CONV_K = 31
N_GROUPS_A = 8
E_B = D_MODEL
CHUNK = 128
N_HEADS_B = 8
HEAD_DIM_B = E_B // N_HEADS_B
N_BRANCH = 2
COLS_A_IN = 2 * E_A
COLS_A_GATE = E_A
COLS_B_UV = 2 * E_B
COLS_B_GATE = E_B
COLS_MERGE = N_BRANCH * D_MODEL
D_IN = COLS_A_IN + COLS_A_GATE + COLS_B_UV + COLS_B_GATE + COLS_MERGE
DEEPNORM_ALPHA = (2.0 * DEPTH) ** 0.25
DEEPNORM_BETA = (8.0 * DEPTH) ** -0.25
LN_EPS = 1e-5

kernel_name = "hybrid_conformer_conv_chunked_gmlp_gated_deepnorm"


def _layer_norm(x, g, b):
    xf = x.astype(jnp.float32)
    mu = jnp.mean(xf, axis=-1, keepdims=True)
    var = jnp.mean(jnp.square(xf - mu), axis=-1, keepdims=True)
    y = (xf - mu) * lax.rsqrt(var + LN_EPS)
    return (y * g.astype(jnp.float32) + b.astype(jnp.float32)).astype(x.dtype)


def _group_norm_channels(x, g, b, n_groups):
    shp = x.shape
    xf = x.astype(jnp.float32).reshape(shp[:-1] + (n_groups, shp[-1] // n_groups))
    mu = jnp.mean(xf, axis=-1, keepdims=True)
    var = jnp.mean(jnp.square(xf - mu), axis=-1, keepdims=True)
    y = ((xf - mu) * lax.rsqrt(var + LN_EPS)).reshape(shp)
    return (y * g.astype(jnp.float32) + b.astype(jnp.float32)).astype(x.dtype)


def _conformer_conv_branch(a_in, a_gate, conv_w, conv_b, gn_g, gn_b, w_pa):
    val, gate = jnp.split(a_in, 2, axis=-1)
    h = val * jax.nn.sigmoid(gate)
    h = lax.conv_general_dilated(
        h, conv_w[:, None, :].astype(h.dtype),
        window_strides=(1,), padding=[(CONV_K - 1, 0)],
        dimension_numbers=("NWC", "WIO", "NWC"),
        feature_group_count=E_A) + conv_b
    h = _group_norm_channels(h, gn_g, gn_b, N_GROUPS_A)
    h = jax.nn.silu(h) * jax.nn.silu(a_gate)
    return jnp.einsum("bse,ed->bsd", h, w_pa)


def _chunked_gmlp_branch(b_uv, b_gate, ln_v_g, ln_v_b, w_spatial, b_spatial, w_pb):
    bsz, seq = b_uv.shape[0], b_uv.shape[1]
    z = jax.nn.gelu(b_uv)
    u, v = jnp.split(z, 2, axis=-1)
    v = _layer_norm(v, ln_v_g, ln_v_b)
    v = v.reshape(bsz, seq // CHUNK, CHUNK, N_HEADS_B, HEAD_DIM_B)
    causal = jnp.tril(jnp.ones((CHUNK, CHUNK), dtype=bool))
    ws = jnp.where(causal[None], w_spatial, jnp.zeros((), w_spatial.dtype))
    v_mix = jnp.einsum("hts,bnshd->bnthd", ws, v) + b_spatial.T[None, None, :, :, None]
    s = u * v_mix.reshape(bsz, seq, E_B)
    s = s * jax.nn.silu(b_gate)
    return jnp.einsum("bse,ed->bsd", s, w_pb)


def _fwd_setup_inputs(seed: int = 0) -> dict:
    key = jax.random.key(seed)
    ks = jax.random.split(key, 20)
    f32 = jnp.float32
    nrm = lambda k, shp, sc: jax.random.normal(k, shp, f32) * sc
    return {
        "x": jax.random.normal(ks[0], (BATCH, SEQ, D_MODEL), f32),
        "w_in": nrm(ks[1], (D_MODEL, D_IN), D_MODEL ** -0.5),
        "b_in": nrm(ks[2], (D_IN,), 0.02),
        "conv_w": nrm(ks[3], (CONV_K, E_A), CONV_K ** -0.5),
        "conv_b": nrm(ks[4], (E_A,), 0.02),
        "gn_g": 1.0 + nrm(ks[5], (E_A,), 0.02),
        "gn_b": nrm(ks[6], (E_A,), 0.02),
        "ln_v_g": 1.0 + nrm(ks[7], (E_B,), 0.02),
        "ln_v_b": nrm(ks[8], (E_B,), 0.02),
        "w_spatial": nrm(ks[9], (N_HEADS_B, CHUNK, CHUNK), CHUNK ** -0.5),
        "b_spatial": 1.0 + nrm(ks[10], (N_HEADS_B, CHUNK), 0.1),
        "w_pa": nrm(ks[11], (E_A, D_MODEL), E_A ** -0.5),
        "w_pb": nrm(ks[12], (E_B, D_MODEL), E_B ** -0.5),
        "w_o": nrm(ks[13], (D_MODEL, D_MODEL), DEEPNORM_BETA * D_MODEL ** -0.5),
        "b_o": nrm(ks[14], (D_MODEL,), 0.02),
        "ln_out_g": 1.0 + nrm(ks[15], (D_MODEL,), 0.02),
        "ln_out_b": nrm(ks[16], (D_MODEL,), 0.02),
    }


def _fwd_reference(x, w_in, b_in, conv_w, conv_b, gn_g, gn_b, ln_v_g, ln_v_b,
              w_spatial, b_spatial, w_pa, w_pb, w_o, b_o, ln_out_g, ln_out_b):
    splits = np.cumsum([COLS_A_IN, COLS_A_GATE, COLS_B_UV, COLS_B_GATE]).tolist()
    for _ in range(DEPTH):
        proj = jnp.einsum("bsd,dk->bsk", x, w_in) + b_in
        a_in, a_gate, b_uv, b_gate, merge = jnp.split(proj, splits, axis=-1)
        y_a = _conformer_conv_branch(a_in, a_gate, conv_w, conv_b, gn_g, gn_b, w_pa)
        y_b = _chunked_gmlp_branch(b_uv, b_gate, ln_v_g, ln_v_b, w_spatial, b_spatial, w_pb)
        g_a, g_b = jnp.split(jax.nn.sigmoid(merge), 2, axis=-1)
        mixed = g_a * y_a + g_b * y_b
        sub = jnp.einsum("bsd,de->bse", mixed, w_o) + b_o
        x = _layer_norm(DEEPNORM_ALPHA * x + sub, ln_out_g, ln_out_b)
    return x


import jax as _jax
import jax.numpy as _jnp

TWIN_FORMAT = 'train_step'
FWD_PARAMS = ['x', 'w_in', 'b_in', 'conv_w', 'conv_b', 'gn_g', 'gn_b', 'ln_v_g', 'ln_v_b', 'w_spatial', 'b_spatial', 'w_pa', 'w_pb', 'w_o', 'b_o', 'ln_out_g', 'ln_out_b']
TWIN_WEIGHTS = ['w_in', 'b_in', 'conv_w', 'conv_b', 'gn_g', 'gn_b', 'ln_v_g', 'ln_v_b', 'w_spatial', 'b_spatial', 'w_pa', 'w_pb', 'w_o', 'b_o', 'ln_out_g', 'ln_out_b']
TWIN_DIFF_INPUT = 'x'
TWIN_INPUTS = ['x', 'w_in', 'b_in', 'conv_w', 'conv_b', 'gn_g', 'gn_b', 'ln_v_g', 'ln_v_b', 'w_spatial', 'b_spatial', 'w_pa', 'w_pb', 'w_o', 'b_o', 'ln_out_g', 'ln_out_b', 'loss_target', 'm_w_in', 'm_b_in', 'm_conv_w', 'm_conv_b', 'm_gn_g', 'm_gn_b', 'm_ln_v_g', 'm_ln_v_b', 'm_w_spatial', 'm_b_spatial', 'm_w_pa', 'm_w_pb', 'm_w_o', 'm_b_o', 'm_ln_out_g', 'm_ln_out_b', 'v_w_in', 'v_b_in', 'v_conv_w', 'v_conv_b', 'v_gn_g', 'v_gn_b', 'v_ln_v_g', 'v_ln_v_b', 'v_w_spatial', 'v_b_spatial', 'v_w_pa', 'v_w_pb', 'v_w_o', 'v_b_o', 'v_ln_out_g', 'v_ln_out_b']
TWIN_OUTPUTS = ['loss', 'grad_x', 'grad_w_in', 'grad_b_in', 'grad_conv_w', 'grad_conv_b', 'grad_gn_g', 'grad_gn_b', 'grad_ln_v_g', 'grad_ln_v_b', 'grad_w_spatial', 'grad_b_spatial', 'grad_w_pa', 'grad_w_pb', 'grad_w_o', 'grad_b_o', 'grad_ln_out_g', 'grad_ln_out_b', 'delta_w_in', 'delta_b_in', 'delta_conv_w', 'delta_conv_b', 'delta_gn_g', 'delta_gn_b', 'delta_ln_v_g', 'delta_ln_v_b', 'delta_w_spatial', 'delta_b_spatial', 'delta_w_pa', 'delta_w_pb', 'delta_w_o', 'delta_b_o', 'delta_ln_out_g', 'delta_ln_out_b', 'new_m_w_in', 'new_m_b_in', 'new_m_conv_w', 'new_m_conv_b', 'new_m_gn_g', 'new_m_gn_b', 'new_m_ln_v_g', 'new_m_ln_v_b', 'new_m_w_spatial', 'new_m_b_spatial', 'new_m_w_pa', 'new_m_w_pb', 'new_m_w_o', 'new_m_b_o', 'new_m_ln_out_g', 'new_m_ln_out_b', 'new_v_w_in', 'new_v_b_in', 'new_v_conv_w', 'new_v_conv_b', 'new_v_gn_g', 'new_v_gn_b', 'new_v_ln_v_g', 'new_v_ln_v_b', 'new_v_w_spatial', 'new_v_b_spatial', 'new_v_w_pa', 'new_v_w_pb', 'new_v_w_o', 'new_v_b_o', 'new_v_ln_out_g', 'new_v_ln_out_b']
TWIN_LEAF_KINDS = {'loss': 'loss', 'grad_x': 'grad_x', 'grad_w_in': 'grad_w', 'grad_b_in': 'grad_w', 'grad_conv_w': 'grad_w', 'grad_conv_b': 'grad_w', 'grad_gn_g': 'grad_w', 'grad_gn_b': 'grad_w', 'grad_ln_v_g': 'grad_w', 'grad_ln_v_b': 'grad_w', 'grad_w_spatial': 'grad_w', 'grad_b_spatial': 'grad_w', 'grad_w_pa': 'grad_w', 'grad_w_pb': 'grad_w', 'grad_w_o': 'grad_w', 'grad_b_o': 'grad_w', 'grad_ln_out_g': 'grad_w', 'grad_ln_out_b': 'grad_w', 'delta_w_in': 'delta_w', 'delta_b_in': 'delta_w', 'delta_conv_w': 'delta_w', 'delta_conv_b': 'delta_w', 'delta_gn_g': 'delta_w', 'delta_gn_b': 'delta_w', 'delta_ln_v_g': 'delta_w', 'delta_ln_v_b': 'delta_w', 'delta_w_spatial': 'delta_w', 'delta_b_spatial': 'delta_w', 'delta_w_pa': 'delta_w', 'delta_w_pb': 'delta_w', 'delta_w_o': 'delta_w', 'delta_b_o': 'delta_w', 'delta_ln_out_g': 'delta_w', 'delta_ln_out_b': 'delta_w', 'new_m_w_in': 'new_m', 'new_m_b_in': 'new_m', 'new_m_conv_w': 'new_m', 'new_m_conv_b': 'new_m', 'new_m_gn_g': 'new_m', 'new_m_gn_b': 'new_m', 'new_m_ln_v_g': 'new_m', 'new_m_ln_v_b': 'new_m', 'new_m_w_spatial': 'new_m', 'new_m_b_spatial': 'new_m', 'new_m_w_pa': 'new_m', 'new_m_w_pb': 'new_m', 'new_m_w_o': 'new_m', 'new_m_b_o': 'new_m', 'new_m_ln_out_g': 'new_m', 'new_m_ln_out_b': 'new_m', 'new_v_w_in': 'new_v', 'new_v_b_in': 'new_v', 'new_v_conv_w': 'new_v', 'new_v_conv_b': 'new_v', 'new_v_gn_g': 'new_v', 'new_v_gn_b': 'new_v', 'new_v_ln_v_g': 'new_v', 'new_v_ln_v_b': 'new_v', 'new_v_w_spatial': 'new_v', 'new_v_b_spatial': 'new_v', 'new_v_w_pa': 'new_v', 'new_v_w_pb': 'new_v', 'new_v_w_o': 'new_v', 'new_v_b_o': 'new_v', 'new_v_ln_out_g': 'new_v', 'new_v_ln_out_b': 'new_v'}


def _forward(args):
    return _fwd_reference(*[args[k] for k in FWD_PARAMS])


def _output_shape():
    out = _jax.eval_shape(lambda: _forward(_fwd_setup_inputs(0)))
    return out.shape, out.dtype

N_MICROBATCH = 1
ADAM_LR = 0.001
ADAM_B1 = 0.9
ADAM_B2 = 0.999
ADAM_EPS = 1e-08
ADAM_WD = 0.01
ADAM_STEP = 10
PER_EXAMPLE_BATCH_AXIS = {'x': 0, 'loss_target': 0}
SHARED_INPUTS = []
_WEIGHT_DTYPES = {'w_in': _jnp.float32, 'b_in': _jnp.float32, 'conv_w': _jnp.float32, 'conv_b': _jnp.float32, 'gn_g': _jnp.float32, 'gn_b': _jnp.float32, 'ln_v_g': _jnp.float32, 'ln_v_b': _jnp.float32, 'w_spatial': _jnp.float32, 'b_spatial': _jnp.float32, 'w_pa': _jnp.float32, 'w_pb': _jnp.float32, 'w_o': _jnp.float32, 'b_o': _jnp.float32, 'ln_out_g': _jnp.float32, 'ln_out_b': _jnp.float32}
MOMENT_SCALE = {'w_in': 1.626697e-02, 'b_in': 1.759115e-02, 'conv_w': 1.760647e-02, 'conv_b': 3.790056e-02, 'gn_g': 2.076188e-02, 'gn_b': 1.783221e-02, 'ln_v_g': 1.337434e-02, 'ln_v_b': 1.363159e-02, 'w_spatial': 1.302067e-02, 'b_spatial': 1.876406e-02, 'w_pa': 1.726179e-02, 'w_pb': 2.319670e-02, 'w_o': 4.839993e-02, 'b_o': 3.569497e-01, 'ln_out_g': 3.195327e+01, 'ln_out_b': 7.584872e-01}


def _to_microbatches(a, axis):
    t = _jnp.moveaxis(a, axis, 0)
    t = t.reshape((N_MICROBATCH, t.shape[0] // N_MICROBATCH) + t.shape[1:])
    return _jnp.moveaxis(t, 1, axis + 1)


def setup_inputs(seed: int = 0) -> dict:
    inp = _fwd_setup_inputs(seed)
    key = _jax.random.fold_in(_jax.random.key(seed), 7919)
    shape, _ = _output_shape()
    out = dict(inp)
    out["loss_target"] = _jax.random.normal(_jax.random.fold_in(key, 0), shape, _jnp.float32)
    for i, name in enumerate(TWIN_WEIGHTS):
        w = inp[name].astype(_jnp.float32)
        if MOMENT_SCALE is None:
            s = _jnp.sqrt(_jnp.mean(_jnp.square(w)) + 1e-30)
        else:
            s = MOMENT_SCALE[name]
        km, kv = _jax.random.split(_jax.random.fold_in(key, i + 1))
        out[name] = w
        out["m_" + name] = s * _jax.random.normal(km, w.shape, _jnp.float32)
        out["v_" + name] = (s * s) * _jax.random.uniform(kv, w.shape, _jnp.float32, 0.5, 1.5)
    if N_MICROBATCH > 1:
        for name, axis in PER_EXAMPLE_BATCH_AXIS.items():
            out[name] = _to_microbatches(out[name], axis)
    return {'x': out['x'], 'w_in': out['w_in'], 'b_in': out['b_in'], 'conv_w': out['conv_w'], 'conv_b': out['conv_b'], 'gn_g': out['gn_g'], 'gn_b': out['gn_b'], 'ln_v_g': out['ln_v_g'], 'ln_v_b': out['ln_v_b'], 'w_spatial': out['w_spatial'], 'b_spatial': out['b_spatial'], 'w_pa': out['w_pa'], 'w_pb': out['w_pb'], 'w_o': out['w_o'], 'b_o': out['b_o'], 'ln_out_g': out['ln_out_g'], 'ln_out_b': out['ln_out_b'], 'loss_target': out['loss_target'], 'm_w_in': out['m_w_in'], 'm_b_in': out['m_b_in'], 'm_conv_w': out['m_conv_w'], 'm_conv_b': out['m_conv_b'], 'm_gn_g': out['m_gn_g'], 'm_gn_b': out['m_gn_b'], 'm_ln_v_g': out['m_ln_v_g'], 'm_ln_v_b': out['m_ln_v_b'], 'm_w_spatial': out['m_w_spatial'], 'm_b_spatial': out['m_b_spatial'], 'm_w_pa': out['m_w_pa'], 'm_w_pb': out['m_w_pb'], 'm_w_o': out['m_w_o'], 'm_b_o': out['m_b_o'], 'm_ln_out_g': out['m_ln_out_g'], 'm_ln_out_b': out['m_ln_out_b'], 'v_w_in': out['v_w_in'], 'v_b_in': out['v_b_in'], 'v_conv_w': out['v_conv_w'], 'v_conv_b': out['v_conv_b'], 'v_gn_g': out['v_gn_g'], 'v_gn_b': out['v_gn_b'], 'v_ln_v_g': out['v_ln_v_g'], 'v_ln_v_b': out['v_ln_v_b'], 'v_w_spatial': out['v_w_spatial'], 'v_b_spatial': out['v_b_spatial'], 'v_w_pa': out['v_w_pa'], 'v_w_pb': out['v_w_pb'], 'v_w_o': out['v_w_o'], 'v_b_o': out['v_b_o'], 'v_ln_out_g': out['v_ln_out_g'], 'v_ln_out_b': out['v_ln_out_b']}


def _loss(weights, diff, rest, loss_target):
    with _jax.named_scope("forward"):
        args = {**rest, TWIN_DIFF_INPUT: diff, **{k: w.astype(_WEIGHT_DTYPES[k]) for k, w in weights.items()}}
        y = _forward(args)
    with _jax.named_scope("loss_head"):
        err = _jnp.square(y.astype(_jnp.float32) - loss_target)
        return 0.5 * _jnp.sum(_jnp.mean(err, axis=-1)) if err.ndim else 0.5 * err


def _adamw(w, g, m, v):
    m = ADAM_B1 * m + (1.0 - ADAM_B1) * g
    v = ADAM_B2 * v + (1.0 - ADAM_B2) * _jnp.square(g)
    m_hat = m / (1.0 - ADAM_B1 ** ADAM_STEP)
    v_hat = v / (1.0 - ADAM_B2 ** ADAM_STEP)
    delta = -ADAM_LR * (m_hat / (_jnp.sqrt(v_hat) + ADAM_EPS) + ADAM_WD * w)
    return delta, m, v


def reference(x, w_in, b_in, conv_w, conv_b, gn_g, gn_b, ln_v_g, ln_v_b, w_spatial, b_spatial, w_pa, w_pb, w_o, b_o, ln_out_g, ln_out_b, loss_target, m_w_in, m_b_in, m_conv_w, m_conv_b, m_gn_g, m_gn_b, m_ln_v_g, m_ln_v_b, m_w_spatial, m_b_spatial, m_w_pa, m_w_pb, m_w_o, m_b_o, m_ln_out_g, m_ln_out_b, v_w_in, v_b_in, v_conv_w, v_conv_b, v_gn_g, v_gn_b, v_ln_v_g, v_ln_v_b, v_w_spatial, v_b_spatial, v_w_pa, v_w_pb, v_w_o, v_b_o, v_ln_out_g, v_ln_out_b):
    given = dict(x=x, w_in=w_in, b_in=b_in, conv_w=conv_w, conv_b=conv_b, gn_g=gn_g, gn_b=gn_b, ln_v_g=ln_v_g, ln_v_b=ln_v_b, w_spatial=w_spatial, b_spatial=b_spatial, w_pa=w_pa, w_pb=w_pb, w_o=w_o, b_o=b_o, ln_out_g=ln_out_g, ln_out_b=ln_out_b, loss_target=loss_target, m_w_in=m_w_in, m_b_in=m_b_in, m_conv_w=m_conv_w, m_conv_b=m_conv_b, m_gn_g=m_gn_g, m_gn_b=m_gn_b, m_ln_v_g=m_ln_v_g, m_ln_v_b=m_ln_v_b, m_w_spatial=m_w_spatial, m_b_spatial=m_b_spatial, m_w_pa=m_w_pa, m_w_pb=m_w_pb, m_w_o=m_w_o, m_b_o=m_b_o, m_ln_out_g=m_ln_out_g, m_ln_out_b=m_ln_out_b, v_w_in=v_w_in, v_b_in=v_b_in, v_conv_w=v_conv_w, v_conv_b=v_conv_b, v_gn_g=v_gn_g, v_gn_b=v_gn_b, v_ln_v_g=v_ln_v_g, v_ln_v_b=v_ln_v_b, v_w_spatial=v_w_spatial, v_b_spatial=v_b_spatial, v_w_pa=v_w_pa, v_w_pb=v_w_pb, v_w_o=v_w_o, v_b_o=v_b_o, v_ln_out_g=v_ln_out_g, v_ln_out_b=v_ln_out_b)
    weights = {n: given[n] for n in TWIN_WEIGHTS}
    shared = {n: given[n] for n in SHARED_INPUTS}
    per_example = {n: given[n] for n in ['x']}
    grad_fn = _jax.value_and_grad(_loss, argnums=(0, 1))

    def one_microbatch(ex, loss_target):
        ex = dict(ex)
        diff = ex.pop(TWIN_DIFF_INPUT)
        return grad_fn(weights, diff, {**shared, **ex}, loss_target)

    if N_MICROBATCH == 1:
        loss, (grad_w, grad_x) = one_microbatch(per_example, given["loss_target"])
    else:
        def body(carry, xs):
            loss_sum, grad_sum = carry
            l_k, (gw_k, gx_k) = one_microbatch(xs[0], xs[1])
            with _jax.named_scope("update"):
                return (loss_sum + l_k, _jax.tree.map(_jnp.add, grad_sum, gw_k)), gx_k

        init = (_jnp.zeros((), _jnp.float32), _jax.tree.map(_jnp.zeros_like, weights))
        (loss, grad_w), grad_x = _jax.lax.scan(body, init, (per_example, given["loss_target"]))
    with _jax.named_scope("update"):
        delta_w, new_m, new_v = {}, {}, {}
        for n in TWIN_WEIGHTS:
            delta_w[n], new_m[n], new_v[n] = _adamw(weights[n], grad_w[n], given["m_" + n], given["v_" + n])
    return (loss, grad_x, *[grad_w[n] for n in TWIN_WEIGHTS], *[delta_w[n] for n in TWIN_WEIGHTS],
            *[new_m[n] for n in TWIN_WEIGHTS], *[new_v[n] for n in TWIN_WEIGHTS])
```

```python
import functools
import math

import jax
import jax.numpy as jnp
from jax import lax
from jax.experimental import pallas as pl
from jax.experimental.pallas import tpu as pltpu

D = 1024
N_GROUPS = 8
GROUP_W = D // N_GROUPS
CHUNK = 128
CONV_K = 31
HALO = 32
D_IN = 8 * D
N_CHIPS = 4
W_BLOCK = D_IN // N_CHIPS
ALPHA = 2.0 ** 0.25
LN_EPS = 1e-5
ADAM_LR, ADAM_B1, ADAM_B2, ADAM_EPS, ADAM_WD, ADAM_STEP = 0.001, 0.9, 0.999, 1e-08, 0.01, 10

TOKEN_TILE = 256
VMEM_LIMIT = 56 * 1024 * 1024
MESH = pl.DeviceIdType.MESH
F32, BF16 = jnp.float32, jnp.bfloat16


def _sigmoid(x):
    return 1.0 / (1.0 + jnp.exp(-x))


def _gelu(x):
    c = math.sqrt(2.0 / math.pi)
    t = jnp.tanh(c * (x + 0.044715 * (x * x * x)))
    return x * (0.5 * (1.0 + t))


def _gelu_and_grad(x):
    c = math.sqrt(2.0 / math.pi)
    x2 = x * x
    t = jnp.tanh(c * (x + 0.044715 * (x2 * x)))
    cdf = 0.5 * (1.0 + t)
    return x * cdf, cdf + 0.5 * x * (1.0 - t * t) * (c * (1.0 + 3.0 * 0.044715 * x2))


def _norm_stats(v):
    mu = jnp.mean(v, axis=-1, keepdims=True)
    vc = v - mu
    var = jnp.mean(vc * vc, axis=-1, keepdims=True)
    rstd = lax.rsqrt(var + LN_EPS)
    return vc * rstd, rstd


def _norm_bwd(dxhat, xhat, rstd):
    m1 = jnp.mean(dxhat, axis=-1, keepdims=True)
    m2 = jnp.mean(dxhat * xhat, axis=-1, keepdims=True)
    return rstd * (dxhat - m1 - xhat * m2)


def _dot(a, b):
    return jnp.dot(a, b, preferred_element_type=F32)


def _dot_nt(a, b):
    return lax.dot_general(a, b, (((1,), (1,)), ((), ())), preferred_element_type=F32)


def _dot_tn(a, b):
    return lax.dot_general(a, b, (((0,), (0,)), ((), ())), preferred_element_type=F32)


def _colsum(v):
    return jnp.sum(v, axis=0, keepdims=True)


def _full(shape):
    return pl.BlockSpec(shape, lambda *_: (0,) * len(shape))


def _resident(shape):
    return pl.BlockSpec(shape, lambda *_: (0,) * len(shape), pipeline_mode=pl.Buffered(1))


def _params(*sem):
    return pltpu.CompilerParams(dimension_semantics=sem, vmem_limit_bytes=VMEM_LIMIT)


def _cast_bf16(w, rows):
    r, c = w.shape

    def body(w_ref, o_ref):
        o_ref[...] = w_ref[...].astype(BF16)

    return pl.pallas_call(
        body, name="cast_bf16", grid=(2, r // 2 // rows),
        in_specs=[pl.BlockSpec((rows, c), lambda h, i: (h * (r // 2 // rows) + i, 0))],
        out_specs=pl.BlockSpec((None, rows, c), lambda h, i: (h, i, 0)),
        out_shape=jax.ShapeDtypeStruct((2, r // 2, c), BF16),
        compiler_params=_params("parallel", "parallel"),
    )(w)


def _position():
    x, y, c = lax.axis_index("x"), lax.axis_index("y"), lax.axis_index("c")
    return x, y, c, 2 * x + y


def _other_chips(x, y):
    return [(1 - x, y), (x, 1 - y), (1 - x, 1 - y)]


def _any_specs(n):
    return [pl.BlockSpec(memory_space=pl.ANY)] * n


def _all_gather_weights(shards):
    n = len(shards)

    def body(*refs):
        ins, outs = refs[:n], refs[n:2 * n]
        send, recv, fsend, frecv, lsem = refs[2 * n:]
        x, y, c, k = _position()
        chips = _other_chips(x, y)
        sibling = (x, y, 1 - c)
        local = [pltpu.make_async_copy(ins[a], outs[a].at[k], lsem.at[a]) for a in range(n)]
        for cp in local:
            cp.start()
        first = []
        for a in range(n):
            for r, (cx, cy) in enumerate(chips):
                cp = pltpu.make_async_remote_copy(
                    src_ref=ins[a].at[c], dst_ref=outs[a].at[k, c],
                    send_sem=send.at[3 * a + r], recv_sem=recv.at[3 * a + r],
                    device_id=(cx, cy, c), device_id_type=MESH)
                cp.start()
                first.append(cp)
        passed = []
        for a in range(n):
            for r, (cx, cy) in enumerate(chips):
                kr = 2 * cx + cy
                pltpu.make_async_remote_copy(
                    src_ref=ins[a].at[c], dst_ref=outs[a].at[kr, c],
                    send_sem=send.at[3 * a + r], recv_sem=recv.at[3 * a + r],
                    device_id=(cx, cy, c), device_id_type=MESH).wait_recv()
                cp = pltpu.make_async_remote_copy(
                    src_ref=outs[a].at[kr, c], dst_ref=outs[a].at[kr, c],
                    send_sem=fsend.at[3 * a + r], recv_sem=frecv.at[3 * a + r],
                    device_id=sibling, device_id_type=MESH)
                cp.start()
                passed.append(cp)
        for a in range(n):
            for r, (cx, cy) in enumerate(chips):
                kr = 2 * cx + cy
                pltpu.make_async_remote_copy(
                    src_ref=outs[a].at[kr, 1 - c], dst_ref=outs[a].at[kr, 1 - c],
                    send_sem=fsend.at[3 * a + r], recv_sem=frecv.at[3 * a + r],
                    device_id=sibling, device_id_type=MESH).wait_recv()
        for cp in first + passed:
            cp.wait_send()
        for cp in local:
            cp.wait()

    return pl.pallas_call(
        body, name="all_gather_weights",
        in_specs=_any_specs(n), out_specs=_any_specs(n),
        out_shape=[jax.ShapeDtypeStruct((N_CHIPS,) + s.shape, s.dtype) for s in shards],
        scratch_shapes=[pltpu.SemaphoreType.DMA((3 * n,)), pltpu.SemaphoreType.DMA((3 * n,)),
                        pltpu.SemaphoreType.DMA((3 * n,)), pltpu.SemaphoreType.DMA((3 * n,)),
                        pltpu.SemaphoreType.DMA((n,))],
    )(*shards)


def _exchange_halves(grads):
    n = len(grads)

    def body(*refs):
        ins, outs = refs[:n], refs[n:2 * n]
        send, recv = refs[2 * n:]
        x, y, c, _ = _position()
        cps = []
        for a in range(n):
            cp = pltpu.make_async_remote_copy(
                src_ref=ins[a].at[1 - c], dst_ref=outs[a], send_sem=send.at[a], recv_sem=recv.at[a],
                device_id=(x, y, 1 - c), device_id_type=MESH)
            cp.start()
            cps.append(cp)
        for cp in cps:
            cp.wait_recv()
        for cp in cps:
            cp.wait_send()

    return pl.pallas_call(
        body, name="rs_exchange_halves",
        in_specs=_any_specs(n), out_specs=_any_specs(n),
        out_shape=[jax.ShapeDtypeStruct(g.shape[1:], g.dtype) for g in grads],
        scratch_shapes=[pltpu.SemaphoreType.DMA((n,)), pltpu.SemaphoreType.DMA((n,))],
    )(*grads)


def _scatter_to_chips(parts):
    n = len(parts)

    def body(*refs):
        ins, outs = refs[:n], refs[n:2 * n]
        send, recv = refs[2 * n:]
        x, y, c, _ = _position()
        cps = []
        for a in range(n):
            for r, (cx, cy) in enumerate(_other_chips(x, y)):
                cp = pltpu.make_async_remote_copy(
                    src_ref=ins[a].at[2 * cx + cy], dst_ref=outs[a].at[r],
                    send_sem=send.at[3 * a + r], recv_sem=recv.at[3 * a + r],
                    device_id=(cx, cy, c), device_id_type=MESH)
                cp.start()
                cps.append(cp)
        for cp in cps:
            cp.wait_recv()
        for cp in cps:
            cp.wait_send()

    return pl.pallas_call(
        body, name="rs_scatter_to_chips",
        in_specs=_any_specs(n), out_specs=_any_specs(n),
        out_shape=[jax.ShapeDtypeStruct((3,) + p.shape[1:], p.dtype) for p in parts],
        scratch_shapes=[pltpu.SemaphoreType.DMA((3 * n,)), pltpu.SemaphoreType.DMA((3 * n,))],
    )(*parts)


def _share_results(halves, small):
    n = len(halves)

    def body(*refs):
        ins, small_in = refs[:n], refs[n]
        outs, small_out = refs[n + 1:2 * n + 1], refs[2 * n + 1]
        send, recv, ssend, srecv, lsem = refs[2 * n + 2:]
        x, y, c, k = _position()
        local = [pltpu.make_async_copy(ins[a], outs[a].at[c], lsem.at[a]) for a in range(n)]
        local.append(pltpu.make_async_copy(small_in, small_out.at[k, c], lsem.at[n]))
        for cp in local:
            cp.start()
        cps = []
        for a in range(n):
            cp = pltpu.make_async_remote_copy(
                src_ref=ins[a], dst_ref=outs[a].at[c], send_sem=send.at[a], recv_sem=recv.at[a],
                device_id=(x, y, 1 - c), device_id_type=MESH)
            cp.start()
            cps.append(cp)
        waits = []
        for p in range(1, 8):
            px, py, pc = x ^ (p >> 2), y ^ ((p >> 1) & 1), c ^ (p & 1)
            cp = pltpu.make_async_remote_copy(
                src_ref=small_in, dst_ref=small_out.at[k, c], send_sem=ssend.at[p - 1], recv_sem=srecv.at[p - 1],
                device_id=(px, py, pc), device_id_type=MESH)
            cp.start()
            cps.append(cp)
            waits.append(pltpu.make_async_remote_copy(
                src_ref=small_in, dst_ref=small_out.at[2 * px + py, pc], send_sem=ssend.at[p - 1],
                recv_sem=srecv.at[p - 1], device_id=(px, py, pc), device_id_type=MESH))
        for a in range(n):
            pltpu.make_async_remote_copy(
                src_ref=ins[a], dst_ref=outs[a].at[1 - c], send_sem=send.at[a], recv_sem=recv.at[a],
                device_id=(x, y, 1 - c), device_id_type=MESH).wait_recv()
        for w in waits:
            w.wait_recv()
        for cp in cps:
            cp.wait_send()
        for cp in local:
            cp.wait()

    return pl.pallas_call(
        body, name="rs_share_results",
        in_specs=_any_specs(n + 1), out_specs=_any_specs(n + 1),
        out_shape=[jax.ShapeDtypeStruct((2,) + h.shape, h.dtype) for h in halves]
        + [jax.ShapeDtypeStruct((N_CHIPS, 2) + small.shape, small.dtype)],
        scratch_shapes=[pltpu.SemaphoreType.DMA((n,)), pltpu.SemaphoreType.DMA((n,)),
                        pltpu.SemaphoreType.DMA((7,)), pltpu.SemaphoreType.DMA((7,)),
                        pltpu.SemaphoreType.DMA((n + 1,))],
    )(*halves, small)


def _row_tile(r, c):
    t = max(8, min(r, (1 << 18) // c))
    while r % t:
        t //= 2
    return t


def _add_halves(g, b1, wire_dtype):
    _, _, r, c = g.shape
    t = _row_tile(r, c)

    def body(c_ref, g_ref, b_ref, q_ref, qw_ref):
        q = g_ref[...] + b_ref[...]
        q_ref[...] = q
        qw_ref[...] = q.astype(wire_dtype)

    core = lax.axis_index("c").astype(jnp.int32).reshape(1)
    return pl.pallas_call(
        body, name="rs_add_halves",
        grid_spec=pltpu.PrefetchScalarGridSpec(
            num_scalar_prefetch=1, grid=(N_CHIPS, r // t),
            in_specs=[pl.BlockSpec((None, None, t, c), lambda j, i, cr: (cr[0], j, i, 0)),
                      pl.BlockSpec((None, t, c), lambda j, i, cr: (j, i, 0))],
            out_specs=[pl.BlockSpec((None, t, c), lambda j, i, cr: (j, i, 0)),
                       pl.BlockSpec((None, t, c), lambda j, i, cr: (j, i, 0))]),
        out_shape=[jax.ShapeDtypeStruct((N_CHIPS, r, c), F32), jax.ShapeDtypeStruct((N_CHIPS, r, c), wire_dtype)],
        compiler_params=_params("parallel", "parallel"),
    )(core, g, b1)


def _add_chips(q, b2):
    _, r, c = q.shape
    t = _row_tile(r, c)

    def body(k_ref, q_ref, b_ref, f_ref):
        f_ref[...] = ((q_ref[...] + b_ref[0].astype(F32)) + b_ref[1].astype(F32)) + b_ref[2].astype(F32)

    chip = (2 * lax.axis_index("x") + lax.axis_index("y")).astype(jnp.int32).reshape(1)
    return pl.pallas_call(
        body, name="rs_add_chips",
        grid_spec=pltpu.PrefetchScalarGridSpec(
            num_scalar_prefetch=1, grid=(r // t,),
            in_specs=[pl.BlockSpec((None, t, c), lambda i, kr: (kr[0], i, 0)),
                      pl.BlockSpec((3, t, c), lambda i, kr: (0, i, 0))],
            out_specs=pl.BlockSpec((t, c), lambda i, kr: (i, 0))),
        out_shape=jax.ShapeDtypeStruct((r, c), F32),
        compiler_params=_params("parallel"),
    )(chip, q, b2)


def _adamw_math(w, g, m, v):
    m = ADAM_B1 * m + (1.0 - ADAM_B1) * g
    v = ADAM_B2 * v + (1.0 - ADAM_B2) * (g * g)
    m_hat = m / (1.0 - ADAM_B1 ** ADAM_STEP)
    v_hat = v / (1.0 - ADAM_B2 ** ADAM_STEP)
    delta = -ADAM_LR * (m_hat / (jnp.sqrt(v_hat) + ADAM_EPS) + ADAM_WD * w)
    return delta, m, v


def _adamw(w, g, m, v):
    r, c = w.shape
    t = _row_tile(r, c) if r % 8 == 0 else r

    def body(w_ref, g_ref, m_ref, v_ref, d_ref, nm_ref, nv_ref):
        d_ref[...], nm_ref[...], nv_ref[...] = _adamw_math(w_ref[...], g_ref[...], m_ref[...], v_ref[...])

    spec = pl.BlockSpec((t, c), lambda i: (i, 0))
    return pl.pallas_call(
        body, name="adamw", grid=(r // t,), in_specs=[spec] * 4, out_specs=[spec] * 3,
        out_shape=[jax.ShapeDtypeStruct((r, c), F32)] * 3, compiler_params=_params("parallel"),
    )(w, g, m, v)


def _adamw_small(w, parts, m, v, loss_row):
    r, c = w.shape

    def body(w_ref, p_ref, m_ref, v_ref, g_ref, d_ref, nm_ref, nv_ref, loss_ref):
        rows = r // 8
        for k in range(N_CHIPS):
            for core in range(2):
                g_ref[(core * N_CHIPS + k) * rows:(core * N_CHIPS + k + 1) * rows, :] = p_ref[2 * k + core]
        g = g_ref[...]
        d_ref[...], nm_ref[...], nv_ref[...] = _adamw_math(w_ref[...], g, m_ref[...], v_ref[...])
        lanes = g_ref[loss_row:loss_row + 8, :]
        loss_ref[...] = jnp.broadcast_to(jnp.sum(jnp.sum(lanes, axis=1, keepdims=True), axis=0, keepdims=True), (8, c))

    return pl.pallas_call(
        body, name="adamw_small",
        in_specs=[_full((r, c)), _full((8, r // 8, c)), _full((r, c)), _full((r, c))],
        out_specs=[_full((r, c))] * 4 + [_full((8, c))],
        out_shape=[jax.ShapeDtypeStruct((r, c), F32)] * 4 + [jax.ShapeDtypeStruct((8, c), F32)],
        compiler_params=_params(),
    )(w, parts, m, v)


def _proj(x, w4, b4):
    t = x.shape[0]
    tm = 1024

    def body(x_ref, w_ref, b_ref, p_ref, xb_ref):
        xb = x_ref[...].astype(BF16)
        xb_ref[...] = xb
        p_ref[...] = _dot(xb, w_ref[...]) + b_ref[...]

    return pl.pallas_call(
        body, name="proj", grid=(t // tm, N_CHIPS),
        in_specs=[pl.BlockSpec((tm, D), lambda i, j: (i, 0)),
                  pl.BlockSpec((None, D, W_BLOCK), lambda i, j: (j, 0, 0)),
                  pl.BlockSpec((None, 1, W_BLOCK), lambda i, j: (j, 0, 0))],
        out_specs=[pl.BlockSpec((tm, W_BLOCK), lambda i, j: (i, j)),
                   pl.BlockSpec((tm, D), lambda i, j: (i, 0))],
        out_shape=[jax.ShapeDtypeStruct((t, D_IN), F32), jax.ShapeDtypeStruct((t, D), BF16)],
        compiler_params=_params("arbitrary", "arbitrary"),
    )(x, w4, b4)


def _conv_taps(src_ref, w_ref, first_offset, step, bias, dst_ref, tm):
    rows = 64
    for g in range(N_GROUPS):
        cs = slice(g * GROUP_W, (g + 1) * GROUP_W)
        for rb in range(tm // rows):
            acc = jnp.zeros((rows, GROUP_W), F32) + (bias[:, cs] if bias is not None else 0.0)
            for k in range(CONV_K):
                acc = acc + w_ref[k:k + 1, cs] * src_ref[pl.ds(rb * rows + first_offset + step * k, rows), cs]
            dst_ref[rb * rows:(rb + 1) * rows, cs] = acc


def _spatial_mix(w_ref, v_bf, tm):
    rows = []
    for q in range(tm // CHUNK):
        cols = [_dot(w_ref[h], v_bf[q * CHUNK:(q + 1) * CHUNK, h * GROUP_W:(h + 1) * GROUP_W])
                for h in range(N_GROUPS)]
        rows.append(jnp.concatenate(cols, axis=1))
    return jnp.concatenate(rows, axis=0)


def _group_norm_fwd(h1, gn_g, gn_b):
    xhat, rstd = [], []
    for g in range(N_GROUPS):
        xh, rs = _norm_stats(h1[:, g * GROUP_W:(g + 1) * GROUP_W])
        xhat.append(xh)
        rstd.append(rs)
    xhat = jnp.concatenate(xhat, axis=1)
    return xhat * gn_g + gn_b, xhat, rstd


def _forward_tiles(p, x, tgt, wpa, wpb, wo, convw, vecs, ws, bsp, tiles_per_seq):
    t = x.shape[0]
    tm = TOKEN_TILE
    hb = tm // HALO

    def body(p_ref, ph_ref, x_ref, t_ref, wpa_ref, wpb_ref, wo_ref, cw_ref, vec_ref, ws_ref, bsp_ref,
             h1_ref, ya_ref, yb_ref, h3_ref, s_ref, mx_ref, dr_ref, drb_ref, acc_ref, he_ref):
        i = pl.program_id(0)
        conv_b, gn_g, gn_b, lnv_g, lnv_b, b_o, lno_g, lno_b = [vec_ref[j:j + 1, :] for j in range(8)]

        keep = jnp.where(i % tiles_per_seq == 0, 0.0, 1.0)
        he_ref[0:HALO, :] = ph_ref[:, 0:D] * _sigmoid(ph_ref[:, D:2 * D]) * keep
        he_ref[HALO:, :] = p_ref[:, 0:D] * _sigmoid(p_ref[:, D:2 * D])
        _conv_taps(he_ref, cw_ref, HALO - (CONV_K - 1), 1, conv_b, h1_ref, tm)
        h2, _, _ = _group_norm_fwd(h1_ref[...], gn_g, gn_b)
        a_gate = p_ref[:, 2 * D:3 * D]
        h3 = ((h2 * _sigmoid(h2)) * (a_gate * _sigmoid(a_gate))).astype(BF16)
        h3_ref[...] = h3
        ya = _dot(h3, wpa_ref[...])
        ya_ref[...] = ya

        u = _gelu(p_ref[:, 3 * D:4 * D])
        vhat, _ = _norm_stats(_gelu(p_ref[:, 4 * D:5 * D]))
        v1 = (vhat * lnv_g + lnv_b).astype(BF16)
        b_gate = p_ref[:, 5 * D:6 * D]
        vmix = _spatial_mix(ws_ref, v1, tm) + jnp.concatenate([bsp_ref[...]] * (tm // CHUNK), axis=0)
        s = (u * vmix * (b_gate * _sigmoid(b_gate))).astype(BF16)
        s_ref[...] = s
        yb = _dot(s, wpb_ref[...])
        yb_ref[...] = yb

        mixed = (_sigmoid(p_ref[:, 6 * D:7 * D]) * ya + _sigmoid(p_ref[:, 7 * D:8 * D]) * yb).astype(BF16)
        mx_ref[...] = mixed
        r = ALPHA * x_ref[...] + (_dot(mixed, wo_ref[...]) + b_o)
        xhat, rstd = _norm_stats(r)
        err = (xhat * lno_g + lno_b) - t_ref[...]
        dout = err * (1.0 / D)
        dr = _norm_bwd(dout * lno_g, xhat, rstd)
        dr_ref[...] = dr
        drb_ref[...] = dr.astype(BF16)

        @pl.when(i == 0)
        def _():
            acc_ref[...] = jnp.zeros_like(acc_ref)

        acc_ref[0:1, :] += _colsum(dout * xhat)
        acc_ref[1:2, :] += _colsum(dout)
        acc_ref[2:3, :] += _colsum(dr)
        acc_ref[3:4, :] += _colsum(err * err) * (0.5 / D)

    tile = lambda w: pl.BlockSpec((tm, w), lambda i: (i, 0))
    f32_out = jax.ShapeDtypeStruct((t, D), F32)
    bf_out = jax.ShapeDtypeStruct((t, D), BF16)
    return pl.pallas_call(
        body, name="forward_tiles", grid=(t // tm,),
        in_specs=[tile(D_IN),
                  pl.BlockSpec((HALO, 2 * D), lambda i: (jnp.maximum(i * hb - 1, 0), 0)),
                  tile(D), tile(D), _resident((D, D)), _resident((D, D)), _resident((D, D)), _full((HALO, D)), _full((8, D)),
                  _full((N_GROUPS, CHUNK, CHUNK)), _full((CHUNK, D))],
        out_specs=[tile(D)] * 8 + [_full((8, D))],
        out_shape=[f32_out, f32_out, f32_out, bf_out, bf_out, bf_out, f32_out, bf_out,
                   jax.ShapeDtypeStruct((8, D), F32)],
        scratch_shapes=[pltpu.VMEM((tm + HALO, D), F32)],
        compiler_params=_params("arbitrary"),
    )(p, p, x, tgt, wpa, wpb, wo, convw, vecs, ws, bsp)


def _backward_tiles(p, h1, ya, yb, drb, wpa, wpb, wo, vecs, ws, wst, bsp):
    t = h1.shape[0]
    tm = TOKEN_TILE

    def body(p_ref, h1_ref, ya_ref, yb_ref, drb_ref, wpa_ref, wpb_ref, wo_ref, vec_ref, ws_ref, wst_ref, bsp_ref,
             dh1_ref, dp_ref, dya_ref, dyb_ref, acc_ref, dbin_ref, dws_ref, dbsp_ref):
        i = pl.program_id(0)
        _, gn_g, gn_b, lnv_g, lnv_b = [vec_ref[j:j + 1, :] for j in range(5)]

        @pl.when(i == 0)
        def _():
            acc_ref[...] = jnp.zeros_like(acc_ref)
            dbin_ref[...] = jnp.zeros_like(dbin_ref)
            dws_ref[...] = jnp.zeros_like(dws_ref)
            dbsp_ref[...] = jnp.zeros_like(dbsp_ref)

        def emit(block, val):
            dbin_ref[0:1, block * D:(block + 1) * D] += _colsum(val)
            dp_ref[:, block * D:(block + 1) * D] = val.astype(BF16)

        dp_ref[:, 0:2 * D] = jnp.zeros((tm, 2 * D), BF16)
        dmixed = _dot_nt(drb_ref[...], wo_ref[...])
        ga = _sigmoid(p_ref[:, 6 * D:7 * D])
        gb = _sigmoid(p_ref[:, 7 * D:8 * D])
        dya = (dmixed * ga).astype(BF16)
        dyb = (dmixed * gb).astype(BF16)
        dya_ref[...] = dya
        dyb_ref[...] = dyb
        emit(6, dmixed * ya_ref[...] * (ga * (1.0 - ga)))
        emit(7, dmixed * yb_ref[...] * (gb * (1.0 - gb)))

        dh3 = _dot_nt(dya, wpa_ref[...])
        h2, xhat, rstd = _group_norm_fwd(h1_ref[...], gn_g, gn_b)
        sg = _sigmoid(h2)
        a_gate = p_ref[:, 2 * D:3 * D]
        sa = _sigmoid(a_gate)
        dh2 = dh3 * (a_gate * sa) * (sg * (1.0 + h2 * (1.0 - sg)))
        emit(2, dh3 * (h2 * sg) * (sa * (1.0 + a_gate * (1.0 - sa))))
        acc_ref[0:1, :] += _colsum(dh2 * xhat)
        acc_ref[1:2, :] += _colsum(dh2)
        dxhat = dh2 * gn_g
        for g in range(N_GROUPS):
            cs = slice(g * GROUP_W, (g + 1) * GROUP_W)
            dh1_ref[:, cs] = _norm_bwd(dxhat[:, cs], xhat[:, cs], rstd[g])

        ds = _dot_nt(dyb, wpb_ref[...])
        u_pre = p_ref[:, 3 * D:4 * D]
        u, du_dpre = _gelu_and_grad(u_pre)
        v0, dv_dpre = _gelu_and_grad(p_ref[:, 4 * D:5 * D])
        vhat, vrstd = _norm_stats(v0)
        v1 = (vhat * lnv_g + lnv_b).astype(BF16)
        vmix = _spatial_mix(ws_ref, v1, tm) + jnp.concatenate([bsp_ref[...]] * (tm // CHUNK), axis=0)
        b_gate = p_ref[:, 5 * D:6 * D]
        sb = _sigmoid(b_gate)
        silu_b = b_gate * sb
        emit(3, ds * vmix * silu_b * du_dpre)
        emit(5, ds * u * vmix * (sb * (1.0 + b_gate * (1.0 - sb))))
        dvmix = ds * u * silu_b
        dvmix_bf = dvmix.astype(BF16)
        for q in range(tm // CHUNK):
            dbsp_ref[...] += dvmix[q * CHUNK:(q + 1) * CHUNK, :]
            for h in range(N_GROUPS):
                blk = (slice(q * CHUNK, (q + 1) * CHUNK), slice(h * GROUP_W, (h + 1) * GROUP_W))
                dws_ref[h] += _dot_nt(dvmix_bf[blk], v1[blk])
        dv1 = _spatial_mix(wst_ref, dvmix_bf, tm)
        acc_ref[2:3, :] += _colsum(dv1 * vhat)
        acc_ref[3:4, :] += _colsum(dv1)
        emit(4, _norm_bwd(dv1 * lnv_g, vhat, vrstd) * dv_dpre)

    tile = lambda w: pl.BlockSpec((tm, w), lambda i: (i, 0))
    return pl.pallas_call(
        body, name="backward_tiles", grid=(t // tm,),
        in_specs=[tile(D_IN), tile(D), tile(D), tile(D), tile(D), _resident((D, D)), _resident((D, D)), _resident((D, D)),
                  _full((8, D)), _full((N_GROUPS, CHUNK, CHUNK)), _full((N_GROUPS, CHUNK, CHUNK)), _full((CHUNK, D))],
        out_specs=[tile(D), tile(D_IN), tile(D), tile(D), _full((8, D)), _full((8, D_IN)),
                   _full((N_GROUPS, CHUNK, CHUNK)), _full((CHUNK, D))],
        out_shape=[jax.ShapeDtypeStruct((t, D), F32), jax.ShapeDtypeStruct((t, D_IN), BF16),
                   jax.ShapeDtypeStruct((t, D), BF16), jax.ShapeDtypeStruct((t, D), BF16),
                   jax.ShapeDtypeStruct((8, D), F32), jax.ShapeDtypeStruct((8, D_IN), F32),
                   jax.ShapeDtypeStruct((N_GROUPS, CHUNK, CHUNK), F32), jax.ShapeDtypeStruct((CHUNK, D), F32)],
        compiler_params=_params("arbitrary"),
    )(p, h1, ya, yb, drb, wpa, wpb, wo, vecs, ws, wst, bsp)


def _conv_backward(dh1, p, dp, convw, tiles_per_seq):
    t = dh1.shape[0]
    tm = TOKEN_TILE
    hb = tm // HALO
    last = t // HALO - 1

    def body(dh1_ref, dnext_ref, p_ref, ph_ref, cw_ref, dp_in_ref, dp_ref, dcw_ref, dbin_ref, de_ref, he_ref, dh0_ref):
        del dp_in_ref
        i = pl.program_id(0)

        @pl.when(i == 0)
        def _():
            dcw_ref[...] = jnp.zeros_like(dcw_ref)
            dbin_ref[...] = jnp.zeros_like(dbin_ref)

        keep_next = jnp.where(i % tiles_per_seq == tiles_per_seq - 1, 0.0, 1.0)
        de_ref[0:tm, :] = dh1_ref[...]
        de_ref[tm:, :] = dnext_ref[...] * keep_next
        _conv_taps(de_ref, cw_ref, CONV_K - 1, -1, None, dh0_ref, tm)

        keep_prev = jnp.where(i % tiles_per_seq == 0, 0.0, 1.0)
        sg = _sigmoid(p_ref[:, D:2 * D])
        val = p_ref[:, 0:D]
        he_ref[0:HALO, :] = ph_ref[:, 0:D] * _sigmoid(ph_ref[:, D:2 * D]) * keep_prev
        he_ref[HALO:, :] = val * sg
        dh1 = dh1_ref[...]
        for k in range(CONV_K):
            dcw_ref[k:k + 1, :] += _colsum(dh1 * he_ref[pl.ds(k + HALO - (CONV_K - 1), tm), :])
        dcw_ref[CONV_K:CONV_K + 1, :] += _colsum(dh1)

        dh0 = dh0_ref[...]
        dval = dh0 * sg
        dglu = dh0 * val * (sg * (1.0 - sg))
        dbin_ref[0:1, 0:D] += _colsum(dval)
        dbin_ref[0:1, D:2 * D] += _colsum(dglu)
        dp_ref[:, 0:D] = dval.astype(BF16)
        dp_ref[:, D:2 * D] = dglu.astype(BF16)

    return pl.pallas_call(
        body, name="conv_backward", grid=(t // tm,),
        in_specs=[pl.BlockSpec((tm, D), lambda i: (i, 0)),
                  pl.BlockSpec((HALO, D), lambda i: (jnp.minimum((i + 1) * hb, last), 0)),
                  pl.BlockSpec((tm, 2 * D), lambda i: (i, 0)),
                  pl.BlockSpec((HALO, 2 * D), lambda i: (jnp.maximum(i * hb - 1, 0), 0)),
                  _full((HALO, D)), pl.BlockSpec(memory_space=pl.ANY)],
        out_specs=[pl.BlockSpec((tm, 2 * D), lambda i: (i, 0)), _full((HALO, D)), _full((8, 2 * D))],
        out_shape=[jax.ShapeDtypeStruct(dp.shape, BF16), jax.ShapeDtypeStruct((HALO, D), F32),
                   jax.ShapeDtypeStruct((8, 2 * D), F32)],
        scratch_shapes=[pltpu.VMEM((tm + HALO, D), F32), pltpu.VMEM((tm + HALO, D), F32), pltpu.VMEM((tm, D), F32)],
        input_output_aliases={5: 0},
        compiler_params=_params("arbitrary"),
    )(dh1, dh1, p, p, convw, dp)


def _grad_w_in(xb, dp):
    t = xb.shape[0]
    tk = 512
    half = D // 2

    def body(x_ref, dp_ref, o_ref):
        @pl.when(pl.program_id(3) == 0)
        def _():
            o_ref[...] = jnp.zeros_like(o_ref)

        o_ref[...] += _dot_tn(x_ref[...], dp_ref[...])

    return pl.pallas_call(
        body, name="grad_w_in", grid=(2, N_CHIPS, 2, t // tk),
        in_specs=[pl.BlockSpec((tk, half), lambda h, j, n, s: (s, h)),
                  pl.BlockSpec((tk, D), lambda h, j, n, s: (s, 2 * j + n))],
        out_specs=pl.BlockSpec((None, None, half, D), lambda h, j, n, s: (h, j, 0, n)),
        out_shape=jax.ShapeDtypeStruct((2, N_CHIPS, half, W_BLOCK), F32),
        compiler_params=_params("parallel", "parallel", "parallel", "arbitrary"),
    )(xb, dp)


def _grad_w_square(a, b):
    t = a.shape[0]
    tk = 512
    rows = D // 8

    def body(a_ref, b_ref, o_ref, acc_ref):
        s = pl.program_id(0)

        @pl.when(s == 0)
        def _():
            acc_ref[...] = jnp.zeros_like(acc_ref)

        acc_ref[...] += _dot_tn(a_ref[...], b_ref[...])

        @pl.when(s == pl.num_programs(0) - 1)
        def _():
            for j in range(N_CHIPS):
                for h in range(2):
                    o_ref[h, j] = acc_ref[(2 * j + h) * rows:(2 * j + h + 1) * rows, :]

    return pl.pallas_call(
        body, name="grad_w_square", grid=(t // tk,),
        in_specs=[pl.BlockSpec((tk, D), lambda s: (s, 0)), pl.BlockSpec((tk, D), lambda s: (s, 0))],
        out_specs=_full((2, N_CHIPS, rows, D)),
        out_shape=jax.ShapeDtypeStruct((2, N_CHIPS, rows, D), F32),
        scratch_shapes=[pltpu.VMEM((D, D), F32)],
        compiler_params=_params("arbitrary"),
    )(a, b)


def _grad_x(dp, w4, dr):
    t = dr.shape[0]
    tm = TOKEN_TILE

    def body(dp_ref, w_ref, dr_ref, o_ref):
        acc = ALPHA * dr_ref[...]
        for j in range(N_CHIPS):
            acc = acc + _dot_nt(dp_ref[:, j * W_BLOCK:(j + 1) * W_BLOCK], w_ref[j])
        o_ref[...] = acc

    return pl.pallas_call(
        body, name="grad_x", grid=(t // tm,),
        in_specs=[pl.BlockSpec((tm, D_IN), lambda i: (i, 0)),
                  pl.BlockSpec((N_CHIPS, D, W_BLOCK), lambda i: (0, 0, 0), pipeline_mode=pl.Buffered(1)),
                  pl.BlockSpec((tm, D), lambda i: (i, 0))],
        out_specs=pl.BlockSpec((tm, D), lambda i: (i, 0)),
        out_shape=jax.ShapeDtypeStruct((t, D), F32),
        compiler_params=_params("parallel"),
    )(dp, w4, dr)


SMALL_ROWS = 1216


def _pack_small(b_in, vec9, w_spatial, b_spatial, loss_lanes):
    parts = [b_in.reshape(64, 128)] + [v.reshape(8, 128) for v in vec9]
    parts += [w_spatial.reshape(N_GROUPS * CHUNK, CHUNK), b_spatial.reshape(8, 128), loss_lanes.reshape(8, 128)]
    used = sum(p.shape[0] for p in parts)
    return jnp.concatenate(parts + [jnp.zeros((SMALL_ROWS - used, 128), F32)], axis=0)


LOSS_ROW = 64 + 9 * 8 + N_GROUPS * CHUNK + 8


def _unpack_small(a):
    out, row = [], 0
    for rows, shape in [(64, (D_IN,))] + [(8, (D,))] * 9 + [(N_GROUPS * CHUNK, (N_GROUPS, CHUNK, CHUNK)), (8, (N_GROUPS, CHUNK))]:
        out.append(a[row:row + rows].reshape(shape))
        row += rows
    return out


def kernel(x, w_in, b_in, conv_w, conv_b, gn_g, gn_b, ln_v_g, ln_v_b, w_spatial, b_spatial, w_pa, w_pb, w_o, b_o, ln_out_g, ln_out_b, loss_target, m_w_in, m_b_in, m_conv_w, m_conv_b, m_gn_g, m_gn_b, m_ln_v_g, m_ln_v_b, m_w_spatial, m_b_spatial, m_w_pa, m_w_pb, m_w_o, m_b_o, m_ln_out_g, m_ln_out_b, v_w_in, v_b_in, v_conv_w, v_conv_b, v_gn_g, v_gn_b, v_ln_v_g, v_ln_v_b, v_w_spatial, v_b_spatial, v_w_pa, v_w_pb, v_w_o, v_b_o, v_ln_out_g, v_ln_out_b):
    n_seq, seq, _ = x.shape
    t = n_seq * seq
    tiles_per_seq = seq // TOKEN_TILE
    x2 = x.reshape(t, D)
    tgt = loss_target.reshape(t, D)

    conv_shard = jnp.pad(conv_w, ((0, HALO - CONV_K), (0, 0))).reshape(2, HALO // 2, D // N_CHIPS)
    win4, wpa4, wpb4, wo4, conv4 = _all_gather_weights(
        [_cast_bf16(w_in, 256), _cast_bf16(w_pa, 128), _cast_bf16(w_pb, 128), _cast_bf16(w_o, 128), conv_shard])
    win4 = win4.reshape(N_CHIPS, D, W_BLOCK)
    wpa, wpb, wo = wpa4.reshape(D, D), wpb4.reshape(D, D), wo4.reshape(D, D)
    convw = conv4.reshape(N_CHIPS, HALO, D // N_CHIPS).transpose(1, 0, 2).reshape(HALO, D)

    zeros = jnp.zeros((D,), F32)
    vecs = jnp.stack([conv_b, gn_g, gn_b, ln_v_g, ln_v_b, b_o, ln_out_g, ln_out_b])
    causal = jnp.tril(jnp.ones((CHUNK, CHUNK), bool))
    ws = jnp.where(causal[None], w_spatial, 0.0)
    ws_bf, wst_bf = ws.astype(BF16), ws.transpose(0, 2, 1).astype(BF16)
    bsp = jnp.repeat(b_spatial.T, GROUP_W, axis=1)

    p, xb = _proj(x2, win4, b_in.reshape(N_CHIPS, 1, W_BLOCK))
    h1, ya, yb, h3, s, mixed, dr, drb, acc_f = _forward_tiles(p, x2, tgt, wpa, wpb, wo, convw, vecs, ws_bf, bsp, tiles_per_seq)
    dh1, dp, dya, dyb, acc_b, dbin_b, dws, dbsp_acc = _backward_tiles(p, h1, ya, yb, drb, wpa, wpb, wo, vecs, ws_bf, wst_bf, bsp)
    dp, dcw, dbin_a = _conv_backward(dh1, p, dp, convw, tiles_per_seq)
    grad_x = _grad_x(dp, win4, dr).reshape(x.shape)
    g_in = _grad_w_in(xb, dp)
    g_pa, g_pb, g_o = _grad_w_square(h3, dya), _grad_w_square(s, dyb), _grad_w_square(mixed, drb)

    d_b_in = jnp.concatenate([dbin_a[0], dbin_b[0, 2 * D:]])
    d_b_spatial = jnp.sum(dbsp_acc.reshape(CHUNK, N_GROUPS, GROUP_W), axis=2).T
    small = _pack_small(
        d_b_in, [dcw[CONV_K], acc_b[0], acc_b[1], acc_b[2], acc_b[3], acc_f[2], acc_f[0], acc_f[1], zeros],
        jnp.where(causal[None], dws, 0.0), d_b_spatial, acc_f[3])
    g_conv = dcw.reshape(2, HALO // 2, N_CHIPS, D // N_CHIPS).transpose(0, 2, 1, 3)
    small = small.reshape(2, N_CHIPS, SMALL_ROWS // 8, 128)

    grads = [g_in, g_pa, g_pb, g_o, g_conv, small]
    wire = [BF16, BF16, BF16, BF16, F32, F32]
    from_sibling = _exchange_halves(grads)
    sums = [_add_halves(g, b, wd) for g, b, wd in zip(grads, from_sibling, wire)]
    from_chips = _scatter_to_chips([qw for _, qw in sums])
    mine = [_add_chips(q, b) for (q, _), b in zip(sums, from_chips)]
    *full, small_parts = _share_results(mine[:5], mine[5])
    grad_w_in, grad_w_pa, grad_w_pb, grad_w_o = [f.reshape(w.shape) for f, w in zip(full[:4], (w_in, w_pa, w_pb, w_o))]
    grad_conv_w = full[4].reshape(HALO, D // N_CHIPS)[:CONV_K]

    big = {}
    for name, w, g, m, v in [("w_in", w_in, grad_w_in, m_w_in, v_w_in), ("w_pa", w_pa, grad_w_pa, m_w_pa, v_w_pa),
                             ("w_pb", w_pb, grad_w_pb, m_w_pb, v_w_pb), ("w_o", w_o, grad_w_o, m_w_o, v_w_o),
                             ("conv_w", conv_w, grad_conv_w, m_conv_w, v_conv_w)]:
        big[name] = (g,) + tuple(_adamw(w, g, m, v))
    small_w = [b_in, conv_b, gn_g, gn_b, ln_v_g, ln_v_b, b_o, ln_out_g, ln_out_b]
    small_m = [m_b_in, m_conv_b, m_gn_g, m_gn_b, m_ln_v_g, m_ln_v_b, m_b_o, m_ln_out_g, m_ln_out_b]
    small_v = [v_b_in, v_conv_b, v_gn_g, v_gn_b, v_ln_v_g, v_ln_v_b, v_b_o, v_ln_out_g, v_ln_out_b]
    pack = lambda vs, wsp, bs: _pack_small(vs[0], vs[1:] + [zeros], wsp, bs, zeros)
    sg, sd, sm, sv, loss8 = _adamw_small(
        pack(small_w, w_spatial, b_spatial), small_parts.reshape(8, SMALL_ROWS // 8, 128),
        pack(small_m, m_w_spatial, m_b_spatial), pack(small_v, v_w_spatial, v_b_spatial), LOSS_ROW)
    names = ["b_in", "conv_b", "gn_g", "gn_b", "ln_v_g", "ln_v_b", "b_o", "ln_out_g", "ln_out_b", "pad", "w_spatial", "b_spatial"]
    per_kind = [dict(zip(names, _unpack_small(a))) for a in (sg, sd, sm, sv)]

    order = ["w_in", "b_in", "conv_w", "conv_b", "gn_g", "gn_b", "ln_v_g", "ln_v_b", "w_spatial", "b_spatial",
             "w_pa", "w_pb", "w_o", "b_o", "ln_out_g", "ln_out_b"]
    outs = [loss8[0, 0], grad_x]
    for kind in range(4):
        outs += [big[n][kind] if n in big else per_kind[kind][n] for n in order]
    return tuple(outs)
```

```python
import functools
import math

import jax
import jax.numpy as jnp
from jax import lax
from jax.experimental import pallas as pl
from jax.experimental.pallas import tpu as pltpu

D = 1024
N_GROUPS = 8
GROUP_W = D // N_GROUPS
CHUNK = 128
CONV_K = 31
HALO = 32
D_IN = 8 * D
N_CHIPS = 4
W_BLOCK = D_IN // N_CHIPS
ALPHA = 2.0 ** 0.25
LN_EPS = 1e-5
ADAM_LR, ADAM_B1, ADAM_B2, ADAM_EPS, ADAM_WD, ADAM_STEP = 0.001, 0.9, 0.999, 1e-08, 0.01, 10

TOKEN_TILE = 256
VMEM_LIMIT = 56 * 1024 * 1024
MESH = pl.DeviceIdType.MESH
F32, BF16 = jnp.float32, jnp.bfloat16


def _sigmoid(x):
    return 1.0 / (1.0 + jnp.exp(-x))


def _gelu(x):
    c = math.sqrt(2.0 / math.pi)
    t = jnp.tanh(c * (x + 0.044715 * (x * x * x)))
    return x * (0.5 * (1.0 + t))


def _gelu_and_grad(x):
    c = math.sqrt(2.0 / math.pi)
    x2 = x * x
    t = jnp.tanh(c * (x + 0.044715 * (x2 * x)))
    cdf = 0.5 * (1.0 + t)
    return x * cdf, cdf + 0.5 * x * (1.0 - t * t) * (c * (1.0 + 3.0 * 0.044715 * x2))


def _norm_stats(v):
    mu = jnp.mean(v, axis=-1, keepdims=True)
    vc = v - mu
    var = jnp.mean(vc * vc, axis=-1, keepdims=True)
    rstd = lax.rsqrt(var + LN_EPS)
    return vc * rstd, rstd


def _norm_bwd(dxhat, xhat, rstd):
    m1 = jnp.mean(dxhat, axis=-1, keepdims=True)
    m2 = jnp.mean(dxhat * xhat, axis=-1, keepdims=True)
    return rstd * (dxhat - m1 - xhat * m2)


def _dot(a, b):
    return jnp.dot(a, b, preferred_element_type=F32)


def _dot_nt(a, b):
    return lax.dot_general(a, b, (((1,), (1,)), ((), ())), preferred_element_type=F32)


def _dot_tn(a, b):
    return lax.dot_general(a, b, (((0,), (0,)), ((), ())), preferred_element_type=F32)


def _colsum(v):
    return jnp.sum(v, axis=0, keepdims=True)


def _full(shape):
    return pl.BlockSpec(shape, lambda *_: (0,) * len(shape))


def _resident(shape):
    return pl.BlockSpec(shape, lambda *_: (0,) * len(shape), pipeline_mode=pl.Buffered(1))


def _params(*sem):
    return pltpu.CompilerParams(dimension_semantics=sem, vmem_limit_bytes=VMEM_LIMIT)


def _chip_index():
    return (2 * lax.axis_index("x") + lax.axis_index("y")).astype(jnp.int32).reshape(1)


def _core_index():
    return lax.axis_index("c").astype(jnp.int32).reshape(1)


def _place_shard(w, rows, dtype):
    r, c = w.shape
    steps = r // 2 // rows

    def body(k_ref, w_ref, o_ref):
        o_ref[...] = w_ref[...].astype(dtype)

    return pl.pallas_call(
        body, name="place_shard",
        grid_spec=pltpu.PrefetchScalarGridSpec(
            num_scalar_prefetch=1, grid=(2, steps),
            in_specs=[pl.BlockSpec((rows, c), lambda h, i, k: (h * steps + i, 0))],
            out_specs=pl.BlockSpec((None, None, rows, c), lambda h, i, k: (k[0], h, i, 0))),
        out_shape=jax.ShapeDtypeStruct((N_CHIPS, 2, r // 2, c), dtype),
        compiler_params=_params("parallel", "parallel"),
    )(_chip_index(), w)


def _position():
    x, y, c = lax.axis_index("x"), lax.axis_index("y"), lax.axis_index("c")
    return x, y, c, 2 * x + y


def _other_chips(x, y):
    return [(1 - x, y), (x, 1 - y), (1 - x, 1 - y)]


def _any_specs(n):
    return [pl.BlockSpec(memory_space=pl.ANY)] * n


def _all_gather_weights(bufs):
    n = len(bufs)

    def body(*refs):
        outs = refs[n:2 * n]
        send, recv, fsend, frecv = refs[2 * n:]
        x, y, c, k = _position()
        chips = _other_chips(x, y)
        sibling = (x, y, 1 - c)
        first = []
        for a in range(n):
            for r, (cx, cy) in enumerate(chips):
                cp = pltpu.make_async_remote_copy(
                    src_ref=outs[a].at[k, c], dst_ref=outs[a].at[k, c],
                    send_sem=send.at[3 * a + r], recv_sem=recv.at[3 * a + r],
                    device_id=(cx, cy, c), device_id_type=MESH)
                cp.start()
                first.append(cp)
        passed = []
        for a in range(n):
            for r, (cx, cy) in enumerate(chips):
                kr = 2 * cx + cy
                pltpu.make_async_remote_copy(
                    src_ref=outs[a].at[kr, c], dst_ref=outs[a].at[kr, c],
                    send_sem=send.at[3 * a + r], recv_sem=recv.at[3 * a + r],
                    device_id=(cx, cy, c), device_id_type=MESH).wait_recv()
                cp = pltpu.make_async_remote_copy(
                    src_ref=outs[a].at[kr, c], dst_ref=outs[a].at[kr, c],
                    send_sem=fsend.at[3 * a + r], recv_sem=frecv.at[3 * a + r],
                    device_id=sibling, device_id_type=MESH)
                cp.start()
                passed.append(cp)
        for a in range(n):
            for r, (cx, cy) in enumerate(chips):
                kr = 2 * cx + cy
                pltpu.make_async_remote_copy(
                    src_ref=outs[a].at[kr, 1 - c], dst_ref=outs[a].at[kr, 1 - c],
                    send_sem=fsend.at[3 * a + r], recv_sem=frecv.at[3 * a + r],
                    device_id=sibling, device_id_type=MESH).wait_recv()
        for cp in first + passed:
            cp.wait_send()

    return pl.pallas_call(
        body, name="all_gather_weights",
        in_specs=_any_specs(n), out_specs=_any_specs(n),
        out_shape=[jax.ShapeDtypeStruct(b.shape, b.dtype) for b in bufs],
        scratch_shapes=[pltpu.SemaphoreType.DMA((3 * n,)), pltpu.SemaphoreType.DMA((3 * n,)),
                        pltpu.SemaphoreType.DMA((3 * n,)), pltpu.SemaphoreType.DMA((3 * n,))],
        input_output_aliases={a: a for a in range(n)},
    )(*bufs)


def _exchange_halves(grads):
    n = len(grads)

    def body(*refs):
        ins, outs = refs[:n], refs[n:2 * n]
        send, recv = refs[2 * n:]
        x, y, c, _ = _position()
        cps = []
        for a in range(n):
            cp = pltpu.make_async_remote_copy(
                src_ref=ins[a].at[1 - c], dst_ref=outs[a], send_sem=send.at[a], recv_sem=recv.at[a],
                device_id=(x, y, 1 - c), device_id_type=MESH)
            cp.start()
            cps.append(cp)
        for cp in cps:
            cp.wait_recv()
        for cp in cps:
            cp.wait_send()

    return pl.pallas_call(
        body, name="rs_exchange_halves",
        in_specs=_any_specs(n), out_specs=_any_specs(n),
        out_shape=[jax.ShapeDtypeStruct(g.shape[1:], g.dtype) for g in grads],
        scratch_shapes=[pltpu.SemaphoreType.DMA((n,)), pltpu.SemaphoreType.DMA((n,))],
    )(*grads)


def _scatter_to_chips(parts):
    n = len(parts)

    def body(*refs):
        ins, outs = refs[:n], refs[n:2 * n]
        send, recv = refs[2 * n:]
        x, y, c, _ = _position()
        cps = []
        for a in range(n):
            for r, (cx, cy) in enumerate(_other_chips(x, y)):
                cp = pltpu.make_async_remote_copy(
                    src_ref=ins[a].at[2 * cx + cy], dst_ref=outs[a].at[r],
                    send_sem=send.at[3 * a + r], recv_sem=recv.at[3 * a + r],
                    device_id=(cx, cy, c), device_id_type=MESH)
                cp.start()
                cps.append(cp)
        for cp in cps:
            cp.wait_recv()
        for cp in cps:
            cp.wait_send()

    return pl.pallas_call(
        body, name="rs_scatter_to_chips",
        in_specs=_any_specs(n), out_specs=_any_specs(n),
        out_shape=[jax.ShapeDtypeStruct((3,) + p.shape[1:], p.dtype) for p in parts],
        scratch_shapes=[pltpu.SemaphoreType.DMA((3 * n,)), pltpu.SemaphoreType.DMA((3 * n,))],
    )(*parts)


def _share_results(bufs, small):
    n = len(bufs)

    def body(*refs):
        outs, small_out = refs[n + 1:2 * n + 1], refs[2 * n + 1]
        send, recv, ssend, srecv = refs[2 * n + 2:]
        x, y, c, k = _position()
        cps = []
        for a in range(n):
            cp = pltpu.make_async_remote_copy(
                src_ref=outs[a].at[c], dst_ref=outs[a].at[c], send_sem=send.at[a], recv_sem=recv.at[a],
                device_id=(x, y, 1 - c), device_id_type=MESH)
            cp.start()
            cps.append(cp)
        waits = []
        for p in range(1, 8):
            px, py, pc = x ^ (p >> 2), y ^ ((p >> 1) & 1), c ^ (p & 1)
            cp = pltpu.make_async_remote_copy(
                src_ref=small_out.at[k, c], dst_ref=small_out.at[k, c], send_sem=ssend.at[p - 1],
                recv_sem=srecv.at[p - 1], device_id=(px, py, pc), device_id_type=MESH)
            cp.start()
            cps.append(cp)
            waits.append(pltpu.make_async_remote_copy(
                src_ref=small_out.at[2 * px + py, pc], dst_ref=small_out.at[2 * px + py, pc], send_sem=ssend.at[p - 1],
                recv_sem=srecv.at[p - 1], device_id=(px, py, pc), device_id_type=MESH))
        for a in range(n):
            pltpu.make_async_remote_copy(
                src_ref=outs[a].at[1 - c], dst_ref=outs[a].at[1 - c], send_sem=send.at[a], recv_sem=recv.at[a],
                device_id=(x, y, 1 - c), device_id_type=MESH).wait_recv()
        for w in waits:
            w.wait_recv()
        for cp in cps:
            cp.wait_send()

    return pl.pallas_call(
        body, name="rs_share_results",
        in_specs=_any_specs(n + 1), out_specs=_any_specs(n + 1),
        out_shape=[jax.ShapeDtypeStruct(b.shape, b.dtype) for b in bufs + [small]],
        scratch_shapes=[pltpu.SemaphoreType.DMA((n,)), pltpu.SemaphoreType.DMA((n,)),
                        pltpu.SemaphoreType.DMA((7,)), pltpu.SemaphoreType.DMA((7,))],
        input_output_aliases={a: a for a in range(n + 1)},
    )(*bufs, small)


def _row_tile(r, c):
    t = max(8, min(r, (1 << 18) // c))
    while r % t:
        t //= 2
    return t


def _add_halves(g, b1, wire_dtype):
    _, _, r, c = g.shape
    t = _row_tile(r, c)

    def body(c_ref, g_ref, b_ref, q_ref, qw_ref):
        q = g_ref[...] + b_ref[...]
        q_ref[...] = q
        qw_ref[...] = q.astype(wire_dtype)

    core = _core_index()
    return pl.pallas_call(
        body, name="rs_add_halves",
        grid_spec=pltpu.PrefetchScalarGridSpec(
            num_scalar_prefetch=1, grid=(N_CHIPS, r // t),
            in_specs=[pl.BlockSpec((None, None, t, c), lambda j, i, cr: (cr[0], j, i, 0)),
                      pl.BlockSpec((None, t, c), lambda j, i, cr: (j, i, 0))],
            out_specs=[pl.BlockSpec((None, t, c), lambda j, i, cr: (j, i, 0)),
                       pl.BlockSpec((None, t, c), lambda j, i, cr: (j, i, 0))]),
        out_shape=[jax.ShapeDtypeStruct((N_CHIPS, r, c), F32), jax.ShapeDtypeStruct((N_CHIPS, r, c), wire_dtype)],
        compiler_params=_params("parallel", "parallel"),
    )(core, g, b1)


def _add_chips(q, b2, per_device):
    _, r, c = q.shape
    t = _row_tile(r, c)

    def body(kc_ref, q_ref, b_ref, f_ref):
        f_ref[...] = ((q_ref[...] + b_ref[0].astype(F32)) + b_ref[1].astype(F32)) + b_ref[2].astype(F32)

    if per_device:
        out_spec = pl.BlockSpec((None, None, t, c), lambda i, kc: (kc[0], kc[1], i, 0))
        out_shape = jax.ShapeDtypeStruct((N_CHIPS, 2, r, c), F32)
    else:
        out_spec = pl.BlockSpec((None, t, c), lambda i, kc: (kc[1], i, 0))
        out_shape = jax.ShapeDtypeStruct((2, r, c), F32)
    return pl.pallas_call(
        body, name="rs_add_chips",
        grid_spec=pltpu.PrefetchScalarGridSpec(
            num_scalar_prefetch=1, grid=(r // t,),
            in_specs=[pl.BlockSpec((None, t, c), lambda i, kc: (kc[0], i, 0)),
                      pl.BlockSpec((3, t, c), lambda i, kc: (0, i, 0))],
            out_specs=out_spec),
        out_shape=out_shape,
        compiler_params=_params("parallel"),
    )(jnp.concatenate([_chip_index(), _core_index()]), q, b2)


def _adamw_math(w, g, m, v):
    m = ADAM_B1 * m + (1.0 - ADAM_B1) * g
    v = ADAM_B2 * v + (1.0 - ADAM_B2) * (g * g)
    m_hat = m / (1.0 - ADAM_B1 ** ADAM_STEP)
    v_hat = v / (1.0 - ADAM_B2 ** ADAM_STEP)
    delta = -ADAM_LR * (m_hat / (jnp.sqrt(v_hat) + ADAM_EPS) + ADAM_WD * w)
    return delta, m, v


def _adamw(w, g, m, v):
    r, c = w.shape
    t = _row_tile(r, c) if r % 8 == 0 else r

    def body(w_ref, g_ref, m_ref, v_ref, d_ref, nm_ref, nv_ref):
        d_ref[...], nm_ref[...], nv_ref[...] = _adamw_math(w_ref[...], g_ref[...], m_ref[...], v_ref[...])

    spec = pl.BlockSpec((t, c), lambda i: (i, 0))
    return pl.pallas_call(
        body, name="adamw", grid=(r // t,), in_specs=[spec] * 4, out_specs=[spec] * 3,
        out_shape=[jax.ShapeDtypeStruct((r, c), F32)] * 3, compiler_params=_params("parallel"),
    )(w, g, m, v)


def _adamw_small(w, parts, m, v, loss_row):
    r, c = w.shape

    def body(w_ref, p_ref, m_ref, v_ref, g_ref, d_ref, nm_ref, nv_ref, loss_ref):
        rows = r // 8
        for k in range(N_CHIPS):
            for core in range(2):
                g_ref[(core * N_CHIPS + k) * rows:(core * N_CHIPS + k + 1) * rows, :] = p_ref[2 * k + core]
        g = g_ref[...]
        d_ref[...], nm_ref[...], nv_ref[...] = _adamw_math(w_ref[...], g, m_ref[...], v_ref[...])
        lanes = g_ref[loss_row:loss_row + 8, :]
        loss_ref[...] = jnp.broadcast_to(jnp.sum(jnp.sum(lanes, axis=1, keepdims=True), axis=0, keepdims=True), (8, c))

    return pl.pallas_call(
        body, name="adamw_small",
        in_specs=[_full((r, c)), _full((8, r // 8, c)), _full((r, c)), _full((r, c))],
        out_specs=[_full((r, c))] * 4 + [_full((8, c))],
        out_shape=[jax.ShapeDtypeStruct((r, c), F32)] * 4 + [jax.ShapeDtypeStruct((8, c), F32)],
        compiler_params=_params(),
    )(w, parts, m, v)


def _proj(x, w4, b4):
    t = x.shape[0]
    tm = 1024

    def body(x_ref, w_ref, b_ref, p_ref, xb_ref):
        xb = x_ref[...].astype(BF16)
        xb_ref[...] = xb
        p_ref[...] = _dot(xb, w_ref[...]) + b_ref[...]

    return pl.pallas_call(
        body, name="proj", grid=(t // tm, N_CHIPS),
        in_specs=[pl.BlockSpec((tm, D), lambda i, j: (i, 0)),
                  pl.BlockSpec((None, D, W_BLOCK), lambda i, j: (j, 0, 0)),
                  pl.BlockSpec((None, 1, W_BLOCK), lambda i, j: (j, 0, 0))],
        out_specs=[pl.BlockSpec((tm, W_BLOCK), lambda i, j: (i, j)),
                   pl.BlockSpec((tm, D), lambda i, j: (i, 0))],
        out_shape=[jax.ShapeDtypeStruct((t, D_IN), F32), jax.ShapeDtypeStruct((t, D), BF16)],
        compiler_params=_params("arbitrary", "arbitrary"),
    )(x, w4, b4)


def _conv_taps(src_ref, w_ref, first_offset, step, bias, dst_ref, tm):
    rows = 64
    for g in range(N_GROUPS):
        cs = slice(g * GROUP_W, (g + 1) * GROUP_W)
        for rb in range(tm // rows):
            acc = jnp.zeros((rows, GROUP_W), F32) + (bias[:, cs] if bias is not None else 0.0)
            for k in range(CONV_K):
                acc = acc + w_ref[k:k + 1, cs] * src_ref[pl.ds(rb * rows + first_offset + step * k, rows), cs]
            dst_ref[rb * rows:(rb + 1) * rows, cs] = acc


def _spatial_mix(w_ref, v_bf, tm):
    rows = []
    for q in range(tm // CHUNK):
        cols = [_dot(w_ref[h], v_bf[q * CHUNK:(q + 1) * CHUNK, h * GROUP_W:(h + 1) * GROUP_W])
                for h in range(N_GROUPS)]
        rows.append(jnp.concatenate(cols, axis=1))
    return jnp.concatenate(rows, axis=0)


def _group_norm_fwd(h1, gn_g, gn_b):
    xhat, rstd = [], []
    for g in range(N_GROUPS):
        xh, rs = _norm_stats(h1[:, g * GROUP_W:(g + 1) * GROUP_W])
        xhat.append(xh)
        rstd.append(rs)
    xhat = jnp.concatenate(xhat, axis=1)
    return xhat * gn_g + gn_b, xhat, rstd


def _forward_tiles(p, x, tgt, wpa, wpb, wo, convw, vecs, ws, bsp, tiles_per_seq):
    t = x.shape[0]
    tm = TOKEN_TILE
    hb = tm // HALO

    def body(p_ref, ph_ref, x_ref, t_ref, wpa_ref, wpb_ref, wo_ref, cw_ref, vec_ref, ws_ref, bsp_ref,
             h1_ref, ya_ref, yb_ref, h3_ref, s_ref, mx_ref, dr_ref, drb_ref, acc_ref, he_ref):
        i = pl.program_id(0)
        conv_b, gn_g, gn_b, lnv_g, lnv_b, b_o, lno_g, lno_b = [vec_ref[j:j + 1, :] for j in range(8)]

        keep = jnp.where(i % tiles_per_seq == 0, 0.0, 1.0)
        he_ref[0:HALO, :] = ph_ref[:, 0:D] * _sigmoid(ph_ref[:, D:2 * D]) * keep
        he_ref[HALO:, :] = p_ref[:, 0:D] * _sigmoid(p_ref[:, D:2 * D])
        _conv_taps(he_ref, cw_ref, HALO - (CONV_K - 1), 1, conv_b, h1_ref, tm)
        h2, _, _ = _group_norm_fwd(h1_ref[...], gn_g, gn_b)
        a_gate = p_ref[:, 2 * D:3 * D]
        h3 = ((h2 * _sigmoid(h2)) * (a_gate * _sigmoid(a_gate))).astype(BF16)
        h3_ref[...] = h3
        ya = _dot(h3, wpa_ref[...])
        ya_ref[...] = ya

        u = _gelu(p_ref[:, 3 * D:4 * D])
        vhat, _ = _norm_stats(_gelu(p_ref[:, 4 * D:5 * D]))
        v1 = (vhat * lnv_g + lnv_b).astype(BF16)
        b_gate = p_ref[:, 5 * D:6 * D]
        vmix = _spatial_mix(ws_ref, v1, tm) + jnp.concatenate([bsp_ref[...]] * (tm // CHUNK), axis=0)
        s = (u * vmix * (b_gate * _sigmoid(b_gate))).astype(BF16)
        s_ref[...] = s
        yb = _dot(s, wpb_ref[...])
        yb_ref[...] = yb

        mixed = (_sigmoid(p_ref[:, 6 * D:7 * D]) * ya + _sigmoid(p_ref[:, 7 * D:8 * D]) * yb).astype(BF16)
        mx_ref[...] = mixed
        r = ALPHA * x_ref[...] + (_dot(mixed, wo_ref[...]) + b_o)
        xhat, rstd = _norm_stats(r)
        err = (xhat * lno_g + lno_b) - t_ref[...]
        dout = err * (1.0 / D)
        dr = _norm_bwd(dout * lno_g, xhat, rstd)
        dr_ref[...] = dr
        drb_ref[...] = dr.astype(BF16)

        @pl.when(i == 0)
        def _():
            acc_ref[...] = jnp.zeros_like(acc_ref)

        acc_ref[0:1, :] += _colsum(dout * xhat)
        acc_ref[1:2, :] += _colsum(dout)
        acc_ref[2:3, :] += _colsum(dr)
        acc_ref[3:4, :] += _colsum(err * err) * (0.5 / D)

    tile = lambda w: pl.BlockSpec((tm, w), lambda i: (i, 0))
    f32_out = jax.ShapeDtypeStruct((t, D), F32)
    bf_out = jax.ShapeDtypeStruct((t, D), BF16)
    return pl.pallas_call(
        body, name="forward_tiles", grid=(t // tm,),
        in_specs=[tile(D_IN),
                  pl.BlockSpec((HALO, 2 * D), lambda i: (jnp.maximum(i * hb - 1, 0), 0)),
                  tile(D), tile(D), _resident((D, D)), _resident((D, D)), _resident((D, D)), _full((HALO, D)), _full((8, D)),
                  _full((N_GROUPS, CHUNK, CHUNK)), _full((CHUNK, D))],
        out_specs=[tile(D)] * 8 + [_full((8, D))],
        out_shape=[f32_out, f32_out, f32_out, bf_out, bf_out, bf_out, f32_out, bf_out,
                   jax.ShapeDtypeStruct((8, D), F32)],
        scratch_shapes=[pltpu.VMEM((tm + HALO, D), F32)],
        compiler_params=_params("arbitrary"),
    )(p, p, x, tgt, wpa, wpb, wo, convw, vecs, ws, bsp)


def _backward_tiles(p, h1, ya, yb, drb, wpa, wpb, wo, vecs, ws, wst, bsp):
    t = h1.shape[0]
    tm = TOKEN_TILE

    def body(p_ref, h1_ref, ya_ref, yb_ref, drb_ref, wpa_ref, wpb_ref, wo_ref, vec_ref, ws_ref, wst_ref, bsp_ref,
             dh1_ref, dp_ref, dya_ref, dyb_ref, acc_ref, dbin_ref, dws_ref, dbsp_ref):
        i = pl.program_id(0)
        _, gn_g, gn_b, lnv_g, lnv_b = [vec_ref[j:j + 1, :] for j in range(5)]

        @pl.when(i == 0)
        def _():
            acc_ref[...] = jnp.zeros_like(acc_ref)
            dbin_ref[...] = jnp.zeros_like(dbin_ref)
            dws_ref[...] = jnp.zeros_like(dws_ref)
            dbsp_ref[...] = jnp.zeros_like(dbsp_ref)

        def emit(block, val):
            dbin_ref[0:1, block * D:(block + 1) * D] += _colsum(val)
            dp_ref[:, block * D:(block + 1) * D] = val.astype(BF16)

        dp_ref[:, 0:2 * D] = jnp.zeros((tm, 2 * D), BF16)
        dmixed = _dot_nt(drb_ref[...], wo_ref[...])
        ga = _sigmoid(p_ref[:, 6 * D:7 * D])
        gb = _sigmoid(p_ref[:, 7 * D:8 * D])
        dya = (dmixed * ga).astype(BF16)
        dyb = (dmixed * gb).astype(BF16)
        dya_ref[...] = dya
        dyb_ref[...] = dyb
        emit(6, dmixed * ya_ref[...] * (ga * (1.0 - ga)))
        emit(7, dmixed * yb_ref[...] * (gb * (1.0 - gb)))

        dh3 = _dot_nt(dya, wpa_ref[...])
        h2, xhat, rstd = _group_norm_fwd(h1_ref[...], gn_g, gn_b)
        sg = _sigmoid(h2)
        a_gate = p_ref[:, 2 * D:3 * D]
        sa = _sigmoid(a_gate)
        dh2 = dh3 * (a_gate * sa) * (sg * (1.0 + h2 * (1.0 - sg)))
        emit(2, dh3 * (h2 * sg) * (sa * (1.0 + a_gate * (1.0 - sa))))
        acc_ref[0:1, :] += _colsum(dh2 * xhat)
        acc_ref[1:2, :] += _colsum(dh2)
        dxhat = dh2 * gn_g
        for g in range(N_GROUPS):
            cs = slice(g * GROUP_W, (g + 1) * GROUP_W)
            dh1_ref[:, cs] = _norm_bwd(dxhat[:, cs], xhat[:, cs], rstd[g])

        ds = _dot_nt(dyb, wpb_ref[...])
        u_pre = p_ref[:, 3 * D:4 * D]
        u, du_dpre = _gelu_and_grad(u_pre)
        v0, dv_dpre = _gelu_and_grad(p_ref[:, 4 * D:5 * D])
        vhat, vrstd = _norm_stats(v0)
        v1 = (vhat * lnv_g + lnv_b).astype(BF16)
        vmix = _spatial_mix(ws_ref, v1, tm) + jnp.concatenate([bsp_ref[...]] * (tm // CHUNK), axis=0)
        b_gate = p_ref[:, 5 * D:6 * D]
        sb = _sigmoid(b_gate)
        silu_b = b_gate * sb
        emit(3, ds * vmix * silu_b * du_dpre)
        emit(5, ds * u * vmix * (sb * (1.0 + b_gate * (1.0 - sb))))
        dvmix = ds * u * silu_b
        dvmix_bf = dvmix.astype(BF16)
        for q in range(tm // CHUNK):
            dbsp_ref[...] += dvmix[q * CHUNK:(q + 1) * CHUNK, :]
            for h in range(N_GROUPS):
                blk = (slice(q * CHUNK, (q + 1) * CHUNK), slice(h * GROUP_W, (h + 1) * GROUP_W))
                dws_ref[h] += _dot_nt(dvmix_bf[blk], v1[blk])
        dv1 = _spatial_mix(wst_ref, dvmix_bf, tm)
        acc_ref[2:3, :] += _colsum(dv1 * vhat)
        acc_ref[3:4, :] += _colsum(dv1)
        emit(4, _norm_bwd(dv1 * lnv_g, vhat, vrstd) * dv_dpre)

    tile = lambda w: pl.BlockSpec((tm, w), lambda i: (i, 0))
    return pl.pallas_call(
        body, name="backward_tiles", grid=(t // tm,),
        in_specs=[tile(D_IN), tile(D), tile(D), tile(D), tile(D), _resident((D, D)), _resident((D, D)), _resident((D, D)),
                  _full((8, D)), _full((N_GROUPS, CHUNK, CHUNK)), _full((N_GROUPS, CHUNK, CHUNK)), _full((CHUNK, D))],
        out_specs=[tile(D), tile(D_IN), tile(D), tile(D), _full((8, D)), _full((8, D_IN)),
                   _full((N_GROUPS, CHUNK, CHUNK)), _full((CHUNK, D))],
        out_shape=[jax.ShapeDtypeStruct((t, D), F32), jax.ShapeDtypeStruct((t, D_IN), BF16),
                   jax.ShapeDtypeStruct((t, D), BF16), jax.ShapeDtypeStruct((t, D), BF16),
                   jax.ShapeDtypeStruct((8, D), F32), jax.ShapeDtypeStruct((8, D_IN), F32),
                   jax.ShapeDtypeStruct((N_GROUPS, CHUNK, CHUNK), F32), jax.ShapeDtypeStruct((CHUNK, D), F32)],
        compiler_params=_params("arbitrary"),
    )(p, h1, ya, yb, drb, wpa, wpb, wo, vecs, ws, wst, bsp)


def _conv_backward(dh1, p, dp, convw, tiles_per_seq):
    t = dh1.shape[0]
    tm = TOKEN_TILE
    hb = tm // HALO
    last = t // HALO - 1

    def body(dh1_ref, dnext_ref, p_ref, ph_ref, cw_ref, dp_in_ref, dp_ref, dcw_ref, dbin_ref, de_ref, he_ref, dh0_ref):
        del dp_in_ref
        i = pl.program_id(0)

        @pl.when(i == 0)
        def _():
            dcw_ref[...] = jnp.zeros_like(dcw_ref)
            dbin_ref[...] = jnp.zeros_like(dbin_ref)

        keep_next = jnp.where(i % tiles_per_seq == tiles_per_seq - 1, 0.0, 1.0)
        de_ref[0:tm, :] = dh1_ref[...]
        de_ref[tm:, :] = dnext_ref[...] * keep_next
        _conv_taps(de_ref, cw_ref, CONV_K - 1, -1, None, dh0_ref, tm)

        keep_prev = jnp.where(i % tiles_per_seq == 0, 0.0, 1.0)
        sg = _sigmoid(p_ref[:, D:2 * D])
        val = p_ref[:, 0:D]
        he_ref[0:HALO, :] = ph_ref[:, 0:D] * _sigmoid(ph_ref[:, D:2 * D]) * keep_prev
        he_ref[HALO:, :] = val * sg
        dh1 = dh1_ref[...]
        for k in range(CONV_K):
            dcw_ref[k:k + 1, :] += _colsum(dh1 * he_ref[pl.ds(k + HALO - (CONV_K - 1), tm), :])
        dcw_ref[CONV_K:CONV_K + 1, :] += _colsum(dh1)

        dh0 = dh0_ref[...]
        dval = dh0 * sg
        dglu = dh0 * val * (sg * (1.0 - sg))
        dbin_ref[0:1, 0:D] += _colsum(dval)
        dbin_ref[0:1, D:2 * D] += _colsum(dglu)
        dp_ref[:, 0:D] = dval.astype(BF16)
        dp_ref[:, D:2 * D] = dglu.astype(BF16)

    return pl.pallas_call(
        body, name="conv_backward", grid=(t // tm,),
        in_specs=[pl.BlockSpec((tm, D), lambda i: (i, 0)),
                  pl.BlockSpec((HALO, D), lambda i: (jnp.minimum((i + 1) * hb, last), 0)),
                  pl.BlockSpec((tm, 2 * D), lambda i: (i, 0)),
                  pl.BlockSpec((HALO, 2 * D), lambda i: (jnp.maximum(i * hb - 1, 0), 0)),
                  _full((HALO, D)), pl.BlockSpec(memory_space=pl.ANY)],
        out_specs=[pl.BlockSpec((tm, 2 * D), lambda i: (i, 0)), _full((HALO, D)), _full((8, 2 * D))],
        out_shape=[jax.ShapeDtypeStruct(dp.shape, BF16), jax.ShapeDtypeStruct((HALO, D), F32),
                   jax.ShapeDtypeStruct((8, 2 * D), F32)],
        scratch_shapes=[pltpu.VMEM((tm + HALO, D), F32), pltpu.VMEM((tm + HALO, D), F32), pltpu.VMEM((tm, D), F32)],
        input_output_aliases={5: 0},
        compiler_params=_params("arbitrary"),
    )(dh1, dh1, p, p, convw, dp)


def _grad_w_in(xb, dp):
    t = xb.shape[0]
    tk = 512
    half = D // 2

    def body(x_ref, dp_ref, o_ref):
        @pl.when(pl.program_id(3) == 0)
        def _():
            o_ref[...] = jnp.zeros_like(o_ref)

        o_ref[...] += _dot_tn(x_ref[...], dp_ref[...])

    return pl.pallas_call(
        body, name="grad_w_in", grid=(2, N_CHIPS, 2, t // tk),
        in_specs=[pl.BlockSpec((tk, half), lambda h, j, n, s: (s, h)),
                  pl.BlockSpec((tk, D), lambda h, j, n, s: (s, 2 * j + n))],
        out_specs=pl.BlockSpec((None, None, half, D), lambda h, j, n, s: (h, j, 0, n)),
        out_shape=jax.ShapeDtypeStruct((2, N_CHIPS, half, W_BLOCK), F32),
        compiler_params=_params("parallel", "parallel", "parallel", "arbitrary"),
    )(xb, dp)


def _grad_w_square(a, b):
    t = a.shape[0]
    tk = 512
    rows = D // 8

    def body(a_ref, b_ref, o_ref, acc_ref):
        s = pl.program_id(0)

        @pl.when(s == 0)
        def _():
            acc_ref[...] = jnp.zeros_like(acc_ref)

        acc_ref[...] += _dot_tn(a_ref[...], b_ref[...])

        @pl.when(s == pl.num_programs(0) - 1)
        def _():
            for j in range(N_CHIPS):
                for h in range(2):
                    o_ref[h, j] = acc_ref[(2 * j + h) * rows:(2 * j + h + 1) * rows, :]

    return pl.pallas_call(
        body, name="grad_w_square", grid=(t // tk,),
        in_specs=[pl.BlockSpec((tk, D), lambda s: (s, 0)), pl.BlockSpec((tk, D), lambda s: (s, 0))],
        out_specs=_full((2, N_CHIPS, rows, D)),
        out_shape=jax.ShapeDtypeStruct((2, N_CHIPS, rows, D), F32),
        scratch_shapes=[pltpu.VMEM((D, D), F32)],
        compiler_params=_params("arbitrary"),
    )(a, b)


def _grad_x(dp, w4, dr):
    t = dr.shape[0]
    tm = TOKEN_TILE

    def body(dp_ref, w_ref, dr_ref, o_ref):
        acc = ALPHA * dr_ref[...]
        for j in range(N_CHIPS):
            acc = acc + _dot_nt(dp_ref[:, j * W_BLOCK:(j + 1) * W_BLOCK], w_ref[j])
        o_ref[...] = acc

    return pl.pallas_call(
        body, name="grad_x", grid=(t // tm,),
        in_specs=[pl.BlockSpec((tm, D_IN), lambda i: (i, 0)),
                  pl.BlockSpec((N_CHIPS, D, W_BLOCK), lambda i: (0, 0, 0), pipeline_mode=pl.Buffered(1)),
                  pl.BlockSpec((tm, D), lambda i: (i, 0))],
        out_specs=pl.BlockSpec((tm, D), lambda i: (i, 0)),
        out_shape=jax.ShapeDtypeStruct((t, D), F32),
        compiler_params=_params("parallel"),
    )(dp, w4, dr)


SMALL_ROWS = 1216


def _pack_small(b_in, vec9, w_spatial, b_spatial, loss_lanes):
    parts = [b_in.reshape(64, 128)] + [v.reshape(8, 128) for v in vec9]
    parts += [w_spatial.reshape(N_GROUPS * CHUNK, CHUNK), b_spatial.reshape(8, 128), loss_lanes.reshape(8, 128)]
    used = sum(p.shape[0] for p in parts)
    return jnp.concatenate(parts + [jnp.zeros((SMALL_ROWS - used, 128), F32)], axis=0)


LOSS_ROW = 64 + 9 * 8 + N_GROUPS * CHUNK + 8


def _unpack_small(a):
    out, row = [], 0
    for rows, shape in [(64, (D_IN,))] + [(8, (D,))] * 9 + [(N_GROUPS * CHUNK, (N_GROUPS, CHUNK, CHUNK)), (8, (N_GROUPS, CHUNK))]:
        out.append(a[row:row + rows].reshape(shape))
        row += rows
    return out


def kernel(x, w_in, b_in, conv_w, conv_b, gn_g, gn_b, ln_v_g, ln_v_b, w_spatial, b_spatial, w_pa, w_pb, w_o, b_o, ln_out_g, ln_out_b, loss_target, m_w_in, m_b_in, m_conv_w, m_conv_b, m_gn_g, m_gn_b, m_ln_v_g, m_ln_v_b, m_w_spatial, m_b_spatial, m_w_pa, m_w_pb, m_w_o, m_b_o, m_ln_out_g, m_ln_out_b, v_w_in, v_b_in, v_conv_w, v_conv_b, v_gn_g, v_gn_b, v_ln_v_g, v_ln_v_b, v_w_spatial, v_b_spatial, v_w_pa, v_w_pb, v_w_o, v_b_o, v_ln_out_g, v_ln_out_b):
    n_seq, seq, _ = x.shape
    t = n_seq * seq
    tiles_per_seq = seq // TOKEN_TILE
    x2 = x.reshape(t, D)
    tgt = loss_target.reshape(t, D)

    conv_shard = jnp.pad(conv_w, ((0, HALO - CONV_K), (0, 0)))
    win4, wpa4, wpb4, wo4, conv4 = _all_gather_weights(
        [_place_shard(w_in, 256, BF16), _place_shard(w_pa, 128, BF16), _place_shard(w_pb, 128, BF16),
         _place_shard(w_o, 128, BF16), _place_shard(conv_shard, HALO // 2, F32)])
    win4 = win4.reshape(N_CHIPS, D, W_BLOCK)
    wpa, wpb, wo = wpa4.reshape(D, D), wpb4.reshape(D, D), wo4.reshape(D, D)
    convw = conv4.reshape(N_CHIPS, HALO, D // N_CHIPS).transpose(1, 0, 2).reshape(HALO, D)

    zeros = jnp.zeros((D,), F32)
    vecs = jnp.stack([conv_b, gn_g, gn_b, ln_v_g, ln_v_b, b_o, ln_out_g, ln_out_b])
    causal = jnp.tril(jnp.ones((CHUNK, CHUNK), bool))
    ws = jnp.where(causal[None], w_spatial, 0.0)
    ws_bf, wst_bf = ws.astype(BF16), ws.transpose(0, 2, 1).astype(BF16)
    bsp = jnp.repeat(b_spatial.T, GROUP_W, axis=1)

    p, xb = _proj(x2, win4, b_in.reshape(N_CHIPS, 1, W_BLOCK))
    h1, ya, yb, h3, s, mixed, dr, drb, acc_f = _forward_tiles(p, x2, tgt, wpa, wpb, wo, convw, vecs, ws_bf, bsp, tiles_per_seq)
    dh1, dp, dya, dyb, acc_b, dbin_b, dws, dbsp_acc = _backward_tiles(p, h1, ya, yb, drb, wpa, wpb, wo, vecs, ws_bf, wst_bf, bsp)
    dp, dcw, dbin_a = _conv_backward(dh1, p, dp, convw, tiles_per_seq)
    grad_x = _grad_x(dp, win4, dr).reshape(x.shape)
    g_in = _grad_w_in(xb, dp)
    g_pa, g_pb, g_o = _grad_w_square(h3, dya), _grad_w_square(s, dyb), _grad_w_square(mixed, drb)

    d_b_in = jnp.concatenate([dbin_a[0], dbin_b[0, 2 * D:]])
    d_b_spatial = jnp.sum(dbsp_acc.reshape(CHUNK, N_GROUPS, GROUP_W), axis=2).T
    small = _pack_small(
        d_b_in, [dcw[CONV_K], acc_b[0], acc_b[1], acc_b[2], acc_b[3], acc_f[2], acc_f[0], acc_f[1], zeros],
        jnp.where(causal[None], dws, 0.0), d_b_spatial, acc_f[3])
    g_conv = dcw.reshape(2, HALO // 2, N_CHIPS, D // N_CHIPS).transpose(0, 2, 1, 3)
    small = small.reshape(2, N_CHIPS, SMALL_ROWS // 8, 128)

    grads = [g_in, g_pa, g_pb, g_o, g_conv, small]
    wire = [BF16, BF16, BF16, BF16, F32, F32]
    from_sibling = _exchange_halves(grads)
    sums = [_add_halves(g, b, wd) for g, b, wd in zip(grads, from_sibling, wire)]
    from_chips = _scatter_to_chips([qw for _, qw in sums])
    mine = [_add_chips(q, b, a == 5) for a, ((q, _), b) in enumerate(zip(sums, from_chips))]
    *full, small_parts = _share_results(mine[:5], mine[5])
    grad_w_in, grad_w_pa, grad_w_pb, grad_w_o = [f.reshape(w.shape) for f, w in zip(full[:4], (w_in, w_pa, w_pb, w_o))]
    grad_conv_w = full[4].reshape(HALO, D // N_CHIPS)[:CONV_K]

    big = {}
    for name, w, g, m, v in [("w_in", w_in, grad_w_in, m_w_in, v_w_in), ("w_pa", w_pa, grad_w_pa, m_w_pa, v_w_pa),
                             ("w_pb", w_pb, grad_w_pb, m_w_pb, v_w_pb), ("w_o", w_o, grad_w_o, m_w_o, v_w_o),
                             ("conv_w", conv_w, grad_conv_w, m_conv_w, v_conv_w)]:
        big[name] = (g,) + tuple(_adamw(w, g, m, v))
    small_w = [b_in, conv_b, gn_g, gn_b, ln_v_g, ln_v_b, b_o, ln_out_g, ln_out_b]
    small_m = [m_b_in, m_conv_b, m_gn_g, m_gn_b, m_ln_v_g, m_ln_v_b, m_b_o, m_ln_out_g, m_ln_out_b]
    small_v = [v_b_in, v_conv_b, v_gn_g, v_gn_b, v_ln_v_g, v_ln_v_b, v_b_o, v_ln_out_g, v_ln_out_b]
    pack = lambda vs, wsp, bs: _pack_small(vs[0], vs[1:] + [zeros], wsp, bs, zeros)
    sg, sd, sm, sv, loss8 = _adamw_small(
        pack(small_w, w_spatial, b_spatial), small_parts.reshape(8, SMALL_ROWS // 8, 128),
        pack(small_m, m_w_spatial, m_b_spatial), pack(small_v, v_w_spatial, v_b_spatial), LOSS_ROW)
    names = ["b_in", "conv_b", "gn_g", "gn_b", "ln_v_g", "ln_v_b", "b_o", "ln_out_g", "ln_out_b", "pad", "w_spatial", "b_spatial"]
    per_kind = [dict(zip(names, _unpack_small(a))) for a in (sg, sd, sm, sv)]

    order = ["w_in", "b_in", "conv_w", "conv_b", "gn_g", "gn_b", "ln_v_g", "ln_v_b", "w_spatial", "b_spatial",
             "w_pa", "w_pb", "w_o", "b_o", "ln_out_g", "ln_out_b"]
    outs = [loss8[0, 0], grad_x]
    for kind in range(4):
        outs += [big[n][kind] if n in big else per_kind[kind][n] for n in order]
    return tuple(outs)
```

```python
import functools
import math

import jax
import jax.numpy as jnp
from jax import lax
from jax.experimental import pallas as pl
from jax.experimental.pallas import tpu as pltpu

D = 1024
N_GROUPS = 8
GROUP_W = D // N_GROUPS
CHUNK = 128
CONV_K = 31
HALO = 32
D_IN = 8 * D
N_CHIPS = 4
W_BLOCK = D_IN // N_CHIPS
ALPHA = 2.0 ** 0.25
LN_EPS = 1e-5
ADAM_LR, ADAM_B1, ADAM_B2, ADAM_EPS, ADAM_WD, ADAM_STEP = 0.001, 0.9, 0.999, 1e-08, 0.01, 10

TOKEN_TILE = 256
VMEM_LIMIT = 56 * 1024 * 1024
MESH = pl.DeviceIdType.MESH
F32, BF16 = jnp.float32, jnp.bfloat16


def _sigmoid(x):
    return 1.0 / (1.0 + jnp.exp(-x))


def _gelu(x):
    c = math.sqrt(2.0 / math.pi)
    t = jnp.tanh(c * (x + 0.044715 * (x * x * x)))
    return x * (0.5 * (1.0 + t))


def _gelu_and_grad(x):
    c = math.sqrt(2.0 / math.pi)
    x2 = x * x
    t = jnp.tanh(c * (x + 0.044715 * (x2 * x)))
    cdf = 0.5 * (1.0 + t)
    return x * cdf, cdf + 0.5 * x * (1.0 - t * t) * (c * (1.0 + 3.0 * 0.044715 * x2))


def _norm_stats(v):
    mu = jnp.mean(v, axis=-1, keepdims=True)
    vc = v - mu
    var = jnp.mean(vc * vc, axis=-1, keepdims=True)
    rstd = lax.rsqrt(var + LN_EPS)
    return vc * rstd, rstd


def _norm_bwd(dxhat, xhat, rstd):
    m1 = jnp.mean(dxhat, axis=-1, keepdims=True)
    m2 = jnp.mean(dxhat * xhat, axis=-1, keepdims=True)
    return rstd * (dxhat - m1 - xhat * m2)


def _dot(a, b):
    return jnp.dot(a, b, preferred_element_type=F32)


def _dot_nt(a, b):
    return lax.dot_general(a, b, (((1,), (1,)), ((), ())), preferred_element_type=F32)


def _dot_tn(a, b):
    return lax.dot_general(a, b, (((0,), (0,)), ((), ())), preferred_element_type=F32)


def _colsum(v):
    return jnp.sum(v, axis=0, keepdims=True)


def _full(shape):
    return pl.BlockSpec(shape, lambda *_: (0,) * len(shape))


def _resident(shape):
    return pl.BlockSpec(shape, lambda *_: (0,) * len(shape), pipeline_mode=pl.Buffered(1))


def _params(*sem):
    return pltpu.CompilerParams(dimension_semantics=sem, vmem_limit_bytes=VMEM_LIMIT)


def _chip_index():
    return (2 * lax.axis_index("x") + lax.axis_index("y")).astype(jnp.int32).reshape(1)


def _core_index():
    return lax.axis_index("c").astype(jnp.int32).reshape(1)


def _place_shard(w, rows, dtype):
    r, c = w.shape
    steps = r // 2 // rows

    def body(k_ref, w_ref, o_ref):
        o_ref[...] = w_ref[...].astype(dtype)

    return pl.pallas_call(
        body, name="place_shard",
        grid_spec=pltpu.PrefetchScalarGridSpec(
            num_scalar_prefetch=1, grid=(2, steps),
            in_specs=[pl.BlockSpec((rows, c), lambda h, i, k: (h * steps + i, 0))],
            out_specs=pl.BlockSpec((None, None, rows, c), lambda h, i, k: (k[0], h, i, 0))),
        out_shape=jax.ShapeDtypeStruct((N_CHIPS, 2, r // 2, c), dtype),
        compiler_params=_params("parallel", "parallel"),
    )(_chip_index(), w)


def _position():
    x, y, c = lax.axis_index("x"), lax.axis_index("y"), lax.axis_index("c")
    return x, y, c, 2 * x + y


def _other_chips(x, y):
    return [(1 - x, y), (x, 1 - y), (1 - x, 1 - y)]


def _any_specs(n):
    return [pl.BlockSpec(memory_space=pl.ANY)] * n


def _all_gather_weights(bufs):
    n = len(bufs)

    def body(*refs):
        outs = refs[n:2 * n]
        send, recv, fsend, frecv = refs[2 * n:]
        x, y, c, k = _position()
        chips = _other_chips(x, y)
        sibling = (x, y, 1 - c)
        first = []
        for a in range(n):
            for r, (cx, cy) in enumerate(chips):
                cp = pltpu.make_async_remote_copy(
                    src_ref=outs[a].at[k, c], dst_ref=outs[a].at[k, c],
                    send_sem=send.at[3 * a + r], recv_sem=recv.at[3 * a + r],
                    device_id=(cx, cy, c), device_id_type=MESH)
                cp.start()
                first.append(cp)
        passed = []
        for a in range(n):
            for r, (cx, cy) in enumerate(chips):
                kr = 2 * cx + cy
                pltpu.make_async_remote_copy(
                    src_ref=outs[a].at[kr, c], dst_ref=outs[a].at[kr, c],
                    send_sem=send.at[3 * a + r], recv_sem=recv.at[3 * a + r],
                    device_id=(cx, cy, c), device_id_type=MESH).wait_recv()
                cp = pltpu.make_async_remote_copy(
                    src_ref=outs[a].at[kr, c], dst_ref=outs[a].at[kr, c],
                    send_sem=fsend.at[3 * a + r], recv_sem=frecv.at[3 * a + r],
                    device_id=sibling, device_id_type=MESH)
                cp.start()
                passed.append(cp)
        for a in range(n):
            for r, (cx, cy) in enumerate(chips):
                kr = 2 * cx + cy
                pltpu.make_async_remote_copy(
                    src_ref=outs[a].at[kr, 1 - c], dst_ref=outs[a].at[kr, 1 - c],
                    send_sem=fsend.at[3 * a + r], recv_sem=frecv.at[3 * a + r],
                    device_id=sibling, device_id_type=MESH).wait_recv()
        for cp in first + passed:
            cp.wait_send()

    return pl.pallas_call(
        body, name="all_gather_weights",
        in_specs=_any_specs(n), out_specs=_any_specs(n),
        out_shape=[jax.ShapeDtypeStruct(b.shape, b.dtype) for b in bufs],
        scratch_shapes=[pltpu.SemaphoreType.DMA((3 * n,)), pltpu.SemaphoreType.DMA((3 * n,)),
                        pltpu.SemaphoreType.DMA((3 * n,)), pltpu.SemaphoreType.DMA((3 * n,))],
        input_output_aliases={a: a for a in range(n)},
    )(*bufs)


def _exchange_halves(grads):
    n = len(grads)

    def body(*refs):
        ins, outs = refs[:n], refs[n:2 * n]
        send, recv = refs[2 * n:]
        x, y, c, _ = _position()
        cps = []
        for a in range(n):
            cp = pltpu.make_async_remote_copy(
                src_ref=ins[a].at[1 - c], dst_ref=outs[a], send_sem=send.at[a], recv_sem=recv.at[a],
                device_id=(x, y, 1 - c), device_id_type=MESH)
            cp.start()
            cps.append(cp)
        for cp in cps:
            cp.wait_recv()
        for cp in cps:
            cp.wait_send()

    return pl.pallas_call(
        body, name="rs_exchange_halves",
        in_specs=_any_specs(n), out_specs=_any_specs(n),
        out_shape=[jax.ShapeDtypeStruct(g.shape[1:], g.dtype) for g in grads],
        scratch_shapes=[pltpu.SemaphoreType.DMA((n,)), pltpu.SemaphoreType.DMA((n,))],
    )(*grads)


def _scatter_to_chips(parts):
    n = len(parts)

    def body(*refs):
        ins, outs = refs[:n], refs[n:2 * n]
        send, recv = refs[2 * n:]
        x, y, c, _ = _position()
        cps = []
        for a in range(n):
            for r, (cx, cy) in enumerate(_other_chips(x, y)):
                cp = pltpu.make_async_remote_copy(
                    src_ref=ins[a].at[2 * cx + cy], dst_ref=outs[a].at[r],
                    send_sem=send.at[3 * a + r], recv_sem=recv.at[3 * a + r],
                    device_id=(cx, cy, c), device_id_type=MESH)
                cp.start()
                cps.append(cp)
        for cp in cps:
            cp.wait_recv()
        for cp in cps:
            cp.wait_send()

    return pl.pallas_call(
        body, name="rs_scatter_to_chips",
        in_specs=_any_specs(n), out_specs=_any_specs(n),
        out_shape=[jax.ShapeDtypeStruct((3,) + p.shape[1:], p.dtype) for p in parts],
        scratch_shapes=[pltpu.SemaphoreType.DMA((3 * n,)), pltpu.SemaphoreType.DMA((3 * n,))],
    )(*parts)


def _share_results(bufs, small):
    n = len(bufs)

    def body(*refs):
        outs, small_out = refs[n + 1:2 * n + 1], refs[2 * n + 1]
        send, recv, ssend, srecv = refs[2 * n + 2:]
        x, y, c, k = _position()
        cps = []
        for a in range(n):
            cp = pltpu.make_async_remote_copy(
                src_ref=outs[a].at[c], dst_ref=outs[a].at[c], send_sem=send.at[a], recv_sem=recv.at[a],
                device_id=(x, y, 1 - c), device_id_type=MESH)
            cp.start()
            cps.append(cp)
        waits = []
        for p in range(1, 8):
            px, py, pc = x ^ (p >> 2), y ^ ((p >> 1) & 1), c ^ (p & 1)
            cp = pltpu.make_async_remote_copy(
                src_ref=small_out.at[k, c], dst_ref=small_out.at[k, c], send_sem=ssend.at[p - 1],
                recv_sem=srecv.at[p - 1], device_id=(px, py, pc), device_id_type=MESH)
            cp.start()
            cps.append(cp)
            waits.append(pltpu.make_async_remote_copy(
                src_ref=small_out.at[2 * px + py, pc], dst_ref=small_out.at[2 * px + py, pc], send_sem=ssend.at[p - 1],
                recv_sem=srecv.at[p - 1], device_id=(px, py, pc), device_id_type=MESH))
        for a in range(n):
            pltpu.make_async_remote_copy(
                src_ref=outs[a].at[1 - c], dst_ref=outs[a].at[1 - c], send_sem=send.at[a], recv_sem=recv.at[a],
                device_id=(x, y, 1 - c), device_id_type=MESH).wait_recv()
        for w in waits:
            w.wait_recv()
        for cp in cps:
            cp.wait_send()

    return pl.pallas_call(
        body, name="rs_share_results",
        in_specs=_any_specs(n + 1), out_specs=_any_specs(n + 1),
        out_shape=[jax.ShapeDtypeStruct(b.shape, b.dtype) for b in bufs + [small]],
        scratch_shapes=[pltpu.SemaphoreType.DMA((n,)), pltpu.SemaphoreType.DMA((n,)),
                        pltpu.SemaphoreType.DMA((7,)), pltpu.SemaphoreType.DMA((7,))],
        input_output_aliases={a: a for a in range(n + 1)},
    )(*bufs, small)


def _row_tile(r, c):
    t = max(8, min(r, (1 << 18) // c))
    while r % t:
        t //= 2
    return t


def _add_halves(g, b1, wire_dtype):
    _, _, r, c = g.shape
    t = _row_tile(r, c)

    def body(c_ref, g_ref, b_ref, q_ref, qw_ref):
        q = g_ref[...] + b_ref[...]
        q_ref[...] = q
        qw_ref[...] = q.astype(wire_dtype)

    core = _core_index()
    return pl.pallas_call(
        body, name="rs_add_halves",
        grid_spec=pltpu.PrefetchScalarGridSpec(
            num_scalar_prefetch=1, grid=(N_CHIPS, r // t),
            in_specs=[pl.BlockSpec((None, None, t, c), lambda j, i, cr: (cr[0], j, i, 0)),
                      pl.BlockSpec((None, t, c), lambda j, i, cr: (j, i, 0))],
            out_specs=[pl.BlockSpec((None, t, c), lambda j, i, cr: (j, i, 0)),
                       pl.BlockSpec((None, t, c), lambda j, i, cr: (j, i, 0))]),
        out_shape=[jax.ShapeDtypeStruct((N_CHIPS, r, c), F32), jax.ShapeDtypeStruct((N_CHIPS, r, c), wire_dtype)],
        compiler_params=_params("parallel", "parallel"),
    )(core, g, b1)


def _add_chips(q, b2, per_device):
    _, r, c = q.shape
    t = _row_tile(r, c)

    def body(kc_ref, q_ref, b_ref, f_ref):
        f_ref[...] = ((q_ref[...] + b_ref[0].astype(F32)) + b_ref[1].astype(F32)) + b_ref[2].astype(F32)

    if per_device:
        out_spec = pl.BlockSpec((None, None, t, c), lambda i, kc: (kc[0], kc[1], i, 0))
        out_shape = jax.ShapeDtypeStruct((N_CHIPS, 2, r, c), F32)
    else:
        out_spec = pl.BlockSpec((None, t, c), lambda i, kc: (kc[1], i, 0))
        out_shape = jax.ShapeDtypeStruct((2, r, c), F32)
    return pl.pallas_call(
        body, name="rs_add_chips",
        grid_spec=pltpu.PrefetchScalarGridSpec(
            num_scalar_prefetch=1, grid=(r // t,),
            in_specs=[pl.BlockSpec((None, t, c), lambda i, kc: (kc[0], i, 0)),
                      pl.BlockSpec((3, t, c), lambda i, kc: (0, i, 0))],
            out_specs=out_spec),
        out_shape=out_shape,
        compiler_params=_params("parallel"),
    )(jnp.concatenate([_chip_index(), _core_index()]), q, b2)


def _adamw_math(w, g, m, v):
    m = ADAM_B1 * m + (1.0 - ADAM_B1) * g
    v = ADAM_B2 * v + (1.0 - ADAM_B2) * (g * g)
    m_hat = m / (1.0 - ADAM_B1 ** ADAM_STEP)
    v_hat = v / (1.0 - ADAM_B2 ** ADAM_STEP)
    delta = -ADAM_LR * (m_hat / (jnp.sqrt(v_hat) + ADAM_EPS) + ADAM_WD * w)
    return delta, m, v


def _adamw(w, g, m, v):
    r, c = w.shape
    t = _row_tile(r, c) if r % 8 == 0 else r

    def body(w_ref, g_ref, m_ref, v_ref, d_ref, nm_ref, nv_ref):
        d_ref[...], nm_ref[...], nv_ref[...] = _adamw_math(w_ref[...], g_ref[...], m_ref[...], v_ref[...])

    spec = pl.BlockSpec((t, c), lambda i: (i, 0))
    return pl.pallas_call(
        body, name="adamw", grid=(r // t,), in_specs=[spec] * 4, out_specs=[spec] * 3,
        out_shape=[jax.ShapeDtypeStruct((r, c), F32)] * 3, compiler_params=_params("parallel"),
    )(w, g, m, v)


def _adamw_small(w, parts, m, v, loss_row):
    r, c = w.shape

    def body(w_ref, p_ref, m_ref, v_ref, g_ref, d_ref, nm_ref, nv_ref, loss_ref):
        rows = r // 8
        for k in range(N_CHIPS):
            for core in range(2):
                g_ref[(core * N_CHIPS + k) * rows:(core * N_CHIPS + k + 1) * rows, :] = p_ref[2 * k + core]
        g = g_ref[...]
        d_ref[...], nm_ref[...], nv_ref[...] = _adamw_math(w_ref[...], g, m_ref[...], v_ref[...])
        lanes = g_ref[loss_row:loss_row + 8, :]
        loss_ref[...] = jnp.broadcast_to(jnp.sum(jnp.sum(lanes, axis=1, keepdims=True), axis=0, keepdims=True), (8, c))

    return pl.pallas_call(
        body, name="adamw_small",
        in_specs=[_full((r, c)), _full((8, r // 8, c)), _full((r, c)), _full((r, c))],
        out_specs=[_full((r, c))] * 4 + [_full((8, c))],
        out_shape=[jax.ShapeDtypeStruct((r, c), F32)] * 4 + [jax.ShapeDtypeStruct((8, c), F32)],
        compiler_params=_params(),
    )(w, parts, m, v)


def _proj(x, w4, b4):
    t = x.shape[0]
    tm = 1024

    def body(x_ref, w_ref, b_ref, p_ref, xt_ref, xb_ref):
        @pl.when(pl.program_id(1) == 0)
        def _():
            xb_ref[...] = x_ref[...].astype(BF16)
            xt_ref[...] = x_ref[...].T.astype(BF16)

        p_ref[...] = _dot(xb_ref[...], w_ref[...]) + b_ref[...]

    return pl.pallas_call(
        body, name="proj", grid=(t // tm, N_CHIPS),
        in_specs=[pl.BlockSpec((tm, D), lambda i, j: (i, 0)),
                  pl.BlockSpec((None, D, W_BLOCK), lambda i, j: (j, 0, 0)),
                  pl.BlockSpec((None, 1, W_BLOCK), lambda i, j: (j, 0, 0))],
        out_specs=[pl.BlockSpec((tm, W_BLOCK), lambda i, j: (i, j)),
                   pl.BlockSpec((D, tm), lambda i, j: (0, i))],
        out_shape=[jax.ShapeDtypeStruct((t, D_IN), F32), jax.ShapeDtypeStruct((D, t), BF16)],
        scratch_shapes=[pltpu.VMEM((tm, D), BF16)],
        compiler_params=_params("arbitrary", "arbitrary"),
    )(x, w4, b4)


SUBLANES = 8
SHIFT_ROWS = HALO - SUBLANES


def _shifted_copies(src_ref, sh_ref, cs, tm):
    for p in range(1, SUBLANES):
        sh_ref[p - 1] = src_ref[pl.ds(p, tm + SHIFT_ROWS), cs]


def _tap(src_ref, sh_ref, cs, offset, start, rows):
    p, q = offset % SUBLANES, offset // SUBLANES
    if p == 0:
        return src_ref[pl.ds(start + SUBLANES * q, rows), cs]
    return sh_ref[p - 1, pl.ds(start + SUBLANES * q, rows), :]


def _conv_taps(src_ref, sh_ref, w_ref, first_offset, step, bias, dst_ref, tm):
    rows = 64
    for g in range(N_GROUPS):
        cs = slice(g * GROUP_W, (g + 1) * GROUP_W)
        _shifted_copies(src_ref, sh_ref, cs, tm)
        for rb in range(tm // rows):
            acc = jnp.zeros((rows, GROUP_W), F32) + (bias[:, cs] if bias is not None else 0.0)
            for k in range(CONV_K):
                acc = acc + w_ref[k:k + 1, cs] * _tap(src_ref, sh_ref, cs, first_offset + step * k, rb * rows, rows)
            dst_ref[rb * rows:(rb + 1) * rows, cs] = acc


def _conv_weight_grad(d_ref, src_ref, sh_ref, first_offset, acc_ref, tm):
    for g in range(N_GROUPS):
        cs = slice(g * GROUP_W, (g + 1) * GROUP_W)
        _shifted_copies(src_ref, sh_ref, cs, tm)
        for k in range(CONV_K):
            prod = d_ref[:, cs] * _tap(src_ref, sh_ref, cs, first_offset + k, 0, tm)
            acc_ref[SUBLANES * k:SUBLANES * (k + 1), cs] += jnp.sum(prod.reshape(tm // SUBLANES, SUBLANES, GROUP_W), axis=0)


def _spatial_mix(w_ref, v_bf, tm):
    rows = []
    for q in range(tm // CHUNK):
        cols = [_dot(w_ref[h], v_bf[q * CHUNK:(q + 1) * CHUNK, h * GROUP_W:(h + 1) * GROUP_W])
                for h in range(N_GROUPS)]
        rows.append(jnp.concatenate(cols, axis=1))
    return jnp.concatenate(rows, axis=0)


def _group_norm_fwd(h1, gn_g, gn_b):
    xhat, rstd = [], []
    for g in range(N_GROUPS):
        xh, rs = _norm_stats(h1[:, g * GROUP_W:(g + 1) * GROUP_W])
        xhat.append(xh)
        rstd.append(rs)
    xhat = jnp.concatenate(xhat, axis=1)
    return xhat * gn_g + gn_b, xhat, rstd


def _forward_tiles(p, x, tgt, wpa, wpb, wo, convw, vecs, ws, bsp, tiles_per_seq):
    t = x.shape[0]
    tm = TOKEN_TILE
    hb = tm // HALO

    def body(p_ref, ph_ref, x_ref, t_ref, wpa_ref, wpb_ref, wo_ref, cw_ref, vec_ref, ws_ref, bsp_ref,
             h1_ref, ya_ref, yb_ref, h3_ref, s_ref, mx_ref, dr_ref, drb_ref, acc_ref, he_ref, sh_ref):
        i = pl.program_id(0)
        conv_b, gn_g, gn_b, lnv_g, lnv_b, b_o, lno_g, lno_b = [vec_ref[j:j + 1, :] for j in range(8)]

        keep = jnp.where(i % tiles_per_seq == 0, 0.0, 1.0)
        he_ref[0:HALO, :] = ph_ref[:, 0:D] * _sigmoid(ph_ref[:, D:2 * D]) * keep
        he_ref[HALO:, :] = p_ref[:, 0:D] * _sigmoid(p_ref[:, D:2 * D])
        _conv_taps(he_ref, sh_ref, cw_ref, HALO - (CONV_K - 1), 1, conv_b, h1_ref, tm)
        h2, _, _ = _group_norm_fwd(h1_ref[...], gn_g, gn_b)
        a_gate = p_ref[:, 2 * D:3 * D]
        h3 = ((h2 * _sigmoid(h2)) * (a_gate * _sigmoid(a_gate))).astype(BF16)
        h3_ref[...] = h3
        ya = _dot(h3, wpa_ref[...])
        ya_ref[...] = ya

        u = _gelu(p_ref[:, 3 * D:4 * D])
        vhat, _ = _norm_stats(_gelu(p_ref[:, 4 * D:5 * D]))
        v1 = (vhat * lnv_g + lnv_b).astype(BF16)
        b_gate = p_ref[:, 5 * D:6 * D]
        vmix = _spatial_mix(ws_ref, v1, tm) + jnp.concatenate([bsp_ref[...]] * (tm // CHUNK), axis=0)
        s = (u * vmix * (b_gate * _sigmoid(b_gate))).astype(BF16)
        s_ref[...] = s
        yb = _dot(s, wpb_ref[...])
        yb_ref[...] = yb

        mixed = (_sigmoid(p_ref[:, 6 * D:7 * D]) * ya + _sigmoid(p_ref[:, 7 * D:8 * D]) * yb).astype(BF16)
        mx_ref[...] = mixed
        r = ALPHA * x_ref[...] + (_dot(mixed, wo_ref[...]) + b_o)
        xhat, rstd = _norm_stats(r)
        err = (xhat * lno_g + lno_b) - t_ref[...]
        dout = err * (1.0 / D)
        dr = _norm_bwd(dout * lno_g, xhat, rstd)
        dr_ref[...] = dr
        drb_ref[...] = dr.astype(BF16)

        @pl.when(i == 0)
        def _():
            acc_ref[...] = jnp.zeros_like(acc_ref)

        acc_ref[0:1, :] += _colsum(dout * xhat)
        acc_ref[1:2, :] += _colsum(dout)
        acc_ref[2:3, :] += _colsum(dr)
        acc_ref[3:4, :] += _colsum(err * err) * (0.5 / D)

    tile = lambda w: pl.BlockSpec((tm, w), lambda i: (i, 0))
    f32_out = jax.ShapeDtypeStruct((t, D), F32)
    bf_out = jax.ShapeDtypeStruct((t, D), BF16)
    return pl.pallas_call(
        body, name="forward_tiles", grid=(t // tm,),
        in_specs=[tile(D_IN),
                  pl.BlockSpec((HALO, 2 * D), lambda i: (jnp.maximum(i * hb - 1, 0), 0)),
                  tile(D), tile(D), _resident((D, D)), _resident((D, D)), _resident((D, D)), _full((HALO, D)), _full((8, D)),
                  _full((N_GROUPS, CHUNK, CHUNK)), _full((CHUNK, D))],
        out_specs=[tile(D)] * 8 + [_full((8, D))],
        out_shape=[f32_out, f32_out, f32_out, bf_out, bf_out, bf_out, f32_out, bf_out,
                   jax.ShapeDtypeStruct((8, D), F32)],
        scratch_shapes=[pltpu.VMEM((tm + HALO, D), F32), pltpu.VMEM((SUBLANES - 1, tm + SHIFT_ROWS, GROUP_W), F32)],
        compiler_params=_params("arbitrary"),
    )(p, p, x, tgt, wpa, wpb, wo, convw, vecs, ws, bsp)


def _backward_tiles(p, h1, ya, yb, drb, wpa, wpb, wo, vecs, ws, wst, bsp):
    t = h1.shape[0]
    tm = TOKEN_TILE

    def body(p_ref, h1_ref, ya_ref, yb_ref, drb_ref, wpa_ref, wpb_ref, wo_ref, vec_ref, ws_ref, wst_ref, bsp_ref,
             dh1_ref, dp_ref, dya_ref, dyb_ref, acc_ref, dbin_ref, dws_ref, dbsp_ref):
        i = pl.program_id(0)
        _, gn_g, gn_b, lnv_g, lnv_b = [vec_ref[j:j + 1, :] for j in range(5)]

        @pl.when(i == 0)
        def _():
            acc_ref[...] = jnp.zeros_like(acc_ref)
            dbin_ref[...] = jnp.zeros_like(dbin_ref)
            dws_ref[...] = jnp.zeros_like(dws_ref)
            dbsp_ref[...] = jnp.zeros_like(dbsp_ref)

        def emit(block, val):
            dbin_ref[0:1, block * D:(block + 1) * D] += _colsum(val)
            dp_ref[:, block * D:(block + 1) * D] = val.astype(BF16)

        dp_ref[:, 0:2 * D] = jnp.zeros((tm, 2 * D), BF16)
        dmixed = _dot_nt(drb_ref[...], wo_ref[...])
        ga = _sigmoid(p_ref[:, 6 * D:7 * D])
        gb = _sigmoid(p_ref[:, 7 * D:8 * D])
        dya = (dmixed * ga).astype(BF16)
        dyb = (dmixed * gb).astype(BF16)
        dya_ref[...] = dya
        dyb_ref[...] = dyb
        emit(6, dmixed * ya_ref[...] * (ga * (1.0 - ga)))
        emit(7, dmixed * yb_ref[...] * (gb * (1.0 - gb)))

        dh3 = _dot_nt(dya, wpa_ref[...])
        h2, xhat, rstd = _group_norm_fwd(h1_ref[...], gn_g, gn_b)
        sg = _sigmoid(h2)
        a_gate = p_ref[:, 2 * D:3 * D]
        sa = _sigmoid(a_gate)
        dh2 = dh3 * (a_gate * sa) * (sg * (1.0 + h2 * (1.0 - sg)))
        emit(2, dh3 * (h2 * sg) * (sa * (1.0 + a_gate * (1.0 - sa))))
        acc_ref[0:1, :] += _colsum(dh2 * xhat)
        acc_ref[1:2, :] += _colsum(dh2)
        dxhat = dh2 * gn_g
        for g in range(N_GROUPS):
            cs = slice(g * GROUP_W, (g + 1) * GROUP_W)
            dh1_ref[:, cs] = _norm_bwd(dxhat[:, cs], xhat[:, cs], rstd[g])

        ds = _dot_nt(dyb, wpb_ref[...])
        u_pre = p_ref[:, 3 * D:4 * D]
        u, du_dpre = _gelu_and_grad(u_pre)
        v0, dv_dpre = _gelu_and_grad(p_ref[:, 4 * D:5 * D])
        vhat, vrstd = _norm_stats(v0)
        v1 = (vhat * lnv_g + lnv_b).astype(BF16)
        vmix = _spatial_mix(ws_ref, v1, tm) + jnp.concatenate([bsp_ref[...]] * (tm // CHUNK), axis=0)
        b_gate = p_ref[:, 5 * D:6 * D]
        sb = _sigmoid(b_gate)
        silu_b = b_gate * sb
        emit(3, ds * vmix * silu_b * du_dpre)
        emit(5, ds * u * vmix * (sb * (1.0 + b_gate * (1.0 - sb))))
        dvmix = ds * u * silu_b
        dvmix_bf = dvmix.astype(BF16)
        for q in range(tm // CHUNK):
            dbsp_ref[...] += dvmix[q * CHUNK:(q + 1) * CHUNK, :]
            for h in range(N_GROUPS):
                blk = (slice(q * CHUNK, (q + 1) * CHUNK), slice(h * GROUP_W, (h + 1) * GROUP_W))
                dws_ref[h] += _dot_nt(dvmix_bf[blk], v1[blk])
        dv1 = _spatial_mix(wst_ref, dvmix_bf, tm)
        acc_ref[2:3, :] += _colsum(dv1 * vhat)
        acc_ref[3:4, :] += _colsum(dv1)
        emit(4, _norm_bwd(dv1 * lnv_g, vhat, vrstd) * dv_dpre)

    tile = lambda w: pl.BlockSpec((tm, w), lambda i: (i, 0))
    return pl.pallas_call(
        body, name="backward_tiles", grid=(t // tm,),
        in_specs=[tile(D_IN), tile(D), tile(D), tile(D), tile(D), _resident((D, D)), _resident((D, D)), _resident((D, D)),
                  _full((8, D)), _full((N_GROUPS, CHUNK, CHUNK)), _full((N_GROUPS, CHUNK, CHUNK)), _full((CHUNK, D))],
        out_specs=[tile(D), tile(D_IN), tile(D), tile(D), _full((8, D)), _full((8, D_IN)),
                   _full((N_GROUPS, CHUNK, CHUNK)), _full((CHUNK, D))],
        out_shape=[jax.ShapeDtypeStruct((t, D), F32), jax.ShapeDtypeStruct((t, D_IN), BF16),
                   jax.ShapeDtypeStruct((t, D), BF16), jax.ShapeDtypeStruct((t, D), BF16),
                   jax.ShapeDtypeStruct((8, D), F32), jax.ShapeDtypeStruct((8, D_IN), F32),
                   jax.ShapeDtypeStruct((N_GROUPS, CHUNK, CHUNK), F32), jax.ShapeDtypeStruct((CHUNK, D), F32)],
        compiler_params=_params("arbitrary"),
    )(p, h1, ya, yb, drb, wpa, wpb, wo, vecs, ws, wst, bsp)


def _conv_backward(dh1, p, dp, convw, tiles_per_seq):
    t = dh1.shape[0]
    tm = TOKEN_TILE
    hb = tm // HALO
    last = t // HALO - 1

    def body(dh1_ref, dnext_ref, p_ref, ph_ref, cw_ref, dp_in_ref, dp_ref, dcw_ref, dbin_ref,
             de_ref, he_ref, dh0_ref, sh_ref):
        del dp_in_ref
        i = pl.program_id(0)

        @pl.when(i == 0)
        def _():
            dcw_ref[...] = jnp.zeros_like(dcw_ref)
            dbin_ref[...] = jnp.zeros_like(dbin_ref)

        keep_next = jnp.where(i % tiles_per_seq == tiles_per_seq - 1, 0.0, 1.0)
        de_ref[0:tm, :] = dh1_ref[...]
        de_ref[tm:, :] = dnext_ref[...] * keep_next
        _conv_taps(de_ref, sh_ref, cw_ref, CONV_K - 1, -1, None, dh0_ref, tm)

        keep_prev = jnp.where(i % tiles_per_seq == 0, 0.0, 1.0)
        sg = _sigmoid(p_ref[:, D:2 * D])
        val = p_ref[:, 0:D]
        he_ref[0:HALO, :] = ph_ref[:, 0:D] * _sigmoid(ph_ref[:, D:2 * D]) * keep_prev
        he_ref[HALO:, :] = val * sg
        _conv_weight_grad(dh1_ref, he_ref, sh_ref, HALO - (CONV_K - 1), dcw_ref, tm)
        dcw_ref[SUBLANES * CONV_K:, :] += jnp.sum(dh1_ref[...].reshape(tm // SUBLANES, SUBLANES, D), axis=0)

        dh0 = dh0_ref[...]
        dval = dh0 * sg
        dglu = dh0 * val * (sg * (1.0 - sg))
        dbin_ref[0:1, 0:D] += _colsum(dval)
        dbin_ref[0:1, D:2 * D] += _colsum(dglu)
        dp_ref[:, 0:D] = dval.astype(BF16)
        dp_ref[:, D:2 * D] = dglu.astype(BF16)

    return pl.pallas_call(
        body, name="conv_backward", grid=(t // tm,),
        in_specs=[pl.BlockSpec((tm, D), lambda i: (i, 0)),
                  pl.BlockSpec((HALO, D), lambda i: (jnp.minimum((i + 1) * hb, last), 0)),
                  pl.BlockSpec((tm, 2 * D), lambda i: (i, 0)),
                  pl.BlockSpec((HALO, 2 * D), lambda i: (jnp.maximum(i * hb - 1, 0), 0)),
                  _full((HALO, D)), pl.BlockSpec(memory_space=pl.ANY)],
        out_specs=[pl.BlockSpec((tm, 2 * D), lambda i: (i, 0)), _full((SUBLANES * HALO, D)), _full((8, 2 * D))],
        out_shape=[jax.ShapeDtypeStruct(dp.shape, BF16), jax.ShapeDtypeStruct((SUBLANES * HALO, D), F32),
                   jax.ShapeDtypeStruct((8, 2 * D), F32)],
        scratch_shapes=[pltpu.VMEM((tm + HALO, D), F32), pltpu.VMEM((tm + HALO, D), F32), pltpu.VMEM((tm, D), F32),
                        pltpu.VMEM((SUBLANES - 1, tm + SHIFT_ROWS, GROUP_W), F32)],
        input_output_aliases={5: 0},
        compiler_params=_params("arbitrary"),
    )(dh1, dh1, p, p, convw, dp)


def _grad_w_in(xb, dp):
    t = xb.shape[1]
    half = D // 2
    tn = 512

    def body(xt_ref, dp_ref, o_ref):
        g = _dot(xt_ref[...], dp_ref[...])
        o_ref[0] = g[:half]
        o_ref[1] = g[half:]

    return pl.pallas_call(
        body, name="grad_w_in", grid=(N_CHIPS, W_BLOCK // tn),
        in_specs=[_resident((D, t)), pl.BlockSpec((t, tn), lambda j, n: (0, j * (W_BLOCK // tn) + n))],
        out_specs=pl.BlockSpec((2, None, half, tn), lambda j, n: (0, j, 0, n)),
        out_shape=jax.ShapeDtypeStruct((2, N_CHIPS, half, W_BLOCK), F32),
        compiler_params=_params("parallel", "parallel"),
    )(xb, dp)


def _grad_w_square(a, b):
    t = a.shape[0]
    tk = 512
    rows = D // 8

    def body(a_ref, b_ref, o_ref, acc_ref):
        s = pl.program_id(0)

        @pl.when(s == 0)
        def _():
            acc_ref[...] = jnp.zeros_like(acc_ref)

        acc_ref[...] += _dot_tn(a_ref[...], b_ref[...])

        @pl.when(s == pl.num_programs(0) - 1)
        def _():
            for j in range(N_CHIPS):
                for h in range(2):
                    o_ref[h, j] = acc_ref[(2 * j + h) * rows:(2 * j + h + 1) * rows, :]

    return pl.pallas_call(
        body, name="grad_w_square", grid=(t // tk,),
        in_specs=[pl.BlockSpec((tk, D), lambda s: (s, 0)), pl.BlockSpec((tk, D), lambda s: (s, 0))],
        out_specs=_full((2, N_CHIPS, rows, D)),
        out_shape=jax.ShapeDtypeStruct((2, N_CHIPS, rows, D), F32),
        scratch_shapes=[pltpu.VMEM((D, D), F32)],
        compiler_params=_params("arbitrary"),
    )(a, b)


def _grad_x(dp, w4, dr):
    t = dr.shape[0]
    tm = TOKEN_TILE

    def body(dp_ref, w_ref, dr_ref, o_ref):
        acc = ALPHA * dr_ref[...]
        for j in range(N_CHIPS):
            acc = acc + _dot_nt(dp_ref[:, j * W_BLOCK:(j + 1) * W_BLOCK], w_ref[j])
        o_ref[...] = acc

    return pl.pallas_call(
        body, name="grad_x", grid=(t // tm,),
        in_specs=[pl.BlockSpec((tm, D_IN), lambda i: (i, 0)),
                  pl.BlockSpec((N_CHIPS, D, W_BLOCK), lambda i: (0, 0, 0), pipeline_mode=pl.Buffered(1)),
                  pl.BlockSpec((tm, D), lambda i: (i, 0))],
        out_specs=pl.BlockSpec((tm, D), lambda i: (i, 0)),
        out_shape=jax.ShapeDtypeStruct((t, D), F32),
        compiler_params=_params("parallel"),
    )(dp, w4, dr)


SMALL_ROWS = 1216


def _pack_small(b_in, vec9, w_spatial, b_spatial, loss_lanes):
    parts = [b_in.reshape(64, 128)] + [v.reshape(8, 128) for v in vec9]
    parts += [w_spatial.reshape(N_GROUPS * CHUNK, CHUNK), b_spatial.reshape(8, 128), loss_lanes.reshape(8, 128)]
    used = sum(p.shape[0] for p in parts)
    return jnp.concatenate(parts + [jnp.zeros((SMALL_ROWS - used, 128), F32)], axis=0)


LOSS_ROW = 64 + 9 * 8 + N_GROUPS * CHUNK + 8


def _unpack_small(a):
    out, row = [], 0
    for rows, shape in [(64, (D_IN,))] + [(8, (D,))] * 9 + [(N_GROUPS * CHUNK, (N_GROUPS, CHUNK, CHUNK)), (8, (N_GROUPS, CHUNK))]:
        out.append(a[row:row + rows].reshape(shape))
        row += rows
    return out


def kernel(x, w_in, b_in, conv_w, conv_b, gn_g, gn_b, ln_v_g, ln_v_b, w_spatial, b_spatial, w_pa, w_pb, w_o, b_o, ln_out_g, ln_out_b, loss_target, m_w_in, m_b_in, m_conv_w, m_conv_b, m_gn_g, m_gn_b, m_ln_v_g, m_ln_v_b, m_w_spatial, m_b_spatial, m_w_pa, m_w_pb, m_w_o, m_b_o, m_ln_out_g, m_ln_out_b, v_w_in, v_b_in, v_conv_w, v_conv_b, v_gn_g, v_gn_b, v_ln_v_g, v_ln_v_b, v_w_spatial, v_b_spatial, v_w_pa, v_w_pb, v_w_o, v_b_o, v_ln_out_g, v_ln_out_b):
    n_seq, seq, _ = x.shape
    t = n_seq * seq
    tiles_per_seq = seq // TOKEN_TILE
    x2 = x.reshape(t, D)
    tgt = loss_target.reshape(t, D)

    conv_shard = jnp.pad(conv_w, ((0, HALO - CONV_K), (0, 0)))
    win4, wpa4, wpb4, wo4, conv4 = _all_gather_weights(
        [_place_shard(w_in, 256, BF16), _place_shard(w_pa, 128, BF16), _place_shard(w_pb, 128, BF16),
         _place_shard(w_o, 128, BF16), _place_shard(conv_shard, HALO // 2, F32)])
    win4 = win4.reshape(N_CHIPS, D, W_BLOCK)
    wpa, wpb, wo = wpa4.reshape(D, D), wpb4.reshape(D, D), wo4.reshape(D, D)
    convw = conv4.reshape(N_CHIPS, HALO, D // N_CHIPS).transpose(1, 0, 2).reshape(HALO, D)

    zeros = jnp.zeros((D,), F32)
    vecs = jnp.stack([conv_b, gn_g, gn_b, ln_v_g, ln_v_b, b_o, ln_out_g, ln_out_b])
    causal = jnp.tril(jnp.ones((CHUNK, CHUNK), bool))
    ws = jnp.where(causal[None], w_spatial, 0.0)
    ws_bf, wst_bf = ws.astype(BF16), ws.transpose(0, 2, 1).astype(BF16)
    bsp = jnp.repeat(b_spatial.T, GROUP_W, axis=1)

    p, xb = _proj(x2, win4, b_in.reshape(N_CHIPS, 1, W_BLOCK))
    h1, ya, yb, h3, s, mixed, dr, drb, acc_f = _forward_tiles(p, x2, tgt, wpa, wpb, wo, convw, vecs, ws_bf, bsp, tiles_per_seq)
    dh1, dp, dya, dyb, acc_b, dbin_b, dws, dbsp_acc = _backward_tiles(p, h1, ya, yb, drb, wpa, wpb, wo, vecs, ws_bf, wst_bf, bsp)
    dp, dcw8, dbin_a = _conv_backward(dh1, p, dp, convw, tiles_per_seq)
    dcw = jnp.sum(dcw8.reshape(HALO, SUBLANES, D), axis=1)
    grad_x = _grad_x(dp, win4, dr).reshape(x.shape)
    g_in = _grad_w_in(xb, dp)
    g_pa, g_pb, g_o = _grad_w_square(h3, dya), _grad_w_square(s, dyb), _grad_w_square(mixed, drb)

    d_b_in = jnp.concatenate([dbin_a[0], dbin_b[0, 2 * D:]])
    d_b_spatial = jnp.sum(dbsp_acc.reshape(CHUNK, N_GROUPS, GROUP_W), axis=2).T
    small = _pack_small(
        d_b_in, [dcw[CONV_K], acc_b[0], acc_b[1], acc_b[2], acc_b[3], acc_f[2], acc_f[0], acc_f[1], zeros],
        jnp.where(causal[None], dws, 0.0), d_b_spatial, acc_f[3])
    g_conv = dcw.reshape(2, HALO // 2, N_CHIPS, D // N_CHIPS).transpose(0, 2, 1, 3)
    small = small.reshape(2, N_CHIPS, SMALL_ROWS // 8, 128)

    grads = [g_in, g_pa, g_pb, g_o, g_conv, small]
    wire = [BF16, BF16, BF16, BF16, F32, F32]
    from_sibling = _exchange_halves(grads)
    sums = [_add_halves(g, b, wd) for g, b, wd in zip(grads, from_sibling, wire)]
    from_chips = _scatter_to_chips([qw for _, qw in sums])
    mine = [_add_chips(q, b, a == 5) for a, ((q, _), b) in enumerate(zip(sums, from_chips))]
    *full, small_parts = _share_results(mine[:5], mine[5])
    grad_w_in, grad_w_pa, grad_w_pb, grad_w_o = [f.reshape(w.shape) for f, w in zip(full[:4], (w_in, w_pa, w_pb, w_o))]
    grad_conv_w = full[4].reshape(HALO, D // N_CHIPS)[:CONV_K]

    big = {}
    for name, w, g, m, v in [("w_in", w_in, grad_w_in, m_w_in, v_w_in), ("w_pa", w_pa, grad_w_pa, m_w_pa, v_w_pa),
                             ("w_pb", w_pb, grad_w_pb, m_w_pb, v_w_pb), ("w_o", w_o, grad_w_o, m_w_o, v_w_o),
                             ("conv_w", conv_w, grad_conv_w, m_conv_w, v_conv_w)]:
        big[name] = (g,) + tuple(_adamw(w, g, m, v))
    small_w = [b_in, conv_b, gn_g, gn_b, ln_v_g, ln_v_b, b_o, ln_out_g, ln_out_b]
    small_m = [m_b_in, m_conv_b, m_gn_g, m_gn_b, m_ln_v_g, m_ln_v_b, m_b_o, m_ln_out_g, m_ln_out_b]
    small_v = [v_b_in, v_conv_b, v_gn_g, v_gn_b, v_ln_v_g, v_ln_v_b, v_b_o, v_ln_out_g, v_ln_out_b]
    pack = lambda vs, wsp, bs: _pack_small(vs[0], vs[1:] + [zeros], wsp, bs, zeros)
    sg, sd, sm, sv, loss8 = _adamw_small(
        pack(small_w, w_spatial, b_spatial), small_parts.reshape(8, SMALL_ROWS // 8, 128),
        pack(small_m, m_w_spatial, m_b_spatial), pack(small_v, v_w_spatial, v_b_spatial), LOSS_ROW)
    names = ["b_in", "conv_b", "gn_g", "gn_b", "ln_v_g", "ln_v_b", "b_o", "ln_out_g", "ln_out_b", "pad", "w_spatial", "b_spatial"]
    per_kind = [dict(zip(names, _unpack_small(a))) for a in (sg, sd, sm, sv)]

    order = ["w_in", "b_in", "conv_w", "conv_b", "gn_g", "gn_b", "ln_v_g", "ln_v_b", "w_spatial", "b_spatial",
             "w_pa", "w_pb", "w_o", "b_o", "ln_out_g", "ln_out_b"]
    outs = [loss8[0, 0], grad_x]
    for kind in range(4):
        outs += [big[n][kind] if n in big else per_kind[kind][n] for n in order]
    return tuple(outs)
```

```python
import functools
import math

import jax
import jax.numpy as jnp
from jax import lax
from jax.experimental import pallas as pl
from jax.experimental.pallas import tpu as pltpu

D = 1024
N_GROUPS = 8
GROUP_W = D // N_GROUPS
CHUNK = 128
CONV_K = 31
HALO = 32
D_IN = 8 * D
N_CHIPS = 4
W_BLOCK = D_IN // N_CHIPS
ALPHA = 2.0 ** 0.25
LN_EPS = 1e-5
ADAM_LR, ADAM_B1, ADAM_B2, ADAM_EPS, ADAM_WD, ADAM_STEP = 0.001, 0.9, 0.999, 1e-08, 0.01, 10

TOKEN_TILE = 256
VMEM_LIMIT = 56 * 1024 * 1024
MESH = pl.DeviceIdType.MESH
F32, BF16 = jnp.float32, jnp.bfloat16


def _sigmoid(x):
    return 1.0 / (1.0 + jnp.exp(-x))


def _gelu(x):
    c = math.sqrt(2.0 / math.pi)
    t = jnp.tanh(c * (x + 0.044715 * (x * x * x)))
    return x * (0.5 * (1.0 + t))


def _gelu_and_grad(x):
    c = math.sqrt(2.0 / math.pi)
    x2 = x * x
    t = jnp.tanh(c * (x + 0.044715 * (x2 * x)))
    cdf = 0.5 * (1.0 + t)
    return x * cdf, cdf + 0.5 * x * (1.0 - t * t) * (c * (1.0 + 3.0 * 0.044715 * x2))


def _norm_stats(v):
    mu = jnp.mean(v, axis=-1, keepdims=True)
    vc = v - mu
    var = jnp.mean(vc * vc, axis=-1, keepdims=True)
    rstd = lax.rsqrt(var + LN_EPS)
    return vc * rstd, rstd


def _norm_bwd(dxhat, xhat, rstd):
    m1 = jnp.mean(dxhat, axis=-1, keepdims=True)
    m2 = jnp.mean(dxhat * xhat, axis=-1, keepdims=True)
    return rstd * (dxhat - m1 - xhat * m2)


def _dot(a, b):
    return jnp.dot(a, b, preferred_element_type=F32)


def _dot_nt(a, b):
    return lax.dot_general(a, b, (((1,), (1,)), ((), ())), preferred_element_type=F32)


def _dot_tn(a, b):
    return lax.dot_general(a, b, (((0,), (0,)), ((), ())), preferred_element_type=F32)


def _colsum(v):
    return jnp.sum(v, axis=0, keepdims=True)


def _full(shape):
    return pl.BlockSpec(shape, lambda *_: (0,) * len(shape))


def _resident(shape):
    return pl.BlockSpec(shape, lambda *_: (0,) * len(shape), pipeline_mode=pl.Buffered(1))


def _params(*sem):
    return pltpu.CompilerParams(dimension_semantics=sem, vmem_limit_bytes=VMEM_LIMIT)


def _chip_index():
    return (2 * lax.axis_index("x") + lax.axis_index("y")).astype(jnp.int32).reshape(1)


def _core_index():
    return lax.axis_index("c").astype(jnp.int32).reshape(1)


def _place_shard(w, rows, dtype):
    r, c = w.shape
    steps = r // 2 // rows

    def body(k_ref, w_ref, o_ref):
        o_ref[...] = w_ref[...].astype(dtype)

    return pl.pallas_call(
        body, name="place_shard",
        grid_spec=pltpu.PrefetchScalarGridSpec(
            num_scalar_prefetch=1, grid=(2, steps),
            in_specs=[pl.BlockSpec((rows, c), lambda h, i, k: (h * steps + i, 0))],
            out_specs=pl.BlockSpec((None, None, rows, c), lambda h, i, k: (k[0], h, i, 0))),
        out_shape=jax.ShapeDtypeStruct((N_CHIPS, 2, r // 2, c), dtype),
        compiler_params=_params("parallel", "parallel"),
    )(_chip_index(), w)


def _position():
    x, y, c = lax.axis_index("x"), lax.axis_index("y"), lax.axis_index("c")
    return x, y, c, 2 * x + y


def _other_chips(x, y):
    return [(1 - x, y), (x, 1 - y), (1 - x, 1 - y)]


def _any_specs(n):
    return [pl.BlockSpec(memory_space=pl.ANY)] * n


def _proj_gather(x, b4, bufs):
    t = x.shape[0]
    tm = 1024
    steps = t // tm
    half = D // 2
    n = len(bufs)
    xi, yi = lax.axis_index("x"), lax.axis_index("y")
    order = jnp.stack([2 * xi + yi, 2 * (1 - xi) + yi, 2 * xi + (1 - yi), 2 * (1 - xi) + (1 - yi)]).astype(jnp.int32)

    def body(order_ref, x_ref, b_ref, *refs):
        p_ref, outs = refs[n], refs[n + 1:2 * n + 1]
        xb_ref, w_ref, lsem, send, recv, fsend, frecv = refs[2 * n + 1:]
        jj, i = pl.program_id(0), pl.program_id(1)
        x_, y_, c, k = _position()
        chips = _other_chips(x_, y_)

        def sent(a, r):
            cx, cy = chips[r]
            return pltpu.make_async_remote_copy(
                src_ref=outs[a].at[k, c], dst_ref=outs[a].at[k, c], send_sem=send.at[3 * a + r],
                recv_sem=recv.at[3 * a + r], device_id=(cx, cy, c), device_id_type=MESH)

        def landed(a, r):
            cx, cy = chips[r]
            return pltpu.make_async_remote_copy(
                src_ref=outs[a].at[2 * cx + cy, c], dst_ref=outs[a].at[2 * cx + cy, c], send_sem=send.at[3 * a + r],
                recv_sem=recv.at[3 * a + r], device_id=(cx, cy, c), device_id_type=MESH)

        def passed(a, r, h):
            cx, cy = chips[r]
            return pltpu.make_async_remote_copy(
                src_ref=outs[a].at[2 * cx + cy, h], dst_ref=outs[a].at[2 * cx + cy, h], send_sem=fsend.at[3 * a + r],
                recv_sem=frecv.at[3 * a + r], device_id=(x_, y_, 1 - c), device_id_type=MESH)

        def load(block):
            cps = [pltpu.make_async_copy(outs[0].at[block, h], w_ref.at[pl.ds(h * half, half)], lsem.at[h])
                   for h in range(2)]
            for cp in cps:
                cp.start()
            for cp in cps:
                cp.wait()

        @pl.when((jj == 0) & (i == 0))
        def _():
            for a in range(n):
                for r in range(3):
                    sent(a, r).start()
            load(k)

        for r in range(3):
            @pl.when((jj == r + 1) & (i == 0))
            def _(r=r):
                landed(0, r).wait_recv()
                passed(0, r, c).start()
                passed(0, r, 1 - c).wait_recv()
                load(2 * chips[r][0] + chips[r][1])

        rows = pl.ds(pl.multiple_of(i * tm, tm), tm)

        @pl.when(jj == 0)
        def _():
            xb_ref[rows, :] = x_ref[...].astype(BF16)

        p_ref[...] = _dot(xb_ref[rows, :], w_ref[...]) + b_ref[...]

        @pl.when((jj == N_CHIPS - 1) & (i == steps - 1))
        def _():
            for a in range(1, n):
                for r in range(3):
                    landed(a, r).wait_recv()
                    passed(a, r, c).start()
            for a in range(1, n):
                for r in range(3):
                    passed(a, r, 1 - c).wait_recv()
            for a in range(n):
                for r in range(3):
                    sent(a, r).wait_send()
                    passed(a, r, c).wait_send()

    any_spec = pl.BlockSpec(memory_space=pl.ANY)
    return pl.pallas_call(
        body, name="proj_gather",
        grid_spec=pltpu.PrefetchScalarGridSpec(
            num_scalar_prefetch=1, grid=(N_CHIPS, steps),
            in_specs=[pl.BlockSpec((tm, D), lambda jj, i, o: (jnp.where(jj == 0, i, steps - 1), 0)),
                      pl.BlockSpec((None, 1, W_BLOCK), lambda jj, i, o: (o[jj], 0, 0))] + [any_spec] * n,
            out_specs=[pl.BlockSpec((tm, W_BLOCK), lambda jj, i, o: (i, o[jj]))] + [any_spec] * n,
            scratch_shapes=[pltpu.VMEM((t, D), BF16), pltpu.VMEM((D, W_BLOCK), BF16), pltpu.SemaphoreType.DMA((2,)),
                            pltpu.SemaphoreType.DMA((3 * n,)), pltpu.SemaphoreType.DMA((3 * n,)),
                            pltpu.SemaphoreType.DMA((3 * n,)), pltpu.SemaphoreType.DMA((3 * n,))]),
        out_shape=[jax.ShapeDtypeStruct((t, D_IN), F32)] + [jax.ShapeDtypeStruct(b.shape, b.dtype) for b in bufs],
        input_output_aliases={3 + a: 1 + a for a in range(n)},
        compiler_params=_params("arbitrary", "arbitrary"),
    )(order, x, b4, *bufs)


def _exchange_halves(grads):
    n = len(grads)

    def body(*refs):
        ins, outs = refs[:n], refs[n:2 * n]
        send, recv = refs[2 * n:]
        x, y, c, _ = _position()
        cps = []
        for a in range(n):
            cp = pltpu.make_async_remote_copy(
                src_ref=ins[a].at[1 - c], dst_ref=outs[a], send_sem=send.at[a], recv_sem=recv.at[a],
                device_id=(x, y, 1 - c), device_id_type=MESH)
            cp.start()
            cps.append(cp)
        for cp in cps:
            cp.wait_recv()
        for cp in cps:
            cp.wait_send()

    return pl.pallas_call(
        body, name="rs_exchange_halves",
        in_specs=_any_specs(n), out_specs=_any_specs(n),
        out_shape=[jax.ShapeDtypeStruct(g.shape[1:], g.dtype) for g in grads],
        scratch_shapes=[pltpu.SemaphoreType.DMA((n,)), pltpu.SemaphoreType.DMA((n,))],
    )(*grads)


def _scatter_to_chips(parts):
    n = len(parts)

    def body(*refs):
        ins, outs = refs[:n], refs[n:2 * n]
        send, recv = refs[2 * n:]
        x, y, c, _ = _position()
        cps = []
        for a in range(n):
            for r, (cx, cy) in enumerate(_other_chips(x, y)):
                cp = pltpu.make_async_remote_copy(
                    src_ref=ins[a].at[2 * cx + cy], dst_ref=outs[a].at[r],
                    send_sem=send.at[3 * a + r], recv_sem=recv.at[3 * a + r],
                    device_id=(cx, cy, c), device_id_type=MESH)
                cp.start()
                cps.append(cp)
        for cp in cps:
            cp.wait_recv()
        for cp in cps:
            cp.wait_send()

    return pl.pallas_call(
        body, name="rs_scatter_to_chips",
        in_specs=_any_specs(n), out_specs=_any_specs(n),
        out_shape=[jax.ShapeDtypeStruct((3,) + p.shape[1:], p.dtype) for p in parts],
        scratch_shapes=[pltpu.SemaphoreType.DMA((3 * n,)), pltpu.SemaphoreType.DMA((3 * n,))],
    )(*parts)


def _share_results(bufs, small):
    n = len(bufs)

    def body(*refs):
        outs, small_out = refs[n + 1:2 * n + 1], refs[2 * n + 1]
        send, recv, ssend, srecv = refs[2 * n + 2:]
        x, y, c, k = _position()
        cps = []
        for a in range(n):
            cp = pltpu.make_async_remote_copy(
                src_ref=outs[a].at[c], dst_ref=outs[a].at[c], send_sem=send.at[a], recv_sem=recv.at[a],
                device_id=(x, y, 1 - c), device_id_type=MESH)
            cp.start()
            cps.append(cp)
        waits = []
        for p in range(1, 8):
            px, py, pc = x ^ (p >> 2), y ^ ((p >> 1) & 1), c ^ (p & 1)
            cp = pltpu.make_async_remote_copy(
                src_ref=small_out.at[k, c], dst_ref=small_out.at[k, c], send_sem=ssend.at[p - 1],
                recv_sem=srecv.at[p - 1], device_id=(px, py, pc), device_id_type=MESH)
            cp.start()
            cps.append(cp)
            waits.append(pltpu.make_async_remote_copy(
                src_ref=small_out.at[2 * px + py, pc], dst_ref=small_out.at[2 * px + py, pc], send_sem=ssend.at[p - 1],
                recv_sem=srecv.at[p - 1], device_id=(px, py, pc), device_id_type=MESH))
        for a in range(n):
            pltpu.make_async_remote_copy(
                src_ref=outs[a].at[1 - c], dst_ref=outs[a].at[1 - c], send_sem=send.at[a], recv_sem=recv.at[a],
                device_id=(x, y, 1 - c), device_id_type=MESH).wait_recv()
        for w in waits:
            w.wait_recv()
        for cp in cps:
            cp.wait_send()

    return pl.pallas_call(
        body, name="rs_share_results",
        in_specs=_any_specs(n + 1), out_specs=_any_specs(n + 1),
        out_shape=[jax.ShapeDtypeStruct(b.shape, b.dtype) for b in bufs + [small]],
        scratch_shapes=[pltpu.SemaphoreType.DMA((n,)), pltpu.SemaphoreType.DMA((n,)),
                        pltpu.SemaphoreType.DMA((7,)), pltpu.SemaphoreType.DMA((7,))],
        input_output_aliases={a: a for a in range(n + 1)},
    )(*bufs, small)


def _row_tile(r, c):
    t = max(8, min(r, (1 << 18) // c))
    while r % t:
        t //= 2
    return t


def _add_halves(g, b1, wire_dtype):
    _, _, r, c = g.shape
    t = _row_tile(r, c)

    def body(c_ref, g_ref, b_ref, q_ref, qw_ref):
        q = g_ref[...] + b_ref[...]
        q_ref[...] = q
        qw_ref[...] = q.astype(wire_dtype)

    core = _core_index()
    return pl.pallas_call(
        body, name="rs_add_halves",
        grid_spec=pltpu.PrefetchScalarGridSpec(
            num_scalar_prefetch=1, grid=(N_CHIPS, r // t),
            in_specs=[pl.BlockSpec((None, None, t, c), lambda j, i, cr: (cr[0], j, i, 0)),
                      pl.BlockSpec((None, t, c), lambda j, i, cr: (j, i, 0))],
            out_specs=[pl.BlockSpec((None, t, c), lambda j, i, cr: (j, i, 0)),
                       pl.BlockSpec((None, t, c), lambda j, i, cr: (j, i, 0))]),
        out_shape=[jax.ShapeDtypeStruct((N_CHIPS, r, c), F32), jax.ShapeDtypeStruct((N_CHIPS, r, c), wire_dtype)],
        compiler_params=_params("parallel", "parallel"),
    )(core, g, b1)


def _add_chips(q, b2, per_device):
    _, r, c = q.shape
    t = _row_tile(r, c)

    def body(kc_ref, q_ref, b_ref, f_ref):
        f_ref[...] = ((q_ref[...] + b_ref[0].astype(F32)) + b_ref[1].astype(F32)) + b_ref[2].astype(F32)

    if per_device:
        out_spec = pl.BlockSpec((None, None, t, c), lambda i, kc: (kc[0], kc[1], i, 0))
        out_shape = jax.ShapeDtypeStruct((N_CHIPS, 2, r, c), F32)
    else:
        out_spec = pl.BlockSpec((None, t, c), lambda i, kc: (kc[1], i, 0))
        out_shape = jax.ShapeDtypeStruct((2, r, c), F32)
    return pl.pallas_call(
        body, name="rs_add_chips",
        grid_spec=pltpu.PrefetchScalarGridSpec(
            num_scalar_prefetch=1, grid=(r // t,),
            in_specs=[pl.BlockSpec((None, t, c), lambda i, kc: (kc[0], i, 0)),
                      pl.BlockSpec((3, t, c), lambda i, kc: (0, i, 0))],
            out_specs=out_spec),
        out_shape=out_shape,
        compiler_params=_params("parallel"),
    )(jnp.concatenate([_chip_index(), _core_index()]), q, b2)


def _adamw_math(w, g, m, v):
    m = ADAM_B1 * m + (1.0 - ADAM_B1) * g
    v = ADAM_B2 * v + (1.0 - ADAM_B2) * (g * g)
    m_hat = m / (1.0 - ADAM_B1 ** ADAM_STEP)
    v_hat = v / (1.0 - ADAM_B2 ** ADAM_STEP)
    delta = -ADAM_LR * (m_hat / (jnp.sqrt(v_hat) + ADAM_EPS) + ADAM_WD * w)
    return delta, m, v


def _adamw(w, g, m, v):
    r, c = w.shape
    t = _row_tile(r, c) if r % 8 == 0 else r

    def body(w_ref, g_ref, m_ref, v_ref, d_ref, nm_ref, nv_ref):
        d_ref[...], nm_ref[...], nv_ref[...] = _adamw_math(w_ref[...], g_ref[...], m_ref[...], v_ref[...])

    spec = pl.BlockSpec((t, c), lambda i: (i, 0))
    return pl.pallas_call(
        body, name="adamw", grid=(r // t,), in_specs=[spec] * 4, out_specs=[spec] * 3,
        out_shape=[jax.ShapeDtypeStruct((r, c), F32)] * 3, compiler_params=_params("parallel"),
    )(w, g, m, v)


def _adamw_small(w, parts, m, v, loss_row):
    r, c = w.shape

    def body(w_ref, p_ref, m_ref, v_ref, g_ref, d_ref, nm_ref, nv_ref, loss_ref):
        rows = r // 8
        for k in range(N_CHIPS):
            for core in range(2):
                g_ref[(core * N_CHIPS + k) * rows:(core * N_CHIPS + k + 1) * rows, :] = p_ref[2 * k + core]
        g = g_ref[...]
        d_ref[...], nm_ref[...], nv_ref[...] = _adamw_math(w_ref[...], g, m_ref[...], v_ref[...])
        lanes = g_ref[loss_row:loss_row + 8, :]
        loss_ref[...] = jnp.broadcast_to(jnp.sum(jnp.sum(lanes, axis=1, keepdims=True), axis=0, keepdims=True), (8, c))

    return pl.pallas_call(
        body, name="adamw_small",
        in_specs=[_full((r, c)), _full((8, r // 8, c)), _full((r, c)), _full((r, c))],
        out_specs=[_full((r, c))] * 4 + [_full((8, c))],
        out_shape=[jax.ShapeDtypeStruct((r, c), F32)] * 4 + [jax.ShapeDtypeStruct((8, c), F32)],
        compiler_params=_params(),
    )(w, parts, m, v)


SUBLANES = 8
SHIFT_ROWS = HALO - SUBLANES


def _shifted_copies(src_ref, sh_ref, cs, tm):
    for p in range(1, SUBLANES):
        sh_ref[p - 1] = src_ref[pl.ds(p, tm + SHIFT_ROWS), cs]


def _tap(src_ref, sh_ref, cs, offset, start, rows):
    p, q = offset % SUBLANES, offset // SUBLANES
    if p == 0:
        return src_ref[pl.ds(start + SUBLANES * q, rows), cs]
    return sh_ref[p - 1, pl.ds(start + SUBLANES * q, rows), :]


def _conv_taps(src_ref, sh_ref, w_ref, first_offset, step, bias, dst_ref, tm):
    rows = 64
    for g in range(N_GROUPS):
        cs = slice(g * GROUP_W, (g + 1) * GROUP_W)
        _shifted_copies(src_ref, sh_ref, cs, tm)
        for rb in range(tm // rows):
            acc = jnp.zeros((rows, GROUP_W), F32) + (bias[:, cs] if bias is not None else 0.0)
            for k in range(CONV_K):
                acc = acc + w_ref[k:k + 1, cs] * _tap(src_ref, sh_ref, cs, first_offset + step * k, rb * rows, rows)
            dst_ref[rb * rows:(rb + 1) * rows, cs] = acc


def _conv_weight_grad(d_ref, src_ref, sh_ref, first_offset, acc_ref, tm):
    for g in range(N_GROUPS):
        cs = slice(g * GROUP_W, (g + 1) * GROUP_W)
        _shifted_copies(src_ref, sh_ref, cs, tm)
        for k in range(CONV_K):
            prod = d_ref[:, cs] * _tap(src_ref, sh_ref, cs, first_offset + k, 0, tm)
            acc_ref[SUBLANES * k:SUBLANES * (k + 1), cs] += jnp.sum(prod.reshape(tm // SUBLANES, SUBLANES, GROUP_W), axis=0)


def _spatial_mix(w_ref, v_bf, tm):
    rows = []
    for q in range(tm // CHUNK):
        cols = [_dot(w_ref[h], v_bf[q * CHUNK:(q + 1) * CHUNK, h * GROUP_W:(h + 1) * GROUP_W])
                for h in range(N_GROUPS)]
        rows.append(jnp.concatenate(cols, axis=1))
    return jnp.concatenate(rows, axis=0)


def _group_norm_fwd(h1, gn_g, gn_b):
    xhat, rstd = [], []
    for g in range(N_GROUPS):
        xh, rs = _norm_stats(h1[:, g * GROUP_W:(g + 1) * GROUP_W])
        xhat.append(xh)
        rstd.append(rs)
    xhat = jnp.concatenate(xhat, axis=1)
    return xhat * gn_g + gn_b, xhat, rstd


def _forward_tiles(p, x, tgt, wpa, wpb, wo, convw, vecs, ws, bsp, tiles_per_seq):
    t = x.shape[0]
    tm = TOKEN_TILE
    hb = tm // HALO

    def body(p_ref, ph_ref, x_ref, t_ref, wpa_ref, wpb_ref, wo_ref, cw_ref, vec_ref, ws_ref, bsp_ref,
             h1_ref, ya_ref, yb_ref, h3_ref, s_ref, mx_ref, dr_ref, drb_ref, xt_ref, acc_ref, he_ref, sh_ref):
        i = pl.program_id(0)
        xt_ref[...] = x_ref[...].T.astype(BF16)
        conv_b, gn_g, gn_b, lnv_g, lnv_b, b_o, lno_g, lno_b = [vec_ref[j:j + 1, :] for j in range(8)]

        keep = jnp.where(i % tiles_per_seq == 0, 0.0, 1.0)
        he_ref[0:HALO, :] = ph_ref[:, 0:D] * _sigmoid(ph_ref[:, D:2 * D]) * keep
        he_ref[HALO:, :] = p_ref[:, 0:D] * _sigmoid(p_ref[:, D:2 * D])
        _conv_taps(he_ref, sh_ref, cw_ref, HALO - (CONV_K - 1), 1, conv_b, h1_ref, tm)
        h2, _, _ = _group_norm_fwd(h1_ref[...], gn_g, gn_b)
        a_gate = p_ref[:, 2 * D:3 * D]
        h3 = ((h2 * _sigmoid(h2)) * (a_gate * _sigmoid(a_gate))).astype(BF16)
        h3_ref[...] = h3
        ya = _dot(h3, wpa_ref[...])
        ya_ref[...] = ya

        u = _gelu(p_ref[:, 3 * D:4 * D])
        vhat, _ = _norm_stats(_gelu(p_ref[:, 4 * D:5 * D]))
        v1 = (vhat * lnv_g + lnv_b).astype(BF16)
        b_gate = p_ref[:, 5 * D:6 * D]
        vmix = _spatial_mix(ws_ref, v1, tm) + jnp.concatenate([bsp_ref[...]] * (tm // CHUNK), axis=0)
        s = (u * vmix * (b_gate * _sigmoid(b_gate))).astype(BF16)
        s_ref[...] = s
        yb = _dot(s, wpb_ref[...])
        yb_ref[...] = yb

        mixed = (_sigmoid(p_ref[:, 6 * D:7 * D]) * ya + _sigmoid(p_ref[:, 7 * D:8 * D]) * yb).astype(BF16)
        mx_ref[...] = mixed
        r = ALPHA * x_ref[...] + (_dot(mixed, wo_ref[...]) + b_o)
        xhat, rstd = _norm_stats(r)
        err = (xhat * lno_g + lno_b) - t_ref[...]
        dout = err * (1.0 / D)
        dr = _norm_bwd(dout * lno_g, xhat, rstd)
        dr_ref[...] = dr
        drb_ref[...] = dr.astype(BF16)

        @pl.when(i == 0)
        def _():
            acc_ref[...] = jnp.zeros_like(acc_ref)

        acc_ref[0:1, :] += _colsum(dout * xhat)
        acc_ref[1:2, :] += _colsum(dout)
        acc_ref[2:3, :] += _colsum(dr)
        acc_ref[3:4, :] += _colsum(err * err) * (0.5 / D)

    tile = lambda w: pl.BlockSpec((tm, w), lambda i: (i, 0))
    f32_out = jax.ShapeDtypeStruct((t, D), F32)
    bf_out = jax.ShapeDtypeStruct((t, D), BF16)
    return pl.pallas_call(
        body, name="forward_tiles", grid=(t // tm,),
        in_specs=[tile(D_IN),
                  pl.BlockSpec((HALO, 2 * D), lambda i: (jnp.maximum(i * hb - 1, 0), 0)),
                  tile(D), tile(D), _resident((D, D)), _resident((D, D)), _resident((D, D)), _full((HALO, D)), _full((8, D)),
                  _full((N_GROUPS, CHUNK, CHUNK)), _full((CHUNK, D))],
        out_specs=[tile(D)] * 8 + [pl.BlockSpec((D, tm), lambda i: (0, i)), _full((8, D))],
        out_shape=[f32_out, f32_out, f32_out, bf_out, bf_out, bf_out, f32_out, bf_out,
                   jax.ShapeDtypeStruct((D, t), BF16), jax.ShapeDtypeStruct((8, D), F32)],
        scratch_shapes=[pltpu.VMEM((tm + HALO, D), F32), pltpu.VMEM((SUBLANES - 1, tm + SHIFT_ROWS, GROUP_W), F32)],
        compiler_params=_params("arbitrary"),
    )(p, p, x, tgt, wpa, wpb, wo, convw, vecs, ws, bsp)


def _backward_tiles(p, h1, ya, yb, drb, wpa, wpb, wo, vecs, ws, wst, bsp):
    t = h1.shape[0]
    tm = TOKEN_TILE

    def body(p_ref, h1_ref, ya_ref, yb_ref, drb_ref, wpa_ref, wpb_ref, wo_ref, vec_ref, ws_ref, wst_ref, bsp_ref,
             dh1_ref, dp_ref, dya_ref, dyb_ref, acc_ref, dbin_ref, dws_ref, dbsp_ref):
        i = pl.program_id(0)
        _, gn_g, gn_b, lnv_g, lnv_b = [vec_ref[j:j + 1, :] for j in range(5)]

        @pl.when(i == 0)
        def _():
            acc_ref[...] = jnp.zeros_like(acc_ref)
            dbin_ref[...] = jnp.zeros_like(dbin_ref)
            dws_ref[...] = jnp.zeros_like(dws_ref)
            dbsp_ref[...] = jnp.zeros_like(dbsp_ref)

        def emit(block, val):
            dbin_ref[0:1, block * D:(block + 1) * D] += _colsum(val)
            dp_ref[:, block * D:(block + 1) * D] = val.astype(BF16)

        dp_ref[:, 0:2 * D] = jnp.zeros((tm, 2 * D), BF16)
        dmixed = _dot_nt(drb_ref[...], wo_ref[...])
        ga = _sigmoid(p_ref[:, 6 * D:7 * D])
        gb = _sigmoid(p_ref[:, 7 * D:8 * D])
        dya = (dmixed * ga).astype(BF16)
        dyb = (dmixed * gb).astype(BF16)
        dya_ref[...] = dya
        dyb_ref[...] = dyb
        emit(6, dmixed * ya_ref[...] * (ga * (1.0 - ga)))
        emit(7, dmixed * yb_ref[...] * (gb * (1.0 - gb)))

        dh3 = _dot_nt(dya, wpa_ref[...])
        h2, xhat, rstd = _group_norm_fwd(h1_ref[...], gn_g, gn_b)
        sg = _sigmoid(h2)
        a_gate = p_ref[:, 2 * D:3 * D]
        sa = _sigmoid(a_gate)
        dh2 = dh3 * (a_gate * sa) * (sg * (1.0 + h2 * (1.0 - sg)))
        emit(2, dh3 * (h2 * sg) * (sa * (1.0 + a_gate * (1.0 - sa))))
        acc_ref[0:1, :] += _colsum(dh2 * xhat)
        acc_ref[1:2, :] += _colsum(dh2)
        dxhat = dh2 * gn_g
        for g in range(N_GROUPS):
            cs = slice(g * GROUP_W, (g + 1) * GROUP_W)
            dh1_ref[:, cs] = _norm_bwd(dxhat[:, cs], xhat[:, cs], rstd[g])

        ds = _dot_nt(dyb, wpb_ref[...])
        u_pre = p_ref[:, 3 * D:4 * D]
        u, du_dpre = _gelu_and_grad(u_pre)
        v0, dv_dpre = _gelu_and_grad(p_ref[:, 4 * D:5 * D])
        vhat, vrstd = _norm_stats(v0)
        v1 = (vhat * lnv_g + lnv_b).astype(BF16)
        vmix = _spatial_mix(ws_ref, v1, tm) + jnp.concatenate([bsp_ref[...]] * (tm // CHUNK), axis=0)
        b_gate = p_ref[:, 5 * D:6 * D]
        sb = _sigmoid(b_gate)
        silu_b = b_gate * sb
        emit(3, ds * vmix * silu_b * du_dpre)
        emit(5, ds * u * vmix * (sb * (1.0 + b_gate * (1.0 - sb))))
        dvmix = ds * u * silu_b
        dvmix_bf = dvmix.astype(BF16)
        for q in range(tm // CHUNK):
            dbsp_ref[...] += dvmix[q * CHUNK:(q + 1) * CHUNK, :]
            for h in range(N_GROUPS):
                blk = (slice(q * CHUNK, (q + 1) * CHUNK), slice(h * GROUP_W, (h + 1) * GROUP_W))
                dws_ref[h] += _dot_nt(dvmix_bf[blk], v1[blk])
        dv1 = _spatial_mix(wst_ref, dvmix_bf, tm)
        acc_ref[2:3, :] += _colsum(dv1 * vhat)
        acc_ref[3:4, :] += _colsum(dv1)
        emit(4, _norm_bwd(dv1 * lnv_g, vhat, vrstd) * dv_dpre)

    tile = lambda w: pl.BlockSpec((tm, w), lambda i: (i, 0))
    return pl.pallas_call(
        body, name="backward_tiles", grid=(t // tm,),
        in_specs=[tile(D_IN), tile(D), tile(D), tile(D), tile(D), _resident((D, D)), _resident((D, D)), _resident((D, D)),
                  _full((8, D)), _full((N_GROUPS, CHUNK, CHUNK)), _full((N_GROUPS, CHUNK, CHUNK)), _full((CHUNK, D))],
        out_specs=[tile(D), tile(D_IN), tile(D), tile(D), _full((8, D)), _full((8, D_IN)),
                   _full((N_GROUPS, CHUNK, CHUNK)), _full((CHUNK, D))],
        out_shape=[jax.ShapeDtypeStruct((t, D), F32), jax.ShapeDtypeStruct((t, D_IN), BF16),
                   jax.ShapeDtypeStruct((t, D), BF16), jax.ShapeDtypeStruct((t, D), BF16),
                   jax.ShapeDtypeStruct((8, D), F32), jax.ShapeDtypeStruct((8, D_IN), F32),
                   jax.ShapeDtypeStruct((N_GROUPS, CHUNK, CHUNK), F32), jax.ShapeDtypeStruct((CHUNK, D), F32)],
        compiler_params=_params("arbitrary"),
    )(p, h1, ya, yb, drb, wpa, wpb, wo, vecs, ws, wst, bsp)


def _conv_backward(dh1, p, dp, convw, tiles_per_seq):
    t = dh1.shape[0]
    tm = TOKEN_TILE
    hb = tm // HALO
    last = t // HALO - 1

    def body(dh1_ref, dnext_ref, p_ref, ph_ref, cw_ref, dp_in_ref, dp_ref, dcw_ref, dbin_ref,
             de_ref, he_ref, dh0_ref, sh_ref):
        del dp_in_ref
        i = pl.program_id(0)

        @pl.when(i == 0)
        def _():
            dcw_ref[...] = jnp.zeros_like(dcw_ref)
            dbin_ref[...] = jnp.zeros_like(dbin_ref)

        keep_next = jnp.where(i % tiles_per_seq == tiles_per_seq - 1, 0.0, 1.0)
        de_ref[0:tm, :] = dh1_ref[...]
        de_ref[tm:, :] = dnext_ref[...] * keep_next
        _conv_taps(de_ref, sh_ref, cw_ref, CONV_K - 1, -1, None, dh0_ref, tm)

        keep_prev = jnp.where(i % tiles_per_seq == 0, 0.0, 1.0)
        sg = _sigmoid(p_ref[:, D:2 * D])
        val = p_ref[:, 0:D]
        he_ref[0:HALO, :] = ph_ref[:, 0:D] * _sigmoid(ph_ref[:, D:2 * D]) * keep_prev
        he_ref[HALO:, :] = val * sg
        _conv_weight_grad(dh1_ref, he_ref, sh_ref, HALO - (CONV_K - 1), dcw_ref, tm)
        dcw_ref[SUBLANES * CONV_K:, :] += jnp.sum(dh1_ref[...].reshape(tm // SUBLANES, SUBLANES, D), axis=0)

        dh0 = dh0_ref[...]
        dval = dh0 * sg
        dglu = dh0 * val * (sg * (1.0 - sg))
        dbin_ref[0:1, 0:D] += _colsum(dval)
        dbin_ref[0:1, D:2 * D] += _colsum(dglu)
        dp_ref[:, 0:D] = dval.astype(BF16)
        dp_ref[:, D:2 * D] = dglu.astype(BF16)

    return pl.pallas_call(
        body, name="conv_backward", grid=(t // tm,),
        in_specs=[pl.BlockSpec((tm, D), lambda i: (i, 0)),
                  pl.BlockSpec((HALO, D), lambda i: (jnp.minimum((i + 1) * hb, last), 0)),
                  pl.BlockSpec((tm, 2 * D), lambda i: (i, 0)),
                  pl.BlockSpec((HALO, 2 * D), lambda i: (jnp.maximum(i * hb - 1, 0), 0)),
                  _full((HALO, D)), pl.BlockSpec(memory_space=pl.ANY)],
        out_specs=[pl.BlockSpec((tm, 2 * D), lambda i: (i, 0)), _full((SUBLANES * HALO, D)), _full((8, 2 * D))],
        out_shape=[jax.ShapeDtypeStruct(dp.shape, BF16), jax.ShapeDtypeStruct((SUBLANES * HALO, D), F32),
                   jax.ShapeDtypeStruct((8, 2 * D), F32)],
        scratch_shapes=[pltpu.VMEM((tm + HALO, D), F32), pltpu.VMEM((tm + HALO, D), F32), pltpu.VMEM((tm, D), F32),
                        pltpu.VMEM((SUBLANES - 1, tm + SHIFT_ROWS, GROUP_W), F32)],
        input_output_aliases={5: 0},
        compiler_params=_params("arbitrary"),
    )(dh1, dh1, p, p, convw, dp)


def _grad_w_in(xt, dp):
    t = xt.shape[1]
    half = D // 2
    tn = 512

    def body(xt_ref, dp_ref, o_ref):
        g = _dot(xt_ref[...], dp_ref[...])
        o_ref[0] = g[:half]
        o_ref[1] = g[half:]

    return pl.pallas_call(
        body, name="grad_w_in", grid=(N_CHIPS, W_BLOCK // tn),
        in_specs=[_resident((D, t)), pl.BlockSpec((t, tn), lambda j, n: (0, j * (W_BLOCK // tn) + n))],
        out_specs=pl.BlockSpec((2, None, half, tn), lambda j, n: (0, j, 0, n)),
        out_shape=jax.ShapeDtypeStruct((2, N_CHIPS, half, W_BLOCK), F32),
        compiler_params=_params("parallel", "parallel"),
    )(xt, dp)


def _grad_w_square(a, b):
    t = a.shape[0]
    tk = 512
    rows = D // 8

    def body(a_ref, b_ref, o_ref, acc_ref):
        s = pl.program_id(0)

        @pl.when(s == 0)
        def _():
            acc_ref[...] = jnp.zeros_like(acc_ref)

        acc_ref[...] += _dot_tn(a_ref[...], b_ref[...])

        @pl.when(s == pl.num_programs(0) - 1)
        def _():
            for j in range(N_CHIPS):
                for h in range(2):
                    o_ref[h, j] = acc_ref[(2 * j + h) * rows:(2 * j + h + 1) * rows, :]

    return pl.pallas_call(
        body, name="grad_w_square", grid=(t // tk,),
        in_specs=[pl.BlockSpec((tk, D), lambda s: (s, 0)), pl.BlockSpec((tk, D), lambda s: (s, 0))],
        out_specs=_full((2, N_CHIPS, rows, D)),
        out_shape=jax.ShapeDtypeStruct((2, N_CHIPS, rows, D), F32),
        scratch_shapes=[pltpu.VMEM((D, D), F32)],
        compiler_params=_params("arbitrary"),
    )(a, b)


def _grad_x(dp, w4, dr):
    t = dr.shape[0]
    tm = TOKEN_TILE

    def body(dp_ref, w_ref, dr_ref, o_ref):
        acc = ALPHA * dr_ref[...]
        for j in range(N_CHIPS):
            acc = acc + _dot_nt(dp_ref[:, j * W_BLOCK:(j + 1) * W_BLOCK], w_ref[j])
        o_ref[...] = acc

    return pl.pallas_call(
        body, name="grad_x", grid=(t // tm,),
        in_specs=[pl.BlockSpec((tm, D_IN), lambda i: (i, 0)),
                  pl.BlockSpec((N_CHIPS, D, W_BLOCK), lambda i: (0, 0, 0), pipeline_mode=pl.Buffered(1)),
                  pl.BlockSpec((tm, D), lambda i: (i, 0))],
        out_specs=pl.BlockSpec((tm, D), lambda i: (i, 0)),
        out_shape=jax.ShapeDtypeStruct((t, D), F32),
        compiler_params=_params("parallel"),
    )(dp, w4, dr)


SMALL_ROWS = 1216


def _pack_small(b_in, vec9, w_spatial, b_spatial, loss_lanes):
    parts = [b_in.reshape(64, 128)] + [v.reshape(8, 128) for v in vec9]
    parts += [w_spatial.reshape(N_GROUPS * CHUNK, CHUNK), b_spatial.reshape(8, 128), loss_lanes.reshape(8, 128)]
    used = sum(p.shape[0] for p in parts)
    return jnp.concatenate(parts + [jnp.zeros((SMALL_ROWS - used, 128), F32)], axis=0)


LOSS_ROW = 64 + 9 * 8 + N_GROUPS * CHUNK + 8


def _unpack_small(a):
    out, row = [], 0
    for rows, shape in [(64, (D_IN,))] + [(8, (D,))] * 9 + [(N_GROUPS * CHUNK, (N_GROUPS, CHUNK, CHUNK)), (8, (N_GROUPS, CHUNK))]:
        out.append(a[row:row + rows].reshape(shape))
        row += rows
    return out


def kernel(x, w_in, b_in, conv_w, conv_b, gn_g, gn_b, ln_v_g, ln_v_b, w_spatial, b_spatial, w_pa, w_pb, w_o, b_o, ln_out_g, ln_out_b, loss_target, m_w_in, m_b_in, m_conv_w, m_conv_b, m_gn_g, m_gn_b, m_ln_v_g, m_ln_v_b, m_w_spatial, m_b_spatial, m_w_pa, m_w_pb, m_w_o, m_b_o, m_ln_out_g, m_ln_out_b, v_w_in, v_b_in, v_conv_w, v_conv_b, v_gn_g, v_gn_b, v_ln_v_g, v_ln_v_b, v_w_spatial, v_b_spatial, v_w_pa, v_w_pb, v_w_o, v_b_o, v_ln_out_g, v_ln_out_b):
    n_seq, seq, _ = x.shape
    t = n_seq * seq
    tiles_per_seq = seq // TOKEN_TILE
    x2 = x.reshape(t, D)
    tgt = loss_target.reshape(t, D)

    conv_shard = jnp.pad(conv_w, ((0, HALO - CONV_K), (0, 0)))
    p, win4, wpa4, wpb4, wo4, conv4 = _proj_gather(
        x2, b_in.reshape(N_CHIPS, 1, W_BLOCK),
        [_place_shard(w_in, 256, BF16), _place_shard(w_pa, 128, BF16), _place_shard(w_pb, 128, BF16),
         _place_shard(w_o, 128, BF16), _place_shard(conv_shard, HALO // 2, F32)])
    win4 = win4.reshape(N_CHIPS, D, W_BLOCK)
    wpa, wpb, wo = wpa4.reshape(D, D), wpb4.reshape(D, D), wo4.reshape(D, D)
    convw = conv4.reshape(N_CHIPS, HALO, D // N_CHIPS).transpose(1, 0, 2).reshape(HALO, D)

    zeros = jnp.zeros((D,), F32)
    vecs = jnp.stack([conv_b, gn_g, gn_b, ln_v_g, ln_v_b, b_o, ln_out_g, ln_out_b])
    causal = jnp.tril(jnp.ones((CHUNK, CHUNK), bool))
    ws = jnp.where(causal[None], w_spatial, 0.0)
    ws_bf, wst_bf = ws.astype(BF16), ws.transpose(0, 2, 1).astype(BF16)
    bsp = jnp.repeat(b_spatial.T, GROUP_W, axis=1)

    h1, ya, yb, h3, s, mixed, dr, drb, xt, acc_f = _forward_tiles(p, x2, tgt, wpa, wpb, wo, convw, vecs, ws_bf, bsp, tiles_per_seq)
    dh1, dp, dya, dyb, acc_b, dbin_b, dws, dbsp_acc = _backward_tiles(p, h1, ya, yb, drb, wpa, wpb, wo, vecs, ws_bf, wst_bf, bsp)
    dp, dcw8, dbin_a = _conv_backward(dh1, p, dp, convw, tiles_per_seq)
    dcw = jnp.sum(dcw8.reshape(HALO, SUBLANES, D), axis=1)
    grad_x = _grad_x(dp, win4, dr).reshape(x.shape)
    g_in = _grad_w_in(xt, dp)
    g_pa, g_pb, g_o = _grad_w_square(h3, dya), _grad_w_square(s, dyb), _grad_w_square(mixed, drb)

    d_b_in = jnp.concatenate([dbin_a[0], dbin_b[0, 2 * D:]])
    d_b_spatial = jnp.sum(dbsp_acc.reshape(CHUNK, N_GROUPS, GROUP_W), axis=2).T
    small = _pack_small(
        d_b_in, [dcw[CONV_K], acc_b[0], acc_b[1], acc_b[2], acc_b[3], acc_f[2], acc_f[0], acc_f[1], zeros],
        jnp.where(causal[None], dws, 0.0), d_b_spatial, acc_f[3])
    g_conv = dcw.reshape(2, HALO // 2, N_CHIPS, D // N_CHIPS).transpose(0, 2, 1, 3)
    small = small.reshape(2, N_CHIPS, SMALL_ROWS // 8, 128)

    grads = [g_in, g_pa, g_pb, g_o, g_conv, small]
    wire = [BF16, BF16, BF16, BF16, F32, F32]
    from_sibling = _exchange_halves(grads)
    sums = [_add_halves(g, b, wd) for g, b, wd in zip(grads, from_sibling, wire)]
    from_chips = _scatter_to_chips([qw for _, qw in sums])
    mine = [_add_chips(q, b, a == 5) for a, ((q, _), b) in enumerate(zip(sums, from_chips))]
    *full, small_parts = _share_results(mine[:5], mine[5])
    grad_w_in, grad_w_pa, grad_w_pb, grad_w_o = [f.reshape(w.shape) for f, w in zip(full[:4], (w_in, w_pa, w_pb, w_o))]
    grad_conv_w = full[4].reshape(HALO, D // N_CHIPS)[:CONV_K]

    big = {}
    for name, w, g, m, v in [("w_in", w_in, grad_w_in, m_w_in, v_w_in), ("w_pa", w_pa, grad_w_pa, m_w_pa, v_w_pa),
                             ("w_pb", w_pb, grad_w_pb, m_w_pb, v_w_pb), ("w_o", w_o, grad_w_o, m_w_o, v_w_o),
                             ("conv_w", conv_w, grad_conv_w, m_conv_w, v_conv_w)]:
        big[name] = (g,) + tuple(_adamw(w, g, m, v))
    small_w = [b_in, conv_b, gn_g, gn_b, ln_v_g, ln_v_b, b_o, ln_out_g, ln_out_b]
    small_m = [m_b_in, m_conv_b, m_gn_g, m_gn_b, m_ln_v_g, m_ln_v_b, m_b_o, m_ln_out_g, m_ln_out_b]
    small_v = [v_b_in, v_conv_b, v_gn_g, v_gn_b, v_ln_v_g, v_ln_v_b, v_b_o, v_ln_out_g, v_ln_out_b]
    pack = lambda vs, wsp, bs: _pack_small(vs[0], vs[1:] + [zeros], wsp, bs, zeros)
    sg, sd, sm, sv, loss8 = _adamw_small(
        pack(small_w, w_spatial, b_spatial), small_parts.reshape(8, SMALL_ROWS // 8, 128),
        pack(small_m, m_w_spatial, m_b_spatial), pack(small_v, v_w_spatial, v_b_spatial), LOSS_ROW)
    names = ["b_in", "conv_b", "gn_g", "gn_b", "ln_v_g", "ln_v_b", "b_o", "ln_out_g", "ln_out_b", "pad", "w_spatial", "b_spatial"]
    per_kind = [dict(zip(names, _unpack_small(a))) for a in (sg, sd, sm, sv)]

    order = ["w_in", "b_in", "conv_w", "conv_b", "gn_g", "gn_b", "ln_v_g", "ln_v_b", "w_spatial", "b_spatial",
             "w_pa", "w_pb", "w_o", "b_o", "ln_out_g", "ln_out_b"]
    outs = [loss8[0, 0], grad_x]
    for kind in range(4):
        outs += [big[n][kind] if n in big else per_kind[kind][n] for n in order]
    return tuple(outs)
```

```python
import functools
import math

import jax
import jax.numpy as jnp
from jax import lax
from jax.experimental import pallas as pl
from jax.experimental.pallas import tpu as pltpu

D = 1024
N_GROUPS = 8
GROUP_W = D // N_GROUPS
CHUNK = 128
CONV_K = 31
HALO = 32
D_IN = 8 * D
N_CHIPS = 4
W_BLOCK = D_IN // N_CHIPS
ALPHA = 2.0 ** 0.25
LN_EPS = 1e-5
ADAM_LR, ADAM_B1, ADAM_B2, ADAM_EPS, ADAM_WD, ADAM_STEP = 0.001, 0.9, 0.999, 1e-08, 0.01, 10

TOKEN_TILE = 256
VMEM_LIMIT = 56 * 1024 * 1024
MESH = pl.DeviceIdType.MESH
F32, BF16 = jnp.float32, jnp.bfloat16


def _sigmoid(x):
    return 1.0 / (1.0 + jnp.exp(-x))


def _gelu(x):
    c = math.sqrt(2.0 / math.pi)
    t = jnp.tanh(c * (x + 0.044715 * (x * x * x)))
    return x * (0.5 * (1.0 + t))


def _gelu_and_grad(x):
    c = math.sqrt(2.0 / math.pi)
    x2 = x * x
    t = jnp.tanh(c * (x + 0.044715 * (x2 * x)))
    cdf = 0.5 * (1.0 + t)
    return x * cdf, cdf + 0.5 * x * (1.0 - t * t) * (c * (1.0 + 3.0 * 0.044715 * x2))


def _norm_stats(v):
    mu = jnp.mean(v, axis=-1, keepdims=True)
    vc = v - mu
    var = jnp.mean(vc * vc, axis=-1, keepdims=True)
    rstd = lax.rsqrt(var + LN_EPS)
    return vc * rstd, rstd


def _norm_bwd(dxhat, xhat, rstd):
    m1 = jnp.mean(dxhat, axis=-1, keepdims=True)
    m2 = jnp.mean(dxhat * xhat, axis=-1, keepdims=True)
    return rstd * (dxhat - m1 - xhat * m2)


def _dot(a, b):
    return jnp.dot(a, b, preferred_element_type=F32)


def _dot_nt(a, b):
    return lax.dot_general(a, b, (((1,), (1,)), ((), ())), preferred_element_type=F32)


def _dot_tn(a, b):
    return lax.dot_general(a, b, (((0,), (0,)), ((), ())), preferred_element_type=F32)


def _colsum(v):
    return jnp.sum(v, axis=0, keepdims=True)


def _full(shape):
    return pl.BlockSpec(shape, lambda *_: (0,) * len(shape))


def _resident(shape):
    return pl.BlockSpec(shape, lambda *_: (0,) * len(shape), pipeline_mode=pl.Buffered(1))


def _params(*sem):
    return pltpu.CompilerParams(dimension_semantics=sem, vmem_limit_bytes=VMEM_LIMIT)


def _chip_index():
    return (2 * lax.axis_index("x") + lax.axis_index("y")).astype(jnp.int32).reshape(1)


def _core_index():
    return lax.axis_index("c").astype(jnp.int32).reshape(1)


def _place_shard(w, rows, dtype):
    r, c = w.shape
    steps = r // 2 // rows

    def body(k_ref, w_ref, o_ref):
        o_ref[...] = w_ref[...].astype(dtype)

    return pl.pallas_call(
        body, name="place_shard",
        grid_spec=pltpu.PrefetchScalarGridSpec(
            num_scalar_prefetch=1, grid=(2, steps),
            in_specs=[pl.BlockSpec((rows, c), lambda h, i, k: (h * steps + i, 0))],
            out_specs=pl.BlockSpec((None, None, rows, c), lambda h, i, k: (k[0], h, i, 0))),
        out_shape=jax.ShapeDtypeStruct((N_CHIPS, 2, r // 2, c), dtype),
        compiler_params=_params("parallel", "parallel"),
    )(_chip_index(), w)


def _position():
    x, y, c = lax.axis_index("x"), lax.axis_index("y"), lax.axis_index("c")
    return x, y, c, 2 * x + y


def _other_chips(x, y):
    return [(1 - x, y), (x, 1 - y), (1 - x, 1 - y)]


def _any_specs(n):
    return [pl.BlockSpec(memory_space=pl.ANY)] * n


def _proj_gather(x, b4, bufs):
    t = x.shape[0]
    tm = 1024
    steps = t // tm
    half = D // 2
    n = len(bufs)
    xi, yi = lax.axis_index("x"), lax.axis_index("y")
    order = jnp.stack([2 * xi + yi, 2 * (1 - xi) + yi, 2 * xi + (1 - yi), 2 * (1 - xi) + (1 - yi)]).astype(jnp.int32)

    def body(order_ref, x_ref, b_ref, *refs):
        p_ref, outs = refs[n], refs[n + 1:2 * n + 1]
        xb_ref, w_ref, lsem, send, recv, fsend, frecv = refs[2 * n + 1:]
        jj, i = pl.program_id(0), pl.program_id(1)
        x_, y_, c, k = _position()
        chips = _other_chips(x_, y_)

        def sent(a, r):
            cx, cy = chips[r]
            return pltpu.make_async_remote_copy(
                src_ref=outs[a].at[k, c], dst_ref=outs[a].at[k, c], send_sem=send.at[3 * a + r],
                recv_sem=recv.at[3 * a + r], device_id=(cx, cy, c), device_id_type=MESH)

        def landed(a, r):
            cx, cy = chips[r]
            return pltpu.make_async_remote_copy(
                src_ref=outs[a].at[2 * cx + cy, c], dst_ref=outs[a].at[2 * cx + cy, c], send_sem=send.at[3 * a + r],
                recv_sem=recv.at[3 * a + r], device_id=(cx, cy, c), device_id_type=MESH)

        def passed(a, r, h):
            cx, cy = chips[r]
            return pltpu.make_async_remote_copy(
                src_ref=outs[a].at[2 * cx + cy, h], dst_ref=outs[a].at[2 * cx + cy, h], send_sem=fsend.at[3 * a + r],
                recv_sem=frecv.at[3 * a + r], device_id=(x_, y_, 1 - c), device_id_type=MESH)

        def load(block):
            cps = [pltpu.make_async_copy(outs[0].at[block, h], w_ref.at[pl.ds(h * half, half)], lsem.at[h])
                   for h in range(2)]
            for cp in cps:
                cp.start()
            for cp in cps:
                cp.wait()

        @pl.when((jj == 0) & (i == 0))
        def _():
            for a in range(n):
                for r in range(3):
                    sent(a, r).start()
            load(k)

        for r in range(3):
            @pl.when((jj == r + 1) & (i == 0))
            def _(r=r):
                landed(0, r).wait_recv()
                passed(0, r, c).start()
                passed(0, r, 1 - c).wait_recv()
                load(2 * chips[r][0] + chips[r][1])

        rows = pl.ds(pl.multiple_of(i * tm, tm), tm)

        @pl.when(jj == 0)
        def _():
            xb_ref[rows, :] = x_ref[...].astype(BF16)

        p_ref[...] = _dot(xb_ref[rows, :], w_ref[...]) + b_ref[...]

        @pl.when((jj == N_CHIPS - 1) & (i == steps - 1))
        def _():
            for a in range(1, n):
                for r in range(3):
                    landed(a, r).wait_recv()
                    passed(a, r, c).start()
            for a in range(1, n):
                for r in range(3):
                    passed(a, r, 1 - c).wait_recv()
            for a in range(n):
                for r in range(3):
                    sent(a, r).wait_send()
                    passed(a, r, c).wait_send()

    any_spec = pl.BlockSpec(memory_space=pl.ANY)
    return pl.pallas_call(
        body, name="proj_gather",
        grid_spec=pltpu.PrefetchScalarGridSpec(
            num_scalar_prefetch=1, grid=(N_CHIPS, steps),
            in_specs=[pl.BlockSpec((tm, D), lambda jj, i, o: (jnp.where(jj == 0, i, steps - 1), 0)),
                      pl.BlockSpec((None, 1, W_BLOCK), lambda jj, i, o: (o[jj], 0, 0))] + [any_spec] * n,
            out_specs=[pl.BlockSpec((tm, W_BLOCK), lambda jj, i, o: (i, o[jj]))] + [any_spec] * n,
            scratch_shapes=[pltpu.VMEM((t, D), BF16), pltpu.VMEM((D, W_BLOCK), BF16), pltpu.SemaphoreType.DMA((2,)),
                            pltpu.SemaphoreType.DMA((3 * n,)), pltpu.SemaphoreType.DMA((3 * n,)),
                            pltpu.SemaphoreType.DMA((3 * n,)), pltpu.SemaphoreType.DMA((3 * n,))]),
        out_shape=[jax.ShapeDtypeStruct((t, D_IN), F32)] + [jax.ShapeDtypeStruct(b.shape, b.dtype) for b in bufs],
        input_output_aliases={3 + a: 1 + a for a in range(n)},
        compiler_params=_params("arbitrary", "arbitrary"),
    )(order, x, b4, *bufs)


def _exchange_halves(grads):
    n = len(grads)

    def body(*refs):
        ins, outs = refs[:n], refs[n:2 * n]
        send, recv = refs[2 * n:]
        x, y, c, _ = _position()
        cps = []
        for a in range(n):
            cp = pltpu.make_async_remote_copy(
                src_ref=ins[a].at[1 - c], dst_ref=outs[a], send_sem=send.at[a], recv_sem=recv.at[a],
                device_id=(x, y, 1 - c), device_id_type=MESH)
            cp.start()
            cps.append(cp)
        for cp in cps:
            cp.wait_recv()
        for cp in cps:
            cp.wait_send()

    return pl.pallas_call(
        body, name="rs_exchange_halves",
        in_specs=_any_specs(n), out_specs=_any_specs(n),
        out_shape=[jax.ShapeDtypeStruct(g.shape[1:], g.dtype) for g in grads],
        scratch_shapes=[pltpu.SemaphoreType.DMA((n,)), pltpu.SemaphoreType.DMA((n,))],
    )(*grads)


def _share_results(bufs, small):
    n = len(bufs)

    def body(*refs):
        outs, small_out = refs[n + 1:2 * n + 1], refs[2 * n + 1]
        send, recv, ssend, srecv = refs[2 * n + 2:]
        x, y, c, k = _position()
        cps = []
        for a in range(n):
            cp = pltpu.make_async_remote_copy(
                src_ref=outs[a].at[c], dst_ref=outs[a].at[c], send_sem=send.at[a], recv_sem=recv.at[a],
                device_id=(x, y, 1 - c), device_id_type=MESH)
            cp.start()
            cps.append(cp)
        waits = []
        for p in range(1, 8):
            px, py, pc = x ^ (p >> 2), y ^ ((p >> 1) & 1), c ^ (p & 1)
            cp = pltpu.make_async_remote_copy(
                src_ref=small_out.at[k, c], dst_ref=small_out.at[k, c], send_sem=ssend.at[p - 1],
                recv_sem=srecv.at[p - 1], device_id=(px, py, pc), device_id_type=MESH)
            cp.start()
            cps.append(cp)
            waits.append(pltpu.make_async_remote_copy(
                src_ref=small_out.at[2 * px + py, pc], dst_ref=small_out.at[2 * px + py, pc], send_sem=ssend.at[p - 1],
                recv_sem=srecv.at[p - 1], device_id=(px, py, pc), device_id_type=MESH))
        for a in range(n):
            pltpu.make_async_remote_copy(
                src_ref=outs[a].at[1 - c], dst_ref=outs[a].at[1 - c], send_sem=send.at[a], recv_sem=recv.at[a],
                device_id=(x, y, 1 - c), device_id_type=MESH).wait_recv()
        for w in waits:
            w.wait_recv()
        for cp in cps:
            cp.wait_send()

    return pl.pallas_call(
        body, name="rs_share_results",
        in_specs=_any_specs(n + 1), out_specs=_any_specs(n + 1),
        out_shape=[jax.ShapeDtypeStruct(b.shape, b.dtype) for b in bufs + [small]],
        scratch_shapes=[pltpu.SemaphoreType.DMA((n,)), pltpu.SemaphoreType.DMA((n,)),
                        pltpu.SemaphoreType.DMA((7,)), pltpu.SemaphoreType.DMA((7,))],
        input_output_aliases={a: a for a in range(n + 1)},
    )(*bufs, small)


def _row_tile(r, c):
    t = max(8, min(r, (1 << 18) // c))
    while r % t:
        t //= 2
    return t


def _add_halves(g, b1, wire_dtype):
    _, _, r, c = g.shape
    t = _row_tile(r, c)

    def body(c_ref, g_ref, b_ref, q_ref, qw_ref):
        q = g_ref[...] + b_ref[...]
        q_ref[...] = q
        qw_ref[...] = q.astype(wire_dtype)

    core = _core_index()
    return pl.pallas_call(
        body, name="rs_add_halves",
        grid_spec=pltpu.PrefetchScalarGridSpec(
            num_scalar_prefetch=1, grid=(N_CHIPS, r // t),
            in_specs=[pl.BlockSpec((None, None, t, c), lambda j, i, cr: (cr[0], j, i, 0)),
                      pl.BlockSpec((None, t, c), lambda j, i, cr: (j, i, 0))],
            out_specs=[pl.BlockSpec((None, t, c), lambda j, i, cr: (j, i, 0)),
                       pl.BlockSpec((None, t, c), lambda j, i, cr: (j, i, 0))]),
        out_shape=[jax.ShapeDtypeStruct((N_CHIPS, r, c), F32), jax.ShapeDtypeStruct((N_CHIPS, r, c), wire_dtype)],
        compiler_params=_params("parallel", "parallel"),
    )(core, g, b1)


def _add_chips(q, b2, per_device):
    r, c = q.shape[-2:]
    t = _row_tile(r, c)

    def body(kc_ref, q_ref, b_ref, f_ref):
        f_ref[...] = ((q_ref[...] + b_ref[0].astype(F32)) + b_ref[1].astype(F32)) + b_ref[2].astype(F32)

    if per_device:
        out_spec = pl.BlockSpec((None, None, t, c), lambda i, kc: (kc[0], kc[1], i, 0))
        out_shape = jax.ShapeDtypeStruct((N_CHIPS, 2, r, c), F32)
    else:
        out_spec = pl.BlockSpec((None, t, c), lambda i, kc: (kc[1], i, 0))
        out_shape = jax.ShapeDtypeStruct((2, r, c), F32)
    return pl.pallas_call(
        body, name="rs_add_chips",
        grid_spec=pltpu.PrefetchScalarGridSpec(
            num_scalar_prefetch=1, grid=(r // t,),
            in_specs=[pl.BlockSpec((None, t, c), lambda i, kc: (kc[0], i, 0)) if q.ndim == 3
                      else pl.BlockSpec((t, c), lambda i, kc: (i, 0)),
                      pl.BlockSpec((3, t, c), lambda i, kc: (0, i, 0))],
            out_specs=out_spec),
        out_shape=out_shape,
        compiler_params=_params("parallel"),
    )(jnp.concatenate([_chip_index(), _core_index()]), q, b2)


def _adamw_math(w, g, m, v):
    m = ADAM_B1 * m + (1.0 - ADAM_B1) * g
    v = ADAM_B2 * v + (1.0 - ADAM_B2) * (g * g)
    m_hat = m / (1.0 - ADAM_B1 ** ADAM_STEP)
    v_hat = v / (1.0 - ADAM_B2 ** ADAM_STEP)
    delta = -ADAM_LR * (m_hat / (jnp.sqrt(v_hat) + ADAM_EPS) + ADAM_WD * w)
    return delta, m, v


def _adamw(w, g, m, v):
    r, c = w.shape
    t = _row_tile(r, c) if r % 8 == 0 else r

    def body(w_ref, g_ref, m_ref, v_ref, d_ref, nm_ref, nv_ref):
        d_ref[...], nm_ref[...], nv_ref[...] = _adamw_math(w_ref[...], g_ref[...], m_ref[...], v_ref[...])

    spec = pl.BlockSpec((t, c), lambda i: (i, 0))
    return pl.pallas_call(
        body, name="adamw", grid=(r // t,), in_specs=[spec] * 4, out_specs=[spec] * 3,
        out_shape=[jax.ShapeDtypeStruct((r, c), F32)] * 3, compiler_params=_params("parallel"),
    )(w, g, m, v)


def _adamw_small(w, parts, m, v, loss_row):
    r, c = w.shape

    def body(w_ref, p_ref, m_ref, v_ref, g_ref, d_ref, nm_ref, nv_ref, loss_ref):
        rows = r // 8
        for k in range(N_CHIPS):
            for core in range(2):
                g_ref[(core * N_CHIPS + k) * rows:(core * N_CHIPS + k + 1) * rows, :] = p_ref[2 * k + core]
        g = g_ref[...]
        d_ref[...], nm_ref[...], nv_ref[...] = _adamw_math(w_ref[...], g, m_ref[...], v_ref[...])
        lanes = g_ref[loss_row:loss_row + 8, :]
        loss_ref[...] = jnp.broadcast_to(jnp.sum(jnp.sum(lanes, axis=1, keepdims=True), axis=0, keepdims=True), (8, c))

    return pl.pallas_call(
        body, name="adamw_small",
        in_specs=[_full((r, c)), _full((8, r // 8, c)), _full((r, c)), _full((r, c))],
        out_specs=[_full((r, c))] * 4 + [_full((8, c))],
        out_shape=[jax.ShapeDtypeStruct((r, c), F32)] * 4 + [jax.ShapeDtypeStruct((8, c), F32)],
        compiler_params=_params(),
    )(w, parts, m, v)


SUBLANES = 8
SHIFT_ROWS = HALO - SUBLANES


def _shifted_copies(src_ref, sh_ref, cs, tm):
    for p in range(1, SUBLANES):
        sh_ref[p - 1] = src_ref[pl.ds(p, tm + SHIFT_ROWS), cs]


def _tap(src_ref, sh_ref, cs, offset, start, rows):
    p, q = offset % SUBLANES, offset // SUBLANES
    if p == 0:
        return src_ref[pl.ds(start + SUBLANES * q, rows), cs]
    return sh_ref[p - 1, pl.ds(start + SUBLANES * q, rows), :]


def _conv_taps(src_ref, sh_ref, w_ref, first_offset, step, bias, dst_ref, tm):
    rows = 64
    for g in range(N_GROUPS):
        cs = slice(g * GROUP_W, (g + 1) * GROUP_W)
        _shifted_copies(src_ref, sh_ref, cs, tm)
        for rb in range(tm // rows):
            acc = jnp.zeros((rows, GROUP_W), F32) + (bias[:, cs] if bias is not None else 0.0)
            for k in range(CONV_K):
                acc = acc + w_ref[k:k + 1, cs] * _tap(src_ref, sh_ref, cs, first_offset + step * k, rb * rows, rows)
            dst_ref[rb * rows:(rb + 1) * rows, cs] = acc


def _conv_weight_grad(d_ref, src_ref, sh_ref, first_offset, acc_ref, tm):
    for g in range(N_GROUPS):
        cs = slice(g * GROUP_W, (g + 1) * GROUP_W)
        _shifted_copies(src_ref, sh_ref, cs, tm)
        for k in range(CONV_K):
            prod = d_ref[:, cs] * _tap(src_ref, sh_ref, cs, first_offset + k, 0, tm)
            acc_ref[SUBLANES * k:SUBLANES * (k + 1), cs] += jnp.sum(prod.reshape(tm // SUBLANES, SUBLANES, GROUP_W), axis=0)


def _spatial_mix(w_ref, v_bf, tm):
    rows = []
    for q in range(tm // CHUNK):
        cols = [_dot(w_ref[h], v_bf[q * CHUNK:(q + 1) * CHUNK, h * GROUP_W:(h + 1) * GROUP_W])
                for h in range(N_GROUPS)]
        rows.append(jnp.concatenate(cols, axis=1))
    return jnp.concatenate(rows, axis=0)


def _group_norm_fwd(h1, gn_g, gn_b):
    xhat, rstd = [], []
    for g in range(N_GROUPS):
        xh, rs = _norm_stats(h1[:, g * GROUP_W:(g + 1) * GROUP_W])
        xhat.append(xh)
        rstd.append(rs)
    xhat = jnp.concatenate(xhat, axis=1)
    return xhat * gn_g + gn_b, xhat, rstd


def _forward_tiles(p, x, tgt, wpa, wpb, wo, convw, vecs, ws, bsp, tiles_per_seq):
    t = x.shape[0]
    tm = TOKEN_TILE
    hb = tm // HALO

    def body(p_ref, ph_ref, x_ref, t_ref, wpa_ref, wpb_ref, wo_ref, cw_ref, vec_ref, ws_ref, bsp_ref,
             h1_ref, ya_ref, yb_ref, h3_ref, s_ref, mx_ref, dr_ref, drb_ref, xt_ref, acc_ref, he_ref, sh_ref):
        i = pl.program_id(0)
        xt_ref[...] = x_ref[...].T.astype(BF16)
        conv_b, gn_g, gn_b, lnv_g, lnv_b, b_o, lno_g, lno_b = [vec_ref[j:j + 1, :] for j in range(8)]

        keep = jnp.where(i % tiles_per_seq == 0, 0.0, 1.0)
        he_ref[0:HALO, :] = ph_ref[:, 0:D] * _sigmoid(ph_ref[:, D:2 * D]) * keep
        he_ref[HALO:, :] = p_ref[:, 0:D] * _sigmoid(p_ref[:, D:2 * D])
        _conv_taps(he_ref, sh_ref, cw_ref, HALO - (CONV_K - 1), 1, conv_b, h1_ref, tm)
        h2, _, _ = _group_norm_fwd(h1_ref[...], gn_g, gn_b)
        a_gate = p_ref[:, 2 * D:3 * D]
        h3 = ((h2 * _sigmoid(h2)) * (a_gate * _sigmoid(a_gate))).astype(BF16)
        h3_ref[...] = h3
        ya = _dot(h3, wpa_ref[...])
        ya_ref[...] = ya

        u = _gelu(p_ref[:, 3 * D:4 * D])
        vhat, _ = _norm_stats(_gelu(p_ref[:, 4 * D:5 * D]))
        v1 = (vhat * lnv_g + lnv_b).astype(BF16)
        b_gate = p_ref[:, 5 * D:6 * D]
        vmix = _spatial_mix(ws_ref, v1, tm) + jnp.concatenate([bsp_ref[...]] * (tm // CHUNK), axis=0)
        s = (u * vmix * (b_gate * _sigmoid(b_gate))).astype(BF16)
        s_ref[...] = s
        yb = _dot(s, wpb_ref[...])
        yb_ref[...] = yb

        mixed = (_sigmoid(p_ref[:, 6 * D:7 * D]) * ya + _sigmoid(p_ref[:, 7 * D:8 * D]) * yb).astype(BF16)
        mx_ref[...] = mixed
        r = ALPHA * x_ref[...] + (_dot(mixed, wo_ref[...]) + b_o)
        xhat, rstd = _norm_stats(r)
        err = (xhat * lno_g + lno_b) - t_ref[...]
        dout = err * (1.0 / D)
        dr = _norm_bwd(dout * lno_g, xhat, rstd)
        dr_ref[...] = dr
        drb_ref[...] = dr.astype(BF16)

        @pl.when(i == 0)
        def _():
            acc_ref[...] = jnp.zeros_like(acc_ref)

        acc_ref[0:1, :] += _colsum(dout * xhat)
        acc_ref[1:2, :] += _colsum(dout)
        acc_ref[2:3, :] += _colsum(dr)
        acc_ref[3:4, :] += _colsum(err * err) * (0.5 / D)

    tile = lambda w: pl.BlockSpec((tm, w), lambda i: (i, 0))
    f32_out = jax.ShapeDtypeStruct((t, D), F32)
    bf_out = jax.ShapeDtypeStruct((t, D), BF16)
    return pl.pallas_call(
        body, name="forward_tiles", grid=(t // tm,),
        in_specs=[tile(D_IN),
                  pl.BlockSpec((HALO, 2 * D), lambda i: (jnp.maximum(i * hb - 1, 0), 0)),
                  tile(D), tile(D), _resident((D, D)), _resident((D, D)), _resident((D, D)), _full((HALO, D)), _full((8, D)),
                  _full((N_GROUPS, CHUNK, CHUNK)), _full((CHUNK, D))],
        out_specs=[tile(D)] * 8 + [pl.BlockSpec((D, tm), lambda i: (0, i)), _full((8, D))],
        out_shape=[f32_out, f32_out, f32_out, bf_out, bf_out, bf_out, f32_out, bf_out,
                   jax.ShapeDtypeStruct((D, t), BF16), jax.ShapeDtypeStruct((8, D), F32)],
        scratch_shapes=[pltpu.VMEM((tm + HALO, D), F32), pltpu.VMEM((SUBLANES - 1, tm + SHIFT_ROWS, GROUP_W), F32)],
        compiler_params=_params("arbitrary"),
    )(p, p, x, tgt, wpa, wpb, wo, convw, vecs, ws, bsp)


def _backward_tiles(p, h1, ya, yb, drb, wpa, wpb, wo, vecs, ws, wst, bsp):
    t = h1.shape[0]
    tm = TOKEN_TILE

    def body(p_ref, h1_ref, ya_ref, yb_ref, drb_ref, wpa_ref, wpb_ref, wo_ref, vec_ref, ws_ref, wst_ref, bsp_ref,
             dh1_ref, dp_ref, dya_ref, dyb_ref, acc_ref, dbin_ref, dws_ref, dbsp_ref):
        i = pl.program_id(0)
        _, gn_g, gn_b, lnv_g, lnv_b = [vec_ref[j:j + 1, :] for j in range(5)]

        @pl.when(i == 0)
        def _():
            acc_ref[...] = jnp.zeros_like(acc_ref)
            dbin_ref[...] = jnp.zeros_like(dbin_ref)
            dws_ref[...] = jnp.zeros_like(dws_ref)
            dbsp_ref[...] = jnp.zeros_like(dbsp_ref)

        def emit(block, val):
            dbin_ref[0:1, block * D:(block + 1) * D] += _colsum(val)
            dp_ref[:, block * D:(block + 1) * D] = val.astype(BF16)

        dp_ref[:, 0:2 * D] = jnp.zeros((tm, 2 * D), BF16)
        dmixed = _dot_nt(drb_ref[...], wo_ref[...])
        ga = _sigmoid(p_ref[:, 6 * D:7 * D])
        gb = _sigmoid(p_ref[:, 7 * D:8 * D])
        dya = (dmixed * ga).astype(BF16)
        dyb = (dmixed * gb).astype(BF16)
        dya_ref[...] = dya
        dyb_ref[...] = dyb
        emit(6, dmixed * ya_ref[...] * (ga * (1.0 - ga)))
        emit(7, dmixed * yb_ref[...] * (gb * (1.0 - gb)))

        dh3 = _dot_nt(dya, wpa_ref[...])
        h2, xhat, rstd = _group_norm_fwd(h1_ref[...], gn_g, gn_b)
        sg = _sigmoid(h2)
        a_gate = p_ref[:, 2 * D:3 * D]
        sa = _sigmoid(a_gate)
        dh2 = dh3 * (a_gate * sa) * (sg * (1.0 + h2 * (1.0 - sg)))
        emit(2, dh3 * (h2 * sg) * (sa * (1.0 + a_gate * (1.0 - sa))))
        acc_ref[0:1, :] += _colsum(dh2 * xhat)
        acc_ref[1:2, :] += _colsum(dh2)
        dxhat = dh2 * gn_g
        for g in range(N_GROUPS):
            cs = slice(g * GROUP_W, (g + 1) * GROUP_W)
            dh1_ref[:, cs] = _norm_bwd(dxhat[:, cs], xhat[:, cs], rstd[g])

        ds = _dot_nt(dyb, wpb_ref[...])
        u_pre = p_ref[:, 3 * D:4 * D]
        u, du_dpre = _gelu_and_grad(u_pre)
        v0, dv_dpre = _gelu_and_grad(p_ref[:, 4 * D:5 * D])
        vhat, vrstd = _norm_stats(v0)
        v1 = (vhat * lnv_g + lnv_b).astype(BF16)
        vmix = _spatial_mix(ws_ref, v1, tm) + jnp.concatenate([bsp_ref[...]] * (tm // CHUNK), axis=0)
        b_gate = p_ref[:, 5 * D:6 * D]
        sb = _sigmoid(b_gate)
        silu_b = b_gate * sb
        emit(3, ds * vmix * silu_b * du_dpre)
        emit(5, ds * u * vmix * (sb * (1.0 + b_gate * (1.0 - sb))))
        dvmix = ds * u * silu_b
        dvmix_bf = dvmix.astype(BF16)
        for q in range(tm // CHUNK):
            dbsp_ref[...] += dvmix[q * CHUNK:(q + 1) * CHUNK, :]
            for h in range(N_GROUPS):
                blk = (slice(q * CHUNK, (q + 1) * CHUNK), slice(h * GROUP_W, (h + 1) * GROUP_W))
                dws_ref[h] += _dot_nt(dvmix_bf[blk], v1[blk])
        dv1 = _spatial_mix(wst_ref, dvmix_bf, tm)
        acc_ref[2:3, :] += _colsum(dv1 * vhat)
        acc_ref[3:4, :] += _colsum(dv1)
        emit(4, _norm_bwd(dv1 * lnv_g, vhat, vrstd) * dv_dpre)

    tile = lambda w: pl.BlockSpec((tm, w), lambda i: (i, 0))
    return pl.pallas_call(
        body, name="backward_tiles", grid=(t // tm,),
        in_specs=[tile(D_IN), tile(D), tile(D), tile(D), tile(D), _resident((D, D)), _resident((D, D)), _resident((D, D)),
                  _full((8, D)), _full((N_GROUPS, CHUNK, CHUNK)), _full((N_GROUPS, CHUNK, CHUNK)), _full((CHUNK, D))],
        out_specs=[tile(D), tile(D_IN), tile(D), tile(D), _full((8, D)), _full((8, D_IN)),
                   _full((N_GROUPS, CHUNK, CHUNK)), _full((CHUNK, D))],
        out_shape=[jax.ShapeDtypeStruct((t, D), F32), jax.ShapeDtypeStruct((t, D_IN), BF16),
                   jax.ShapeDtypeStruct((t, D), BF16), jax.ShapeDtypeStruct((t, D), BF16),
                   jax.ShapeDtypeStruct((8, D), F32), jax.ShapeDtypeStruct((8, D_IN), F32),
                   jax.ShapeDtypeStruct((N_GROUPS, CHUNK, CHUNK), F32), jax.ShapeDtypeStruct((CHUNK, D), F32)],
        compiler_params=_params("arbitrary"),
    )(p, h1, ya, yb, drb, wpa, wpb, wo, vecs, ws, wst, bsp)


def _conv_backward(dh1, p, dp, convw, tiles_per_seq):
    t = dh1.shape[0]
    tm = TOKEN_TILE
    hb = tm // HALO
    last = t // HALO - 1

    def body(dh1_ref, dnext_ref, p_ref, ph_ref, cw_ref, dp_in_ref, dp_ref, dcw_ref, dbin_ref,
             de_ref, he_ref, dh0_ref, sh_ref):
        del dp_in_ref
        i = pl.program_id(0)

        @pl.when(i == 0)
        def _():
            dcw_ref[...] = jnp.zeros_like(dcw_ref)
            dbin_ref[...] = jnp.zeros_like(dbin_ref)

        keep_next = jnp.where(i % tiles_per_seq == tiles_per_seq - 1, 0.0, 1.0)
        de_ref[0:tm, :] = dh1_ref[...]
        de_ref[tm:, :] = dnext_ref[...] * keep_next
        _conv_taps(de_ref, sh_ref, cw_ref, CONV_K - 1, -1, None, dh0_ref, tm)

        keep_prev = jnp.where(i % tiles_per_seq == 0, 0.0, 1.0)
        sg = _sigmoid(p_ref[:, D:2 * D])
        val = p_ref[:, 0:D]
        he_ref[0:HALO, :] = ph_ref[:, 0:D] * _sigmoid(ph_ref[:, D:2 * D]) * keep_prev
        he_ref[HALO:, :] = val * sg
        _conv_weight_grad(dh1_ref, he_ref, sh_ref, HALO - (CONV_K - 1), dcw_ref, tm)
        dcw_ref[SUBLANES * CONV_K:, :] += jnp.sum(dh1_ref[...].reshape(tm // SUBLANES, SUBLANES, D), axis=0)

        dh0 = dh0_ref[...]
        dval = dh0 * sg
        dglu = dh0 * val * (sg * (1.0 - sg))
        dbin_ref[0:1, 0:D] += _colsum(dval)
        dbin_ref[0:1, D:2 * D] += _colsum(dglu)
        dp_ref[:, 0:D] = dval.astype(BF16)
        dp_ref[:, D:2 * D] = dglu.astype(BF16)

    return pl.pallas_call(
        body, name="conv_backward", grid=(t // tm,),
        in_specs=[pl.BlockSpec((tm, D), lambda i: (i, 0)),
                  pl.BlockSpec((HALO, D), lambda i: (jnp.minimum((i + 1) * hb, last), 0)),
                  pl.BlockSpec((tm, 2 * D), lambda i: (i, 0)),
                  pl.BlockSpec((HALO, 2 * D), lambda i: (jnp.maximum(i * hb - 1, 0), 0)),
                  _full((HALO, D)), pl.BlockSpec(memory_space=pl.ANY)],
        out_specs=[pl.BlockSpec((tm, 2 * D), lambda i: (i, 0)), _full((SUBLANES * HALO, D)), _full((8, 2 * D))],
        out_shape=[jax.ShapeDtypeStruct(dp.shape, BF16), jax.ShapeDtypeStruct((SUBLANES * HALO, D), F32),
                   jax.ShapeDtypeStruct((8, 2 * D), F32)],
        scratch_shapes=[pltpu.VMEM((tm + HALO, D), F32), pltpu.VMEM((tm + HALO, D), F32), pltpu.VMEM((tm, D), F32),
                        pltpu.VMEM((SUBLANES - 1, tm + SHIFT_ROWS, GROUP_W), F32)],
        input_output_aliases={5: 0},
        compiler_params=_params("arbitrary"),
    )(dh1, dh1, p, p, convw, dp)


def _grad_in_and_x(xt, dp, w4, dr, chip_parts):
    t = dr.shape[0]
    tm = TOKEN_TILE
    half, tn = D // 2, 512
    nb = W_BLOCK // tn
    n_w, n_x = 2 * N_CHIPS * nb, t // tm
    ns = len(chip_parts)
    xi, yi, ci = lax.axis_index("x"), lax.axis_index("y"), lax.axis_index("c")
    others = [2 * (1 - xi) + yi, 2 * xi + (1 - yi), 2 * (1 - xi) + (1 - yi)]
    blocks = others + others + [2 * xi + yi] * 2
    halves = [1 - ci] * 3 + [ci] * 3 + [1 - ci, ci]
    table = jnp.stack([jnp.stack([b * nb + n for b in blocks for n in range(nb)]),
                       jnp.stack([h for h in halves for _ in range(nb)])]).astype(jnp.int32)

    def body(tab_ref, xt_ref, dpc_ref, dpr_ref, w_ref, dr_ref, *refs):
        parts = refs[:ns]
        dx_ref, qk_ref, b2_ref, b1_ref = refs[ns:ns + 4]
        lands = refs[ns + 4:2 * ns + 4]
        g_ref, st_ref, sb_ref, tmp_ref, d2d_send, d2d_recv, ici_send, ici_recv, own_sem, tmp_sem, p_send, p_recv = refs[2 * ns + 4:]
        s = pl.program_id(0)
        x_, y_, c, _ = _position()
        chips = _other_chips(x_, y_)
        n = s % nb
        grp = s // nb
        cols = pl.ds(pl.multiple_of(n * tn, tn), tn)

        def part(a, r):
            cx, cy = chips[r]
            return pltpu.make_async_remote_copy(
                src_ref=parts[a].at[2 * cx + cy], dst_ref=lands[a].at[r], send_sem=p_send.at[3 * a + r],
                recv_sem=p_recv.at[3 * a + r], device_id=(cx, cy, c), device_id_type=MESH)

        def to_sibling(slot, land):
            return pltpu.make_async_remote_copy(
                src_ref=st_ref.at[slot], dst_ref=b1_ref.at[land, :, cols], send_sem=d2d_send.at[slot],
                recv_sem=d2d_recv.at[land * nb + n], device_id=(x_, y_, 1 - c), device_id_type=MESH)

        def to_chip(slot, r):
            cx, cy = chips[r]
            return pltpu.make_async_remote_copy(
                src_ref=sb_ref.at[slot], dst_ref=b2_ref.at[r, :, cols], send_sem=ici_send.at[slot],
                recv_sem=ici_recv.at[r], device_id=(cx, cy, c), device_id_type=MESH)

        def from_chip(r):
            cx, cy = chips[r]
            return pltpu.make_async_remote_copy(
                src_ref=b2_ref.at[r], dst_ref=b2_ref.at[r], send_sem=ici_send.at[0],
                recv_sem=ici_recv.at[r], device_id=(cx, cy, c), device_id_type=MESH)

        def to_result(slot):
            return pltpu.make_async_copy(st_ref.at[slot], qk_ref.at[:, cols], own_sem.at[slot])

        def with_sibling_piece(land):
            to_sibling(0, land).wait_recv()
            load = pltpu.make_async_copy(b1_ref.at[land, :, cols], tmp_ref, tmp_sem)
            load.start()
            load.wait()
            return g_ref[...] + tmp_ref[...]

        @pl.when(s == 0)
        def _():
            for a in range(ns):
                for r in range(3):
                    part(a, r).start()

        @pl.when(s < n_w)
        def _():
            g_ref[...] = _dot(xt_ref[tab_ref[1, s]], dpc_ref[...])

        for g in range(2 * N_CHIPS):
            @pl.when(grp == g)
            def _(g=g):
                if g in (0, 1, 2, 6):
                    use = s if g < 3 else 3 * nb + n
                    slot = use % 2

                    @pl.when(use >= 2)
                    def _():
                        to_sibling(slot, 0).wait_send()

                    st_ref[slot] = g_ref[...]
                    to_sibling(slot, min(g, 3)).start()
                elif g in (3, 4, 5):
                    use = s - 3 * nb
                    slot = use % 2
                    piece = with_sibling_piece(g - 3)

                    @pl.when(use >= 2)
                    def _():
                        to_chip(slot, g - 3).wait_send()

                    sb_ref[slot] = piece.astype(BF16)
                    to_chip(slot, g - 3).start()
                else:
                    slot = n % 2
                    piece = with_sibling_piece(3)

                    @pl.when(n < 2)
                    def _():
                        to_sibling(slot, 0).wait_send()

                    @pl.when(n >= 2)
                    def _():
                        to_result(slot).wait()

                    st_ref[slot] = piece
                    to_result(slot).start()

        @pl.when(s >= n_w)
        def _():
            acc = ALPHA * dr_ref[...]
            for j in range(N_CHIPS):
                acc = acc + _dot_nt(dpr_ref[:, j * W_BLOCK:(j + 1) * W_BLOCK], w_ref[j])
            dx_ref[...] = acc

        @pl.when(s == n_w + n_x - 1)
        def _():
            for slot in range(2):
                to_chip(slot, 0).wait_send()
                to_result(slot).wait()
            for r in range(3):
                from_chip(r).wait_recv()
            for a in range(ns):
                for r in range(3):
                    part(a, r).wait_recv()
                    part(a, r).wait_send()

    any_spec = pl.BlockSpec(memory_space=pl.ANY)
    tile = lambda s, tab: (jnp.maximum(s - n_w, 0), 0)
    return pl.pallas_call(
        body, name="grad_in_and_x",
        grid_spec=pltpu.PrefetchScalarGridSpec(
            num_scalar_prefetch=1, grid=(n_w + n_x,),
            in_specs=[pl.BlockSpec((2, half, t), lambda s, tab: (0, 0, 0), pipeline_mode=pl.Buffered(1)),
                      pl.BlockSpec((t, tn), lambda s, tab: (0, tab[0, jnp.minimum(s, n_w - 1)])),
                      pl.BlockSpec((tm, D_IN), tile),
                      pl.BlockSpec((N_CHIPS, D, W_BLOCK), lambda s, tab: (0, 0, 0), pipeline_mode=pl.Buffered(1)),
                      pl.BlockSpec((tm, D), tile)] + [any_spec] * ns,
            out_specs=[pl.BlockSpec((tm, D), tile)] + [any_spec] * (3 + ns),
            scratch_shapes=[pltpu.VMEM((half, tn), F32), pltpu.VMEM((2, half, tn), F32), pltpu.VMEM((2, half, tn), BF16),
                            pltpu.VMEM((half, tn), F32),
                            pltpu.SemaphoreType.DMA((2,)), pltpu.SemaphoreType.DMA((N_CHIPS * nb,)),
                            pltpu.SemaphoreType.DMA((2,)), pltpu.SemaphoreType.DMA((3,)),
                            pltpu.SemaphoreType.DMA((2,)), pltpu.SemaphoreType.DMA,
                            pltpu.SemaphoreType.DMA((3 * ns,)), pltpu.SemaphoreType.DMA((3 * ns,))]),
        out_shape=[jax.ShapeDtypeStruct((t, D), F32), jax.ShapeDtypeStruct((half, W_BLOCK), F32),
                   jax.ShapeDtypeStruct((3, half, W_BLOCK), BF16), jax.ShapeDtypeStruct((N_CHIPS, half, W_BLOCK), F32)]
        + [jax.ShapeDtypeStruct((3,) + q.shape[1:], q.dtype) for q in chip_parts],
        compiler_params=_params("arbitrary"),
    )(table, xt, dp, dp, w4, dr, *chip_parts)


def _grad_w_square(a, b):
    t = a.shape[0]
    tk = 512
    rows = D // 8

    def body(a_ref, b_ref, o_ref, acc_ref):
        s = pl.program_id(0)

        @pl.when(s == 0)
        def _():
            acc_ref[...] = jnp.zeros_like(acc_ref)

        acc_ref[...] += _dot_tn(a_ref[...], b_ref[...])

        @pl.when(s == pl.num_programs(0) - 1)
        def _():
            for j in range(N_CHIPS):
                for h in range(2):
                    o_ref[h, j] = acc_ref[(2 * j + h) * rows:(2 * j + h + 1) * rows, :]

    return pl.pallas_call(
        body, name="grad_w_square", grid=(t // tk,),
        in_specs=[pl.BlockSpec((tk, D), lambda s: (s, 0)), pl.BlockSpec((tk, D), lambda s: (s, 0))],
        out_specs=_full((2, N_CHIPS, rows, D)),
        out_shape=jax.ShapeDtypeStruct((2, N_CHIPS, rows, D), F32),
        scratch_shapes=[pltpu.VMEM((D, D), F32)],
        compiler_params=_params("arbitrary"),
    )(a, b)


SMALL_ROWS = 1216


def _pack_small(b_in, vec9, w_spatial, b_spatial, loss_lanes):
    parts = [b_in.reshape(64, 128)] + [v.reshape(8, 128) for v in vec9]
    parts += [w_spatial.reshape(N_GROUPS * CHUNK, CHUNK), b_spatial.reshape(8, 128), loss_lanes.reshape(8, 128)]
    used = sum(p.shape[0] for p in parts)
    return jnp.concatenate(parts + [jnp.zeros((SMALL_ROWS - used, 128), F32)], axis=0)


LOSS_ROW = 64 + 9 * 8 + N_GROUPS * CHUNK + 8


def _unpack_small(a):
    out, row = [], 0
    for rows, shape in [(64, (D_IN,))] + [(8, (D,))] * 9 + [(N_GROUPS * CHUNK, (N_GROUPS, CHUNK, CHUNK)), (8, (N_GROUPS, CHUNK))]:
        out.append(a[row:row + rows].reshape(shape))
        row += rows
    return out


def kernel(x, w_in, b_in, conv_w, conv_b, gn_g, gn_b, ln_v_g, ln_v_b, w_spatial, b_spatial, w_pa, w_pb, w_o, b_o, ln_out_g, ln_out_b, loss_target, m_w_in, m_b_in, m_conv_w, m_conv_b, m_gn_g, m_gn_b, m_ln_v_g, m_ln_v_b, m_w_spatial, m_b_spatial, m_w_pa, m_w_pb, m_w_o, m_b_o, m_ln_out_g, m_ln_out_b, v_w_in, v_b_in, v_conv_w, v_conv_b, v_gn_g, v_gn_b, v_ln_v_g, v_ln_v_b, v_w_spatial, v_b_spatial, v_w_pa, v_w_pb, v_w_o, v_b_o, v_ln_out_g, v_ln_out_b):
    n_seq, seq, _ = x.shape
    t = n_seq * seq
    tiles_per_seq = seq // TOKEN_TILE
    x2 = x.reshape(t, D)
    tgt = loss_target.reshape(t, D)

    conv_shard = jnp.pad(conv_w, ((0, HALO - CONV_K), (0, 0)))
    p, win4, wpa4, wpb4, wo4, conv4 = _proj_gather(
        x2, b_in.reshape(N_CHIPS, 1, W_BLOCK),
        [_place_shard(w_in, 256, BF16), _place_shard(w_pa, 128, BF16), _place_shard(w_pb, 128, BF16),
         _place_shard(w_o, 128, BF16), _place_shard(conv_shard, HALO // 2, F32)])
    win4 = win4.reshape(N_CHIPS, D, W_BLOCK)
    wpa, wpb, wo = wpa4.reshape(D, D), wpb4.reshape(D, D), wo4.reshape(D, D)
    convw = conv4.reshape(N_CHIPS, HALO, D // N_CHIPS).transpose(1, 0, 2).reshape(HALO, D)

    zeros = jnp.zeros((D,), F32)
    vecs = jnp.stack([conv_b, gn_g, gn_b, ln_v_g, ln_v_b, b_o, ln_out_g, ln_out_b])
    causal = jnp.tril(jnp.ones((CHUNK, CHUNK), bool))
    ws = jnp.where(causal[None], w_spatial, 0.0)
    ws_bf, wst_bf = ws.astype(BF16), ws.transpose(0, 2, 1).astype(BF16)
    bsp = jnp.repeat(b_spatial.T, GROUP_W, axis=1)

    h1, ya, yb, h3, s, mixed, dr, drb, xt, acc_f = _forward_tiles(p, x2, tgt, wpa, wpb, wo, convw, vecs, ws_bf, bsp, tiles_per_seq)
    dh1, dp, dya, dyb, acc_b, dbin_b, dws, dbsp_acc = _backward_tiles(p, h1, ya, yb, drb, wpa, wpb, wo, vecs, ws_bf, wst_bf, bsp)
    dp, dcw8, dbin_a = _conv_backward(dh1, p, dp, convw, tiles_per_seq)
    dcw = jnp.sum(dcw8.reshape(HALO, SUBLANES, D), axis=1)
    g_pa, g_pb, g_o = _grad_w_square(h3, dya), _grad_w_square(s, dyb), _grad_w_square(mixed, drb)

    d_b_in = jnp.concatenate([dbin_a[0], dbin_b[0, 2 * D:]])
    d_b_spatial = jnp.sum(dbsp_acc.reshape(CHUNK, N_GROUPS, GROUP_W), axis=2).T
    small = _pack_small(
        d_b_in, [dcw[CONV_K], acc_b[0], acc_b[1], acc_b[2], acc_b[3], acc_f[2], acc_f[0], acc_f[1], zeros],
        jnp.where(causal[None], dws, 0.0), d_b_spatial, acc_f[3])
    g_conv = dcw.reshape(2, HALO // 2, N_CHIPS, D // N_CHIPS).transpose(0, 2, 1, 3)
    small = small.reshape(2, N_CHIPS, SMALL_ROWS // 8, 128)

    grads = [g_pa, g_pb, g_o, g_conv, small]
    wire = [BF16, BF16, BF16, F32, F32]
    from_sibling = _exchange_halves(grads)
    sums = [_add_halves(g, b, wd) for g, b, wd in zip(grads, from_sibling, wire)]
    grad_x, q_in, chips_in, _, *from_chips = _grad_in_and_x(
        xt.reshape(2, D // 2, t), dp, win4, dr, [qw for _, qw in sums])
    grad_x = grad_x.reshape(x.shape)
    mine = [_add_chips(q_in, chips_in, False)]
    mine += [_add_chips(q, b, a == 4) for a, ((q, _), b) in enumerate(zip(sums, from_chips))]
    *full, small_parts = _share_results(mine[:5], mine[5])
    grad_w_in, grad_w_pa, grad_w_pb, grad_w_o = [f.reshape(w.shape) for f, w in zip(full[:4], (w_in, w_pa, w_pb, w_o))]
    grad_conv_w = full[4].reshape(HALO, D // N_CHIPS)[:CONV_K]

    big = {}
    for name, w, g, m, v in [("w_in", w_in, grad_w_in, m_w_in, v_w_in), ("w_pa", w_pa, grad_w_pa, m_w_pa, v_w_pa),
                             ("w_pb", w_pb, grad_w_pb, m_w_pb, v_w_pb), ("w_o", w_o, grad_w_o, m_w_o, v_w_o),
                             ("conv_w", conv_w, grad_conv_w, m_conv_w, v_conv_w)]:
        big[name] = (g,) + tuple(_adamw(w, g, m, v))
    small_w = [b_in, conv_b, gn_g, gn_b, ln_v_g, ln_v_b, b_o, ln_out_g, ln_out_b]
    small_m = [m_b_in, m_conv_b, m_gn_g, m_gn_b, m_ln_v_g, m_ln_v_b, m_b_o, m_ln_out_g, m_ln_out_b]
    small_v = [v_b_in, v_conv_b, v_gn_g, v_gn_b, v_ln_v_g, v_ln_v_b, v_b_o, v_ln_out_g, v_ln_out_b]
    pack = lambda vs, wsp, bs: _pack_small(vs[0], vs[1:] + [zeros], wsp, bs, zeros)
    sg, sd, sm, sv, loss8 = _adamw_small(
        pack(small_w, w_spatial, b_spatial), small_parts.reshape(8, SMALL_ROWS // 8, 128),
        pack(small_m, m_w_spatial, m_b_spatial), pack(small_v, v_w_spatial, v_b_spatial), LOSS_ROW)
    names = ["b_in", "conv_b", "gn_g", "gn_b", "ln_v_g", "ln_v_b", "b_o", "ln_out_g", "ln_out_b", "pad", "w_spatial", "b_spatial"]
    per_kind = [dict(zip(names, _unpack_small(a))) for a in (sg, sd, sm, sv)]

    order = ["w_in", "b_in", "conv_w", "conv_b", "gn_g", "gn_b", "ln_v_g", "ln_v_b", "w_spatial", "b_spatial",
             "w_pa", "w_pb", "w_o", "b_o", "ln_out_g", "ln_out_b"]
    outs = [loss8[0, 0], grad_x]
    for kind in range(4):
        outs += [big[n][kind] if n in big else per_kind[kind][n] for n in order]
    return tuple(outs)
```

```python
import functools
import math

import jax
import jax.numpy as jnp
from jax import lax
from jax.experimental import pallas as pl
from jax.experimental.pallas import tpu as pltpu

D = 1024
N_GROUPS = 8
GROUP_W = D // N_GROUPS
CHUNK = 128
CONV_K = 31
HALO = 32
D_IN = 8 * D
N_CHIPS = 4
W_BLOCK = D_IN // N_CHIPS
ALPHA = 2.0 ** 0.25
LN_EPS = 1e-5
ADAM_LR, ADAM_B1, ADAM_B2, ADAM_EPS, ADAM_WD, ADAM_STEP = 0.001, 0.9, 0.999, 1e-08, 0.01, 10

TOKEN_TILE = 256
VMEM_LIMIT = 56 * 1024 * 1024
MESH = pl.DeviceIdType.MESH
F32, BF16 = jnp.float32, jnp.bfloat16


def _sigmoid(x):
    return 1.0 / (1.0 + jnp.exp(-x))


def _gelu(x):
    c = math.sqrt(2.0 / math.pi)
    t = jnp.tanh(c * (x + 0.044715 * (x * x * x)))
    return x * (0.5 * (1.0 + t))


def _gelu_and_grad(x):
    c = math.sqrt(2.0 / math.pi)
    x2 = x * x
    t = jnp.tanh(c * (x + 0.044715 * (x2 * x)))
    cdf = 0.5 * (1.0 + t)
    return x * cdf, cdf + 0.5 * x * (1.0 - t * t) * (c * (1.0 + 3.0 * 0.044715 * x2))


def _norm_stats(v):
    mu = jnp.mean(v, axis=-1, keepdims=True)
    vc = v - mu
    var = jnp.mean(vc * vc, axis=-1, keepdims=True)
    rstd = lax.rsqrt(var + LN_EPS)
    return vc * rstd, rstd


def _norm_bwd(dxhat, xhat, rstd):
    m1 = jnp.mean(dxhat, axis=-1, keepdims=True)
    m2 = jnp.mean(dxhat * xhat, axis=-1, keepdims=True)
    return rstd * (dxhat - m1 - xhat * m2)


def _dot(a, b):
    return jnp.dot(a, b, preferred_element_type=F32)


def _dot_nt(a, b):
    return lax.dot_general(a, b, (((1,), (1,)), ((), ())), preferred_element_type=F32)


def _dot_tn(a, b):
    return lax.dot_general(a, b, (((0,), (0,)), ((), ())), preferred_element_type=F32)


def _colsum(v):
    return jnp.sum(v, axis=0, keepdims=True)


def _full(shape):
    return pl.BlockSpec(shape, lambda *_: (0,) * len(shape))


def _resident(shape):
    return pl.BlockSpec(shape, lambda *_: (0,) * len(shape), pipeline_mode=pl.Buffered(1))


def _params(*sem):
    return pltpu.CompilerParams(dimension_semantics=sem, vmem_limit_bytes=VMEM_LIMIT)


def _chip_index():
    return (2 * lax.axis_index("x") + lax.axis_index("y")).astype(jnp.int32).reshape(1)


def _core_index():
    return lax.axis_index("c").astype(jnp.int32).reshape(1)


def _place_shard(w, rows, dtype):
    r, c = w.shape
    steps = r // 2 // rows

    def body(k_ref, w_ref, o_ref):
        o_ref[...] = w_ref[...].astype(dtype)

    return pl.pallas_call(
        body, name="place_shard",
        grid_spec=pltpu.PrefetchScalarGridSpec(
            num_scalar_prefetch=1, grid=(2, steps),
            in_specs=[pl.BlockSpec((rows, c), lambda h, i, k: (h * steps + i, 0))],
            out_specs=pl.BlockSpec((None, None, rows, c), lambda h, i, k: (k[0], h, i, 0))),
        out_shape=jax.ShapeDtypeStruct((N_CHIPS, 2, r // 2, c), dtype),
        compiler_params=_params("parallel", "parallel"),
    )(_chip_index(), w)


def _position():
    x, y, c = lax.axis_index("x"), lax.axis_index("y"), lax.axis_index("c")
    return x, y, c, 2 * x + y


def _other_chips(x, y):
    return [(1 - x, y), (x, 1 - y), (1 - x, 1 - y)]


def _any_specs(n):
    return [pl.BlockSpec(memory_space=pl.ANY)] * n


def _proj_gather(x, b4, bufs):
    t = x.shape[0]
    tm = 1024
    steps = t // tm
    half = D // 2
    n = len(bufs)
    xi, yi = lax.axis_index("x"), lax.axis_index("y")
    order = jnp.stack([2 * xi + yi, 2 * (1 - xi) + yi, 2 * xi + (1 - yi), 2 * (1 - xi) + (1 - yi)]).astype(jnp.int32)

    def body(order_ref, x_ref, b_ref, *refs):
        p_ref, outs = refs[n], refs[n + 1:2 * n + 1]
        xb_ref, w_ref, lsem, send, recv, fsend, frecv = refs[2 * n + 1:]
        jj, i = pl.program_id(0), pl.program_id(1)
        x_, y_, c, k = _position()
        chips = _other_chips(x_, y_)

        def sent(a, r):
            cx, cy = chips[r]
            return pltpu.make_async_remote_copy(
                src_ref=outs[a].at[k, c], dst_ref=outs[a].at[k, c], send_sem=send.at[3 * a + r],
                recv_sem=recv.at[3 * a + r], device_id=(cx, cy, c), device_id_type=MESH)

        def landed(a, r):
            cx, cy = chips[r]
            return pltpu.make_async_remote_copy(
                src_ref=outs[a].at[2 * cx + cy, c], dst_ref=outs[a].at[2 * cx + cy, c], send_sem=send.at[3 * a + r],
                recv_sem=recv.at[3 * a + r], device_id=(cx, cy, c), device_id_type=MESH)

        def passed(a, r, h):
            cx, cy = chips[r]
            return pltpu.make_async_remote_copy(
                src_ref=outs[a].at[2 * cx + cy, h], dst_ref=outs[a].at[2 * cx + cy, h], send_sem=fsend.at[3 * a + r],
                recv_sem=frecv.at[3 * a + r], device_id=(x_, y_, 1 - c), device_id_type=MESH)

        def load(block):
            cps = [pltpu.make_async_copy(outs[0].at[block, h], w_ref.at[pl.ds(h * half, half)], lsem.at[h])
                   for h in range(2)]
            for cp in cps:
                cp.start()
            for cp in cps:
                cp.wait()

        @pl.when((jj == 0) & (i == 0))
        def _():
            for a in range(n):
                for r in range(3):
                    sent(a, r).start()
            load(k)

        for r in range(3):
            @pl.when((jj == r + 1) & (i == 0))
            def _(r=r):
                landed(0, r).wait_recv()
                passed(0, r, c).start()
                passed(0, r, 1 - c).wait_recv()
                load(2 * chips[r][0] + chips[r][1])

        rows = pl.ds(pl.multiple_of(i * tm, tm), tm)

        @pl.when(jj == 0)
        def _():
            xb_ref[rows, :] = x_ref[...].astype(BF16)

        p_ref[...] = _dot(xb_ref[rows, :], w_ref[...]) + b_ref[...]

        @pl.when((jj == N_CHIPS - 1) & (i == steps - 1))
        def _():
            for a in range(1, n):
                for r in range(3):
                    landed(a, r).wait_recv()
                    passed(a, r, c).start()
            for a in range(1, n):
                for r in range(3):
                    passed(a, r, 1 - c).wait_recv()
            for a in range(n):
                for r in range(3):
                    sent(a, r).wait_send()
                    passed(a, r, c).wait_send()

    any_spec = pl.BlockSpec(memory_space=pl.ANY)
    return pl.pallas_call(
        body, name="proj_gather",
        grid_spec=pltpu.PrefetchScalarGridSpec(
            num_scalar_prefetch=1, grid=(N_CHIPS, steps),
            in_specs=[pl.BlockSpec((tm, D), lambda jj, i, o: (jnp.where(jj == 0, i, steps - 1), 0)),
                      pl.BlockSpec((None, 1, W_BLOCK), lambda jj, i, o: (o[jj], 0, 0))] + [any_spec] * n,
            out_specs=[pl.BlockSpec((tm, W_BLOCK), lambda jj, i, o: (i, o[jj]))] + [any_spec] * n,
            scratch_shapes=[pltpu.VMEM((t, D), BF16), pltpu.VMEM((D, W_BLOCK), BF16), pltpu.SemaphoreType.DMA((2,)),
                            pltpu.SemaphoreType.DMA((3 * n,)), pltpu.SemaphoreType.DMA((3 * n,)),
                            pltpu.SemaphoreType.DMA((3 * n,)), pltpu.SemaphoreType.DMA((3 * n,))]),
        out_shape=[jax.ShapeDtypeStruct((t, D_IN), F32)] + [jax.ShapeDtypeStruct(b.shape, b.dtype) for b in bufs],
        input_output_aliases={3 + a: 1 + a for a in range(n)},
        compiler_params=_params("arbitrary", "arbitrary"),
    )(order, x, b4, *bufs)


def _exchange_halves(grads):
    n = len(grads)

    def body(*refs):
        ins, outs = refs[:n], refs[n:2 * n]
        send, recv = refs[2 * n:]
        x, y, c, _ = _position()
        cps = []
        for a in range(n):
            cp = pltpu.make_async_remote_copy(
                src_ref=ins[a].at[1 - c], dst_ref=outs[a], send_sem=send.at[a], recv_sem=recv.at[a],
                device_id=(x, y, 1 - c), device_id_type=MESH)
            cp.start()
            cps.append(cp)
        for cp in cps:
            cp.wait_recv()
        for cp in cps:
            cp.wait_send()

    return pl.pallas_call(
        body, name="rs_exchange_halves",
        in_specs=_any_specs(n), out_specs=_any_specs(n),
        out_shape=[jax.ShapeDtypeStruct(g.shape[1:], g.dtype) for g in grads],
        scratch_shapes=[pltpu.SemaphoreType.DMA((n,)), pltpu.SemaphoreType.DMA((n,))],
    )(*grads)


def _share_results(bufs, small):
    n = len(bufs)

    def body(*refs):
        outs, small_out = refs[n + 1:2 * n + 1], refs[2 * n + 1]
        send, recv, ssend, srecv = refs[2 * n + 2:]
        x, y, c, k = _position()
        cps = []
        for a in range(n):
            cp = pltpu.make_async_remote_copy(
                src_ref=outs[a].at[c], dst_ref=outs[a].at[c], send_sem=send.at[a], recv_sem=recv.at[a],
                device_id=(x, y, 1 - c), device_id_type=MESH)
            cp.start()
            cps.append(cp)
        waits = []
        for p in range(1, 8):
            px, py, pc = x ^ (p >> 2), y ^ ((p >> 1) & 1), c ^ (p & 1)
            cp = pltpu.make_async_remote_copy(
                src_ref=small_out.at[k, c], dst_ref=small_out.at[k, c], send_sem=ssend.at[p - 1],
                recv_sem=srecv.at[p - 1], device_id=(px, py, pc), device_id_type=MESH)
            cp.start()
            cps.append(cp)
            waits.append(pltpu.make_async_remote_copy(
                src_ref=small_out.at[2 * px + py, pc], dst_ref=small_out.at[2 * px + py, pc], send_sem=ssend.at[p - 1],
                recv_sem=srecv.at[p - 1], device_id=(px, py, pc), device_id_type=MESH))
        for a in range(n):
            pltpu.make_async_remote_copy(
                src_ref=outs[a].at[1 - c], dst_ref=outs[a].at[1 - c], send_sem=send.at[a], recv_sem=recv.at[a],
                device_id=(x, y, 1 - c), device_id_type=MESH).wait_recv()
        for w in waits:
            w.wait_recv()
        for cp in cps:
            cp.wait_send()

    return pl.pallas_call(
        body, name="rs_share_results",
        in_specs=_any_specs(n + 1), out_specs=_any_specs(n + 1),
        out_shape=[jax.ShapeDtypeStruct(b.shape, b.dtype) for b in bufs + [small]],
        scratch_shapes=[pltpu.SemaphoreType.DMA((n,)), pltpu.SemaphoreType.DMA((n,)),
                        pltpu.SemaphoreType.DMA((7,)), pltpu.SemaphoreType.DMA((7,))],
        input_output_aliases={a: a for a in range(n + 1)},
    )(*bufs, small)


def _row_tile(r, c):
    t = max(8, min(r, (1 << 18) // c))
    while r % t:
        t //= 2
    return t


def _add_halves(g, b1, wire_dtype):
    _, _, r, c = g.shape
    t = _row_tile(r, c)

    def body(c_ref, g_ref, b_ref, q_ref, qw_ref):
        q = g_ref[...] + b_ref[...]
        q_ref[...] = q
        qw_ref[...] = q.astype(wire_dtype)

    core = _core_index()
    return pl.pallas_call(
        body, name="rs_add_halves",
        grid_spec=pltpu.PrefetchScalarGridSpec(
            num_scalar_prefetch=1, grid=(N_CHIPS, r // t),
            in_specs=[pl.BlockSpec((None, None, t, c), lambda j, i, cr: (cr[0], j, i, 0)),
                      pl.BlockSpec((None, t, c), lambda j, i, cr: (j, i, 0))],
            out_specs=[pl.BlockSpec((None, t, c), lambda j, i, cr: (j, i, 0)),
                       pl.BlockSpec((None, t, c), lambda j, i, cr: (j, i, 0))]),
        out_shape=[jax.ShapeDtypeStruct((N_CHIPS, r, c), F32), jax.ShapeDtypeStruct((N_CHIPS, r, c), wire_dtype)],
        compiler_params=_params("parallel", "parallel"),
    )(core, g, b1)


def _add_chips(q, b2, per_device):
    r, c = q.shape[-2:]
    t = _row_tile(r, c)

    def body(kc_ref, q_ref, b_ref, f_ref):
        f_ref[...] = ((q_ref[...] + b_ref[0].astype(F32)) + b_ref[1].astype(F32)) + b_ref[2].astype(F32)

    if per_device:
        out_spec = pl.BlockSpec((None, None, t, c), lambda i, kc: (kc[0], kc[1], i, 0))
        out_shape = jax.ShapeDtypeStruct((N_CHIPS, 2, r, c), F32)
    else:
        out_spec = pl.BlockSpec((None, t, c), lambda i, kc: (kc[1], i, 0))
        out_shape = jax.ShapeDtypeStruct((2, r, c), F32)
    return pl.pallas_call(
        body, name="rs_add_chips",
        grid_spec=pltpu.PrefetchScalarGridSpec(
            num_scalar_prefetch=1, grid=(r // t,),
            in_specs=[pl.BlockSpec((None, t, c), lambda i, kc: (kc[0], i, 0)) if q.ndim == 3
                      else pl.BlockSpec((t, c), lambda i, kc: (i, 0)),
                      pl.BlockSpec((3, t, c), lambda i, kc: (0, i, 0))],
            out_specs=out_spec),
        out_shape=out_shape,
        compiler_params=_params("parallel"),
    )(jnp.concatenate([_chip_index(), _core_index()]), q, b2)


def _adamw_math(w, g, m, v):
    m = ADAM_B1 * m + (1.0 - ADAM_B1) * g
    v = ADAM_B2 * v + (1.0 - ADAM_B2) * (g * g)
    m_hat = m / (1.0 - ADAM_B1 ** ADAM_STEP)
    v_hat = v / (1.0 - ADAM_B2 ** ADAM_STEP)
    delta = -ADAM_LR * (m_hat / (jnp.sqrt(v_hat) + ADAM_EPS) + ADAM_WD * w)
    return delta, m, v


def _adamw(w, g, m, v):
    r, c = w.shape
    t = _row_tile(r, c) if r % 8 == 0 else r

    def body(w_ref, g_ref, m_ref, v_ref, d_ref, nm_ref, nv_ref):
        d_ref[...], nm_ref[...], nv_ref[...] = _adamw_math(w_ref[...], g_ref[...], m_ref[...], v_ref[...])

    spec = pl.BlockSpec((t, c), lambda i: (i, 0))
    return pl.pallas_call(
        body, name="adamw", grid=(r // t,), in_specs=[spec] * 4, out_specs=[spec] * 3,
        out_shape=[jax.ShapeDtypeStruct((r, c), F32)] * 3, compiler_params=_params("parallel"),
    )(w, g, m, v)


def _adamw_small(w, parts, m, v, loss_row):
    r, c = w.shape

    def body(w_ref, p_ref, m_ref, v_ref, g_ref, d_ref, nm_ref, nv_ref, loss_ref):
        rows = r // 8
        for k in range(N_CHIPS):
            for core in range(2):
                g_ref[(core * N_CHIPS + k) * rows:(core * N_CHIPS + k + 1) * rows, :] = p_ref[2 * k + core]
        g = g_ref[...]
        d_ref[...], nm_ref[...], nv_ref[...] = _adamw_math(w_ref[...], g, m_ref[...], v_ref[...])
        lanes = g_ref[loss_row:loss_row + 8, :]
        loss_ref[...] = jnp.broadcast_to(jnp.sum(jnp.sum(lanes, axis=1, keepdims=True), axis=0, keepdims=True), (8, c))

    return pl.pallas_call(
        body, name="adamw_small",
        in_specs=[_full((r, c)), _full((8, r // 8, c)), _full((r, c)), _full((r, c))],
        out_specs=[_full((r, c))] * 4 + [_full((8, c))],
        out_shape=[jax.ShapeDtypeStruct((r, c), F32)] * 4 + [jax.ShapeDtypeStruct((8, c), F32)],
        compiler_params=_params(),
    )(w, parts, m, v)


SUBLANES = 8
SHIFT_ROWS = HALO - SUBLANES


def _shifted_copies(src_ref, sh_ref, cs, tm):
    for p in range(1, SUBLANES):
        sh_ref[p - 1] = src_ref[pl.ds(p, tm + SHIFT_ROWS), cs]


def _tap(src_ref, sh_ref, cs, offset, start, rows):
    p, q = offset % SUBLANES, offset // SUBLANES
    if p == 0:
        return src_ref[pl.ds(start + SUBLANES * q, rows), cs]
    return sh_ref[p - 1, pl.ds(start + SUBLANES * q, rows), :]


def _conv_taps(src_ref, sh_ref, w_ref, first_offset, step, bias, dst_ref, tm):
    rows = 64
    for g in range(N_GROUPS):
        cs = slice(g * GROUP_W, (g + 1) * GROUP_W)
        _shifted_copies(src_ref, sh_ref, cs, tm)
        for rb in range(tm // rows):
            acc = jnp.zeros((rows, GROUP_W), F32) + (bias[:, cs] if bias is not None else 0.0)
            for k in range(CONV_K):
                acc = acc + w_ref[k:k + 1, cs] * _tap(src_ref, sh_ref, cs, first_offset + step * k, rb * rows, rows)
            dst_ref[rb * rows:(rb + 1) * rows, cs] = acc


def _conv_weight_grad(d_ref, src_ref, sh_ref, first_offset, acc_ref, tm):
    for g in range(N_GROUPS):
        cs = slice(g * GROUP_W, (g + 1) * GROUP_W)
        _shifted_copies(src_ref, sh_ref, cs, tm)
        for k in range(CONV_K):
            prod = d_ref[:, cs] * _tap(src_ref, sh_ref, cs, first_offset + k, 0, tm)
            acc_ref[SUBLANES * k:SUBLANES * (k + 1), cs] += jnp.sum(prod.reshape(tm // SUBLANES, SUBLANES, GROUP_W), axis=0)


def _spatial_mix(w_ref, v_bf, tm):
    rows = []
    for q in range(tm // CHUNK):
        cols = [_dot(w_ref[h], v_bf[q * CHUNK:(q + 1) * CHUNK, h * GROUP_W:(h + 1) * GROUP_W])
                for h in range(N_GROUPS)]
        rows.append(jnp.concatenate(cols, axis=1))
    return jnp.concatenate(rows, axis=0)


def _group_norm_fwd(h1, gn_g, gn_b):
    xhat, rstd = [], []
    for g in range(N_GROUPS):
        xh, rs = _norm_stats(h1[:, g * GROUP_W:(g + 1) * GROUP_W])
        xhat.append(xh)
        rstd.append(rs)
    xhat = jnp.concatenate(xhat, axis=1)
    return xhat * gn_g + gn_b, xhat, rstd


def _forward_tiles(p, x, tgt, wpa, wpb, wo, convw, vecs, ws, bsp, tiles_per_seq):
    t = x.shape[0]
    tm = TOKEN_TILE
    hb = tm // HALO

    def body(p_ref, ph_ref, x_ref, t_ref, wpa_ref, wpb_ref, wo_ref, cw_ref, vec_ref, ws_ref, bsp_ref,
             h1_ref, ya_ref, yb_ref, h3_ref, s_ref, mx_ref, dr_ref, drb_ref, xt_ref, acc_ref, he_ref, sh_ref):
        i = pl.program_id(0)
        xt_ref[...] = x_ref[...].T.astype(BF16)
        conv_b, gn_g, gn_b, lnv_g, lnv_b, b_o, lno_g, lno_b = [vec_ref[j:j + 1, :] for j in range(8)]

        keep = jnp.where(i % tiles_per_seq == 0, 0.0, 1.0)
        he_ref[0:HALO, :] = ph_ref[:, 0:D] * _sigmoid(ph_ref[:, D:2 * D]) * keep
        he_ref[HALO:, :] = p_ref[:, 0:D] * _sigmoid(p_ref[:, D:2 * D])
        _conv_taps(he_ref, sh_ref, cw_ref, HALO - (CONV_K - 1), 1, conv_b, h1_ref, tm)
        h2, _, _ = _group_norm_fwd(h1_ref[...], gn_g, gn_b)
        a_gate = p_ref[:, 2 * D:3 * D]
        h3 = ((h2 * _sigmoid(h2)) * (a_gate * _sigmoid(a_gate))).astype(BF16)
        h3_ref[...] = h3
        ya = _dot(h3, wpa_ref[...])
        ya_ref[...] = ya

        u = _gelu(p_ref[:, 3 * D:4 * D])
        vhat, _ = _norm_stats(_gelu(p_ref[:, 4 * D:5 * D]))
        v1 = (vhat * lnv_g + lnv_b).astype(BF16)
        b_gate = p_ref[:, 5 * D:6 * D]
        vmix = _spatial_mix(ws_ref, v1, tm) + jnp.concatenate([bsp_ref[...]] * (tm // CHUNK), axis=0)
        s = (u * vmix * (b_gate * _sigmoid(b_gate))).astype(BF16)
        s_ref[...] = s
        yb = _dot(s, wpb_ref[...])
        yb_ref[...] = yb

        mixed = (_sigmoid(p_ref[:, 6 * D:7 * D]) * ya + _sigmoid(p_ref[:, 7 * D:8 * D]) * yb).astype(BF16)
        mx_ref[...] = mixed
        r = ALPHA * x_ref[...] + (_dot(mixed, wo_ref[...]) + b_o)
        xhat, rstd = _norm_stats(r)
        err = (xhat * lno_g + lno_b) - t_ref[...]
        dout = err * (1.0 / D)
        dr = _norm_bwd(dout * lno_g, xhat, rstd)
        dr_ref[...] = dr
        drb_ref[...] = dr.astype(BF16)

        @pl.when(i == 0)
        def _():
            acc_ref[...] = jnp.zeros_like(acc_ref)

        acc_ref[0:1, :] += _colsum(dout * xhat)
        acc_ref[1:2, :] += _colsum(dout)
        acc_ref[2:3, :] += _colsum(dr)
        acc_ref[3:4, :] += _colsum(err * err) * (0.5 / D)

    tile = lambda w: pl.BlockSpec((tm, w), lambda i: (i, 0))
    f32_out = jax.ShapeDtypeStruct((t, D), F32)
    bf_out = jax.ShapeDtypeStruct((t, D), BF16)
    return pl.pallas_call(
        body, name="forward_tiles", grid=(t // tm,),
        in_specs=[tile(D_IN),
                  pl.BlockSpec((HALO, 2 * D), lambda i: (jnp.maximum(i * hb - 1, 0), 0)),
                  tile(D), tile(D), _resident((D, D)), _resident((D, D)), _resident((D, D)), _full((HALO, D)), _full((8, D)),
                  _full((N_GROUPS, CHUNK, CHUNK)), _full((CHUNK, D))],
        out_specs=[tile(D)] * 8 + [pl.BlockSpec((D, tm), lambda i: (0, i)), _full((8, D))],
        out_shape=[f32_out, f32_out, f32_out, bf_out, bf_out, bf_out, f32_out, bf_out,
                   jax.ShapeDtypeStruct((D, t), BF16), jax.ShapeDtypeStruct((8, D), F32)],
        scratch_shapes=[pltpu.VMEM((tm + HALO, D), F32), pltpu.VMEM((SUBLANES - 1, tm + SHIFT_ROWS, GROUP_W), F32)],
        compiler_params=_params("arbitrary"),
    )(p, p, x, tgt, wpa, wpb, wo, convw, vecs, ws, bsp)


def _backward_tiles(p, h1, ya, yb, drb, wpa, wpb, wo, vecs, ws, wst, bsp):
    t = h1.shape[0]
    tm = TOKEN_TILE

    def body(p_ref, h1_ref, ya_ref, yb_ref, drb_ref, wpa_ref, wpb_ref, wo_ref, vec_ref, ws_ref, wst_ref, bsp_ref,
             dh1_ref, dp_ref, dya_ref, dyb_ref, acc_ref, dbin_ref, dws_ref, dbsp_ref):
        i = pl.program_id(0)
        _, gn_g, gn_b, lnv_g, lnv_b = [vec_ref[j:j + 1, :] for j in range(5)]

        @pl.when(i == 0)
        def _():
            acc_ref[...] = jnp.zeros_like(acc_ref)
            dbin_ref[...] = jnp.zeros_like(dbin_ref)
            dws_ref[...] = jnp.zeros_like(dws_ref)
            dbsp_ref[...] = jnp.zeros_like(dbsp_ref)

        def emit(block, val):
            dbin_ref[0:1, block * D:(block + 1) * D] += _colsum(val)
            dp_ref[:, block * D:(block + 1) * D] = val.astype(BF16)

        dp_ref[:, 0:2 * D] = jnp.zeros((tm, 2 * D), BF16)
        dmixed = _dot_nt(drb_ref[...], wo_ref[...])
        ga = _sigmoid(p_ref[:, 6 * D:7 * D])
        gb = _sigmoid(p_ref[:, 7 * D:8 * D])
        dya = (dmixed * ga).astype(BF16)
        dyb = (dmixed * gb).astype(BF16)
        dya_ref[...] = dya
        dyb_ref[...] = dyb
        emit(6, dmixed * ya_ref[...] * (ga * (1.0 - ga)))
        emit(7, dmixed * yb_ref[...] * (gb * (1.0 - gb)))

        dh3 = _dot_nt(dya, wpa_ref[...])
        h2, xhat, rstd = _group_norm_fwd(h1_ref[...], gn_g, gn_b)
        sg = _sigmoid(h2)
        a_gate = p_ref[:, 2 * D:3 * D]
        sa = _sigmoid(a_gate)
        dh2 = dh3 * (a_gate * sa) * (sg * (1.0 + h2 * (1.0 - sg)))
        emit(2, dh3 * (h2 * sg) * (sa * (1.0 + a_gate * (1.0 - sa))))
        acc_ref[0:1, :] += _colsum(dh2 * xhat)
        acc_ref[1:2, :] += _colsum(dh2)
        dxhat = dh2 * gn_g
        for g in range(N_GROUPS):
            cs = slice(g * GROUP_W, (g + 1) * GROUP_W)
            dh1_ref[:, cs] = _norm_bwd(dxhat[:, cs], xhat[:, cs], rstd[g])

        ds = _dot_nt(dyb, wpb_ref[...])
        u_pre = p_ref[:, 3 * D:4 * D]
        u, du_dpre = _gelu_and_grad(u_pre)
        v0, dv_dpre = _gelu_and_grad(p_ref[:, 4 * D:5 * D])
        vhat, vrstd = _norm_stats(v0)
        v1 = (vhat * lnv_g + lnv_b).astype(BF16)
        vmix = _spatial_mix(ws_ref, v1, tm) + jnp.concatenate([bsp_ref[...]] * (tm // CHUNK), axis=0)
        b_gate = p_ref[:, 5 * D:6 * D]
        sb = _sigmoid(b_gate)
        silu_b = b_gate * sb
        emit(3, ds * vmix * silu_b * du_dpre)
        emit(5, ds * u * vmix * (sb * (1.0 + b_gate * (1.0 - sb))))
        dvmix = ds * u * silu_b
        dvmix_bf = dvmix.astype(BF16)
        for q in range(tm // CHUNK):
            dbsp_ref[...] += dvmix[q * CHUNK:(q + 1) * CHUNK, :]
            for h in range(N_GROUPS):
                blk = (slice(q * CHUNK, (q + 1) * CHUNK), slice(h * GROUP_W, (h + 1) * GROUP_W))
                dws_ref[h] += _dot_nt(dvmix_bf[blk], v1[blk])
        dv1 = _spatial_mix(wst_ref, dvmix_bf, tm)
        acc_ref[2:3, :] += _colsum(dv1 * vhat)
        acc_ref[3:4, :] += _colsum(dv1)
        emit(4, _norm_bwd(dv1 * lnv_g, vhat, vrstd) * dv_dpre)

    tile = lambda w: pl.BlockSpec((tm, w), lambda i: (i, 0))
    return pl.pallas_call(
        body, name="backward_tiles", grid=(t // tm,),
        in_specs=[tile(D_IN), tile(D), tile(D), tile(D), tile(D), _resident((D, D)), _resident((D, D)), _resident((D, D)),
                  _full((8, D)), _full((N_GROUPS, CHUNK, CHUNK)), _full((N_GROUPS, CHUNK, CHUNK)), _full((CHUNK, D))],
        out_specs=[tile(D), tile(D_IN), tile(D), tile(D), _full((8, D)), _full((8, D_IN)),
                   _full((N_GROUPS, CHUNK, CHUNK)), _full((CHUNK, D))],
        out_shape=[jax.ShapeDtypeStruct((t, D), F32), jax.ShapeDtypeStruct((t, D_IN), BF16),
                   jax.ShapeDtypeStruct((t, D), BF16), jax.ShapeDtypeStruct((t, D), BF16),
                   jax.ShapeDtypeStruct((8, D), F32), jax.ShapeDtypeStruct((8, D_IN), F32),
                   jax.ShapeDtypeStruct((N_GROUPS, CHUNK, CHUNK), F32), jax.ShapeDtypeStruct((CHUNK, D), F32)],
        compiler_params=_params("arbitrary"),
    )(p, h1, ya, yb, drb, wpa, wpb, wo, vecs, ws, wst, bsp)


def _conv_backward(dh1, p, dp, convw, tiles_per_seq):
    t = dh1.shape[0]
    tm = TOKEN_TILE
    hb = tm // HALO
    last = t // HALO - 1

    def body(dh1_ref, dnext_ref, p_ref, ph_ref, cw_ref, dp_in_ref, dp_ref, dcw_ref, dbin_ref,
             de_ref, he_ref, dh0_ref, sh_ref):
        del dp_in_ref
        i = pl.program_id(0)

        @pl.when(i == 0)
        def _():
            dcw_ref[...] = jnp.zeros_like(dcw_ref)
            dbin_ref[...] = jnp.zeros_like(dbin_ref)

        keep_next = jnp.where(i % tiles_per_seq == tiles_per_seq - 1, 0.0, 1.0)
        de_ref[0:tm, :] = dh1_ref[...]
        de_ref[tm:, :] = dnext_ref[...] * keep_next
        _conv_taps(de_ref, sh_ref, cw_ref, CONV_K - 1, -1, None, dh0_ref, tm)

        keep_prev = jnp.where(i % tiles_per_seq == 0, 0.0, 1.0)
        sg = _sigmoid(p_ref[:, D:2 * D])
        val = p_ref[:, 0:D]
        he_ref[0:HALO, :] = ph_ref[:, 0:D] * _sigmoid(ph_ref[:, D:2 * D]) * keep_prev
        he_ref[HALO:, :] = val * sg
        _conv_weight_grad(dh1_ref, he_ref, sh_ref, HALO - (CONV_K - 1), dcw_ref, tm)
        dcw_ref[SUBLANES * CONV_K:, :] += jnp.sum(dh1_ref[...].reshape(tm // SUBLANES, SUBLANES, D), axis=0)

        dh0 = dh0_ref[...]
        dval = dh0 * sg
        dglu = dh0 * val * (sg * (1.0 - sg))
        dbin_ref[0:1, 0:D] += _colsum(dval)
        dbin_ref[0:1, D:2 * D] += _colsum(dglu)
        dp_ref[:, 0:D] = dval.astype(BF16)
        dp_ref[:, D:2 * D] = dglu.astype(BF16)

    return pl.pallas_call(
        body, name="conv_backward", grid=(t // tm,),
        in_specs=[pl.BlockSpec((tm, D), lambda i: (i, 0)),
                  pl.BlockSpec((HALO, D), lambda i: (jnp.minimum((i + 1) * hb, last), 0)),
                  pl.BlockSpec((tm, 2 * D), lambda i: (i, 0)),
                  pl.BlockSpec((HALO, 2 * D), lambda i: (jnp.maximum(i * hb - 1, 0), 0)),
                  _full((HALO, D)), pl.BlockSpec(memory_space=pl.ANY)],
        out_specs=[pl.BlockSpec((tm, 2 * D), lambda i: (i, 0)), _full((SUBLANES * HALO, D)), _full((8, 2 * D))],
        out_shape=[jax.ShapeDtypeStruct(dp.shape, BF16), jax.ShapeDtypeStruct((SUBLANES * HALO, D), F32),
                   jax.ShapeDtypeStruct((8, 2 * D), F32)],
        scratch_shapes=[pltpu.VMEM((tm + HALO, D), F32), pltpu.VMEM((tm + HALO, D), F32), pltpu.VMEM((tm, D), F32),
                        pltpu.VMEM((SUBLANES - 1, tm + SHIFT_ROWS, GROUP_W), F32)],
        input_output_aliases={5: 0},
        compiler_params=_params("arbitrary"),
    )(dh1, dh1, p, p, convw, dp)


def _grad_in_and_x(xt, dp, w4, dr, chip_parts):
    t = dr.shape[0]
    tm = TOKEN_TILE
    half, tn = D // 2, 512
    nb = W_BLOCK // tn
    n_w, n_x = 2 * N_CHIPS * nb, t // tm
    ns = len(chip_parts)
    xi, yi, ci = lax.axis_index("x"), lax.axis_index("y"), lax.axis_index("c")
    others = [2 * (1 - xi) + yi, 2 * xi + (1 - yi), 2 * (1 - xi) + (1 - yi)]
    blocks = others + others + [2 * xi + yi] * 2
    halves = [1 - ci] * 3 + [ci] * 3 + [1 - ci, ci]
    table = jnp.stack([jnp.stack([b * nb + n for b in blocks for n in range(nb)]),
                       jnp.stack([h for h in halves for _ in range(nb)])]).astype(jnp.int32)

    def body(tab_ref, xt_ref, dpc_ref, dpr_ref, w_ref, dr_ref, *refs):
        parts = refs[:ns]
        dx_ref, qk_ref, b2_ref, b1_ref, wire_ref = refs[ns:ns + 5]
        lands = refs[ns + 5:2 * ns + 5]
        (g_ref, st_ref, sb_ref, tmp_ref, d2d_send, d2d_recv, ici_send, ici_recv, own_sem, tmp_sem, wire_sem,
         p_send, p_recv) = refs[2 * ns + 5:]
        s = pl.program_id(0)
        x_, y_, c, _ = _position()
        chips = _other_chips(x_, y_)
        n = s % nb
        grp = s // nb
        cols = pl.ds(pl.multiple_of(n * tn, tn), tn)

        def part(a, r):
            cx, cy = chips[r]
            return pltpu.make_async_remote_copy(
                src_ref=parts[a].at[2 * cx + cy], dst_ref=lands[a].at[r], send_sem=p_send.at[3 * a + r],
                recv_sem=p_recv.at[3 * a + r], device_id=(cx, cy, c), device_id_type=MESH)

        def to_sibling(slot, land):
            return pltpu.make_async_remote_copy(
                src_ref=st_ref.at[slot], dst_ref=b1_ref.at[land, :, cols], send_sem=d2d_send.at[slot],
                recv_sem=d2d_recv.at[land * nb + n], device_id=(x_, y_, 1 - c), device_id_type=MESH)

        def to_chip(r):
            cx, cy = chips[r]
            return pltpu.make_async_remote_copy(
                src_ref=wire_ref.at[r, :, cols], dst_ref=b2_ref.at[r, :, cols], send_sem=ici_send.at[r],
                recv_sem=ici_recv.at[r], device_id=(cx, cy, c), device_id_type=MESH)

        def all_of_chip(r):
            cx, cy = chips[r]
            return pltpu.make_async_remote_copy(
                src_ref=wire_ref.at[r], dst_ref=b2_ref.at[r], send_sem=ici_send.at[r],
                recv_sem=ici_recv.at[r], device_id=(cx, cy, c), device_id_type=MESH)

        def to_result(slot):
            return pltpu.make_async_copy(st_ref.at[slot], qk_ref.at[:, cols], own_sem.at[slot])

        def sibling_piece(land):
            return pltpu.make_async_copy(b1_ref.at[land, :, cols], tmp_ref, tmp_sem)

        @pl.when(s == 0)
        def _():
            for a in range(ns):
                for r in range(3):
                    part(a, r).start()

        own_half = ((grp >= 3) & (grp <= 5)) | (grp == 7)
        land = jnp.where(grp == 7, 3, grp - 3)

        @pl.when(own_half)
        def _():
            to_sibling(0, land).wait_recv()
            sibling_piece(land).start()

        @pl.when(s < n_w)
        def _():
            g_ref[...] = _dot(xt_ref[tab_ref[1, s]], dpc_ref[...])

        @pl.when(own_half)
        def _():
            sibling_piece(land).wait()

        for g in range(2 * N_CHIPS):
            @pl.when(grp == g)
            def _(g=g):
                if g in (0, 1, 2, 6):
                    use = s if g < 3 else 3 * nb + n
                    slot = use % 2

                    @pl.when(use >= 2)
                    def _():
                        to_sibling(slot, 0).wait_send()

                    st_ref[slot] = g_ref[...]
                    to_sibling(slot, min(g, 3)).start()
                elif g in (3, 4, 5):
                    sb_ref[...] = (g_ref[...] + tmp_ref[...]).astype(BF16)
                    stage = pltpu.make_async_copy(sb_ref, wire_ref.at[g - 3, :, cols], wire_sem)
                    stage.start()
                    stage.wait()
                    to_chip(g - 3).start()
                else:
                    slot = n % 2
                    piece = g_ref[...] + tmp_ref[...]

                    @pl.when(n < 2)
                    def _():
                        to_sibling(slot, 0).wait_send()

                    @pl.when(n >= 2)
                    def _():
                        to_result(slot).wait()

                    st_ref[slot] = piece
                    to_result(slot).start()

        @pl.when(s >= n_w)
        def _():
            acc = ALPHA * dr_ref[...]
            for j in range(N_CHIPS):
                acc = acc + _dot_nt(dpr_ref[:, j * W_BLOCK:(j + 1) * W_BLOCK], w_ref[j])
            dx_ref[...] = acc

        @pl.when(s == n_w + n_x - 1)
        def _():
            for slot in range(2):
                to_result(slot).wait()
            for r in range(3):
                all_of_chip(r).wait_recv()
                all_of_chip(r).wait_send()
            for a in range(ns):
                for r in range(3):
                    part(a, r).wait_recv()
                    part(a, r).wait_send()

    any_spec = pl.BlockSpec(memory_space=pl.ANY)
    tile = lambda s, tab: (jnp.maximum(s - n_w, 0), 0)
    return pl.pallas_call(
        body, name="grad_in_and_x",
        grid_spec=pltpu.PrefetchScalarGridSpec(
            num_scalar_prefetch=1, grid=(n_w + n_x,),
            in_specs=[pl.BlockSpec((2, half, t), lambda s, tab: (0, 0, 0), pipeline_mode=pl.Buffered(1)),
                      pl.BlockSpec((t, tn), lambda s, tab: (0, tab[0, jnp.minimum(s, n_w - 1)])),
                      pl.BlockSpec((tm, D_IN), tile),
                      pl.BlockSpec((N_CHIPS, D, W_BLOCK), lambda s, tab: (0, 0, 0), pipeline_mode=pl.Buffered(1)),
                      pl.BlockSpec((tm, D), tile)] + [any_spec] * ns,
            out_specs=[pl.BlockSpec((tm, D), tile)] + [any_spec] * (4 + ns),
            scratch_shapes=[pltpu.VMEM((half, tn), F32), pltpu.VMEM((2, half, tn), F32), pltpu.VMEM((half, tn), BF16),
                            pltpu.VMEM((half, tn), F32),
                            pltpu.SemaphoreType.DMA((2,)), pltpu.SemaphoreType.DMA((N_CHIPS * nb,)),
                            pltpu.SemaphoreType.DMA((3,)), pltpu.SemaphoreType.DMA((3,)),
                            pltpu.SemaphoreType.DMA((2,)), pltpu.SemaphoreType.DMA, pltpu.SemaphoreType.DMA,
                            pltpu.SemaphoreType.DMA((3 * ns,)), pltpu.SemaphoreType.DMA((3 * ns,))]),
        out_shape=[jax.ShapeDtypeStruct((t, D), F32), jax.ShapeDtypeStruct((half, W_BLOCK), F32),
                   jax.ShapeDtypeStruct((3, half, W_BLOCK), BF16), jax.ShapeDtypeStruct((N_CHIPS, half, W_BLOCK), F32),
                   jax.ShapeDtypeStruct((3, half, W_BLOCK), BF16)]
        + [jax.ShapeDtypeStruct((3,) + q.shape[1:], q.dtype) for q in chip_parts],
        compiler_params=_params("arbitrary"),
    )(table, xt, dp, dp, w4, dr, *chip_parts)


def _grad_w_square(a, b):
    t = a.shape[0]
    tk = 512
    rows = D // 8

    def body(a_ref, b_ref, o_ref, acc_ref):
        s = pl.program_id(0)

        @pl.when(s == 0)
        def _():
            acc_ref[...] = jnp.zeros_like(acc_ref)

        acc_ref[...] += _dot_tn(a_ref[...], b_ref[...])

        @pl.when(s == pl.num_programs(0) - 1)
        def _():
            for j in range(N_CHIPS):
                for h in range(2):
                    o_ref[h, j] = acc_ref[(2 * j + h) * rows:(2 * j + h + 1) * rows, :]

    return pl.pallas_call(
        body, name="grad_w_square", grid=(t // tk,),
        in_specs=[pl.BlockSpec((tk, D), lambda s: (s, 0)), pl.BlockSpec((tk, D), lambda s: (s, 0))],
        out_specs=_full((2, N_CHIPS, rows, D)),
        out_shape=jax.ShapeDtypeStruct((2, N_CHIPS, rows, D), F32),
        scratch_shapes=[pltpu.VMEM((D, D), F32)],
        compiler_params=_params("arbitrary"),
    )(a, b)


SMALL_ROWS = 1216


def _pack_small(b_in, vec9, w_spatial, b_spatial, loss_lanes):
    parts = [b_in.reshape(64, 128)] + [v.reshape(8, 128) for v in vec9]
    parts += [w_spatial.reshape(N_GROUPS * CHUNK, CHUNK), b_spatial.reshape(8, 128), loss_lanes.reshape(8, 128)]
    used = sum(p.shape[0] for p in parts)
    return jnp.concatenate(parts + [jnp.zeros((SMALL_ROWS - used, 128), F32)], axis=0)


LOSS_ROW = 64 + 9 * 8 + N_GROUPS * CHUNK + 8


def _unpack_small(a):
    out, row = [], 0
    for rows, shape in [(64, (D_IN,))] + [(8, (D,))] * 9 + [(N_GROUPS * CHUNK, (N_GROUPS, CHUNK, CHUNK)), (8, (N_GROUPS, CHUNK))]:
        out.append(a[row:row + rows].reshape(shape))
        row += rows
    return out


def kernel(x, w_in, b_in, conv_w, conv_b, gn_g, gn_b, ln_v_g, ln_v_b, w_spatial, b_spatial, w_pa, w_pb, w_o, b_o, ln_out_g, ln_out_b, loss_target, m_w_in, m_b_in, m_conv_w, m_conv_b, m_gn_g, m_gn_b, m_ln_v_g, m_ln_v_b, m_w_spatial, m_b_spatial, m_w_pa, m_w_pb, m_w_o, m_b_o, m_ln_out_g, m_ln_out_b, v_w_in, v_b_in, v_conv_w, v_conv_b, v_gn_g, v_gn_b, v_ln_v_g, v_ln_v_b, v_w_spatial, v_b_spatial, v_w_pa, v_w_pb, v_w_o, v_b_o, v_ln_out_g, v_ln_out_b):
    n_seq, seq, _ = x.shape
    t = n_seq * seq
    tiles_per_seq = seq // TOKEN_TILE
    x2 = x.reshape(t, D)
    tgt = loss_target.reshape(t, D)

    conv_shard = jnp.pad(conv_w, ((0, HALO - CONV_K), (0, 0)))
    p, win4, wpa4, wpb4, wo4, conv4 = _proj_gather(
        x2, b_in.reshape(N_CHIPS, 1, W_BLOCK),
        [_place_shard(w_in, 256, BF16), _place_shard(w_pa, 128, BF16), _place_shard(w_pb, 128, BF16),
         _place_shard(w_o, 128, BF16), _place_shard(conv_shard, HALO // 2, F32)])
    win4 = win4.reshape(N_CHIPS, D, W_BLOCK)
    wpa, wpb, wo = wpa4.reshape(D, D), wpb4.reshape(D, D), wo4.reshape(D, D)
    convw = conv4.reshape(N_CHIPS, HALO, D // N_CHIPS).transpose(1, 0, 2).reshape(HALO, D)

    zeros = jnp.zeros((D,), F32)
    vecs = jnp.stack([conv_b, gn_g, gn_b, ln_v_g, ln_v_b, b_o, ln_out_g, ln_out_b])
    causal = jnp.tril(jnp.ones((CHUNK, CHUNK), bool))
    ws = jnp.where(causal[None], w_spatial, 0.0)
    ws_bf, wst_bf = ws.astype(BF16), ws.transpose(0, 2, 1).astype(BF16)
    bsp = jnp.repeat(b_spatial.T, GROUP_W, axis=1)

    h1, ya, yb, h3, s, mixed, dr, drb, xt, acc_f = _forward_tiles(p, x2, tgt, wpa, wpb, wo, convw, vecs, ws_bf, bsp, tiles_per_seq)
    dh1, dp, dya, dyb, acc_b, dbin_b, dws, dbsp_acc = _backward_tiles(p, h1, ya, yb, drb, wpa, wpb, wo, vecs, ws_bf, wst_bf, bsp)
    dp, dcw8, dbin_a = _conv_backward(dh1, p, dp, convw, tiles_per_seq)
    dcw = jnp.sum(dcw8.reshape(HALO, SUBLANES, D), axis=1)
    g_pa, g_pb, g_o = _grad_w_square(h3, dya), _grad_w_square(s, dyb), _grad_w_square(mixed, drb)

    d_b_in = jnp.concatenate([dbin_a[0], dbin_b[0, 2 * D:]])
    d_b_spatial = jnp.sum(dbsp_acc.reshape(CHUNK, N_GROUPS, GROUP_W), axis=2).T
    small = _pack_small(
        d_b_in, [dcw[CONV_K], acc_b[0], acc_b[1], acc_b[2], acc_b[3], acc_f[2], acc_f[0], acc_f[1], zeros],
        jnp.where(causal[None], dws, 0.0), d_b_spatial, acc_f[3])
    g_conv = dcw.reshape(2, HALO // 2, N_CHIPS, D // N_CHIPS).transpose(0, 2, 1, 3)
    small = small.reshape(2, N_CHIPS, SMALL_ROWS // 8, 128)

    grads = [g_pa, g_pb, g_o, g_conv, small]
    wire = [BF16, BF16, BF16, F32, F32]
    from_sibling = _exchange_halves(grads)
    sums = [_add_halves(g, b, wd) for g, b, wd in zip(grads, from_sibling, wire)]
    grad_x, q_in, chips_in, _, _, *from_chips = _grad_in_and_x(
        xt.reshape(2, D // 2, t), dp, win4, dr, [qw for _, qw in sums])
    grad_x = grad_x.reshape(x.shape)
    mine = [_add_chips(q_in, chips_in, False)]
    mine += [_add_chips(q, b, a == 4) for a, ((q, _), b) in enumerate(zip(sums, from_chips))]
    *full, small_parts = _share_results(mine[:5], mine[5])
    grad_w_in, grad_w_pa, grad_w_pb, grad_w_o = [f.reshape(w.shape) for f, w in zip(full[:4], (w_in, w_pa, w_pb, w_o))]
    grad_conv_w = full[4].reshape(HALO, D // N_CHIPS)[:CONV_K]

    big = {}
    for name, w, g, m, v in [("w_in", w_in, grad_w_in, m_w_in, v_w_in), ("w_pa", w_pa, grad_w_pa, m_w_pa, v_w_pa),
                             ("w_pb", w_pb, grad_w_pb, m_w_pb, v_w_pb), ("w_o", w_o, grad_w_o, m_w_o, v_w_o),
                             ("conv_w", conv_w, grad_conv_w, m_conv_w, v_conv_w)]:
        big[name] = (g,) + tuple(_adamw(w, g, m, v))
    small_w = [b_in, conv_b, gn_g, gn_b, ln_v_g, ln_v_b, b_o, ln_out_g, ln_out_b]
    small_m = [m_b_in, m_conv_b, m_gn_g, m_gn_b, m_ln_v_g, m_ln_v_b, m_b_o, m_ln_out_g, m_ln_out_b]
    small_v = [v_b_in, v_conv_b, v_gn_g, v_gn_b, v_ln_v_g, v_ln_v_b, v_b_o, v_ln_out_g, v_ln_out_b]
    pack = lambda vs, wsp, bs: _pack_small(vs[0], vs[1:] + [zeros], wsp, bs, zeros)
    sg, sd, sm, sv, loss8 = _adamw_small(
        pack(small_w, w_spatial, b_spatial), small_parts.reshape(8, SMALL_ROWS // 8, 128),
        pack(small_m, m_w_spatial, m_b_spatial), pack(small_v, v_w_spatial, v_b_spatial), LOSS_ROW)
    names = ["b_in", "conv_b", "gn_g", "gn_b", "ln_v_g", "ln_v_b", "b_o", "ln_out_g", "ln_out_b", "pad", "w_spatial", "b_spatial"]
    per_kind = [dict(zip(names, _unpack_small(a))) for a in (sg, sd, sm, sv)]

    order = ["w_in", "b_in", "conv_w", "conv_b", "gn_g", "gn_b", "ln_v_g", "ln_v_b", "w_spatial", "b_spatial",
             "w_pa", "w_pb", "w_o", "b_o", "ln_out_g", "ln_out_b"]
    outs = [loss8[0, 0], grad_x]
    for kind in range(4):
        outs += [big[n][kind] if n in big else per_kind[kind][n] for n in order]
    return tuple(outs)
```

```python
import functools
import math

import jax
import jax.numpy as jnp
from jax import lax
from jax.experimental import pallas as pl
from jax.experimental.pallas import tpu as pltpu

D = 1024
N_GROUPS = 8
GROUP_W = D // N_GROUPS
CHUNK = 128
CONV_K = 31
HALO = 32
D_IN = 8 * D
N_CHIPS = 4
W_BLOCK = D_IN // N_CHIPS
ALPHA = 2.0 ** 0.25
LN_EPS = 1e-5
ADAM_LR, ADAM_B1, ADAM_B2, ADAM_EPS, ADAM_WD, ADAM_STEP = 0.001, 0.9, 0.999, 1e-08, 0.01, 10

TOKEN_TILE = 256
VMEM_LIMIT = 56 * 1024 * 1024
MESH = pl.DeviceIdType.MESH
F32, BF16 = jnp.float32, jnp.bfloat16


def _sigmoid(x):
    return 1.0 / (1.0 + jnp.exp(-x))


def _gelu(x):
    c = math.sqrt(2.0 / math.pi)
    t = jnp.tanh(c * (x + 0.044715 * (x * x * x)))
    return x * (0.5 * (1.0 + t))


def _gelu_and_grad(x):
    c = math.sqrt(2.0 / math.pi)
    x2 = x * x
    t = jnp.tanh(c * (x + 0.044715 * (x2 * x)))
    cdf = 0.5 * (1.0 + t)
    return x * cdf, cdf + 0.5 * x * (1.0 - t * t) * (c * (1.0 + 3.0 * 0.044715 * x2))


def _norm_stats(v):
    mu = jnp.mean(v, axis=-1, keepdims=True)
    vc = v - mu
    var = jnp.mean(vc * vc, axis=-1, keepdims=True)
    rstd = lax.rsqrt(var + LN_EPS)
    return vc * rstd, rstd


def _norm_bwd(dxhat, xhat, rstd):
    m1 = jnp.mean(dxhat, axis=-1, keepdims=True)
    m2 = jnp.mean(dxhat * xhat, axis=-1, keepdims=True)
    return rstd * (dxhat - m1 - xhat * m2)


def _dot(a, b):
    return jnp.dot(a, b, preferred_element_type=F32)


def _dot_nt(a, b):
    return lax.dot_general(a, b, (((1,), (1,)), ((), ())), preferred_element_type=F32)


def _dot_tn(a, b):
    return lax.dot_general(a, b, (((0,), (0,)), ((), ())), preferred_element_type=F32)


def _colsum(v):
    return jnp.sum(v, axis=0, keepdims=True)


def _full(shape):
    return pl.BlockSpec(shape, lambda *_: (0,) * len(shape))


def _resident(shape):
    return pl.BlockSpec(shape, lambda *_: (0,) * len(shape), pipeline_mode=pl.Buffered(1))


def _params(*sem):
    return pltpu.CompilerParams(dimension_semantics=sem, vmem_limit_bytes=VMEM_LIMIT)


def _chip_index():
    return (2 * lax.axis_index("x") + lax.axis_index("y")).astype(jnp.int32).reshape(1)


def _core_index():
    return lax.axis_index("c").astype(jnp.int32).reshape(1)


def _place_shard(w, rows, dtype):
    r, c = w.shape
    steps = r // 2 // rows

    def body(k_ref, w_ref, o_ref):
        o_ref[...] = w_ref[...].astype(dtype)

    return pl.pallas_call(
        body, name="place_shard",
        grid_spec=pltpu.PrefetchScalarGridSpec(
            num_scalar_prefetch=1, grid=(2, steps),
            in_specs=[pl.BlockSpec((rows, c), lambda h, i, k: (h * steps + i, 0))],
            out_specs=pl.BlockSpec((None, None, rows, c), lambda h, i, k: (k[0], h, i, 0))),
        out_shape=jax.ShapeDtypeStruct((N_CHIPS, 2, r // 2, c), dtype),
        compiler_params=_params("parallel", "parallel"),
    )(_chip_index(), w)


def _position():
    x, y, c = lax.axis_index("x"), lax.axis_index("y"), lax.axis_index("c")
    return x, y, c, 2 * x + y


def _other_chips(x, y):
    return [(1 - x, y), (x, 1 - y), (1 - x, 1 - y)]


def _any_specs(n):
    return [pl.BlockSpec(memory_space=pl.ANY)] * n


def _proj_gather(x, b4, bufs):
    t = x.shape[0]
    tm = 1024
    steps = t // tm
    half = D // 2
    n = len(bufs)
    xi, yi = lax.axis_index("x"), lax.axis_index("y")
    order = jnp.stack([2 * xi + yi, 2 * (1 - xi) + yi, 2 * xi + (1 - yi), 2 * (1 - xi) + (1 - yi)]).astype(jnp.int32)

    def body(order_ref, x_ref, b_ref, *refs):
        p_ref, outs = refs[n], refs[n + 1:2 * n + 1]
        xb_ref, w_ref, lsem, send, recv, fsend, frecv = refs[2 * n + 1:]
        jj, i = pl.program_id(0), pl.program_id(1)
        x_, y_, c, k = _position()
        chips = _other_chips(x_, y_)

        def sent(a, r):
            cx, cy = chips[r]
            return pltpu.make_async_remote_copy(
                src_ref=outs[a].at[k, c], dst_ref=outs[a].at[k, c], send_sem=send.at[3 * a + r],
                recv_sem=recv.at[3 * a + r], device_id=(cx, cy, c), device_id_type=MESH)

        def landed(a, r):
            cx, cy = chips[r]
            return pltpu.make_async_remote_copy(
                src_ref=outs[a].at[2 * cx + cy, c], dst_ref=outs[a].at[2 * cx + cy, c], send_sem=send.at[3 * a + r],
                recv_sem=recv.at[3 * a + r], device_id=(cx, cy, c), device_id_type=MESH)

        def passed(a, r, h):
            cx, cy = chips[r]
            return pltpu.make_async_remote_copy(
                src_ref=outs[a].at[2 * cx + cy, h], dst_ref=outs[a].at[2 * cx + cy, h], send_sem=fsend.at[3 * a + r],
                recv_sem=frecv.at[3 * a + r], device_id=(x_, y_, 1 - c), device_id_type=MESH)

        def load(block):
            cps = [pltpu.make_async_copy(outs[0].at[block, h], w_ref.at[pl.ds(h * half, half)], lsem.at[h])
                   for h in range(2)]
            for cp in cps:
                cp.start()
            for cp in cps:
                cp.wait()

        @pl.when((jj == 0) & (i == 0))
        def _():
            for a in range(n):
                for r in range(3):
                    sent(a, r).start()
            load(k)

        for r in range(3):
            @pl.when((jj == r + 1) & (i == 0))
            def _(r=r):
                landed(0, r).wait_recv()
                passed(0, r, c).start()
                passed(0, r, 1 - c).wait_recv()
                load(2 * chips[r][0] + chips[r][1])

        rows = pl.ds(pl.multiple_of(i * tm, tm), tm)

        @pl.when(jj == 0)
        def _():
            xb_ref[rows, :] = x_ref[...].astype(BF16)

        p_ref[...] = _dot(xb_ref[rows, :], w_ref[...]) + b_ref[...]

        @pl.when((jj == N_CHIPS - 1) & (i == steps - 1))
        def _():
            for a in range(1, n):
                for r in range(3):
                    landed(a, r).wait_recv()
                    passed(a, r, c).start()
            for a in range(1, n):
                for r in range(3):
                    passed(a, r, 1 - c).wait_recv()
            for a in range(n):
                for r in range(3):
                    sent(a, r).wait_send()
                    passed(a, r, c).wait_send()

    any_spec = pl.BlockSpec(memory_space=pl.ANY)
    return pl.pallas_call(
        body, name="proj_gather",
        grid_spec=pltpu.PrefetchScalarGridSpec(
            num_scalar_prefetch=1, grid=(N_CHIPS, steps),
            in_specs=[pl.BlockSpec((tm, D), lambda jj, i, o: (jnp.where(jj == 0, i, steps - 1), 0)),
                      pl.BlockSpec((None, 1, W_BLOCK), lambda jj, i, o: (o[jj], 0, 0))] + [any_spec] * n,
            out_specs=[pl.BlockSpec((tm, W_BLOCK), lambda jj, i, o: (i, o[jj]))] + [any_spec] * n,
            scratch_shapes=[pltpu.VMEM((t, D), BF16), pltpu.VMEM((D, W_BLOCK), BF16), pltpu.SemaphoreType.DMA((2,)),
                            pltpu.SemaphoreType.DMA((3 * n,)), pltpu.SemaphoreType.DMA((3 * n,)),
                            pltpu.SemaphoreType.DMA((3 * n,)), pltpu.SemaphoreType.DMA((3 * n,))]),
        out_shape=[jax.ShapeDtypeStruct((t, D_IN), F32)] + [jax.ShapeDtypeStruct(b.shape, b.dtype) for b in bufs],
        input_output_aliases={3 + a: 1 + a for a in range(n)},
        compiler_params=_params("arbitrary", "arbitrary"),
    )(order, x, b4, *bufs)


def _exchange_halves(grads):
    n = len(grads)

    def body(*refs):
        ins, outs = refs[:n], refs[n:2 * n]
        send, recv = refs[2 * n:]
        x, y, c, _ = _position()
        cps = []
        for a in range(n):
            cp = pltpu.make_async_remote_copy(
                src_ref=ins[a].at[1 - c], dst_ref=outs[a], send_sem=send.at[a], recv_sem=recv.at[a],
                device_id=(x, y, 1 - c), device_id_type=MESH)
            cp.start()
            cps.append(cp)
        for cp in cps:
            cp.wait_recv()
        for cp in cps:
            cp.wait_send()

    return pl.pallas_call(
        body, name="rs_exchange_halves",
        in_specs=_any_specs(n), out_specs=_any_specs(n),
        out_shape=[jax.ShapeDtypeStruct(g.shape[1:], g.dtype) for g in grads],
        scratch_shapes=[pltpu.SemaphoreType.DMA((n,)), pltpu.SemaphoreType.DMA((n,))],
    )(*grads)


def _share_results(bufs, small):
    n = len(bufs)

    def body(*refs):
        outs, small_out = refs[n + 1:2 * n + 1], refs[2 * n + 1]
        send, recv, ssend, srecv = refs[2 * n + 2:]
        x, y, c, k = _position()
        cps = []
        for a in range(n):
            cp = pltpu.make_async_remote_copy(
                src_ref=outs[a].at[c], dst_ref=outs[a].at[c], send_sem=send.at[a], recv_sem=recv.at[a],
                device_id=(x, y, 1 - c), device_id_type=MESH)
            cp.start()
            cps.append(cp)
        waits = []
        for p in range(1, 8):
            px, py, pc = x ^ (p >> 2), y ^ ((p >> 1) & 1), c ^ (p & 1)
            cp = pltpu.make_async_remote_copy(
                src_ref=small_out.at[k, c], dst_ref=small_out.at[k, c], send_sem=ssend.at[p - 1],
                recv_sem=srecv.at[p - 1], device_id=(px, py, pc), device_id_type=MESH)
            cp.start()
            cps.append(cp)
            waits.append(pltpu.make_async_remote_copy(
                src_ref=small_out.at[2 * px + py, pc], dst_ref=small_out.at[2 * px + py, pc], send_sem=ssend.at[p - 1],
                recv_sem=srecv.at[p - 1], device_id=(px, py, pc), device_id_type=MESH))
        for a in range(n):
            pltpu.make_async_remote_copy(
                src_ref=outs[a].at[1 - c], dst_ref=outs[a].at[1 - c], send_sem=send.at[a], recv_sem=recv.at[a],
                device_id=(x, y, 1 - c), device_id_type=MESH).wait_recv()
        for w in waits:
            w.wait_recv()
        for cp in cps:
            cp.wait_send()

    return pl.pallas_call(
        body, name="rs_share_results",
        in_specs=_any_specs(n + 1), out_specs=_any_specs(n + 1),
        out_shape=[jax.ShapeDtypeStruct(b.shape, b.dtype) for b in bufs + [small]],
        scratch_shapes=[pltpu.SemaphoreType.DMA((n,)), pltpu.SemaphoreType.DMA((n,)),
                        pltpu.SemaphoreType.DMA((7,)), pltpu.SemaphoreType.DMA((7,))],
        input_output_aliases={a: a for a in range(n + 1)},
    )(*bufs, small)


def _row_tile(r, c):
    t = max(8, min(r, (1 << 18) // c))
    while r % t:
        t //= 2
    return t


def _add_halves(g, b1, wire_dtype):
    _, _, r, c = g.shape
    t = _row_tile(r, c)

    def body(c_ref, g_ref, b_ref, q_ref, qw_ref):
        q = g_ref[...] + b_ref[...]
        q_ref[...] = q
        qw_ref[...] = q.astype(wire_dtype)

    core = _core_index()
    return pl.pallas_call(
        body, name="rs_add_halves",
        grid_spec=pltpu.PrefetchScalarGridSpec(
            num_scalar_prefetch=1, grid=(N_CHIPS, r // t),
            in_specs=[pl.BlockSpec((None, None, t, c), lambda j, i, cr: (cr[0], j, i, 0)),
                      pl.BlockSpec((None, t, c), lambda j, i, cr: (j, i, 0))],
            out_specs=[pl.BlockSpec((None, t, c), lambda j, i, cr: (j, i, 0)),
                       pl.BlockSpec((None, t, c), lambda j, i, cr: (j, i, 0))]),
        out_shape=[jax.ShapeDtypeStruct((N_CHIPS, r, c), F32), jax.ShapeDtypeStruct((N_CHIPS, r, c), wire_dtype)],
        compiler_params=_params("parallel", "parallel"),
    )(core, g, b1)


def _add_chips(q, b2, per_device):
    r, c = q.shape[-2:]
    t = _row_tile(r, c)

    def body(kc_ref, q_ref, b_ref, f_ref):
        f_ref[...] = ((q_ref[...] + b_ref[0].astype(F32)) + b_ref[1].astype(F32)) + b_ref[2].astype(F32)

    if per_device:
        out_spec = pl.BlockSpec((None, None, t, c), lambda i, kc: (kc[0], kc[1], i, 0))
        out_shape = jax.ShapeDtypeStruct((N_CHIPS, 2, r, c), F32)
    else:
        out_spec = pl.BlockSpec((None, t, c), lambda i, kc: (kc[1], i, 0))
        out_shape = jax.ShapeDtypeStruct((2, r, c), F32)
    return pl.pallas_call(
        body, name="rs_add_chips",
        grid_spec=pltpu.PrefetchScalarGridSpec(
            num_scalar_prefetch=1, grid=(r // t,),
            in_specs=[pl.BlockSpec((None, t, c), lambda i, kc: (kc[0], i, 0)) if q.ndim == 3
                      else pl.BlockSpec((t, c), lambda i, kc: (i, 0)),
                      pl.BlockSpec((3, t, c), lambda i, kc: (0, i, 0))],
            out_specs=out_spec),
        out_shape=out_shape,
        compiler_params=_params("parallel"),
    )(jnp.concatenate([_chip_index(), _core_index()]), q, b2)


def _adamw_math(w, g, m, v):
    m = ADAM_B1 * m + (1.0 - ADAM_B1) * g
    v = ADAM_B2 * v + (1.0 - ADAM_B2) * (g * g)
    m_hat = m / (1.0 - ADAM_B1 ** ADAM_STEP)
    v_hat = v / (1.0 - ADAM_B2 ** ADAM_STEP)
    delta = -ADAM_LR * (m_hat / (jnp.sqrt(v_hat) + ADAM_EPS) + ADAM_WD * w)
    return delta, m, v


def _adamw(w, g, m, v):
    r, c = w.shape
    t = _row_tile(r, c) if r % 8 == 0 else r

    def body(w_ref, g_ref, m_ref, v_ref, d_ref, nm_ref, nv_ref):
        d_ref[...], nm_ref[...], nv_ref[...] = _adamw_math(w_ref[...], g_ref[...], m_ref[...], v_ref[...])

    spec = pl.BlockSpec((t, c), lambda i: (i, 0))
    return pl.pallas_call(
        body, name="adamw", grid=(r // t,), in_specs=[spec] * 4, out_specs=[spec] * 3,
        out_shape=[jax.ShapeDtypeStruct((r, c), F32)] * 3, compiler_params=_params("parallel"),
    )(w, g, m, v)


ROW_B_IN = 0
ROW_VECS = 8
ROW_LOSS = 16
ROW_B_SPATIAL = 24
ROW_W_SPATIAL = 32
SMALL_ROWS = 192
N_VECS = 8


def _pack_small(acc_f, acc_b, dbin_a, dbin_b, dcw8, dws, dbsp):
    cols = D // N_CHIPS

    def body(af_ref, ab_ref, da_ref, db_ref, cw_ref, ws_ref, bs_ref, o_ref, gc_ref):
        o_ref[...] = jnp.zeros_like(o_ref)
        for j in range(D_IN // D):
            src = da_ref if j < 2 else db_ref
            o_ref[ROW_B_IN + j:ROW_B_IN + j + 1, :] = src[0:1, j * D:(j + 1) * D]
        dcw = jnp.sum(cw_ref[...].reshape(HALO, SUBLANES, D), axis=1)
        o_ref[ROW_VECS:ROW_VECS + 1, :] = dcw[CONV_K:CONV_K + 1]
        o_ref[ROW_VECS + 1:ROW_VECS + 5, :] = ab_ref[0:4, :]
        o_ref[ROW_VECS + 5:ROW_VECS + 6, :] = af_ref[2:3, :]
        o_ref[ROW_VECS + 6:ROW_VECS + 8, :] = af_ref[0:2, :]
        o_ref[ROW_LOSS:ROW_LOSS + 1, :] = af_ref[3:4, :]
        head = lax.broadcasted_iota(jnp.int32, (N_GROUPS, D), 0)
        lane = lax.broadcasted_iota(jnp.int32, (N_GROUPS, D), 1)
        indicator = jnp.where(lane // GROUP_W == head, 1.0, 0.0)
        o_ref[ROW_B_SPATIAL:ROW_B_SPATIAL + N_GROUPS, 0:CHUNK] = lax.dot_general(
            indicator, bs_ref[...], (((1,), (1,)), ((), ())), precision=lax.Precision.HIGHEST, preferred_element_type=F32)
        t_idx = lax.broadcasted_iota(jnp.int32, (CHUNK, D), 0)
        s_idx = lax.broadcasted_iota(jnp.int32, (CHUNK, D), 1) % CHUNK
        o_ref[ROW_W_SPATIAL:ROW_W_SPATIAL + CHUNK, :] = jnp.where(s_idx <= t_idx, ws_ref[...], 0.0)
        for h in range(2):
            for j in range(N_CHIPS):
                gc_ref[h, j] = dcw[h * (HALO // 2):(h + 1) * (HALO // 2), j * cols:(j + 1) * cols]

    ins = [acc_f, acc_b, dbin_a, dbin_b, dcw8, dws, dbsp]
    return pl.pallas_call(
        body, name="pack_small",
        in_specs=[_full(a.shape) for a in ins],
        out_specs=[_full((SMALL_ROWS, D)), _full((2, N_CHIPS, HALO // 2, cols))],
        out_shape=[jax.ShapeDtypeStruct((SMALL_ROWS, D), F32), jax.ShapeDtypeStruct((2, N_CHIPS, HALO // 2, cols), F32)],
        compiler_params=_params(),
    )(*ins)


def _adamw_small(parts, vecs, b_in, w_spatial, b_spatial):
    triples = list(vecs) + [b_in, w_spatial, b_spatial]
    n_in = 3 * len(triples)

    def body(p_ref, *refs):
        ins = [refs[3 * i:3 * i + 3] for i in range(len(triples))]
        outs = [refs[n_in + 4 * i:n_in + 4 * i + 4] for i in range(len(triples))]
        loss_ref, g_ref = refs[n_in + 4 * len(triples):]
        rows = SMALL_ROWS // 8
        for k in range(N_CHIPS):
            for core in range(2):
                g_ref[(core * N_CHIPS + k) * rows:(core * N_CHIPS + k + 1) * rows, :] = p_ref[2 * k + core]

        def step(g, wmv, out, get, put):
            d, nm, nv = _adamw_math(get(wmv[0]), g, get(wmv[1]), get(wmv[2]))
            for o, val in zip(out, (g, d, nm, nv)):
                put(o, val)

        for i in range(N_VECS):
            step(g_ref[ROW_VECS + i:ROW_VECS + i + 1, :], ins[i], outs[i],
                 lambda r: r[...].reshape(1, D), lambda o, val: o.__setitem__(Ellipsis, val.reshape(D)))
        for j in range(D_IN // D):
            piece = pl.ds(j * D, D)
            step(g_ref[ROW_B_IN + j:ROW_B_IN + j + 1, :], ins[N_VECS], outs[N_VECS],
                 lambda r: r[piece].reshape(1, D), lambda o, val: o.__setitem__(piece, val.reshape(D)))
        for h in range(N_GROUPS):
            step(g_ref[ROW_W_SPATIAL:ROW_W_SPATIAL + CHUNK, h * CHUNK:(h + 1) * CHUNK], ins[N_VECS + 1], outs[N_VECS + 1],
                 lambda r: r[h], lambda o, val: o.__setitem__(h, val))
        step(g_ref[ROW_B_SPATIAL:ROW_B_SPATIAL + N_GROUPS, 0:CHUNK], ins[N_VECS + 2], outs[N_VECS + 2],
             lambda r: r[...], lambda o, val: o.__setitem__(Ellipsis, val))
        lanes = g_ref[ROW_LOSS:ROW_LOSS + 1, :]
        loss_ref[...] = jnp.broadcast_to(jnp.sum(lanes, axis=1, keepdims=True), (8, 128))

    flat = [a for tr in triples for a in tr]
    out_shape = [jax.ShapeDtypeStruct(tr[0].shape, F32) for tr in triples for _ in range(4)]
    out_shape.append(jax.ShapeDtypeStruct((8, 128), F32))
    res = pl.pallas_call(
        body, name="adamw_small",
        in_specs=[_full(parts.shape)] + [_full(a.shape) for a in flat],
        out_specs=[_full(o.shape) for o in out_shape],
        out_shape=out_shape,
        scratch_shapes=[pltpu.VMEM((SMALL_ROWS, D), F32)],
        compiler_params=_params(),
    )(parts, *flat)
    return [res[4 * i:4 * i + 4] for i in range(len(triples))], res[-1]


SUBLANES = 8
SHIFT_ROWS = HALO - SUBLANES


def _shifted_copies(src_ref, sh_ref, cs, tm):
    for p in range(1, SUBLANES):
        sh_ref[p - 1] = src_ref[pl.ds(p, tm + SHIFT_ROWS), cs]


def _tap(src_ref, sh_ref, cs, offset, start, rows):
    p, q = offset % SUBLANES, offset // SUBLANES
    if p == 0:
        return src_ref[pl.ds(start + SUBLANES * q, rows), cs]
    return sh_ref[p - 1, pl.ds(start + SUBLANES * q, rows), :]


def _conv_taps(src_ref, sh_ref, w_ref, first_offset, step, bias, dst_ref, tm):
    rows = 64
    for g in range(N_GROUPS):
        cs = slice(g * GROUP_W, (g + 1) * GROUP_W)
        _shifted_copies(src_ref, sh_ref, cs, tm)
        for rb in range(tm // rows):
            acc = jnp.zeros((rows, GROUP_W), F32) + (bias[:, cs] if bias is not None else 0.0)
            for k in range(CONV_K):
                acc = acc + w_ref[k:k + 1, cs] * _tap(src_ref, sh_ref, cs, first_offset + step * k, rb * rows, rows)
            dst_ref[rb * rows:(rb + 1) * rows, cs] = acc


def _conv_weight_grad(d_ref, src_ref, sh_ref, first_offset, acc_ref, tm):
    for g in range(N_GROUPS):
        cs = slice(g * GROUP_W, (g + 1) * GROUP_W)
        _shifted_copies(src_ref, sh_ref, cs, tm)
        for k in range(CONV_K):
            prod = d_ref[:, cs] * _tap(src_ref, sh_ref, cs, first_offset + k, 0, tm)
            acc_ref[SUBLANES * k:SUBLANES * (k + 1), cs] += jnp.sum(prod.reshape(tm // SUBLANES, SUBLANES, GROUP_W), axis=0)


def _spatial_mix(w_ref, v_bf, tm):
    rows = []
    for q in range(tm // CHUNK):
        cols = [_dot(w_ref[h], v_bf[q * CHUNK:(q + 1) * CHUNK, h * GROUP_W:(h + 1) * GROUP_W])
                for h in range(N_GROUPS)]
        rows.append(jnp.concatenate(cols, axis=1))
    return jnp.concatenate(rows, axis=0)


def _group_norm_fwd(h1, gn_g, gn_b):
    xhat, rstd = [], []
    for g in range(N_GROUPS):
        xh, rs = _norm_stats(h1[:, g * GROUP_W:(g + 1) * GROUP_W])
        xhat.append(xh)
        rstd.append(rs)
    xhat = jnp.concatenate(xhat, axis=1)
    return xhat * gn_g + gn_b, xhat, rstd


def _forward_tiles(p, x, tgt, wpa, wpb, wo, convw, vecs, ws, bsp, tiles_per_seq):
    t = x.shape[0]
    tm = TOKEN_TILE
    hb = tm // HALO

    def body(p_ref, ph_ref, x_ref, t_ref, wpa_ref, wpb_ref, wo_ref, cw_ref, vec_ref, ws_ref, bsp_ref,
             h1_ref, ya_ref, yb_ref, h3_ref, s_ref, mx_ref, dr_ref, drb_ref, xt_ref, acc_ref, he_ref, sh_ref):
        i = pl.program_id(0)
        xt_ref[...] = x_ref[...].T.astype(BF16)
        conv_b, gn_g, gn_b, lnv_g, lnv_b, b_o, lno_g, lno_b = [vec_ref[j:j + 1, :] for j in range(8)]

        keep = jnp.where(i % tiles_per_seq == 0, 0.0, 1.0)
        he_ref[0:HALO, :] = ph_ref[:, 0:D] * _sigmoid(ph_ref[:, D:2 * D]) * keep
        he_ref[HALO:, :] = p_ref[:, 0:D] * _sigmoid(p_ref[:, D:2 * D])
        _conv_taps(he_ref, sh_ref, cw_ref, HALO - (CONV_K - 1), 1, conv_b, h1_ref, tm)
        h2, _, _ = _group_norm_fwd(h1_ref[...], gn_g, gn_b)
        a_gate = p_ref[:, 2 * D:3 * D]
        h3 = ((h2 * _sigmoid(h2)) * (a_gate * _sigmoid(a_gate))).astype(BF16)
        h3_ref[...] = h3
        ya = _dot(h3, wpa_ref[...])
        ya_ref[...] = ya

        u = _gelu(p_ref[:, 3 * D:4 * D])
        vhat, _ = _norm_stats(_gelu(p_ref[:, 4 * D:5 * D]))
        v1 = (vhat * lnv_g + lnv_b).astype(BF16)
        b_gate = p_ref[:, 5 * D:6 * D]
        vmix = _spatial_mix(ws_ref, v1, tm) + jnp.concatenate([bsp_ref[...]] * (tm // CHUNK), axis=0)
        s = (u * vmix * (b_gate * _sigmoid(b_gate))).astype(BF16)
        s_ref[...] = s
        yb = _dot(s, wpb_ref[...])
        yb_ref[...] = yb

        mixed = (_sigmoid(p_ref[:, 6 * D:7 * D]) * ya + _sigmoid(p_ref[:, 7 * D:8 * D]) * yb).astype(BF16)
        mx_ref[...] = mixed
        r = ALPHA * x_ref[...] + (_dot(mixed, wo_ref[...]) + b_o)
        xhat, rstd = _norm_stats(r)
        err = (xhat * lno_g + lno_b) - t_ref[...]
        dout = err * (1.0 / D)
        dr = _norm_bwd(dout * lno_g, xhat, rstd)
        dr_ref[...] = dr
        drb_ref[...] = dr.astype(BF16)

        @pl.when(i == 0)
        def _():
            acc_ref[...] = jnp.zeros_like(acc_ref)

        acc_ref[0:1, :] += _colsum(dout * xhat)
        acc_ref[1:2, :] += _colsum(dout)
        acc_ref[2:3, :] += _colsum(dr)
        acc_ref[3:4, :] += _colsum(err * err) * (0.5 / D)

    tile = lambda w: pl.BlockSpec((tm, w), lambda i: (i, 0))
    f32_out = jax.ShapeDtypeStruct((t, D), F32)
    bf_out = jax.ShapeDtypeStruct((t, D), BF16)
    return pl.pallas_call(
        body, name="forward_tiles", grid=(t // tm,),
        in_specs=[tile(D_IN),
                  pl.BlockSpec((HALO, 2 * D), lambda i: (jnp.maximum(i * hb - 1, 0), 0)),
                  tile(D), tile(D), _resident((D, D)), _resident((D, D)), _resident((D, D)), _full((HALO, D)), _full((8, D)),
                  _full((N_GROUPS, CHUNK, CHUNK)), _full((CHUNK, D))],
        out_specs=[tile(D)] * 8 + [pl.BlockSpec((D, tm), lambda i: (0, i)), _full((8, D))],
        out_shape=[f32_out, f32_out, f32_out, bf_out, bf_out, bf_out, f32_out, bf_out,
                   jax.ShapeDtypeStruct((D, t), BF16), jax.ShapeDtypeStruct((8, D), F32)],
        scratch_shapes=[pltpu.VMEM((tm + HALO, D), F32), pltpu.VMEM((SUBLANES - 1, tm + SHIFT_ROWS, GROUP_W), F32)],
        compiler_params=_params("arbitrary"),
    )(p, p, x, tgt, wpa, wpb, wo, convw, vecs, ws, bsp)


def _backward_tiles(p, h1, ya, yb, drb, wpa, wpb, wo, vecs, ws, wst, bsp):
    t = h1.shape[0]
    tm = TOKEN_TILE

    def body(p_ref, h1_ref, ya_ref, yb_ref, drb_ref, wpa_ref, wpb_ref, wo_ref, vec_ref, ws_ref, wst_ref, bsp_ref,
             dh1_ref, dp_ref, dya_ref, dyb_ref, acc_ref, dbin_ref, dws_ref, dbsp_ref):
        i = pl.program_id(0)
        _, gn_g, gn_b, lnv_g, lnv_b = [vec_ref[j:j + 1, :] for j in range(5)]

        @pl.when(i == 0)
        def _():
            acc_ref[...] = jnp.zeros_like(acc_ref)
            dbin_ref[...] = jnp.zeros_like(dbin_ref)
            dws_ref[...] = jnp.zeros_like(dws_ref)
            dbsp_ref[...] = jnp.zeros_like(dbsp_ref)

        def emit(block, val):
            dbin_ref[0:1, block * D:(block + 1) * D] += _colsum(val)
            dp_ref[:, block * D:(block + 1) * D] = val.astype(BF16)

        dp_ref[:, 0:2 * D] = jnp.zeros((tm, 2 * D), BF16)
        dmixed = _dot_nt(drb_ref[...], wo_ref[...])
        ga = _sigmoid(p_ref[:, 6 * D:7 * D])
        gb = _sigmoid(p_ref[:, 7 * D:8 * D])
        dya = (dmixed * ga).astype(BF16)
        dyb = (dmixed * gb).astype(BF16)
        dya_ref[...] = dya
        dyb_ref[...] = dyb
        emit(6, dmixed * ya_ref[...] * (ga * (1.0 - ga)))
        emit(7, dmixed * yb_ref[...] * (gb * (1.0 - gb)))

        dh3 = _dot_nt(dya, wpa_ref[...])
        h2, xhat, rstd = _group_norm_fwd(h1_ref[...], gn_g, gn_b)
        sg = _sigmoid(h2)
        a_gate = p_ref[:, 2 * D:3 * D]
        sa = _sigmoid(a_gate)
        dh2 = dh3 * (a_gate * sa) * (sg * (1.0 + h2 * (1.0 - sg)))
        emit(2, dh3 * (h2 * sg) * (sa * (1.0 + a_gate * (1.0 - sa))))
        acc_ref[0:1, :] += _colsum(dh2 * xhat)
        acc_ref[1:2, :] += _colsum(dh2)
        dxhat = dh2 * gn_g
        for g in range(N_GROUPS):
            cs = slice(g * GROUP_W, (g + 1) * GROUP_W)
            dh1_ref[:, cs] = _norm_bwd(dxhat[:, cs], xhat[:, cs], rstd[g])

        ds = _dot_nt(dyb, wpb_ref[...])
        u_pre = p_ref[:, 3 * D:4 * D]
        u, du_dpre = _gelu_and_grad(u_pre)
        v0, dv_dpre = _gelu_and_grad(p_ref[:, 4 * D:5 * D])
        vhat, vrstd = _norm_stats(v0)
        v1 = (vhat * lnv_g + lnv_b).astype(BF16)
        vmix = _spatial_mix(ws_ref, v1, tm) + jnp.concatenate([bsp_ref[...]] * (tm // CHUNK), axis=0)
        b_gate = p_ref[:, 5 * D:6 * D]
        sb = _sigmoid(b_gate)
        silu_b = b_gate * sb
        emit(3, ds * vmix * silu_b * du_dpre)
        emit(5, ds * u * vmix * (sb * (1.0 + b_gate * (1.0 - sb))))
        dvmix = ds * u * silu_b
        dvmix_bf = dvmix.astype(BF16)
        for q in range(tm // CHUNK):
            dbsp_ref[...] += dvmix[q * CHUNK:(q + 1) * CHUNK, :]
            for h in range(N_GROUPS):
                blk = (slice(q * CHUNK, (q + 1) * CHUNK), slice(h * GROUP_W, (h + 1) * GROUP_W))
                dws_ref[:, h * GROUP_W:(h + 1) * GROUP_W] += _dot_nt(dvmix_bf[blk], v1[blk])
        dv1 = _spatial_mix(wst_ref, dvmix_bf, tm)
        acc_ref[2:3, :] += _colsum(dv1 * vhat)
        acc_ref[3:4, :] += _colsum(dv1)
        emit(4, _norm_bwd(dv1 * lnv_g, vhat, vrstd) * dv_dpre)

    tile = lambda w: pl.BlockSpec((tm, w), lambda i: (i, 0))
    return pl.pallas_call(
        body, name="backward_tiles", grid=(t // tm,),
        in_specs=[tile(D_IN), tile(D), tile(D), tile(D), tile(D), _resident((D, D)), _resident((D, D)), _resident((D, D)),
                  _full((8, D)), _full((N_GROUPS, CHUNK, CHUNK)), _full((N_GROUPS, CHUNK, CHUNK)), _full((CHUNK, D))],
        out_specs=[tile(D), tile(D_IN), tile(D), tile(D), _full((8, D)), _full((8, D_IN)),
                   _full((CHUNK, D)), _full((CHUNK, D))],
        out_shape=[jax.ShapeDtypeStruct((t, D), F32), jax.ShapeDtypeStruct((t, D_IN), BF16),
                   jax.ShapeDtypeStruct((t, D), BF16), jax.ShapeDtypeStruct((t, D), BF16),
                   jax.ShapeDtypeStruct((8, D), F32), jax.ShapeDtypeStruct((8, D_IN), F32),
                   jax.ShapeDtypeStruct((CHUNK, D), F32), jax.ShapeDtypeStruct((CHUNK, D), F32)],
        compiler_params=_params("arbitrary"),
    )(p, h1, ya, yb, drb, wpa, wpb, wo, vecs, ws, wst, bsp)


def _conv_backward(dh1, p, dp, convw, tiles_per_seq):
    t = dh1.shape[0]
    tm = TOKEN_TILE
    hb = tm // HALO
    last = t // HALO - 1

    def body(dh1_ref, dnext_ref, p_ref, ph_ref, cw_ref, dp_in_ref, dp_ref, dcw_ref, dbin_ref,
             de_ref, he_ref, dh0_ref, sh_ref):
        del dp_in_ref
        i = pl.program_id(0)

        @pl.when(i == 0)
        def _():
            dcw_ref[...] = jnp.zeros_like(dcw_ref)
            dbin_ref[...] = jnp.zeros_like(dbin_ref)

        keep_next = jnp.where(i % tiles_per_seq == tiles_per_seq - 1, 0.0, 1.0)
        de_ref[0:tm, :] = dh1_ref[...]
        de_ref[tm:, :] = dnext_ref[...] * keep_next
        _conv_taps(de_ref, sh_ref, cw_ref, CONV_K - 1, -1, None, dh0_ref, tm)

        keep_prev = jnp.where(i % tiles_per_seq == 0, 0.0, 1.0)
        sg = _sigmoid(p_ref[:, D:2 * D])
        val = p_ref[:, 0:D]
        he_ref[0:HALO, :] = ph_ref[:, 0:D] * _sigmoid(ph_ref[:, D:2 * D]) * keep_prev
        he_ref[HALO:, :] = val * sg
        _conv_weight_grad(dh1_ref, he_ref, sh_ref, HALO - (CONV_K - 1), dcw_ref, tm)
        dcw_ref[SUBLANES * CONV_K:, :] += jnp.sum(dh1_ref[...].reshape(tm // SUBLANES, SUBLANES, D), axis=0)

        dh0 = dh0_ref[...]
        dval = dh0 * sg
        dglu = dh0 * val * (sg * (1.0 - sg))
        dbin_ref[0:1, 0:D] += _colsum(dval)
        dbin_ref[0:1, D:2 * D] += _colsum(dglu)
        dp_ref[:, 0:D] = dval.astype(BF16)
        dp_ref[:, D:2 * D] = dglu.astype(BF16)

    return pl.pallas_call(
        body, name="conv_backward", grid=(t // tm,),
        in_specs=[pl.BlockSpec((tm, D), lambda i: (i, 0)),
                  pl.BlockSpec((HALO, D), lambda i: (jnp.minimum((i + 1) * hb, last), 0)),
                  pl.BlockSpec((tm, 2 * D), lambda i: (i, 0)),
                  pl.BlockSpec((HALO, 2 * D), lambda i: (jnp.maximum(i * hb - 1, 0), 0)),
                  _full((HALO, D)), pl.BlockSpec(memory_space=pl.ANY)],
        out_specs=[pl.BlockSpec((tm, 2 * D), lambda i: (i, 0)), _full((SUBLANES * HALO, D)), _full((8, 2 * D))],
        out_shape=[jax.ShapeDtypeStruct(dp.shape, BF16), jax.ShapeDtypeStruct((SUBLANES * HALO, D), F32),
                   jax.ShapeDtypeStruct((8, 2 * D), F32)],
        scratch_shapes=[pltpu.VMEM((tm + HALO, D), F32), pltpu.VMEM((tm + HALO, D), F32), pltpu.VMEM((tm, D), F32),
                        pltpu.VMEM((SUBLANES - 1, tm + SHIFT_ROWS, GROUP_W), F32)],
        input_output_aliases={5: 0},
        compiler_params=_params("arbitrary"),
    )(dh1, dh1, p, p, convw, dp)


def _grad_in_and_x(xt, dp, w4, dr, chip_parts):
    t = dr.shape[0]
    tm = TOKEN_TILE
    half, tn = D // 2, 512
    nb = W_BLOCK // tn
    n_w, n_x = 2 * N_CHIPS * nb, t // tm
    ns = len(chip_parts)
    xi, yi, ci = lax.axis_index("x"), lax.axis_index("y"), lax.axis_index("c")
    others = [2 * (1 - xi) + yi, 2 * xi + (1 - yi), 2 * (1 - xi) + (1 - yi)]
    blocks = others + others + [2 * xi + yi] * 2
    halves = [1 - ci] * 3 + [ci] * 3 + [1 - ci, ci]
    table = jnp.stack([jnp.stack([b * nb + n for b in blocks for n in range(nb)]),
                       jnp.stack([h for h in halves for _ in range(nb)])]).astype(jnp.int32)

    def body(tab_ref, xt_ref, dpc_ref, dpr_ref, w_ref, dr_ref, *refs):
        parts = refs[:ns]
        dx_ref, qk_ref, b2_ref, b1_ref, wire_ref = refs[ns:ns + 5]
        lands = refs[ns + 5:2 * ns + 5]
        (g_ref, st_ref, sb_ref, tmp_ref, d2d_send, d2d_recv, ici_send, ici_recv, own_sem, tmp_sem, wire_sem,
         p_send, p_recv) = refs[2 * ns + 5:]
        s = pl.program_id(0)
        x_, y_, c, _ = _position()
        chips = _other_chips(x_, y_)
        n = s % nb
        grp = s // nb
        cols = pl.ds(pl.multiple_of(n * tn, tn), tn)

        def part(a, r):
            cx, cy = chips[r]
            return pltpu.make_async_remote_copy(
                src_ref=parts[a].at[2 * cx + cy], dst_ref=lands[a].at[r], send_sem=p_send.at[3 * a + r],
                recv_sem=p_recv.at[3 * a + r], device_id=(cx, cy, c), device_id_type=MESH)

        def to_sibling(slot, land):
            return pltpu.make_async_remote_copy(
                src_ref=st_ref.at[slot], dst_ref=b1_ref.at[land, :, cols], send_sem=d2d_send.at[slot],
                recv_sem=d2d_recv.at[land * nb + n], device_id=(x_, y_, 1 - c), device_id_type=MESH)

        def to_chip(r):
            cx, cy = chips[r]
            return pltpu.make_async_remote_copy(
                src_ref=wire_ref.at[r, :, cols], dst_ref=b2_ref.at[r, :, cols], send_sem=ici_send.at[r],
                recv_sem=ici_recv.at[r], device_id=(cx, cy, c), device_id_type=MESH)

        def all_of_chip(r):
            cx, cy = chips[r]
            return pltpu.make_async_remote_copy(
                src_ref=wire_ref.at[r], dst_ref=b2_ref.at[r], send_sem=ici_send.at[r],
                recv_sem=ici_recv.at[r], device_id=(cx, cy, c), device_id_type=MESH)

        def to_result(slot):
            return pltpu.make_async_copy(st_ref.at[slot], qk_ref.at[:, cols], own_sem.at[slot])

        def sibling_piece(land):
            return pltpu.make_async_copy(b1_ref.at[land, :, cols], tmp_ref, tmp_sem)

        @pl.when(s == 0)
        def _():
            for a in range(ns):
                for r in range(3):
                    part(a, r).start()

        own_half = ((grp >= 3) & (grp <= 5)) | (grp == 7)
        land = jnp.where(grp == 7, 3, grp - 3)

        @pl.when(own_half)
        def _():
            to_sibling(0, land).wait_recv()
            sibling_piece(land).start()

        @pl.when(s < n_w)
        def _():
            g_ref[...] = _dot(xt_ref[tab_ref[1, s]], dpc_ref[...])

        @pl.when(own_half)
        def _():
            sibling_piece(land).wait()

        for g in range(2 * N_CHIPS):
            @pl.when(grp == g)
            def _(g=g):
                if g in (0, 1, 2, 6):
                    use = s if g < 3 else 3 * nb + n
                    slot = use % 2

                    @pl.when(use >= 2)
                    def _():
                        to_sibling(slot, 0).wait_send()

                    st_ref[slot] = g_ref[...]
                    to_sibling(slot, min(g, 3)).start()
                elif g in (3, 4, 5):
                    sb_ref[...] = (g_ref[...] + tmp_ref[...]).astype(BF16)
                    stage = pltpu.make_async_copy(sb_ref, wire_ref.at[g - 3, :, cols], wire_sem)
                    stage.start()
                    stage.wait()
                    to_chip(g - 3).start()
                else:
                    slot = n % 2
                    piece = g_ref[...] + tmp_ref[...]

                    @pl.when(n < 2)
                    def _():
                        to_sibling(slot, 0).wait_send()

                    @pl.when(n >= 2)
                    def _():
                        to_result(slot).wait()

                    st_ref[slot] = piece
                    to_result(slot).start()

        @pl.when(s >= n_w)
        def _():
            acc = ALPHA * dr_ref[...]
            for j in range(N_CHIPS):
                acc = acc + _dot_nt(dpr_ref[:, j * W_BLOCK:(j + 1) * W_BLOCK], w_ref[j])
            dx_ref[...] = acc

        @pl.when(s == n_w + n_x - 1)
        def _():
            for slot in range(2):
                to_result(slot).wait()
            for r in range(3):
                all_of_chip(r).wait_recv()
                all_of_chip(r).wait_send()
            for a in range(ns):
                for r in range(3):
                    part(a, r).wait_recv()
                    part(a, r).wait_send()

    any_spec = pl.BlockSpec(memory_space=pl.ANY)
    tile = lambda s, tab: (jnp.maximum(s - n_w, 0), 0)
    return pl.pallas_call(
        body, name="grad_in_and_x",
        grid_spec=pltpu.PrefetchScalarGridSpec(
            num_scalar_prefetch=1, grid=(n_w + n_x,),
            in_specs=[pl.BlockSpec((2, half, t), lambda s, tab: (0, 0, 0), pipeline_mode=pl.Buffered(1)),
                      pl.BlockSpec((t, tn), lambda s, tab: (0, tab[0, jnp.minimum(s, n_w - 1)])),
                      pl.BlockSpec((tm, D_IN), tile),
                      pl.BlockSpec((N_CHIPS, D, W_BLOCK), lambda s, tab: (0, 0, 0), pipeline_mode=pl.Buffered(1)),
                      pl.BlockSpec((tm, D), tile)] + [any_spec] * ns,
            out_specs=[pl.BlockSpec((tm, D), tile)] + [any_spec] * (4 + ns),
            scratch_shapes=[pltpu.VMEM((half, tn), F32), pltpu.VMEM((2, half, tn), F32), pltpu.VMEM((half, tn), BF16),
                            pltpu.VMEM((half, tn), F32),
                            pltpu.SemaphoreType.DMA((2,)), pltpu.SemaphoreType.DMA((N_CHIPS * nb,)),
                            pltpu.SemaphoreType.DMA((3,)), pltpu.SemaphoreType.DMA((3,)),
                            pltpu.SemaphoreType.DMA((2,)), pltpu.SemaphoreType.DMA, pltpu.SemaphoreType.DMA,
                            pltpu.SemaphoreType.DMA((3 * ns,)), pltpu.SemaphoreType.DMA((3 * ns,))]),
        out_shape=[jax.ShapeDtypeStruct((t, D), F32), jax.ShapeDtypeStruct((half, W_BLOCK), F32),
                   jax.ShapeDtypeStruct((3, half, W_BLOCK), BF16), jax.ShapeDtypeStruct((N_CHIPS, half, W_BLOCK), F32),
                   jax.ShapeDtypeStruct((3, half, W_BLOCK), BF16)]
        + [jax.ShapeDtypeStruct((3,) + q.shape[1:], q.dtype) for q in chip_parts],
        compiler_params=_params("arbitrary"),
    )(table, xt, dp, dp, w4, dr, *chip_parts)


def _grad_w_square(a, b):
    t = a.shape[0]
    tk = 512
    rows = D // 8

    def body(a_ref, b_ref, o_ref, acc_ref):
        s = pl.program_id(0)

        @pl.when(s == 0)
        def _():
            acc_ref[...] = jnp.zeros_like(acc_ref)

        acc_ref[...] += _dot_tn(a_ref[...], b_ref[...])

        @pl.when(s == pl.num_programs(0) - 1)
        def _():
            for j in range(N_CHIPS):
                for h in range(2):
                    o_ref[h, j] = acc_ref[(2 * j + h) * rows:(2 * j + h + 1) * rows, :]

    return pl.pallas_call(
        body, name="grad_w_square", grid=(t // tk,),
        in_specs=[pl.BlockSpec((tk, D), lambda s: (s, 0)), pl.BlockSpec((tk, D), lambda s: (s, 0))],
        out_specs=_full((2, N_CHIPS, rows, D)),
        out_shape=jax.ShapeDtypeStruct((2, N_CHIPS, rows, D), F32),
        scratch_shapes=[pltpu.VMEM((D, D), F32)],
        compiler_params=_params("arbitrary"),
    )(a, b)


def kernel(x, w_in, b_in, conv_w, conv_b, gn_g, gn_b, ln_v_g, ln_v_b, w_spatial, b_spatial, w_pa, w_pb, w_o, b_o, ln_out_g, ln_out_b, loss_target, m_w_in, m_b_in, m_conv_w, m_conv_b, m_gn_g, m_gn_b, m_ln_v_g, m_ln_v_b, m_w_spatial, m_b_spatial, m_w_pa, m_w_pb, m_w_o, m_b_o, m_ln_out_g, m_ln_out_b, v_w_in, v_b_in, v_conv_w, v_conv_b, v_gn_g, v_gn_b, v_ln_v_g, v_ln_v_b, v_w_spatial, v_b_spatial, v_w_pa, v_w_pb, v_w_o, v_b_o, v_ln_out_g, v_ln_out_b):
    n_seq, seq, _ = x.shape
    t = n_seq * seq
    tiles_per_seq = seq // TOKEN_TILE
    x2 = x.reshape(t, D)
    tgt = loss_target.reshape(t, D)

    conv_shard = jnp.pad(conv_w, ((0, HALO - CONV_K), (0, 0)))
    p, win4, wpa4, wpb4, wo4, conv4 = _proj_gather(
        x2, b_in.reshape(N_CHIPS, 1, W_BLOCK),
        [_place_shard(w_in, 256, BF16), _place_shard(w_pa, 128, BF16), _place_shard(w_pb, 128, BF16),
         _place_shard(w_o, 128, BF16), _place_shard(conv_shard, HALO // 2, F32)])
    win4 = win4.reshape(N_CHIPS, D, W_BLOCK)
    wpa, wpb, wo = wpa4.reshape(D, D), wpb4.reshape(D, D), wo4.reshape(D, D)
    convw = conv4.reshape(N_CHIPS, HALO, D // N_CHIPS).transpose(1, 0, 2).reshape(HALO, D)

    vecs = jnp.stack([conv_b, gn_g, gn_b, ln_v_g, ln_v_b, b_o, ln_out_g, ln_out_b])
    causal = jnp.tril(jnp.ones((CHUNK, CHUNK), bool))
    ws = jnp.where(causal[None], w_spatial, 0.0)
    ws_bf, wst_bf = ws.astype(BF16), ws.transpose(0, 2, 1).astype(BF16)
    bsp = jnp.repeat(b_spatial.T, GROUP_W, axis=1)

    h1, ya, yb, h3, s, mixed, dr, drb, xt, acc_f = _forward_tiles(p, x2, tgt, wpa, wpb, wo, convw, vecs, ws_bf, bsp, tiles_per_seq)
    dh1, dp, dya, dyb, acc_b, dbin_b, dws, dbsp_acc = _backward_tiles(p, h1, ya, yb, drb, wpa, wpb, wo, vecs, ws_bf, wst_bf, bsp)
    dp, dcw8, dbin_a = _conv_backward(dh1, p, dp, convw, tiles_per_seq)
    g_pa, g_pb, g_o = _grad_w_square(h3, dya), _grad_w_square(s, dyb), _grad_w_square(mixed, drb)

    small, g_conv = _pack_small(acc_f, acc_b, dbin_a, dbin_b, dcw8, dws, dbsp_acc)
    small = small.reshape(2, N_CHIPS, SMALL_ROWS // 8, D)

    grads = [g_pa, g_pb, g_o, g_conv, small]
    wire = [BF16, BF16, BF16, F32, F32]
    from_sibling = _exchange_halves(grads)
    sums = [_add_halves(g, b, wd) for g, b, wd in zip(grads, from_sibling, wire)]
    grad_x, q_in, chips_in, _, _, *from_chips = _grad_in_and_x(
        xt.reshape(2, D // 2, t), dp, win4, dr, [qw for _, qw in sums])
    grad_x = grad_x.reshape(x.shape)
    mine = [_add_chips(q_in, chips_in, False)]
    mine += [_add_chips(q, b, a == 4) for a, ((q, _), b) in enumerate(zip(sums, from_chips))]
    *full, small_parts = _share_results(mine[:5], mine[5])
    grad_w_in, grad_w_pa, grad_w_pb, grad_w_o = [f.reshape(w.shape) for f, w in zip(full[:4], (w_in, w_pa, w_pb, w_o))]
    grad_conv_w = full[4].reshape(HALO, D // N_CHIPS)[:CONV_K]

    big = {}
    for name, w, g, m, v in [("w_in", w_in, grad_w_in, m_w_in, v_w_in), ("w_pa", w_pa, grad_w_pa, m_w_pa, v_w_pa),
                             ("w_pb", w_pb, grad_w_pb, m_w_pb, v_w_pb), ("w_o", w_o, grad_w_o, m_w_o, v_w_o),
                             ("conv_w", conv_w, grad_conv_w, m_conv_w, v_conv_w)]:
        big[name] = (g,) + tuple(_adamw(w, g, m, v))
    vec_names = ["conv_b", "gn_g", "gn_b", "ln_v_g", "ln_v_b", "b_o", "ln_out_g", "ln_out_b"]
    vec_triples = [(conv_b, m_conv_b, v_conv_b), (gn_g, m_gn_g, v_gn_g), (gn_b, m_gn_b, v_gn_b),
                   (ln_v_g, m_ln_v_g, v_ln_v_g), (ln_v_b, m_ln_v_b, v_ln_v_b), (b_o, m_b_o, v_b_o),
                   (ln_out_g, m_ln_out_g, v_ln_out_g), (ln_out_b, m_ln_out_b, v_ln_out_b)]
    small_res, loss8 = _adamw_small(
        small_parts.reshape(8, SMALL_ROWS // 8, D), vec_triples, (b_in, m_b_in, v_b_in),
        (w_spatial, m_w_spatial, v_w_spatial), (b_spatial, m_b_spatial, v_b_spatial))
    per_name = dict(zip(vec_names + ["b_in", "w_spatial", "b_spatial"], small_res))

    order = ["w_in", "b_in", "conv_w", "conv_b", "gn_g", "gn_b", "ln_v_g", "ln_v_b", "w_spatial", "b_spatial",
             "w_pa", "w_pb", "w_o", "b_o", "ln_out_g", "ln_out_b"]
    outs = [loss8[0, 0], grad_x]
    for kind in range(4):
        outs += [big[n][kind] if n in big else per_name[n][kind] for n in order]
    return tuple(outs)
```

```python
import functools
import math

import jax
import jax.numpy as jnp
from jax import lax
from jax.experimental import pallas as pl
from jax.experimental.pallas import tpu as pltpu

D = 1024
N_GROUPS = 8
GROUP_W = D // N_GROUPS
CHUNK = 128
CONV_K = 31
HALO = 32
D_IN = 8 * D
N_CHIPS = 4
W_BLOCK = D_IN // N_CHIPS
ALPHA = 2.0 ** 0.25
LN_EPS = 1e-5
ADAM_LR, ADAM_B1, ADAM_B2, ADAM_EPS, ADAM_WD, ADAM_STEP = 0.001, 0.9, 0.999, 1e-08, 0.01, 10

TOKEN_TILE = 256
VMEM_LIMIT = 56 * 1024 * 1024
MESH = pl.DeviceIdType.MESH
F32, BF16 = jnp.float32, jnp.bfloat16


def _sigmoid(x):
    return 1.0 / (1.0 + jnp.exp(-x))


def _gelu(x):
    c = math.sqrt(2.0 / math.pi)
    t = jnp.tanh(c * (x + 0.044715 * (x * x * x)))
    return x * (0.5 * (1.0 + t))


def _gelu_and_grad(x):
    c = math.sqrt(2.0 / math.pi)
    x2 = x * x
    t = jnp.tanh(c * (x + 0.044715 * (x2 * x)))
    cdf = 0.5 * (1.0 + t)
    return x * cdf, cdf + 0.5 * x * (1.0 - t * t) * (c * (1.0 + 3.0 * 0.044715 * x2))


def _norm_stats(v):
    mu = jnp.mean(v, axis=-1, keepdims=True)
    vc = v - mu
    var = jnp.mean(vc * vc, axis=-1, keepdims=True)
    rstd = lax.rsqrt(var + LN_EPS)
    return vc * rstd, rstd


def _norm_bwd(dxhat, xhat, rstd):
    m1 = jnp.mean(dxhat, axis=-1, keepdims=True)
    m2 = jnp.mean(dxhat * xhat, axis=-1, keepdims=True)
    return rstd * (dxhat - m1 - xhat * m2)


def _dot(a, b):
    return jnp.dot(a, b, preferred_element_type=F32)


def _dot_nt(a, b):
    return lax.dot_general(a, b, (((1,), (1,)), ((), ())), preferred_element_type=F32)


def _dot_tn(a, b):
    return lax.dot_general(a, b, (((0,), (0,)), ((), ())), preferred_element_type=F32)


def _colsum(v):
    return jnp.sum(v, axis=0, keepdims=True)


def _full(shape):
    return pl.BlockSpec(shape, lambda *_: (0,) * len(shape))


def _resident(shape):
    return pl.BlockSpec(shape, lambda *_: (0,) * len(shape), pipeline_mode=pl.Buffered(1))


def _params(*sem):
    return pltpu.CompilerParams(dimension_semantics=sem, vmem_limit_bytes=VMEM_LIMIT)


def _chip_index():
    return (2 * lax.axis_index("x") + lax.axis_index("y")).astype(jnp.int32).reshape(1)


def _core_index():
    return lax.axis_index("c").astype(jnp.int32).reshape(1)


def _place_shard(w, rows, dtype):
    r, c = w.shape
    steps = r // 2 // rows

    def body(k_ref, w_ref, o_ref):
        o_ref[...] = w_ref[...].astype(dtype)

    return pl.pallas_call(
        body, name="place_shard",
        grid_spec=pltpu.PrefetchScalarGridSpec(
            num_scalar_prefetch=1, grid=(2, steps),
            in_specs=[pl.BlockSpec((rows, c), lambda h, i, k: (h * steps + i, 0))],
            out_specs=pl.BlockSpec((None, None, rows, c), lambda h, i, k: (k[0], h, i, 0))),
        out_shape=jax.ShapeDtypeStruct((N_CHIPS, 2, r // 2, c), dtype),
        compiler_params=_params("parallel", "parallel"),
    )(_chip_index(), w)


def _position():
    x, y, c = lax.axis_index("x"), lax.axis_index("y"), lax.axis_index("c")
    return x, y, c, 2 * x + y


def _other_chips(x, y):
    return [(1 - x, y), (x, 1 - y), (1 - x, 1 - y)]


def _any_specs(n):
    return [pl.BlockSpec(memory_space=pl.ANY)] * n


def _proj_gather(x, b4, bufs):
    t = x.shape[0]
    tm = 1024
    steps = t // tm
    half = D // 2
    n = len(bufs)
    xi, yi = lax.axis_index("x"), lax.axis_index("y")
    order = jnp.stack([2 * xi + yi, 2 * (1 - xi) + yi, 2 * xi + (1 - yi), 2 * (1 - xi) + (1 - yi)]).astype(jnp.int32)

    def body(order_ref, x_ref, b_ref, *refs):
        p_ref, outs = refs[n], refs[n + 1:2 * n + 1]
        xb_ref, w_ref, lsem, send, recv, fsend, frecv = refs[2 * n + 1:]
        jj, i = pl.program_id(0), pl.program_id(1)
        x_, y_, c, k = _position()
        chips = _other_chips(x_, y_)

        def sent(a, r):
            cx, cy = chips[r]
            return pltpu.make_async_remote_copy(
                src_ref=outs[a].at[k, c], dst_ref=outs[a].at[k, c], send_sem=send.at[3 * a + r],
                recv_sem=recv.at[3 * a + r], device_id=(cx, cy, c), device_id_type=MESH)

        def landed(a, r):
            cx, cy = chips[r]
            return pltpu.make_async_remote_copy(
                src_ref=outs[a].at[2 * cx + cy, c], dst_ref=outs[a].at[2 * cx + cy, c], send_sem=send.at[3 * a + r],
                recv_sem=recv.at[3 * a + r], device_id=(cx, cy, c), device_id_type=MESH)

        def passed(a, r, h):
            cx, cy = chips[r]
            return pltpu.make_async_remote_copy(
                src_ref=outs[a].at[2 * cx + cy, h], dst_ref=outs[a].at[2 * cx + cy, h], send_sem=fsend.at[3 * a + r],
                recv_sem=frecv.at[3 * a + r], device_id=(x_, y_, 1 - c), device_id_type=MESH)

        def load(block):
            cps = [pltpu.make_async_copy(outs[0].at[block, h], w_ref.at[pl.ds(h * half, half)], lsem.at[h])
                   for h in range(2)]
            for cp in cps:
                cp.start()
            for cp in cps:
                cp.wait()

        @pl.when((jj == 0) & (i == 0))
        def _():
            for a in range(n):
                for r in range(3):
                    sent(a, r).start()
            load(k)

        for r in range(3):
            @pl.when((jj == r + 1) & (i == 0))
            def _(r=r):
                landed(0, r).wait_recv()
                passed(0, r, c).start()
                passed(0, r, 1 - c).wait_recv()
                load(2 * chips[r][0] + chips[r][1])

        rows = pl.ds(pl.multiple_of(i * tm, tm), tm)

        @pl.when(jj == 0)
        def _():
            xb_ref[rows, :] = x_ref[...].astype(BF16)

        p_ref[...] = _dot(xb_ref[rows, :], w_ref[...]) + b_ref[...]

        @pl.when((jj == N_CHIPS - 1) & (i == steps - 1))
        def _():
            for a in range(1, n):
                for r in range(3):
                    landed(a, r).wait_recv()
                    passed(a, r, c).start()
            for a in range(1, n):
                for r in range(3):
                    passed(a, r, 1 - c).wait_recv()
            for a in range(n):
                for r in range(3):
                    sent(a, r).wait_send()
                    passed(a, r, c).wait_send()

    any_spec = pl.BlockSpec(memory_space=pl.ANY)
    return pl.pallas_call(
        body, name="proj_gather",
        grid_spec=pltpu.PrefetchScalarGridSpec(
            num_scalar_prefetch=1, grid=(N_CHIPS, steps),
            in_specs=[pl.BlockSpec((tm, D), lambda jj, i, o: (jnp.where(jj == 0, i, steps - 1), 0)),
                      pl.BlockSpec((None, 1, W_BLOCK), lambda jj, i, o: (o[jj], 0, 0))] + [any_spec] * n,
            out_specs=[pl.BlockSpec((tm, W_BLOCK), lambda jj, i, o: (i, o[jj]))] + [any_spec] * n,
            scratch_shapes=[pltpu.VMEM((t, D), BF16), pltpu.VMEM((D, W_BLOCK), BF16), pltpu.SemaphoreType.DMA((2,)),
                            pltpu.SemaphoreType.DMA((3 * n,)), pltpu.SemaphoreType.DMA((3 * n,)),
                            pltpu.SemaphoreType.DMA((3 * n,)), pltpu.SemaphoreType.DMA((3 * n,))]),
        out_shape=[jax.ShapeDtypeStruct((t, D_IN), F32)] + [jax.ShapeDtypeStruct(b.shape, b.dtype) for b in bufs],
        input_output_aliases={3 + a: 1 + a for a in range(n)},
        compiler_params=_params("arbitrary", "arbitrary"),
    )(order, x, b4, *bufs)


def _exchange_halves(grads):
    n = len(grads)

    def body(*refs):
        ins, outs = refs[:n], refs[n:2 * n]
        send, recv = refs[2 * n:]
        x, y, c, _ = _position()
        cps = []
        for a in range(n):
            cp = pltpu.make_async_remote_copy(
                src_ref=ins[a].at[1 - c], dst_ref=outs[a], send_sem=send.at[a], recv_sem=recv.at[a],
                device_id=(x, y, 1 - c), device_id_type=MESH)
            cp.start()
            cps.append(cp)
        for cp in cps:
            cp.wait_recv()
        for cp in cps:
            cp.wait_send()

    return pl.pallas_call(
        body, name="rs_exchange_halves",
        in_specs=_any_specs(n), out_specs=_any_specs(n),
        out_shape=[jax.ShapeDtypeStruct(g.shape[1:], g.dtype) for g in grads],
        scratch_shapes=[pltpu.SemaphoreType.DMA((n,)), pltpu.SemaphoreType.DMA((n,))],
    )(*grads)


def _share_results(bufs, small):
    n = len(bufs)

    def body(*refs):
        outs, small_out = refs[n + 1:2 * n + 1], refs[2 * n + 1]
        send, recv, ssend, srecv = refs[2 * n + 2:]
        x, y, c, k = _position()
        cps = []
        for a in range(n):
            cp = pltpu.make_async_remote_copy(
                src_ref=outs[a].at[c], dst_ref=outs[a].at[c], send_sem=send.at[a], recv_sem=recv.at[a],
                device_id=(x, y, 1 - c), device_id_type=MESH)
            cp.start()
            cps.append(cp)
        waits = []
        for p in range(1, 8):
            px, py, pc = x ^ (p >> 2), y ^ ((p >> 1) & 1), c ^ (p & 1)
            cp = pltpu.make_async_remote_copy(
                src_ref=small_out.at[k, c], dst_ref=small_out.at[k, c], send_sem=ssend.at[p - 1],
                recv_sem=srecv.at[p - 1], device_id=(px, py, pc), device_id_type=MESH)
            cp.start()
            cps.append(cp)
            waits.append(pltpu.make_async_remote_copy(
                src_ref=small_out.at[2 * px + py, pc], dst_ref=small_out.at[2 * px + py, pc], send_sem=ssend.at[p - 1],
                recv_sem=srecv.at[p - 1], device_id=(px, py, pc), device_id_type=MESH))
        for a in range(n):
            pltpu.make_async_remote_copy(
                src_ref=outs[a].at[1 - c], dst_ref=outs[a].at[1 - c], send_sem=send.at[a], recv_sem=recv.at[a],
                device_id=(x, y, 1 - c), device_id_type=MESH).wait_recv()
        for w in waits:
            w.wait_recv()
        for cp in cps:
            cp.wait_send()

    return pl.pallas_call(
        body, name="rs_share_results",
        in_specs=_any_specs(n + 1), out_specs=_any_specs(n + 1),
        out_shape=[jax.ShapeDtypeStruct(b.shape, b.dtype) for b in bufs + [small]],
        scratch_shapes=[pltpu.SemaphoreType.DMA((n,)), pltpu.SemaphoreType.DMA((n,)),
                        pltpu.SemaphoreType.DMA((7,)), pltpu.SemaphoreType.DMA((7,))],
        input_output_aliases={a: a for a in range(n + 1)},
    )(*bufs, small)


def _row_tile(r, c):
    t = max(8, min(r, (1 << 18) // c))
    while r % t:
        t //= 2
    return t


def _add_halves(g, b1, wire_dtype):
    _, _, r, c = g.shape
    t = _row_tile(r, c)

    def body(c_ref, g_ref, b_ref, q_ref, qw_ref):
        q = g_ref[...] + b_ref[...]
        q_ref[...] = q
        qw_ref[...] = q.astype(wire_dtype)

    core = _core_index()
    return pl.pallas_call(
        body, name="rs_add_halves",
        grid_spec=pltpu.PrefetchScalarGridSpec(
            num_scalar_prefetch=1, grid=(N_CHIPS, r // t),
            in_specs=[pl.BlockSpec((None, None, t, c), lambda j, i, cr: (cr[0], j, i, 0)),
                      pl.BlockSpec((None, t, c), lambda j, i, cr: (j, i, 0))],
            out_specs=[pl.BlockSpec((None, t, c), lambda j, i, cr: (j, i, 0)),
                       pl.BlockSpec((None, t, c), lambda j, i, cr: (j, i, 0))]),
        out_shape=[jax.ShapeDtypeStruct((N_CHIPS, r, c), F32), jax.ShapeDtypeStruct((N_CHIPS, r, c), wire_dtype)],
        compiler_params=_params("parallel", "parallel"),
    )(core, g, b1)


def _add_chips(q, b2, per_device):
    r, c = q.shape[-2:]
    t = _row_tile(r, c)

    def body(kc_ref, q_ref, b_ref, f_ref):
        f_ref[...] = ((q_ref[...] + b_ref[0].astype(F32)) + b_ref[1].astype(F32)) + b_ref[2].astype(F32)

    if per_device:
        out_spec = pl.BlockSpec((None, None, t, c), lambda i, kc: (kc[0], kc[1], i, 0))
        out_shape = jax.ShapeDtypeStruct((N_CHIPS, 2, r, c), F32)
    else:
        out_spec = pl.BlockSpec((None, t, c), lambda i, kc: (kc[1], i, 0))
        out_shape = jax.ShapeDtypeStruct((2, r, c), F32)
    return pl.pallas_call(
        body, name="rs_add_chips",
        grid_spec=pltpu.PrefetchScalarGridSpec(
            num_scalar_prefetch=1, grid=(r // t,),
            in_specs=[pl.BlockSpec((None, t, c), lambda i, kc: (kc[0], i, 0)) if q.ndim == 3
                      else pl.BlockSpec((t, c), lambda i, kc: (i, 0)),
                      pl.BlockSpec((3, t, c), lambda i, kc: (0, i, 0))],
            out_specs=out_spec),
        out_shape=out_shape,
        compiler_params=_params("parallel"),
    )(jnp.concatenate([_chip_index(), _core_index()]), q, b2)


def _adamw_math(w, g, m, v):
    m = ADAM_B1 * m + (1.0 - ADAM_B1) * g
    v = ADAM_B2 * v + (1.0 - ADAM_B2) * (g * g)
    m_hat = m / (1.0 - ADAM_B1 ** ADAM_STEP)
    v_hat = v / (1.0 - ADAM_B2 ** ADAM_STEP)
    delta = -ADAM_LR * (m_hat / (jnp.sqrt(v_hat) + ADAM_EPS) + ADAM_WD * w)
    return delta, m, v


def _adamw(w, g, m, v):
    r, c = w.shape
    t = _row_tile(r, c) if r % 8 == 0 else r

    def body(w_ref, g_ref, m_ref, v_ref, d_ref, nm_ref, nv_ref):
        d_ref[...], nm_ref[...], nv_ref[...] = _adamw_math(w_ref[...], g_ref[...], m_ref[...], v_ref[...])

    spec = pl.BlockSpec((t, c), lambda i: (i, 0))
    return pl.pallas_call(
        body, name="adamw", grid=(r // t,), in_specs=[spec] * 4, out_specs=[spec] * 3,
        out_shape=[jax.ShapeDtypeStruct((r, c), F32)] * 3, compiler_params=_params("parallel"),
    )(w, g, m, v)


ROW_B_IN = 0
ROW_VECS = 8
ROW_LOSS = 16
ROW_B_SPATIAL = 24
ROW_W_SPATIAL = 32
SMALL_ROWS = 192
N_VECS = 8


def _pack_small(acc_f, acc_b, dbin_a, dbin_b, dcw8, dws, dbsp):
    cols = D // N_CHIPS

    def body(af_ref, ab_ref, da_ref, db_ref, cw_ref, ws_ref, bs_ref, o_ref, gc_ref):
        o_ref[...] = jnp.zeros_like(o_ref)
        for j in range(D_IN // D):
            src = da_ref if j < 2 else db_ref
            o_ref[ROW_B_IN + j:ROW_B_IN + j + 1, :] = src[0:1, j * D:(j + 1) * D]
        dcw = jnp.sum(cw_ref[...].reshape(HALO, SUBLANES, D), axis=1)
        o_ref[ROW_VECS:ROW_VECS + 1, :] = dcw[CONV_K:CONV_K + 1]
        o_ref[ROW_VECS + 1:ROW_VECS + 5, :] = ab_ref[0:4, :]
        o_ref[ROW_VECS + 5:ROW_VECS + 6, :] = af_ref[2:3, :]
        o_ref[ROW_VECS + 6:ROW_VECS + 8, :] = af_ref[0:2, :]
        o_ref[ROW_LOSS:ROW_LOSS + 1, :] = af_ref[3:4, :]
        head = lax.broadcasted_iota(jnp.int32, (N_GROUPS, D), 0)
        lane = lax.broadcasted_iota(jnp.int32, (N_GROUPS, D), 1)
        indicator = jnp.where(lane // GROUP_W == head, 1.0, 0.0)
        o_ref[ROW_B_SPATIAL:ROW_B_SPATIAL + N_GROUPS, 0:CHUNK] = lax.dot_general(
            indicator, bs_ref[...], (((1,), (1,)), ((), ())), precision=lax.Precision.HIGHEST, preferred_element_type=F32)
        t_idx = lax.broadcasted_iota(jnp.int32, (CHUNK, D), 0)
        s_idx = lax.broadcasted_iota(jnp.int32, (CHUNK, D), 1) % CHUNK
        o_ref[ROW_W_SPATIAL:ROW_W_SPATIAL + CHUNK, :] = jnp.where(s_idx <= t_idx, ws_ref[...], 0.0)
        for h in range(2):
            for j in range(N_CHIPS):
                gc_ref[h, j] = dcw[h * (HALO // 2):(h + 1) * (HALO // 2), j * cols:(j + 1) * cols]

    ins = [acc_f, acc_b, dbin_a, dbin_b, dcw8, dws, dbsp]
    return pl.pallas_call(
        body, name="pack_small",
        in_specs=[_full(a.shape) for a in ins],
        out_specs=[_full((SMALL_ROWS, D)), _full((2, N_CHIPS, HALO // 2, cols))],
        out_shape=[jax.ShapeDtypeStruct((SMALL_ROWS, D), F32), jax.ShapeDtypeStruct((2, N_CHIPS, HALO // 2, cols), F32)],
        compiler_params=_params(),
    )(*ins)


def _adamw_small(parts, vecs, b_in, w_spatial, b_spatial):
    triples = list(vecs) + [b_in, w_spatial, b_spatial]
    n_in = 3 * len(triples)

    def body(p_ref, *refs):
        ins = [refs[3 * i:3 * i + 3] for i in range(len(triples))]
        outs = [refs[n_in + 4 * i:n_in + 4 * i + 4] for i in range(len(triples))]
        loss_ref, g_ref = refs[n_in + 4 * len(triples):]
        rows = SMALL_ROWS // 8
        for k in range(N_CHIPS):
            for core in range(2):
                g_ref[(core * N_CHIPS + k) * rows:(core * N_CHIPS + k + 1) * rows, :] = p_ref[2 * k + core]

        def step(g, wmv, out, get, put):
            d, nm, nv = _adamw_math(get(wmv[0]), g, get(wmv[1]), get(wmv[2]))
            for o, val in zip(out, (g, d, nm, nv)):
                put(o, val)

        for i in range(N_VECS):
            step(g_ref[ROW_VECS + i:ROW_VECS + i + 1, :], ins[i], outs[i],
                 lambda r: r[...].reshape(1, D), lambda o, val: o.__setitem__(Ellipsis, val.reshape(D)))
        for j in range(D_IN // D):
            piece = pl.ds(j * D, D)
            step(g_ref[ROW_B_IN + j:ROW_B_IN + j + 1, :], ins[N_VECS], outs[N_VECS],
                 lambda r: r[piece].reshape(1, D), lambda o, val: o.__setitem__(piece, val.reshape(D)))
        for h in range(N_GROUPS):
            step(g_ref[ROW_W_SPATIAL:ROW_W_SPATIAL + CHUNK, h * CHUNK:(h + 1) * CHUNK], ins[N_VECS + 1], outs[N_VECS + 1],
                 lambda r: r[h], lambda o, val: o.__setitem__(h, val))
        step(g_ref[ROW_B_SPATIAL:ROW_B_SPATIAL + N_GROUPS, 0:CHUNK], ins[N_VECS + 2], outs[N_VECS + 2],
             lambda r: r[...], lambda o, val: o.__setitem__(Ellipsis, val))
        lanes = g_ref[ROW_LOSS:ROW_LOSS + 1, :]
        loss_ref[...] = jnp.broadcast_to(jnp.sum(lanes, axis=1, keepdims=True), (8, 128))

    flat = [a for tr in triples for a in tr]
    out_shape = [jax.ShapeDtypeStruct(tr[0].shape, F32) for tr in triples for _ in range(4)]
    out_shape.append(jax.ShapeDtypeStruct((8, 128), F32))
    res = pl.pallas_call(
        body, name="adamw_small",
        in_specs=[_full(parts.shape)] + [_full(a.shape) for a in flat],
        out_specs=[_full(o.shape) for o in out_shape],
        out_shape=out_shape,
        scratch_shapes=[pltpu.VMEM((SMALL_ROWS, D), F32)],
        compiler_params=_params(),
    )(parts, *flat)
    return [res[4 * i:4 * i + 4] for i in range(len(triples))], res[-1]


SUBLANES = 8
SHIFT_ROWS = HALO - SUBLANES


def _shifted_copies(src_ref, sh_ref, cs, tm):
    for p in range(1, SUBLANES):
        sh_ref[p - 1] = src_ref[pl.ds(p, tm + SHIFT_ROWS), cs]


def _tap(src_ref, sh_ref, cs, offset, start, rows):
    p, q = offset % SUBLANES, offset // SUBLANES
    if p == 0:
        return src_ref[pl.ds(start + SUBLANES * q, rows), cs]
    return sh_ref[p - 1, pl.ds(start + SUBLANES * q, rows), :]


def _conv_taps(src_ref, sh_ref, w_ref, first_offset, step, bias, dst_ref, tm):
    rows = 64
    for g in range(N_GROUPS):
        cs = slice(g * GROUP_W, (g + 1) * GROUP_W)
        _shifted_copies(src_ref, sh_ref, cs, tm)
        for rb in range(tm // rows):
            acc = jnp.zeros((rows, GROUP_W), F32) + (bias[:, cs] if bias is not None else 0.0)
            for k in range(CONV_K):
                acc = acc + w_ref[k:k + 1, cs] * _tap(src_ref, sh_ref, cs, first_offset + step * k, rb * rows, rows)
            dst_ref[rb * rows:(rb + 1) * rows, cs] = acc


def _conv_weight_grad(d_ref, src_ref, sh_ref, first_offset, acc_ref, tm):
    rows = 64
    for g in range(N_GROUPS):
        cs = slice(g * GROUP_W, (g + 1) * GROUP_W)
        _shifted_copies(src_ref, sh_ref, cs, tm)
        for rb in range(tm // rows):
            d = d_ref[rb * rows:(rb + 1) * rows, cs]
            for k in range(CONV_K):
                prod = d * _tap(src_ref, sh_ref, cs, first_offset + k, rb * rows, rows)
                acc_ref[SUBLANES * k:SUBLANES * (k + 1), cs] += jnp.sum(
                    prod.reshape(rows // SUBLANES, SUBLANES, GROUP_W), axis=0)


def _spatial_mix(w_ref, v_bf, tm):
    rows = []
    for q in range(tm // CHUNK):
        cols = [_dot(w_ref[h], v_bf[q * CHUNK:(q + 1) * CHUNK, h * GROUP_W:(h + 1) * GROUP_W])
                for h in range(N_GROUPS)]
        rows.append(jnp.concatenate(cols, axis=1))
    return jnp.concatenate(rows, axis=0)


def _group_norm_fwd(h1, gn_g, gn_b):
    xhat, rstd = [], []
    for g in range(N_GROUPS):
        xh, rs = _norm_stats(h1[:, g * GROUP_W:(g + 1) * GROUP_W])
        xhat.append(xh)
        rstd.append(rs)
    xhat = jnp.concatenate(xhat, axis=1)
    return xhat * gn_g + gn_b, xhat, rstd


def _forward_tiles(p, x, tgt, wpa, wpb, wo, convw, vecs, ws, bsp, tiles_per_seq):
    t = x.shape[0]
    tm = TOKEN_TILE
    hb = tm // HALO

    def body(p_ref, ph_ref, x_ref, t_ref, wpa_ref, wpb_ref, wo_ref, cw_ref, vec_ref, ws_ref, bsp_ref,
             h1_ref, ya_ref, yb_ref, h3_ref, s_ref, mx_ref, dr_ref, drb_ref, xt_ref, acc_ref, he_ref, sh_ref):
        i = pl.program_id(0)
        xt_ref[...] = x_ref[...].T.astype(BF16)
        conv_b, gn_g, gn_b, lnv_g, lnv_b, b_o, lno_g, lno_b = [vec_ref[j:j + 1, :] for j in range(8)]

        keep = jnp.where(i % tiles_per_seq == 0, 0.0, 1.0)
        he_ref[0:HALO, :] = ph_ref[:, 0:D] * _sigmoid(ph_ref[:, D:2 * D]) * keep
        he_ref[HALO:, :] = p_ref[:, 0:D] * _sigmoid(p_ref[:, D:2 * D])
        _conv_taps(he_ref, sh_ref, cw_ref, HALO - (CONV_K - 1), 1, conv_b, h1_ref, tm)
        h2, _, _ = _group_norm_fwd(h1_ref[...], gn_g, gn_b)
        a_gate = p_ref[:, 2 * D:3 * D]
        h3 = ((h2 * _sigmoid(h2)) * (a_gate * _sigmoid(a_gate))).astype(BF16)
        h3_ref[...] = h3
        ya = _dot(h3, wpa_ref[...])
        ya_ref[...] = ya

        u = _gelu(p_ref[:, 3 * D:4 * D])
        vhat, _ = _norm_stats(_gelu(p_ref[:, 4 * D:5 * D]))
        v1 = (vhat * lnv_g + lnv_b).astype(BF16)
        b_gate = p_ref[:, 5 * D:6 * D]
        vmix = _spatial_mix(ws_ref, v1, tm) + jnp.concatenate([bsp_ref[...]] * (tm // CHUNK), axis=0)
        s = (u * vmix * (b_gate * _sigmoid(b_gate))).astype(BF16)
        s_ref[...] = s
        yb = _dot(s, wpb_ref[...])
        yb_ref[...] = yb

        mixed = (_sigmoid(p_ref[:, 6 * D:7 * D]) * ya + _sigmoid(p_ref[:, 7 * D:8 * D]) * yb).astype(BF16)
        mx_ref[...] = mixed
        r = ALPHA * x_ref[...] + (_dot(mixed, wo_ref[...]) + b_o)
        xhat, rstd = _norm_stats(r)
        err = (xhat * lno_g + lno_b) - t_ref[...]
        dout = err * (1.0 / D)
        dr = _norm_bwd(dout * lno_g, xhat, rstd)
        dr_ref[...] = dr
        drb_ref[...] = dr.astype(BF16)

        @pl.when(i == 0)
        def _():
            acc_ref[...] = jnp.zeros_like(acc_ref)

        acc_ref[0:1, :] += _colsum(dout * xhat)
        acc_ref[1:2, :] += _colsum(dout)
        acc_ref[2:3, :] += _colsum(dr)
        acc_ref[3:4, :] += _colsum(err * err) * (0.5 / D)

    tile = lambda w: pl.BlockSpec((tm, w), lambda i: (i, 0))
    f32_out = jax.ShapeDtypeStruct((t, D), F32)
    bf_out = jax.ShapeDtypeStruct((t, D), BF16)
    return pl.pallas_call(
        body, name="forward_tiles", grid=(t // tm,),
        in_specs=[tile(D_IN),
                  pl.BlockSpec((HALO, 2 * D), lambda i: (jnp.maximum(i * hb - 1, 0), 0)),
                  tile(D), tile(D), _resident((D, D)), _resident((D, D)), _resident((D, D)), _full((HALO, D)), _full((8, D)),
                  _full((N_GROUPS, CHUNK, CHUNK)), _full((CHUNK, D))],
        out_specs=[tile(D)] * 8 + [pl.BlockSpec((D, tm), lambda i: (0, i)), _full((8, D))],
        out_shape=[f32_out, f32_out, f32_out, bf_out, bf_out, bf_out, f32_out, bf_out,
                   jax.ShapeDtypeStruct((D, t), BF16), jax.ShapeDtypeStruct((8, D), F32)],
        scratch_shapes=[pltpu.VMEM((tm + HALO, D), F32), pltpu.VMEM((SUBLANES - 1, tm + SHIFT_ROWS, GROUP_W), F32)],
        compiler_params=_params("arbitrary"),
    )(p, p, x, tgt, wpa, wpb, wo, convw, vecs, ws, bsp)


def _backward_tiles(p, h1, ya, yb, drb, wpa, wpb, wo, vecs, ws, wst, bsp):
    t = h1.shape[0]
    tm = TOKEN_TILE

    def body(p_ref, h1_ref, ya_ref, yb_ref, drb_ref, wpa_ref, wpb_ref, wo_ref, vec_ref, ws_ref, wst_ref, bsp_ref,
             dh1_ref, dp_ref, dya_ref, dyb_ref, acc_ref, dbin_ref, dws_ref, dbsp_ref):
        i = pl.program_id(0)
        _, gn_g, gn_b, lnv_g, lnv_b = [vec_ref[j:j + 1, :] for j in range(5)]

        @pl.when(i == 0)
        def _():
            acc_ref[...] = jnp.zeros_like(acc_ref)
            dbin_ref[...] = jnp.zeros_like(dbin_ref)
            dws_ref[...] = jnp.zeros_like(dws_ref)
            dbsp_ref[...] = jnp.zeros_like(dbsp_ref)

        def emit(block, val):
            dbin_ref[0:1, block * D:(block + 1) * D] += _colsum(val)
            dp_ref[:, block * D:(block + 1) * D] = val.astype(BF16)

        dp_ref[:, 0:2 * D] = jnp.zeros((tm, 2 * D), BF16)
        dmixed = _dot_nt(drb_ref[...], wo_ref[...])
        ga = _sigmoid(p_ref[:, 6 * D:7 * D])
        gb = _sigmoid(p_ref[:, 7 * D:8 * D])
        dya = (dmixed * ga).astype(BF16)
        dyb = (dmixed * gb).astype(BF16)
        dya_ref[...] = dya
        dyb_ref[...] = dyb
        emit(6, dmixed * ya_ref[...] * (ga * (1.0 - ga)))
        emit(7, dmixed * yb_ref[...] * (gb * (1.0 - gb)))

        dh3 = _dot_nt(dya, wpa_ref[...])
        h2, xhat, rstd = _group_norm_fwd(h1_ref[...], gn_g, gn_b)
        sg = _sigmoid(h2)
        a_gate = p_ref[:, 2 * D:3 * D]
        sa = _sigmoid(a_gate)
        dh2 = dh3 * (a_gate * sa) * (sg * (1.0 + h2 * (1.0 - sg)))
        emit(2, dh3 * (h2 * sg) * (sa * (1.0 + a_gate * (1.0 - sa))))
        acc_ref[0:1, :] += _colsum(dh2 * xhat)
        acc_ref[1:2, :] += _colsum(dh2)
        dxhat = dh2 * gn_g
        for g in range(N_GROUPS):
            cs = slice(g * GROUP_W, (g + 1) * GROUP_W)
            dh1_ref[:, cs] = _norm_bwd(dxhat[:, cs], xhat[:, cs], rstd[g])

        ds = _dot_nt(dyb, wpb_ref[...])
        u_pre = p_ref[:, 3 * D:4 * D]
        u, du_dpre = _gelu_and_grad(u_pre)
        v0, dv_dpre = _gelu_and_grad(p_ref[:, 4 * D:5 * D])
        vhat, vrstd = _norm_stats(v0)
        v1 = (vhat * lnv_g + lnv_b).astype(BF16)
        vmix = _spatial_mix(ws_ref, v1, tm) + jnp.concatenate([bsp_ref[...]] * (tm // CHUNK), axis=0)
        b_gate = p_ref[:, 5 * D:6 * D]
        sb = _sigmoid(b_gate)
        silu_b = b_gate * sb
        emit(3, ds * vmix * silu_b * du_dpre)
        emit(5, ds * u * vmix * (sb * (1.0 + b_gate * (1.0 - sb))))
        dvmix = ds * u * silu_b
        dvmix_bf = dvmix.astype(BF16)
        for q in range(tm // CHUNK):
            dbsp_ref[...] += dvmix[q * CHUNK:(q + 1) * CHUNK, :]
            for h in range(N_GROUPS):
                blk = (slice(q * CHUNK, (q + 1) * CHUNK), slice(h * GROUP_W, (h + 1) * GROUP_W))
                dws_ref[:, h * GROUP_W:(h + 1) * GROUP_W] += _dot_nt(dvmix_bf[blk], v1[blk])
        dv1 = _spatial_mix(wst_ref, dvmix_bf, tm)
        acc_ref[2:3, :] += _colsum(dv1 * vhat)
        acc_ref[3:4, :] += _colsum(dv1)
        emit(4, _norm_bwd(dv1 * lnv_g, vhat, vrstd) * dv_dpre)

    tile = lambda w: pl.BlockSpec((tm, w), lambda i: (i, 0))
    return pl.pallas_call(
        body, name="backward_tiles", grid=(t // tm,),
        in_specs=[tile(D_IN), tile(D), tile(D), tile(D), tile(D), _resident((D, D)), _resident((D, D)), _resident((D, D)),
                  _full((8, D)), _full((N_GROUPS, CHUNK, CHUNK)), _full((N_GROUPS, CHUNK, CHUNK)), _full((CHUNK, D))],
        out_specs=[tile(D), tile(D_IN), tile(D), tile(D), _full((8, D)), _full((8, D_IN)),
                   _full((CHUNK, D)), _full((CHUNK, D))],
        out_shape=[jax.ShapeDtypeStruct((t, D), F32), jax.ShapeDtypeStruct((t, D_IN), BF16),
                   jax.ShapeDtypeStruct((t, D), BF16), jax.ShapeDtypeStruct((t, D), BF16),
                   jax.ShapeDtypeStruct((8, D), F32), jax.ShapeDtypeStruct((8, D_IN), F32),
                   jax.ShapeDtypeStruct((CHUNK, D), F32), jax.ShapeDtypeStruct((CHUNK, D), F32)],
        compiler_params=_params("arbitrary"),
    )(p, h1, ya, yb, drb, wpa, wpb, wo, vecs, ws, wst, bsp)


def _conv_backward(dh1, p, dp, convw, pairs, tiles_per_seq):
    t = dh1.shape[0]
    tm = TOKEN_TILE
    hb = tm // HALO
    last = t // HALO - 1
    n_sq = len(pairs)
    span = 2
    rows = D // 8

    def body(dh1_ref, dnext_ref, p_ref, ph_ref, cw_ref, dp_in_ref, *refs):
        del dp_in_ref
        sq_in = refs[:2 * n_sq]
        dp_ref, dcw_ref, dbin_ref = refs[2 * n_sq:2 * n_sq + 3]
        sq_out = refs[2 * n_sq + 3:3 * n_sq + 3]
        de_ref, he_ref, dh0_ref, sh_ref, acc_ref, sq_sem = refs[3 * n_sq + 3:]
        i = pl.program_id(0)

        @pl.when(i == 0)
        def _():
            dcw_ref[...] = jnp.zeros_like(dcw_ref)
            dbin_ref[...] = jnp.zeros_like(dbin_ref)
            acc_ref[...] = jnp.zeros_like(acc_ref)

        @pl.when(i % span == span - 1)
        def _():
            for a in range(n_sq):
                acc_ref[a] += _dot_tn(sq_in[2 * a][...], sq_in[2 * a + 1][...])

        keep_next = jnp.where(i % tiles_per_seq == tiles_per_seq - 1, 0.0, 1.0)
        de_ref[0:tm, :] = dh1_ref[...]
        de_ref[tm:, :] = dnext_ref[...] * keep_next
        _conv_taps(de_ref, sh_ref, cw_ref, CONV_K - 1, -1, None, dh0_ref, tm)

        keep_prev = jnp.where(i % tiles_per_seq == 0, 0.0, 1.0)
        sg = _sigmoid(p_ref[:, D:2 * D])
        val = p_ref[:, 0:D]
        he_ref[0:HALO, :] = ph_ref[:, 0:D] * _sigmoid(ph_ref[:, D:2 * D]) * keep_prev
        he_ref[HALO:, :] = val * sg
        _conv_weight_grad(dh1_ref, he_ref, sh_ref, HALO - (CONV_K - 1), dcw_ref, tm)
        dcw_ref[SUBLANES * CONV_K:, :] += jnp.sum(dh1_ref[...].reshape(tm // SUBLANES, SUBLANES, D), axis=0)

        dh0 = dh0_ref[...]
        dval = dh0 * sg
        dglu = dh0 * val * (sg * (1.0 - sg))
        dbin_ref[0:1, 0:D] += _colsum(dval)
        dbin_ref[0:1, D:2 * D] += _colsum(dglu)
        dp_ref[:, 0:D] = dval.astype(BF16)
        dp_ref[:, D:2 * D] = dglu.astype(BF16)

        @pl.when(i == t // tm - 1)
        def _():
            cps = [pltpu.make_async_copy(acc_ref.at[a, pl.ds((2 * j + h) * rows, rows)], sq_out[a].at[h, j],
                                         sq_sem.at[(a * N_CHIPS + j) * 2 + h])
                   for a in range(n_sq) for j in range(N_CHIPS) for h in range(2)]
            for cp in cps:
                cp.start()
            for cp in cps:
                cp.wait()

    any_spec = pl.BlockSpec(memory_space=pl.ANY)
    wide = pl.BlockSpec((span * tm, D), lambda i: (i // span, 0))
    return pl.pallas_call(
        body, name="conv_backward", grid=(t // tm,),
        in_specs=[pl.BlockSpec((tm, D), lambda i: (i, 0)),
                  pl.BlockSpec((HALO, D), lambda i: (jnp.minimum((i + 1) * hb, last), 0)),
                  pl.BlockSpec((tm, 2 * D), lambda i: (i, 0)),
                  pl.BlockSpec((HALO, 2 * D), lambda i: (jnp.maximum(i * hb - 1, 0), 0)),
                  _full((HALO, D)), any_spec] + [wide] * (2 * n_sq),
        out_specs=[pl.BlockSpec((tm, 2 * D), lambda i: (i, 0)), _full((SUBLANES * HALO, D)), _full((8, 2 * D))]
        + [any_spec] * n_sq,
        out_shape=[jax.ShapeDtypeStruct(dp.shape, BF16), jax.ShapeDtypeStruct((SUBLANES * HALO, D), F32),
                   jax.ShapeDtypeStruct((8, 2 * D), F32)]
        + [jax.ShapeDtypeStruct((2, N_CHIPS, rows, D), F32)] * n_sq,
        scratch_shapes=[pltpu.VMEM((tm + HALO, D), F32), pltpu.VMEM((tm + HALO, D), F32), pltpu.VMEM((tm, D), F32),
                        pltpu.VMEM((SUBLANES - 1, tm + SHIFT_ROWS, GROUP_W), F32), pltpu.VMEM((n_sq, D, D), F32),
                        pltpu.SemaphoreType.DMA((n_sq * N_CHIPS * 2,))],
        input_output_aliases={5: 0},
        compiler_params=_params("arbitrary"),
    )(dh1, dh1, p, p, convw, dp, *[a for pair in pairs for a in pair])


def _grad_in_and_x(xt, dp, w4, dr, chip_parts):
    t = dr.shape[0]
    tm = TOKEN_TILE
    half, tn = D // 2, 512
    nb = W_BLOCK // tn
    n_w, n_x = 2 * N_CHIPS * nb, t // tm
    ns = len(chip_parts)
    xi, yi, ci = lax.axis_index("x"), lax.axis_index("y"), lax.axis_index("c")
    others = [2 * (1 - xi) + yi, 2 * xi + (1 - yi), 2 * (1 - xi) + (1 - yi)]
    blocks = others + others + [2 * xi + yi] * 2
    halves = [1 - ci] * 3 + [ci] * 3 + [1 - ci, ci]
    table = jnp.stack([jnp.stack([b * nb + n for b in blocks for n in range(nb)]),
                       jnp.stack([h for h in halves for _ in range(nb)])]).astype(jnp.int32)

    def body(tab_ref, xt_ref, dpc_ref, dpr_ref, w_ref, dr_ref, *refs):
        parts = refs[:ns]
        dx_ref, qk_ref, b2_ref, b1_ref, wire_ref = refs[ns:ns + 5]
        lands = refs[ns + 5:2 * ns + 5]
        (g_ref, st_ref, sb_ref, tmp_ref, d2d_send, d2d_recv, ici_send, ici_recv, own_sem, tmp_sem, wire_sem,
         p_send, p_recv) = refs[2 * ns + 5:]
        s = pl.program_id(0)
        x_, y_, c, _ = _position()
        chips = _other_chips(x_, y_)
        n = s % nb
        grp = s // nb
        cols = pl.ds(pl.multiple_of(n * tn, tn), tn)

        def part(a, r):
            cx, cy = chips[r]
            return pltpu.make_async_remote_copy(
                src_ref=parts[a].at[2 * cx + cy], dst_ref=lands[a].at[r], send_sem=p_send.at[3 * a + r],
                recv_sem=p_recv.at[3 * a + r], device_id=(cx, cy, c), device_id_type=MESH)

        def to_sibling(slot, land):
            return pltpu.make_async_remote_copy(
                src_ref=st_ref.at[slot], dst_ref=b1_ref.at[land, :, cols], send_sem=d2d_send.at[slot],
                recv_sem=d2d_recv.at[land * nb + n], device_id=(x_, y_, 1 - c), device_id_type=MESH)

        def to_chip(r):
            cx, cy = chips[r]
            return pltpu.make_async_remote_copy(
                src_ref=wire_ref.at[r, :, cols], dst_ref=b2_ref.at[r, :, cols], send_sem=ici_send.at[r],
                recv_sem=ici_recv.at[r], device_id=(cx, cy, c), device_id_type=MESH)

        def all_of_chip(r):
            cx, cy = chips[r]
            return pltpu.make_async_remote_copy(
                src_ref=wire_ref.at[r], dst_ref=b2_ref.at[r], send_sem=ici_send.at[r],
                recv_sem=ici_recv.at[r], device_id=(cx, cy, c), device_id_type=MESH)

        def to_result(slot):
            return pltpu.make_async_copy(st_ref.at[slot], qk_ref.at[:, cols], own_sem.at[slot])

        def sibling_piece(land):
            return pltpu.make_async_copy(b1_ref.at[land, :, cols], tmp_ref, tmp_sem)

        @pl.when(s == 0)
        def _():
            for a in range(ns):
                for r in range(3):
                    part(a, r).start()

        own_half = ((grp >= 3) & (grp <= 5)) | (grp == 7)
        land = jnp.where(grp == 7, 3, grp - 3)

        @pl.when(own_half)
        def _():
            to_sibling(0, land).wait_recv()
            sibling_piece(land).start()

        @pl.when(s < n_w)
        def _():
            g_ref[...] = _dot(xt_ref[tab_ref[1, s]], dpc_ref[...])

        @pl.when(own_half)
        def _():
            sibling_piece(land).wait()

        for g in range(2 * N_CHIPS):
            @pl.when(grp == g)
            def _(g=g):
                if g in (0, 1, 2, 6):
                    use = s if g < 3 else 3 * nb + n
                    slot = use % 2

                    @pl.when(use >= 2)
                    def _():
                        to_sibling(slot, 0).wait_send()

                    st_ref[slot] = g_ref[...]
                    to_sibling(slot, min(g, 3)).start()
                elif g in (3, 4, 5):
                    sb_ref[...] = (g_ref[...] + tmp_ref[...]).astype(BF16)
                    stage = pltpu.make_async_copy(sb_ref, wire_ref.at[g - 3, :, cols], wire_sem)
                    stage.start()
                    stage.wait()
                    to_chip(g - 3).start()
                else:
                    slot = n % 2
                    piece = g_ref[...] + tmp_ref[...]

                    @pl.when(n < 2)
                    def _():
                        to_sibling(slot, 0).wait_send()

                    @pl.when(n >= 2)
                    def _():
                        to_result(slot).wait()

                    st_ref[slot] = piece
                    to_result(slot).start()

        @pl.when(s >= n_w)
        def _():
            acc = ALPHA * dr_ref[...]
            for j in range(N_CHIPS):
                acc = acc + _dot_nt(dpr_ref[:, j * W_BLOCK:(j + 1) * W_BLOCK], w_ref[j])
            dx_ref[...] = acc

        @pl.when(s == n_w + n_x - 1)
        def _():
            for slot in range(2):
                to_result(slot).wait()
            for r in range(3):
                all_of_chip(r).wait_recv()
                all_of_chip(r).wait_send()
            for a in range(ns):
                for r in range(3):
                    part(a, r).wait_recv()
                    part(a, r).wait_send()

    any_spec = pl.BlockSpec(memory_space=pl.ANY)
    tile = lambda s, tab: (jnp.maximum(s - n_w, 0), 0)
    return pl.pallas_call(
        body, name="grad_in_and_x",
        grid_spec=pltpu.PrefetchScalarGridSpec(
            num_scalar_prefetch=1, grid=(n_w + n_x,),
            in_specs=[pl.BlockSpec((2, half, t), lambda s, tab: (0, 0, 0), pipeline_mode=pl.Buffered(1)),
                      pl.BlockSpec((t, tn), lambda s, tab: (0, tab[0, jnp.minimum(s, n_w - 1)])),
                      pl.BlockSpec((tm, D_IN), tile),
                      pl.BlockSpec((N_CHIPS, D, W_BLOCK), lambda s, tab: (0, 0, 0), pipeline_mode=pl.Buffered(1)),
                      pl.BlockSpec((tm, D), tile)] + [any_spec] * ns,
            out_specs=[pl.BlockSpec((tm, D), tile)] + [any_spec] * (4 + ns),
            scratch_shapes=[pltpu.VMEM((half, tn), F32), pltpu.VMEM((2, half, tn), F32), pltpu.VMEM((half, tn), BF16),
                            pltpu.VMEM((half, tn), F32),
                            pltpu.SemaphoreType.DMA((2,)), pltpu.SemaphoreType.DMA((N_CHIPS * nb,)),
                            pltpu.SemaphoreType.DMA((3,)), pltpu.SemaphoreType.DMA((3,)),
                            pltpu.SemaphoreType.DMA((2,)), pltpu.SemaphoreType.DMA, pltpu.SemaphoreType.DMA,
                            pltpu.SemaphoreType.DMA((3 * ns,)), pltpu.SemaphoreType.DMA((3 * ns,))]),
        out_shape=[jax.ShapeDtypeStruct((t, D), F32), jax.ShapeDtypeStruct((half, W_BLOCK), F32),
                   jax.ShapeDtypeStruct((3, half, W_BLOCK), BF16), jax.ShapeDtypeStruct((N_CHIPS, half, W_BLOCK), F32),
                   jax.ShapeDtypeStruct((3, half, W_BLOCK), BF16)]
        + [jax.ShapeDtypeStruct((3,) + q.shape[1:], q.dtype) for q in chip_parts],
        compiler_params=_params("arbitrary"),
    )(table, xt, dp, dp, w4, dr, *chip_parts)


def kernel(x, w_in, b_in, conv_w, conv_b, gn_g, gn_b, ln_v_g, ln_v_b, w_spatial, b_spatial, w_pa, w_pb, w_o, b_o, ln_out_g, ln_out_b, loss_target, m_w_in, m_b_in, m_conv_w, m_conv_b, m_gn_g, m_gn_b, m_ln_v_g, m_ln_v_b, m_w_spatial, m_b_spatial, m_w_pa, m_w_pb, m_w_o, m_b_o, m_ln_out_g, m_ln_out_b, v_w_in, v_b_in, v_conv_w, v_conv_b, v_gn_g, v_gn_b, v_ln_v_g, v_ln_v_b, v_w_spatial, v_b_spatial, v_w_pa, v_w_pb, v_w_o, v_b_o, v_ln_out_g, v_ln_out_b):
    n_seq, seq, _ = x.shape
    t = n_seq * seq
    tiles_per_seq = seq // TOKEN_TILE
    x2 = x.reshape(t, D)
    tgt = loss_target.reshape(t, D)

    conv_shard = jnp.pad(conv_w, ((0, HALO - CONV_K), (0, 0)))
    p, win4, wpa4, wpb4, wo4, conv4 = _proj_gather(
        x2, b_in.reshape(N_CHIPS, 1, W_BLOCK),
        [_place_shard(w_in, 256, BF16), _place_shard(w_pa, 128, BF16), _place_shard(w_pb, 128, BF16),
         _place_shard(w_o, 128, BF16), _place_shard(conv_shard, HALO // 2, F32)])
    win4 = win4.reshape(N_CHIPS, D, W_BLOCK)
    wpa, wpb, wo = wpa4.reshape(D, D), wpb4.reshape(D, D), wo4.reshape(D, D)
    convw = conv4.reshape(N_CHIPS, HALO, D // N_CHIPS).transpose(1, 0, 2).reshape(HALO, D)

    vecs = jnp.stack([conv_b, gn_g, gn_b, ln_v_g, ln_v_b, b_o, ln_out_g, ln_out_b])
    causal = jnp.tril(jnp.ones((CHUNK, CHUNK), bool))
    ws = jnp.where(causal[None], w_spatial, 0.0)
    ws_bf, wst_bf = ws.astype(BF16), ws.transpose(0, 2, 1).astype(BF16)
    bsp = jnp.repeat(b_spatial.T, GROUP_W, axis=1)

    h1, ya, yb, h3, s, mixed, dr, drb, xt, acc_f = _forward_tiles(p, x2, tgt, wpa, wpb, wo, convw, vecs, ws_bf, bsp, tiles_per_seq)
    dh1, dp, dya, dyb, acc_b, dbin_b, dws, dbsp_acc = _backward_tiles(p, h1, ya, yb, drb, wpa, wpb, wo, vecs, ws_bf, wst_bf, bsp)
    dp, dcw8, dbin_a, g_pa, g_pb, g_o = _conv_backward(
        dh1, p, dp, convw, [(h3, dya), (s, dyb), (mixed, drb)], tiles_per_seq)

    small, g_conv = _pack_small(acc_f, acc_b, dbin_a, dbin_b, dcw8, dws, dbsp_acc)
    small = small.reshape(2, N_CHIPS, SMALL_ROWS // 8, D)

    grads = [g_pa, g_pb, g_o, g_conv, small]
    wire = [BF16, BF16, BF16, F32, F32]
    from_sibling = _exchange_halves(grads)
    sums = [_add_halves(g, b, wd) for g, b, wd in zip(grads, from_sibling, wire)]
    grad_x, q_in, chips_in, _, _, *from_chips = _grad_in_and_x(
        xt.reshape(2, D // 2, t), dp, win4, dr, [qw for _, qw in sums])
    grad_x = grad_x.reshape(x.shape)
    mine = [_add_chips(q_in, chips_in, False)]
    mine += [_add_chips(q, b, a == 4) for a, ((q, _), b) in enumerate(zip(sums, from_chips))]
    *full, small_parts = _share_results(mine[:5], mine[5])
    grad_w_in, grad_w_pa, grad_w_pb, grad_w_o = [f.reshape(w.shape) for f, w in zip(full[:4], (w_in, w_pa, w_pb, w_o))]
    grad_conv_w = full[4].reshape(HALO, D // N_CHIPS)[:CONV_K]

    big = {}
    for name, w, g, m, v in [("w_in", w_in, grad_w_in, m_w_in, v_w_in), ("w_pa", w_pa, grad_w_pa, m_w_pa, v_w_pa),
                             ("w_pb", w_pb, grad_w_pb, m_w_pb, v_w_pb), ("w_o", w_o, grad_w_o, m_w_o, v_w_o),
                             ("conv_w", conv_w, grad_conv_w, m_conv_w, v_conv_w)]:
        big[name] = (g,) + tuple(_adamw(w, g, m, v))
    vec_names = ["conv_b", "gn_g", "gn_b", "ln_v_g", "ln_v_b", "b_o", "ln_out_g", "ln_out_b"]
    vec_triples = [(conv_b, m_conv_b, v_conv_b), (gn_g, m_gn_g, v_gn_g), (gn_b, m_gn_b, v_gn_b),
                   (ln_v_g, m_ln_v_g, v_ln_v_g), (ln_v_b, m_ln_v_b, v_ln_v_b), (b_o, m_b_o, v_b_o),
                   (ln_out_g, m_ln_out_g, v_ln_out_g), (ln_out_b, m_ln_out_b, v_ln_out_b)]
    small_res, loss8 = _adamw_small(
        small_parts.reshape(8, SMALL_ROWS // 8, D), vec_triples, (b_in, m_b_in, v_b_in),
        (w_spatial, m_w_spatial, v_w_spatial), (b_spatial, m_b_spatial, v_b_spatial))
    per_name = dict(zip(vec_names + ["b_in", "w_spatial", "b_spatial"], small_res))

    order = ["w_in", "b_in", "conv_w", "conv_b", "gn_g", "gn_b", "ln_v_g", "ln_v_b", "w_spatial", "b_spatial",
             "w_pa", "w_pb", "w_o", "b_o", "ln_out_g", "ln_out_b"]
    outs = [loss8[0, 0], grad_x]
    for kind in range(4):
        outs += [big[n][kind] if n in big else per_name[n][kind] for n in order]
    return tuple(outs)
```

```python
import functools
import math

import jax
import jax.numpy as jnp
from jax import lax
from jax.experimental import pallas as pl
from jax.experimental.pallas import tpu as pltpu

D = 1024
N_GROUPS = 8
GROUP_W = D // N_GROUPS
CHUNK = 128
CONV_K = 31
HALO = 32
D_IN = 8 * D
N_CHIPS = 4
W_BLOCK = D_IN // N_CHIPS
ALPHA = 2.0 ** 0.25
LN_EPS = 1e-5
ADAM_LR, ADAM_B1, ADAM_B2, ADAM_EPS, ADAM_WD, ADAM_STEP = 0.001, 0.9, 0.999, 1e-08, 0.01, 10

TOKEN_TILE = 256
VMEM_LIMIT = 56 * 1024 * 1024
MESH = pl.DeviceIdType.MESH
F32, BF16 = jnp.float32, jnp.bfloat16


def _sigmoid(x):
    return 1.0 / (1.0 + jnp.exp(-x))


def _gelu(x):
    c = math.sqrt(2.0 / math.pi)
    t = jnp.tanh(c * (x + 0.044715 * (x * x * x)))
    return x * (0.5 * (1.0 + t))


def _gelu_and_grad(x):
    c = math.sqrt(2.0 / math.pi)
    x2 = x * x
    t = jnp.tanh(c * (x + 0.044715 * (x2 * x)))
    cdf = 0.5 * (1.0 + t)
    return x * cdf, cdf + 0.5 * x * (1.0 - t * t) * (c * (1.0 + 3.0 * 0.044715 * x2))


def _norm_stats(v):
    mu = jnp.mean(v, axis=-1, keepdims=True)
    vc = v - mu
    var = jnp.mean(vc * vc, axis=-1, keepdims=True)
    rstd = lax.rsqrt(var + LN_EPS)
    return vc * rstd, rstd


def _norm_bwd(dxhat, xhat, rstd):
    m1 = jnp.mean(dxhat, axis=-1, keepdims=True)
    m2 = jnp.mean(dxhat * xhat, axis=-1, keepdims=True)
    return rstd * (dxhat - m1 - xhat * m2)


def _dot(a, b):
    return jnp.dot(a, b, preferred_element_type=F32)


def _dot_nt(a, b):
    return lax.dot_general(a, b, (((1,), (1,)), ((), ())), preferred_element_type=F32)


def _dot_tn(a, b):
    return lax.dot_general(a, b, (((0,), (0,)), ((), ())), preferred_element_type=F32)


def _colsum(v):
    return jnp.sum(v, axis=0, keepdims=True)


def _full(shape):
    return pl.BlockSpec(shape, lambda *_: (0,) * len(shape))


def _resident(shape):
    return pl.BlockSpec(shape, lambda *_: (0,) * len(shape), pipeline_mode=pl.Buffered(1))


def _params(*sem):
    return pltpu.CompilerParams(dimension_semantics=sem, vmem_limit_bytes=VMEM_LIMIT)


def _chip_index():
    return (2 * lax.axis_index("x") + lax.axis_index("y")).astype(jnp.int32).reshape(1)


def _core_index():
    return lax.axis_index("c").astype(jnp.int32).reshape(1)


def _place_shard(w, rows, dtype):
    r, c = w.shape
    steps = r // 2 // rows

    def body(k_ref, w_ref, o_ref):
        o_ref[...] = w_ref[...].astype(dtype)

    return pl.pallas_call(
        body, name="place_shard",
        grid_spec=pltpu.PrefetchScalarGridSpec(
            num_scalar_prefetch=1, grid=(2, steps),
            in_specs=[pl.BlockSpec((rows, c), lambda h, i, k: (h * steps + i, 0))],
            out_specs=pl.BlockSpec((None, None, rows, c), lambda h, i, k: (k[0], h, i, 0))),
        out_shape=jax.ShapeDtypeStruct((N_CHIPS, 2, r // 2, c), dtype),
        compiler_params=_params("parallel", "parallel"),
    )(_chip_index(), w)


def _position():
    x, y, c = lax.axis_index("x"), lax.axis_index("y"), lax.axis_index("c")
    return x, y, c, 2 * x + y


def _other_chips(x, y):
    return [(1 - x, y), (x, 1 - y), (1 - x, 1 - y)]


def _any_specs(n):
    return [pl.BlockSpec(memory_space=pl.ANY)] * n


def _proj_gather(x, b4, bufs):
    t = x.shape[0]
    tm = 1024
    steps = t // tm
    half = D // 2
    n = len(bufs)
    xi, yi = lax.axis_index("x"), lax.axis_index("y")
    order = jnp.stack([2 * xi + yi, 2 * (1 - xi) + yi, 2 * xi + (1 - yi), 2 * (1 - xi) + (1 - yi)]).astype(jnp.int32)

    def body(order_ref, x_ref, b_ref, *refs):
        p_ref, outs = refs[n], refs[n + 1:2 * n + 1]
        xb_ref, w_ref, lsem, send, recv, fsend, frecv = refs[2 * n + 1:]
        jj, i = pl.program_id(0), pl.program_id(1)
        x_, y_, c, k = _position()
        chips = _other_chips(x_, y_)

        def sent(a, r):
            cx, cy = chips[r]
            return pltpu.make_async_remote_copy(
                src_ref=outs[a].at[k, c], dst_ref=outs[a].at[k, c], send_sem=send.at[3 * a + r],
                recv_sem=recv.at[3 * a + r], device_id=(cx, cy, c), device_id_type=MESH)

        def landed(a, r):
            cx, cy = chips[r]
            return pltpu.make_async_remote_copy(
                src_ref=outs[a].at[2 * cx + cy, c], dst_ref=outs[a].at[2 * cx + cy, c], send_sem=send.at[3 * a + r],
                recv_sem=recv.at[3 * a + r], device_id=(cx, cy, c), device_id_type=MESH)

        def passed(a, r, h):
            cx, cy = chips[r]
            return pltpu.make_async_remote_copy(
                src_ref=outs[a].at[2 * cx + cy, h], dst_ref=outs[a].at[2 * cx + cy, h], send_sem=fsend.at[3 * a + r],
                recv_sem=frecv.at[3 * a + r], device_id=(x_, y_, 1 - c), device_id_type=MESH)

        def load(block):
            cps = [pltpu.make_async_copy(outs[0].at[block, h], w_ref.at[pl.ds(h * half, half)], lsem.at[h])
                   for h in range(2)]
            for cp in cps:
                cp.start()
            for cp in cps:
                cp.wait()

        @pl.when((jj == 0) & (i == 0))
        def _():
            for a in range(n):
                for r in range(3):
                    sent(a, r).start()
            load(k)

        for r in range(3):
            @pl.when((jj == r + 1) & (i == 0))
            def _(r=r):
                landed(0, r).wait_recv()
                passed(0, r, c).start()
                passed(0, r, 1 - c).wait_recv()
                load(2 * chips[r][0] + chips[r][1])

        rows = pl.ds(pl.multiple_of(i * tm, tm), tm)

        @pl.when(jj == 0)
        def _():
            xb_ref[rows, :] = x_ref[...].astype(BF16)

        p_ref[...] = _dot(xb_ref[rows, :], w_ref[...]) + b_ref[...]

        @pl.when((jj == N_CHIPS - 1) & (i == steps - 1))
        def _():
            for a in range(1, n):
                for r in range(3):
                    landed(a, r).wait_recv()
                    passed(a, r, c).start()
            for a in range(1, n):
                for r in range(3):
                    passed(a, r, 1 - c).wait_recv()
            for a in range(n):
                for r in range(3):
                    sent(a, r).wait_send()
                    passed(a, r, c).wait_send()

    any_spec = pl.BlockSpec(memory_space=pl.ANY)
    return pl.pallas_call(
        body, name="proj_gather",
        grid_spec=pltpu.PrefetchScalarGridSpec(
            num_scalar_prefetch=1, grid=(N_CHIPS, steps),
            in_specs=[pl.BlockSpec((tm, D), lambda jj, i, o: (jnp.where(jj == 0, i, steps - 1), 0)),
                      pl.BlockSpec((None, 1, W_BLOCK), lambda jj, i, o: (o[jj], 0, 0))] + [any_spec] * n,
            out_specs=[pl.BlockSpec((tm, W_BLOCK), lambda jj, i, o: (i, o[jj]))] + [any_spec] * n,
            scratch_shapes=[pltpu.VMEM((t, D), BF16), pltpu.VMEM((D, W_BLOCK), BF16), pltpu.SemaphoreType.DMA((2,)),
                            pltpu.SemaphoreType.DMA((3 * n,)), pltpu.SemaphoreType.DMA((3 * n,)),
                            pltpu.SemaphoreType.DMA((3 * n,)), pltpu.SemaphoreType.DMA((3 * n,))]),
        out_shape=[jax.ShapeDtypeStruct((t, D_IN), F32)] + [jax.ShapeDtypeStruct(b.shape, b.dtype) for b in bufs],
        input_output_aliases={3 + a: 1 + a for a in range(n)},
        compiler_params=_params("arbitrary", "arbitrary"),
    )(order, x, b4, *bufs)


def _share_results(bufs, small):
    n = len(bufs)

    def body(*refs):
        outs, small_out = refs[n + 1:2 * n + 1], refs[2 * n + 1]
        send, recv, ssend, srecv = refs[2 * n + 2:]
        x, y, c, k = _position()
        cps = []
        for a in range(n):
            cp = pltpu.make_async_remote_copy(
                src_ref=outs[a].at[c], dst_ref=outs[a].at[c], send_sem=send.at[a], recv_sem=recv.at[a],
                device_id=(x, y, 1 - c), device_id_type=MESH)
            cp.start()
            cps.append(cp)
        waits = []
        for p in range(1, 8):
            px, py, pc = x ^ (p >> 2), y ^ ((p >> 1) & 1), c ^ (p & 1)
            cp = pltpu.make_async_remote_copy(
                src_ref=small_out.at[k, c], dst_ref=small_out.at[k, c], send_sem=ssend.at[p - 1],
                recv_sem=srecv.at[p - 1], device_id=(px, py, pc), device_id_type=MESH)
            cp.start()
            cps.append(cp)
            waits.append(pltpu.make_async_remote_copy(
                src_ref=small_out.at[2 * px + py, pc], dst_ref=small_out.at[2 * px + py, pc], send_sem=ssend.at[p - 1],
                recv_sem=srecv.at[p - 1], device_id=(px, py, pc), device_id_type=MESH))
        for a in range(n):
            pltpu.make_async_remote_copy(
                src_ref=outs[a].at[1 - c], dst_ref=outs[a].at[1 - c], send_sem=send.at[a], recv_sem=recv.at[a],
                device_id=(x, y, 1 - c), device_id_type=MESH).wait_recv()
        for w in waits:
            w.wait_recv()
        for cp in cps:
            cp.wait_send()

    return pl.pallas_call(
        body, name="rs_share_results",
        in_specs=_any_specs(n + 1), out_specs=_any_specs(n + 1),
        out_shape=[jax.ShapeDtypeStruct(b.shape, b.dtype) for b in bufs + [small]],
        scratch_shapes=[pltpu.SemaphoreType.DMA((n,)), pltpu.SemaphoreType.DMA((n,)),
                        pltpu.SemaphoreType.DMA((7,)), pltpu.SemaphoreType.DMA((7,))],
        input_output_aliases={a: a for a in range(n + 1)},
    )(*bufs, small)


def _row_tile(r, c):
    t = max(8, min(r, (1 << 18) // c))
    while r % t:
        t //= 2
    return t


def _add_devices(g, lands, sib, per_device):
    r, c = g.shape[-2:]
    t = _row_tile(r, c)

    def body(kc_ref, g_ref, l_ref, s_ref, f_ref):
        f = g_ref[...] + s_ref[...]
        for i in range(l_ref.shape[0]):
            f = f + l_ref[i].astype(F32)
        f_ref[...] = f

    if per_device:
        out_spec = pl.BlockSpec((None, None, t, c), lambda i, kc: (kc[0], kc[1], i, 0))
        out_shape = jax.ShapeDtypeStruct((N_CHIPS, 2, r, c), F32)
    else:
        out_spec = pl.BlockSpec((None, t, c), lambda i, kc: (kc[1], i, 0))
        out_shape = jax.ShapeDtypeStruct((2, r, c), F32)
    return pl.pallas_call(
        body, name="rs_add_devices",
        grid_spec=pltpu.PrefetchScalarGridSpec(
            num_scalar_prefetch=1, grid=(r // t,),
            in_specs=[pl.BlockSpec((None, None, t, c), lambda i, kc: (kc[1], kc[0], i, 0)),
                      pl.BlockSpec((lands.shape[0], t, c), lambda i, kc: (0, i, 0)),
                      pl.BlockSpec((t, c), lambda i, kc: (i, 0))],
            out_specs=out_spec),
        out_shape=out_shape,
        compiler_params=_params("parallel"),
    )(jnp.concatenate([_chip_index(), _core_index()]), g, lands, sib)


def _add_chips(q, b2):
    r, c = q.shape
    t = _row_tile(r, c)

    def body(c_ref, q_ref, b_ref, f_ref):
        f_ref[...] = ((q_ref[...] + b_ref[0].astype(F32)) + b_ref[1].astype(F32)) + b_ref[2].astype(F32)

    return pl.pallas_call(
        body, name="rs_add_chips",
        grid_spec=pltpu.PrefetchScalarGridSpec(
            num_scalar_prefetch=1, grid=(r // t,),
            in_specs=[pl.BlockSpec((t, c), lambda i, cr: (i, 0)), pl.BlockSpec((3, t, c), lambda i, cr: (0, i, 0))],
            out_specs=pl.BlockSpec((None, t, c), lambda i, cr: (cr[0], i, 0))),
        out_shape=jax.ShapeDtypeStruct((2, r, c), F32),
        compiler_params=_params("parallel"),
    )(_core_index(), q, b2)


def _adamw_math(w, g, m, v):
    m = ADAM_B1 * m + (1.0 - ADAM_B1) * g
    v = ADAM_B2 * v + (1.0 - ADAM_B2) * (g * g)
    m_hat = m / (1.0 - ADAM_B1 ** ADAM_STEP)
    v_hat = v / (1.0 - ADAM_B2 ** ADAM_STEP)
    delta = -ADAM_LR * (m_hat / (jnp.sqrt(v_hat) + ADAM_EPS) + ADAM_WD * w)
    return delta, m, v


def _adamw(w, g, m, v):
    r, c = w.shape
    t = _row_tile(r, c) if r % 8 == 0 else r

    def body(w_ref, g_ref, m_ref, v_ref, d_ref, nm_ref, nv_ref):
        d_ref[...], nm_ref[...], nv_ref[...] = _adamw_math(w_ref[...], g_ref[...], m_ref[...], v_ref[...])

    spec = pl.BlockSpec((t, c), lambda i: (i, 0))
    return pl.pallas_call(
        body, name="adamw", grid=(r // t,), in_specs=[spec] * 4, out_specs=[spec] * 3,
        out_shape=[jax.ShapeDtypeStruct((r, c), F32)] * 3, compiler_params=_params("parallel"),
    )(w, g, m, v)


ROW_B_IN = 0
ROW_VECS = 8
ROW_LOSS = 16
ROW_B_SPATIAL = 24
ROW_W_SPATIAL = 32
SMALL_ROWS = 192
N_VECS = 8


def _pack_small(acc_f, acc_b, dbin_a, dbin_b, dcw8, dws, dbsp):
    cols = D // N_CHIPS

    def body(af_ref, ab_ref, da_ref, db_ref, cw_ref, ws_ref, bs_ref, o_ref, gc_ref):
        o_ref[...] = jnp.zeros_like(o_ref)
        for j in range(D_IN // D):
            src = da_ref if j < 2 else db_ref
            o_ref[ROW_B_IN + j:ROW_B_IN + j + 1, :] = src[0:1, j * D:(j + 1) * D]
        dcw = jnp.sum(cw_ref[...].reshape(HALO, SUBLANES, D), axis=1)
        o_ref[ROW_VECS:ROW_VECS + 1, :] = dcw[CONV_K:CONV_K + 1]
        o_ref[ROW_VECS + 1:ROW_VECS + 5, :] = ab_ref[0:4, :]
        o_ref[ROW_VECS + 5:ROW_VECS + 6, :] = af_ref[2:3, :]
        o_ref[ROW_VECS + 6:ROW_VECS + 8, :] = af_ref[0:2, :]
        o_ref[ROW_LOSS:ROW_LOSS + 1, :] = af_ref[3:4, :]
        head = lax.broadcasted_iota(jnp.int32, (N_GROUPS, D), 0)
        lane = lax.broadcasted_iota(jnp.int32, (N_GROUPS, D), 1)
        indicator = jnp.where(lane // GROUP_W == head, 1.0, 0.0)
        o_ref[ROW_B_SPATIAL:ROW_B_SPATIAL + N_GROUPS, 0:CHUNK] = lax.dot_general(
            indicator, bs_ref[...], (((1,), (1,)), ((), ())), precision=lax.Precision.HIGHEST, preferred_element_type=F32)
        t_idx = lax.broadcasted_iota(jnp.int32, (CHUNK, D), 0)
        s_idx = lax.broadcasted_iota(jnp.int32, (CHUNK, D), 1) % CHUNK
        o_ref[ROW_W_SPATIAL:ROW_W_SPATIAL + CHUNK, :] = jnp.where(s_idx <= t_idx, ws_ref[...], 0.0)
        for h in range(2):
            for j in range(N_CHIPS):
                gc_ref[h, j] = dcw[h * (HALO // 2):(h + 1) * (HALO // 2), j * cols:(j + 1) * cols]

    ins = [acc_f, acc_b, dbin_a, dbin_b, dcw8, dws, dbsp]
    return pl.pallas_call(
        body, name="pack_small",
        in_specs=[_full(a.shape) for a in ins],
        out_specs=[_full((SMALL_ROWS, D)), _full((2, N_CHIPS, HALO // 2, cols))],
        out_shape=[jax.ShapeDtypeStruct((SMALL_ROWS, D), F32), jax.ShapeDtypeStruct((2, N_CHIPS, HALO // 2, cols), F32)],
        compiler_params=_params(),
    )(*ins)


def _adamw_small(parts, vecs, b_in, w_spatial, b_spatial):
    triples = list(vecs) + [b_in, w_spatial, b_spatial]
    n_in = 3 * len(triples)

    def body(p_ref, *refs):
        ins = [refs[3 * i:3 * i + 3] for i in range(len(triples))]
        outs = [refs[n_in + 4 * i:n_in + 4 * i + 4] for i in range(len(triples))]
        loss_ref, g_ref = refs[n_in + 4 * len(triples):]
        rows = SMALL_ROWS // 8
        for k in range(N_CHIPS):
            for core in range(2):
                g_ref[(core * N_CHIPS + k) * rows:(core * N_CHIPS + k + 1) * rows, :] = p_ref[2 * k + core]

        def step(g, wmv, out, get, put):
            d, nm, nv = _adamw_math(get(wmv[0]), g, get(wmv[1]), get(wmv[2]))
            for o, val in zip(out, (g, d, nm, nv)):
                put(o, val)

        for i in range(N_VECS):
            step(g_ref[ROW_VECS + i:ROW_VECS + i + 1, :], ins[i], outs[i],
                 lambda r: r[...].reshape(1, D), lambda o, val: o.__setitem__(Ellipsis, val.reshape(D)))
        for j in range(D_IN // D):
            piece = pl.ds(j * D, D)
            step(g_ref[ROW_B_IN + j:ROW_B_IN + j + 1, :], ins[N_VECS], outs[N_VECS],
                 lambda r: r[piece].reshape(1, D), lambda o, val: o.__setitem__(piece, val.reshape(D)))
        for h in range(N_GROUPS):
            step(g_ref[ROW_W_SPATIAL:ROW_W_SPATIAL + CHUNK, h * CHUNK:(h + 1) * CHUNK], ins[N_VECS + 1], outs[N_VECS + 1],
                 lambda r: r[h], lambda o, val: o.__setitem__(h, val))
        step(g_ref[ROW_B_SPATIAL:ROW_B_SPATIAL + N_GROUPS, 0:CHUNK], ins[N_VECS + 2], outs[N_VECS + 2],
             lambda r: r[...], lambda o, val: o.__setitem__(Ellipsis, val))
        lanes = g_ref[ROW_LOSS:ROW_LOSS + 1, :]
        loss_ref[...] = jnp.broadcast_to(jnp.sum(lanes, axis=1, keepdims=True), (8, 128))

    flat = [a for tr in triples for a in tr]
    out_shape = [jax.ShapeDtypeStruct(tr[0].shape, F32) for tr in triples for _ in range(4)]
    out_shape.append(jax.ShapeDtypeStruct((8, 128), F32))
    res = pl.pallas_call(
        body, name="adamw_small",
        in_specs=[_full(parts.shape)] + [_full(a.shape) for a in flat],
        out_specs=[_full(o.shape) for o in out_shape],
        out_shape=out_shape,
        scratch_shapes=[pltpu.VMEM((SMALL_ROWS, D), F32)],
        compiler_params=_params(),
    )(parts, *flat)
    return [res[4 * i:4 * i + 4] for i in range(len(triples))], res[-1]


SUBLANES = 8
SHIFT_ROWS = HALO - SUBLANES


def _shifted_copies(src_ref, sh_ref, cs, tm):
    for p in range(1, SUBLANES):
        sh_ref[p - 1] = src_ref[pl.ds(p, tm + SHIFT_ROWS), cs]


def _tap(src_ref, sh_ref, cs, offset, start, rows):
    p, q = offset % SUBLANES, offset // SUBLANES
    if p == 0:
        return src_ref[pl.ds(start + SUBLANES * q, rows), cs]
    return sh_ref[p - 1, pl.ds(start + SUBLANES * q, rows), :]


def _conv_taps(src_ref, sh_ref, w_ref, first_offset, step, bias, dst_ref, tm):
    rows = 64
    for g in range(N_GROUPS):
        cs = slice(g * GROUP_W, (g + 1) * GROUP_W)
        _shifted_copies(src_ref, sh_ref, cs, tm)
        for rb in range(tm // rows):
            acc = jnp.zeros((rows, GROUP_W), F32) + (bias[:, cs] if bias is not None else 0.0)
            for k in range(CONV_K):
                acc = acc + w_ref[k:k + 1, cs] * _tap(src_ref, sh_ref, cs, first_offset + step * k, rb * rows, rows)
            dst_ref[rb * rows:(rb + 1) * rows, cs] = acc


def _conv_weight_grad(d_ref, src_ref, sh_ref, first_offset, acc_ref, tm):
    rows = 64
    for g in range(N_GROUPS):
        cs = slice(g * GROUP_W, (g + 1) * GROUP_W)
        _shifted_copies(src_ref, sh_ref, cs, tm)
        for rb in range(tm // rows):
            d = d_ref[rb * rows:(rb + 1) * rows, cs]
            for k in range(CONV_K):
                prod = d * _tap(src_ref, sh_ref, cs, first_offset + k, rb * rows, rows)
                acc_ref[SUBLANES * k:SUBLANES * (k + 1), cs] += jnp.sum(
                    prod.reshape(rows // SUBLANES, SUBLANES, GROUP_W), axis=0)


def _spatial_mix(w_ref, v_bf, tm):
    rows = []
    for q in range(tm // CHUNK):
        cols = [_dot(w_ref[h], v_bf[q * CHUNK:(q + 1) * CHUNK, h * GROUP_W:(h + 1) * GROUP_W])
                for h in range(N_GROUPS)]
        rows.append(jnp.concatenate(cols, axis=1))
    return jnp.concatenate(rows, axis=0)


def _group_norm_fwd(h1, gn_g, gn_b):
    xhat, rstd = [], []
    for g in range(N_GROUPS):
        xh, rs = _norm_stats(h1[:, g * GROUP_W:(g + 1) * GROUP_W])
        xhat.append(xh)
        rstd.append(rs)
    xhat = jnp.concatenate(xhat, axis=1)
    return xhat * gn_g + gn_b, xhat, rstd


def _forward_tiles(p, x, tgt, wpa, wpb, wo, convw, vecs, ws, bsp, tiles_per_seq):
    t = x.shape[0]
    tm = TOKEN_TILE
    hb = tm // HALO

    def body(p_ref, ph_ref, x_ref, t_ref, wpa_ref, wpb_ref, wo_ref, cw_ref, vec_ref, ws_ref, bsp_ref,
             h1_ref, ya_ref, yb_ref, h3_ref, s_ref, mx_ref, dr_ref, drb_ref, xt_ref, acc_ref, he_ref, sh_ref):
        i = pl.program_id(0)
        xt_ref[...] = x_ref[...].T.astype(BF16)
        conv_b, gn_g, gn_b, lnv_g, lnv_b, b_o, lno_g, lno_b = [vec_ref[j:j + 1, :] for j in range(8)]

        keep = jnp.where(i % tiles_per_seq == 0, 0.0, 1.0)
        he_ref[0:HALO, :] = ph_ref[:, 0:D] * _sigmoid(ph_ref[:, D:2 * D]) * keep
        he_ref[HALO:, :] = p_ref[:, 0:D] * _sigmoid(p_ref[:, D:2 * D])
        _conv_taps(he_ref, sh_ref, cw_ref, HALO - (CONV_K - 1), 1, conv_b, h1_ref, tm)
        h2, _, _ = _group_norm_fwd(h1_ref[...], gn_g, gn_b)
        a_gate = p_ref[:, 2 * D:3 * D]
        h3 = ((h2 * _sigmoid(h2)) * (a_gate * _sigmoid(a_gate))).astype(BF16)
        h3_ref[...] = h3
        ya = _dot(h3, wpa_ref[...])
        ya_ref[...] = ya

        u = _gelu(p_ref[:, 3 * D:4 * D])
        vhat, _ = _norm_stats(_gelu(p_ref[:, 4 * D:5 * D]))
        v1 = (vhat * lnv_g + lnv_b).astype(BF16)
        b_gate = p_ref[:, 5 * D:6 * D]
        vmix = _spatial_mix(ws_ref, v1, tm) + jnp.concatenate([bsp_ref[...]] * (tm // CHUNK), axis=0)
        s = (u * vmix * (b_gate * _sigmoid(b_gate))).astype(BF16)
        s_ref[...] = s
        yb = _dot(s, wpb_ref[...])
        yb_ref[...] = yb

        mixed = (_sigmoid(p_ref[:, 6 * D:7 * D]) * ya + _sigmoid(p_ref[:, 7 * D:8 * D]) * yb).astype(BF16)
        mx_ref[...] = mixed
        r = ALPHA * x_ref[...] + (_dot(mixed, wo_ref[...]) + b_o)
        xhat, rstd = _norm_stats(r)
        err = (xhat * lno_g + lno_b) - t_ref[...]
        dout = err * (1.0 / D)
        dr = _norm_bwd(dout * lno_g, xhat, rstd)
        dr_ref[...] = dr
        drb_ref[...] = dr.astype(BF16)

        @pl.when(i == 0)
        def _():
            acc_ref[...] = jnp.zeros_like(acc_ref)

        acc_ref[0:1, :] += _colsum(dout * xhat)
        acc_ref[1:2, :] += _colsum(dout)
        acc_ref[2:3, :] += _colsum(dr)
        acc_ref[3:4, :] += _colsum(err * err) * (0.5 / D)

    tile = lambda w: pl.BlockSpec((tm, w), lambda i: (i, 0))
    f32_out = jax.ShapeDtypeStruct((t, D), F32)
    bf_out = jax.ShapeDtypeStruct((t, D), BF16)
    return pl.pallas_call(
        body, name="forward_tiles", grid=(t // tm,),
        in_specs=[tile(D_IN),
                  pl.BlockSpec((HALO, 2 * D), lambda i: (jnp.maximum(i * hb - 1, 0), 0)),
                  tile(D), tile(D), _resident((D, D)), _resident((D, D)), _resident((D, D)), _full((HALO, D)), _full((8, D)),
                  _full((N_GROUPS, CHUNK, CHUNK)), _full((CHUNK, D))],
        out_specs=[tile(D)] * 8 + [pl.BlockSpec((D, tm), lambda i: (0, i)), _full((8, D))],
        out_shape=[f32_out, f32_out, f32_out, bf_out, bf_out, bf_out, f32_out, bf_out,
                   jax.ShapeDtypeStruct((D, t), BF16), jax.ShapeDtypeStruct((8, D), F32)],
        scratch_shapes=[pltpu.VMEM((tm + HALO, D), F32), pltpu.VMEM((SUBLANES - 1, tm + SHIFT_ROWS, GROUP_W), F32)],
        compiler_params=_params("arbitrary"),
    )(p, p, x, tgt, wpa, wpb, wo, convw, vecs, ws, bsp)


def _backward_tiles(p, h1, ya, yb, drb, wpa, wpb, wo, vecs, ws, wst, bsp):
    t = h1.shape[0]
    tm = TOKEN_TILE

    def body(p_ref, h1_ref, ya_ref, yb_ref, drb_ref, wpa_ref, wpb_ref, wo_ref, vec_ref, ws_ref, wst_ref, bsp_ref,
             dh1_ref, dp_ref, dya_ref, dyb_ref, acc_ref, dbin_ref, dws_ref, dbsp_ref):
        i = pl.program_id(0)
        _, gn_g, gn_b, lnv_g, lnv_b = [vec_ref[j:j + 1, :] for j in range(5)]

        @pl.when(i == 0)
        def _():
            acc_ref[...] = jnp.zeros_like(acc_ref)
            dbin_ref[...] = jnp.zeros_like(dbin_ref)
            dws_ref[...] = jnp.zeros_like(dws_ref)
            dbsp_ref[...] = jnp.zeros_like(dbsp_ref)

        def emit(block, val):
            dbin_ref[0:1, block * D:(block + 1) * D] += _colsum(val)
            dp_ref[:, block * D:(block + 1) * D] = val.astype(BF16)

        dp_ref[:, 0:2 * D] = jnp.zeros((tm, 2 * D), BF16)
        dmixed = _dot_nt(drb_ref[...], wo_ref[...])
        ga = _sigmoid(p_ref[:, 6 * D:7 * D])
        gb = _sigmoid(p_ref[:, 7 * D:8 * D])
        dya = (dmixed * ga).astype(BF16)
        dyb = (dmixed * gb).astype(BF16)
        dya_ref[...] = dya
        dyb_ref[...] = dyb
        emit(6, dmixed * ya_ref[...] * (ga * (1.0 - ga)))
        emit(7, dmixed * yb_ref[...] * (gb * (1.0 - gb)))

        dh3 = _dot_nt(dya, wpa_ref[...])
        h2, xhat, rstd = _group_norm_fwd(h1_ref[...], gn_g, gn_b)
        sg = _sigmoid(h2)
        a_gate = p_ref[:, 2 * D:3 * D]
        sa = _sigmoid(a_gate)
        dh2 = dh3 * (a_gate * sa) * (sg * (1.0 + h2 * (1.0 - sg)))
        emit(2, dh3 * (h2 * sg) * (sa * (1.0 + a_gate * (1.0 - sa))))
        acc_ref[0:1, :] += _colsum(dh2 * xhat)
        acc_ref[1:2, :] += _colsum(dh2)
        dxhat = dh2 * gn_g
        for g in range(N_GROUPS):
            cs = slice(g * GROUP_W, (g + 1) * GROUP_W)
            dh1_ref[:, cs] = _norm_bwd(dxhat[:, cs], xhat[:, cs], rstd[g])

        ds = _dot_nt(dyb, wpb_ref[...])
        u_pre = p_ref[:, 3 * D:4 * D]
        u, du_dpre = _gelu_and_grad(u_pre)
        v0, dv_dpre = _gelu_and_grad(p_ref[:, 4 * D:5 * D])
        vhat, vrstd = _norm_stats(v0)
        v1 = (vhat * lnv_g + lnv_b).astype(BF16)
        vmix = _spatial_mix(ws_ref, v1, tm) + jnp.concatenate([bsp_ref[...]] * (tm // CHUNK), axis=0)
        b_gate = p_ref[:, 5 * D:6 * D]
        sb = _sigmoid(b_gate)
        silu_b = b_gate * sb
        emit(3, ds * vmix * silu_b * du_dpre)
        emit(5, ds * u * vmix * (sb * (1.0 + b_gate * (1.0 - sb))))
        dvmix = ds * u * silu_b
        dvmix_bf = dvmix.astype(BF16)
        for q in range(tm // CHUNK):
            dbsp_ref[...] += dvmix[q * CHUNK:(q + 1) * CHUNK, :]
            for h in range(N_GROUPS):
                blk = (slice(q * CHUNK, (q + 1) * CHUNK), slice(h * GROUP_W, (h + 1) * GROUP_W))
                dws_ref[:, h * GROUP_W:(h + 1) * GROUP_W] += _dot_nt(dvmix_bf[blk], v1[blk])
        dv1 = _spatial_mix(wst_ref, dvmix_bf, tm)
        acc_ref[2:3, :] += _colsum(dv1 * vhat)
        acc_ref[3:4, :] += _colsum(dv1)
        emit(4, _norm_bwd(dv1 * lnv_g, vhat, vrstd) * dv_dpre)

    tile = lambda w: pl.BlockSpec((tm, w), lambda i: (i, 0))
    return pl.pallas_call(
        body, name="backward_tiles", grid=(t // tm,),
        in_specs=[tile(D_IN), tile(D), tile(D), tile(D), tile(D), _resident((D, D)), _resident((D, D)), _resident((D, D)),
                  _full((8, D)), _full((N_GROUPS, CHUNK, CHUNK)), _full((N_GROUPS, CHUNK, CHUNK)), _full((CHUNK, D))],
        out_specs=[tile(D), tile(D_IN), tile(D), tile(D), _full((8, D)), _full((8, D_IN)),
                   _full((CHUNK, D)), _full((CHUNK, D))],
        out_shape=[jax.ShapeDtypeStruct((t, D), F32), jax.ShapeDtypeStruct((t, D_IN), BF16),
                   jax.ShapeDtypeStruct((t, D), BF16), jax.ShapeDtypeStruct((t, D), BF16),
                   jax.ShapeDtypeStruct((8, D), F32), jax.ShapeDtypeStruct((8, D_IN), F32),
                   jax.ShapeDtypeStruct((CHUNK, D), F32), jax.ShapeDtypeStruct((CHUNK, D), F32)],
        compiler_params=_params("arbitrary"),
    )(p, h1, ya, yb, drb, wpa, wpb, wo, vecs, ws, wst, bsp)


def _conv_backward(dh1, p, dp, convw, pairs, tiles_per_seq):
    t = dh1.shape[0]
    tm = TOKEN_TILE
    hb = tm // HALO
    last = t // HALO - 1
    n_sq = len(pairs)
    span = 2
    rows = D // 8

    def body(dh1_ref, dnext_ref, p_ref, ph_ref, cw_ref, dp_in_ref, *refs):
        del dp_in_ref
        sq_in = refs[:2 * n_sq]
        dp_ref, dcw_ref, dbin_ref = refs[2 * n_sq:2 * n_sq + 3]
        sq_out = refs[2 * n_sq + 3:3 * n_sq + 3]
        sq_wire = refs[3 * n_sq + 3:4 * n_sq + 3]
        de_ref, he_ref, dh0_ref, sh_ref, acc_ref, wire_ref, sq_sem, wire_sem = refs[4 * n_sq + 3:]
        i = pl.program_id(0)

        @pl.when(i == 0)
        def _():
            dcw_ref[...] = jnp.zeros_like(dcw_ref)
            dbin_ref[...] = jnp.zeros_like(dbin_ref)
            acc_ref[...] = jnp.zeros_like(acc_ref)

        @pl.when(i % span == span - 1)
        def _():
            for a in range(n_sq):
                acc_ref[a] += _dot_tn(sq_in[2 * a][...], sq_in[2 * a + 1][...])

        keep_next = jnp.where(i % tiles_per_seq == tiles_per_seq - 1, 0.0, 1.0)
        de_ref[0:tm, :] = dh1_ref[...]
        de_ref[tm:, :] = dnext_ref[...] * keep_next
        _conv_taps(de_ref, sh_ref, cw_ref, CONV_K - 1, -1, None, dh0_ref, tm)

        keep_prev = jnp.where(i % tiles_per_seq == 0, 0.0, 1.0)
        sg = _sigmoid(p_ref[:, D:2 * D])
        val = p_ref[:, 0:D]
        he_ref[0:HALO, :] = ph_ref[:, 0:D] * _sigmoid(ph_ref[:, D:2 * D]) * keep_prev
        he_ref[HALO:, :] = val * sg
        _conv_weight_grad(dh1_ref, he_ref, sh_ref, HALO - (CONV_K - 1), dcw_ref, tm)
        dcw_ref[SUBLANES * CONV_K:, :] += jnp.sum(dh1_ref[...].reshape(tm // SUBLANES, SUBLANES, D), axis=0)

        dh0 = dh0_ref[...]
        dval = dh0 * sg
        dglu = dh0 * val * (sg * (1.0 - sg))
        dbin_ref[0:1, 0:D] += _colsum(dval)
        dbin_ref[0:1, D:2 * D] += _colsum(dglu)
        dp_ref[:, 0:D] = dval.astype(BF16)
        dp_ref[:, D:2 * D] = dglu.astype(BF16)

        @pl.when(i == t // tm - 1)
        def _():
            cps = [pltpu.make_async_copy(acc_ref.at[a, pl.ds((2 * j + h) * rows, rows)], sq_out[a].at[h, j],
                                         sq_sem.at[(a * N_CHIPS + j) * 2 + h])
                   for a in range(n_sq) for j in range(N_CHIPS) for h in range(2)]
            for cp in cps:
                cp.start()
            for a in range(n_sq):
                wire_ref[...] = acc_ref[a].astype(BF16)
                narrow = [pltpu.make_async_copy(wire_ref.at[pl.ds((2 * j + h) * rows, rows)], sq_wire[a].at[h, j],
                                                wire_sem.at[2 * j + h]) for j in range(N_CHIPS) for h in range(2)]
                for cp in narrow:
                    cp.start()
                for cp in narrow:
                    cp.wait()
            for cp in cps:
                cp.wait()

    any_spec = pl.BlockSpec(memory_space=pl.ANY)
    wide = pl.BlockSpec((span * tm, D), lambda i: (i // span, 0))
    return pl.pallas_call(
        body, name="conv_backward", grid=(t // tm,),
        in_specs=[pl.BlockSpec((tm, D), lambda i: (i, 0)),
                  pl.BlockSpec((HALO, D), lambda i: (jnp.minimum((i + 1) * hb, last), 0)),
                  pl.BlockSpec((tm, 2 * D), lambda i: (i, 0)),
                  pl.BlockSpec((HALO, 2 * D), lambda i: (jnp.maximum(i * hb - 1, 0), 0)),
                  _full((HALO, D)), any_spec] + [wide] * (2 * n_sq),
        out_specs=[pl.BlockSpec((tm, 2 * D), lambda i: (i, 0)), _full((SUBLANES * HALO, D)), _full((8, 2 * D))]
        + [any_spec] * (2 * n_sq),
        out_shape=[jax.ShapeDtypeStruct(dp.shape, BF16), jax.ShapeDtypeStruct((SUBLANES * HALO, D), F32),
                   jax.ShapeDtypeStruct((8, 2 * D), F32)]
        + [jax.ShapeDtypeStruct((2, N_CHIPS, rows, D), F32)] * n_sq
        + [jax.ShapeDtypeStruct((2, N_CHIPS, rows, D), BF16)] * n_sq,
        scratch_shapes=[pltpu.VMEM((tm + HALO, D), F32), pltpu.VMEM((tm + HALO, D), F32), pltpu.VMEM((tm, D), F32),
                        pltpu.VMEM((SUBLANES - 1, tm + SHIFT_ROWS, GROUP_W), F32), pltpu.VMEM((n_sq, D, D), F32),
                        pltpu.VMEM((D, D), BF16), pltpu.SemaphoreType.DMA((n_sq * N_CHIPS * 2,)),
                        pltpu.SemaphoreType.DMA((N_CHIPS * 2,))],
        input_output_aliases={5: 0},
        compiler_params=_params("arbitrary"),
    )(dh1, dh1, p, p, convw, dp, *[a for pair in pairs for a in pair])


def _grad_in_and_x(xt, dp, w4, dr, wires, grads):
    t = dr.shape[0]
    tm = TOKEN_TILE
    half, tn = D // 2, 512
    nb = W_BLOCK // tn
    n_w, n_x = 2 * N_CHIPS * nb, t // tm
    ns = len(grads)
    xi, yi, ci = lax.axis_index("x"), lax.axis_index("y"), lax.axis_index("c")
    others = [2 * (1 - xi) + yi, 2 * xi + (1 - yi), 2 * (1 - xi) + (1 - yi)]
    blocks = others + others + [2 * xi + yi] * 2
    halves = [1 - ci] * 3 + [ci] * 3 + [1 - ci, ci]
    table = jnp.stack([jnp.stack([b * nb + n for b in blocks for n in range(nb)]),
                       jnp.stack([h for h in halves for _ in range(nb)])]).astype(jnp.int32)

    def body(tab_ref, xt_ref, dpc_ref, dpr_ref, w_ref, dr_ref, *refs):
        parts, fulls = refs[:ns], refs[ns:2 * ns]
        dx_ref, qk_ref, b2_ref, b1_ref, wire_ref = refs[2 * ns:2 * ns + 5]
        lands, sibs = refs[2 * ns + 5:3 * ns + 5], refs[3 * ns + 5:4 * ns + 5]
        (g_ref, st_ref, sb_ref, tmp_ref, d2d_send, d2d_recv, ici_send, ici_recv, own_sem, tmp_sem, wire_sem,
         p_send, p_recv, s_send, s_recv) = refs[4 * ns + 5:]
        s = pl.program_id(0)
        x_, y_, c, k = _position()
        chips = _other_chips(x_, y_)
        n = s % nb
        grp = s // nb
        cols = pl.ds(pl.multiple_of(n * tn, tn), tn)

        def part(a, r, core):
            cx, cy = chips[r]
            return pltpu.make_async_remote_copy(
                src_ref=parts[a].at[core, 2 * cx + cy], dst_ref=lands[a].at[2 * r + c],
                send_sem=p_send.at[6 * a + 2 * r + core], recv_sem=p_recv.at[6 * a + 2 * r + c],
                device_id=(cx, cy, core), device_id_type=MESH)

        def landed(a, r, core):
            cx, cy = chips[r]
            return pltpu.make_async_remote_copy(
                src_ref=lands[a].at[2 * r + core], dst_ref=lands[a].at[2 * r + core],
                send_sem=p_send.at[6 * a + 2 * r + core], recv_sem=p_recv.at[6 * a + 2 * r + core],
                device_id=(cx, cy, core), device_id_type=MESH)

        def to_sibling_whole(a):
            return pltpu.make_async_remote_copy(
                src_ref=fulls[a].at[1 - c, k], dst_ref=sibs[a], send_sem=s_send.at[a], recv_sem=s_recv.at[a],
                device_id=(x_, y_, 1 - c), device_id_type=MESH)

        def to_sibling(slot, land):
            return pltpu.make_async_remote_copy(
                src_ref=st_ref.at[slot], dst_ref=b1_ref.at[land, :, cols], send_sem=d2d_send.at[slot],
                recv_sem=d2d_recv.at[land * nb + n], device_id=(x_, y_, 1 - c), device_id_type=MESH)

        def to_chip(r):
            cx, cy = chips[r]
            return pltpu.make_async_remote_copy(
                src_ref=wire_ref.at[r, :, cols], dst_ref=b2_ref.at[r, :, cols], send_sem=ici_send.at[r],
                recv_sem=ici_recv.at[r], device_id=(cx, cy, c), device_id_type=MESH)

        def all_of_chip(r):
            cx, cy = chips[r]
            return pltpu.make_async_remote_copy(
                src_ref=wire_ref.at[r], dst_ref=b2_ref.at[r], send_sem=ici_send.at[r],
                recv_sem=ici_recv.at[r], device_id=(cx, cy, c), device_id_type=MESH)

        def to_result(slot):
            return pltpu.make_async_copy(st_ref.at[slot], qk_ref.at[:, cols], own_sem.at[slot])

        def sibling_piece(land):
            return pltpu.make_async_copy(b1_ref.at[land, :, cols], tmp_ref, tmp_sem)

        @pl.when(s == 0)
        def _():
            for a in range(ns):
                to_sibling_whole(a).start()
                for r in range(3):
                    for core in range(2):
                        part(a, r, core).start()

        own_half = ((grp >= 3) & (grp <= 5)) | (grp == 7)
        land = jnp.where(grp == 7, 3, grp - 3)

        @pl.when(own_half)
        def _():
            to_sibling(0, land).wait_recv()
            sibling_piece(land).start()

        @pl.when(s < n_w)
        def _():
            g_ref[...] = _dot(xt_ref[tab_ref[1, s]], dpc_ref[...])

        @pl.when(own_half)
        def _():
            sibling_piece(land).wait()

        for g in range(2 * N_CHIPS):
            @pl.when(grp == g)
            def _(g=g):
                if g in (0, 1, 2, 6):
                    use = s if g < 3 else 3 * nb + n
                    slot = use % 2

                    @pl.when(use >= 2)
                    def _():
                        to_sibling(slot, 0).wait_send()

                    st_ref[slot] = g_ref[...]
                    to_sibling(slot, min(g, 3)).start()
                elif g in (3, 4, 5):
                    sb_ref[...] = (g_ref[...] + tmp_ref[...]).astype(BF16)
                    stage = pltpu.make_async_copy(sb_ref, wire_ref.at[g - 3, :, cols], wire_sem)
                    stage.start()
                    stage.wait()
                    to_chip(g - 3).start()
                else:
                    slot = n % 2
                    piece = g_ref[...] + tmp_ref[...]

                    @pl.when(n < 2)
                    def _():
                        to_sibling(slot, 0).wait_send()

                    @pl.when(n >= 2)
                    def _():
                        to_result(slot).wait()

                    st_ref[slot] = piece
                    to_result(slot).start()

        @pl.when(s >= n_w)
        def _():
            acc = ALPHA * dr_ref[...]
            for j in range(N_CHIPS):
                acc = acc + _dot_nt(dpr_ref[:, j * W_BLOCK:(j + 1) * W_BLOCK], w_ref[j])
            dx_ref[...] = acc

        @pl.when(s == n_w + n_x - 1)
        def _():
            for slot in range(2):
                to_result(slot).wait()
            for r in range(3):
                all_of_chip(r).wait_recv()
                all_of_chip(r).wait_send()
            for a in range(ns):
                to_sibling_whole(a).wait_recv()
                to_sibling_whole(a).wait_send()
                for r in range(3):
                    for core in range(2):
                        landed(a, r, core).wait_recv()
                        part(a, r, core).wait_send()

    any_spec = pl.BlockSpec(memory_space=pl.ANY)
    tile = lambda s, tab: (jnp.maximum(s - n_w, 0), 0)
    return pl.pallas_call(
        body, name="grad_in_and_x",
        grid_spec=pltpu.PrefetchScalarGridSpec(
            num_scalar_prefetch=1, grid=(n_w + n_x,),
            in_specs=[pl.BlockSpec((2, half, t), lambda s, tab: (0, 0, 0), pipeline_mode=pl.Buffered(1)),
                      pl.BlockSpec((t, tn), lambda s, tab: (0, tab[0, jnp.minimum(s, n_w - 1)])),
                      pl.BlockSpec((tm, D_IN), tile),
                      pl.BlockSpec((N_CHIPS, D, W_BLOCK), lambda s, tab: (0, 0, 0), pipeline_mode=pl.Buffered(1)),
                      pl.BlockSpec((tm, D), tile)] + [any_spec] * (2 * ns),
            out_specs=[pl.BlockSpec((tm, D), tile)] + [any_spec] * (4 + 2 * ns),
            scratch_shapes=[pltpu.VMEM((half, tn), F32), pltpu.VMEM((2, half, tn), F32), pltpu.VMEM((half, tn), BF16),
                            pltpu.VMEM((half, tn), F32),
                            pltpu.SemaphoreType.DMA((2,)), pltpu.SemaphoreType.DMA((N_CHIPS * nb,)),
                            pltpu.SemaphoreType.DMA((3,)), pltpu.SemaphoreType.DMA((3,)),
                            pltpu.SemaphoreType.DMA((2,)), pltpu.SemaphoreType.DMA, pltpu.SemaphoreType.DMA,
                            pltpu.SemaphoreType.DMA((6 * ns,)), pltpu.SemaphoreType.DMA((6 * ns,)),
                            pltpu.SemaphoreType.DMA((ns,)), pltpu.SemaphoreType.DMA((ns,))]),
        out_shape=[jax.ShapeDtypeStruct((t, D), F32), jax.ShapeDtypeStruct((half, W_BLOCK), F32),
                   jax.ShapeDtypeStruct((3, half, W_BLOCK), BF16), jax.ShapeDtypeStruct((N_CHIPS, half, W_BLOCK), F32),
                   jax.ShapeDtypeStruct((3, half, W_BLOCK), BF16)]
        + [jax.ShapeDtypeStruct((6,) + w.shape[2:], w.dtype) for w in wires]
        + [jax.ShapeDtypeStruct(g.shape[2:], F32) for g in grads],
        compiler_params=_params("arbitrary"),
    )(table, xt, dp, dp, w4, dr, *wires, *grads)


def kernel(x, w_in, b_in, conv_w, conv_b, gn_g, gn_b, ln_v_g, ln_v_b, w_spatial, b_spatial, w_pa, w_pb, w_o, b_o, ln_out_g, ln_out_b, loss_target, m_w_in, m_b_in, m_conv_w, m_conv_b, m_gn_g, m_gn_b, m_ln_v_g, m_ln_v_b, m_w_spatial, m_b_spatial, m_w_pa, m_w_pb, m_w_o, m_b_o, m_ln_out_g, m_ln_out_b, v_w_in, v_b_in, v_conv_w, v_conv_b, v_gn_g, v_gn_b, v_ln_v_g, v_ln_v_b, v_w_spatial, v_b_spatial, v_w_pa, v_w_pb, v_w_o, v_b_o, v_ln_out_g, v_ln_out_b):
    n_seq, seq, _ = x.shape
    t = n_seq * seq
    tiles_per_seq = seq // TOKEN_TILE
    x2 = x.reshape(t, D)
    tgt = loss_target.reshape(t, D)

    conv_shard = jnp.pad(conv_w, ((0, HALO - CONV_K), (0, 0)))
    p, win4, wpa4, wpb4, wo4, conv4 = _proj_gather(
        x2, b_in.reshape(N_CHIPS, 1, W_BLOCK),
        [_place_shard(w_in, 256, BF16), _place_shard(w_pa, 128, BF16), _place_shard(w_pb, 128, BF16),
         _place_shard(w_o, 128, BF16), _place_shard(conv_shard, HALO // 2, F32)])
    win4 = win4.reshape(N_CHIPS, D, W_BLOCK)
    wpa, wpb, wo = wpa4.reshape(D, D), wpb4.reshape(D, D), wo4.reshape(D, D)
    convw = conv4.reshape(N_CHIPS, HALO, D // N_CHIPS).transpose(1, 0, 2).reshape(HALO, D)

    vecs = jnp.stack([conv_b, gn_g, gn_b, ln_v_g, ln_v_b, b_o, ln_out_g, ln_out_b])
    causal = jnp.tril(jnp.ones((CHUNK, CHUNK), bool))
    ws = jnp.where(causal[None], w_spatial, 0.0)
    ws_bf, wst_bf = ws.astype(BF16), ws.transpose(0, 2, 1).astype(BF16)
    bsp = jnp.repeat(b_spatial.T, GROUP_W, axis=1)

    h1, ya, yb, h3, s, mixed, dr, drb, xt, acc_f = _forward_tiles(p, x2, tgt, wpa, wpb, wo, convw, vecs, ws_bf, bsp, tiles_per_seq)
    dh1, dp, dya, dyb, acc_b, dbin_b, dws, dbsp_acc = _backward_tiles(p, h1, ya, yb, drb, wpa, wpb, wo, vecs, ws_bf, wst_bf, bsp)
    dp, dcw8, dbin_a, *square = _conv_backward(dh1, p, dp, convw, [(h3, dya), (s, dyb), (mixed, drb)], tiles_per_seq)

    small, g_conv = _pack_small(acc_f, acc_b, dbin_a, dbin_b, dcw8, dws, dbsp_acc)
    small = small.reshape(2, N_CHIPS, SMALL_ROWS // 8, D)

    grads = square[:3] + [g_conv, small]
    wires = square[3:] + [g_conv, small]
    grad_x, q_in, chips_in, _, _, *landed = _grad_in_and_x(xt.reshape(2, D // 2, t), dp, win4, dr, wires, grads)
    grad_x = grad_x.reshape(x.shape)
    mine = [_add_chips(q_in, chips_in)]
    mine += [_add_devices(g, land, sib, a == 4) for a, (g, land, sib) in enumerate(zip(grads, landed[:5], landed[5:]))]
    *full, small_parts = _share_results(mine[:5], mine[5])
    grad_w_in, grad_w_pa, grad_w_pb, grad_w_o = [f.reshape(w.shape) for f, w in zip(full[:4], (w_in, w_pa, w_pb, w_o))]
    grad_conv_w = full[4].reshape(HALO, D // N_CHIPS)[:CONV_K]

    big = {}
    for name, w, g, m, v in [("w_in", w_in, grad_w_in, m_w_in, v_w_in), ("w_pa", w_pa, grad_w_pa, m_w_pa, v_w_pa),
                             ("w_pb", w_pb, grad_w_pb, m_w_pb, v_w_pb), ("w_o", w_o, grad_w_o, m_w_o, v_w_o),
                             ("conv_w", conv_w, grad_conv_w, m_conv_w, v_conv_w)]:
        big[name] = (g,) + tuple(_adamw(w, g, m, v))
    vec_names = ["conv_b", "gn_g", "gn_b", "ln_v_g", "ln_v_b", "b_o", "ln_out_g", "ln_out_b"]
    vec_triples = [(conv_b, m_conv_b, v_conv_b), (gn_g, m_gn_g, v_gn_g), (gn_b, m_gn_b, v_gn_b),
                   (ln_v_g, m_ln_v_g, v_ln_v_g), (ln_v_b, m_ln_v_b, v_ln_v_b), (b_o, m_b_o, v_b_o),
                   (ln_out_g, m_ln_out_g, v_ln_out_g), (ln_out_b, m_ln_out_b, v_ln_out_b)]
    small_res, loss8 = _adamw_small(
        small_parts.reshape(8, SMALL_ROWS // 8, D), vec_triples, (b_in, m_b_in, v_b_in),
        (w_spatial, m_w_spatial, v_w_spatial), (b_spatial, m_b_spatial, v_b_spatial))
    per_name = dict(zip(vec_names + ["b_in", "w_spatial", "b_spatial"], small_res))

    order = ["w_in", "b_in", "conv_w", "conv_b", "gn_g", "gn_b", "ln_v_g", "ln_v_b", "w_spatial", "b_spatial",
             "w_pa", "w_pb", "w_o", "b_o", "ln_out_g", "ln_out_b"]
    outs = [loss8[0, 0], grad_x]
    for kind in range(4):
        outs += [big[n][kind] if n in big else per_name[n][kind] for n in order]
    return tuple(outs)
```

```python
import functools
import math

import jax
import jax.numpy as jnp
from jax import lax
from jax.experimental import pallas as pl
from jax.experimental.pallas import tpu as pltpu

D = 1024
N_GROUPS = 8
GROUP_W = D // N_GROUPS
CHUNK = 128
CONV_K = 31
HALO = 32
D_IN = 8 * D
N_CHIPS = 4
W_BLOCK = D_IN // N_CHIPS
ALPHA = 2.0 ** 0.25
LN_EPS = 1e-5
ADAM_LR, ADAM_B1, ADAM_B2, ADAM_EPS, ADAM_WD, ADAM_STEP = 0.001, 0.9, 0.999, 1e-08, 0.01, 10

TOKEN_TILE = 256
VMEM_LIMIT = 56 * 1024 * 1024
MESH = pl.DeviceIdType.MESH
F32, BF16 = jnp.float32, jnp.bfloat16


def _sigmoid(x):
    return 1.0 / (1.0 + jnp.exp(-x))


def _gelu(x):
    c = math.sqrt(2.0 / math.pi)
    t = jnp.tanh(c * (x + 0.044715 * (x * x * x)))
    return x * (0.5 * (1.0 + t))


def _gelu_and_grad(x):
    c = math.sqrt(2.0 / math.pi)
    x2 = x * x
    t = jnp.tanh(c * (x + 0.044715 * (x2 * x)))
    cdf = 0.5 * (1.0 + t)
    return x * cdf, cdf + 0.5 * x * (1.0 - t * t) * (c * (1.0 + 3.0 * 0.044715 * x2))


def _norm_stats(v):
    mu = jnp.mean(v, axis=-1, keepdims=True)
    vc = v - mu
    var = jnp.mean(vc * vc, axis=-1, keepdims=True)
    rstd = lax.rsqrt(var + LN_EPS)
    return vc * rstd, rstd


def _norm_bwd(dxhat, xhat, rstd):
    m1 = jnp.mean(dxhat, axis=-1, keepdims=True)
    m2 = jnp.mean(dxhat * xhat, axis=-1, keepdims=True)
    return rstd * (dxhat - m1 - xhat * m2)


def _dot(a, b):
    return jnp.dot(a, b, preferred_element_type=F32)


def _dot_nt(a, b):
    return lax.dot_general(a, b, (((1,), (1,)), ((), ())), preferred_element_type=F32)


def _dot_tn(a, b):
    return lax.dot_general(a, b, (((0,), (0,)), ((), ())), preferred_element_type=F32)


def _colsum(v):
    return jnp.sum(v, axis=0, keepdims=True)


def _full(shape):
    return pl.BlockSpec(shape, lambda *_: (0,) * len(shape))


def _resident(shape):
    return pl.BlockSpec(shape, lambda *_: (0,) * len(shape), pipeline_mode=pl.Buffered(1))


def _params(*sem):
    return pltpu.CompilerParams(dimension_semantics=sem, vmem_limit_bytes=VMEM_LIMIT)


def _chip_index():
    return (2 * lax.axis_index("x") + lax.axis_index("y")).astype(jnp.int32).reshape(1)


def _core_index():
    return lax.axis_index("c").astype(jnp.int32).reshape(1)


def _place_shard(w, rows, dtype):
    r, c = w.shape
    steps = r // 2 // rows

    def body(k_ref, w_ref, o_ref):
        o_ref[...] = w_ref[...].astype(dtype)

    return pl.pallas_call(
        body, name="place_shard",
        grid_spec=pltpu.PrefetchScalarGridSpec(
            num_scalar_prefetch=1, grid=(2, steps),
            in_specs=[pl.BlockSpec((rows, c), lambda h, i, k: (h * steps + i, 0))],
            out_specs=pl.BlockSpec((None, None, rows, c), lambda h, i, k: (k[0], h, i, 0))),
        out_shape=jax.ShapeDtypeStruct((N_CHIPS, 2, r // 2, c), dtype),
        compiler_params=_params("parallel", "parallel"),
    )(_chip_index(), w)


def _position():
    x, y, c = lax.axis_index("x"), lax.axis_index("y"), lax.axis_index("c")
    return x, y, c, 2 * x + y


def _other_chips(x, y):
    return [(1 - x, y), (x, 1 - y), (1 - x, 1 - y)]


def _any_specs(n):
    return [pl.BlockSpec(memory_space=pl.ANY)] * n


def _proj_gather(x, b4, bufs):
    t = x.shape[0]
    tm = 1024
    steps = t // tm
    half = D // 2
    n = len(bufs)
    xi, yi = lax.axis_index("x"), lax.axis_index("y")
    order = jnp.stack([2 * xi + yi, 2 * (1 - xi) + yi, 2 * xi + (1 - yi), 2 * (1 - xi) + (1 - yi)]).astype(jnp.int32)

    def body(order_ref, x_ref, b_ref, *refs):
        p_ref, outs = refs[n], refs[n + 1:2 * n + 1]
        xb_ref, w_ref, lsem, send, recv, hop_send, hop_recv, fsend, frecv = refs[2 * n + 1:]
        jj, i = pl.program_id(0), pl.program_id(1)
        x_, y_, c, k = _position()
        nbrs = [(1 - x_, y_), (x_, 1 - y_)]
        blocks = [2 * (1 - x_) + y_, 2 * x_ + (1 - y_), 2 * (1 - x_) + (1 - y_)]

        def quarter(a, block, q):
            rows = outs[a].shape[2] // 2
            return outs[a].at[block, c, pl.ds(q * rows, rows)]

        def copy(ref, to, send_sem, recv_sem):
            return pltpu.make_async_remote_copy(src_ref=ref, dst_ref=ref, send_sem=send_sem, recv_sem=recv_sem,
                                                device_id=(to[0], to[1], c), device_id_type=MESH)

        def sent(a, nb, q):
            return copy(quarter(a, k, q), nbrs[nb], send.at[4 * a + 2 * nb + q], recv.at[4 * a + 2 * nb + q])

        def landed(a, nb, q):
            return copy(quarter(a, blocks[nb], q), nbrs[nb], send.at[4 * a + 2 * nb + q], recv.at[4 * a + 2 * nb + q])

        def hopped(a, nb):
            return copy(quarter(a, blocks[nb], nb), nbrs[1 - nb], hop_send.at[2 * a + nb], hop_recv.at[2 * a + 1 - nb])

        def from_diagonal(a, via):
            return copy(quarter(a, blocks[2], 1 - via), nbrs[via], hop_send.at[2 * a + via], hop_recv.at[2 * a + via])

        def passed(a, r, h):
            return pltpu.make_async_remote_copy(
                src_ref=outs[a].at[blocks[r], h], dst_ref=outs[a].at[blocks[r], h], send_sem=fsend.at[3 * a + r],
                recv_sem=frecv.at[3 * a + r], device_id=(x_, y_, 1 - c), device_id_type=MESH)

        def load(block):
            cps = [pltpu.make_async_copy(outs[0].at[block, h], w_ref.at[pl.ds(h * half, half)], lsem.at[h])
                   for h in range(2)]
            for cp in cps:
                cp.start()
            for cp in cps:
                cp.wait()

        def pass_on(arrays):
            for a in arrays:
                for nb in range(2):
                    landed(a, nb, nb).wait_recv()
                    hopped(a, nb).start()

        @pl.when((jj == 0) & (i == 0))
        def _():
            for a in range(n):
                for nb, q in ((0, 0), (1, 1), (0, 1), (1, 0)):
                    sent(a, nb, q).start()
            load(k)

        @pl.when((jj == 1) & (i == 0))
        def _():
            pass_on([0])
            landed(0, 0, 1).wait_recv()
            passed(0, 0, c).start()
            passed(0, 0, 1 - c).wait_recv()
            load(blocks[0])

        @pl.when((jj == 2) & (i == 0))
        def _():
            landed(0, 1, 0).wait_recv()
            passed(0, 1, c).start()
            passed(0, 1, 1 - c).wait_recv()
            load(blocks[1])

        @pl.when((jj == 3) & (i == 0))
        def _():
            from_diagonal(0, 0).wait_recv()
            from_diagonal(0, 1).wait_recv()
            passed(0, 2, c).start()
            passed(0, 2, 1 - c).wait_recv()
            load(blocks[2])
            pass_on(range(1, n))

        rows = pl.ds(pl.multiple_of(i * tm, tm), tm)

        @pl.when(jj == 0)
        def _():
            xb_ref[rows, :] = x_ref[...].astype(BF16)

        p_ref[...] = _dot(xb_ref[rows, :], w_ref[...]) + b_ref[...]

        @pl.when((jj == N_CHIPS - 1) & (i == steps - 1))
        def _():
            for a in range(1, n):
                landed(a, 0, 1).wait_recv()
                passed(a, 0, c).start()
                landed(a, 1, 0).wait_recv()
                passed(a, 1, c).start()
                from_diagonal(a, 0).wait_recv()
                from_diagonal(a, 1).wait_recv()
                passed(a, 2, c).start()
            for a in range(1, n):
                for r in range(3):
                    passed(a, r, 1 - c).wait_recv()
            for a in range(n):
                for nb in range(2):
                    for q in range(2):
                        sent(a, nb, q).wait_send()
                    hopped(a, nb).wait_send()
                for r in range(3):
                    passed(a, r, c).wait_send()

    any_spec = pl.BlockSpec(memory_space=pl.ANY)
    return pl.pallas_call(
        body, name="proj_gather",
        grid_spec=pltpu.PrefetchScalarGridSpec(
            num_scalar_prefetch=1, grid=(N_CHIPS, steps),
            in_specs=[pl.BlockSpec((tm, D), lambda jj, i, o: (jnp.where(jj == 0, i, steps - 1), 0)),
                      pl.BlockSpec((None, 1, W_BLOCK), lambda jj, i, o: (o[jj], 0, 0))] + [any_spec] * n,
            out_specs=[pl.BlockSpec((tm, W_BLOCK), lambda jj, i, o: (i, o[jj]))] + [any_spec] * n,
            scratch_shapes=[pltpu.VMEM((t, D), BF16), pltpu.VMEM((D, W_BLOCK), BF16), pltpu.SemaphoreType.DMA((2,)),
                            pltpu.SemaphoreType.DMA((4 * n,)), pltpu.SemaphoreType.DMA((4 * n,)),
                            pltpu.SemaphoreType.DMA((2 * n,)), pltpu.SemaphoreType.DMA((2 * n,)),
                            pltpu.SemaphoreType.DMA((3 * n,)), pltpu.SemaphoreType.DMA((3 * n,))]),
        out_shape=[jax.ShapeDtypeStruct((t, D_IN), F32)] + [jax.ShapeDtypeStruct(b.shape, b.dtype) for b in bufs],
        input_output_aliases={3 + a: 1 + a for a in range(n)},
        compiler_params=_params("arbitrary", "arbitrary"),
    )(order, x, b4, *bufs)


def _share_results(bufs, small):
    n = len(bufs)

    def body(*refs):
        outs, small_out = refs[n + 1:2 * n + 1], refs[2 * n + 1]
        send, recv, ssend, srecv = refs[2 * n + 2:]
        x, y, c, k = _position()
        cps = []
        for a in range(n):
            cp = pltpu.make_async_remote_copy(
                src_ref=outs[a].at[c], dst_ref=outs[a].at[c], send_sem=send.at[a], recv_sem=recv.at[a],
                device_id=(x, y, 1 - c), device_id_type=MESH)
            cp.start()
            cps.append(cp)
        waits = []
        for p in range(1, 8):
            px, py, pc = x ^ (p >> 2), y ^ ((p >> 1) & 1), c ^ (p & 1)
            cp = pltpu.make_async_remote_copy(
                src_ref=small_out.at[k, c], dst_ref=small_out.at[k, c], send_sem=ssend.at[p - 1],
                recv_sem=srecv.at[p - 1], device_id=(px, py, pc), device_id_type=MESH)
            cp.start()
            cps.append(cp)
            waits.append(pltpu.make_async_remote_copy(
                src_ref=small_out.at[2 * px + py, pc], dst_ref=small_out.at[2 * px + py, pc], send_sem=ssend.at[p - 1],
                recv_sem=srecv.at[p - 1], device_id=(px, py, pc), device_id_type=MESH))
        for a in range(n):
            pltpu.make_async_remote_copy(
                src_ref=outs[a].at[1 - c], dst_ref=outs[a].at[1 - c], send_sem=send.at[a], recv_sem=recv.at[a],
                device_id=(x, y, 1 - c), device_id_type=MESH).wait_recv()
        for w in waits:
            w.wait_recv()
        for cp in cps:
            cp.wait_send()

    return pl.pallas_call(
        body, name="rs_share_results",
        in_specs=_any_specs(n + 1), out_specs=_any_specs(n + 1),
        out_shape=[jax.ShapeDtypeStruct(b.shape, b.dtype) for b in bufs + [small]],
        scratch_shapes=[pltpu.SemaphoreType.DMA((n,)), pltpu.SemaphoreType.DMA((n,)),
                        pltpu.SemaphoreType.DMA((7,)), pltpu.SemaphoreType.DMA((7,))],
        input_output_aliases={a: a for a in range(n + 1)},
    )(*bufs, small)


def _row_tile(r, c):
    t = max(8, min(r, (1 << 18) // c))
    while r % t:
        t //= 2
    return t


def _add_devices(g, lands, sib, per_device):
    r, c = g.shape[-2:]
    t = _row_tile(r, c)

    def body(kc_ref, g_ref, l_ref, s_ref, f_ref):
        f = g_ref[...] + s_ref[...]
        for i in range(l_ref.shape[0]):
            f = f + l_ref[i].astype(F32)
        f_ref[...] = f

    if per_device:
        out_spec = pl.BlockSpec((None, None, t, c), lambda i, kc: (kc[0], kc[1], i, 0))
        out_shape = jax.ShapeDtypeStruct((N_CHIPS, 2, r, c), F32)
    else:
        out_spec = pl.BlockSpec((None, t, c), lambda i, kc: (kc[1], i, 0))
        out_shape = jax.ShapeDtypeStruct((2, r, c), F32)
    return pl.pallas_call(
        body, name="rs_add_devices",
        grid_spec=pltpu.PrefetchScalarGridSpec(
            num_scalar_prefetch=1, grid=(r // t,),
            in_specs=[pl.BlockSpec((None, None, t, c), lambda i, kc: (kc[1], kc[0], i, 0)),
                      pl.BlockSpec((lands.shape[0], t, c), lambda i, kc: (0, i, 0)),
                      pl.BlockSpec((t, c), lambda i, kc: (i, 0))],
            out_specs=out_spec),
        out_shape=out_shape,
        compiler_params=_params("parallel"),
    )(jnp.concatenate([_chip_index(), _core_index()]), g, lands, sib)


def _add_chips(q, b2):
    r, c = q.shape
    t = _row_tile(r, c)

    def body(c_ref, q_ref, b_ref, f_ref):
        f_ref[...] = ((q_ref[...] + b_ref[0].astype(F32)) + b_ref[1].astype(F32)) + b_ref[2].astype(F32)

    return pl.pallas_call(
        body, name="rs_add_chips",
        grid_spec=pltpu.PrefetchScalarGridSpec(
            num_scalar_prefetch=1, grid=(r // t,),
            in_specs=[pl.BlockSpec((t, c), lambda i, cr: (i, 0)), pl.BlockSpec((3, t, c), lambda i, cr: (0, i, 0))],
            out_specs=pl.BlockSpec((None, t, c), lambda i, cr: (cr[0], i, 0))),
        out_shape=jax.ShapeDtypeStruct((2, r, c), F32),
        compiler_params=_params("parallel"),
    )(_core_index(), q, b2)


def _adamw_math(w, g, m, v):
    m = ADAM_B1 * m + (1.0 - ADAM_B1) * g
    v = ADAM_B2 * v + (1.0 - ADAM_B2) * (g * g)
    m_hat = m / (1.0 - ADAM_B1 ** ADAM_STEP)
    v_hat = v / (1.0 - ADAM_B2 ** ADAM_STEP)
    delta = -ADAM_LR * (m_hat / (jnp.sqrt(v_hat) + ADAM_EPS) + ADAM_WD * w)
    return delta, m, v


def _adamw(w, g, m, v):
    r, c = w.shape
    t = _row_tile(r, c) if r % 8 == 0 else r

    def body(w_ref, g_ref, m_ref, v_ref, d_ref, nm_ref, nv_ref):
        d_ref[...], nm_ref[...], nv_ref[...] = _adamw_math(w_ref[...], g_ref[...], m_ref[...], v_ref[...])

    spec = pl.BlockSpec((t, c), lambda i: (i, 0))
    return pl.pallas_call(
        body, name="adamw", grid=(r // t,), in_specs=[spec] * 4, out_specs=[spec] * 3,
        out_shape=[jax.ShapeDtypeStruct((r, c), F32)] * 3, compiler_params=_params("parallel"),
    )(w, g, m, v)


ROW_B_IN = 0
ROW_VECS = 8
ROW_LOSS = 16
ROW_B_SPATIAL = 24
ROW_W_SPATIAL = 32
SMALL_ROWS = 192
N_VECS = 8


def _pack_small(acc_f, acc_b, dbin_a, dbin_b, dcw8, dws, dbsp):
    cols = D // N_CHIPS

    def body(af_ref, ab_ref, da_ref, db_ref, cw_ref, ws_ref, bs_ref, o_ref, gc_ref):
        o_ref[...] = jnp.zeros_like(o_ref)
        for j in range(D_IN // D):
            src = da_ref if j < 2 else db_ref
            o_ref[ROW_B_IN + j:ROW_B_IN + j + 1, :] = src[0:1, j * D:(j + 1) * D]
        dcw = jnp.sum(cw_ref[...].reshape(HALO, SUBLANES, D), axis=1)
        o_ref[ROW_VECS:ROW_VECS + 1, :] = dcw[CONV_K:CONV_K + 1]
        o_ref[ROW_VECS + 1:ROW_VECS + 5, :] = ab_ref[0:4, :]
        o_ref[ROW_VECS + 5:ROW_VECS + 6, :] = af_ref[2:3, :]
        o_ref[ROW_VECS + 6:ROW_VECS + 8, :] = af_ref[0:2, :]
        o_ref[ROW_LOSS:ROW_LOSS + 1, :] = af_ref[3:4, :]
        head = lax.broadcasted_iota(jnp.int32, (N_GROUPS, D), 0)
        lane = lax.broadcasted_iota(jnp.int32, (N_GROUPS, D), 1)
        indicator = jnp.where(lane // GROUP_W == head, 1.0, 0.0)
        o_ref[ROW_B_SPATIAL:ROW_B_SPATIAL + N_GROUPS, 0:CHUNK] = lax.dot_general(
            indicator, bs_ref[...], (((1,), (1,)), ((), ())), precision=lax.Precision.HIGHEST, preferred_element_type=F32)
        t_idx = lax.broadcasted_iota(jnp.int32, (CHUNK, D), 0)
        s_idx = lax.broadcasted_iota(jnp.int32, (CHUNK, D), 1) % CHUNK
        o_ref[ROW_W_SPATIAL:ROW_W_SPATIAL + CHUNK, :] = jnp.where(s_idx <= t_idx, ws_ref[...], 0.0)
        for h in range(2):
            for j in range(N_CHIPS):
                gc_ref[h, j] = dcw[h * (HALO // 2):(h + 1) * (HALO // 2), j * cols:(j + 1) * cols]

    ins = [acc_f, acc_b, dbin_a, dbin_b, dcw8, dws, dbsp]
    return pl.pallas_call(
        body, name="pack_small",
        in_specs=[_full(a.shape) for a in ins],
        out_specs=[_full((SMALL_ROWS, D)), _full((2, N_CHIPS, HALO // 2, cols))],
        out_shape=[jax.ShapeDtypeStruct((SMALL_ROWS, D), F32), jax.ShapeDtypeStruct((2, N_CHIPS, HALO // 2, cols), F32)],
        compiler_params=_params(),
    )(*ins)


def _adamw_small(parts, vecs, b_in, w_spatial, b_spatial):
    triples = list(vecs) + [b_in, w_spatial, b_spatial]
    n_in = 3 * len(triples)

    def body(p_ref, *refs):
        ins = [refs[3 * i:3 * i + 3] for i in range(len(triples))]
        outs = [refs[n_in + 4 * i:n_in + 4 * i + 4] for i in range(len(triples))]
        loss_ref, g_ref = refs[n_in + 4 * len(triples):]
        rows = SMALL_ROWS // 8
        for k in range(N_CHIPS):
            for core in range(2):
                g_ref[(core * N_CHIPS + k) * rows:(core * N_CHIPS + k + 1) * rows, :] = p_ref[2 * k + core]

        def step(g, wmv, out, get, put):
            d, nm, nv = _adamw_math(get(wmv[0]), g, get(wmv[1]), get(wmv[2]))
            for o, val in zip(out, (g, d, nm, nv)):
                put(o, val)

        for i in range(N_VECS):
            step(g_ref[ROW_VECS + i:ROW_VECS + i + 1, :], ins[i], outs[i],
                 lambda r: r[...].reshape(1, D), lambda o, val: o.__setitem__(Ellipsis, val.reshape(D)))
        for j in range(D_IN // D):
            piece = pl.ds(j * D, D)
            step(g_ref[ROW_B_IN + j:ROW_B_IN + j + 1, :], ins[N_VECS], outs[N_VECS],
                 lambda r: r[piece].reshape(1, D), lambda o, val: o.__setitem__(piece, val.reshape(D)))
        for h in range(N_GROUPS):
            step(g_ref[ROW_W_SPATIAL:ROW_W_SPATIAL + CHUNK, h * CHUNK:(h + 1) * CHUNK], ins[N_VECS + 1], outs[N_VECS + 1],
                 lambda r: r[h], lambda o, val: o.__setitem__(h, val))
        step(g_ref[ROW_B_SPATIAL:ROW_B_SPATIAL + N_GROUPS, 0:CHUNK], ins[N_VECS + 2], outs[N_VECS + 2],
             lambda r: r[...], lambda o, val: o.__setitem__(Ellipsis, val))
        lanes = g_ref[ROW_LOSS:ROW_LOSS + 1, :]
        loss_ref[...] = jnp.broadcast_to(jnp.sum(lanes, axis=1, keepdims=True), (8, 128))

    flat = [a for tr in triples for a in tr]
    out_shape = [jax.ShapeDtypeStruct(tr[0].shape, F32) for tr in triples for _ in range(4)]
    out_shape.append(jax.ShapeDtypeStruct((8, 128), F32))
    res = pl.pallas_call(
        body, name="adamw_small",
        in_specs=[_full(parts.shape)] + [_full(a.shape) for a in flat],
        out_specs=[_full(o.shape) for o in out_shape],
        out_shape=out_shape,
        scratch_shapes=[pltpu.VMEM((SMALL_ROWS, D), F32)],
        compiler_params=_params(),
    )(parts, *flat)
    return [res[4 * i:4 * i + 4] for i in range(len(triples))], res[-1]


SUBLANES = 8
SHIFT_ROWS = HALO - SUBLANES


def _shifted_copies(src_ref, sh_ref, cs, tm):
    for p in range(1, SUBLANES):
        sh_ref[p - 1] = src_ref[pl.ds(p, tm + SHIFT_ROWS), cs]


def _tap(src_ref, sh_ref, cs, offset, start, rows):
    p, q = offset % SUBLANES, offset // SUBLANES
    if p == 0:
        return src_ref[pl.ds(start + SUBLANES * q, rows), cs]
    return sh_ref[p - 1, pl.ds(start + SUBLANES * q, rows), :]


def _conv_taps(src_ref, sh_ref, w_ref, first_offset, step, bias, dst_ref, tm):
    rows = 64
    for g in range(N_GROUPS):
        cs = slice(g * GROUP_W, (g + 1) * GROUP_W)
        _shifted_copies(src_ref, sh_ref, cs, tm)
        for rb in range(tm // rows):
            acc = jnp.zeros((rows, GROUP_W), F32) + (bias[:, cs] if bias is not None else 0.0)
            for k in range(CONV_K):
                acc = acc + w_ref[k:k + 1, cs] * _tap(src_ref, sh_ref, cs, first_offset + step * k, rb * rows, rows)
            dst_ref[rb * rows:(rb + 1) * rows, cs] = acc


def _conv_weight_grad(d_ref, src_ref, sh_ref, first_offset, acc_ref, tm):
    rows = 64
    for g in range(N_GROUPS):
        cs = slice(g * GROUP_W, (g + 1) * GROUP_W)
        _shifted_copies(src_ref, sh_ref, cs, tm)
        for rb in range(tm // rows):
            d = d_ref[rb * rows:(rb + 1) * rows, cs]
            for k in range(CONV_K):
                prod = d * _tap(src_ref, sh_ref, cs, first_offset + k, rb * rows, rows)
                acc_ref[SUBLANES * k:SUBLANES * (k + 1), cs] += jnp.sum(
                    prod.reshape(rows // SUBLANES, SUBLANES, GROUP_W), axis=0)


def _spatial_mix(w_ref, v_bf, tm):
    rows = []
    for q in range(tm // CHUNK):
        cols = [_dot(w_ref[h], v_bf[q * CHUNK:(q + 1) * CHUNK, h * GROUP_W:(h + 1) * GROUP_W])
                for h in range(N_GROUPS)]
        rows.append(jnp.concatenate(cols, axis=1))
    return jnp.concatenate(rows, axis=0)


def _group_norm_fwd(h1, gn_g, gn_b):
    xhat, rstd = [], []
    for g in range(N_GROUPS):
        xh, rs = _norm_stats(h1[:, g * GROUP_W:(g + 1) * GROUP_W])
        xhat.append(xh)
        rstd.append(rs)
    xhat = jnp.concatenate(xhat, axis=1)
    return xhat * gn_g + gn_b, xhat, rstd


def _forward_tiles(p, x, tgt, wpa, wpb, wo, convw, vecs, ws, bsp, tiles_per_seq):
    t = x.shape[0]
    tm = TOKEN_TILE
    hb = tm // HALO

    def body(p_ref, ph_ref, x_ref, t_ref, wpa_ref, wpb_ref, wo_ref, cw_ref, vec_ref, ws_ref, bsp_ref,
             h1_ref, ya_ref, yb_ref, h3_ref, s_ref, mx_ref, dr_ref, drb_ref, xt_ref, acc_ref, he_ref, sh_ref):
        i = pl.program_id(0)
        xt_ref[...] = x_ref[...].T.astype(BF16)
        conv_b, gn_g, gn_b, lnv_g, lnv_b, b_o, lno_g, lno_b = [vec_ref[j:j + 1, :] for j in range(8)]

        keep = jnp.where(i % tiles_per_seq == 0, 0.0, 1.0)
        he_ref[0:HALO, :] = ph_ref[:, 0:D] * _sigmoid(ph_ref[:, D:2 * D]) * keep
        he_ref[HALO:, :] = p_ref[:, 0:D] * _sigmoid(p_ref[:, D:2 * D])
        _conv_taps(he_ref, sh_ref, cw_ref, HALO - (CONV_K - 1), 1, conv_b, h1_ref, tm)
        h2, _, _ = _group_norm_fwd(h1_ref[...], gn_g, gn_b)
        a_gate = p_ref[:, 2 * D:3 * D]
        h3 = ((h2 * _sigmoid(h2)) * (a_gate * _sigmoid(a_gate))).astype(BF16)
        h3_ref[...] = h3
        ya = _dot(h3, wpa_ref[...])
        ya_ref[...] = ya

        u = _gelu(p_ref[:, 3 * D:4 * D])
        vhat, _ = _norm_stats(_gelu(p_ref[:, 4 * D:5 * D]))
        v1 = (vhat * lnv_g + lnv_b).astype(BF16)
        b_gate = p_ref[:, 5 * D:6 * D]
        vmix = _spatial_mix(ws_ref, v1, tm) + jnp.concatenate([bsp_ref[...]] * (tm // CHUNK), axis=0)
        s = (u * vmix * (b_gate * _sigmoid(b_gate))).astype(BF16)
        s_ref[...] = s
        yb = _dot(s, wpb_ref[...])
        yb_ref[...] = yb

        mixed = (_sigmoid(p_ref[:, 6 * D:7 * D]) * ya + _sigmoid(p_ref[:, 7 * D:8 * D]) * yb).astype(BF16)
        mx_ref[...] = mixed
        r = ALPHA * x_ref[...] + (_dot(mixed, wo_ref[...]) + b_o)
        xhat, rstd = _norm_stats(r)
        err = (xhat * lno_g + lno_b) - t_ref[...]
        dout = err * (1.0 / D)
        dr = _norm_bwd(dout * lno_g, xhat, rstd)
        dr_ref[...] = dr
        drb_ref[...] = dr.astype(BF16)

        @pl.when(i == 0)
        def _():
            acc_ref[...] = jnp.zeros_like(acc_ref)

        acc_ref[0:1, :] += _colsum(dout * xhat)
        acc_ref[1:2, :] += _colsum(dout)
        acc_ref[2:3, :] += _colsum(dr)
        acc_ref[3:4, :] += _colsum(err * err) * (0.5 / D)

    tile = lambda w: pl.BlockSpec((tm, w), lambda i: (i, 0))
    f32_out = jax.ShapeDtypeStruct((t, D), F32)
    bf_out = jax.ShapeDtypeStruct((t, D), BF16)
    return pl.pallas_call(
        body, name="forward_tiles", grid=(t // tm,),
        in_specs=[tile(D_IN),
                  pl.BlockSpec((HALO, 2 * D), lambda i: (jnp.maximum(i * hb - 1, 0), 0)),
                  tile(D), tile(D), _resident((D, D)), _resident((D, D)), _resident((D, D)), _full((HALO, D)), _full((8, D)),
                  _full((N_GROUPS, CHUNK, CHUNK)), _full((CHUNK, D))],
        out_specs=[tile(D)] * 8 + [pl.BlockSpec((D, tm), lambda i: (0, i)), _full((8, D))],
        out_shape=[f32_out, f32_out, f32_out, bf_out, bf_out, bf_out, f32_out, bf_out,
                   jax.ShapeDtypeStruct((D, t), BF16), jax.ShapeDtypeStruct((8, D), F32)],
        scratch_shapes=[pltpu.VMEM((tm + HALO, D), F32), pltpu.VMEM((SUBLANES - 1, tm + SHIFT_ROWS, GROUP_W), F32)],
        compiler_params=_params("arbitrary"),
    )(p, p, x, tgt, wpa, wpb, wo, convw, vecs, ws, bsp)


def _backward_tiles(p, h1, ya, yb, drb, wpa, wpb, wo, vecs, ws, wst, bsp):
    t = h1.shape[0]
    tm = TOKEN_TILE

    def body(p_ref, h1_ref, ya_ref, yb_ref, drb_ref, wpa_ref, wpb_ref, wo_ref, vec_ref, ws_ref, wst_ref, bsp_ref,
             dh1_ref, dp_ref, dya_ref, dyb_ref, acc_ref, dbin_ref, dws_ref, dbsp_ref):
        i = pl.program_id(0)
        _, gn_g, gn_b, lnv_g, lnv_b = [vec_ref[j:j + 1, :] for j in range(5)]

        @pl.when(i == 0)
        def _():
            acc_ref[...] = jnp.zeros_like(acc_ref)
            dbin_ref[...] = jnp.zeros_like(dbin_ref)
            dws_ref[...] = jnp.zeros_like(dws_ref)
            dbsp_ref[...] = jnp.zeros_like(dbsp_ref)

        def emit(block, val):
            dbin_ref[0:1, block * D:(block + 1) * D] += _colsum(val)
            dp_ref[:, block * D:(block + 1) * D] = val.astype(BF16)

        dp_ref[:, 0:2 * D] = jnp.zeros((tm, 2 * D), BF16)
        dmixed = _dot_nt(drb_ref[...], wo_ref[...])
        ga = _sigmoid(p_ref[:, 6 * D:7 * D])
        gb = _sigmoid(p_ref[:, 7 * D:8 * D])
        dya = (dmixed * ga).astype(BF16)
        dyb = (dmixed * gb).astype(BF16)
        dya_ref[...] = dya
        dyb_ref[...] = dyb
        emit(6, dmixed * ya_ref[...] * (ga * (1.0 - ga)))
        emit(7, dmixed * yb_ref[...] * (gb * (1.0 - gb)))

        dh3 = _dot_nt(dya, wpa_ref[...])
        h2, xhat, rstd = _group_norm_fwd(h1_ref[...], gn_g, gn_b)
        sg = _sigmoid(h2)
        a_gate = p_ref[:, 2 * D:3 * D]
        sa = _sigmoid(a_gate)
        dh2 = dh3 * (a_gate * sa) * (sg * (1.0 + h2 * (1.0 - sg)))
        emit(2, dh3 * (h2 * sg) * (sa * (1.0 + a_gate * (1.0 - sa))))
        acc_ref[0:1, :] += _colsum(dh2 * xhat)
        acc_ref[1:2, :] += _colsum(dh2)
        dxhat = dh2 * gn_g
        for g in range(N_GROUPS):
            cs = slice(g * GROUP_W, (g + 1) * GROUP_W)
            dh1_ref[:, cs] = _norm_bwd(dxhat[:, cs], xhat[:, cs], rstd[g])

        ds = _dot_nt(dyb, wpb_ref[...])
        u_pre = p_ref[:, 3 * D:4 * D]
        u, du_dpre = _gelu_and_grad(u_pre)
        v0, dv_dpre = _gelu_and_grad(p_ref[:, 4 * D:5 * D])
        vhat, vrstd = _norm_stats(v0)
        v1 = (vhat * lnv_g + lnv_b).astype(BF16)
        vmix = _spatial_mix(ws_ref, v1, tm) + jnp.concatenate([bsp_ref[...]] * (tm // CHUNK), axis=0)
        b_gate = p_ref[:, 5 * D:6 * D]
        sb = _sigmoid(b_gate)
        silu_b = b_gate * sb
        emit(3, ds * vmix * silu_b * du_dpre)
        emit(5, ds * u * vmix * (sb * (1.0 + b_gate * (1.0 - sb))))
        dvmix = ds * u * silu_b
        dvmix_bf = dvmix.astype(BF16)
        for q in range(tm // CHUNK):
            dbsp_ref[...] += dvmix[q * CHUNK:(q + 1) * CHUNK, :]
            for h in range(N_GROUPS):
                blk = (slice(q * CHUNK, (q + 1) * CHUNK), slice(h * GROUP_W, (h + 1) * GROUP_W))
                dws_ref[:, h * GROUP_W:(h + 1) * GROUP_W] += _dot_nt(dvmix_bf[blk], v1[blk])
        dv1 = _spatial_mix(wst_ref, dvmix_bf, tm)
        acc_ref[2:3, :] += _colsum(dv1 * vhat)
        acc_ref[3:4, :] += _colsum(dv1)
        emit(4, _norm_bwd(dv1 * lnv_g, vhat, vrstd) * dv_dpre)

    tile = lambda w: pl.BlockSpec((tm, w), lambda i: (i, 0))
    return pl.pallas_call(
        body, name="backward_tiles", grid=(t // tm,),
        in_specs=[tile(D_IN), tile(D), tile(D), tile(D), tile(D), _resident((D, D)), _resident((D, D)), _resident((D, D)),
                  _full((8, D)), _full((N_GROUPS, CHUNK, CHUNK)), _full((N_GROUPS, CHUNK, CHUNK)), _full((CHUNK, D))],
        out_specs=[tile(D), tile(D_IN), tile(D), tile(D), _full((8, D)), _full((8, D_IN)),
                   _full((CHUNK, D)), _full((CHUNK, D))],
        out_shape=[jax.ShapeDtypeStruct((t, D), F32), jax.ShapeDtypeStruct((t, D_IN), BF16),
                   jax.ShapeDtypeStruct((t, D), BF16), jax.ShapeDtypeStruct((t, D), BF16),
                   jax.ShapeDtypeStruct((8, D), F32), jax.ShapeDtypeStruct((8, D_IN), F32),
                   jax.ShapeDtypeStruct((CHUNK, D), F32), jax.ShapeDtypeStruct((CHUNK, D), F32)],
        compiler_params=_params("arbitrary"),
    )(p, h1, ya, yb, drb, wpa, wpb, wo, vecs, ws, wst, bsp)


def _conv_backward(dh1, p, dp, convw, pairs, tiles_per_seq):
    t = dh1.shape[0]
    tm = TOKEN_TILE
    hb = tm // HALO
    last = t // HALO - 1
    n_sq = len(pairs)
    span = 2
    rows = D // 8

    def body(dh1_ref, dnext_ref, p_ref, ph_ref, cw_ref, dp_in_ref, *refs):
        del dp_in_ref
        sq_in = refs[:2 * n_sq]
        dp_ref, dcw_ref, dbin_ref = refs[2 * n_sq:2 * n_sq + 3]
        sq_out = refs[2 * n_sq + 3:3 * n_sq + 3]
        sq_wire = refs[3 * n_sq + 3:4 * n_sq + 3]
        de_ref, he_ref, dh0_ref, sh_ref, acc_ref, wire_ref, sq_sem, wire_sem = refs[4 * n_sq + 3:]
        i = pl.program_id(0)

        @pl.when(i == 0)
        def _():
            dcw_ref[...] = jnp.zeros_like(dcw_ref)
            dbin_ref[...] = jnp.zeros_like(dbin_ref)
            acc_ref[...] = jnp.zeros_like(acc_ref)

        @pl.when(i % span == span - 1)
        def _():
            for a in range(n_sq):
                acc_ref[a] += _dot_tn(sq_in[2 * a][...], sq_in[2 * a + 1][...])

        keep_next = jnp.where(i % tiles_per_seq == tiles_per_seq - 1, 0.0, 1.0)
        de_ref[0:tm, :] = dh1_ref[...]
        de_ref[tm:, :] = dnext_ref[...] * keep_next
        _conv_taps(de_ref, sh_ref, cw_ref, CONV_K - 1, -1, None, dh0_ref, tm)

        keep_prev = jnp.where(i % tiles_per_seq == 0, 0.0, 1.0)
        sg = _sigmoid(p_ref[:, D:2 * D])
        val = p_ref[:, 0:D]
        he_ref[0:HALO, :] = ph_ref[:, 0:D] * _sigmoid(ph_ref[:, D:2 * D]) * keep_prev
        he_ref[HALO:, :] = val * sg
        _conv_weight_grad(dh1_ref, he_ref, sh_ref, HALO - (CONV_K - 1), dcw_ref, tm)
        dcw_ref[SUBLANES * CONV_K:, :] += jnp.sum(dh1_ref[...].reshape(tm // SUBLANES, SUBLANES, D), axis=0)

        dh0 = dh0_ref[...]
        dval = dh0 * sg
        dglu = dh0 * val * (sg * (1.0 - sg))
        dbin_ref[0:1, 0:D] += _colsum(dval)
        dbin_ref[0:1, D:2 * D] += _colsum(dglu)
        dp_ref[:, 0:D] = dval.astype(BF16)
        dp_ref[:, D:2 * D] = dglu.astype(BF16)

        @pl.when(i == t // tm - 1)
        def _():
            cps = [pltpu.make_async_copy(acc_ref.at[a, pl.ds((2 * j + h) * rows, rows)], sq_out[a].at[h, j],
                                         sq_sem.at[(a * N_CHIPS + j) * 2 + h])
                   for a in range(n_sq) for j in range(N_CHIPS) for h in range(2)]
            for cp in cps:
                cp.start()
            for a in range(n_sq):
                wire_ref[...] = acc_ref[a].astype(BF16)
                narrow = [pltpu.make_async_copy(wire_ref.at[pl.ds((2 * j + h) * rows, rows)], sq_wire[a].at[h, j],
                                                wire_sem.at[2 * j + h]) for j in range(N_CHIPS) for h in range(2)]
                for cp in narrow:
                    cp.start()
                for cp in narrow:
                    cp.wait()
            for cp in cps:
                cp.wait()

    any_spec = pl.BlockSpec(memory_space=pl.ANY)
    wide = pl.BlockSpec((span * tm, D), lambda i: (i // span, 0))
    return pl.pallas_call(
        body, name="conv_backward", grid=(t // tm,),
        in_specs=[pl.BlockSpec((tm, D), lambda i: (i, 0)),
                  pl.BlockSpec((HALO, D), lambda i: (jnp.minimum((i + 1) * hb, last), 0)),
                  pl.BlockSpec((tm, 2 * D), lambda i: (i, 0)),
                  pl.BlockSpec((HALO, 2 * D), lambda i: (jnp.maximum(i * hb - 1, 0), 0)),
                  _full((HALO, D)), any_spec] + [wide] * (2 * n_sq),
        out_specs=[pl.BlockSpec((tm, 2 * D), lambda i: (i, 0)), _full((SUBLANES * HALO, D)), _full((8, 2 * D))]
        + [any_spec] * (2 * n_sq),
        out_shape=[jax.ShapeDtypeStruct(dp.shape, BF16), jax.ShapeDtypeStruct((SUBLANES * HALO, D), F32),
                   jax.ShapeDtypeStruct((8, 2 * D), F32)]
        + [jax.ShapeDtypeStruct((2, N_CHIPS, rows, D), F32)] * n_sq
        + [jax.ShapeDtypeStruct((2, N_CHIPS, rows, D), BF16)] * n_sq,
        scratch_shapes=[pltpu.VMEM((tm + HALO, D), F32), pltpu.VMEM((tm + HALO, D), F32), pltpu.VMEM((tm, D), F32),
                        pltpu.VMEM((SUBLANES - 1, tm + SHIFT_ROWS, GROUP_W), F32), pltpu.VMEM((n_sq, D, D), F32),
                        pltpu.VMEM((D, D), BF16), pltpu.SemaphoreType.DMA((n_sq * N_CHIPS * 2,)),
                        pltpu.SemaphoreType.DMA((N_CHIPS * 2,))],
        input_output_aliases={5: 0},
        compiler_params=_params("arbitrary"),
    )(dh1, dh1, p, p, convw, dp, *[a for pair in pairs for a in pair])


def _grad_in_and_x(xt, dp, w4, dr, wires, grads):
    t = dr.shape[0]
    tm = TOKEN_TILE
    half, tn = D // 2, 512
    nb = W_BLOCK // tn
    n_w, n_x = 2 * N_CHIPS * nb, t // tm
    ns = len(grads)
    xi, yi, ci = lax.axis_index("x"), lax.axis_index("y"), lax.axis_index("c")
    others = [2 * (1 - xi) + yi, 2 * xi + (1 - yi), 2 * (1 - xi) + (1 - yi)]
    blocks = others + others + [2 * xi + yi] * 2
    halves = [1 - ci] * 3 + [ci] * 3 + [1 - ci, ci]
    table = jnp.stack([jnp.stack([b * nb + n for b in blocks for n in range(nb)]),
                       jnp.stack([h for h in halves for _ in range(nb)])]).astype(jnp.int32)

    def body(tab_ref, xt_ref, dpc_ref, dpr_ref, w_ref, dr_ref, *refs):
        parts, fulls = refs[:ns], refs[ns:2 * ns]
        dx_ref, qk_ref, b2_ref, b1_ref, wire_ref = refs[2 * ns:2 * ns + 5]
        lands, sibs = refs[2 * ns + 5:3 * ns + 5], refs[3 * ns + 5:4 * ns + 5]
        (g_ref, st_ref, sb_ref, tmp_ref, d2d_send, d2d_recv, ici_send, ici_recv, own_sem, tmp_sem, wire_sem,
         p_send, p_recv, s_send, s_recv) = refs[4 * ns + 5:]
        s = pl.program_id(0)
        x_, y_, c, k = _position()
        chips = _other_chips(x_, y_)
        n = s % nb
        grp = s // nb
        cols = pl.ds(pl.multiple_of(n * tn, tn), tn)

        def part(a, r, core):
            cx, cy = chips[r]
            return pltpu.make_async_remote_copy(
                src_ref=parts[a].at[core, 2 * cx + cy], dst_ref=lands[a].at[2 * r + c],
                send_sem=p_send.at[6 * a + 2 * r + core], recv_sem=p_recv.at[6 * a + 2 * r + c],
                device_id=(cx, cy, core), device_id_type=MESH)

        def landed(a, r, core):
            cx, cy = chips[r]
            return pltpu.make_async_remote_copy(
                src_ref=lands[a].at[2 * r + core], dst_ref=lands[a].at[2 * r + core],
                send_sem=p_send.at[6 * a + 2 * r + core], recv_sem=p_recv.at[6 * a + 2 * r + core],
                device_id=(cx, cy, core), device_id_type=MESH)

        def to_sibling_whole(a):
            return pltpu.make_async_remote_copy(
                src_ref=fulls[a].at[1 - c, k], dst_ref=sibs[a], send_sem=s_send.at[a], recv_sem=s_recv.at[a],
                device_id=(x_, y_, 1 - c), device_id_type=MESH)

        def to_sibling(slot, land):
            return pltpu.make_async_remote_copy(
                src_ref=st_ref.at[slot], dst_ref=b1_ref.at[land, :, cols], send_sem=d2d_send.at[slot],
                recv_sem=d2d_recv.at[land * nb + n], device_id=(x_, y_, 1 - c), device_id_type=MESH)

        def to_chip(r):
            cx, cy = chips[r]
            return pltpu.make_async_remote_copy(
                src_ref=wire_ref.at[r, :, cols], dst_ref=b2_ref.at[r, :, cols], send_sem=ici_send.at[r],
                recv_sem=ici_recv.at[r], device_id=(cx, cy, c), device_id_type=MESH)

        def all_of_chip(r):
            cx, cy = chips[r]
            return pltpu.make_async_remote_copy(
                src_ref=wire_ref.at[r], dst_ref=b2_ref.at[r], send_sem=ici_send.at[r],
                recv_sem=ici_recv.at[r], device_id=(cx, cy, c), device_id_type=MESH)

        def to_result(slot):
            return pltpu.make_async_copy(st_ref.at[slot], qk_ref.at[:, cols], own_sem.at[slot])

        def sibling_piece(land):
            return pltpu.make_async_copy(b1_ref.at[land, :, cols], tmp_ref, tmp_sem)

        @pl.when(s == 0)
        def _():
            for a in range(ns):
                to_sibling_whole(a).start()
                for r in range(3):
                    for core in range(2):
                        part(a, r, core).start()

        own_half = ((grp >= 3) & (grp <= 5)) | (grp == 7)
        land = jnp.where(grp == 7, 3, grp - 3)

        @pl.when(own_half)
        def _():
            to_sibling(0, land).wait_recv()
            sibling_piece(land).start()

        @pl.when(s < n_w)
        def _():
            g_ref[...] = _dot(xt_ref[tab_ref[1, s]], dpc_ref[...])

        @pl.when(own_half)
        def _():
            sibling_piece(land).wait()

        for g in range(2 * N_CHIPS):
            @pl.when(grp == g)
            def _(g=g):
                if g in (0, 1, 2, 6):
                    use = s if g < 3 else 3 * nb + n
                    slot = use % 2

                    @pl.when(use >= 2)
                    def _():
                        to_sibling(slot, 0).wait_send()

                    st_ref[slot] = g_ref[...]
                    to_sibling(slot, min(g, 3)).start()
                elif g in (3, 4, 5):
                    sb_ref[...] = (g_ref[...] + tmp_ref[...]).astype(BF16)
                    stage = pltpu.make_async_copy(sb_ref, wire_ref.at[g - 3, :, cols], wire_sem)
                    stage.start()
                    stage.wait()
                    to_chip(g - 3).start()
                else:
                    slot = n % 2
                    piece = g_ref[...] + tmp_ref[...]

                    @pl.when(n < 2)
                    def _():
                        to_sibling(slot, 0).wait_send()

                    @pl.when(n >= 2)
                    def _():
                        to_result(slot).wait()

                    st_ref[slot] = piece
                    to_result(slot).start()

        @pl.when(s >= n_w)
        def _():
            acc = ALPHA * dr_ref[...]
            for j in range(N_CHIPS):
                acc = acc + _dot_nt(dpr_ref[:, j * W_BLOCK:(j + 1) * W_BLOCK], w_ref[j])
            dx_ref[...] = acc

        @pl.when(s == n_w + n_x - 1)
        def _():
            for slot in range(2):
                to_result(slot).wait()
            for r in range(3):
                all_of_chip(r).wait_recv()
                all_of_chip(r).wait_send()
            for a in range(ns):
                to_sibling_whole(a).wait_recv()
                to_sibling_whole(a).wait_send()
                for r in range(3):
                    for core in range(2):
                        landed(a, r, core).wait_recv()
                        part(a, r, core).wait_send()

    any_spec = pl.BlockSpec(memory_space=pl.ANY)
    tile = lambda s, tab: (jnp.maximum(s - n_w, 0), 0)
    return pl.pallas_call(
        body, name="grad_in_and_x",
        grid_spec=pltpu.PrefetchScalarGridSpec(
            num_scalar_prefetch=1, grid=(n_w + n_x,),
            in_specs=[pl.BlockSpec((2, half, t), lambda s, tab: (0, 0, 0), pipeline_mode=pl.Buffered(1)),
                      pl.BlockSpec((t, tn), lambda s, tab: (0, tab[0, jnp.minimum(s, n_w - 1)])),
                      pl.BlockSpec((tm, D_IN), tile),
                      pl.BlockSpec((N_CHIPS, D, W_BLOCK), lambda s, tab: (0, 0, 0), pipeline_mode=pl.Buffered(1)),
                      pl.BlockSpec((tm, D), tile)] + [any_spec] * (2 * ns),
            out_specs=[pl.BlockSpec((tm, D), tile)] + [any_spec] * (4 + 2 * ns),
            scratch_shapes=[pltpu.VMEM((half, tn), F32), pltpu.VMEM((2, half, tn), F32), pltpu.VMEM((half, tn), BF16),
                            pltpu.VMEM((half, tn), F32),
                            pltpu.SemaphoreType.DMA((2,)), pltpu.SemaphoreType.DMA((N_CHIPS * nb,)),
                            pltpu.SemaphoreType.DMA((3,)), pltpu.SemaphoreType.DMA((3,)),
                            pltpu.SemaphoreType.DMA((2,)), pltpu.SemaphoreType.DMA, pltpu.SemaphoreType.DMA,
                            pltpu.SemaphoreType.DMA((6 * ns,)), pltpu.SemaphoreType.DMA((6 * ns,)),
                            pltpu.SemaphoreType.DMA((ns,)), pltpu.SemaphoreType.DMA((ns,))]),
        out_shape=[jax.ShapeDtypeStruct((t, D), F32), jax.ShapeDtypeStruct((half, W_BLOCK), F32),
                   jax.ShapeDtypeStruct((3, half, W_BLOCK), BF16), jax.ShapeDtypeStruct((N_CHIPS, half, W_BLOCK), F32),
                   jax.ShapeDtypeStruct((3, half, W_BLOCK), BF16)]
        + [jax.ShapeDtypeStruct((6,) + w.shape[2:], w.dtype) for w in wires]
        + [jax.ShapeDtypeStruct(g.shape[2:], F32) for g in grads],
        compiler_params=_params("arbitrary"),
    )(table, xt, dp, dp, w4, dr, *wires, *grads)


def kernel(x, w_in, b_in, conv_w, conv_b, gn_g, gn_b, ln_v_g, ln_v_b, w_spatial, b_spatial, w_pa, w_pb, w_o, b_o, ln_out_g, ln_out_b, loss_target, m_w_in, m_b_in, m_conv_w, m_conv_b, m_gn_g, m_gn_b, m_ln_v_g, m_ln_v_b, m_w_spatial, m_b_spatial, m_w_pa, m_w_pb, m_w_o, m_b_o, m_ln_out_g, m_ln_out_b, v_w_in, v_b_in, v_conv_w, v_conv_b, v_gn_g, v_gn_b, v_ln_v_g, v_ln_v_b, v_w_spatial, v_b_spatial, v_w_pa, v_w_pb, v_w_o, v_b_o, v_ln_out_g, v_ln_out_b):
    n_seq, seq, _ = x.shape
    t = n_seq * seq
    tiles_per_seq = seq // TOKEN_TILE
    x2 = x.reshape(t, D)
    tgt = loss_target.reshape(t, D)

    conv_shard = jnp.pad(conv_w, ((0, HALO - CONV_K), (0, 0)))
    p, win4, wpa4, wpb4, wo4, conv4 = _proj_gather(
        x2, b_in.reshape(N_CHIPS, 1, W_BLOCK),
        [_place_shard(w_in, 256, BF16), _place_shard(w_pa, 128, BF16), _place_shard(w_pb, 128, BF16),
         _place_shard(w_o, 128, BF16), _place_shard(conv_shard, HALO // 2, F32)])
    win4 = win4.reshape(N_CHIPS, D, W_BLOCK)
    wpa, wpb, wo = wpa4.reshape(D, D), wpb4.reshape(D, D), wo4.reshape(D, D)
    convw = conv4.reshape(N_CHIPS, HALO, D // N_CHIPS).transpose(1, 0, 2).reshape(HALO, D)

    vecs = jnp.stack([conv_b, gn_g, gn_b, ln_v_g, ln_v_b, b_o, ln_out_g, ln_out_b])
    causal = jnp.tril(jnp.ones((CHUNK, CHUNK), bool))
    ws = jnp.where(causal[None], w_spatial, 0.0)
    ws_bf, wst_bf = ws.astype(BF16), ws.transpose(0, 2, 1).astype(BF16)
    bsp = jnp.repeat(b_spatial.T, GROUP_W, axis=1)

    h1, ya, yb, h3, s, mixed, dr, drb, xt, acc_f = _forward_tiles(p, x2, tgt, wpa, wpb, wo, convw, vecs, ws_bf, bsp, tiles_per_seq)
    dh1, dp, dya, dyb, acc_b, dbin_b, dws, dbsp_acc = _backward_tiles(p, h1, ya, yb, drb, wpa, wpb, wo, vecs, ws_bf, wst_bf, bsp)
    dp, dcw8, dbin_a, *square = _conv_backward(dh1, p, dp, convw, [(h3, dya), (s, dyb), (mixed, drb)], tiles_per_seq)

    small, g_conv = _pack_small(acc_f, acc_b, dbin_a, dbin_b, dcw8, dws, dbsp_acc)
    small = small.reshape(2, N_CHIPS, SMALL_ROWS // 8, D)

    grads = square[:3] + [g_conv, small]
    wires = square[3:] + [g_conv, small]
    grad_x, q_in, chips_in, _, _, *landed = _grad_in_and_x(xt.reshape(2, D // 2, t), dp, win4, dr, wires, grads)
    grad_x = grad_x.reshape(x.shape)
    mine = [_add_chips(q_in, chips_in)]
    mine += [_add_devices(g, land, sib, a == 4) for a, (g, land, sib) in enumerate(zip(grads, landed[:5], landed[5:]))]
    *full, small_parts = _share_results(mine[:5], mine[5])
    grad_w_in, grad_w_pa, grad_w_pb, grad_w_o = [f.reshape(w.shape) for f, w in zip(full[:4], (w_in, w_pa, w_pb, w_o))]
    grad_conv_w = full[4].reshape(HALO, D // N_CHIPS)[:CONV_K]

    big = {}
    for name, w, g, m, v in [("w_in", w_in, grad_w_in, m_w_in, v_w_in), ("w_pa", w_pa, grad_w_pa, m_w_pa, v_w_pa),
                             ("w_pb", w_pb, grad_w_pb, m_w_pb, v_w_pb), ("w_o", w_o, grad_w_o, m_w_o, v_w_o),
                             ("conv_w", conv_w, grad_conv_w, m_conv_w, v_conv_w)]:
        big[name] = (g,) + tuple(_adamw(w, g, m, v))
    vec_names = ["conv_b", "gn_g", "gn_b", "ln_v_g", "ln_v_b", "b_o", "ln_out_g", "ln_out_b"]
    vec_triples = [(conv_b, m_conv_b, v_conv_b), (gn_g, m_gn_g, v_gn_g), (gn_b, m_gn_b, v_gn_b),
                   (ln_v_g, m_ln_v_g, v_ln_v_g), (ln_v_b, m_ln_v_b, v_ln_v_b), (b_o, m_b_o, v_b_o),
                   (ln_out_g, m_ln_out_g, v_ln_out_g), (ln_out_b, m_ln_out_b, v_ln_out_b)]
    small_res, loss8 = _adamw_small(
        small_parts.reshape(8, SMALL_ROWS // 8, D), vec_triples, (b_in, m_b_in, v_b_in),
        (w_spatial, m_w_spatial, v_w_spatial), (b_spatial, m_b_spatial, v_b_spatial))
    per_name = dict(zip(vec_names + ["b_in", "w_spatial", "b_spatial"], small_res))

    order = ["w_in", "b_in", "conv_w", "conv_b", "gn_g", "gn_b", "ln_v_g", "ln_v_b", "w_spatial", "b_spatial",
             "w_pa", "w_pb", "w_o", "b_o", "ln_out_g", "ln_out_b"]
    outs = [loss8[0, 0], grad_x]
    for kind in range(4):
        outs += [big[n][kind] if n in big else per_name[n][kind] for n in order]
    return tuple(outs)
```

```python
import functools
import math

import jax
import jax.numpy as jnp
from jax import lax
from jax.experimental import pallas as pl
from jax.experimental.pallas import tpu as pltpu

D = 1024
N_GROUPS = 8
GROUP_W = D // N_GROUPS
CHUNK = 128
CONV_K = 31
HALO = 32
D_IN = 8 * D
N_CHIPS = 4
W_BLOCK = D_IN // N_CHIPS
ALPHA = 2.0 ** 0.25
LN_EPS = 1e-5
ADAM_LR, ADAM_B1, ADAM_B2, ADAM_EPS, ADAM_WD, ADAM_STEP = 0.001, 0.9, 0.999, 1e-08, 0.01, 10

TOKEN_TILE = 256
VMEM_LIMIT = 56 * 1024 * 1024
MESH = pl.DeviceIdType.MESH
F32, BF16 = jnp.float32, jnp.bfloat16


def _sigmoid(x):
    return 1.0 / (1.0 + jnp.exp(-x))


def _gelu(x):
    c = math.sqrt(2.0 / math.pi)
    t = jnp.tanh(c * (x + 0.044715 * (x * x * x)))
    return x * (0.5 * (1.0 + t))


def _gelu_and_grad(x):
    c = math.sqrt(2.0 / math.pi)
    x2 = x * x
    t = jnp.tanh(c * (x + 0.044715 * (x2 * x)))
    cdf = 0.5 * (1.0 + t)
    return x * cdf, cdf + 0.5 * x * (1.0 - t * t) * (c * (1.0 + 3.0 * 0.044715 * x2))


def _norm_stats(v):
    mu = jnp.mean(v, axis=-1, keepdims=True)
    vc = v - mu
    var = jnp.mean(vc * vc, axis=-1, keepdims=True)
    rstd = lax.rsqrt(var + LN_EPS)
    return vc * rstd, rstd


def _norm_bwd(dxhat, xhat, rstd):
    m1 = jnp.mean(dxhat, axis=-1, keepdims=True)
    m2 = jnp.mean(dxhat * xhat, axis=-1, keepdims=True)
    return rstd * (dxhat - m1 - xhat * m2)


def _dot(a, b):
    return jnp.dot(a, b, preferred_element_type=F32)


def _dot_nt(a, b):
    return lax.dot_general(a, b, (((1,), (1,)), ((), ())), preferred_element_type=F32)


def _dot_tn(a, b):
    return lax.dot_general(a, b, (((0,), (0,)), ((), ())), preferred_element_type=F32)


def _colsum(v):
    return jnp.sum(v, axis=0, keepdims=True)


def _full(shape):
    return pl.BlockSpec(shape, lambda *_: (0,) * len(shape))


def _resident(shape):
    return pl.BlockSpec(shape, lambda *_: (0,) * len(shape), pipeline_mode=pl.Buffered(1))


def _params(*sem):
    return pltpu.CompilerParams(dimension_semantics=sem, vmem_limit_bytes=VMEM_LIMIT)


def _chip_index():
    return (2 * lax.axis_index("x") + lax.axis_index("y")).astype(jnp.int32).reshape(1)


def _core_index():
    return lax.axis_index("c").astype(jnp.int32).reshape(1)


def _place_shard(w, rows, dtype):
    r, c = w.shape
    steps = r // 2 // rows

    def body(k_ref, w_ref, o_ref):
        o_ref[...] = w_ref[...].astype(dtype)

    return pl.pallas_call(
        body, name="place_shard",
        grid_spec=pltpu.PrefetchScalarGridSpec(
            num_scalar_prefetch=1, grid=(2, steps),
            in_specs=[pl.BlockSpec((rows, c), lambda h, i, k: (h * steps + i, 0))],
            out_specs=pl.BlockSpec((None, None, rows, c), lambda h, i, k: (k[0], h, i, 0))),
        out_shape=jax.ShapeDtypeStruct((N_CHIPS, 2, r // 2, c), dtype),
        compiler_params=_params("parallel", "parallel"),
    )(_chip_index(), w)


def _position():
    x, y, c = lax.axis_index("x"), lax.axis_index("y"), lax.axis_index("c")
    return x, y, c, 2 * x + y


def _other_chips(x, y):
    return [(1 - x, y), (x, 1 - y), (1 - x, 1 - y)]


def _any_specs(n):
    return [pl.BlockSpec(memory_space=pl.ANY)] * n


def _proj_gather(x, b4, bufs):
    t = x.shape[0]
    tm = 1024
    steps = t // tm
    half = D // 2
    n = len(bufs)
    xi, yi = lax.axis_index("x"), lax.axis_index("y")
    order = jnp.stack([2 * xi + yi, 2 * (1 - xi) + yi, 2 * xi + (1 - yi), 2 * (1 - xi) + (1 - yi)]).astype(jnp.int32)

    def body(order_ref, x_ref, b_ref, *refs):
        p_ref, outs = refs[n], refs[n + 1:2 * n + 1]
        xb_ref, w_ref, lsem, send, recv, hop_send, hop_recv, fsend, frecv = refs[2 * n + 1:]
        jj, i = pl.program_id(0), pl.program_id(1)
        x_, y_, c, k = _position()
        nbrs = [(1 - x_, y_), (x_, 1 - y_)]
        blocks = [2 * (1 - x_) + y_, 2 * x_ + (1 - y_), 2 * (1 - x_) + (1 - y_)]

        def quarter(a, block, q):
            rows = outs[a].shape[2] // 2
            return outs[a].at[block, c, pl.ds(q * rows, rows)]

        def copy(ref, to, send_sem, recv_sem):
            return pltpu.make_async_remote_copy(src_ref=ref, dst_ref=ref, send_sem=send_sem, recv_sem=recv_sem,
                                                device_id=(to[0], to[1], c), device_id_type=MESH)

        def sent(a, nb, q):
            return copy(quarter(a, k, q), nbrs[nb], send.at[4 * a + 2 * nb + q], recv.at[4 * a + 2 * nb + q])

        def landed(a, nb, q):
            return copy(quarter(a, blocks[nb], q), nbrs[nb], send.at[4 * a + 2 * nb + q], recv.at[4 * a + 2 * nb + q])

        def hopped(a, nb):
            return copy(quarter(a, blocks[nb], nb), nbrs[1 - nb], hop_send.at[2 * a + nb], hop_recv.at[2 * a + 1 - nb])

        def from_diagonal(a, via):
            return copy(quarter(a, blocks[2], 1 - via), nbrs[via], hop_send.at[2 * a + via], hop_recv.at[2 * a + via])

        def passed(a, r, h):
            return pltpu.make_async_remote_copy(
                src_ref=outs[a].at[blocks[r], h], dst_ref=outs[a].at[blocks[r], h], send_sem=fsend.at[3 * a + r],
                recv_sem=frecv.at[3 * a + r], device_id=(x_, y_, 1 - c), device_id_type=MESH)

        def load(block):
            cps = [pltpu.make_async_copy(outs[0].at[block, h], w_ref.at[pl.ds(h * half, half)], lsem.at[h])
                   for h in range(2)]
            for cp in cps:
                cp.start()
            for cp in cps:
                cp.wait()

        def pass_on(arrays):
            for a in arrays:
                for nb in range(2):
                    landed(a, nb, nb).wait_recv()
                    hopped(a, nb).start()

        @pl.when((jj == 0) & (i == 0))
        def _():
            for a in range(n):
                for nb, q in ((0, 0), (1, 1), (0, 1), (1, 0)):
                    sent(a, nb, q).start()
            load(k)

        @pl.when((jj == 1) & (i == 0))
        def _():
            pass_on([0])
            landed(0, 0, 1).wait_recv()
            passed(0, 0, c).start()
            passed(0, 0, 1 - c).wait_recv()
            load(blocks[0])

        @pl.when((jj == 2) & (i == 0))
        def _():
            landed(0, 1, 0).wait_recv()
            passed(0, 1, c).start()
            passed(0, 1, 1 - c).wait_recv()
            load(blocks[1])
            pass_on(range(1, n))
            for a in range(1, n):
                landed(a, 0, 1).wait_recv()
                passed(a, 0, c).start()
                landed(a, 1, 0).wait_recv()
                passed(a, 1, c).start()

        @pl.when((jj == 3) & (i == 0))
        def _():
            from_diagonal(0, 0).wait_recv()
            from_diagonal(0, 1).wait_recv()
            passed(0, 2, c).start()
            passed(0, 2, 1 - c).wait_recv()
            load(blocks[2])

        rows = pl.ds(pl.multiple_of(i * tm, tm), tm)

        @pl.when(jj == 0)
        def _():
            xb_ref[rows, :] = x_ref[...].astype(BF16)

        p_ref[...] = _dot(xb_ref[rows, :], w_ref[...]) + b_ref[...]

        @pl.when((jj == N_CHIPS - 1) & (i == steps - 1))
        def _():
            for a in range(1, n):
                from_diagonal(a, 0).wait_recv()
                from_diagonal(a, 1).wait_recv()
                passed(a, 2, c).start()
            for a in range(1, n):
                for r in range(3):
                    passed(a, r, 1 - c).wait_recv()
            for a in range(n):
                for nb in range(2):
                    for q in range(2):
                        sent(a, nb, q).wait_send()
                    hopped(a, nb).wait_send()
                for r in range(3):
                    passed(a, r, c).wait_send()

    any_spec = pl.BlockSpec(memory_space=pl.ANY)
    return pl.pallas_call(
        body, name="proj_gather",
        grid_spec=pltpu.PrefetchScalarGridSpec(
            num_scalar_prefetch=1, grid=(N_CHIPS, steps),
            in_specs=[pl.BlockSpec((tm, D), lambda jj, i, o: (jnp.where(jj == 0, i, steps - 1), 0)),
                      pl.BlockSpec((None, 1, W_BLOCK), lambda jj, i, o: (o[jj], 0, 0))] + [any_spec] * n,
            out_specs=[pl.BlockSpec((tm, W_BLOCK), lambda jj, i, o: (i, o[jj]))] + [any_spec] * n,
            scratch_shapes=[pltpu.VMEM((t, D), BF16), pltpu.VMEM((D, W_BLOCK), BF16), pltpu.SemaphoreType.DMA((2,)),
                            pltpu.SemaphoreType.DMA((4 * n,)), pltpu.SemaphoreType.DMA((4 * n,)),
                            pltpu.SemaphoreType.DMA((2 * n,)), pltpu.SemaphoreType.DMA((2 * n,)),
                            pltpu.SemaphoreType.DMA((3 * n,)), pltpu.SemaphoreType.DMA((3 * n,))]),
        out_shape=[jax.ShapeDtypeStruct((t, D_IN), F32)] + [jax.ShapeDtypeStruct(b.shape, b.dtype) for b in bufs],
        input_output_aliases={3 + a: 1 + a for a in range(n)},
        compiler_params=_params("arbitrary", "arbitrary"),
    )(order, x, b4, *bufs)


def _share_results(bufs, small):
    n = len(bufs)

    def body(*refs):
        outs, small_out = refs[n + 1:2 * n + 1], refs[2 * n + 1]
        send, recv, ssend, srecv = refs[2 * n + 2:]
        x, y, c, k = _position()
        cps = []
        for a in range(n):
            cp = pltpu.make_async_remote_copy(
                src_ref=outs[a].at[c], dst_ref=outs[a].at[c], send_sem=send.at[a], recv_sem=recv.at[a],
                device_id=(x, y, 1 - c), device_id_type=MESH)
            cp.start()
            cps.append(cp)
        waits = []
        for p in range(1, 8):
            px, py, pc = x ^ (p >> 2), y ^ ((p >> 1) & 1), c ^ (p & 1)
            cp = pltpu.make_async_remote_copy(
                src_ref=small_out.at[k, c], dst_ref=small_out.at[k, c], send_sem=ssend.at[p - 1],
                recv_sem=srecv.at[p - 1], device_id=(px, py, pc), device_id_type=MESH)
            cp.start()
            cps.append(cp)
            waits.append(pltpu.make_async_remote_copy(
                src_ref=small_out.at[2 * px + py, pc], dst_ref=small_out.at[2 * px + py, pc], send_sem=ssend.at[p - 1],
                recv_sem=srecv.at[p - 1], device_id=(px, py, pc), device_id_type=MESH))
        for a in range(n):
            pltpu.make_async_remote_copy(
                src_ref=outs[a].at[1 - c], dst_ref=outs[a].at[1 - c], send_sem=send.at[a], recv_sem=recv.at[a],
                device_id=(x, y, 1 - c), device_id_type=MESH).wait_recv()
        for w in waits:
            w.wait_recv()
        for cp in cps:
            cp.wait_send()

    return pl.pallas_call(
        body, name="rs_share_results",
        in_specs=_any_specs(n + 1), out_specs=_any_specs(n + 1),
        out_shape=[jax.ShapeDtypeStruct(b.shape, b.dtype) for b in bufs + [small]],
        scratch_shapes=[pltpu.SemaphoreType.DMA((n,)), pltpu.SemaphoreType.DMA((n,)),
                        pltpu.SemaphoreType.DMA((7,)), pltpu.SemaphoreType.DMA((7,))],
        input_output_aliases={a: a for a in range(n + 1)},
    )(*bufs, small)


def _row_tile(r, c):
    t = max(8, min(r, (1 << 18) // c))
    while r % t:
        t //= 2
    return t


def _add_devices(g, lands, sib, per_device):
    r, c = g.shape[-2:]
    t = _row_tile(r, c)

    def body(kc_ref, g_ref, l_ref, s_ref, f_ref):
        f = g_ref[...] + s_ref[...]
        for i in range(l_ref.shape[0]):
            f = f + l_ref[i].astype(F32)
        f_ref[...] = f

    if per_device:
        out_spec = pl.BlockSpec((None, None, t, c), lambda i, kc: (kc[0], kc[1], i, 0))
        out_shape = jax.ShapeDtypeStruct((N_CHIPS, 2, r, c), F32)
    else:
        out_spec = pl.BlockSpec((None, t, c), lambda i, kc: (kc[1], i, 0))
        out_shape = jax.ShapeDtypeStruct((2, r, c), F32)
    return pl.pallas_call(
        body, name="rs_add_devices",
        grid_spec=pltpu.PrefetchScalarGridSpec(
            num_scalar_prefetch=1, grid=(r // t,),
            in_specs=[pl.BlockSpec((None, None, t, c), lambda i, kc: (kc[1], kc[0], i, 0)),
                      pl.BlockSpec((lands.shape[0], t, c), lambda i, kc: (0, i, 0)),
                      pl.BlockSpec((t, c), lambda i, kc: (i, 0))],
            out_specs=out_spec),
        out_shape=out_shape,
        compiler_params=_params("parallel"),
    )(jnp.concatenate([_chip_index(), _core_index()]), g, lands, sib)


def _add_chips(q, b2):
    r, c = q.shape
    t = _row_tile(r, c)

    def body(c_ref, q_ref, b_ref, f_ref):
        f_ref[...] = ((q_ref[...] + b_ref[0].astype(F32)) + b_ref[1].astype(F32)) + b_ref[2].astype(F32)

    return pl.pallas_call(
        body, name="rs_add_chips",
        grid_spec=pltpu.PrefetchScalarGridSpec(
            num_scalar_prefetch=1, grid=(r // t,),
            in_specs=[pl.BlockSpec((t, c), lambda i, cr: (i, 0)), pl.BlockSpec((3, t, c), lambda i, cr: (0, i, 0))],
            out_specs=pl.BlockSpec((None, t, c), lambda i, cr: (cr[0], i, 0))),
        out_shape=jax.ShapeDtypeStruct((2, r, c), F32),
        compiler_params=_params("parallel"),
    )(_core_index(), q, b2)


def _adamw_math(w, g, m, v):
    m = ADAM_B1 * m + (1.0 - ADAM_B1) * g
    v = ADAM_B2 * v + (1.0 - ADAM_B2) * (g * g)
    m_hat = m / (1.0 - ADAM_B1 ** ADAM_STEP)
    v_hat = v / (1.0 - ADAM_B2 ** ADAM_STEP)
    delta = -ADAM_LR * (m_hat / (jnp.sqrt(v_hat) + ADAM_EPS) + ADAM_WD * w)
    return delta, m, v


def _adamw(w, g, m, v):
    r, c = w.shape
    t = _row_tile(r, c) if r % 8 == 0 else r

    def body(w_ref, g_ref, m_ref, v_ref, d_ref, nm_ref, nv_ref):
        d_ref[...], nm_ref[...], nv_ref[...] = _adamw_math(w_ref[...], g_ref[...], m_ref[...], v_ref[...])

    spec = pl.BlockSpec((t, c), lambda i: (i, 0))
    return pl.pallas_call(
        body, name="adamw", grid=(r // t,), in_specs=[spec] * 4, out_specs=[spec] * 3,
        out_shape=[jax.ShapeDtypeStruct((r, c), F32)] * 3, compiler_params=_params("parallel"),
    )(w, g, m, v)


ROW_B_IN = 0
ROW_VECS = 8
ROW_LOSS = 16
ROW_B_SPATIAL = 24
ROW_W_SPATIAL = 32
SMALL_ROWS = 192
N_VECS = 8


def _pack_small(acc_f, acc_b, dbin_a, dbin_b, dcw8, dws, dbsp):
    cols = D // N_CHIPS

    def body(af_ref, ab_ref, da_ref, db_ref, cw_ref, ws_ref, bs_ref, o_ref, gc_ref):
        o_ref[...] = jnp.zeros_like(o_ref)
        for j in range(D_IN // D):
            src = da_ref if j < 2 else db_ref
            o_ref[ROW_B_IN + j:ROW_B_IN + j + 1, :] = src[0:1, j * D:(j + 1) * D]
        dcw = jnp.sum(cw_ref[...].reshape(HALO, SUBLANES, D), axis=1)
        o_ref[ROW_VECS:ROW_VECS + 1, :] = dcw[CONV_K:CONV_K + 1]
        o_ref[ROW_VECS + 1:ROW_VECS + 5, :] = ab_ref[0:4, :]
        o_ref[ROW_VECS + 5:ROW_VECS + 6, :] = af_ref[2:3, :]
        o_ref[ROW_VECS + 6:ROW_VECS + 8, :] = af_ref[0:2, :]
        o_ref[ROW_LOSS:ROW_LOSS + 1, :] = af_ref[3:4, :]
        head = lax.broadcasted_iota(jnp.int32, (N_GROUPS, D), 0)
        lane = lax.broadcasted_iota(jnp.int32, (N_GROUPS, D), 1)
        indicator = jnp.where(lane // GROUP_W == head, 1.0, 0.0)
        o_ref[ROW_B_SPATIAL:ROW_B_SPATIAL + N_GROUPS, 0:CHUNK] = lax.dot_general(
            indicator, bs_ref[...], (((1,), (1,)), ((), ())), precision=lax.Precision.HIGHEST, preferred_element_type=F32)
        t_idx = lax.broadcasted_iota(jnp.int32, (CHUNK, D), 0)
        s_idx = lax.broadcasted_iota(jnp.int32, (CHUNK, D), 1) % CHUNK
        o_ref[ROW_W_SPATIAL:ROW_W_SPATIAL + CHUNK, :] = jnp.where(s_idx <= t_idx, ws_ref[...], 0.0)
        for h in range(2):
            for j in range(N_CHIPS):
                gc_ref[h, j] = dcw[h * (HALO // 2):(h + 1) * (HALO // 2), j * cols:(j + 1) * cols]

    ins = [acc_f, acc_b, dbin_a, dbin_b, dcw8, dws, dbsp]
    return pl.pallas_call(
        body, name="pack_small",
        in_specs=[_full(a.shape) for a in ins],
        out_specs=[_full((SMALL_ROWS, D)), _full((2, N_CHIPS, HALO // 2, cols))],
        out_shape=[jax.ShapeDtypeStruct((SMALL_ROWS, D), F32), jax.ShapeDtypeStruct((2, N_CHIPS, HALO // 2, cols), F32)],
        compiler_params=_params(),
    )(*ins)


def _adamw_small(parts, vecs, b_in, w_spatial, b_spatial):
    triples = list(vecs) + [b_in, w_spatial, b_spatial]
    n_in = 3 * len(triples)

    def body(p_ref, *refs):
        ins = [refs[3 * i:3 * i + 3] for i in range(len(triples))]
        outs = [refs[n_in + 4 * i:n_in + 4 * i + 4] for i in range(len(triples))]
        loss_ref, g_ref = refs[n_in + 4 * len(triples):]
        rows = SMALL_ROWS // 8
        for k in range(N_CHIPS):
            for core in range(2):
                g_ref[(core * N_CHIPS + k) * rows:(core * N_CHIPS + k + 1) * rows, :] = p_ref[2 * k + core]

        def step(g, wmv, out, get, put):
            d, nm, nv = _adamw_math(get(wmv[0]), g, get(wmv[1]), get(wmv[2]))
            for o, val in zip(out, (g, d, nm, nv)):
                put(o, val)

        for i in range(N_VECS):
            step(g_ref[ROW_VECS + i:ROW_VECS + i + 1, :], ins[i], outs[i],
                 lambda r: r[...].reshape(1, D), lambda o, val: o.__setitem__(Ellipsis, val.reshape(D)))
        for j in range(D_IN // D):
            piece = pl.ds(j * D, D)
            step(g_ref[ROW_B_IN + j:ROW_B_IN + j + 1, :], ins[N_VECS], outs[N_VECS],
                 lambda r: r[piece].reshape(1, D), lambda o, val: o.__setitem__(piece, val.reshape(D)))
        for h in range(N_GROUPS):
            step(g_ref[ROW_W_SPATIAL:ROW_W_SPATIAL + CHUNK, h * CHUNK:(h + 1) * CHUNK], ins[N_VECS + 1], outs[N_VECS + 1],
                 lambda r: r[h], lambda o, val: o.__setitem__(h, val))
        step(g_ref[ROW_B_SPATIAL:ROW_B_SPATIAL + N_GROUPS, 0:CHUNK], ins[N_VECS + 2], outs[N_VECS + 2],
             lambda r: r[...], lambda o, val: o.__setitem__(Ellipsis, val))
        lanes = g_ref[ROW_LOSS:ROW_LOSS + 1, :]
        loss_ref[...] = jnp.broadcast_to(jnp.sum(lanes, axis=1, keepdims=True), (8, 128))

    flat = [a for tr in triples for a in tr]
    out_shape = [jax.ShapeDtypeStruct(tr[0].shape, F32) for tr in triples for _ in range(4)]
    out_shape.append(jax.ShapeDtypeStruct((8, 128), F32))
    res = pl.pallas_call(
        body, name="adamw_small",
        in_specs=[_full(parts.shape)] + [_full(a.shape) for a in flat],
        out_specs=[_full(o.shape) for o in out_shape],
        out_shape=out_shape,
        scratch_shapes=[pltpu.VMEM((SMALL_ROWS, D), F32)],
        compiler_params=_params(),
    )(parts, *flat)
    return [res[4 * i:4 * i + 4] for i in range(len(triples))], res[-1]


SUBLANES = 8
SHIFT_ROWS = HALO - SUBLANES


def _shifted_copies(src_ref, sh_ref, cs, tm):
    for p in range(1, SUBLANES):
        sh_ref[p - 1] = src_ref[pl.ds(p, tm + SHIFT_ROWS), cs]


def _tap(src_ref, sh_ref, cs, offset, start, rows):
    p, q = offset % SUBLANES, offset // SUBLANES
    if p == 0:
        return src_ref[pl.ds(start + SUBLANES * q, rows), cs]
    return sh_ref[p - 1, pl.ds(start + SUBLANES * q, rows), :]


def _conv_taps(src_ref, sh_ref, w_ref, first_offset, step, bias, dst_ref, tm):
    rows = 64
    for g in range(N_GROUPS):
        cs = slice(g * GROUP_W, (g + 1) * GROUP_W)
        _shifted_copies(src_ref, sh_ref, cs, tm)
        for rb in range(tm // rows):
            acc = jnp.zeros((rows, GROUP_W), F32) + (bias[:, cs] if bias is not None else 0.0)
            for k in range(CONV_K):
                acc = acc + w_ref[k:k + 1, cs] * _tap(src_ref, sh_ref, cs, first_offset + step * k, rb * rows, rows)
            dst_ref[rb * rows:(rb + 1) * rows, cs] = acc


def _conv_weight_grad(d_ref, src_ref, sh_ref, first_offset, acc_ref, tm):
    rows = 64
    for g in range(N_GROUPS):
        cs = slice(g * GROUP_W, (g + 1) * GROUP_W)
        _shifted_copies(src_ref, sh_ref, cs, tm)
        for rb in range(tm // rows):
            d = d_ref[rb * rows:(rb + 1) * rows, cs]
            for k in range(CONV_K):
                prod = d * _tap(src_ref, sh_ref, cs, first_offset + k, rb * rows, rows)
                acc_ref[SUBLANES * k:SUBLANES * (k + 1), cs] += jnp.sum(
                    prod.reshape(rows // SUBLANES, SUBLANES, GROUP_W), axis=0)


def _spatial_mix(w_ref, v_bf, tm):
    rows = []
    for q in range(tm // CHUNK):
        cols = [_dot(w_ref[h], v_bf[q * CHUNK:(q + 1) * CHUNK, h * GROUP_W:(h + 1) * GROUP_W])
                for h in range(N_GROUPS)]
        rows.append(jnp.concatenate(cols, axis=1))
    return jnp.concatenate(rows, axis=0)


def _group_norm_fwd(h1, gn_g, gn_b):
    xhat, rstd = [], []
    for g in range(N_GROUPS):
        xh, rs = _norm_stats(h1[:, g * GROUP_W:(g + 1) * GROUP_W])
        xhat.append(xh)
        rstd.append(rs)
    xhat = jnp.concatenate(xhat, axis=1)
    return xhat * gn_g + gn_b, xhat, rstd


def _forward_tiles(p, x, tgt, wpa, wpb, wo, convw, vecs, ws, bsp, tiles_per_seq):
    t = x.shape[0]
    tm = TOKEN_TILE
    hb = tm // HALO

    def body(p_ref, ph_ref, x_ref, t_ref, wpa_ref, wpb_ref, wo_ref, cw_ref, vec_ref, ws_ref, bsp_ref,
             h1_ref, ya_ref, yb_ref, h3_ref, s_ref, mx_ref, dr_ref, drb_ref, xt_ref, acc_ref, he_ref, sh_ref):
        i = pl.program_id(0)
        xt_ref[...] = x_ref[...].T.astype(BF16)
        conv_b, gn_g, gn_b, lnv_g, lnv_b, b_o, lno_g, lno_b = [vec_ref[j:j + 1, :] for j in range(8)]

        keep = jnp.where(i % tiles_per_seq == 0, 0.0, 1.0)
        he_ref[0:HALO, :] = ph_ref[:, 0:D] * _sigmoid(ph_ref[:, D:2 * D]) * keep
        he_ref[HALO:, :] = p_ref[:, 0:D] * _sigmoid(p_ref[:, D:2 * D])
        _conv_taps(he_ref, sh_ref, cw_ref, HALO - (CONV_K - 1), 1, conv_b, h1_ref, tm)
        h2, _, _ = _group_norm_fwd(h1_ref[...], gn_g, gn_b)
        a_gate = p_ref[:, 2 * D:3 * D]
        h3 = ((h2 * _sigmoid(h2)) * (a_gate * _sigmoid(a_gate))).astype(BF16)
        h3_ref[...] = h3
        ya = _dot(h3, wpa_ref[...])
        ya_ref[...] = ya

        u = _gelu(p_ref[:, 3 * D:4 * D])
        vhat, _ = _norm_stats(_gelu(p_ref[:, 4 * D:5 * D]))
        v1 = (vhat * lnv_g + lnv_b).astype(BF16)
        b_gate = p_ref[:, 5 * D:6 * D]
        vmix = _spatial_mix(ws_ref, v1, tm) + jnp.concatenate([bsp_ref[...]] * (tm // CHUNK), axis=0)
        s = (u * vmix * (b_gate * _sigmoid(b_gate))).astype(BF16)
        s_ref[...] = s
        yb = _dot(s, wpb_ref[...])
        yb_ref[...] = yb

        mixed = (_sigmoid(p_ref[:, 6 * D:7 * D]) * ya + _sigmoid(p_ref[:, 7 * D:8 * D]) * yb).astype(BF16)
        mx_ref[...] = mixed
        r = ALPHA * x_ref[...] + (_dot(mixed, wo_ref[...]) + b_o)
        xhat, rstd = _norm_stats(r)
        err = (xhat * lno_g + lno_b) - t_ref[...]
        dout = err * (1.0 / D)
        dr = _norm_bwd(dout * lno_g, xhat, rstd)
        dr_ref[...] = dr
        drb_ref[...] = dr.astype(BF16)

        @pl.when(i == 0)
        def _():
            acc_ref[...] = jnp.zeros_like(acc_ref)

        acc_ref[0:1, :] += _colsum(dout * xhat)
        acc_ref[1:2, :] += _colsum(dout)
        acc_ref[2:3, :] += _colsum(dr)
        acc_ref[3:4, :] += _colsum(err * err) * (0.5 / D)

    tile = lambda w: pl.BlockSpec((tm, w), lambda i: (i, 0))
    f32_out = jax.ShapeDtypeStruct((t, D), F32)
    bf_out = jax.ShapeDtypeStruct((t, D), BF16)
    return pl.pallas_call(
        body, name="forward_tiles", grid=(t // tm,),
        in_specs=[tile(D_IN),
                  pl.BlockSpec((HALO, 2 * D), lambda i: (jnp.maximum(i * hb - 1, 0), 0)),
                  tile(D), tile(D), _resident((D, D)), _resident((D, D)), _resident((D, D)), _full((HALO, D)), _full((8, D)),
                  _full((N_GROUPS, CHUNK, CHUNK)), _full((CHUNK, D))],
        out_specs=[tile(D)] * 8 + [pl.BlockSpec((D, tm), lambda i: (0, i)), _full((8, D))],
        out_shape=[f32_out, f32_out, f32_out, bf_out, bf_out, bf_out, f32_out, bf_out,
                   jax.ShapeDtypeStruct((D, t), BF16), jax.ShapeDtypeStruct((8, D), F32)],
        scratch_shapes=[pltpu.VMEM((tm + HALO, D), F32), pltpu.VMEM((SUBLANES - 1, tm + SHIFT_ROWS, GROUP_W), F32)],
        compiler_params=_params("arbitrary"),
    )(p, p, x, tgt, wpa, wpb, wo, convw, vecs, ws, bsp)


def _backward_tiles(p, h1, ya, yb, drb, wpa, wpb, wo, vecs, ws, wst, bsp):
    t = h1.shape[0]
    tm = TOKEN_TILE

    def body(p_ref, h1_ref, ya_ref, yb_ref, drb_ref, wpa_ref, wpb_ref, wo_ref, vec_ref, ws_ref, wst_ref, bsp_ref,
             dh1_ref, dp_ref, dya_ref, dyb_ref, acc_ref, dbin_ref, dws_ref, dbsp_ref):
        i = pl.program_id(0)
        _, gn_g, gn_b, lnv_g, lnv_b = [vec_ref[j:j + 1, :] for j in range(5)]

        @pl.when(i == 0)
        def _():
            acc_ref[...] = jnp.zeros_like(acc_ref)
            dbin_ref[...] = jnp.zeros_like(dbin_ref)
            dws_ref[...] = jnp.zeros_like(dws_ref)
            dbsp_ref[...] = jnp.zeros_like(dbsp_ref)

        def emit(block, val):
            dbin_ref[0:1, block * D:(block + 1) * D] += _colsum(val)
            dp_ref[:, block * D:(block + 1) * D] = val.astype(BF16)

        dp_ref[:, 0:2 * D] = jnp.zeros((tm, 2 * D), BF16)
        dmixed = _dot_nt(drb_ref[...], wo_ref[...])
        ga = _sigmoid(p_ref[:, 6 * D:7 * D])
        gb = _sigmoid(p_ref[:, 7 * D:8 * D])
        dya = (dmixed * ga).astype(BF16)
        dyb = (dmixed * gb).astype(BF16)
        dya_ref[...] = dya
        dyb_ref[...] = dyb
        emit(6, dmixed * ya_ref[...] * (ga * (1.0 - ga)))
        emit(7, dmixed * yb_ref[...] * (gb * (1.0 - gb)))

        dh3 = _dot_nt(dya, wpa_ref[...])
        h2, xhat, rstd = _group_norm_fwd(h1_ref[...], gn_g, gn_b)
        sg = _sigmoid(h2)
        a_gate = p_ref[:, 2 * D:3 * D]
        sa = _sigmoid(a_gate)
        dh2 = dh3 * (a_gate * sa) * (sg * (1.0 + h2 * (1.0 - sg)))
        emit(2, dh3 * (h2 * sg) * (sa * (1.0 + a_gate * (1.0 - sa))))
        acc_ref[0:1, :] += _colsum(dh2 * xhat)
        acc_ref[1:2, :] += _colsum(dh2)
        dxhat = dh2 * gn_g
        for g in range(N_GROUPS):
            cs = slice(g * GROUP_W, (g + 1) * GROUP_W)
            dh1_ref[:, cs] = _norm_bwd(dxhat[:, cs], xhat[:, cs], rstd[g])

        ds = _dot_nt(dyb, wpb_ref[...])
        u_pre = p_ref[:, 3 * D:4 * D]
        u, du_dpre = _gelu_and_grad(u_pre)
        v0, dv_dpre = _gelu_and_grad(p_ref[:, 4 * D:5 * D])
        vhat, vrstd = _norm_stats(v0)
        v1 = (vhat * lnv_g + lnv_b).astype(BF16)
        vmix = _spatial_mix(ws_ref, v1, tm) + jnp.concatenate([bsp_ref[...]] * (tm // CHUNK), axis=0)
        b_gate = p_ref[:, 5 * D:6 * D]
        sb = _sigmoid(b_gate)
        silu_b = b_gate * sb
        emit(3, ds * vmix * silu_b * du_dpre)
        emit(5, ds * u * vmix * (sb * (1.0 + b_gate * (1.0 - sb))))
        dvmix = ds * u * silu_b
        dvmix_bf = dvmix.astype(BF16)
        for q in range(tm // CHUNK):
            dbsp_ref[...] += dvmix[q * CHUNK:(q + 1) * CHUNK, :]
            for h in range(N_GROUPS):
                blk = (slice(q * CHUNK, (q + 1) * CHUNK), slice(h * GROUP_W, (h + 1) * GROUP_W))
                dws_ref[:, h * GROUP_W:(h + 1) * GROUP_W] += _dot_nt(dvmix_bf[blk], v1[blk])
        dv1 = _spatial_mix(wst_ref, dvmix_bf, tm)
        acc_ref[2:3, :] += _colsum(dv1 * vhat)
        acc_ref[3:4, :] += _colsum(dv1)
        emit(4, _norm_bwd(dv1 * lnv_g, vhat, vrstd) * dv_dpre)

    tile = lambda w: pl.BlockSpec((tm, w), lambda i: (i, 0))
    return pl.pallas_call(
        body, name="backward_tiles", grid=(t // tm,),
        in_specs=[tile(D_IN), tile(D), tile(D), tile(D), tile(D), _resident((D, D)), _resident((D, D)), _resident((D, D)),
                  _full((8, D)), _full((N_GROUPS, CHUNK, CHUNK)), _full((N_GROUPS, CHUNK, CHUNK)), _full((CHUNK, D))],
        out_specs=[tile(D), tile(D_IN), tile(D), tile(D), _full((8, D)), _full((8, D_IN)),
                   _full((CHUNK, D)), _full((CHUNK, D))],
        out_shape=[jax.ShapeDtypeStruct((t, D), F32), jax.ShapeDtypeStruct((t, D_IN), BF16),
                   jax.ShapeDtypeStruct((t, D), BF16), jax.ShapeDtypeStruct((t, D), BF16),
                   jax.ShapeDtypeStruct((8, D), F32), jax.ShapeDtypeStruct((8, D_IN), F32),
                   jax.ShapeDtypeStruct((CHUNK, D), F32), jax.ShapeDtypeStruct((CHUNK, D), F32)],
        compiler_params=_params("arbitrary"),
    )(p, h1, ya, yb, drb, wpa, wpb, wo, vecs, ws, wst, bsp)


def _conv_backward(dh1, p, dp, convw, pairs, tiles_per_seq):
    t = dh1.shape[0]
    tm = TOKEN_TILE
    hb = tm // HALO
    last = t // HALO - 1
    n_sq = len(pairs)
    span = 2
    rows = D // 8

    def body(dh1_ref, dnext_ref, p_ref, ph_ref, cw_ref, dp_in_ref, *refs):
        del dp_in_ref
        sq_in = refs[:2 * n_sq]
        dp_ref, dcw_ref, dbin_ref = refs[2 * n_sq:2 * n_sq + 3]
        sq_out = refs[2 * n_sq + 3:3 * n_sq + 3]
        sq_wire = refs[3 * n_sq + 3:4 * n_sq + 3]
        de_ref, he_ref, dh0_ref, sh_ref, acc_ref, wire_ref, sq_sem, wire_sem = refs[4 * n_sq + 3:]
        i = pl.program_id(0)

        @pl.when(i == 0)
        def _():
            dcw_ref[...] = jnp.zeros_like(dcw_ref)
            dbin_ref[...] = jnp.zeros_like(dbin_ref)
            acc_ref[...] = jnp.zeros_like(acc_ref)

        @pl.when(i % span == span - 1)
        def _():
            for a in range(n_sq):
                acc_ref[a] += _dot_tn(sq_in[2 * a][...], sq_in[2 * a + 1][...])

        keep_next = jnp.where(i % tiles_per_seq == tiles_per_seq - 1, 0.0, 1.0)
        de_ref[0:tm, :] = dh1_ref[...]
        de_ref[tm:, :] = dnext_ref[...] * keep_next
        _conv_taps(de_ref, sh_ref, cw_ref, CONV_K - 1, -1, None, dh0_ref, tm)

        keep_prev = jnp.where(i % tiles_per_seq == 0, 0.0, 1.0)
        sg = _sigmoid(p_ref[:, D:2 * D])
        val = p_ref[:, 0:D]
        he_ref[0:HALO, :] = ph_ref[:, 0:D] * _sigmoid(ph_ref[:, D:2 * D]) * keep_prev
        he_ref[HALO:, :] = val * sg
        _conv_weight_grad(dh1_ref, he_ref, sh_ref, HALO - (CONV_K - 1), dcw_ref, tm)
        dcw_ref[SUBLANES * CONV_K:, :] += jnp.sum(dh1_ref[...].reshape(tm // SUBLANES, SUBLANES, D), axis=0)

        dh0 = dh0_ref[...]
        dval = dh0 * sg
        dglu = dh0 * val * (sg * (1.0 - sg))
        dbin_ref[0:1, 0:D] += _colsum(dval)
        dbin_ref[0:1, D:2 * D] += _colsum(dglu)
        dp_ref[:, 0:D] = dval.astype(BF16)
        dp_ref[:, D:2 * D] = dglu.astype(BF16)

        @pl.when(i == t // tm - 1)
        def _():
            cps = [pltpu.make_async_copy(acc_ref.at[a, pl.ds((2 * j + h) * rows, rows)], sq_out[a].at[h, j],
                                         sq_sem.at[(a * N_CHIPS + j) * 2 + h])
                   for a in range(n_sq) for j in range(N_CHIPS) for h in range(2)]
            for cp in cps:
                cp.start()
            for a in range(n_sq):
                wire_ref[...] = acc_ref[a].astype(BF16)
                narrow = [pltpu.make_async_copy(wire_ref.at[pl.ds((2 * j + h) * rows, rows)], sq_wire[a].at[h, j],
                                                wire_sem.at[2 * j + h]) for j in range(N_CHIPS) for h in range(2)]
                for cp in narrow:
                    cp.start()
                for cp in narrow:
                    cp.wait()
            for cp in cps:
                cp.wait()

    any_spec = pl.BlockSpec(memory_space=pl.ANY)
    wide = pl.BlockSpec((span * tm, D), lambda i: (i // span, 0))
    return pl.pallas_call(
        body, name="conv_backward", grid=(t // tm,),
        in_specs=[pl.BlockSpec((tm, D), lambda i: (i, 0)),
                  pl.BlockSpec((HALO, D), lambda i: (jnp.minimum((i + 1) * hb, last), 0)),
                  pl.BlockSpec((tm, 2 * D), lambda i: (i, 0)),
                  pl.BlockSpec((HALO, 2 * D), lambda i: (jnp.maximum(i * hb - 1, 0), 0)),
                  _full((HALO, D)), any_spec] + [wide] * (2 * n_sq),
        out_specs=[pl.BlockSpec((tm, 2 * D), lambda i: (i, 0)), _full((SUBLANES * HALO, D)), _full((8, 2 * D))]
        + [any_spec] * (2 * n_sq),
        out_shape=[jax.ShapeDtypeStruct(dp.shape, BF16), jax.ShapeDtypeStruct((SUBLANES * HALO, D), F32),
                   jax.ShapeDtypeStruct((8, 2 * D), F32)]
        + [jax.ShapeDtypeStruct((2, N_CHIPS, rows, D), F32)] * n_sq
        + [jax.ShapeDtypeStruct((2, N_CHIPS, rows, D), BF16)] * n_sq,
        scratch_shapes=[pltpu.VMEM((tm + HALO, D), F32), pltpu.VMEM((tm + HALO, D), F32), pltpu.VMEM((tm, D), F32),
                        pltpu.VMEM((SUBLANES - 1, tm + SHIFT_ROWS, GROUP_W), F32), pltpu.VMEM((n_sq, D, D), F32),
                        pltpu.VMEM((D, D), BF16), pltpu.SemaphoreType.DMA((n_sq * N_CHIPS * 2,)),
                        pltpu.SemaphoreType.DMA((N_CHIPS * 2,))],
        input_output_aliases={5: 0},
        compiler_params=_params("arbitrary"),
    )(dh1, dh1, p, p, convw, dp, *[a for pair in pairs for a in pair])


def _grad_in_and_x(xt, dp, w4, dr, wires, grads):
    t = dr.shape[0]
    tm = TOKEN_TILE
    half, tn = D // 2, 512
    nb = W_BLOCK // tn
    n_w, n_x = 2 * N_CHIPS * nb, t // tm
    ns = len(grads)
    xi, yi, ci = lax.axis_index("x"), lax.axis_index("y"), lax.axis_index("c")
    others = [2 * (1 - xi) + yi, 2 * xi + (1 - yi), 2 * (1 - xi) + (1 - yi)]
    blocks = others + others + [2 * xi + yi] * 2
    halves = [1 - ci] * 3 + [ci] * 3 + [1 - ci, ci]
    table = jnp.stack([jnp.stack([b * nb + n for b in blocks for n in range(nb)]),
                       jnp.stack([h for h in halves for _ in range(nb)])]).astype(jnp.int32)

    def body(tab_ref, xt_ref, dpc_ref, dpr_ref, w_ref, dr_ref, *refs):
        parts, fulls = refs[:ns], refs[ns:2 * ns]
        dx_ref, qk_ref, b2_ref, b1_ref, wire_ref = refs[2 * ns:2 * ns + 5]
        lands, sibs = refs[2 * ns + 5:3 * ns + 5], refs[3 * ns + 5:4 * ns + 5]
        (g_ref, st_ref, sb_ref, tmp_ref, d2d_send, d2d_recv, ici_send, ici_recv, own_sem, tmp_sem, wire_sem,
         p_send, p_recv, s_send, s_recv) = refs[4 * ns + 5:]
        s = pl.program_id(0)
        x_, y_, c, k = _position()
        chips = _other_chips(x_, y_)
        n = s % nb
        grp = s // nb
        cols = pl.ds(pl.multiple_of(n * tn, tn), tn)

        def part(a, r, core):
            cx, cy = chips[r]
            return pltpu.make_async_remote_copy(
                src_ref=parts[a].at[core, 2 * cx + cy], dst_ref=lands[a].at[2 * r + c],
                send_sem=p_send.at[6 * a + 2 * r + core], recv_sem=p_recv.at[6 * a + 2 * r + c],
                device_id=(cx, cy, core), device_id_type=MESH)

        def landed(a, r, core):
            cx, cy = chips[r]
            return pltpu.make_async_remote_copy(
                src_ref=lands[a].at[2 * r + core], dst_ref=lands[a].at[2 * r + core],
                send_sem=p_send.at[6 * a + 2 * r + core], recv_sem=p_recv.at[6 * a + 2 * r + core],
                device_id=(cx, cy, core), device_id_type=MESH)

        def to_sibling_whole(a):
            return pltpu.make_async_remote_copy(
                src_ref=fulls[a].at[1 - c, k], dst_ref=sibs[a], send_sem=s_send.at[a], recv_sem=s_recv.at[a],
                device_id=(x_, y_, 1 - c), device_id_type=MESH)

        def to_sibling(slot, land):
            return pltpu.make_async_remote_copy(
                src_ref=st_ref.at[slot], dst_ref=b1_ref.at[land, :, cols], send_sem=d2d_send.at[slot],
                recv_sem=d2d_recv.at[land * nb + n], device_id=(x_, y_, 1 - c), device_id_type=MESH)

        def to_chip(r):
            cx, cy = chips[r]
            return pltpu.make_async_remote_copy(
                src_ref=wire_ref.at[r, :, cols], dst_ref=b2_ref.at[r, :, cols], send_sem=ici_send.at[r],
                recv_sem=ici_recv.at[r], device_id=(cx, cy, c), device_id_type=MESH)

        def all_of_chip(r):
            cx, cy = chips[r]
            return pltpu.make_async_remote_copy(
                src_ref=wire_ref.at[r], dst_ref=b2_ref.at[r], send_sem=ici_send.at[r],
                recv_sem=ici_recv.at[r], device_id=(cx, cy, c), device_id_type=MESH)

        def to_result(slot):
            return pltpu.make_async_copy(st_ref.at[slot], qk_ref.at[:, cols], own_sem.at[slot])

        def sibling_piece(land):
            return pltpu.make_async_copy(b1_ref.at[land, :, cols], tmp_ref, tmp_sem)

        @pl.when(s == 0)
        def _():
            for a in range(ns):
                to_sibling_whole(a).start()
                for r in range(3):
                    for core in range(2):
                        part(a, r, core).start()

        own_half = ((grp >= 3) & (grp <= 5)) | (grp == 7)
        land = jnp.where(grp == 7, 3, grp - 3)

        @pl.when(own_half)
        def _():
            to_sibling(0, land).wait_recv()
            sibling_piece(land).start()

        @pl.when(s < n_w)
        def _():
            g_ref[...] = _dot(xt_ref[tab_ref[1, s]], dpc_ref[...])

        @pl.when(own_half)
        def _():
            sibling_piece(land).wait()

        for g in range(2 * N_CHIPS):
            @pl.when(grp == g)
            def _(g=g):
                if g in (0, 1, 2, 6):
                    use = s if g < 3 else 3 * nb + n
                    slot = use % 2

                    @pl.when(use >= 2)
                    def _():
                        to_sibling(slot, 0).wait_send()

                    st_ref[slot] = g_ref[...]
                    to_sibling(slot, min(g, 3)).start()
                elif g in (3, 4, 5):
                    sb_ref[...] = (g_ref[...] + tmp_ref[...]).astype(BF16)
                    stage = pltpu.make_async_copy(sb_ref, wire_ref.at[g - 3, :, cols], wire_sem)
                    stage.start()
                    stage.wait()
                    to_chip(g - 3).start()
                else:
                    slot = n % 2
                    piece = g_ref[...] + tmp_ref[...]

                    @pl.when(n < 2)
                    def _():
                        to_sibling(slot, 0).wait_send()

                    @pl.when(n >= 2)
                    def _():
                        to_result(slot).wait()

                    st_ref[slot] = piece
                    to_result(slot).start()

        @pl.when(s >= n_w)
        def _():
            acc = ALPHA * dr_ref[...]
            for j in range(N_CHIPS):
                acc = acc + _dot_nt(dpr_ref[:, j * W_BLOCK:(j + 1) * W_BLOCK], w_ref[j])
            dx_ref[...] = acc

        @pl.when(s == n_w + n_x - 1)
        def _():
            for slot in range(2):
                to_result(slot).wait()
            for r in range(3):
                all_of_chip(r).wait_recv()
                all_of_chip(r).wait_send()
            for a in range(ns):
                to_sibling_whole(a).wait_recv()
                to_sibling_whole(a).wait_send()
                for r in range(3):
                    for core in range(2):
                        landed(a, r, core).wait_recv()
                        part(a, r, core).wait_send()

    any_spec = pl.BlockSpec(memory_space=pl.ANY)
    tile = lambda s, tab: (jnp.maximum(s - n_w, 0), 0)
    return pl.pallas_call(
        body, name="grad_in_and_x",
        grid_spec=pltpu.PrefetchScalarGridSpec(
            num_scalar_prefetch=1, grid=(n_w + n_x,),
            in_specs=[pl.BlockSpec((2, half, t), lambda s, tab: (0, 0, 0), pipeline_mode=pl.Buffered(1)),
                      pl.BlockSpec((t, tn), lambda s, tab: (0, tab[0, jnp.minimum(s, n_w - 1)])),
                      pl.BlockSpec((tm, D_IN), tile),
                      pl.BlockSpec((N_CHIPS, D, W_BLOCK), lambda s, tab: (0, 0, 0), pipeline_mode=pl.Buffered(1)),
                      pl.BlockSpec((tm, D), tile)] + [any_spec] * (2 * ns),
            out_specs=[pl.BlockSpec((tm, D), tile)] + [any_spec] * (4 + 2 * ns),
            scratch_shapes=[pltpu.VMEM((half, tn), F32), pltpu.VMEM((2, half, tn), F32), pltpu.VMEM((half, tn), BF16),
                            pltpu.VMEM((half, tn), F32),
                            pltpu.SemaphoreType.DMA((2,)), pltpu.SemaphoreType.DMA((N_CHIPS * nb,)),
                            pltpu.SemaphoreType.DMA((3,)), pltpu.SemaphoreType.DMA((3,)),
                            pltpu.SemaphoreType.DMA((2,)), pltpu.SemaphoreType.DMA, pltpu.SemaphoreType.DMA,
                            pltpu.SemaphoreType.DMA((6 * ns,)), pltpu.SemaphoreType.DMA((6 * ns,)),
                            pltpu.SemaphoreType.DMA((ns,)), pltpu.SemaphoreType.DMA((ns,))]),
        out_shape=[jax.ShapeDtypeStruct((t, D), F32), jax.ShapeDtypeStruct((half, W_BLOCK), F32),
                   jax.ShapeDtypeStruct((3, half, W_BLOCK), BF16), jax.ShapeDtypeStruct((N_CHIPS, half, W_BLOCK), F32),
                   jax.ShapeDtypeStruct((3, half, W_BLOCK), BF16)]
        + [jax.ShapeDtypeStruct((6,) + w.shape[2:], w.dtype) for w in wires]
        + [jax.ShapeDtypeStruct(g.shape[2:], F32) for g in grads],
        compiler_params=_params("arbitrary"),
    )(table, xt, dp, dp, w4, dr, *wires, *grads)


def kernel(x, w_in, b_in, conv_w, conv_b, gn_g, gn_b, ln_v_g, ln_v_b, w_spatial, b_spatial, w_pa, w_pb, w_o, b_o, ln_out_g, ln_out_b, loss_target, m_w_in, m_b_in, m_conv_w, m_conv_b, m_gn_g, m_gn_b, m_ln_v_g, m_ln_v_b, m_w_spatial, m_b_spatial, m_w_pa, m_w_pb, m_w_o, m_b_o, m_ln_out_g, m_ln_out_b, v_w_in, v_b_in, v_conv_w, v_conv_b, v_gn_g, v_gn_b, v_ln_v_g, v_ln_v_b, v_w_spatial, v_b_spatial, v_w_pa, v_w_pb, v_w_o, v_b_o, v_ln_out_g, v_ln_out_b):
    n_seq, seq, _ = x.shape
    t = n_seq * seq
    tiles_per_seq = seq // TOKEN_TILE
    x2 = x.reshape(t, D)
    tgt = loss_target.reshape(t, D)

    conv_shard = jnp.pad(conv_w, ((0, HALO - CONV_K), (0, 0)))
    p, win4, wpa4, wpb4, wo4, conv4 = _proj_gather(
        x2, b_in.reshape(N_CHIPS, 1, W_BLOCK),
        [_place_shard(w_in, 256, BF16), _place_shard(w_pa, 128, BF16), _place_shard(w_pb, 128, BF16),
         _place_shard(w_o, 128, BF16), _place_shard(conv_shard, HALO // 2, F32)])
    win4 = win4.reshape(N_CHIPS, D, W_BLOCK)
    wpa, wpb, wo = wpa4.reshape(D, D), wpb4.reshape(D, D), wo4.reshape(D, D)
    convw = conv4.reshape(N_CHIPS, HALO, D // N_CHIPS).transpose(1, 0, 2).reshape(HALO, D)

    vecs = jnp.stack([conv_b, gn_g, gn_b, ln_v_g, ln_v_b, b_o, ln_out_g, ln_out_b])
    causal = jnp.tril(jnp.ones((CHUNK, CHUNK), bool))
    ws = jnp.where(causal[None], w_spatial, 0.0)
    ws_bf, wst_bf = ws.astype(BF16), ws.transpose(0, 2, 1).astype(BF16)
    bsp = jnp.repeat(b_spatial.T, GROUP_W, axis=1)

    h1, ya, yb, h3, s, mixed, dr, drb, xt, acc_f = _forward_tiles(p, x2, tgt, wpa, wpb, wo, convw, vecs, ws_bf, bsp, tiles_per_seq)
    dh1, dp, dya, dyb, acc_b, dbin_b, dws, dbsp_acc = _backward_tiles(p, h1, ya, yb, drb, wpa, wpb, wo, vecs, ws_bf, wst_bf, bsp)
    dp, dcw8, dbin_a, *square = _conv_backward(dh1, p, dp, convw, [(h3, dya), (s, dyb), (mixed, drb)], tiles_per_seq)

    small, g_conv = _pack_small(acc_f, acc_b, dbin_a, dbin_b, dcw8, dws, dbsp_acc)
    small = small.reshape(2, N_CHIPS, SMALL_ROWS // 8, D)

    grads = square[:3] + [g_conv, small]
    wires = square[3:] + [g_conv, small]
    grad_x, q_in, chips_in, _, _, *landed = _grad_in_and_x(xt.reshape(2, D // 2, t), dp, win4, dr, wires, grads)
    grad_x = grad_x.reshape(x.shape)
    mine = [_add_chips(q_in, chips_in)]
    mine += [_add_devices(g, land, sib, a == 4) for a, (g, land, sib) in enumerate(zip(grads, landed[:5], landed[5:]))]
    *full, small_parts = _share_results(mine[:5], mine[5])
    grad_w_in, grad_w_pa, grad_w_pb, grad_w_o = [f.reshape(w.shape) for f, w in zip(full[:4], (w_in, w_pa, w_pb, w_o))]
    grad_conv_w = full[4].reshape(HALO, D // N_CHIPS)[:CONV_K]

    big = {}
    for name, w, g, m, v in [("w_in", w_in, grad_w_in, m_w_in, v_w_in), ("w_pa", w_pa, grad_w_pa, m_w_pa, v_w_pa),
                             ("w_pb", w_pb, grad_w_pb, m_w_pb, v_w_pb), ("w_o", w_o, grad_w_o, m_w_o, v_w_o),
                             ("conv_w", conv_w, grad_conv_w, m_conv_w, v_conv_w)]:
        big[name] = (g,) + tuple(_adamw(w, g, m, v))
    vec_names = ["conv_b", "gn_g", "gn_b", "ln_v_g", "ln_v_b", "b_o", "ln_out_g", "ln_out_b"]
    vec_triples = [(conv_b, m_conv_b, v_conv_b), (gn_g, m_gn_g, v_gn_g), (gn_b, m_gn_b, v_gn_b),
                   (ln_v_g, m_ln_v_g, v_ln_v_g), (ln_v_b, m_ln_v_b, v_ln_v_b), (b_o, m_b_o, v_b_o),
                   (ln_out_g, m_ln_out_g, v_ln_out_g), (ln_out_b, m_ln_out_b, v_ln_out_b)]
    small_res, loss8 = _adamw_small(
        small_parts.reshape(8, SMALL_ROWS // 8, D), vec_triples, (b_in, m_b_in, v_b_in),
        (w_spatial, m_w_spatial, v_w_spatial), (b_spatial, m_b_spatial, v_b_spatial))
    per_name = dict(zip(vec_names + ["b_in", "w_spatial", "b_spatial"], small_res))

    order = ["w_in", "b_in", "conv_w", "conv_b", "gn_g", "gn_b", "ln_v_g", "ln_v_b", "w_spatial", "b_spatial",
             "w_pa", "w_pb", "w_o", "b_o", "ln_out_g", "ln_out_b"]
    outs = [loss8[0, 0], grad_x]
    for kind in range(4):
        outs += [big[n][kind] if n in big else per_name[n][kind] for n in order]
    return tuple(outs)
```

```python
import functools
import math

import jax
import jax.numpy as jnp
from jax import lax
from jax.experimental import pallas as pl
from jax.experimental.pallas import tpu as pltpu

D = 1024
N_GROUPS = 8
GROUP_W = D // N_GROUPS
CHUNK = 128
CONV_K = 31
HALO = 32
D_IN = 8 * D
N_CHIPS = 4
W_BLOCK = D_IN // N_CHIPS
ALPHA = 2.0 ** 0.25
LN_EPS = 1e-5
ADAM_LR, ADAM_B1, ADAM_B2, ADAM_EPS, ADAM_WD, ADAM_STEP = 0.001, 0.9, 0.999, 1e-08, 0.01, 10

TOKEN_TILE = 256
VMEM_LIMIT = 56 * 1024 * 1024
MESH = pl.DeviceIdType.MESH
F32, BF16 = jnp.float32, jnp.bfloat16


def _sigmoid(x):
    return 1.0 / (1.0 + jnp.exp(-x))


def _gelu(x):
    c = math.sqrt(2.0 / math.pi)
    t = jnp.tanh(c * (x + 0.044715 * (x * x * x)))
    return x * (0.5 * (1.0 + t))


def _gelu_and_grad(x):
    c = math.sqrt(2.0 / math.pi)
    x2 = x * x
    t = jnp.tanh(c * (x + 0.044715 * (x2 * x)))
    cdf = 0.5 * (1.0 + t)
    return x * cdf, cdf + 0.5 * x * (1.0 - t * t) * (c * (1.0 + 3.0 * 0.044715 * x2))


def _norm_stats(v):
    mu = jnp.mean(v, axis=-1, keepdims=True)
    vc = v - mu
    var = jnp.mean(vc * vc, axis=-1, keepdims=True)
    rstd = lax.rsqrt(var + LN_EPS)
    return vc * rstd, rstd


def _norm_bwd(dxhat, xhat, rstd):
    m1 = jnp.mean(dxhat, axis=-1, keepdims=True)
    m2 = jnp.mean(dxhat * xhat, axis=-1, keepdims=True)
    return rstd * (dxhat - m1 - xhat * m2)


def _dot(a, b):
    return jnp.dot(a, b, preferred_element_type=F32)


def _dot_nt(a, b):
    return lax.dot_general(a, b, (((1,), (1,)), ((), ())), preferred_element_type=F32)


def _dot_tn(a, b):
    return lax.dot_general(a, b, (((0,), (0,)), ((), ())), preferred_element_type=F32)


def _colsum(v):
    return jnp.sum(v, axis=0, keepdims=True)


def _full(shape):
    return pl.BlockSpec(shape, lambda *_: (0,) * len(shape))


def _resident(shape):
    return pl.BlockSpec(shape, lambda *_: (0,) * len(shape), pipeline_mode=pl.Buffered(1))


def _params(*sem):
    return pltpu.CompilerParams(dimension_semantics=sem, vmem_limit_bytes=VMEM_LIMIT)


def _chip_index():
    return (2 * lax.axis_index("x") + lax.axis_index("y")).astype(jnp.int32).reshape(1)


def _core_index():
    return lax.axis_index("c").astype(jnp.int32).reshape(1)


def _place_shard(w, rows, dtype):
    r, c = w.shape
    steps = r // 2 // rows

    def body(k_ref, w_ref, o_ref):
        o_ref[...] = w_ref[...].astype(dtype)

    return pl.pallas_call(
        body, name="place_shard",
        grid_spec=pltpu.PrefetchScalarGridSpec(
            num_scalar_prefetch=1, grid=(2, steps),
            in_specs=[pl.BlockSpec((rows, c), lambda h, i, k: (h * steps + i, 0))],
            out_specs=pl.BlockSpec((None, None, rows, c), lambda h, i, k: (k[0], h, i, 0))),
        out_shape=jax.ShapeDtypeStruct((N_CHIPS, 2, r // 2, c), dtype),
        compiler_params=_params("parallel", "parallel"),
    )(_chip_index(), w)


def _position():
    x, y, c = lax.axis_index("x"), lax.axis_index("y"), lax.axis_index("c")
    return x, y, c, 2 * x + y


def _other_chips(x, y):
    return [(1 - x, y), (x, 1 - y), (1 - x, 1 - y)]


def _any_specs(n):
    return [pl.BlockSpec(memory_space=pl.ANY)] * n


def _proj_gather(x, b4, bufs):
    t = x.shape[0]
    tm = 1024
    steps = t // tm
    ahead = steps // 2
    half = D // 2
    n = len(bufs)
    xi, yi = lax.axis_index("x"), lax.axis_index("y")
    order = jnp.stack([2 * xi + yi, 2 * (1 - xi) + yi, 2 * xi + (1 - yi), 2 * (1 - xi) + (1 - yi)]).astype(jnp.int32)

    def body(order_ref, x_ref, b_ref, *refs):
        p_ref, outs = refs[n], refs[n + 1:2 * n + 1]
        xb_ref, w_ref, lsem, send, recv, hop_send, hop_recv, fsend, frecv = refs[2 * n + 1:]
        jj, i = pl.program_id(0), pl.program_id(1)
        x_, y_, c, k = _position()
        nbrs = [(1 - x_, y_), (x_, 1 - y_)]
        blocks = [2 * (1 - x_) + y_, 2 * x_ + (1 - y_), 2 * (1 - x_) + (1 - y_)]

        def quarter(a, block, q):
            rows = outs[a].shape[2] // 2
            return outs[a].at[block, c, pl.ds(q * rows, rows)]

        def copy(ref, to, send_sem, recv_sem):
            return pltpu.make_async_remote_copy(src_ref=ref, dst_ref=ref, send_sem=send_sem, recv_sem=recv_sem,
                                                device_id=(to[0], to[1], c), device_id_type=MESH)

        def sent(a, nb, q):
            return copy(quarter(a, k, q), nbrs[nb], send.at[4 * a + 2 * nb + q], recv.at[4 * a + 2 * nb + q])

        def landed(a, nb, q):
            return copy(quarter(a, blocks[nb], q), nbrs[nb], send.at[4 * a + 2 * nb + q], recv.at[4 * a + 2 * nb + q])

        def hopped(a, nb):
            return copy(quarter(a, blocks[nb], nb), nbrs[1 - nb], hop_send.at[2 * a + nb], hop_recv.at[2 * a + 1 - nb])

        def from_diagonal(a, via):
            return copy(quarter(a, blocks[2], 1 - via), nbrs[via], hop_send.at[2 * a + via], hop_recv.at[2 * a + via])

        def passed(a, r, h):
            return pltpu.make_async_remote_copy(
                src_ref=outs[a].at[blocks[r], h], dst_ref=outs[a].at[blocks[r], h], send_sem=fsend.at[3 * a + r],
                recv_sem=frecv.at[3 * a + r], device_id=(x_, y_, 1 - c), device_id_type=MESH)

        def load(block, slot):
            return [pltpu.make_async_copy(outs[0].at[block, h], w_ref.at[slot, pl.ds(h * half, half)],
                                          lsem.at[2 * slot + h]) for h in range(2)]

        def pass_on(arrays):
            for a in arrays:
                for nb in range(2):
                    landed(a, nb, nb).wait_recv()
                    hopped(a, nb).start()

        @pl.when((jj == 0) & (i == 0))
        def _():
            for a in range(n):
                for nb, q in ((0, 0), (1, 1), (0, 1), (1, 0)):
                    sent(a, nb, q).start()
            for cp in load(k, 0):
                cp.start()

        @pl.when((jj == 0) & (i == ahead))
        def _():
            pass_on([0])
            landed(0, 0, 1).wait_recv()
            passed(0, 0, c).start()
            passed(0, 0, 1 - c).wait_recv()
            for cp in load(blocks[0], 1):
                cp.start()

        @pl.when((jj == 1) & (i == ahead))
        def _():
            landed(0, 1, 0).wait_recv()
            passed(0, 1, c).start()
            passed(0, 1, 1 - c).wait_recv()
            for cp in load(blocks[1], 0):
                cp.start()
            pass_on(range(1, n))
            for a in range(1, n):
                landed(a, 0, 1).wait_recv()
                passed(a, 0, c).start()
                landed(a, 1, 0).wait_recv()
                passed(a, 1, c).start()

        @pl.when((jj == 2) & (i == ahead))
        def _():
            from_diagonal(0, 0).wait_recv()
            from_diagonal(0, 1).wait_recv()
            passed(0, 2, c).start()
            passed(0, 2, 1 - c).wait_recv()
            for cp in load(blocks[2], 1):
                cp.start()

        slot = jj % 2

        @pl.when(i == 0)
        def _():
            for cp in load(k, slot):
                cp.wait()

        rows = pl.ds(pl.multiple_of(i * tm, tm), tm)

        @pl.when(jj == 0)
        def _():
            xb_ref[rows, :] = x_ref[...].astype(BF16)

        p_ref[...] = _dot(xb_ref[rows, :], w_ref[slot]) + b_ref[...]

        @pl.when((jj == N_CHIPS - 1) & (i == steps - 1))
        def _():
            for a in range(1, n):
                from_diagonal(a, 0).wait_recv()
                from_diagonal(a, 1).wait_recv()
                passed(a, 2, c).start()
            for a in range(1, n):
                for r in range(3):
                    passed(a, r, 1 - c).wait_recv()
            for a in range(n):
                for nb in range(2):
                    for q in range(2):
                        sent(a, nb, q).wait_send()
                    hopped(a, nb).wait_send()
                for r in range(3):
                    passed(a, r, c).wait_send()

    any_spec = pl.BlockSpec(memory_space=pl.ANY)
    return pl.pallas_call(
        body, name="proj_gather",
        grid_spec=pltpu.PrefetchScalarGridSpec(
            num_scalar_prefetch=1, grid=(N_CHIPS, steps),
            in_specs=[pl.BlockSpec((tm, D), lambda jj, i, o: (jnp.where(jj == 0, i, steps - 1), 0)),
                      pl.BlockSpec((None, 1, W_BLOCK), lambda jj, i, o: (o[jj], 0, 0))] + [any_spec] * n,
            out_specs=[pl.BlockSpec((tm, W_BLOCK), lambda jj, i, o: (i, o[jj]))] + [any_spec] * n,
            scratch_shapes=[pltpu.VMEM((t, D), BF16), pltpu.VMEM((2, D, W_BLOCK), BF16), pltpu.SemaphoreType.DMA((4,)),
                            pltpu.SemaphoreType.DMA((4 * n,)), pltpu.SemaphoreType.DMA((4 * n,)),
                            pltpu.SemaphoreType.DMA((2 * n,)), pltpu.SemaphoreType.DMA((2 * n,)),
                            pltpu.SemaphoreType.DMA((3 * n,)), pltpu.SemaphoreType.DMA((3 * n,))]),
        out_shape=[jax.ShapeDtypeStruct((t, D_IN), F32)] + [jax.ShapeDtypeStruct(b.shape, b.dtype) for b in bufs],
        input_output_aliases={3 + a: 1 + a for a in range(n)},
        compiler_params=_params("arbitrary", "arbitrary"),
    )(order, x, b4, *bufs)


def _share_results(bufs, small):
    n = len(bufs)

    def body(*refs):
        outs, small_out = refs[n + 1:2 * n + 1], refs[2 * n + 1]
        send, recv, ssend, srecv = refs[2 * n + 2:]
        x, y, c, k = _position()
        cps = []
        for a in range(n):
            cp = pltpu.make_async_remote_copy(
                src_ref=outs[a].at[c], dst_ref=outs[a].at[c], send_sem=send.at[a], recv_sem=recv.at[a],
                device_id=(x, y, 1 - c), device_id_type=MESH)
            cp.start()
            cps.append(cp)
        waits = []
        for p in range(1, 8):
            px, py, pc = x ^ (p >> 2), y ^ ((p >> 1) & 1), c ^ (p & 1)
            cp = pltpu.make_async_remote_copy(
                src_ref=small_out.at[k, c], dst_ref=small_out.at[k, c], send_sem=ssend.at[p - 1],
                recv_sem=srecv.at[p - 1], device_id=(px, py, pc), device_id_type=MESH)
            cp.start()
            cps.append(cp)
            waits.append(pltpu.make_async_remote_copy(
                src_ref=small_out.at[2 * px + py, pc], dst_ref=small_out.at[2 * px + py, pc], send_sem=ssend.at[p - 1],
                recv_sem=srecv.at[p - 1], device_id=(px, py, pc), device_id_type=MESH))
        for a in range(n):
            pltpu.make_async_remote_copy(
                src_ref=outs[a].at[1 - c], dst_ref=outs[a].at[1 - c], send_sem=send.at[a], recv_sem=recv.at[a],
                device_id=(x, y, 1 - c), device_id_type=MESH).wait_recv()
        for w in waits:
            w.wait_recv()
        for cp in cps:
            cp.wait_send()

    return pl.pallas_call(
        body, name="rs_share_results",
        in_specs=_any_specs(n + 1), out_specs=_any_specs(n + 1),
        out_shape=[jax.ShapeDtypeStruct(b.shape, b.dtype) for b in bufs + [small]],
        scratch_shapes=[pltpu.SemaphoreType.DMA((n,)), pltpu.SemaphoreType.DMA((n,)),
                        pltpu.SemaphoreType.DMA((7,)), pltpu.SemaphoreType.DMA((7,))],
        input_output_aliases={a: a for a in range(n + 1)},
    )(*bufs, small)


def _row_tile(r, c):
    t = max(8, min(r, (1 << 18) // c))
    while r % t:
        t //= 2
    return t


def _add_devices(g, lands, sib, per_device):
    r, c = g.shape[-2:]
    t = _row_tile(r, c)

    def body(kc_ref, g_ref, l_ref, s_ref, f_ref):
        f = g_ref[...] + s_ref[...]
        for i in range(l_ref.shape[0]):
            f = f + l_ref[i].astype(F32)
        f_ref[...] = f

    if per_device:
        out_spec = pl.BlockSpec((None, None, t, c), lambda i, kc: (kc[0], kc[1], i, 0))
        out_shape = jax.ShapeDtypeStruct((N_CHIPS, 2, r, c), F32)
    else:
        out_spec = pl.BlockSpec((None, t, c), lambda i, kc: (kc[1], i, 0))
        out_shape = jax.ShapeDtypeStruct((2, r, c), F32)
    return pl.pallas_call(
        body, name="rs_add_devices",
        grid_spec=pltpu.PrefetchScalarGridSpec(
            num_scalar_prefetch=1, grid=(r // t,),
            in_specs=[pl.BlockSpec((None, None, t, c), lambda i, kc: (kc[1], kc[0], i, 0)),
                      pl.BlockSpec((lands.shape[0], t, c), lambda i, kc: (0, i, 0)),
                      pl.BlockSpec((t, c), lambda i, kc: (i, 0))],
            out_specs=out_spec),
        out_shape=out_shape,
        compiler_params=_params("parallel"),
    )(jnp.concatenate([_chip_index(), _core_index()]), g, lands, sib)


def _add_chips(q, b2):
    r, c = q.shape
    t = _row_tile(r, c)

    def body(c_ref, q_ref, b_ref, f_ref):
        f_ref[...] = ((q_ref[...] + b_ref[0].astype(F32)) + b_ref[1].astype(F32)) + b_ref[2].astype(F32)

    return pl.pallas_call(
        body, name="rs_add_chips",
        grid_spec=pltpu.PrefetchScalarGridSpec(
            num_scalar_prefetch=1, grid=(r // t,),
            in_specs=[pl.BlockSpec((t, c), lambda i, cr: (i, 0)), pl.BlockSpec((3, t, c), lambda i, cr: (0, i, 0))],
            out_specs=pl.BlockSpec((None, t, c), lambda i, cr: (cr[0], i, 0))),
        out_shape=jax.ShapeDtypeStruct((2, r, c), F32),
        compiler_params=_params("parallel"),
    )(_core_index(), q, b2)


def _adamw_math(w, g, m, v):
    m = ADAM_B1 * m + (1.0 - ADAM_B1) * g
    v = ADAM_B2 * v + (1.0 - ADAM_B2) * (g * g)
    m_hat = m / (1.0 - ADAM_B1 ** ADAM_STEP)
    v_hat = v / (1.0 - ADAM_B2 ** ADAM_STEP)
    delta = -ADAM_LR * (m_hat / (jnp.sqrt(v_hat) + ADAM_EPS) + ADAM_WD * w)
    return delta, m, v


def _adamw(w, g, m, v):
    r, c = w.shape
    t = _row_tile(r, c) if r % 8 == 0 else r

    def body(w_ref, g_ref, m_ref, v_ref, d_ref, nm_ref, nv_ref):
        d_ref[...], nm_ref[...], nv_ref[...] = _adamw_math(w_ref[...], g_ref[...], m_ref[...], v_ref[...])

    spec = pl.BlockSpec((t, c), lambda i: (i, 0))
    return pl.pallas_call(
        body, name="adamw", grid=(r // t,), in_specs=[spec] * 4, out_specs=[spec] * 3,
        out_shape=[jax.ShapeDtypeStruct((r, c), F32)] * 3, compiler_params=_params("parallel"),
    )(w, g, m, v)


ROW_B_IN = 0
ROW_VECS = 8
ROW_LOSS = 16
ROW_B_SPATIAL = 24
ROW_W_SPATIAL = 32
SMALL_ROWS = 192
N_VECS = 8


def _pack_small(acc_f, acc_b, dbin_a, dbin_b, dcw8, dws, dbsp):
    cols = D // N_CHIPS

    def body(af_ref, ab_ref, da_ref, db_ref, cw_ref, ws_ref, bs_ref, o_ref, gc_ref):
        o_ref[...] = jnp.zeros_like(o_ref)
        for j in range(D_IN // D):
            src = da_ref if j < 2 else db_ref
            o_ref[ROW_B_IN + j:ROW_B_IN + j + 1, :] = src[0:1, j * D:(j + 1) * D]
        dcw = jnp.sum(cw_ref[...].reshape(HALO, SUBLANES, D), axis=1)
        o_ref[ROW_VECS:ROW_VECS + 1, :] = dcw[CONV_K:CONV_K + 1]
        o_ref[ROW_VECS + 1:ROW_VECS + 5, :] = ab_ref[0:4, :]
        o_ref[ROW_VECS + 5:ROW_VECS + 6, :] = af_ref[2:3, :]
        o_ref[ROW_VECS + 6:ROW_VECS + 8, :] = af_ref[0:2, :]
        o_ref[ROW_LOSS:ROW_LOSS + 1, :] = af_ref[3:4, :]
        head = lax.broadcasted_iota(jnp.int32, (N_GROUPS, D), 0)
        lane = lax.broadcasted_iota(jnp.int32, (N_GROUPS, D), 1)
        indicator = jnp.where(lane // GROUP_W == head, 1.0, 0.0)
        o_ref[ROW_B_SPATIAL:ROW_B_SPATIAL + N_GROUPS, 0:CHUNK] = lax.dot_general(
            indicator, bs_ref[...], (((1,), (1,)), ((), ())), precision=lax.Precision.HIGHEST, preferred_element_type=F32)
        t_idx = lax.broadcasted_iota(jnp.int32, (CHUNK, D), 0)
        s_idx = lax.broadcasted_iota(jnp.int32, (CHUNK, D), 1) % CHUNK
        o_ref[ROW_W_SPATIAL:ROW_W_SPATIAL + CHUNK, :] = jnp.where(s_idx <= t_idx, ws_ref[...], 0.0)
        for h in range(2):
            for j in range(N_CHIPS):
                gc_ref[h, j] = dcw[h * (HALO // 2):(h + 1) * (HALO // 2), j * cols:(j + 1) * cols]

    ins = [acc_f, acc_b, dbin_a, dbin_b, dcw8, dws, dbsp]
    return pl.pallas_call(
        body, name="pack_small",
        in_specs=[_full(a.shape) for a in ins],
        out_specs=[_full((SMALL_ROWS, D)), _full((2, N_CHIPS, HALO // 2, cols))],
        out_shape=[jax.ShapeDtypeStruct((SMALL_ROWS, D), F32), jax.ShapeDtypeStruct((2, N_CHIPS, HALO // 2, cols), F32)],
        compiler_params=_params(),
    )(*ins)


def _adamw_small(parts, vecs, b_in, w_spatial, b_spatial):
    triples = list(vecs) + [b_in, w_spatial, b_spatial]
    n_in = 3 * len(triples)

    def body(p_ref, *refs):
        ins = [refs[3 * i:3 * i + 3] for i in range(len(triples))]
        outs = [refs[n_in + 4 * i:n_in + 4 * i + 4] for i in range(len(triples))]
        loss_ref, g_ref = refs[n_in + 4 * len(triples):]
        rows = SMALL_ROWS // 8
        for k in range(N_CHIPS):
            for core in range(2):
                g_ref[(core * N_CHIPS + k) * rows:(core * N_CHIPS + k + 1) * rows, :] = p_ref[2 * k + core]

        def step(g, wmv, out, get, put):
            d, nm, nv = _adamw_math(get(wmv[0]), g, get(wmv[1]), get(wmv[2]))
            for o, val in zip(out, (g, d, nm, nv)):
                put(o, val)

        for i in range(N_VECS):
            step(g_ref[ROW_VECS + i:ROW_VECS + i + 1, :], ins[i], outs[i],
                 lambda r: r[...].reshape(1, D), lambda o, val: o.__setitem__(Ellipsis, val.reshape(D)))
        for j in range(D_IN // D):
            piece = pl.ds(j * D, D)
            step(g_ref[ROW_B_IN + j:ROW_B_IN + j + 1, :], ins[N_VECS], outs[N_VECS],
                 lambda r: r[piece].reshape(1, D), lambda o, val: o.__setitem__(piece, val.reshape(D)))
        for h in range(N_GROUPS):
            step(g_ref[ROW_W_SPATIAL:ROW_W_SPATIAL + CHUNK, h * CHUNK:(h + 1) * CHUNK], ins[N_VECS + 1], outs[N_VECS + 1],
                 lambda r: r[h], lambda o, val: o.__setitem__(h, val))
        step(g_ref[ROW_B_SPATIAL:ROW_B_SPATIAL + N_GROUPS, 0:CHUNK], ins[N_VECS + 2], outs[N_VECS + 2],
             lambda r: r[...], lambda o, val: o.__setitem__(Ellipsis, val))
        lanes = g_ref[ROW_LOSS:ROW_LOSS + 1, :]
        loss_ref[...] = jnp.broadcast_to(jnp.sum(lanes, axis=1, keepdims=True), (8, 128))

    flat = [a for tr in triples for a in tr]
    out_shape = [jax.ShapeDtypeStruct(tr[0].shape, F32) for tr in triples for _ in range(4)]
    out_shape.append(jax.ShapeDtypeStruct((8, 128), F32))
    res = pl.pallas_call(
        body, name="adamw_small",
        in_specs=[_full(parts.shape)] + [_full(a.shape) for a in flat],
        out_specs=[_full(o.shape) for o in out_shape],
        out_shape=out_shape,
        scratch_shapes=[pltpu.VMEM((SMALL_ROWS, D), F32)],
        compiler_params=_params(),
    )(parts, *flat)
    return [res[4 * i:4 * i + 4] for i in range(len(triples))], res[-1]


SUBLANES = 8
SHIFT_ROWS = HALO - SUBLANES


def _shifted_copies(src_ref, sh_ref, cs, tm):
    for p in range(1, SUBLANES):
        sh_ref[p - 1] = src_ref[pl.ds(p, tm + SHIFT_ROWS), cs]


def _tap(src_ref, sh_ref, cs, offset, start, rows):
    p, q = offset % SUBLANES, offset // SUBLANES
    if p == 0:
        return src_ref[pl.ds(start + SUBLANES * q, rows), cs]
    return sh_ref[p - 1, pl.ds(start + SUBLANES * q, rows), :]


def _conv_taps(src_ref, sh_ref, w_ref, first_offset, step, bias, dst_ref, tm):
    rows = 64
    for g in range(N_GROUPS):
        cs = slice(g * GROUP_W, (g + 1) * GROUP_W)
        _shifted_copies(src_ref, sh_ref, cs, tm)
        for rb in range(tm // rows):
            acc = jnp.zeros((rows, GROUP_W), F32) + (bias[:, cs] if bias is not None else 0.0)
            for k in range(CONV_K):
                acc = acc + w_ref[k:k + 1, cs] * _tap(src_ref, sh_ref, cs, first_offset + step * k, rb * rows, rows)
            dst_ref[rb * rows:(rb + 1) * rows, cs] = acc


def _conv_weight_grad(d_ref, src_ref, sh_ref, first_offset, acc_ref, tm):
    rows = 64
    for g in range(N_GROUPS):
        cs = slice(g * GROUP_W, (g + 1) * GROUP_W)
        _shifted_copies(src_ref, sh_ref, cs, tm)
        for rb in range(tm // rows):
            d = d_ref[rb * rows:(rb + 1) * rows, cs]
            for k in range(CONV_K):
                prod = d * _tap(src_ref, sh_ref, cs, first_offset + k, rb * rows, rows)
                acc_ref[SUBLANES * k:SUBLANES * (k + 1), cs] += jnp.sum(
                    prod.reshape(rows // SUBLANES, SUBLANES, GROUP_W), axis=0)


def _spatial_mix(w_ref, v_bf, tm):
    rows = []
    for q in range(tm // CHUNK):
        cols = [_dot(w_ref[h], v_bf[q * CHUNK:(q + 1) * CHUNK, h * GROUP_W:(h + 1) * GROUP_W])
                for h in range(N_GROUPS)]
        rows.append(jnp.concatenate(cols, axis=1))
    return jnp.concatenate(rows, axis=0)


def _group_norm_fwd(h1, gn_g, gn_b):
    xhat, rstd = [], []
    for g in range(N_GROUPS):
        xh, rs = _norm_stats(h1[:, g * GROUP_W:(g + 1) * GROUP_W])
        xhat.append(xh)
        rstd.append(rs)
    xhat = jnp.concatenate(xhat, axis=1)
    return xhat * gn_g + gn_b, xhat, rstd


def _forward_tiles(p, x, tgt, wpa, wpb, wo, convw, vecs, ws, bsp, tiles_per_seq):
    t = x.shape[0]
    tm = TOKEN_TILE
    hb = tm // HALO

    def body(p_ref, ph_ref, x_ref, t_ref, wpa_ref, wpb_ref, wo_ref, cw_ref, vec_ref, ws_ref, bsp_ref,
             h1_ref, ya_ref, yb_ref, h3_ref, s_ref, mx_ref, dr_ref, drb_ref, xt_ref, acc_ref, he_ref, sh_ref):
        i = pl.program_id(0)
        xt_ref[...] = x_ref[...].T.astype(BF16)
        conv_b, gn_g, gn_b, lnv_g, lnv_b, b_o, lno_g, lno_b = [vec_ref[j:j + 1, :] for j in range(8)]

        keep = jnp.where(i % tiles_per_seq == 0, 0.0, 1.0)
        he_ref[0:HALO, :] = ph_ref[:, 0:D] * _sigmoid(ph_ref[:, D:2 * D]) * keep
        he_ref[HALO:, :] = p_ref[:, 0:D] * _sigmoid(p_ref[:, D:2 * D])
        _conv_taps(he_ref, sh_ref, cw_ref, HALO - (CONV_K - 1), 1, conv_b, h1_ref, tm)
        h2, _, _ = _group_norm_fwd(h1_ref[...], gn_g, gn_b)
        a_gate = p_ref[:, 2 * D:3 * D]
        h3 = ((h2 * _sigmoid(h2)) * (a_gate * _sigmoid(a_gate))).astype(BF16)
        h3_ref[...] = h3
        ya = _dot(h3, wpa_ref[...])
        ya_ref[...] = ya

        u = _gelu(p_ref[:, 3 * D:4 * D])
        vhat, _ = _norm_stats(_gelu(p_ref[:, 4 * D:5 * D]))
        v1 = (vhat * lnv_g + lnv_b).astype(BF16)
        b_gate = p_ref[:, 5 * D:6 * D]
        vmix = _spatial_mix(ws_ref, v1, tm) + jnp.concatenate([bsp_ref[...]] * (tm // CHUNK), axis=0)
        s = (u * vmix * (b_gate * _sigmoid(b_gate))).astype(BF16)
        s_ref[...] = s
        yb = _dot(s, wpb_ref[...])
        yb_ref[...] = yb

        mixed = (_sigmoid(p_ref[:, 6 * D:7 * D]) * ya + _sigmoid(p_ref[:, 7 * D:8 * D]) * yb).astype(BF16)
        mx_ref[...] = mixed
        r = ALPHA * x_ref[...] + (_dot(mixed, wo_ref[...]) + b_o)
        xhat, rstd = _norm_stats(r)
        err = (xhat * lno_g + lno_b) - t_ref[...]
        dout = err * (1.0 / D)
        dr = _norm_bwd(dout * lno_g, xhat, rstd)
        dr_ref[...] = dr
        drb_ref[...] = dr.astype(BF16)

        @pl.when(i == 0)
        def _():
            acc_ref[...] = jnp.zeros_like(acc_ref)

        acc_ref[0:1, :] += _colsum(dout * xhat)
        acc_ref[1:2, :] += _colsum(dout)
        acc_ref[2:3, :] += _colsum(dr)
        acc_ref[3:4, :] += _colsum(err * err) * (0.5 / D)

    tile = lambda w: pl.BlockSpec((tm, w), lambda i: (i, 0))
    f32_out = jax.ShapeDtypeStruct((t, D), F32)
    bf_out = jax.ShapeDtypeStruct((t, D), BF16)
    return pl.pallas_call(
        body, name="forward_tiles", grid=(t // tm,),
        in_specs=[tile(D_IN),
                  pl.BlockSpec((HALO, 2 * D), lambda i: (jnp.maximum(i * hb - 1, 0), 0)),
                  tile(D), tile(D), _resident((D, D)), _resident((D, D)), _resident((D, D)), _full((HALO, D)), _full((8, D)),
                  _full((N_GROUPS, CHUNK, CHUNK)), _full((CHUNK, D))],
        out_specs=[tile(D)] * 8 + [pl.BlockSpec((D, tm), lambda i: (0, i)), _full((8, D))],
        out_shape=[f32_out, f32_out, f32_out, bf_out, bf_out, bf_out, f32_out, bf_out,
                   jax.ShapeDtypeStruct((D, t), BF16), jax.ShapeDtypeStruct((8, D), F32)],
        scratch_shapes=[pltpu.VMEM((tm + HALO, D), F32), pltpu.VMEM((SUBLANES - 1, tm + SHIFT_ROWS, GROUP_W), F32)],
        compiler_params=_params("arbitrary"),
    )(p, p, x, tgt, wpa, wpb, wo, convw, vecs, ws, bsp)


def _backward_tiles(p, h1, ya, yb, drb, wpa, wpb, wo, vecs, ws, wst, bsp):
    t = h1.shape[0]
    tm = TOKEN_TILE

    def body(p_ref, h1_ref, ya_ref, yb_ref, drb_ref, wpa_ref, wpb_ref, wo_ref, vec_ref, ws_ref, wst_ref, bsp_ref,
             dh1_ref, dp_ref, dya_ref, dyb_ref, acc_ref, dbin_ref, dws_ref, dbsp_ref):
        i = pl.program_id(0)
        _, gn_g, gn_b, lnv_g, lnv_b = [vec_ref[j:j + 1, :] for j in range(5)]

        @pl.when(i == 0)
        def _():
            acc_ref[...] = jnp.zeros_like(acc_ref)
            dbin_ref[...] = jnp.zeros_like(dbin_ref)
            dws_ref[...] = jnp.zeros_like(dws_ref)
            dbsp_ref[...] = jnp.zeros_like(dbsp_ref)

        def emit(block, val):
            dbin_ref[0:1, block * D:(block + 1) * D] += _colsum(val)
            dp_ref[:, block * D:(block + 1) * D] = val.astype(BF16)

        dp_ref[:, 0:2 * D] = jnp.zeros((tm, 2 * D), BF16)
        dmixed = _dot_nt(drb_ref[...], wo_ref[...])
        ga = _sigmoid(p_ref[:, 6 * D:7 * D])
        gb = _sigmoid(p_ref[:, 7 * D:8 * D])
        dya = (dmixed * ga).astype(BF16)
        dyb = (dmixed * gb).astype(BF16)
        dya_ref[...] = dya
        dyb_ref[...] = dyb
        emit(6, dmixed * ya_ref[...] * (ga * (1.0 - ga)))
        emit(7, dmixed * yb_ref[...] * (gb * (1.0 - gb)))

        dh3 = _dot_nt(dya, wpa_ref[...])
        h2, xhat, rstd = _group_norm_fwd(h1_ref[...], gn_g, gn_b)
        sg = _sigmoid(h2)
        a_gate = p_ref[:, 2 * D:3 * D]
        sa = _sigmoid(a_gate)
        dh2 = dh3 * (a_gate * sa) * (sg * (1.0 + h2 * (1.0 - sg)))
        emit(2, dh3 * (h2 * sg) * (sa * (1.0 + a_gate * (1.0 - sa))))
        acc_ref[0:1, :] += _colsum(dh2 * xhat)
        acc_ref[1:2, :] += _colsum(dh2)
        dxhat = dh2 * gn_g
        for g in range(N_GROUPS):
            cs = slice(g * GROUP_W, (g + 1) * GROUP_W)
            dh1_ref[:, cs] = _norm_bwd(dxhat[:, cs], xhat[:, cs], rstd[g])

        ds = _dot_nt(dyb, wpb_ref[...])
        u_pre = p_ref[:, 3 * D:4 * D]
        u, du_dpre = _gelu_and_grad(u_pre)
        v0, dv_dpre = _gelu_and_grad(p_ref[:, 4 * D:5 * D])
        vhat, vrstd = _norm_stats(v0)
        v1 = (vhat * lnv_g + lnv_b).astype(BF16)
        vmix = _spatial_mix(ws_ref, v1, tm) + jnp.concatenate([bsp_ref[...]] * (tm // CHUNK), axis=0)
        b_gate = p_ref[:, 5 * D:6 * D]
        sb = _sigmoid(b_gate)
        silu_b = b_gate * sb
        emit(3, ds * vmix * silu_b * du_dpre)
        emit(5, ds * u * vmix * (sb * (1.0 + b_gate * (1.0 - sb))))
        dvmix = ds * u * silu_b
        dvmix_bf = dvmix.astype(BF16)
        for q in range(tm // CHUNK):
            dbsp_ref[...] += dvmix[q * CHUNK:(q + 1) * CHUNK, :]
            for h in range(N_GROUPS):
                blk = (slice(q * CHUNK, (q + 1) * CHUNK), slice(h * GROUP_W, (h + 1) * GROUP_W))
                dws_ref[:, h * GROUP_W:(h + 1) * GROUP_W] += _dot_nt(dvmix_bf[blk], v1[blk])
        dv1 = _spatial_mix(wst_ref, dvmix_bf, tm)
        acc_ref[2:3, :] += _colsum(dv1 * vhat)
        acc_ref[3:4, :] += _colsum(dv1)
        emit(4, _norm_bwd(dv1 * lnv_g, vhat, vrstd) * dv_dpre)

    tile = lambda w: pl.BlockSpec((tm, w), lambda i: (i, 0))
    return pl.pallas_call(
        body, name="backward_tiles", grid=(t // tm,),
        in_specs=[tile(D_IN), tile(D), tile(D), tile(D), tile(D), _resident((D, D)), _resident((D, D)), _resident((D, D)),
                  _full((8, D)), _full((N_GROUPS, CHUNK, CHUNK)), _full((N_GROUPS, CHUNK, CHUNK)), _full((CHUNK, D))],
        out_specs=[tile(D), tile(D_IN), tile(D), tile(D), _full((8, D)), _full((8, D_IN)),
                   _full((CHUNK, D)), _full((CHUNK, D))],
        out_shape=[jax.ShapeDtypeStruct((t, D), F32), jax.ShapeDtypeStruct((t, D_IN), BF16),
                   jax.ShapeDtypeStruct((t, D), BF16), jax.ShapeDtypeStruct((t, D), BF16),
                   jax.ShapeDtypeStruct((8, D), F32), jax.ShapeDtypeStruct((8, D_IN), F32),
                   jax.ShapeDtypeStruct((CHUNK, D), F32), jax.ShapeDtypeStruct((CHUNK, D), F32)],
        compiler_params=_params("arbitrary"),
    )(p, h1, ya, yb, drb, wpa, wpb, wo, vecs, ws, wst, bsp)


def _conv_backward(dh1, p, dp, convw, pairs, tiles_per_seq):
    t = dh1.shape[0]
    tm = TOKEN_TILE
    hb = tm // HALO
    last = t // HALO - 1
    n_sq = len(pairs)
    span = 2
    rows = D // 8

    def body(dh1_ref, dnext_ref, p_ref, ph_ref, cw_ref, dp_in_ref, *refs):
        del dp_in_ref
        sq_in = refs[:2 * n_sq]
        dp_ref, dcw_ref, dbin_ref = refs[2 * n_sq:2 * n_sq + 3]
        sq_out = refs[2 * n_sq + 3:3 * n_sq + 3]
        sq_wire = refs[3 * n_sq + 3:4 * n_sq + 3]
        de_ref, he_ref, dh0_ref, sh_ref, acc_ref, wire_ref, sq_sem, wire_sem = refs[4 * n_sq + 3:]
        i = pl.program_id(0)

        @pl.when(i == 0)
        def _():
            dcw_ref[...] = jnp.zeros_like(dcw_ref)
            dbin_ref[...] = jnp.zeros_like(dbin_ref)
            acc_ref[...] = jnp.zeros_like(acc_ref)

        @pl.when(i % span == span - 1)
        def _():
            for a in range(n_sq):
                acc_ref[a] += _dot_tn(sq_in[2 * a][...], sq_in[2 * a + 1][...])

        keep_next = jnp.where(i % tiles_per_seq == tiles_per_seq - 1, 0.0, 1.0)
        de_ref[0:tm, :] = dh1_ref[...]
        de_ref[tm:, :] = dnext_ref[...] * keep_next
        _conv_taps(de_ref, sh_ref, cw_ref, CONV_K - 1, -1, None, dh0_ref, tm)

        keep_prev = jnp.where(i % tiles_per_seq == 0, 0.0, 1.0)
        sg = _sigmoid(p_ref[:, D:2 * D])
        val = p_ref[:, 0:D]
        he_ref[0:HALO, :] = ph_ref[:, 0:D] * _sigmoid(ph_ref[:, D:2 * D]) * keep_prev
        he_ref[HALO:, :] = val * sg
        _conv_weight_grad(dh1_ref, he_ref, sh_ref, HALO - (CONV_K - 1), dcw_ref, tm)
        dcw_ref[SUBLANES * CONV_K:, :] += jnp.sum(dh1_ref[...].reshape(tm // SUBLANES, SUBLANES, D), axis=0)

        dh0 = dh0_ref[...]
        dval = dh0 * sg
        dglu = dh0 * val * (sg * (1.0 - sg))
        dbin_ref[0:1, 0:D] += _colsum(dval)
        dbin_ref[0:1, D:2 * D] += _colsum(dglu)
        dp_ref[:, 0:D] = dval.astype(BF16)
        dp_ref[:, D:2 * D] = dglu.astype(BF16)

        @pl.when(i == t // tm - 1)
        def _():
            cps = [pltpu.make_async_copy(acc_ref.at[a, pl.ds((2 * j + h) * rows, rows)], sq_out[a].at[h, j],
                                         sq_sem.at[(a * N_CHIPS + j) * 2 + h])
                   for a in range(n_sq) for j in range(N_CHIPS) for h in range(2)]
            for cp in cps:
                cp.start()
            for a in range(n_sq):
                wire_ref[...] = acc_ref[a].astype(BF16)
                narrow = [pltpu.make_async_copy(wire_ref.at[pl.ds((2 * j + h) * rows, rows)], sq_wire[a].at[h, j],
                                                wire_sem.at[2 * j + h]) for j in range(N_CHIPS) for h in range(2)]
                for cp in narrow:
                    cp.start()
                for cp in narrow:
                    cp.wait()
            for cp in cps:
                cp.wait()

    any_spec = pl.BlockSpec(memory_space=pl.ANY)
    wide = pl.BlockSpec((span * tm, D), lambda i: (i // span, 0))
    return pl.pallas_call(
        body, name="conv_backward", grid=(t // tm,),
        in_specs=[pl.BlockSpec((tm, D), lambda i: (i, 0)),
                  pl.BlockSpec((HALO, D), lambda i: (jnp.minimum((i + 1) * hb, last), 0)),
                  pl.BlockSpec((tm, 2 * D), lambda i: (i, 0)),
                  pl.BlockSpec((HALO, 2 * D), lambda i: (jnp.maximum(i * hb - 1, 0), 0)),
                  _full((HALO, D)), any_spec] + [wide] * (2 * n_sq),
        out_specs=[pl.BlockSpec((tm, 2 * D), lambda i: (i, 0)), _full((SUBLANES * HALO, D)), _full((8, 2 * D))]
        + [any_spec] * (2 * n_sq),
        out_shape=[jax.ShapeDtypeStruct(dp.shape, BF16), jax.ShapeDtypeStruct((SUBLANES * HALO, D), F32),
                   jax.ShapeDtypeStruct((8, 2 * D), F32)]
        + [jax.ShapeDtypeStruct((2, N_CHIPS, rows, D), F32)] * n_sq
        + [jax.ShapeDtypeStruct((2, N_CHIPS, rows, D), BF16)] * n_sq,
        scratch_shapes=[pltpu.VMEM((tm + HALO, D), F32), pltpu.VMEM((tm + HALO, D), F32), pltpu.VMEM((tm, D), F32),
                        pltpu.VMEM((SUBLANES - 1, tm + SHIFT_ROWS, GROUP_W), F32), pltpu.VMEM((n_sq, D, D), F32),
                        pltpu.VMEM((D, D), BF16), pltpu.SemaphoreType.DMA((n_sq * N_CHIPS * 2,)),
                        pltpu.SemaphoreType.DMA((N_CHIPS * 2,))],
        input_output_aliases={5: 0},
        compiler_params=_params("arbitrary"),
    )(dh1, dh1, p, p, convw, dp, *[a for pair in pairs for a in pair])


def _grad_in_and_x(xt, dp, w4, dr, wires, grads):
    t = dr.shape[0]
    tm = TOKEN_TILE
    half, tn = D // 2, 512
    nb = W_BLOCK // tn
    n_w, n_x = 2 * N_CHIPS * nb, t // tm
    ns = len(grads)
    xi, yi, ci = lax.axis_index("x"), lax.axis_index("y"), lax.axis_index("c")
    others = [2 * (1 - xi) + yi, 2 * xi + (1 - yi), 2 * (1 - xi) + (1 - yi)]
    blocks = others + others + [2 * xi + yi] * 2
    halves = [1 - ci] * 3 + [ci] * 3 + [1 - ci, ci]
    table = jnp.stack([jnp.stack([b * nb + n for b in blocks for n in range(nb)]),
                       jnp.stack([h for h in halves for _ in range(nb)])]).astype(jnp.int32)

    def body(tab_ref, xt_ref, dpc_ref, dpr_ref, w_ref, dr_ref, *refs):
        parts, fulls = refs[:ns], refs[ns:2 * ns]
        dx_ref, qk_ref, b2_ref, b1_ref, wire_ref = refs[2 * ns:2 * ns + 5]
        lands, sibs = refs[2 * ns + 5:3 * ns + 5], refs[3 * ns + 5:4 * ns + 5]
        (g_ref, st_ref, sb_ref, tmp_ref, d2d_send, d2d_recv, ici_send, ici_recv, own_sem, tmp_sem, wire_sem,
         p_send, p_recv, s_send, s_recv) = refs[4 * ns + 5:]
        s = pl.program_id(0)
        x_, y_, c, k = _position()
        chips = _other_chips(x_, y_)
        n = s % nb
        grp = s // nb
        cols = pl.ds(pl.multiple_of(n * tn, tn), tn)

        def part(a, r, core):
            cx, cy = chips[r]
            return pltpu.make_async_remote_copy(
                src_ref=parts[a].at[core, 2 * cx + cy], dst_ref=lands[a].at[2 * r + c],
                send_sem=p_send.at[6 * a + 2 * r + core], recv_sem=p_recv.at[6 * a + 2 * r + c],
                device_id=(cx, cy, core), device_id_type=MESH)

        def landed(a, r, core):
            cx, cy = chips[r]
            return pltpu.make_async_remote_copy(
                src_ref=lands[a].at[2 * r + core], dst_ref=lands[a].at[2 * r + core],
                send_sem=p_send.at[6 * a + 2 * r + core], recv_sem=p_recv.at[6 * a + 2 * r + core],
                device_id=(cx, cy, core), device_id_type=MESH)

        def to_sibling_whole(a):
            return pltpu.make_async_remote_copy(
                src_ref=fulls[a].at[1 - c, k], dst_ref=sibs[a], send_sem=s_send.at[a], recv_sem=s_recv.at[a],
                device_id=(x_, y_, 1 - c), device_id_type=MESH)

        def to_sibling(slot, land):
            return pltpu.make_async_remote_copy(
                src_ref=st_ref.at[slot], dst_ref=b1_ref.at[land, :, cols], send_sem=d2d_send.at[slot],
                recv_sem=d2d_recv.at[land * nb + n], device_id=(x_, y_, 1 - c), device_id_type=MESH)

        def to_chip(r):
            cx, cy = chips[r]
            return pltpu.make_async_remote_copy(
                src_ref=wire_ref.at[r, :, cols], dst_ref=b2_ref.at[r, :, cols], send_sem=ici_send.at[r],
                recv_sem=ici_recv.at[r], device_id=(cx, cy, c), device_id_type=MESH)

        def all_of_chip(r):
            cx, cy = chips[r]
            return pltpu.make_async_remote_copy(
                src_ref=wire_ref.at[r], dst_ref=b2_ref.at[r], send_sem=ici_send.at[r],
                recv_sem=ici_recv.at[r], device_id=(cx, cy, c), device_id_type=MESH)

        def to_result(slot):
            return pltpu.make_async_copy(st_ref.at[slot], qk_ref.at[:, cols], own_sem.at[slot])

        def sibling_piece(land):
            return pltpu.make_async_copy(b1_ref.at[land, :, cols], tmp_ref, tmp_sem)

        @pl.when(s == 0)
        def _():
            for a in range(ns):
                to_sibling_whole(a).start()
                for r in range(3):
                    for core in range(2):
                        part(a, r, core).start()

        own_half = ((grp >= 3) & (grp <= 5)) | (grp == 7)
        land = jnp.where(grp == 7, 3, grp - 3)

        @pl.when(own_half)
        def _():
            to_sibling(0, land).wait_recv()
            sibling_piece(land).start()

        @pl.when(s < n_w)
        def _():
            g_ref[...] = _dot(xt_ref[tab_ref[1, s]], dpc_ref[...])

        @pl.when(own_half)
        def _():
            sibling_piece(land).wait()

        for g in range(2 * N_CHIPS):
            @pl.when(grp == g)
            def _(g=g):
                if g in (0, 1, 2, 6):
                    use = s if g < 3 else 3 * nb + n
                    slot = use % 2

                    @pl.when(use >= 2)
                    def _():
                        to_sibling(slot, 0).wait_send()

                    st_ref[slot] = g_ref[...]
                    to_sibling(slot, min(g, 3)).start()
                elif g in (3, 4, 5):
                    sb_ref[...] = (g_ref[...] + tmp_ref[...]).astype(BF16)
                    stage = pltpu.make_async_copy(sb_ref, wire_ref.at[g - 3, :, cols], wire_sem)
                    stage.start()
                    stage.wait()
                    to_chip(g - 3).start()
                else:
                    slot = n % 2
                    piece = g_ref[...] + tmp_ref[...]

                    @pl.when(n < 2)
                    def _():
                        to_sibling(slot, 0).wait_send()

                    @pl.when(n >= 2)
                    def _():
                        to_result(slot).wait()

                    st_ref[slot] = piece
                    to_result(slot).start()

        @pl.when(s >= n_w)
        def _():
            acc = ALPHA * dr_ref[...]
            for j in range(N_CHIPS):
                acc = acc + _dot_nt(dpr_ref[:, j * W_BLOCK:(j + 1) * W_BLOCK], w_ref[j])
            dx_ref[...] = acc

        @pl.when(s == n_w + n_x - 1)
        def _():
            for slot in range(2):
                to_result(slot).wait()
            for r in range(3):
                all_of_chip(r).wait_recv()
                all_of_chip(r).wait_send()
            for a in range(ns):
                to_sibling_whole(a).wait_recv()
                to_sibling_whole(a).wait_send()
                for r in range(3):
                    for core in range(2):
                        landed(a, r, core).wait_recv()
                        part(a, r, core).wait_send()

    any_spec = pl.BlockSpec(memory_space=pl.ANY)
    tile = lambda s, tab: (jnp.maximum(s - n_w, 0), 0)
    return pl.pallas_call(
        body, name="grad_in_and_x",
        grid_spec=pltpu.PrefetchScalarGridSpec(
            num_scalar_prefetch=1, grid=(n_w + n_x,),
            in_specs=[pl.BlockSpec((2, half, t), lambda s, tab: (0, 0, 0), pipeline_mode=pl.Buffered(1)),
                      pl.BlockSpec((t, tn), lambda s, tab: (0, tab[0, jnp.minimum(s, n_w - 1)])),
                      pl.BlockSpec((tm, D_IN), tile),
                      pl.BlockSpec((N_CHIPS, D, W_BLOCK), lambda s, tab: (0, 0, 0), pipeline_mode=pl.Buffered(1)),
                      pl.BlockSpec((tm, D), tile)] + [any_spec] * (2 * ns),
            out_specs=[pl.BlockSpec((tm, D), tile)] + [any_spec] * (4 + 2 * ns),
            scratch_shapes=[pltpu.VMEM((half, tn), F32), pltpu.VMEM((2, half, tn), F32), pltpu.VMEM((half, tn), BF16),
                            pltpu.VMEM((half, tn), F32),
                            pltpu.SemaphoreType.DMA((2,)), pltpu.SemaphoreType.DMA((N_CHIPS * nb,)),
                            pltpu.SemaphoreType.DMA((3,)), pltpu.SemaphoreType.DMA((3,)),
                            pltpu.SemaphoreType.DMA((2,)), pltpu.SemaphoreType.DMA, pltpu.SemaphoreType.DMA,
                            pltpu.SemaphoreType.DMA((6 * ns,)), pltpu.SemaphoreType.DMA((6 * ns,)),
                            pltpu.SemaphoreType.DMA((ns,)), pltpu.SemaphoreType.DMA((ns,))]),
        out_shape=[jax.ShapeDtypeStruct((t, D), F32), jax.ShapeDtypeStruct((half, W_BLOCK), F32),
                   jax.ShapeDtypeStruct((3, half, W_BLOCK), BF16), jax.ShapeDtypeStruct((N_CHIPS, half, W_BLOCK), F32),
                   jax.ShapeDtypeStruct((3, half, W_BLOCK), BF16)]
        + [jax.ShapeDtypeStruct((6,) + w.shape[2:], w.dtype) for w in wires]
        + [jax.ShapeDtypeStruct(g.shape[2:], F32) for g in grads],
        compiler_params=_params("arbitrary"),
    )(table, xt, dp, dp, w4, dr, *wires, *grads)


def kernel(x, w_in, b_in, conv_w, conv_b, gn_g, gn_b, ln_v_g, ln_v_b, w_spatial, b_spatial, w_pa, w_pb, w_o, b_o, ln_out_g, ln_out_b, loss_target, m_w_in, m_b_in, m_conv_w, m_conv_b, m_gn_g, m_gn_b, m_ln_v_g, m_ln_v_b, m_w_spatial, m_b_spatial, m_w_pa, m_w_pb, m_w_o, m_b_o, m_ln_out_g, m_ln_out_b, v_w_in, v_b_in, v_conv_w, v_conv_b, v_gn_g, v_gn_b, v_ln_v_g, v_ln_v_b, v_w_spatial, v_b_spatial, v_w_pa, v_w_pb, v_w_o, v_b_o, v_ln_out_g, v_ln_out_b):
    n_seq, seq, _ = x.shape
    t = n_seq * seq
    tiles_per_seq = seq // TOKEN_TILE
    x2 = x.reshape(t, D)
    tgt = loss_target.reshape(t, D)

    conv_shard = jnp.pad(conv_w, ((0, HALO - CONV_K), (0, 0)))
    p, win4, wpa4, wpb4, wo4, conv4 = _proj_gather(
        x2, b_in.reshape(N_CHIPS, 1, W_BLOCK),
        [_place_shard(w_in, 256, BF16), _place_shard(w_pa, 128, BF16), _place_shard(w_pb, 128, BF16),
         _place_shard(w_o, 128, BF16), _place_shard(conv_shard, HALO // 2, F32)])
    win4 = win4.reshape(N_CHIPS, D, W_BLOCK)
    wpa, wpb, wo = wpa4.reshape(D, D), wpb4.reshape(D, D), wo4.reshape(D, D)
    convw = conv4.reshape(N_CHIPS, HALO, D // N_CHIPS).transpose(1, 0, 2).reshape(HALO, D)

    vecs = jnp.stack([conv_b, gn_g, gn_b, ln_v_g, ln_v_b, b_o, ln_out_g, ln_out_b])
    causal = jnp.tril(jnp.ones((CHUNK, CHUNK), bool))
    ws = jnp.where(causal[None], w_spatial, 0.0)
    ws_bf, wst_bf = ws.astype(BF16), ws.transpose(0, 2, 1).astype(BF16)
    bsp = jnp.repeat(b_spatial.T, GROUP_W, axis=1)

    h1, ya, yb, h3, s, mixed, dr, drb, xt, acc_f = _forward_tiles(p, x2, tgt, wpa, wpb, wo, convw, vecs, ws_bf, bsp, tiles_per_seq)
    dh1, dp, dya, dyb, acc_b, dbin_b, dws, dbsp_acc = _backward_tiles(p, h1, ya, yb, drb, wpa, wpb, wo, vecs, ws_bf, wst_bf, bsp)
    dp, dcw8, dbin_a, *square = _conv_backward(dh1, p, dp, convw, [(h3, dya), (s, dyb), (mixed, drb)], tiles_per_seq)

    small, g_conv = _pack_small(acc_f, acc_b, dbin_a, dbin_b, dcw8, dws, dbsp_acc)
    small = small.reshape(2, N_CHIPS, SMALL_ROWS // 8, D)

    grads = square[:3] + [g_conv, small]
    wires = square[3:] + [g_conv, small]
    grad_x, q_in, chips_in, _, _, *landed = _grad_in_and_x(xt.reshape(2, D // 2, t), dp, win4, dr, wires, grads)
    grad_x = grad_x.reshape(x.shape)
    mine = [_add_chips(q_in, chips_in)]
    mine += [_add_devices(g, land, sib, a == 4) for a, (g, land, sib) in enumerate(zip(grads, landed[:5], landed[5:]))]
    *full, small_parts = _share_results(mine[:5], mine[5])
    grad_w_in, grad_w_pa, grad_w_pb, grad_w_o = [f.reshape(w.shape) for f, w in zip(full[:4], (w_in, w_pa, w_pb, w_o))]
    grad_conv_w = full[4].reshape(HALO, D // N_CHIPS)[:CONV_K]

    big = {}
    for name, w, g, m, v in [("w_in", w_in, grad_w_in, m_w_in, v_w_in), ("w_pa", w_pa, grad_w_pa, m_w_pa, v_w_pa),
                             ("w_pb", w_pb, grad_w_pb, m_w_pb, v_w_pb), ("w_o", w_o, grad_w_o, m_w_o, v_w_o),
                             ("conv_w", conv_w, grad_conv_w, m_conv_w, v_conv_w)]:
        big[name] = (g,) + tuple(_adamw(w, g, m, v))
    vec_names = ["conv_b", "gn_g", "gn_b", "ln_v_g", "ln_v_b", "b_o", "ln_out_g", "ln_out_b"]
    vec_triples = [(conv_b, m_conv_b, v_conv_b), (gn_g, m_gn_g, v_gn_g), (gn_b, m_gn_b, v_gn_b),
                   (ln_v_g, m_ln_v_g, v_ln_v_g), (ln_v_b, m_ln_v_b, v_ln_v_b), (b_o, m_b_o, v_b_o),
                   (ln_out_g, m_ln_out_g, v_ln_out_g), (ln_out_b, m_ln_out_b, v_ln_out_b)]
    small_res, loss8 = _adamw_small(
        small_parts.reshape(8, SMALL_ROWS // 8, D), vec_triples, (b_in, m_b_in, v_b_in),
        (w_spatial, m_w_spatial, v_w_spatial), (b_spatial, m_b_spatial, v_b_spatial))
    per_name = dict(zip(vec_names + ["b_in", "w_spatial", "b_spatial"], small_res))

    order = ["w_in", "b_in", "conv_w", "conv_b", "gn_g", "gn_b", "ln_v_g", "ln_v_b", "w_spatial", "b_spatial",
             "w_pa", "w_pb", "w_o", "b_o", "ln_out_g", "ln_out_b"]
    outs = [loss8[0, 0], grad_x]
    for kind in range(4):
        outs += [big[n][kind] if n in big else per_name[n][kind] for n in order]
    return tuple(outs)
```

```python
import functools
import math

import jax
import jax.numpy as jnp
from jax import lax
from jax.experimental import pallas as pl
from jax.experimental.pallas import tpu as pltpu

D = 1024
N_GROUPS = 8
GROUP_W = D // N_GROUPS
CHUNK = 128
CONV_K = 31
HALO = 32
D_IN = 8 * D
N_CHIPS = 4
W_BLOCK = D_IN // N_CHIPS
ALPHA = 2.0 ** 0.25
LN_EPS = 1e-5
ADAM_LR, ADAM_B1, ADAM_B2, ADAM_EPS, ADAM_WD, ADAM_STEP = 0.001, 0.9, 0.999, 1e-08, 0.01, 10

TOKEN_TILE = 256
VMEM_LIMIT = 56 * 1024 * 1024
MESH = pl.DeviceIdType.MESH
F32, BF16 = jnp.float32, jnp.bfloat16


def _sigmoid(x):
    return 1.0 / (1.0 + jnp.exp(-x))


def _gelu(x):
    c = math.sqrt(2.0 / math.pi)
    t = jnp.tanh(c * (x + 0.044715 * (x * x * x)))
    return x * (0.5 * (1.0 + t))


def _gelu_and_grad(x):
    c = math.sqrt(2.0 / math.pi)
    x2 = x * x
    t = jnp.tanh(c * (x + 0.044715 * (x2 * x)))
    cdf = 0.5 * (1.0 + t)
    return x * cdf, cdf + 0.5 * x * (1.0 - t * t) * (c * (1.0 + 3.0 * 0.044715 * x2))


def _norm_stats(v):
    mu = jnp.mean(v, axis=-1, keepdims=True)
    vc = v - mu
    var = jnp.mean(vc * vc, axis=-1, keepdims=True)
    rstd = lax.rsqrt(var + LN_EPS)
    return vc * rstd, rstd


def _norm_bwd(dxhat, xhat, rstd):
    m1 = jnp.mean(dxhat, axis=-1, keepdims=True)
    m2 = jnp.mean(dxhat * xhat, axis=-1, keepdims=True)
    return rstd * (dxhat - m1 - xhat * m2)


def _dot(a, b):
    return jnp.dot(a, b, preferred_element_type=F32)


def _dot_nt(a, b):
    return lax.dot_general(a, b, (((1,), (1,)), ((), ())), preferred_element_type=F32)


def _dot_tn(a, b):
    return lax.dot_general(a, b, (((0,), (0,)), ((), ())), preferred_element_type=F32)


def _colsum(v):
    return jnp.sum(v, axis=0, keepdims=True)


def _full(shape):
    return pl.BlockSpec(shape, lambda *_: (0,) * len(shape))


def _resident(shape):
    return pl.BlockSpec(shape, lambda *_: (0,) * len(shape), pipeline_mode=pl.Buffered(1))


def _params(*sem):
    return pltpu.CompilerParams(dimension_semantics=sem, vmem_limit_bytes=VMEM_LIMIT)


def _chip_index():
    return (2 * lax.axis_index("x") + lax.axis_index("y")).astype(jnp.int32).reshape(1)


def _core_index():
    return lax.axis_index("c").astype(jnp.int32).reshape(1)


def _place_shard(w, rows, dtype):
    r, c = w.shape
    steps = r // 2 // rows

    def body(k_ref, w_ref, o_ref):
        o_ref[...] = w_ref[...].astype(dtype)

    return pl.pallas_call(
        body, name="place_shard",
        grid_spec=pltpu.PrefetchScalarGridSpec(
            num_scalar_prefetch=1, grid=(2, steps),
            in_specs=[pl.BlockSpec((rows, c), lambda h, i, k: (h * steps + i, 0))],
            out_specs=pl.BlockSpec((None, None, rows, c), lambda h, i, k: (k[0], h, i, 0))),
        out_shape=jax.ShapeDtypeStruct((N_CHIPS, 2, r // 2, c), dtype),
        compiler_params=_params("parallel", "parallel"),
    )(_chip_index(), w)


def _position():
    x, y, c = lax.axis_index("x"), lax.axis_index("y"), lax.axis_index("c")
    return x, y, c, 2 * x + y


def _other_chips(x, y):
    return [(1 - x, y), (x, 1 - y), (1 - x, 1 - y)]


def _any_specs(n):
    return [pl.BlockSpec(memory_space=pl.ANY)] * n


def _proj_gather(x, b_in, bufs):
    t = x.shape[0]
    tm = 1024
    steps = t // tm
    ahead = steps // 2
    half = D // 2
    chunk = W_BLOCK // 2
    n = len(bufs)
    xi, yi = lax.axis_index("x"), lax.axis_index("y")
    chips = [2 * xi + yi, 2 * (1 - xi) + yi, 2 * xi + (1 - yi), 2 * (1 - xi) + (1 - yi)]
    plan = [(0, 0), (0, 1), (1, 0), (2, 1), (1, 1), (2, 0), (3, 0), (3, 1)]
    order = jnp.stack([2 * chips[ch] + q for ch, q in plan]).astype(jnp.int32)

    def body(order_ref, x_ref, b_ref, *refs):
        p_ref, outs = refs[n], refs[n + 1:2 * n + 1]
        xb_ref, w_ref, lsem, send, recv, hop_send, hop_recv, fsend, frecv, qsend, qrecv = refs[2 * n + 1:]
        jj, i = pl.program_id(0), pl.program_id(1)
        x_, y_, c, k = _position()
        nbrs = [(1 - x_, y_), (x_, 1 - y_)]
        blocks = [2 * (1 - x_) + y_, 2 * x_ + (1 - y_), 2 * (1 - x_) + (1 - y_)]

        def quarter(a, block, q, h):
            if a == 0:
                return outs[0].at[block, h, :, pl.ds(q * chunk, chunk)]
            rows = outs[a].shape[2] // 2
            return outs[a].at[block, h, pl.ds(q * rows, rows)]

        def copy(ref, to, send_sem, recv_sem):
            return pltpu.make_async_remote_copy(src_ref=ref, dst_ref=ref, send_sem=send_sem, recv_sem=recv_sem,
                                                device_id=(to[0], to[1], c), device_id_type=MESH)

        def sent(a, nb, q):
            return copy(quarter(a, k, q, c), nbrs[nb], send.at[4 * a + 2 * nb + q], recv.at[4 * a + 2 * nb + q])

        def landed(a, nb, q):
            return copy(quarter(a, blocks[nb], q, c), nbrs[nb], send.at[4 * a + 2 * nb + q], recv.at[4 * a + 2 * nb + q])

        def hopped(a, nb):
            return copy(quarter(a, blocks[nb], nb, c), nbrs[1 - nb], hop_send.at[2 * a + nb], hop_recv.at[2 * a + 1 - nb])

        def from_diagonal(a, via):
            return copy(quarter(a, blocks[2], 1 - via, c), nbrs[via], hop_send.at[2 * a + via], hop_recv.at[2 * a + via])

        def passed(a, r, h):
            return pltpu.make_async_remote_copy(
                src_ref=outs[a].at[blocks[r], h], dst_ref=outs[a].at[blocks[r], h], send_sem=fsend.at[3 * a + r],
                recv_sem=frecv.at[3 * a + r], device_id=(x_, y_, 1 - c), device_id_type=MESH)

        def passed_quarter(r, q, h):
            ref = quarter(0, blocks[r], q, h)
            return pltpu.make_async_remote_copy(
                src_ref=ref, dst_ref=ref, send_sem=qsend.at[2 * r + q], recv_sem=qrecv.at[2 * r + q],
                device_id=(x_, y_, 1 - c), device_id_type=MESH)

        def load(block, q, slot):
            return [pltpu.make_async_copy(quarter(0, block, q, h), w_ref.at[slot, pl.ds(h * half, half)],
                                          lsem.at[2 * slot + h]) for h in range(2)]

        def pass_on(arrays):
            for a in arrays:
                for nb in range(2):
                    landed(a, nb, nb).wait_recv()
                    hopped(a, nb).start()

        @pl.when((jj == 0) & (i == 0))
        def _():
            for a in range(n):
                for nb, q in ((0, 0), (1, 1), (0, 1), (1, 0)):
                    sent(a, nb, q).start()
            for cp in load(k, 0, 0):
                cp.start()

        for nxt in range(1, len(plan)):
            @pl.when((jj == nxt - 1) & (i == ahead))
            def _(nxt=nxt):
                ch, q = plan[nxt]
                if ch in (1, 2):
                    landed(0, ch - 1, q).wait_recv()
                    if q == ch - 1:
                        hopped(0, ch - 1).start()
                elif ch == 3:
                    from_diagonal(0, 1 - q).wait_recv()
                if ch:
                    passed_quarter(ch - 1, q, c).start()
                    passed_quarter(ch - 1, q, 1 - c).wait_recv()
                for cp in load(k if ch == 0 else blocks[ch - 1], q, nxt % 2):
                    cp.start()
                if nxt == 5:
                    pass_on(range(1, n))
                    for a in range(1, n):
                        landed(a, 0, 1).wait_recv()
                        passed(a, 0, c).start()
                        landed(a, 1, 0).wait_recv()
                        passed(a, 1, c).start()

        slot = jj % 2

        @pl.when(i == 0)
        def _():
            for cp in load(k, 0, slot):
                cp.wait()

        rows = pl.ds(pl.multiple_of(i * tm, tm), tm)

        @pl.when(jj == 0)
        def _():
            xb_ref[rows, :] = x_ref[...].astype(BF16)

        p_ref[...] = _dot(xb_ref[rows, :], w_ref[slot]) + b_ref[...]

        @pl.when((jj == len(plan) - 1) & (i == steps - 1))
        def _():
            for a in range(1, n):
                from_diagonal(a, 0).wait_recv()
                from_diagonal(a, 1).wait_recv()
                passed(a, 2, c).start()
            for a in range(1, n):
                for r in range(3):
                    passed(a, r, 1 - c).wait_recv()
                    passed(a, r, c).wait_send()
            for r in range(3):
                for q in range(2):
                    passed_quarter(r, q, c).wait_send()
            for a in range(n):
                for nb in range(2):
                    for q in range(2):
                        sent(a, nb, q).wait_send()
                    hopped(a, nb).wait_send()

    any_spec = pl.BlockSpec(memory_space=pl.ANY)
    return pl.pallas_call(
        body, name="proj_gather",
        grid_spec=pltpu.PrefetchScalarGridSpec(
            num_scalar_prefetch=1, grid=(len(plan), steps),
            in_specs=[pl.BlockSpec((tm, D), lambda jj, i, o: (jnp.where(jj == 0, i, steps - 1), 0)),
                      pl.BlockSpec((None, 1, chunk), lambda jj, i, o: (o[jj], 0, 0))] + [any_spec] * n,
            out_specs=[pl.BlockSpec((tm, chunk), lambda jj, i, o: (i, o[jj]))] + [any_spec] * n,
            scratch_shapes=[pltpu.VMEM((t, D), BF16), pltpu.VMEM((2, D, chunk), BF16), pltpu.SemaphoreType.DMA((4,)),
                            pltpu.SemaphoreType.DMA((4 * n,)), pltpu.SemaphoreType.DMA((4 * n,)),
                            pltpu.SemaphoreType.DMA((2 * n,)), pltpu.SemaphoreType.DMA((2 * n,)),
                            pltpu.SemaphoreType.DMA((3 * n,)), pltpu.SemaphoreType.DMA((3 * n,)),
                            pltpu.SemaphoreType.DMA((6,)), pltpu.SemaphoreType.DMA((6,))]),
        out_shape=[jax.ShapeDtypeStruct((t, D_IN), F32)] + [jax.ShapeDtypeStruct(b.shape, b.dtype) for b in bufs],
        input_output_aliases={3 + a: 1 + a for a in range(n)},
        compiler_params=_params("arbitrary", "arbitrary"),
    )(order, x, b_in.reshape(D_IN // chunk, 1, chunk), *bufs)


def _share_results(bufs, small):
    n = len(bufs)

    def body(*refs):
        outs, small_out = refs[n + 1:2 * n + 1], refs[2 * n + 1]
        send, recv, ssend, srecv = refs[2 * n + 2:]
        x, y, c, k = _position()
        cps = []
        for a in range(n):
            cp = pltpu.make_async_remote_copy(
                src_ref=outs[a].at[c], dst_ref=outs[a].at[c], send_sem=send.at[a], recv_sem=recv.at[a],
                device_id=(x, y, 1 - c), device_id_type=MESH)
            cp.start()
            cps.append(cp)
        waits = []
        for p in range(1, 8):
            px, py, pc = x ^ (p >> 2), y ^ ((p >> 1) & 1), c ^ (p & 1)
            cp = pltpu.make_async_remote_copy(
                src_ref=small_out.at[k, c], dst_ref=small_out.at[k, c], send_sem=ssend.at[p - 1],
                recv_sem=srecv.at[p - 1], device_id=(px, py, pc), device_id_type=MESH)
            cp.start()
            cps.append(cp)
            waits.append(pltpu.make_async_remote_copy(
                src_ref=small_out.at[2 * px + py, pc], dst_ref=small_out.at[2 * px + py, pc], send_sem=ssend.at[p - 1],
                recv_sem=srecv.at[p - 1], device_id=(px, py, pc), device_id_type=MESH))
        for a in range(n):
            pltpu.make_async_remote_copy(
                src_ref=outs[a].at[1 - c], dst_ref=outs[a].at[1 - c], send_sem=send.at[a], recv_sem=recv.at[a],
                device_id=(x, y, 1 - c), device_id_type=MESH).wait_recv()
        for w in waits:
            w.wait_recv()
        for cp in cps:
            cp.wait_send()

    return pl.pallas_call(
        body, name="rs_share_results",
        in_specs=_any_specs(n + 1), out_specs=_any_specs(n + 1),
        out_shape=[jax.ShapeDtypeStruct(b.shape, b.dtype) for b in bufs + [small]],
        scratch_shapes=[pltpu.SemaphoreType.DMA((n,)), pltpu.SemaphoreType.DMA((n,)),
                        pltpu.SemaphoreType.DMA((7,)), pltpu.SemaphoreType.DMA((7,))],
        input_output_aliases={a: a for a in range(n + 1)},
    )(*bufs, small)


def _row_tile(r, c):
    t = max(8, min(r, (1 << 18) // c))
    while r % t:
        t //= 2
    return t


def _add_devices(g, lands, sib, per_device):
    r, c = g.shape[-2:]
    t = _row_tile(r, c)

    def body(kc_ref, g_ref, l_ref, s_ref, f_ref):
        f = g_ref[...] + s_ref[...]
        for i in range(l_ref.shape[0]):
            f = f + l_ref[i].astype(F32)
        f_ref[...] = f

    if per_device:
        out_spec = pl.BlockSpec((None, None, t, c), lambda i, kc: (kc[0], kc[1], i, 0))
        out_shape = jax.ShapeDtypeStruct((N_CHIPS, 2, r, c), F32)
    else:
        out_spec = pl.BlockSpec((None, t, c), lambda i, kc: (kc[1], i, 0))
        out_shape = jax.ShapeDtypeStruct((2, r, c), F32)
    return pl.pallas_call(
        body, name="rs_add_devices",
        grid_spec=pltpu.PrefetchScalarGridSpec(
            num_scalar_prefetch=1, grid=(r // t,),
            in_specs=[pl.BlockSpec((None, None, t, c), lambda i, kc: (kc[1], kc[0], i, 0)),
                      pl.BlockSpec((lands.shape[0], t, c), lambda i, kc: (0, i, 0)),
                      pl.BlockSpec((t, c), lambda i, kc: (i, 0))],
            out_specs=out_spec),
        out_shape=out_shape,
        compiler_params=_params("parallel"),
    )(jnp.concatenate([_chip_index(), _core_index()]), g, lands, sib)


def _add_chips(q, b2):
    r, c = q.shape
    t = _row_tile(r, c)

    def body(c_ref, q_ref, b_ref, f_ref):
        f_ref[...] = ((q_ref[...] + b_ref[0].astype(F32)) + b_ref[1].astype(F32)) + b_ref[2].astype(F32)

    return pl.pallas_call(
        body, name="rs_add_chips",
        grid_spec=pltpu.PrefetchScalarGridSpec(
            num_scalar_prefetch=1, grid=(r // t,),
            in_specs=[pl.BlockSpec((t, c), lambda i, cr: (i, 0)), pl.BlockSpec((3, t, c), lambda i, cr: (0, i, 0))],
            out_specs=pl.BlockSpec((None, t, c), lambda i, cr: (cr[0], i, 0))),
        out_shape=jax.ShapeDtypeStruct((2, r, c), F32),
        compiler_params=_params("parallel"),
    )(_core_index(), q, b2)


def _adamw_math(w, g, m, v):
    m = ADAM_B1 * m + (1.0 - ADAM_B1) * g
    v = ADAM_B2 * v + (1.0 - ADAM_B2) * (g * g)
    m_hat = m / (1.0 - ADAM_B1 ** ADAM_STEP)
    v_hat = v / (1.0 - ADAM_B2 ** ADAM_STEP)
    delta = -ADAM_LR * (m_hat / (jnp.sqrt(v_hat) + ADAM_EPS) + ADAM_WD * w)
    return delta, m, v


def _adamw(w, g, m, v):
    r, c = w.shape
    t = _row_tile(r, c) if r % 8 == 0 else r

    def body(w_ref, g_ref, m_ref, v_ref, d_ref, nm_ref, nv_ref):
        d_ref[...], nm_ref[...], nv_ref[...] = _adamw_math(w_ref[...], g_ref[...], m_ref[...], v_ref[...])

    spec = pl.BlockSpec((t, c), lambda i: (i, 0))
    return pl.pallas_call(
        body, name="adamw", grid=(r // t,), in_specs=[spec] * 4, out_specs=[spec] * 3,
        out_shape=[jax.ShapeDtypeStruct((r, c), F32)] * 3, compiler_params=_params("parallel"),
    )(w, g, m, v)


ROW_B_IN = 0
ROW_VECS = 8
ROW_LOSS = 16
ROW_B_SPATIAL = 24
ROW_W_SPATIAL = 32
SMALL_ROWS = 192
N_VECS = 8


def _pack_small(acc_f, acc_b, dbin_a, dbin_b, dcw8, dws, dbsp):
    cols = D // N_CHIPS

    def body(af_ref, ab_ref, da_ref, db_ref, cw_ref, ws_ref, bs_ref, o_ref, gc_ref):
        o_ref[...] = jnp.zeros_like(o_ref)
        for j in range(D_IN // D):
            src = da_ref if j < 2 else db_ref
            o_ref[ROW_B_IN + j:ROW_B_IN + j + 1, :] = src[0:1, j * D:(j + 1) * D]
        dcw = jnp.sum(cw_ref[...].reshape(HALO, SUBLANES, D), axis=1)
        o_ref[ROW_VECS:ROW_VECS + 1, :] = dcw[CONV_K:CONV_K + 1]
        o_ref[ROW_VECS + 1:ROW_VECS + 5, :] = ab_ref[0:4, :]
        o_ref[ROW_VECS + 5:ROW_VECS + 6, :] = af_ref[2:3, :]
        o_ref[ROW_VECS + 6:ROW_VECS + 8, :] = af_ref[0:2, :]
        o_ref[ROW_LOSS:ROW_LOSS + 1, :] = af_ref[3:4, :]
        head = lax.broadcasted_iota(jnp.int32, (N_GROUPS, D), 0)
        lane = lax.broadcasted_iota(jnp.int32, (N_GROUPS, D), 1)
        indicator = jnp.where(lane // GROUP_W == head, 1.0, 0.0)
        o_ref[ROW_B_SPATIAL:ROW_B_SPATIAL + N_GROUPS, 0:CHUNK] = lax.dot_general(
            indicator, bs_ref[...], (((1,), (1,)), ((), ())), precision=lax.Precision.HIGHEST, preferred_element_type=F32)
        t_idx = lax.broadcasted_iota(jnp.int32, (CHUNK, D), 0)
        s_idx = lax.broadcasted_iota(jnp.int32, (CHUNK, D), 1) % CHUNK
        o_ref[ROW_W_SPATIAL:ROW_W_SPATIAL + CHUNK, :] = jnp.where(s_idx <= t_idx, ws_ref[...], 0.0)
        for h in range(2):
            for j in range(N_CHIPS):
                gc_ref[h, j] = dcw[h * (HALO // 2):(h + 1) * (HALO // 2), j * cols:(j + 1) * cols]

    ins = [acc_f, acc_b, dbin_a, dbin_b, dcw8, dws, dbsp]
    return pl.pallas_call(
        body, name="pack_small",
        in_specs=[_full(a.shape) for a in ins],
        out_specs=[_full((SMALL_ROWS, D)), _full((2, N_CHIPS, HALO // 2, cols))],
        out_shape=[jax.ShapeDtypeStruct((SMALL_ROWS, D), F32), jax.ShapeDtypeStruct((2, N_CHIPS, HALO // 2, cols), F32)],
        compiler_params=_params(),
    )(*ins)


def _adamw_small(parts, vecs, b_in, w_spatial, b_spatial):
    triples = list(vecs) + [b_in, w_spatial, b_spatial]
    n_in = 3 * len(triples)

    def body(p_ref, *refs):
        ins = [refs[3 * i:3 * i + 3] for i in range(len(triples))]
        outs = [refs[n_in + 4 * i:n_in + 4 * i + 4] for i in range(len(triples))]
        loss_ref, g_ref = refs[n_in + 4 * len(triples):]
        rows = SMALL_ROWS // 8
        for k in range(N_CHIPS):
            for core in range(2):
                g_ref[(core * N_CHIPS + k) * rows:(core * N_CHIPS + k + 1) * rows, :] = p_ref[2 * k + core]

        def step(g, wmv, out, get, put):
            d, nm, nv = _adamw_math(get(wmv[0]), g, get(wmv[1]), get(wmv[2]))
            for o, val in zip(out, (g, d, nm, nv)):
                put(o, val)

        for i in range(N_VECS):
            step(g_ref[ROW_VECS + i:ROW_VECS + i + 1, :], ins[i], outs[i],
                 lambda r: r[...].reshape(1, D), lambda o, val: o.__setitem__(Ellipsis, val.reshape(D)))
        for j in range(D_IN // D):
            piece = pl.ds(j * D, D)
            step(g_ref[ROW_B_IN + j:ROW_B_IN + j + 1, :], ins[N_VECS], outs[N_VECS],
                 lambda r: r[piece].reshape(1, D), lambda o, val: o.__setitem__(piece, val.reshape(D)))
        for h in range(N_GROUPS):
            step(g_ref[ROW_W_SPATIAL:ROW_W_SPATIAL + CHUNK, h * CHUNK:(h + 1) * CHUNK], ins[N_VECS + 1], outs[N_VECS + 1],
                 lambda r: r[h], lambda o, val: o.__setitem__(h, val))
        step(g_ref[ROW_B_SPATIAL:ROW_B_SPATIAL + N_GROUPS, 0:CHUNK], ins[N_VECS + 2], outs[N_VECS + 2],
             lambda r: r[...], lambda o, val: o.__setitem__(Ellipsis, val))
        lanes = g_ref[ROW_LOSS:ROW_LOSS + 1, :]
        loss_ref[...] = jnp.broadcast_to(jnp.sum(lanes, axis=1, keepdims=True), (8, 128))

    flat = [a for tr in triples for a in tr]
    out_shape = [jax.ShapeDtypeStruct(tr[0].shape, F32) for tr in triples for _ in range(4)]
    out_shape.append(jax.ShapeDtypeStruct((8, 128), F32))
    res = pl.pallas_call(
        body, name="adamw_small",
        in_specs=[_full(parts.shape)] + [_full(a.shape) for a in flat],
        out_specs=[_full(o.shape) for o in out_shape],
        out_shape=out_shape,
        scratch_shapes=[pltpu.VMEM((SMALL_ROWS, D), F32)],
        compiler_params=_params(),
    )(parts, *flat)
    return [res[4 * i:4 * i + 4] for i in range(len(triples))], res[-1]


SUBLANES = 8
SHIFT_ROWS = HALO - SUBLANES


def _shifted_copies(src_ref, sh_ref, cs, tm):
    for p in range(1, SUBLANES):
        sh_ref[p - 1] = src_ref[pl.ds(p, tm + SHIFT_ROWS), cs]


def _tap(src_ref, sh_ref, cs, offset, start, rows):
    p, q = offset % SUBLANES, offset // SUBLANES
    if p == 0:
        return src_ref[pl.ds(start + SUBLANES * q, rows), cs]
    return sh_ref[p - 1, pl.ds(start + SUBLANES * q, rows), :]


def _conv_taps(src_ref, sh_ref, w_ref, first_offset, step, bias, dst_ref, tm):
    rows = 64
    for g in range(N_GROUPS):
        cs = slice(g * GROUP_W, (g + 1) * GROUP_W)
        _shifted_copies(src_ref, sh_ref, cs, tm)
        for rb in range(tm // rows):
            acc = jnp.zeros((rows, GROUP_W), F32) + (bias[:, cs] if bias is not None else 0.0)
            for k in range(CONV_K):
                acc = acc + w_ref[k:k + 1, cs] * _tap(src_ref, sh_ref, cs, first_offset + step * k, rb * rows, rows)
            dst_ref[rb * rows:(rb + 1) * rows, cs] = acc


def _conv_weight_grad(d_ref, src_ref, sh_ref, first_offset, acc_ref, tm):
    rows = 64
    for g in range(N_GROUPS):
        cs = slice(g * GROUP_W, (g + 1) * GROUP_W)
        _shifted_copies(src_ref, sh_ref, cs, tm)
        for rb in range(tm // rows):
            d = d_ref[rb * rows:(rb + 1) * rows, cs]
            for k in range(CONV_K):
                prod = d * _tap(src_ref, sh_ref, cs, first_offset + k, rb * rows, rows)
                acc_ref[SUBLANES * k:SUBLANES * (k + 1), cs] += jnp.sum(
                    prod.reshape(rows // SUBLANES, SUBLANES, GROUP_W), axis=0)


def _spatial_mix(w_ref, v_bf, tm):
    rows = []
    for q in range(tm // CHUNK):
        cols = [_dot(w_ref[h], v_bf[q * CHUNK:(q + 1) * CHUNK, h * GROUP_W:(h + 1) * GROUP_W])
                for h in range(N_GROUPS)]
        rows.append(jnp.concatenate(cols, axis=1))
    return jnp.concatenate(rows, axis=0)


def _group_norm_fwd(h1, gn_g, gn_b):
    xhat, rstd = [], []
    for g in range(N_GROUPS):
        xh, rs = _norm_stats(h1[:, g * GROUP_W:(g + 1) * GROUP_W])
        xhat.append(xh)
        rstd.append(rs)
    xhat = jnp.concatenate(xhat, axis=1)
    return xhat * gn_g + gn_b, xhat, rstd


def _forward_tiles(p, x, tgt, wpa, wpb, wo, convw, vecs, ws, bsp, tiles_per_seq):
    t = x.shape[0]
    tm = TOKEN_TILE
    hb = tm // HALO

    def body(p_ref, ph_ref, x_ref, t_ref, wpa_ref, wpb_ref, wo_ref, cw_ref, vec_ref, ws_ref, bsp_ref,
             h1_ref, ya_ref, yb_ref, h3_ref, s_ref, mx_ref, dr_ref, drb_ref, xt_ref, acc_ref, he_ref, sh_ref):
        i = pl.program_id(0)
        xt_ref[...] = x_ref[...].T.astype(BF16)
        conv_b, gn_g, gn_b, lnv_g, lnv_b, b_o, lno_g, lno_b = [vec_ref[j:j + 1, :] for j in range(8)]

        keep = jnp.where(i % tiles_per_seq == 0, 0.0, 1.0)
        he_ref[0:HALO, :] = ph_ref[:, 0:D] * _sigmoid(ph_ref[:, D:2 * D]) * keep
        he_ref[HALO:, :] = p_ref[:, 0:D] * _sigmoid(p_ref[:, D:2 * D])
        _conv_taps(he_ref, sh_ref, cw_ref, HALO - (CONV_K - 1), 1, conv_b, h1_ref, tm)
        h2, _, _ = _group_norm_fwd(h1_ref[...], gn_g, gn_b)
        a_gate = p_ref[:, 2 * D:3 * D]
        h3 = ((h2 * _sigmoid(h2)) * (a_gate * _sigmoid(a_gate))).astype(BF16)
        h3_ref[...] = h3
        ya = _dot(h3, wpa_ref[...])
        ya_ref[...] = ya

        u = _gelu(p_ref[:, 3 * D:4 * D])
        vhat, _ = _norm_stats(_gelu(p_ref[:, 4 * D:5 * D]))
        v1 = (vhat * lnv_g + lnv_b).astype(BF16)
        b_gate = p_ref[:, 5 * D:6 * D]
        vmix = _spatial_mix(ws_ref, v1, tm) + jnp.concatenate([bsp_ref[...]] * (tm // CHUNK), axis=0)
        s = (u * vmix * (b_gate * _sigmoid(b_gate))).astype(BF16)
        s_ref[...] = s
        yb = _dot(s, wpb_ref[...])
        yb_ref[...] = yb

        mixed = (_sigmoid(p_ref[:, 6 * D:7 * D]) * ya + _sigmoid(p_ref[:, 7 * D:8 * D]) * yb).astype(BF16)
        mx_ref[...] = mixed
        r = ALPHA * x_ref[...] + (_dot(mixed, wo_ref[...]) + b_o)
        xhat, rstd = _norm_stats(r)
        err = (xhat * lno_g + lno_b) - t_ref[...]
        dout = err * (1.0 / D)
        dr = _norm_bwd(dout * lno_g, xhat, rstd)
        dr_ref[...] = dr
        drb_ref[...] = dr.astype(BF16)

        @pl.when(i == 0)
        def _():
            acc_ref[...] = jnp.zeros_like(acc_ref)

        acc_ref[0:1, :] += _colsum(dout * xhat)
        acc_ref[1:2, :] += _colsum(dout)
        acc_ref[2:3, :] += _colsum(dr)
        acc_ref[3:4, :] += _colsum(err * err) * (0.5 / D)

    tile = lambda w: pl.BlockSpec((tm, w), lambda i: (i, 0))
    f32_out = jax.ShapeDtypeStruct((t, D), F32)
    bf_out = jax.ShapeDtypeStruct((t, D), BF16)
    return pl.pallas_call(
        body, name="forward_tiles", grid=(t // tm,),
        in_specs=[tile(D_IN),
                  pl.BlockSpec((HALO, 2 * D), lambda i: (jnp.maximum(i * hb - 1, 0), 0)),
                  tile(D), tile(D), _resident((D, D)), _resident((D, D)), _resident((D, D)), _full((HALO, D)), _full((8, D)),
                  _full((N_GROUPS, CHUNK, CHUNK)), _full((CHUNK, D))],
        out_specs=[tile(D)] * 8 + [pl.BlockSpec((D, tm), lambda i: (0, i)), _full((8, D))],
        out_shape=[f32_out, f32_out, f32_out, bf_out, bf_out, bf_out, f32_out, bf_out,
                   jax.ShapeDtypeStruct((D, t), BF16), jax.ShapeDtypeStruct((8, D), F32)],
        scratch_shapes=[pltpu.VMEM((tm + HALO, D), F32), pltpu.VMEM((SUBLANES - 1, tm + SHIFT_ROWS, GROUP_W), F32)],
        compiler_params=_params("arbitrary"),
    )(p, p, x, tgt, wpa, wpb, wo, convw, vecs, ws, bsp)


def _backward_tiles(p, h1, ya, yb, drb, wpa, wpb, wo, vecs, ws, wst, bsp):
    t = h1.shape[0]
    tm = TOKEN_TILE

    def body(p_ref, h1_ref, ya_ref, yb_ref, drb_ref, wpa_ref, wpb_ref, wo_ref, vec_ref, ws_ref, wst_ref, bsp_ref,
             dh1_ref, dp_ref, dya_ref, dyb_ref, acc_ref, dbin_ref, dws_ref, dbsp_ref):
        i = pl.program_id(0)
        _, gn_g, gn_b, lnv_g, lnv_b = [vec_ref[j:j + 1, :] for j in range(5)]

        @pl.when(i == 0)
        def _():
            acc_ref[...] = jnp.zeros_like(acc_ref)
            dbin_ref[...] = jnp.zeros_like(dbin_ref)
            dws_ref[...] = jnp.zeros_like(dws_ref)
            dbsp_ref[...] = jnp.zeros_like(dbsp_ref)

        def emit(block, val):
            dbin_ref[0:1, block * D:(block + 1) * D] += _colsum(val)
            dp_ref[:, block * D:(block + 1) * D] = val.astype(BF16)

        dp_ref[:, 0:2 * D] = jnp.zeros((tm, 2 * D), BF16)
        dmixed = _dot_nt(drb_ref[...], wo_ref[...])
        ga = _sigmoid(p_ref[:, 6 * D:7 * D])
        gb = _sigmoid(p_ref[:, 7 * D:8 * D])
        dya = (dmixed * ga).astype(BF16)
        dyb = (dmixed * gb).astype(BF16)
        dya_ref[...] = dya
        dyb_ref[...] = dyb
        emit(6, dmixed * ya_ref[...] * (ga * (1.0 - ga)))
        emit(7, dmixed * yb_ref[...] * (gb * (1.0 - gb)))

        dh3 = _dot_nt(dya, wpa_ref[...])
        h2, xhat, rstd = _group_norm_fwd(h1_ref[...], gn_g, gn_b)
        sg = _sigmoid(h2)
        a_gate = p_ref[:, 2 * D:3 * D]
        sa = _sigmoid(a_gate)
        dh2 = dh3 * (a_gate * sa) * (sg * (1.0 + h2 * (1.0 - sg)))
        emit(2, dh3 * (h2 * sg) * (sa * (1.0 + a_gate * (1.0 - sa))))
        acc_ref[0:1, :] += _colsum(dh2 * xhat)
        acc_ref[1:2, :] += _colsum(dh2)
        dxhat = dh2 * gn_g
        for g in range(N_GROUPS):
            cs = slice(g * GROUP_W, (g + 1) * GROUP_W)
            dh1_ref[:, cs] = _norm_bwd(dxhat[:, cs], xhat[:, cs], rstd[g])

        ds = _dot_nt(dyb, wpb_ref[...])
        u_pre = p_ref[:, 3 * D:4 * D]
        u, du_dpre = _gelu_and_grad(u_pre)
        v0, dv_dpre = _gelu_and_grad(p_ref[:, 4 * D:5 * D])
        vhat, vrstd = _norm_stats(v0)
        v1 = (vhat * lnv_g + lnv_b).astype(BF16)
        vmix = _spatial_mix(ws_ref, v1, tm) + jnp.concatenate([bsp_ref[...]] * (tm // CHUNK), axis=0)
        b_gate = p_ref[:, 5 * D:6 * D]
        sb = _sigmoid(b_gate)
        silu_b = b_gate * sb
        emit(3, ds * vmix * silu_b * du_dpre)
        emit(5, ds * u * vmix * (sb * (1.0 + b_gate * (1.0 - sb))))
        dvmix = ds * u * silu_b
        dvmix_bf = dvmix.astype(BF16)
        for q in range(tm // CHUNK):
            dbsp_ref[...] += dvmix[q * CHUNK:(q + 1) * CHUNK, :]
            for h in range(N_GROUPS):
                blk = (slice(q * CHUNK, (q + 1) * CHUNK), slice(h * GROUP_W, (h + 1) * GROUP_W))
                dws_ref[:, h * GROUP_W:(h + 1) * GROUP_W] += _dot_nt(dvmix_bf[blk], v1[blk])
        dv1 = _spatial_mix(wst_ref, dvmix_bf, tm)
        acc_ref[2:3, :] += _colsum(dv1 * vhat)
        acc_ref[3:4, :] += _colsum(dv1)
        emit(4, _norm_bwd(dv1 * lnv_g, vhat, vrstd) * dv_dpre)

    tile = lambda w: pl.BlockSpec((tm, w), lambda i: (i, 0))
    return pl.pallas_call(
        body, name="backward_tiles", grid=(t // tm,),
        in_specs=[tile(D_IN), tile(D), tile(D), tile(D), tile(D), _resident((D, D)), _resident((D, D)), _resident((D, D)),
                  _full((8, D)), _full((N_GROUPS, CHUNK, CHUNK)), _full((N_GROUPS, CHUNK, CHUNK)), _full((CHUNK, D))],
        out_specs=[tile(D), tile(D_IN), tile(D), tile(D), _full((8, D)), _full((8, D_IN)),
                   _full((CHUNK, D)), _full((CHUNK, D))],
        out_shape=[jax.ShapeDtypeStruct((t, D), F32), jax.ShapeDtypeStruct((t, D_IN), BF16),
                   jax.ShapeDtypeStruct((t, D), BF16), jax.ShapeDtypeStruct((t, D), BF16),
                   jax.ShapeDtypeStruct((8, D), F32), jax.ShapeDtypeStruct((8, D_IN), F32),
                   jax.ShapeDtypeStruct((CHUNK, D), F32), jax.ShapeDtypeStruct((CHUNK, D), F32)],
        compiler_params=_params("arbitrary"),
    )(p, h1, ya, yb, drb, wpa, wpb, wo, vecs, ws, wst, bsp)


def _conv_backward(dh1, p, dp, convw, pairs, tiles_per_seq):
    t = dh1.shape[0]
    tm = TOKEN_TILE
    hb = tm // HALO
    last = t // HALO - 1
    n_sq = len(pairs)
    span = 2
    rows = D // 8

    def body(dh1_ref, dnext_ref, p_ref, ph_ref, cw_ref, dp_in_ref, *refs):
        del dp_in_ref
        sq_in = refs[:2 * n_sq]
        dp_ref, dcw_ref, dbin_ref = refs[2 * n_sq:2 * n_sq + 3]
        sq_out = refs[2 * n_sq + 3:3 * n_sq + 3]
        sq_wire = refs[3 * n_sq + 3:4 * n_sq + 3]
        de_ref, he_ref, dh0_ref, sh_ref, acc_ref, wire_ref, sq_sem, wire_sem = refs[4 * n_sq + 3:]
        i = pl.program_id(0)

        @pl.when(i == 0)
        def _():
            dcw_ref[...] = jnp.zeros_like(dcw_ref)
            dbin_ref[...] = jnp.zeros_like(dbin_ref)
            acc_ref[...] = jnp.zeros_like(acc_ref)

        @pl.when(i % span == span - 1)
        def _():
            for a in range(n_sq):
                acc_ref[a] += _dot_tn(sq_in[2 * a][...], sq_in[2 * a + 1][...])

        keep_next = jnp.where(i % tiles_per_seq == tiles_per_seq - 1, 0.0, 1.0)
        de_ref[0:tm, :] = dh1_ref[...]
        de_ref[tm:, :] = dnext_ref[...] * keep_next
        _conv_taps(de_ref, sh_ref, cw_ref, CONV_K - 1, -1, None, dh0_ref, tm)

        keep_prev = jnp.where(i % tiles_per_seq == 0, 0.0, 1.0)
        sg = _sigmoid(p_ref[:, D:2 * D])
        val = p_ref[:, 0:D]
        he_ref[0:HALO, :] = ph_ref[:, 0:D] * _sigmoid(ph_ref[:, D:2 * D]) * keep_prev
        he_ref[HALO:, :] = val * sg
        _conv_weight_grad(dh1_ref, he_ref, sh_ref, HALO - (CONV_K - 1), dcw_ref, tm)
        dcw_ref[SUBLANES * CONV_K:, :] += jnp.sum(dh1_ref[...].reshape(tm // SUBLANES, SUBLANES, D), axis=0)

        dh0 = dh0_ref[...]
        dval = dh0 * sg
        dglu = dh0 * val * (sg * (1.0 - sg))
        dbin_ref[0:1, 0:D] += _colsum(dval)
        dbin_ref[0:1, D:2 * D] += _colsum(dglu)
        dp_ref[:, 0:D] = dval.astype(BF16)
        dp_ref[:, D:2 * D] = dglu.astype(BF16)

        @pl.when(i == t // tm - 1)
        def _():
            cps = [pltpu.make_async_copy(acc_ref.at[a, pl.ds((2 * j + h) * rows, rows)], sq_out[a].at[h, j],
                                         sq_sem.at[(a * N_CHIPS + j) * 2 + h])
                   for a in range(n_sq) for j in range(N_CHIPS) for h in range(2)]
            for cp in cps:
                cp.start()
            for a in range(n_sq):
                wire_ref[...] = acc_ref[a].astype(BF16)
                narrow = [pltpu.make_async_copy(wire_ref.at[pl.ds((2 * j + h) * rows, rows)], sq_wire[a].at[h, j],
                                                wire_sem.at[2 * j + h]) for j in range(N_CHIPS) for h in range(2)]
                for cp in narrow:
                    cp.start()
                for cp in narrow:
                    cp.wait()
            for cp in cps:
                cp.wait()

    any_spec = pl.BlockSpec(memory_space=pl.ANY)
    wide = pl.BlockSpec((span * tm, D), lambda i: (i // span, 0))
    return pl.pallas_call(
        body, name="conv_backward", grid=(t // tm,),
        in_specs=[pl.BlockSpec((tm, D), lambda i: (i, 0)),
                  pl.BlockSpec((HALO, D), lambda i: (jnp.minimum((i + 1) * hb, last), 0)),
                  pl.BlockSpec((tm, 2 * D), lambda i: (i, 0)),
                  pl.BlockSpec((HALO, 2 * D), lambda i: (jnp.maximum(i * hb - 1, 0), 0)),
                  _full((HALO, D)), any_spec] + [wide] * (2 * n_sq),
        out_specs=[pl.BlockSpec((tm, 2 * D), lambda i: (i, 0)), _full((SUBLANES * HALO, D)), _full((8, 2 * D))]
        + [any_spec] * (2 * n_sq),
        out_shape=[jax.ShapeDtypeStruct(dp.shape, BF16), jax.ShapeDtypeStruct((SUBLANES * HALO, D), F32),
                   jax.ShapeDtypeStruct((8, 2 * D), F32)]
        + [jax.ShapeDtypeStruct((2, N_CHIPS, rows, D), F32)] * n_sq
        + [jax.ShapeDtypeStruct((2, N_CHIPS, rows, D), BF16)] * n_sq,
        scratch_shapes=[pltpu.VMEM((tm + HALO, D), F32), pltpu.VMEM((tm + HALO, D), F32), pltpu.VMEM((tm, D), F32),
                        pltpu.VMEM((SUBLANES - 1, tm + SHIFT_ROWS, GROUP_W), F32), pltpu.VMEM((n_sq, D, D), F32),
                        pltpu.VMEM((D, D), BF16), pltpu.SemaphoreType.DMA((n_sq * N_CHIPS * 2,)),
                        pltpu.SemaphoreType.DMA((N_CHIPS * 2,))],
        input_output_aliases={5: 0},
        compiler_params=_params("arbitrary"),
    )(dh1, dh1, p, p, convw, dp, *[a for pair in pairs for a in pair])


def _grad_in_and_x(xt, dp, w4, dr, wires, grads):
    t = dr.shape[0]
    tm = TOKEN_TILE
    half, tn = D // 2, 512
    nb = W_BLOCK // tn
    n_w, n_x = 2 * N_CHIPS * nb, t // tm
    ns = len(grads)
    xi, yi, ci = lax.axis_index("x"), lax.axis_index("y"), lax.axis_index("c")
    others = [2 * (1 - xi) + yi, 2 * xi + (1 - yi), 2 * (1 - xi) + (1 - yi)]
    blocks = others + others + [2 * xi + yi] * 2
    halves = [1 - ci] * 3 + [ci] * 3 + [1 - ci, ci]
    table = jnp.stack([jnp.stack([b * nb + n for b in blocks for n in range(nb)]),
                       jnp.stack([h for h in halves for _ in range(nb)])]).astype(jnp.int32)

    def body(tab_ref, xt_ref, dpc_ref, dpr_ref, w_ref, dr_ref, *refs):
        parts, fulls = refs[:ns], refs[ns:2 * ns]
        dx_ref, qk_ref, b2_ref, b1_ref, wire_ref = refs[2 * ns:2 * ns + 5]
        lands, sibs = refs[2 * ns + 5:3 * ns + 5], refs[3 * ns + 5:4 * ns + 5]
        (g_ref, st_ref, sb_ref, tmp_ref, d2d_send, d2d_recv, ici_send, ici_recv, own_sem, tmp_sem, wire_sem,
         p_send, p_recv, s_send, s_recv) = refs[4 * ns + 5:]
        s = pl.program_id(0)
        x_, y_, c, k = _position()
        chips = _other_chips(x_, y_)
        n = s % nb
        grp = s // nb
        cols = pl.ds(pl.multiple_of(n * tn, tn), tn)

        def part(a, r, core):
            cx, cy = chips[r]
            return pltpu.make_async_remote_copy(
                src_ref=parts[a].at[core, 2 * cx + cy], dst_ref=lands[a].at[2 * r + c],
                send_sem=p_send.at[6 * a + 2 * r + core], recv_sem=p_recv.at[6 * a + 2 * r + c],
                device_id=(cx, cy, core), device_id_type=MESH)

        def landed(a, r, core):
            cx, cy = chips[r]
            return pltpu.make_async_remote_copy(
                src_ref=lands[a].at[2 * r + core], dst_ref=lands[a].at[2 * r + core],
                send_sem=p_send.at[6 * a + 2 * r + core], recv_sem=p_recv.at[6 * a + 2 * r + core],
                device_id=(cx, cy, core), device_id_type=MESH)

        def to_sibling_whole(a):
            return pltpu.make_async_remote_copy(
                src_ref=fulls[a].at[1 - c, k], dst_ref=sibs[a], send_sem=s_send.at[a], recv_sem=s_recv.at[a],
                device_id=(x_, y_, 1 - c), device_id_type=MESH)

        def to_sibling(slot, land):
            return pltpu.make_async_remote_copy(
                src_ref=st_ref.at[slot], dst_ref=b1_ref.at[land, :, cols], send_sem=d2d_send.at[slot],
                recv_sem=d2d_recv.at[land * nb + n], device_id=(x_, y_, 1 - c), device_id_type=MESH)

        def to_chip(r):
            cx, cy = chips[r]
            return pltpu.make_async_remote_copy(
                src_ref=wire_ref.at[r, :, cols], dst_ref=b2_ref.at[r, :, cols], send_sem=ici_send.at[r],
                recv_sem=ici_recv.at[r], device_id=(cx, cy, c), device_id_type=MESH)

        def all_of_chip(r):
            cx, cy = chips[r]
            return pltpu.make_async_remote_copy(
                src_ref=wire_ref.at[r], dst_ref=b2_ref.at[r], send_sem=ici_send.at[r],
                recv_sem=ici_recv.at[r], device_id=(cx, cy, c), device_id_type=MESH)

        def to_result(slot):
            return pltpu.make_async_copy(st_ref.at[slot], qk_ref.at[:, cols], own_sem.at[slot])

        def sibling_piece(land):
            return pltpu.make_async_copy(b1_ref.at[land, :, cols], tmp_ref, tmp_sem)

        @pl.when(s == 0)
        def _():
            for a in range(ns):
                to_sibling_whole(a).start()
                for r in range(3):
                    for core in range(2):
                        part(a, r, core).start()

        own_half = ((grp >= 3) & (grp <= 5)) | (grp == 7)
        land = jnp.where(grp == 7, 3, grp - 3)

        @pl.when(own_half)
        def _():
            to_sibling(0, land).wait_recv()
            sibling_piece(land).start()

        @pl.when(s < n_w)
        def _():
            g_ref[...] = _dot(xt_ref[tab_ref[1, s]], dpc_ref[...])

        @pl.when(own_half)
        def _():
            sibling_piece(land).wait()

        for g in range(2 * N_CHIPS):
            @pl.when(grp == g)
            def _(g=g):
                if g in (0, 1, 2, 6):
                    use = s if g < 3 else 3 * nb + n
                    slot = use % 2

                    @pl.when(use >= 2)
                    def _():
                        to_sibling(slot, 0).wait_send()

                    st_ref[slot] = g_ref[...]
                    to_sibling(slot, min(g, 3)).start()
                elif g in (3, 4, 5):
                    sb_ref[...] = (g_ref[...] + tmp_ref[...]).astype(BF16)
                    stage = pltpu.make_async_copy(sb_ref, wire_ref.at[g - 3, :, cols], wire_sem)
                    stage.start()
                    stage.wait()
                    to_chip(g - 3).start()
                else:
                    slot = n % 2
                    piece = g_ref[...] + tmp_ref[...]

                    @pl.when(n < 2)
                    def _():
                        to_sibling(slot, 0).wait_send()

                    @pl.when(n >= 2)
                    def _():
                        to_result(slot).wait()

                    st_ref[slot] = piece
                    to_result(slot).start()

        @pl.when(s >= n_w)
        def _():
            acc = ALPHA * dr_ref[...]
            for j in range(N_CHIPS):
                acc = acc + _dot_nt(dpr_ref[:, j * W_BLOCK:(j + 1) * W_BLOCK], w_ref[j])
            dx_ref[...] = acc

        @pl.when(s == n_w + n_x - 1)
        def _():
            for slot in range(2):
                to_result(slot).wait()
            for r in range(3):
                all_of_chip(r).wait_recv()
                all_of_chip(r).wait_send()
            for a in range(ns):
                to_sibling_whole(a).wait_recv()
                to_sibling_whole(a).wait_send()
                for r in range(3):
                    for core in range(2):
                        landed(a, r, core).wait_recv()
                        part(a, r, core).wait_send()

    any_spec = pl.BlockSpec(memory_space=pl.ANY)
    tile = lambda s, tab: (jnp.maximum(s - n_w, 0), 0)
    return pl.pallas_call(
        body, name="grad_in_and_x",
        grid_spec=pltpu.PrefetchScalarGridSpec(
            num_scalar_prefetch=1, grid=(n_w + n_x,),
            in_specs=[pl.BlockSpec((2, half, t), lambda s, tab: (0, 0, 0), pipeline_mode=pl.Buffered(1)),
                      pl.BlockSpec((t, tn), lambda s, tab: (0, tab[0, jnp.minimum(s, n_w - 1)])),
                      pl.BlockSpec((tm, D_IN), tile),
                      pl.BlockSpec((N_CHIPS, D, W_BLOCK), lambda s, tab: (0, 0, 0), pipeline_mode=pl.Buffered(1)),
                      pl.BlockSpec((tm, D), tile)] + [any_spec] * (2 * ns),
            out_specs=[pl.BlockSpec((tm, D), tile)] + [any_spec] * (4 + 2 * ns),
            scratch_shapes=[pltpu.VMEM((half, tn), F32), pltpu.VMEM((2, half, tn), F32), pltpu.VMEM((half, tn), BF16),
                            pltpu.VMEM((half, tn), F32),
                            pltpu.SemaphoreType.DMA((2,)), pltpu.SemaphoreType.DMA((N_CHIPS * nb,)),
                            pltpu.SemaphoreType.DMA((3,)), pltpu.SemaphoreType.DMA((3,)),
                            pltpu.SemaphoreType.DMA((2,)), pltpu.SemaphoreType.DMA, pltpu.SemaphoreType.DMA,
                            pltpu.SemaphoreType.DMA((6 * ns,)), pltpu.SemaphoreType.DMA((6 * ns,)),
                            pltpu.SemaphoreType.DMA((ns,)), pltpu.SemaphoreType.DMA((ns,))]),
        out_shape=[jax.ShapeDtypeStruct((t, D), F32), jax.ShapeDtypeStruct((half, W_BLOCK), F32),
                   jax.ShapeDtypeStruct((3, half, W_BLOCK), BF16), jax.ShapeDtypeStruct((N_CHIPS, half, W_BLOCK), F32),
                   jax.ShapeDtypeStruct((3, half, W_BLOCK), BF16)]
        + [jax.ShapeDtypeStruct((6,) + w.shape[2:], w.dtype) for w in wires]
        + [jax.ShapeDtypeStruct(g.shape[2:], F32) for g in grads],
        compiler_params=_params("arbitrary"),
    )(table, xt, dp, dp, w4, dr, *wires, *grads)


def kernel(x, w_in, b_in, conv_w, conv_b, gn_g, gn_b, ln_v_g, ln_v_b, w_spatial, b_spatial, w_pa, w_pb, w_o, b_o, ln_out_g, ln_out_b, loss_target, m_w_in, m_b_in, m_conv_w, m_conv_b, m_gn_g, m_gn_b, m_ln_v_g, m_ln_v_b, m_w_spatial, m_b_spatial, m_w_pa, m_w_pb, m_w_o, m_b_o, m_ln_out_g, m_ln_out_b, v_w_in, v_b_in, v_conv_w, v_conv_b, v_gn_g, v_gn_b, v_ln_v_g, v_ln_v_b, v_w_spatial, v_b_spatial, v_w_pa, v_w_pb, v_w_o, v_b_o, v_ln_out_g, v_ln_out_b):
    n_seq, seq, _ = x.shape
    t = n_seq * seq
    tiles_per_seq = seq // TOKEN_TILE
    x2 = x.reshape(t, D)
    tgt = loss_target.reshape(t, D)

    conv_shard = jnp.pad(conv_w, ((0, HALO - CONV_K), (0, 0)))
    p, win4, wpa4, wpb4, wo4, conv4 = _proj_gather(
        x2, b_in,
        [_place_shard(w_in, 256, BF16), _place_shard(w_pa, 128, BF16), _place_shard(w_pb, 128, BF16),
         _place_shard(w_o, 128, BF16), _place_shard(conv_shard, HALO // 2, F32)])
    win4 = win4.reshape(N_CHIPS, D, W_BLOCK)
    wpa, wpb, wo = wpa4.reshape(D, D), wpb4.reshape(D, D), wo4.reshape(D, D)
    convw = conv4.reshape(N_CHIPS, HALO, D // N_CHIPS).transpose(1, 0, 2).reshape(HALO, D)

    vecs = jnp.stack([conv_b, gn_g, gn_b, ln_v_g, ln_v_b, b_o, ln_out_g, ln_out_b])
    causal = jnp.tril(jnp.ones((CHUNK, CHUNK), bool))
    ws = jnp.where(causal[None], w_spatial, 0.0)
    ws_bf, wst_bf = ws.astype(BF16), ws.transpose(0, 2, 1).astype(BF16)
    bsp = jnp.repeat(b_spatial.T, GROUP_W, axis=1)

    h1, ya, yb, h3, s, mixed, dr, drb, xt, acc_f = _forward_tiles(p, x2, tgt, wpa, wpb, wo, convw, vecs, ws_bf, bsp, tiles_per_seq)
    dh1, dp, dya, dyb, acc_b, dbin_b, dws, dbsp_acc = _backward_tiles(p, h1, ya, yb, drb, wpa, wpb, wo, vecs, ws_bf, wst_bf, bsp)
    dp, dcw8, dbin_a, *square = _conv_backward(dh1, p, dp, convw, [(h3, dya), (s, dyb), (mixed, drb)], tiles_per_seq)

    small, g_conv = _pack_small(acc_f, acc_b, dbin_a, dbin_b, dcw8, dws, dbsp_acc)
    small = small.reshape(2, N_CHIPS, SMALL_ROWS // 8, D)

    grads = square[:3] + [g_conv, small]
    wires = square[3:] + [g_conv, small]
    grad_x, q_in, chips_in, _, _, *landed = _grad_in_and_x(xt.reshape(2, D // 2, t), dp, win4, dr, wires, grads)
    grad_x = grad_x.reshape(x.shape)
    mine = [_add_chips(q_in, chips_in)]
    mine += [_add_devices(g, land, sib, a == 4) for a, (g, land, sib) in enumerate(zip(grads, landed[:5], landed[5:]))]
    *full, small_parts = _share_results(mine[:5], mine[5])
    grad_w_in, grad_w_pa, grad_w_pb, grad_w_o = [f.reshape(w.shape) for f, w in zip(full[:4], (w_in, w_pa, w_pb, w_o))]
    grad_conv_w = full[4].reshape(HALO, D // N_CHIPS)[:CONV_K]

    big = {}
    for name, w, g, m, v in [("w_in", w_in, grad_w_in, m_w_in, v_w_in), ("w_pa", w_pa, grad_w_pa, m_w_pa, v_w_pa),
                             ("w_pb", w_pb, grad_w_pb, m_w_pb, v_w_pb), ("w_o", w_o, grad_w_o, m_w_o, v_w_o),
                             ("conv_w", conv_w, grad_conv_w, m_conv_w, v_conv_w)]:
        big[name] = (g,) + tuple(_adamw(w, g, m, v))
    vec_names = ["conv_b", "gn_g", "gn_b", "ln_v_g", "ln_v_b", "b_o", "ln_out_g", "ln_out_b"]
    vec_triples = [(conv_b, m_conv_b, v_conv_b), (gn_g, m_gn_g, v_gn_g), (gn_b, m_gn_b, v_gn_b),
                   (ln_v_g, m_ln_v_g, v_ln_v_g), (ln_v_b, m_ln_v_b, v_ln_v_b), (b_o, m_b_o, v_b_o),
                   (ln_out_g, m_ln_out_g, v_ln_out_g), (ln_out_b, m_ln_out_b, v_ln_out_b)]
    small_res, loss8 = _adamw_small(
        small_parts.reshape(8, SMALL_ROWS // 8, D), vec_triples, (b_in, m_b_in, v_b_in),
        (w_spatial, m_w_spatial, v_w_spatial), (b_spatial, m_b_spatial, v_b_spatial))
    per_name = dict(zip(vec_names + ["b_in", "w_spatial", "b_spatial"], small_res))

    order = ["w_in", "b_in", "conv_w", "conv_b", "gn_g", "gn_b", "ln_v_g", "ln_v_b", "w_spatial", "b_spatial",
             "w_pa", "w_pb", "w_o", "b_o", "ln_out_g", "ln_out_b"]
    outs = [loss8[0, 0], grad_x]
    for kind in range(4):
        outs += [big[n][kind] if n in big else per_name[n][kind] for n in order]
    return tuple(outs)
```

```python
import functools
import math

import jax
import jax.numpy as jnp
from jax import lax
from jax.experimental import pallas as pl
from jax.experimental.pallas import tpu as pltpu

D = 1024
N_GROUPS = 8
GROUP_W = D // N_GROUPS
CHUNK = 128
CONV_K = 31
HALO = 32
D_IN = 8 * D
N_CHIPS = 4
W_BLOCK = D_IN // N_CHIPS
ALPHA = 2.0 ** 0.25
LN_EPS = 1e-5
ADAM_LR, ADAM_B1, ADAM_B2, ADAM_EPS, ADAM_WD, ADAM_STEP = 0.001, 0.9, 0.999, 1e-08, 0.01, 10

TOKEN_TILE = 256
VMEM_LIMIT = 56 * 1024 * 1024
MESH = pl.DeviceIdType.MESH
F32, BF16 = jnp.float32, jnp.bfloat16


def _sigmoid(x):
    return 1.0 / (1.0 + jnp.exp(-x))


def _gelu(x):
    c = math.sqrt(2.0 / math.pi)
    t = jnp.tanh(c * (x + 0.044715 * (x * x * x)))
    return x * (0.5 * (1.0 + t))


def _gelu_and_grad(x):
    c = math.sqrt(2.0 / math.pi)
    x2 = x * x
    t = jnp.tanh(c * (x + 0.044715 * (x2 * x)))
    cdf = 0.5 * (1.0 + t)
    return x * cdf, cdf + 0.5 * x * (1.0 - t * t) * (c * (1.0 + 3.0 * 0.044715 * x2))


def _norm_stats(v):
    mu = jnp.mean(v, axis=-1, keepdims=True)
    vc = v - mu
    var = jnp.mean(vc * vc, axis=-1, keepdims=True)
    rstd = lax.rsqrt(var + LN_EPS)
    return vc * rstd, rstd


def _norm_bwd(dxhat, xhat, rstd):
    m1 = jnp.mean(dxhat, axis=-1, keepdims=True)
    m2 = jnp.mean(dxhat * xhat, axis=-1, keepdims=True)
    return rstd * (dxhat - m1 - xhat * m2)


def _dot(a, b):
    return jnp.dot(a, b, preferred_element_type=F32)


def _dot_nt(a, b):
    return lax.dot_general(a, b, (((1,), (1,)), ((), ())), preferred_element_type=F32)


def _dot_tn(a, b):
    return lax.dot_general(a, b, (((0,), (0,)), ((), ())), preferred_element_type=F32)


def _colsum(v):
    return jnp.sum(v, axis=0, keepdims=True)


def _full(shape):
    return pl.BlockSpec(shape, lambda *_: (0,) * len(shape))


def _resident(shape):
    return pl.BlockSpec(shape, lambda *_: (0,) * len(shape), pipeline_mode=pl.Buffered(1))


def _params(*sem):
    return pltpu.CompilerParams(dimension_semantics=sem, vmem_limit_bytes=VMEM_LIMIT)


def _chip_index():
    return (2 * lax.axis_index("x") + lax.axis_index("y")).astype(jnp.int32).reshape(1)


def _core_index():
    return lax.axis_index("c").astype(jnp.int32).reshape(1)


def _place_shard(w, rows, dtype):
    r, c = w.shape
    steps = r // 2 // rows

    def body(k_ref, w_ref, o_ref):
        o_ref[...] = w_ref[...].astype(dtype)

    return pl.pallas_call(
        body, name="place_shard",
        grid_spec=pltpu.PrefetchScalarGridSpec(
            num_scalar_prefetch=1, grid=(2, steps),
            in_specs=[pl.BlockSpec((rows, c), lambda h, i, k: (h * steps + i, 0))],
            out_specs=pl.BlockSpec((None, None, rows, c), lambda h, i, k: (k[0], h, i, 0))),
        out_shape=jax.ShapeDtypeStruct((N_CHIPS, 2, r // 2, c), dtype),
        compiler_params=_params("parallel", "parallel"),
    )(_chip_index(), w)


def _position():
    x, y, c = lax.axis_index("x"), lax.axis_index("y"), lax.axis_index("c")
    return x, y, c, 2 * x + y


def _other_chips(x, y):
    return [(1 - x, y), (x, 1 - y), (1 - x, 1 - y)]


def _any_specs(n):
    return [pl.BlockSpec(memory_space=pl.ANY)] * n


def _proj_gather(x, b_in, bufs):
    t = x.shape[0]
    tm = min(2048, t)
    steps = t // tm
    ahead = steps // 2
    half = D // 2
    chunk = W_BLOCK // 2
    n = len(bufs)
    xi, yi = lax.axis_index("x"), lax.axis_index("y")
    chips = [2 * xi + yi, 2 * (1 - xi) + yi, 2 * xi + (1 - yi), 2 * (1 - xi) + (1 - yi)]
    plan = [(0, 0), (0, 1), (1, 0), (2, 1), (1, 1), (2, 0), (3, 0), (3, 1)]
    order = jnp.stack([2 * chips[ch] + q for ch, q in plan]).astype(jnp.int32)

    def body(order_ref, x_ref, b_ref, *refs):
        p_ref, outs = refs[n], refs[n + 1:2 * n + 1]
        xb_ref, w_ref, lsem, send, recv, hop_send, hop_recv, fsend, frecv, qsend, qrecv = refs[2 * n + 1:]
        jj, i = pl.program_id(0), pl.program_id(1)
        x_, y_, c, k = _position()
        nbrs = [(1 - x_, y_), (x_, 1 - y_)]
        blocks = [2 * (1 - x_) + y_, 2 * x_ + (1 - y_), 2 * (1 - x_) + (1 - y_)]

        def quarter(a, block, q, h):
            if a == 0:
                return outs[0].at[block, h, :, pl.ds(q * chunk, chunk)]
            rows = outs[a].shape[2] // 2
            return outs[a].at[block, h, pl.ds(q * rows, rows)]

        def copy(ref, to, send_sem, recv_sem):
            return pltpu.make_async_remote_copy(src_ref=ref, dst_ref=ref, send_sem=send_sem, recv_sem=recv_sem,
                                                device_id=(to[0], to[1], c), device_id_type=MESH)

        def sent(a, nb, q):
            return copy(quarter(a, k, q, c), nbrs[nb], send.at[4 * a + 2 * nb + q], recv.at[4 * a + 2 * nb + q])

        def landed(a, nb, q):
            return copy(quarter(a, blocks[nb], q, c), nbrs[nb], send.at[4 * a + 2 * nb + q], recv.at[4 * a + 2 * nb + q])

        def hopped(a, nb):
            return copy(quarter(a, blocks[nb], nb, c), nbrs[1 - nb], hop_send.at[2 * a + nb], hop_recv.at[2 * a + 1 - nb])

        def from_diagonal(a, via):
            return copy(quarter(a, blocks[2], 1 - via, c), nbrs[via], hop_send.at[2 * a + via], hop_recv.at[2 * a + via])

        def passed(a, r, h):
            return pltpu.make_async_remote_copy(
                src_ref=outs[a].at[blocks[r], h], dst_ref=outs[a].at[blocks[r], h], send_sem=fsend.at[3 * a + r],
                recv_sem=frecv.at[3 * a + r], device_id=(x_, y_, 1 - c), device_id_type=MESH)

        def passed_quarter(r, q, h):
            ref = quarter(0, blocks[r], q, h)
            return pltpu.make_async_remote_copy(
                src_ref=ref, dst_ref=ref, send_sem=qsend.at[2 * r + q], recv_sem=qrecv.at[2 * r + q],
                device_id=(x_, y_, 1 - c), device_id_type=MESH)

        def load(block, q, slot):
            return [pltpu.make_async_copy(quarter(0, block, q, h), w_ref.at[slot, pl.ds(h * half, half)],
                                          lsem.at[2 * slot + h]) for h in range(2)]

        def pass_on(arrays):
            for a in arrays:
                for nb in range(2):
                    landed(a, nb, nb).wait_recv()
                    hopped(a, nb).start()

        @pl.when((jj == 0) & (i == 0))
        def _():
            for a in range(n):
                for nb, q in ((0, 0), (1, 1), (0, 1), (1, 0)):
                    sent(a, nb, q).start()
            for cp in load(k, 0, 0):
                cp.start()

        for nxt in range(1, len(plan)):
            @pl.when((jj == nxt - 1) & (i == ahead))
            def _(nxt=nxt):
                ch, q = plan[nxt]
                if ch in (1, 2):
                    landed(0, ch - 1, q).wait_recv()
                    if q == ch - 1:
                        hopped(0, ch - 1).start()
                elif ch == 3:
                    from_diagonal(0, 1 - q).wait_recv()
                if ch:
                    passed_quarter(ch - 1, q, c).start()
                    passed_quarter(ch - 1, q, 1 - c).wait_recv()
                for cp in load(k if ch == 0 else blocks[ch - 1], q, nxt % 2):
                    cp.start()
                if nxt == 5:
                    pass_on(range(1, n))
                    for a in range(1, n):
                        landed(a, 0, 1).wait_recv()
                        passed(a, 0, c).start()
                        landed(a, 1, 0).wait_recv()
                        passed(a, 1, c).start()

        slot = jj % 2

        @pl.when(i == 0)
        def _():
            for cp in load(k, 0, slot):
                cp.wait()

        rows = pl.ds(pl.multiple_of(i * tm, tm), tm)

        @pl.when(jj == 0)
        def _():
            xb_ref[rows, :] = x_ref[...].astype(BF16)

        p_ref[...] = _dot(xb_ref[rows, :], w_ref[slot]) + b_ref[...]

        @pl.when((jj == len(plan) - 1) & (i == steps - 1))
        def _():
            for a in range(1, n):
                from_diagonal(a, 0).wait_recv()
                from_diagonal(a, 1).wait_recv()
                passed(a, 2, c).start()
            for a in range(1, n):
                for r in range(3):
                    passed(a, r, 1 - c).wait_recv()
                    passed(a, r, c).wait_send()
            for r in range(3):
                for q in range(2):
                    passed_quarter(r, q, c).wait_send()
            for a in range(n):
                for nb in range(2):
                    for q in range(2):
                        sent(a, nb, q).wait_send()
                    hopped(a, nb).wait_send()

    any_spec = pl.BlockSpec(memory_space=pl.ANY)
    return pl.pallas_call(
        body, name="proj_gather",
        grid_spec=pltpu.PrefetchScalarGridSpec(
            num_scalar_prefetch=1, grid=(len(plan), steps),
            in_specs=[pl.BlockSpec((tm, D), lambda jj, i, o: (jnp.where(jj == 0, i, steps - 1), 0)),
                      pl.BlockSpec((None, 1, chunk), lambda jj, i, o: (o[jj], 0, 0))] + [any_spec] * n,
            out_specs=[pl.BlockSpec((tm, chunk), lambda jj, i, o: (i, o[jj]))] + [any_spec] * n,
            scratch_shapes=[pltpu.VMEM((t, D), BF16), pltpu.VMEM((2, D, chunk), BF16), pltpu.SemaphoreType.DMA((4,)),
                            pltpu.SemaphoreType.DMA((4 * n,)), pltpu.SemaphoreType.DMA((4 * n,)),
                            pltpu.SemaphoreType.DMA((2 * n,)), pltpu.SemaphoreType.DMA((2 * n,)),
                            pltpu.SemaphoreType.DMA((3 * n,)), pltpu.SemaphoreType.DMA((3 * n,)),
                            pltpu.SemaphoreType.DMA((6,)), pltpu.SemaphoreType.DMA((6,))]),
        out_shape=[jax.ShapeDtypeStruct((t, D_IN), F32)] + [jax.ShapeDtypeStruct(b.shape, b.dtype) for b in bufs],
        input_output_aliases={3 + a: 1 + a for a in range(n)},
        compiler_params=_params("arbitrary", "arbitrary"),
    )(order, x, b_in.reshape(D_IN // chunk, 1, chunk), *bufs)


def _share_results(bufs, small):
    n = len(bufs)

    def body(*refs):
        outs, small_out = refs[n + 1:2 * n + 1], refs[2 * n + 1]
        send, recv, ssend, srecv = refs[2 * n + 2:]
        x, y, c, k = _position()
        cps = []
        for a in range(n):
            cp = pltpu.make_async_remote_copy(
                src_ref=outs[a].at[c], dst_ref=outs[a].at[c], send_sem=send.at[a], recv_sem=recv.at[a],
                device_id=(x, y, 1 - c), device_id_type=MESH)
            cp.start()
            cps.append(cp)
        waits = []
        for p in range(1, 8):
            px, py, pc = x ^ (p >> 2), y ^ ((p >> 1) & 1), c ^ (p & 1)
            cp = pltpu.make_async_remote_copy(
                src_ref=small_out.at[k, c], dst_ref=small_out.at[k, c], send_sem=ssend.at[p - 1],
                recv_sem=srecv.at[p - 1], device_id=(px, py, pc), device_id_type=MESH)
            cp.start()
            cps.append(cp)
            waits.append(pltpu.make_async_remote_copy(
                src_ref=small_out.at[2 * px + py, pc], dst_ref=small_out.at[2 * px + py, pc], send_sem=ssend.at[p - 1],
                recv_sem=srecv.at[p - 1], device_id=(px, py, pc), device_id_type=MESH))
        for a in range(n):
            pltpu.make_async_remote_copy(
                src_ref=outs[a].at[1 - c], dst_ref=outs[a].at[1 - c], send_sem=send.at[a], recv_sem=recv.at[a],
                device_id=(x, y, 1 - c), device_id_type=MESH).wait_recv()
        for w in waits:
            w.wait_recv()
        for cp in cps:
            cp.wait_send()

    return pl.pallas_call(
        body, name="rs_share_results",
        in_specs=_any_specs(n + 1), out_specs=_any_specs(n + 1),
        out_shape=[jax.ShapeDtypeStruct(b.shape, b.dtype) for b in bufs + [small]],
        scratch_shapes=[pltpu.SemaphoreType.DMA((n,)), pltpu.SemaphoreType.DMA((n,)),
                        pltpu.SemaphoreType.DMA((7,)), pltpu.SemaphoreType.DMA((7,))],
        input_output_aliases={a: a for a in range(n + 1)},
    )(*bufs, small)


def _row_tile(r, c):
    t = max(8, min(r, (1 << 18) // c))
    while r % t:
        t //= 2
    return t


def _add_devices(g, lands, sib, per_device):
    r, c = g.shape[-2:]
    t = _row_tile(r, c)

    def body(kc_ref, g_ref, l_ref, s_ref, f_ref):
        f = g_ref[...] + s_ref[...]
        for i in range(l_ref.shape[0]):
            f = f + l_ref[i].astype(F32)
        f_ref[...] = f

    if per_device:
        out_spec = pl.BlockSpec((None, None, t, c), lambda i, kc: (kc[0], kc[1], i, 0))
        out_shape = jax.ShapeDtypeStruct((N_CHIPS, 2, r, c), F32)
    else:
        out_spec = pl.BlockSpec((None, t, c), lambda i, kc: (kc[1], i, 0))
        out_shape = jax.ShapeDtypeStruct((2, r, c), F32)
    return pl.pallas_call(
        body, name="rs_add_devices",
        grid_spec=pltpu.PrefetchScalarGridSpec(
            num_scalar_prefetch=1, grid=(r // t,),
            in_specs=[pl.BlockSpec((None, None, t, c), lambda i, kc: (kc[1], kc[0], i, 0)),
                      pl.BlockSpec((lands.shape[0], t, c), lambda i, kc: (0, i, 0)),
                      pl.BlockSpec((t, c), lambda i, kc: (i, 0))],
            out_specs=out_spec),
        out_shape=out_shape,
        compiler_params=_params("parallel"),
    )(jnp.concatenate([_chip_index(), _core_index()]), g, lands, sib)


def _add_chips(q, b2):
    r, c = q.shape
    t = _row_tile(r, c)

    def body(c_ref, q_ref, b_ref, f_ref):
        f_ref[...] = ((q_ref[...] + b_ref[0].astype(F32)) + b_ref[1].astype(F32)) + b_ref[2].astype(F32)

    return pl.pallas_call(
        body, name="rs_add_chips",
        grid_spec=pltpu.PrefetchScalarGridSpec(
            num_scalar_prefetch=1, grid=(r // t,),
            in_specs=[pl.BlockSpec((t, c), lambda i, cr: (i, 0)), pl.BlockSpec((3, t, c), lambda i, cr: (0, i, 0))],
            out_specs=pl.BlockSpec((None, t, c), lambda i, cr: (cr[0], i, 0))),
        out_shape=jax.ShapeDtypeStruct((2, r, c), F32),
        compiler_params=_params("parallel"),
    )(_core_index(), q, b2)


def _adamw_math(w, g, m, v):
    m = ADAM_B1 * m + (1.0 - ADAM_B1) * g
    v = ADAM_B2 * v + (1.0 - ADAM_B2) * (g * g)
    m_hat = m / (1.0 - ADAM_B1 ** ADAM_STEP)
    v_hat = v / (1.0 - ADAM_B2 ** ADAM_STEP)
    delta = -ADAM_LR * (m_hat / (jnp.sqrt(v_hat) + ADAM_EPS) + ADAM_WD * w)
    return delta, m, v


def _adamw(w, g, m, v):
    r, c = w.shape
    t = _row_tile(r, c) if r % 8 == 0 else r

    def body(w_ref, g_ref, m_ref, v_ref, d_ref, nm_ref, nv_ref):
        d_ref[...], nm_ref[...], nv_ref[...] = _adamw_math(w_ref[...], g_ref[...], m_ref[...], v_ref[...])

    spec = pl.BlockSpec((t, c), lambda i: (i, 0))
    return pl.pallas_call(
        body, name="adamw", grid=(r // t,), in_specs=[spec] * 4, out_specs=[spec] * 3,
        out_shape=[jax.ShapeDtypeStruct((r, c), F32)] * 3, compiler_params=_params("parallel"),
    )(w, g, m, v)


ROW_B_IN = 0
ROW_VECS = 8
ROW_LOSS = 16
ROW_B_SPATIAL = 24
ROW_W_SPATIAL = 32
SMALL_ROWS = 192
N_VECS = 8


def _pack_small(acc_f, acc_b, dbin_a, dbin_b, dcw8, dws, dbsp):
    cols = D // N_CHIPS

    def body(af_ref, ab_ref, da_ref, db_ref, cw_ref, ws_ref, bs_ref, o_ref, gc_ref):
        o_ref[...] = jnp.zeros_like(o_ref)
        for j in range(D_IN // D):
            src = da_ref if j < 2 else db_ref
            o_ref[ROW_B_IN + j:ROW_B_IN + j + 1, :] = src[0:1, j * D:(j + 1) * D]
        dcw = jnp.sum(cw_ref[...].reshape(HALO, SUBLANES, D), axis=1)
        o_ref[ROW_VECS:ROW_VECS + 1, :] = dcw[CONV_K:CONV_K + 1]
        o_ref[ROW_VECS + 1:ROW_VECS + 5, :] = ab_ref[0:4, :]
        o_ref[ROW_VECS + 5:ROW_VECS + 6, :] = af_ref[2:3, :]
        o_ref[ROW_VECS + 6:ROW_VECS + 8, :] = af_ref[0:2, :]
        o_ref[ROW_LOSS:ROW_LOSS + 1, :] = af_ref[3:4, :]
        head = lax.broadcasted_iota(jnp.int32, (N_GROUPS, D), 0)
        lane = lax.broadcasted_iota(jnp.int32, (N_GROUPS, D), 1)
        indicator = jnp.where(lane // GROUP_W == head, 1.0, 0.0)
        o_ref[ROW_B_SPATIAL:ROW_B_SPATIAL + N_GROUPS, 0:CHUNK] = lax.dot_general(
            indicator, bs_ref[...], (((1,), (1,)), ((), ())), precision=lax.Precision.HIGHEST, preferred_element_type=F32)
        t_idx = lax.broadcasted_iota(jnp.int32, (CHUNK, D), 0)
        s_idx = lax.broadcasted_iota(jnp.int32, (CHUNK, D), 1) % CHUNK
        o_ref[ROW_W_SPATIAL:ROW_W_SPATIAL + CHUNK, :] = jnp.where(s_idx <= t_idx, ws_ref[...], 0.0)
        for h in range(2):
            for j in range(N_CHIPS):
                gc_ref[h, j] = dcw[h * (HALO // 2):(h + 1) * (HALO // 2), j * cols:(j + 1) * cols]

    ins = [acc_f, acc_b, dbin_a, dbin_b, dcw8, dws, dbsp]
    return pl.pallas_call(
        body, name="pack_small",
        in_specs=[_full(a.shape) for a in ins],
        out_specs=[_full((SMALL_ROWS, D)), _full((2, N_CHIPS, HALO // 2, cols))],
        out_shape=[jax.ShapeDtypeStruct((SMALL_ROWS, D), F32), jax.ShapeDtypeStruct((2, N_CHIPS, HALO // 2, cols), F32)],
        compiler_params=_params(),
    )(*ins)


def _adamw_small(parts, vecs, b_in, w_spatial, b_spatial):
    triples = list(vecs) + [b_in, w_spatial, b_spatial]
    n_in = 3 * len(triples)

    def body(p_ref, *refs):
        ins = [refs[3 * i:3 * i + 3] for i in range(len(triples))]
        outs = [refs[n_in + 4 * i:n_in + 4 * i + 4] for i in range(len(triples))]
        loss_ref, g_ref = refs[n_in + 4 * len(triples):]
        rows = SMALL_ROWS // 8
        for k in range(N_CHIPS):
            for core in range(2):
                g_ref[(core * N_CHIPS + k) * rows:(core * N_CHIPS + k + 1) * rows, :] = p_ref[2 * k + core]

        def step(g, wmv, out, get, put):
            d, nm, nv = _adamw_math(get(wmv[0]), g, get(wmv[1]), get(wmv[2]))
            for o, val in zip(out, (g, d, nm, nv)):
                put(o, val)

        for i in range(N_VECS):
            step(g_ref[ROW_VECS + i:ROW_VECS + i + 1, :], ins[i], outs[i],
                 lambda r: r[...].reshape(1, D), lambda o, val: o.__setitem__(Ellipsis, val.reshape(D)))
        for j in range(D_IN // D):
            piece = pl.ds(j * D, D)
            step(g_ref[ROW_B_IN + j:ROW_B_IN + j + 1, :], ins[N_VECS], outs[N_VECS],
                 lambda r: r[piece].reshape(1, D), lambda o, val: o.__setitem__(piece, val.reshape(D)))
        for h in range(N_GROUPS):
            step(g_ref[ROW_W_SPATIAL:ROW_W_SPATIAL + CHUNK, h * CHUNK:(h + 1) * CHUNK], ins[N_VECS + 1], outs[N_VECS + 1],
                 lambda r: r[h], lambda o, val: o.__setitem__(h, val))
        step(g_ref[ROW_B_SPATIAL:ROW_B_SPATIAL + N_GROUPS, 0:CHUNK], ins[N_VECS + 2], outs[N_VECS + 2],
             lambda r: r[...], lambda o, val: o.__setitem__(Ellipsis, val))
        lanes = g_ref[ROW_LOSS:ROW_LOSS + 1, :]
        loss_ref[...] = jnp.broadcast_to(jnp.sum(lanes, axis=1, keepdims=True), (8, 128))

    flat = [a for tr in triples for a in tr]
    out_shape = [jax.ShapeDtypeStruct(tr[0].shape, F32) for tr in triples for _ in range(4)]
    out_shape.append(jax.ShapeDtypeStruct((8, 128), F32))
    res = pl.pallas_call(
        body, name="adamw_small",
        in_specs=[_full(parts.shape)] + [_full(a.shape) for a in flat],
        out_specs=[_full(o.shape) for o in out_shape],
        out_shape=out_shape,
        scratch_shapes=[pltpu.VMEM((SMALL_ROWS, D), F32)],
        compiler_params=_params(),
    )(parts, *flat)
    return [res[4 * i:4 * i + 4] for i in range(len(triples))], res[-1]


SUBLANES = 8
SHIFT_ROWS = HALO - SUBLANES


def _shifted_copies(src_ref, sh_ref, cs, tm):
    for p in range(1, SUBLANES):
        sh_ref[p - 1] = src_ref[pl.ds(p, tm + SHIFT_ROWS), cs]


def _tap(src_ref, sh_ref, cs, offset, start, rows):
    p, q = offset % SUBLANES, offset // SUBLANES
    if p == 0:
        return src_ref[pl.ds(start + SUBLANES * q, rows), cs]
    return sh_ref[p - 1, pl.ds(start + SUBLANES * q, rows), :]


def _conv_taps(src_ref, sh_ref, w_ref, first_offset, step, bias, dst_ref, tm):
    rows = 64
    for g in range(N_GROUPS):
        cs = slice(g * GROUP_W, (g + 1) * GROUP_W)
        _shifted_copies(src_ref, sh_ref, cs, tm)
        for rb in range(tm // rows):
            acc = jnp.zeros((rows, GROUP_W), F32) + (bias[:, cs] if bias is not None else 0.0)
            for k in range(CONV_K):
                acc = acc + w_ref[k:k + 1, cs] * _tap(src_ref, sh_ref, cs, first_offset + step * k, rb * rows, rows)
            dst_ref[rb * rows:(rb + 1) * rows, cs] = acc


def _conv_weight_grad(d_ref, src_ref, sh_ref, first_offset, acc_ref, tm):
    rows = 64
    for g in range(N_GROUPS):
        cs = slice(g * GROUP_W, (g + 1) * GROUP_W)
        _shifted_copies(src_ref, sh_ref, cs, tm)
        for rb in range(tm // rows):
            d = d_ref[rb * rows:(rb + 1) * rows, cs]
            for k in range(CONV_K):
                prod = d * _tap(src_ref, sh_ref, cs, first_offset + k, rb * rows, rows)
                acc_ref[SUBLANES * k:SUBLANES * (k + 1), cs] += jnp.sum(
                    prod.reshape(rows // SUBLANES, SUBLANES, GROUP_W), axis=0)


def _spatial_mix(w_ref, v_bf, tm):
    rows = []
    for q in range(tm // CHUNK):
        cols = [_dot(w_ref[h], v_bf[q * CHUNK:(q + 1) * CHUNK, h * GROUP_W:(h + 1) * GROUP_W])
                for h in range(N_GROUPS)]
        rows.append(jnp.concatenate(cols, axis=1))
    return jnp.concatenate(rows, axis=0)


def _group_norm_fwd(h1, gn_g, gn_b):
    xhat, rstd = [], []
    for g in range(N_GROUPS):
        xh, rs = _norm_stats(h1[:, g * GROUP_W:(g + 1) * GROUP_W])
        xhat.append(xh)
        rstd.append(rs)
    xhat = jnp.concatenate(xhat, axis=1)
    return xhat * gn_g + gn_b, xhat, rstd


def _forward_tiles(p, x, tgt, wpa, wpb, wo, convw, vecs, ws, bsp, tiles_per_seq):
    t = x.shape[0]
    tm = TOKEN_TILE
    hb = tm // HALO

    def body(p_ref, ph_ref, x_ref, t_ref, wpa_ref, wpb_ref, wo_ref, cw_ref, vec_ref, ws_ref, bsp_ref,
             h1_ref, ya_ref, yb_ref, h3_ref, s_ref, mx_ref, dr_ref, drb_ref, xt_ref, acc_ref, he_ref, sh_ref):
        i = pl.program_id(0)
        xt_ref[...] = x_ref[...].T.astype(BF16)
        conv_b, gn_g, gn_b, lnv_g, lnv_b, b_o, lno_g, lno_b = [vec_ref[j:j + 1, :] for j in range(8)]

        keep = jnp.where(i % tiles_per_seq == 0, 0.0, 1.0)
        he_ref[0:HALO, :] = ph_ref[:, 0:D] * _sigmoid(ph_ref[:, D:2 * D]) * keep
        he_ref[HALO:, :] = p_ref[:, 0:D] * _sigmoid(p_ref[:, D:2 * D])
        _conv_taps(he_ref, sh_ref, cw_ref, HALO - (CONV_K - 1), 1, conv_b, h1_ref, tm)
        h2, _, _ = _group_norm_fwd(h1_ref[...], gn_g, gn_b)
        a_gate = p_ref[:, 2 * D:3 * D]
        h3 = ((h2 * _sigmoid(h2)) * (a_gate * _sigmoid(a_gate))).astype(BF16)
        h3_ref[...] = h3
        ya = _dot(h3, wpa_ref[...])
        ya_ref[...] = ya

        u = _gelu(p_ref[:, 3 * D:4 * D])
        vhat, _ = _norm_stats(_gelu(p_ref[:, 4 * D:5 * D]))
        v1 = (vhat * lnv_g + lnv_b).astype(BF16)
        b_gate = p_ref[:, 5 * D:6 * D]
        vmix = _spatial_mix(ws_ref, v1, tm) + jnp.concatenate([bsp_ref[...]] * (tm // CHUNK), axis=0)
        s = (u * vmix * (b_gate * _sigmoid(b_gate))).astype(BF16)
        s_ref[...] = s
        yb = _dot(s, wpb_ref[...])
        yb_ref[...] = yb

        mixed = (_sigmoid(p_ref[:, 6 * D:7 * D]) * ya + _sigmoid(p_ref[:, 7 * D:8 * D]) * yb).astype(BF16)
        mx_ref[...] = mixed
        r = ALPHA * x_ref[...] + (_dot(mixed, wo_ref[...]) + b_o)
        xhat, rstd = _norm_stats(r)
        err = (xhat * lno_g + lno_b) - t_ref[...]
        dout = err * (1.0 / D)
        dr = _norm_bwd(dout * lno_g, xhat, rstd)
        dr_ref[...] = dr
        drb_ref[...] = dr.astype(BF16)

        @pl.when(i == 0)
        def _():
            acc_ref[...] = jnp.zeros_like(acc_ref)

        acc_ref[0:1, :] += _colsum(dout * xhat)
        acc_ref[1:2, :] += _colsum(dout)
        acc_ref[2:3, :] += _colsum(dr)
        acc_ref[3:4, :] += _colsum(err * err) * (0.5 / D)

    tile = lambda w: pl.BlockSpec((tm, w), lambda i: (i, 0))
    f32_out = jax.ShapeDtypeStruct((t, D), F32)
    bf_out = jax.ShapeDtypeStruct((t, D), BF16)
    return pl.pallas_call(
        body, name="forward_tiles", grid=(t // tm,),
        in_specs=[tile(D_IN),
                  pl.BlockSpec((HALO, 2 * D), lambda i: (jnp.maximum(i * hb - 1, 0), 0)),
                  tile(D), tile(D), _resident((D, D)), _resident((D, D)), _resident((D, D)), _full((HALO, D)), _full((8, D)),
                  _full((N_GROUPS, CHUNK, CHUNK)), _full((CHUNK, D))],
        out_specs=[tile(D)] * 8 + [pl.BlockSpec((D, tm), lambda i: (0, i)), _full((8, D))],
        out_shape=[f32_out, f32_out, f32_out, bf_out, bf_out, bf_out, f32_out, bf_out,
                   jax.ShapeDtypeStruct((D, t), BF16), jax.ShapeDtypeStruct((8, D), F32)],
        scratch_shapes=[pltpu.VMEM((tm + HALO, D), F32), pltpu.VMEM((SUBLANES - 1, tm + SHIFT_ROWS, GROUP_W), F32)],
        compiler_params=_params("arbitrary"),
    )(p, p, x, tgt, wpa, wpb, wo, convw, vecs, ws, bsp)


def _backward_tiles(p, h1, ya, yb, drb, wpa, wpb, wo, vecs, ws, wst, bsp):
    t = h1.shape[0]
    tm = TOKEN_TILE

    def body(p_ref, h1_ref, ya_ref, yb_ref, drb_ref, wpa_ref, wpb_ref, wo_ref, vec_ref, ws_ref, wst_ref, bsp_ref,
             dh1_ref, dp_ref, dya_ref, dyb_ref, acc_ref, dbin_ref, dws_ref, dbsp_ref):
        i = pl.program_id(0)
        _, gn_g, gn_b, lnv_g, lnv_b = [vec_ref[j:j + 1, :] for j in range(5)]

        @pl.when(i == 0)
        def _():
            acc_ref[...] = jnp.zeros_like(acc_ref)
            dbin_ref[...] = jnp.zeros_like(dbin_ref)
            dws_ref[...] = jnp.zeros_like(dws_ref)
            dbsp_ref[...] = jnp.zeros_like(dbsp_ref)

        def emit(block, val):
            dbin_ref[0:1, block * D:(block + 1) * D] += _colsum(val)
            dp_ref[:, block * D:(block + 1) * D] = val.astype(BF16)

        dp_ref[:, 0:2 * D] = jnp.zeros((tm, 2 * D), BF16)
        dmixed = _dot_nt(drb_ref[...], wo_ref[...])
        ga = _sigmoid(p_ref[:, 6 * D:7 * D])
        gb = _sigmoid(p_ref[:, 7 * D:8 * D])
        dya = (dmixed * ga).astype(BF16)
        dyb = (dmixed * gb).astype(BF16)
        dya_ref[...] = dya
        dyb_ref[...] = dyb
        emit(6, dmixed * ya_ref[...] * (ga * (1.0 - ga)))
        emit(7, dmixed * yb_ref[...] * (gb * (1.0 - gb)))

        dh3 = _dot_nt(dya, wpa_ref[...])
        h2, xhat, rstd = _group_norm_fwd(h1_ref[...], gn_g, gn_b)
        sg = _sigmoid(h2)
        a_gate = p_ref[:, 2 * D:3 * D]
        sa = _sigmoid(a_gate)
        dh2 = dh3 * (a_gate * sa) * (sg * (1.0 + h2 * (1.0 - sg)))
        emit(2, dh3 * (h2 * sg) * (sa * (1.0 + a_gate * (1.0 - sa))))
        acc_ref[0:1, :] += _colsum(dh2 * xhat)
        acc_ref[1:2, :] += _colsum(dh2)
        dxhat = dh2 * gn_g
        for g in range(N_GROUPS):
            cs = slice(g * GROUP_W, (g + 1) * GROUP_W)
            dh1_ref[:, cs] = _norm_bwd(dxhat[:, cs], xhat[:, cs], rstd[g])

        ds = _dot_nt(dyb, wpb_ref[...])
        u_pre = p_ref[:, 3 * D:4 * D]
        u, du_dpre = _gelu_and_grad(u_pre)
        v0, dv_dpre = _gelu_and_grad(p_ref[:, 4 * D:5 * D])
        vhat, vrstd = _norm_stats(v0)
        v1 = (vhat * lnv_g + lnv_b).astype(BF16)
        vmix = _spatial_mix(ws_ref, v1, tm) + jnp.concatenate([bsp_ref[...]] * (tm // CHUNK), axis=0)
        b_gate = p_ref[:, 5 * D:6 * D]
        sb = _sigmoid(b_gate)
        silu_b = b_gate * sb
        emit(3, ds * vmix * silu_b * du_dpre)
        emit(5, ds * u * vmix * (sb * (1.0 + b_gate * (1.0 - sb))))
        dvmix = ds * u * silu_b
        dvmix_bf = dvmix.astype(BF16)
        for q in range(tm // CHUNK):
            dbsp_ref[...] += dvmix[q * CHUNK:(q + 1) * CHUNK, :]
            for h in range(N_GROUPS):
                blk = (slice(q * CHUNK, (q + 1) * CHUNK), slice(h * GROUP_W, (h + 1) * GROUP_W))
                dws_ref[:, h * GROUP_W:(h + 1) * GROUP_W] += _dot_nt(dvmix_bf[blk], v1[blk])
        dv1 = _spatial_mix(wst_ref, dvmix_bf, tm)
        acc_ref[2:3, :] += _colsum(dv1 * vhat)
        acc_ref[3:4, :] += _colsum(dv1)
        emit(4, _norm_bwd(dv1 * lnv_g, vhat, vrstd) * dv_dpre)

    tile = lambda w: pl.BlockSpec((tm, w), lambda i: (i, 0))
    return pl.pallas_call(
        body, name="backward_tiles", grid=(t // tm,),
        in_specs=[tile(D_IN), tile(D), tile(D), tile(D), tile(D), _resident((D, D)), _resident((D, D)), _resident((D, D)),
                  _full((8, D)), _full((N_GROUPS, CHUNK, CHUNK)), _full((N_GROUPS, CHUNK, CHUNK)), _full((CHUNK, D))],
        out_specs=[tile(D), tile(D_IN), tile(D), tile(D), _full((8, D)), _full((8, D_IN)),
                   _full((CHUNK, D)), _full((CHUNK, D))],
        out_shape=[jax.ShapeDtypeStruct((t, D), F32), jax.ShapeDtypeStruct((t, D_IN), BF16),
                   jax.ShapeDtypeStruct((t, D), BF16), jax.ShapeDtypeStruct((t, D), BF16),
                   jax.ShapeDtypeStruct((8, D), F32), jax.ShapeDtypeStruct((8, D_IN), F32),
                   jax.ShapeDtypeStruct((CHUNK, D), F32), jax.ShapeDtypeStruct((CHUNK, D), F32)],
        compiler_params=_params("arbitrary"),
    )(p, h1, ya, yb, drb, wpa, wpb, wo, vecs, ws, wst, bsp)


def _conv_backward(dh1, p, dp, convw, pairs, tiles_per_seq):
    t = dh1.shape[0]
    tm = TOKEN_TILE
    hb = tm // HALO
    last = t // HALO - 1
    n_sq = len(pairs)
    span = 2
    rows = D // 8

    def body(dh1_ref, dnext_ref, p_ref, ph_ref, cw_ref, dp_in_ref, *refs):
        del dp_in_ref
        sq_in = refs[:2 * n_sq]
        dp_ref, dcw_ref, dbin_ref = refs[2 * n_sq:2 * n_sq + 3]
        sq_out = refs[2 * n_sq + 3:3 * n_sq + 3]
        sq_wire = refs[3 * n_sq + 3:4 * n_sq + 3]
        de_ref, he_ref, dh0_ref, sh_ref, acc_ref, wire_ref, sq_sem, wire_sem = refs[4 * n_sq + 3:]
        i = pl.program_id(0)

        @pl.when(i == 0)
        def _():
            dcw_ref[...] = jnp.zeros_like(dcw_ref)
            dbin_ref[...] = jnp.zeros_like(dbin_ref)
            acc_ref[...] = jnp.zeros_like(acc_ref)

        @pl.when(i % span == span - 1)
        def _():
            for a in range(n_sq):
                acc_ref[a] += _dot_tn(sq_in[2 * a][...], sq_in[2 * a + 1][...])

        keep_next = jnp.where(i % tiles_per_seq == tiles_per_seq - 1, 0.0, 1.0)
        de_ref[0:tm, :] = dh1_ref[...]
        de_ref[tm:, :] = dnext_ref[...] * keep_next
        _conv_taps(de_ref, sh_ref, cw_ref, CONV_K - 1, -1, None, dh0_ref, tm)

        keep_prev = jnp.where(i % tiles_per_seq == 0, 0.0, 1.0)
        sg = _sigmoid(p_ref[:, D:2 * D])
        val = p_ref[:, 0:D]
        he_ref[0:HALO, :] = ph_ref[:, 0:D] * _sigmoid(ph_ref[:, D:2 * D]) * keep_prev
        he_ref[HALO:, :] = val * sg
        _conv_weight_grad(dh1_ref, he_ref, sh_ref, HALO - (CONV_K - 1), dcw_ref, tm)
        dcw_ref[SUBLANES * CONV_K:, :] += jnp.sum(dh1_ref[...].reshape(tm // SUBLANES, SUBLANES, D), axis=0)

        dh0 = dh0_ref[...]
        dval = dh0 * sg
        dglu = dh0 * val * (sg * (1.0 - sg))
        dbin_ref[0:1, 0:D] += _colsum(dval)
        dbin_ref[0:1, D:2 * D] += _colsum(dglu)
        dp_ref[:, 0:D] = dval.astype(BF16)
        dp_ref[:, D:2 * D] = dglu.astype(BF16)

        @pl.when(i == t // tm - 1)
        def _():
            cps = [pltpu.make_async_copy(acc_ref.at[a, pl.ds((2 * j + h) * rows, rows)], sq_out[a].at[h, j],
                                         sq_sem.at[(a * N_CHIPS + j) * 2 + h])
                   for a in range(n_sq) for j in range(N_CHIPS) for h in range(2)]
            for cp in cps:
                cp.start()
            for a in range(n_sq):
                wire_ref[...] = acc_ref[a].astype(BF16)
                narrow = [pltpu.make_async_copy(wire_ref.at[pl.ds((2 * j + h) * rows, rows)], sq_wire[a].at[h, j],
                                                wire_sem.at[2 * j + h]) for j in range(N_CHIPS) for h in range(2)]
                for cp in narrow:
                    cp.start()
                for cp in narrow:
                    cp.wait()
            for cp in cps:
                cp.wait()

    any_spec = pl.BlockSpec(memory_space=pl.ANY)
    wide = pl.BlockSpec((span * tm, D), lambda i: (i // span, 0))
    return pl.pallas_call(
        body, name="conv_backward", grid=(t // tm,),
        in_specs=[pl.BlockSpec((tm, D), lambda i: (i, 0)),
                  pl.BlockSpec((HALO, D), lambda i: (jnp.minimum((i + 1) * hb, last), 0)),
                  pl.BlockSpec((tm, 2 * D), lambda i: (i, 0)),
                  pl.BlockSpec((HALO, 2 * D), lambda i: (jnp.maximum(i * hb - 1, 0), 0)),
                  _full((HALO, D)), any_spec] + [wide] * (2 * n_sq),
        out_specs=[pl.BlockSpec((tm, 2 * D), lambda i: (i, 0)), _full((SUBLANES * HALO, D)), _full((8, 2 * D))]
        + [any_spec] * (2 * n_sq),
        out_shape=[jax.ShapeDtypeStruct(dp.shape, BF16), jax.ShapeDtypeStruct((SUBLANES * HALO, D), F32),
                   jax.ShapeDtypeStruct((8, 2 * D), F32)]
        + [jax.ShapeDtypeStruct((2, N_CHIPS, rows, D), F32)] * n_sq
        + [jax.ShapeDtypeStruct((2, N_CHIPS, rows, D), BF16)] * n_sq,
        scratch_shapes=[pltpu.VMEM((tm + HALO, D), F32), pltpu.VMEM((tm + HALO, D), F32), pltpu.VMEM((tm, D), F32),
                        pltpu.VMEM((SUBLANES - 1, tm + SHIFT_ROWS, GROUP_W), F32), pltpu.VMEM((n_sq, D, D), F32),
                        pltpu.VMEM((D, D), BF16), pltpu.SemaphoreType.DMA((n_sq * N_CHIPS * 2,)),
                        pltpu.SemaphoreType.DMA((N_CHIPS * 2,))],
        input_output_aliases={5: 0},
        compiler_params=_params("arbitrary"),
    )(dh1, dh1, p, p, convw, dp, *[a for pair in pairs for a in pair])


def _grad_in_and_x(xt, dp, w4, dr, wires, grads):
    t = dr.shape[0]
    tm = TOKEN_TILE
    half, tn = D // 2, 512
    nb = W_BLOCK // tn
    n_w, n_x = 2 * N_CHIPS * nb, t // tm
    ns = len(grads)
    xi, yi, ci = lax.axis_index("x"), lax.axis_index("y"), lax.axis_index("c")
    others = [2 * (1 - xi) + yi, 2 * xi + (1 - yi), 2 * (1 - xi) + (1 - yi)]
    blocks = others + others + [2 * xi + yi] * 2
    halves = [1 - ci] * 3 + [ci] * 3 + [1 - ci, ci]
    table = jnp.stack([jnp.stack([b * nb + n for b in blocks for n in range(nb)]),
                       jnp.stack([h for h in halves for _ in range(nb)])]).astype(jnp.int32)

    def body(tab_ref, xt_ref, dpc_ref, dpr_ref, w_ref, dr_ref, *refs):
        parts, fulls = refs[:ns], refs[ns:2 * ns]
        dx_ref, qk_ref, b2_ref, b1_ref, wire_ref = refs[2 * ns:2 * ns + 5]
        lands, sibs = refs[2 * ns + 5:3 * ns + 5], refs[3 * ns + 5:4 * ns + 5]
        (g_ref, st_ref, sb_ref, tmp_ref, d2d_send, d2d_recv, ici_send, ici_recv, own_sem, tmp_sem, wire_sem,
         p_send, p_recv, s_send, s_recv) = refs[4 * ns + 5:]
        s = pl.program_id(0)
        x_, y_, c, k = _position()
        chips = _other_chips(x_, y_)
        n = s % nb
        grp = s // nb
        cols = pl.ds(pl.multiple_of(n * tn, tn), tn)

        def part(a, r, core):
            cx, cy = chips[r]
            return pltpu.make_async_remote_copy(
                src_ref=parts[a].at[core, 2 * cx + cy], dst_ref=lands[a].at[2 * r + c],
                send_sem=p_send.at[6 * a + 2 * r + core], recv_sem=p_recv.at[6 * a + 2 * r + c],
                device_id=(cx, cy, core), device_id_type=MESH)

        def landed(a, r, core):
            cx, cy = chips[r]
            return pltpu.make_async_remote_copy(
                src_ref=lands[a].at[2 * r + core], dst_ref=lands[a].at[2 * r + core],
                send_sem=p_send.at[6 * a + 2 * r + core], recv_sem=p_recv.at[6 * a + 2 * r + core],
                device_id=(cx, cy, core), device_id_type=MESH)

        def to_sibling_whole(a):
            return pltpu.make_async_remote_copy(
                src_ref=fulls[a].at[1 - c, k], dst_ref=sibs[a], send_sem=s_send.at[a], recv_sem=s_recv.at[a],
                device_id=(x_, y_, 1 - c), device_id_type=MESH)

        def to_sibling(slot, land):
            return pltpu.make_async_remote_copy(
                src_ref=st_ref.at[slot], dst_ref=b1_ref.at[land, :, cols], send_sem=d2d_send.at[slot],
                recv_sem=d2d_recv.at[land * nb + n], device_id=(x_, y_, 1 - c), device_id_type=MESH)

        def to_chip(r):
            cx, cy = chips[r]
            return pltpu.make_async_remote_copy(
                src_ref=wire_ref.at[r, :, cols], dst_ref=b2_ref.at[r, :, cols], send_sem=ici_send.at[r],
                recv_sem=ici_recv.at[r], device_id=(cx, cy, c), device_id_type=MESH)

        def all_of_chip(r):
            cx, cy = chips[r]
            return pltpu.make_async_remote_copy(
                src_ref=wire_ref.at[r], dst_ref=b2_ref.at[r], send_sem=ici_send.at[r],
                recv_sem=ici_recv.at[r], device_id=(cx, cy, c), device_id_type=MESH)

        def to_result(slot):
            return pltpu.make_async_copy(st_ref.at[slot], qk_ref.at[:, cols], own_sem.at[slot])

        def sibling_piece(land):
            return pltpu.make_async_copy(b1_ref.at[land, :, cols], tmp_ref, tmp_sem)

        @pl.when(s == 0)
        def _():
            for a in range(ns):
                to_sibling_whole(a).start()
                for r in range(3):
                    for core in range(2):
                        part(a, r, core).start()

        own_half = ((grp >= 3) & (grp <= 5)) | (grp == 7)
        land = jnp.where(grp == 7, 3, grp - 3)

        @pl.when(own_half)
        def _():
            to_sibling(0, land).wait_recv()
            sibling_piece(land).start()

        @pl.when(s < n_w)
        def _():
            g_ref[...] = _dot(xt_ref[tab_ref[1, s]], dpc_ref[...])

        @pl.when(own_half)
        def _():
            sibling_piece(land).wait()

        for g in range(2 * N_CHIPS):
            @pl.when(grp == g)
            def _(g=g):
                if g in (0, 1, 2, 6):
                    use = s if g < 3 else 3 * nb + n
                    slot = use % 2

                    @pl.when(use >= 2)
                    def _():
                        to_sibling(slot, 0).wait_send()

                    st_ref[slot] = g_ref[...]
                    to_sibling(slot, min(g, 3)).start()
                elif g in (3, 4, 5):
                    sb_ref[...] = (g_ref[...] + tmp_ref[...]).astype(BF16)
                    stage = pltpu.make_async_copy(sb_ref, wire_ref.at[g - 3, :, cols], wire_sem)
                    stage.start()
                    stage.wait()
                    to_chip(g - 3).start()
                else:
                    slot = n % 2
                    piece = g_ref[...] + tmp_ref[...]

                    @pl.when(n < 2)
                    def _():
                        to_sibling(slot, 0).wait_send()

                    @pl.when(n >= 2)
                    def _():
                        to_result(slot).wait()

                    st_ref[slot] = piece
                    to_result(slot).start()

        @pl.when(s >= n_w)
        def _():
            acc = ALPHA * dr_ref[...]
            for j in range(N_CHIPS):
                acc = acc + _dot_nt(dpr_ref[:, j * W_BLOCK:(j + 1) * W_BLOCK], w_ref[j])
            dx_ref[...] = acc

        @pl.when(s == n_w + n_x - 1)
        def _():
            for slot in range(2):
                to_result(slot).wait()
            for r in range(3):
                all_of_chip(r).wait_recv()
                all_of_chip(r).wait_send()
            for a in range(ns):
                to_sibling_whole(a).wait_recv()
                to_sibling_whole(a).wait_send()
                for r in range(3):
                    for core in range(2):
                        landed(a, r, core).wait_recv()
                        part(a, r, core).wait_send()

    any_spec = pl.BlockSpec(memory_space=pl.ANY)
    tile = lambda s, tab: (jnp.maximum(s - n_w, 0), 0)
    return pl.pallas_call(
        body, name="grad_in_and_x",
        grid_spec=pltpu.PrefetchScalarGridSpec(
            num_scalar_prefetch=1, grid=(n_w + n_x,),
            in_specs=[pl.BlockSpec((2, half, t), lambda s, tab: (0, 0, 0), pipeline_mode=pl.Buffered(1)),
                      pl.BlockSpec((t, tn), lambda s, tab: (0, tab[0, jnp.minimum(s, n_w - 1)])),
                      pl.BlockSpec((tm, D_IN), tile),
                      pl.BlockSpec((N_CHIPS, D, W_BLOCK), lambda s, tab: (0, 0, 0), pipeline_mode=pl.Buffered(1)),
                      pl.BlockSpec((tm, D), tile)] + [any_spec] * (2 * ns),
            out_specs=[pl.BlockSpec((tm, D), tile)] + [any_spec] * (4 + 2 * ns),
            scratch_shapes=[pltpu.VMEM((half, tn), F32), pltpu.VMEM((2, half, tn), F32), pltpu.VMEM((half, tn), BF16),
                            pltpu.VMEM((half, tn), F32),
                            pltpu.SemaphoreType.DMA((2,)), pltpu.SemaphoreType.DMA((N_CHIPS * nb,)),
                            pltpu.SemaphoreType.DMA((3,)), pltpu.SemaphoreType.DMA((3,)),
                            pltpu.SemaphoreType.DMA((2,)), pltpu.SemaphoreType.DMA, pltpu.SemaphoreType.DMA,
                            pltpu.SemaphoreType.DMA((6 * ns,)), pltpu.SemaphoreType.DMA((6 * ns,)),
                            pltpu.SemaphoreType.DMA((ns,)), pltpu.SemaphoreType.DMA((ns,))]),
        out_shape=[jax.ShapeDtypeStruct((t, D), F32), jax.ShapeDtypeStruct((half, W_BLOCK), F32),
                   jax.ShapeDtypeStruct((3, half, W_BLOCK), BF16), jax.ShapeDtypeStruct((N_CHIPS, half, W_BLOCK), F32),
                   jax.ShapeDtypeStruct((3, half, W_BLOCK), BF16)]
        + [jax.ShapeDtypeStruct((6,) + w.shape[2:], w.dtype) for w in wires]
        + [jax.ShapeDtypeStruct(g.shape[2:], F32) for g in grads],
        compiler_params=_params("arbitrary"),
    )(table, xt, dp, dp, w4, dr, *wires, *grads)


def kernel(x, w_in, b_in, conv_w, conv_b, gn_g, gn_b, ln_v_g, ln_v_b, w_spatial, b_spatial, w_pa, w_pb, w_o, b_o, ln_out_g, ln_out_b, loss_target, m_w_in, m_b_in, m_conv_w, m_conv_b, m_gn_g, m_gn_b, m_ln_v_g, m_ln_v_b, m_w_spatial, m_b_spatial, m_w_pa, m_w_pb, m_w_o, m_b_o, m_ln_out_g, m_ln_out_b, v_w_in, v_b_in, v_conv_w, v_conv_b, v_gn_g, v_gn_b, v_ln_v_g, v_ln_v_b, v_w_spatial, v_b_spatial, v_w_pa, v_w_pb, v_w_o, v_b_o, v_ln_out_g, v_ln_out_b):
    n_seq, seq, _ = x.shape
    t = n_seq * seq
    tiles_per_seq = seq // TOKEN_TILE
    x2 = x.reshape(t, D)
    tgt = loss_target.reshape(t, D)

    conv_shard = jnp.pad(conv_w, ((0, HALO - CONV_K), (0, 0)))
    p, win4, wpa4, wpb4, wo4, conv4 = _proj_gather(
        x2, b_in,
        [_place_shard(w_in, 256, BF16), _place_shard(w_pa, 128, BF16), _place_shard(w_pb, 128, BF16),
         _place_shard(w_o, 128, BF16), _place_shard(conv_shard, HALO // 2, F32)])
    win4 = win4.reshape(N_CHIPS, D, W_BLOCK)
    wpa, wpb, wo = wpa4.reshape(D, D), wpb4.reshape(D, D), wo4.reshape(D, D)
    convw = conv4.reshape(N_CHIPS, HALO, D // N_CHIPS).transpose(1, 0, 2).reshape(HALO, D)

    vecs = jnp.stack([conv_b, gn_g, gn_b, ln_v_g, ln_v_b, b_o, ln_out_g, ln_out_b])
    causal = jnp.tril(jnp.ones((CHUNK, CHUNK), bool))
    ws = jnp.where(causal[None], w_spatial, 0.0)
    ws_bf, wst_bf = ws.astype(BF16), ws.transpose(0, 2, 1).astype(BF16)
    bsp = jnp.repeat(b_spatial.T, GROUP_W, axis=1)

    h1, ya, yb, h3, s, mixed, dr, drb, xt, acc_f = _forward_tiles(p, x2, tgt, wpa, wpb, wo, convw, vecs, ws_bf, bsp, tiles_per_seq)
    dh1, dp, dya, dyb, acc_b, dbin_b, dws, dbsp_acc = _backward_tiles(p, h1, ya, yb, drb, wpa, wpb, wo, vecs, ws_bf, wst_bf, bsp)
    dp, dcw8, dbin_a, *square = _conv_backward(dh1, p, dp, convw, [(h3, dya), (s, dyb), (mixed, drb)], tiles_per_seq)

    small, g_conv = _pack_small(acc_f, acc_b, dbin_a, dbin_b, dcw8, dws, dbsp_acc)
    small = small.reshape(2, N_CHIPS, SMALL_ROWS // 8, D)

    grads = square[:3] + [g_conv, small]
    wires = square[3:] + [g_conv, small]
    grad_x, q_in, chips_in, _, _, *landed = _grad_in_and_x(xt.reshape(2, D // 2, t), dp, win4, dr, wires, grads)
    grad_x = grad_x.reshape(x.shape)
    mine = [_add_chips(q_in, chips_in)]
    mine += [_add_devices(g, land, sib, a == 4) for a, (g, land, sib) in enumerate(zip(grads, landed[:5], landed[5:]))]
    *full, small_parts = _share_results(mine[:5], mine[5])
    grad_w_in, grad_w_pa, grad_w_pb, grad_w_o = [f.reshape(w.shape) for f, w in zip(full[:4], (w_in, w_pa, w_pb, w_o))]
    grad_conv_w = full[4].reshape(HALO, D // N_CHIPS)[:CONV_K]

    big = {}
    for name, w, g, m, v in [("w_in", w_in, grad_w_in, m_w_in, v_w_in), ("w_pa", w_pa, grad_w_pa, m_w_pa, v_w_pa),
                             ("w_pb", w_pb, grad_w_pb, m_w_pb, v_w_pb), ("w_o", w_o, grad_w_o, m_w_o, v_w_o),
                             ("conv_w", conv_w, grad_conv_w, m_conv_w, v_conv_w)]:
        big[name] = (g,) + tuple(_adamw(w, g, m, v))
    vec_names = ["conv_b", "gn_g", "gn_b", "ln_v_g", "ln_v_b", "b_o", "ln_out_g", "ln_out_b"]
    vec_triples = [(conv_b, m_conv_b, v_conv_b), (gn_g, m_gn_g, v_gn_g), (gn_b, m_gn_b, v_gn_b),
                   (ln_v_g, m_ln_v_g, v_ln_v_g), (ln_v_b, m_ln_v_b, v_ln_v_b), (b_o, m_b_o, v_b_o),
                   (ln_out_g, m_ln_out_g, v_ln_out_g), (ln_out_b, m_ln_out_b, v_ln_out_b)]
    small_res, loss8 = _adamw_small(
        small_parts.reshape(8, SMALL_ROWS // 8, D), vec_triples, (b_in, m_b_in, v_b_in),
        (w_spatial, m_w_spatial, v_w_spatial), (b_spatial, m_b_spatial, v_b_spatial))
    per_name = dict(zip(vec_names + ["b_in", "w_spatial", "b_spatial"], small_res))

    order = ["w_in", "b_in", "conv_w", "conv_b", "gn_g", "gn_b", "ln_v_g", "ln_v_b", "w_spatial", "b_spatial",
             "w_pa", "w_pb", "w_o", "b_o", "ln_out_g", "ln_out_b"]
    outs = [loss8[0, 0], grad_x]
    for kind in range(4):
        outs += [big[n][kind] if n in big else per_name[n][kind] for n in order]
    return tuple(outs)
```

```python
import functools
import math

import jax
import jax.numpy as jnp
from jax import lax
from jax.experimental import pallas as pl
from jax.experimental.pallas import tpu as pltpu

D = 1024
N_GROUPS = 8
GROUP_W = D // N_GROUPS
CHUNK = 128
CONV_K = 31
HALO = 32
D_IN = 8 * D
N_CHIPS = 4
W_BLOCK = D_IN // N_CHIPS
ALPHA = 2.0 ** 0.25
LN_EPS = 1e-5
ADAM_LR, ADAM_B1, ADAM_B2, ADAM_EPS, ADAM_WD, ADAM_STEP = 0.001, 0.9, 0.999, 1e-08, 0.01, 10

TOKEN_TILE = 256
VMEM_LIMIT = 56 * 1024 * 1024
MESH = pl.DeviceIdType.MESH
F32, BF16 = jnp.float32, jnp.bfloat16


def _sigmoid(x):
    return 1.0 / (1.0 + jnp.exp(-x))


def _gelu(x):
    c = math.sqrt(2.0 / math.pi)
    t = jnp.tanh(c * (x + 0.044715 * (x * x * x)))
    return x * (0.5 * (1.0 + t))


def _gelu_and_grad(x):
    c = math.sqrt(2.0 / math.pi)
    x2 = x * x
    t = jnp.tanh(c * (x + 0.044715 * (x2 * x)))
    cdf = 0.5 * (1.0 + t)
    return x * cdf, cdf + 0.5 * x * (1.0 - t * t) * (c * (1.0 + 3.0 * 0.044715 * x2))


def _norm_stats(v):
    mu = jnp.mean(v, axis=-1, keepdims=True)
    vc = v - mu
    var = jnp.mean(vc * vc, axis=-1, keepdims=True)
    rstd = lax.rsqrt(var + LN_EPS)
    return vc * rstd, rstd


def _norm_bwd(dxhat, xhat, rstd):
    m1 = jnp.mean(dxhat, axis=-1, keepdims=True)
    m2 = jnp.mean(dxhat * xhat, axis=-1, keepdims=True)
    return rstd * (dxhat - m1 - xhat * m2)


def _dot(a, b):
    return jnp.dot(a, b, preferred_element_type=F32)


def _dot_nt(a, b):
    return lax.dot_general(a, b, (((1,), (1,)), ((), ())), preferred_element_type=F32)


def _dot_tn(a, b):
    return lax.dot_general(a, b, (((0,), (0,)), ((), ())), preferred_element_type=F32)


def _colsum(v):
    return jnp.sum(v, axis=0, keepdims=True)


def _full(shape):
    return pl.BlockSpec(shape, lambda *_: (0,) * len(shape))


def _resident(shape):
    return pl.BlockSpec(shape, lambda *_: (0,) * len(shape), pipeline_mode=pl.Buffered(1))


def _params(*sem):
    return pltpu.CompilerParams(dimension_semantics=sem, vmem_limit_bytes=VMEM_LIMIT)


def _chip_index():
    return (2 * lax.axis_index("x") + lax.axis_index("y")).astype(jnp.int32).reshape(1)


def _core_index():
    return lax.axis_index("c").astype(jnp.int32).reshape(1)


def _place_shards(ws, dtypes):
    n = len(ws)

    def body(k_ref, *refs):
        for w_ref, o_ref, dtype in zip(refs[:n], refs[n:], dtypes):
            rows = w_ref.shape[0] // 2
            for h in range(2):
                o_ref[h] = w_ref[h * rows:(h + 1) * rows, :].astype(dtype)

    return pl.pallas_call(
        body, name="place_shards",
        grid_spec=pltpu.PrefetchScalarGridSpec(
            num_scalar_prefetch=1, grid=(1,),
            in_specs=[pl.BlockSpec(w.shape, lambda i, k: (0, 0)) for w in ws],
            out_specs=[pl.BlockSpec((None, 2, w.shape[0] // 2, w.shape[1]), lambda i, k: (k[0], 0, 0, 0)) for w in ws]),
        out_shape=[jax.ShapeDtypeStruct((N_CHIPS, 2, w.shape[0] // 2, w.shape[1]), dt) for w, dt in zip(ws, dtypes)],
        compiler_params=_params("arbitrary"),
    )(_chip_index(), *ws)


def _position():
    x, y, c = lax.axis_index("x"), lax.axis_index("y"), lax.axis_index("c")
    return x, y, c, 2 * x + y


def _other_chips(x, y):
    return [(1 - x, y), (x, 1 - y), (1 - x, 1 - y)]


def _any_specs(n):
    return [pl.BlockSpec(memory_space=pl.ANY)] * n


def _proj_gather(x, b_in, bufs):
    t = x.shape[0]
    tm = 1024
    steps = t // tm
    ahead = steps // 2
    half = D // 2
    chunk = W_BLOCK // 2
    n = len(bufs)
    xi, yi = lax.axis_index("x"), lax.axis_index("y")
    chips = [2 * xi + yi, 2 * (1 - xi) + yi, 2 * xi + (1 - yi), 2 * (1 - xi) + (1 - yi)]
    plan = [(0, 0), (0, 1), (1, 0), (2, 1), (1, 1), (2, 0), (3, 0), (3, 1)]
    order = jnp.stack([2 * chips[ch] + q for ch, q in plan]).astype(jnp.int32)

    def body(order_ref, x_ref, b_ref, *refs):
        p_ref, outs = refs[n], refs[n + 1:2 * n + 1]
        xb_ref, w_ref, lsem, send, recv, hop_send, hop_recv, fsend, frecv, qsend, qrecv = refs[2 * n + 1:]
        jj, i = pl.program_id(0), pl.program_id(1)
        x_, y_, c, k = _position()
        nbrs = [(1 - x_, y_), (x_, 1 - y_)]
        blocks = [2 * (1 - x_) + y_, 2 * x_ + (1 - y_), 2 * (1 - x_) + (1 - y_)]

        def quarter(a, block, q, h):
            if a == 0:
                return outs[0].at[block, h, :, pl.ds(q * chunk, chunk)]
            rows = outs[a].shape[2] // 2
            return outs[a].at[block, h, pl.ds(q * rows, rows)]

        def copy(ref, to, send_sem, recv_sem):
            return pltpu.make_async_remote_copy(src_ref=ref, dst_ref=ref, send_sem=send_sem, recv_sem=recv_sem,
                                                device_id=(to[0], to[1], c), device_id_type=MESH)

        def sent(a, nb, q):
            return copy(quarter(a, k, q, c), nbrs[nb], send.at[4 * a + 2 * nb + q], recv.at[4 * a + 2 * nb + q])

        def landed(a, nb, q):
            return copy(quarter(a, blocks[nb], q, c), nbrs[nb], send.at[4 * a + 2 * nb + q], recv.at[4 * a + 2 * nb + q])

        def hopped(a, nb):
            return copy(quarter(a, blocks[nb], nb, c), nbrs[1 - nb], hop_send.at[2 * a + nb], hop_recv.at[2 * a + 1 - nb])

        def from_diagonal(a, via):
            return copy(quarter(a, blocks[2], 1 - via, c), nbrs[via], hop_send.at[2 * a + via], hop_recv.at[2 * a + via])

        def passed(a, r, h):
            return pltpu.make_async_remote_copy(
                src_ref=outs[a].at[blocks[r], h], dst_ref=outs[a].at[blocks[r], h], send_sem=fsend.at[3 * a + r],
                recv_sem=frecv.at[3 * a + r], device_id=(x_, y_, 1 - c), device_id_type=MESH)

        def passed_quarter(r, q, h):
            ref = quarter(0, blocks[r], q, h)
            return pltpu.make_async_remote_copy(
                src_ref=ref, dst_ref=ref, send_sem=qsend.at[2 * r + q], recv_sem=qrecv.at[2 * r + q],
                device_id=(x_, y_, 1 - c), device_id_type=MESH)

        def load(block, q, slot):
            return [pltpu.make_async_copy(quarter(0, block, q, h), w_ref.at[slot, pl.ds(h * half, half)],
                                          lsem.at[2 * slot + h]) for h in range(2)]

        def pass_on(arrays):
            for a in arrays:
                for nb in range(2):
                    landed(a, nb, nb).wait_recv()
                    hopped(a, nb).start()

        @pl.when((jj == 0) & (i == 0))
        def _():
            for a in range(n):
                for nb, q in ((0, 0), (1, 1), (0, 1), (1, 0)):
                    sent(a, nb, q).start()
            for cp in load(k, 0, 0):
                cp.start()

        for nxt in range(1, len(plan)):
            @pl.when((jj == nxt - 1) & (i == ahead))
            def _(nxt=nxt):
                ch, q = plan[nxt]
                if ch in (1, 2):
                    landed(0, ch - 1, q).wait_recv()
                    if q == ch - 1:
                        hopped(0, ch - 1).start()
                elif ch == 3:
                    from_diagonal(0, 1 - q).wait_recv()
                if ch:
                    passed_quarter(ch - 1, q, c).start()
                    passed_quarter(ch - 1, q, 1 - c).wait_recv()
                for cp in load(k if ch == 0 else blocks[ch - 1], q, nxt % 2):
                    cp.start()
                if nxt == 5:
                    pass_on(range(1, n))
                    for a in range(1, n):
                        landed(a, 0, 1).wait_recv()
                        passed(a, 0, c).start()
                        landed(a, 1, 0).wait_recv()
                        passed(a, 1, c).start()

        slot = jj % 2

        @pl.when(i == 0)
        def _():
            for cp in load(k, 0, slot):
                cp.wait()

        rows = pl.ds(pl.multiple_of(i * tm, tm), tm)

        @pl.when(jj == 0)
        def _():
            xb_ref[rows, :] = x_ref[...].astype(BF16)

        p_ref[...] = _dot(xb_ref[rows, :], w_ref[slot]) + b_ref[...]

        @pl.when((jj == len(plan) - 1) & (i == steps - 1))
        def _():
            for a in range(1, n):
                from_diagonal(a, 0).wait_recv()
                from_diagonal(a, 1).wait_recv()
                passed(a, 2, c).start()
            for a in range(1, n):
                for r in range(3):
                    passed(a, r, 1 - c).wait_recv()
                    passed(a, r, c).wait_send()
            for r in range(3):
                for q in range(2):
                    passed_quarter(r, q, c).wait_send()
            for a in range(n):
                for nb in range(2):
                    for q in range(2):
                        sent(a, nb, q).wait_send()
                    hopped(a, nb).wait_send()

    any_spec = pl.BlockSpec(memory_space=pl.ANY)
    return pl.pallas_call(
        body, name="proj_gather",
        grid_spec=pltpu.PrefetchScalarGridSpec(
            num_scalar_prefetch=1, grid=(len(plan), steps),
            in_specs=[pl.BlockSpec((tm, D), lambda jj, i, o: (jnp.where(jj == 0, i, steps - 1), 0)),
                      pl.BlockSpec((None, 1, chunk), lambda jj, i, o: (o[jj], 0, 0))] + [any_spec] * n,
            out_specs=[pl.BlockSpec((tm, chunk), lambda jj, i, o: (i, o[jj]))] + [any_spec] * n,
            scratch_shapes=[pltpu.VMEM((t, D), BF16), pltpu.VMEM((2, D, chunk), BF16), pltpu.SemaphoreType.DMA((4,)),
                            pltpu.SemaphoreType.DMA((4 * n,)), pltpu.SemaphoreType.DMA((4 * n,)),
                            pltpu.SemaphoreType.DMA((2 * n,)), pltpu.SemaphoreType.DMA((2 * n,)),
                            pltpu.SemaphoreType.DMA((3 * n,)), pltpu.SemaphoreType.DMA((3 * n,)),
                            pltpu.SemaphoreType.DMA((6,)), pltpu.SemaphoreType.DMA((6,))]),
        out_shape=[jax.ShapeDtypeStruct((t, D_IN), F32)] + [jax.ShapeDtypeStruct(b.shape, b.dtype) for b in bufs],
        input_output_aliases={3 + a: 1 + a for a in range(n)},
        compiler_params=_params("arbitrary", "arbitrary"),
    )(order, x, b_in.reshape(D_IN // chunk, 1, chunk), *bufs)


def _share_results(bufs, small):
    n = len(bufs)

    def body(*refs):
        outs, small_out = refs[n + 1:2 * n + 1], refs[2 * n + 1]
        send, recv, ssend, srecv = refs[2 * n + 2:]
        x, y, c, k = _position()
        cps = []
        for a in range(n):
            cp = pltpu.make_async_remote_copy(
                src_ref=outs[a].at[c], dst_ref=outs[a].at[c], send_sem=send.at[a], recv_sem=recv.at[a],
                device_id=(x, y, 1 - c), device_id_type=MESH)
            cp.start()
            cps.append(cp)
        waits = []
        for p in range(1, 8):
            px, py, pc = x ^ (p >> 2), y ^ ((p >> 1) & 1), c ^ (p & 1)
            cp = pltpu.make_async_remote_copy(
                src_ref=small_out.at[k, c], dst_ref=small_out.at[k, c], send_sem=ssend.at[p - 1],
                recv_sem=srecv.at[p - 1], device_id=(px, py, pc), device_id_type=MESH)
            cp.start()
            cps.append(cp)
            waits.append(pltpu.make_async_remote_copy(
                src_ref=small_out.at[2 * px + py, pc], dst_ref=small_out.at[2 * px + py, pc], send_sem=ssend.at[p - 1],
                recv_sem=srecv.at[p - 1], device_id=(px, py, pc), device_id_type=MESH))
        for a in range(n):
            pltpu.make_async_remote_copy(
                src_ref=outs[a].at[1 - c], dst_ref=outs[a].at[1 - c], send_sem=send.at[a], recv_sem=recv.at[a],
                device_id=(x, y, 1 - c), device_id_type=MESH).wait_recv()
        for w in waits:
            w.wait_recv()
        for cp in cps:
            cp.wait_send()

    return pl.pallas_call(
        body, name="rs_share_results",
        in_specs=_any_specs(n + 1), out_specs=_any_specs(n + 1),
        out_shape=[jax.ShapeDtypeStruct(b.shape, b.dtype) for b in bufs + [small]],
        scratch_shapes=[pltpu.SemaphoreType.DMA((n,)), pltpu.SemaphoreType.DMA((n,)),
                        pltpu.SemaphoreType.DMA((7,)), pltpu.SemaphoreType.DMA((7,))],
        input_output_aliases={a: a for a in range(n + 1)},
    )(*bufs, small)


def _row_tile(r, c):
    t = max(8, min(r, (1 << 18) // c))
    while r % t:
        t //= 2
    return t


def _add_devices(grads, lands, sibs):
    n = len(grads)

    def body(kc_ref, *refs):
        for g_ref, l_ref, s_ref, f_ref in zip(refs[:n], refs[n:2 * n], refs[2 * n:3 * n], refs[3 * n:]):
            f = g_ref[...] + s_ref[...]
            for i in range(l_ref.shape[0]):
                f = f + l_ref[i].astype(F32)
            f_ref[...] = f

    shapes = [g.shape[2:] for g in grads]
    out_specs = [pl.BlockSpec((None,) + sh, lambda i, kc: (kc[1], 0, 0)) for sh in shapes[:-1]]
    out_specs.append(pl.BlockSpec((None, None) + shapes[-1], lambda i, kc: (kc[0], kc[1], 0, 0)))
    out_shape = [jax.ShapeDtypeStruct((2,) + sh, F32) for sh in shapes[:-1]]
    out_shape.append(jax.ShapeDtypeStruct((N_CHIPS, 2) + shapes[-1], F32))
    return pl.pallas_call(
        body, name="rs_add_devices",
        grid_spec=pltpu.PrefetchScalarGridSpec(
            num_scalar_prefetch=1, grid=(1,),
            in_specs=[pl.BlockSpec((None, None) + sh, lambda i, kc: (kc[1], kc[0], 0, 0)) for sh in shapes]
            + [pl.BlockSpec(l.shape, lambda i, kc: (0, 0, 0)) for l in lands]
            + [pl.BlockSpec(sh, lambda i, kc: (0, 0)) for sh in shapes],
            out_specs=out_specs),
        out_shape=out_shape,
        compiler_params=_params("arbitrary"),
    )(jnp.concatenate([_chip_index(), _core_index()]), *grads, *lands, *sibs)


def _add_chips(q, b2):
    r, c = q.shape
    t = _row_tile(r, c)

    def body(c_ref, q_ref, b_ref, f_ref):
        f_ref[...] = ((q_ref[...] + b_ref[0].astype(F32)) + b_ref[1].astype(F32)) + b_ref[2].astype(F32)

    return pl.pallas_call(
        body, name="rs_add_chips",
        grid_spec=pltpu.PrefetchScalarGridSpec(
            num_scalar_prefetch=1, grid=(r // t,),
            in_specs=[pl.BlockSpec((t, c), lambda i, cr: (i, 0)), pl.BlockSpec((3, t, c), lambda i, cr: (0, i, 0))],
            out_specs=pl.BlockSpec((None, t, c), lambda i, cr: (cr[0], i, 0))),
        out_shape=jax.ShapeDtypeStruct((2, r, c), F32),
        compiler_params=_params("parallel"),
    )(_core_index(), q, b2)


def _adamw_math(w, g, m, v):
    m = ADAM_B1 * m + (1.0 - ADAM_B1) * g
    v = ADAM_B2 * v + (1.0 - ADAM_B2) * (g * g)
    m_hat = m / (1.0 - ADAM_B1 ** ADAM_STEP)
    v_hat = v / (1.0 - ADAM_B2 ** ADAM_STEP)
    delta = -ADAM_LR * (m_hat / (jnp.sqrt(v_hat) + ADAM_EPS) + ADAM_WD * w)
    return delta, m, v


def _adamw_group(quads, conv):
    n = len(quads)
    r, c = quads[0][0].shape

    def body(*refs):
        ins, outs = refs[:4 * (n + 1)], refs[4 * (n + 1):]
        for i in range(n + 1):
            w_ref, g_ref, m_ref, v_ref = ins[4 * i:4 * i + 4]
            res = _adamw_math(w_ref[...], g_ref[...], m_ref[...], v_ref[...])
            for o_ref, val in zip(outs[3 * i:3 * i + 3], res):
                o_ref[...] = val

    half = pl.BlockSpec((r // 2, c), lambda i: (i, 0))
    whole = pl.BlockSpec(conv[0].shape, lambda i: (0, 0))
    res = pl.pallas_call(
        body, name="adamw_group", grid=(2,),
        in_specs=[half] * (4 * n) + [whole] * 4, out_specs=[half] * (3 * n) + [whole] * 3,
        out_shape=[jax.ShapeDtypeStruct((r, c), F32)] * (3 * n) + [jax.ShapeDtypeStruct(conv[0].shape, F32)] * 3,
        compiler_params=_params("arbitrary"),
    )(*[a for quad in quads for a in quad], *conv)
    return [res[3 * i:3 * i + 3] for i in range(n + 1)]


def _adamw(w, g, m, v):
    r, c = w.shape
    t = _row_tile(r, c) if r % 8 == 0 else r

    def body(w_ref, g_ref, m_ref, v_ref, d_ref, nm_ref, nv_ref):
        d_ref[...], nm_ref[...], nv_ref[...] = _adamw_math(w_ref[...], g_ref[...], m_ref[...], v_ref[...])

    spec = pl.BlockSpec((t, c), lambda i: (i, 0))
    return pl.pallas_call(
        body, name="adamw", grid=(r // t,), in_specs=[spec] * 4, out_specs=[spec] * 3,
        out_shape=[jax.ShapeDtypeStruct((r, c), F32)] * 3, compiler_params=_params("parallel"),
    )(w, g, m, v)


ROW_B_IN = 0
ROW_VECS = 8
ROW_LOSS = 16
ROW_B_SPATIAL = 24
ROW_W_SPATIAL = 32
SMALL_ROWS = 192
N_VECS = 8


def _pack_small(acc_f, acc_b, dbin_a, dbin_b, dcw8, dws, dbsp):
    cols = D // N_CHIPS

    def body(af_ref, ab_ref, da_ref, db_ref, cw_ref, ws_ref, bs_ref, o_ref, gc_ref):
        o_ref[...] = jnp.zeros_like(o_ref)
        for j in range(D_IN // D):
            src = da_ref if j < 2 else db_ref
            o_ref[ROW_B_IN + j:ROW_B_IN + j + 1, :] = src[0:1, j * D:(j + 1) * D]
        dcw = jnp.sum(cw_ref[...].reshape(HALO, SUBLANES, D), axis=1)
        o_ref[ROW_VECS:ROW_VECS + 1, :] = dcw[CONV_K:CONV_K + 1]
        o_ref[ROW_VECS + 1:ROW_VECS + 5, :] = ab_ref[0:4, :]
        o_ref[ROW_VECS + 5:ROW_VECS + 6, :] = af_ref[2:3, :]
        o_ref[ROW_VECS + 6:ROW_VECS + 8, :] = af_ref[0:2, :]
        o_ref[ROW_LOSS:ROW_LOSS + 1, :] = af_ref[3:4, :]
        head = lax.broadcasted_iota(jnp.int32, (N_GROUPS, D), 0)
        lane = lax.broadcasted_iota(jnp.int32, (N_GROUPS, D), 1)
        indicator = jnp.where(lane // GROUP_W == head, 1.0, 0.0)
        o_ref[ROW_B_SPATIAL:ROW_B_SPATIAL + N_GROUPS, 0:CHUNK] = lax.dot_general(
            indicator, bs_ref[...], (((1,), (1,)), ((), ())), precision=lax.Precision.HIGHEST, preferred_element_type=F32)
        t_idx = lax.broadcasted_iota(jnp.int32, (CHUNK, D), 0)
        s_idx = lax.broadcasted_iota(jnp.int32, (CHUNK, D), 1) % CHUNK
        o_ref[ROW_W_SPATIAL:ROW_W_SPATIAL + CHUNK, :] = jnp.where(s_idx <= t_idx, ws_ref[...], 0.0)
        for h in range(2):
            for j in range(N_CHIPS):
                gc_ref[h, j] = dcw[h * (HALO // 2):(h + 1) * (HALO // 2), j * cols:(j + 1) * cols]

    ins = [acc_f, acc_b, dbin_a, dbin_b, dcw8, dws, dbsp]
    return pl.pallas_call(
        body, name="pack_small",
        in_specs=[_full(a.shape) for a in ins],
        out_specs=[_full((SMALL_ROWS, D)), _full((2, N_CHIPS, HALO // 2, cols))],
        out_shape=[jax.ShapeDtypeStruct((SMALL_ROWS, D), F32), jax.ShapeDtypeStruct((2, N_CHIPS, HALO // 2, cols), F32)],
        compiler_params=_params(),
    )(*ins)


def _adamw_small(parts, vecs, b_in, w_spatial, b_spatial):
    triples = list(vecs) + [b_in, w_spatial, b_spatial]
    n_in = 3 * len(triples)

    def body(p_ref, *refs):
        ins = [refs[3 * i:3 * i + 3] for i in range(len(triples))]
        outs = [refs[n_in + 4 * i:n_in + 4 * i + 4] for i in range(len(triples))]
        loss_ref, g_ref = refs[n_in + 4 * len(triples):]
        rows = SMALL_ROWS // 8
        for k in range(N_CHIPS):
            for core in range(2):
                g_ref[(core * N_CHIPS + k) * rows:(core * N_CHIPS + k + 1) * rows, :] = p_ref[2 * k + core]

        def step(g, wmv, out, get, put):
            d, nm, nv = _adamw_math(get(wmv[0]), g, get(wmv[1]), get(wmv[2]))
            for o, val in zip(out, (g, d, nm, nv)):
                put(o, val)

        for i in range(N_VECS):
            step(g_ref[ROW_VECS + i:ROW_VECS + i + 1, :], ins[i], outs[i],
                 lambda r: r[...].reshape(1, D), lambda o, val: o.__setitem__(Ellipsis, val.reshape(D)))
        for j in range(D_IN // D):
            piece = pl.ds(j * D, D)
            step(g_ref[ROW_B_IN + j:ROW_B_IN + j + 1, :], ins[N_VECS], outs[N_VECS],
                 lambda r: r[piece].reshape(1, D), lambda o, val: o.__setitem__(piece, val.reshape(D)))
        for h in range(N_GROUPS):
            step(g_ref[ROW_W_SPATIAL:ROW_W_SPATIAL + CHUNK, h * CHUNK:(h + 1) * CHUNK], ins[N_VECS + 1], outs[N_VECS + 1],
                 lambda r: r[h], lambda o, val: o.__setitem__(h, val))
        step(g_ref[ROW_B_SPATIAL:ROW_B_SPATIAL + N_GROUPS, 0:CHUNK], ins[N_VECS + 2], outs[N_VECS + 2],
             lambda r: r[...], lambda o, val: o.__setitem__(Ellipsis, val))
        lanes = g_ref[ROW_LOSS:ROW_LOSS + 1, :]
        loss_ref[...] = jnp.broadcast_to(jnp.sum(lanes, axis=1, keepdims=True), (8, 128))

    flat = [a for tr in triples for a in tr]
    out_shape = [jax.ShapeDtypeStruct(tr[0].shape, F32) for tr in triples for _ in range(4)]
    out_shape.append(jax.ShapeDtypeStruct((8, 128), F32))
    res = pl.pallas_call(
        body, name="adamw_small",
        in_specs=[_full(parts.shape)] + [_full(a.shape) for a in flat],
        out_specs=[_full(o.shape) for o in out_shape],
        out_shape=out_shape,
        scratch_shapes=[pltpu.VMEM((SMALL_ROWS, D), F32)],
        compiler_params=_params(),
    )(parts, *flat)
    return [res[4 * i:4 * i + 4] for i in range(len(triples))], res[-1]


SUBLANES = 8
SHIFT_ROWS = HALO - SUBLANES


def _shifted_copies(src_ref, sh_ref, cs, tm):
    for p in range(1, SUBLANES):
        sh_ref[p - 1] = src_ref[pl.ds(p, tm + SHIFT_ROWS), cs]


def _tap(src_ref, sh_ref, cs, offset, start, rows):
    p, q = offset % SUBLANES, offset // SUBLANES
    if p == 0:
        return src_ref[pl.ds(start + SUBLANES * q, rows), cs]
    return sh_ref[p - 1, pl.ds(start + SUBLANES * q, rows), :]


def _conv_taps(src_ref, sh_ref, w_ref, first_offset, step, bias, dst_ref, tm):
    rows = 64
    for g in range(N_GROUPS):
        cs = slice(g * GROUP_W, (g + 1) * GROUP_W)
        _shifted_copies(src_ref, sh_ref, cs, tm)
        for rb in range(tm // rows):
            acc = jnp.zeros((rows, GROUP_W), F32) + (bias[:, cs] if bias is not None else 0.0)
            for k in range(CONV_K):
                acc = acc + w_ref[k:k + 1, cs] * _tap(src_ref, sh_ref, cs, first_offset + step * k, rb * rows, rows)
            dst_ref[rb * rows:(rb + 1) * rows, cs] = acc


def _conv_weight_grad(d_ref, src_ref, sh_ref, first_offset, acc_ref, tm):
    rows = 64
    for g in range(N_GROUPS):
        cs = slice(g * GROUP_W, (g + 1) * GROUP_W)
        _shifted_copies(src_ref, sh_ref, cs, tm)
        for rb in range(tm // rows):
            d = d_ref[rb * rows:(rb + 1) * rows, cs]
            for k in range(CONV_K):
                prod = d * _tap(src_ref, sh_ref, cs, first_offset + k, rb * rows, rows)
                acc_ref[SUBLANES * k:SUBLANES * (k + 1), cs] += jnp.sum(
                    prod.reshape(rows // SUBLANES, SUBLANES, GROUP_W), axis=0)


def _spatial_mix(w_ref, v_bf, tm):
    rows = []
    for q in range(tm // CHUNK):
        cols = [_dot(w_ref[h], v_bf[q * CHUNK:(q + 1) * CHUNK, h * GROUP_W:(h + 1) * GROUP_W])
                for h in range(N_GROUPS)]
        rows.append(jnp.concatenate(cols, axis=1))
    return jnp.concatenate(rows, axis=0)


def _group_norm_fwd(h1, gn_g, gn_b):
    xhat, rstd = [], []
    for g in range(N_GROUPS):
        xh, rs = _norm_stats(h1[:, g * GROUP_W:(g + 1) * GROUP_W])
        xhat.append(xh)
        rstd.append(rs)
    xhat = jnp.concatenate(xhat, axis=1)
    return xhat * gn_g + gn_b, xhat, rstd


def _forward_tiles(p, x, tgt, wpa, wpb, wo, convw, vecs, ws, bsp, tiles_per_seq):
    t = x.shape[0]
    tm = TOKEN_TILE
    hb = tm // HALO

    def body(p_ref, ph_ref, x_ref, t_ref, wpa_ref, wpb_ref, wo_ref, cw_ref, vec_ref, ws_ref, bsp_ref,
             h1_ref, ya_ref, yb_ref, h3_ref, s_ref, mx_ref, dr_ref, drb_ref, xt_ref, acc_ref, he_ref, sh_ref):
        i = pl.program_id(0)
        xt_ref[...] = x_ref[...].T.astype(BF16)
        conv_b, gn_g, gn_b, lnv_g, lnv_b, b_o, lno_g, lno_b = [vec_ref[j:j + 1, :] for j in range(8)]

        keep = jnp.where(i % tiles_per_seq == 0, 0.0, 1.0)
        he_ref[0:HALO, :] = ph_ref[:, 0:D] * _sigmoid(ph_ref[:, D:2 * D]) * keep
        he_ref[HALO:, :] = p_ref[:, 0:D] * _sigmoid(p_ref[:, D:2 * D])
        _conv_taps(he_ref, sh_ref, cw_ref, HALO - (CONV_K - 1), 1, conv_b, h1_ref, tm)
        h2, _, _ = _group_norm_fwd(h1_ref[...], gn_g, gn_b)
        a_gate = p_ref[:, 2 * D:3 * D]
        h3 = ((h2 * _sigmoid(h2)) * (a_gate * _sigmoid(a_gate))).astype(BF16)
        h3_ref[...] = h3
        ya = _dot(h3, wpa_ref[...])
        ya_ref[...] = ya

        u = _gelu(p_ref[:, 3 * D:4 * D])
        vhat, _ = _norm_stats(_gelu(p_ref[:, 4 * D:5 * D]))
        v1 = (vhat * lnv_g + lnv_b).astype(BF16)
        b_gate = p_ref[:, 5 * D:6 * D]
        vmix = _spatial_mix(ws_ref, v1, tm) + jnp.concatenate([bsp_ref[...]] * (tm // CHUNK), axis=0)
        s = (u * vmix * (b_gate * _sigmoid(b_gate))).astype(BF16)
        s_ref[...] = s
        yb = _dot(s, wpb_ref[...])
        yb_ref[...] = yb

        mixed = (_sigmoid(p_ref[:, 6 * D:7 * D]) * ya + _sigmoid(p_ref[:, 7 * D:8 * D]) * yb).astype(BF16)
        mx_ref[...] = mixed
        r = ALPHA * x_ref[...] + (_dot(mixed, wo_ref[...]) + b_o)
        xhat, rstd = _norm_stats(r)
        err = (xhat * lno_g + lno_b) - t_ref[...]
        dout = err * (1.0 / D)
        dr = _norm_bwd(dout * lno_g, xhat, rstd)
        dr_ref[...] = dr
        drb_ref[...] = dr.astype(BF16)

        @pl.when(i == 0)
        def _():
            acc_ref[...] = jnp.zeros_like(acc_ref)

        acc_ref[0:1, :] += _colsum(dout * xhat)
        acc_ref[1:2, :] += _colsum(dout)
        acc_ref[2:3, :] += _colsum(dr)
        acc_ref[3:4, :] += _colsum(err * err) * (0.5 / D)

    tile = lambda w: pl.BlockSpec((tm, w), lambda i: (i, 0))
    f32_out = jax.ShapeDtypeStruct((t, D), F32)
    bf_out = jax.ShapeDtypeStruct((t, D), BF16)
    return pl.pallas_call(
        body, name="forward_tiles", grid=(t // tm,),
        in_specs=[tile(D_IN),
                  pl.BlockSpec((HALO, 2 * D), lambda i: (jnp.maximum(i * hb - 1, 0), 0)),
                  tile(D), tile(D), _resident((D, D)), _resident((D, D)), _resident((D, D)), _full((HALO, D)), _full((8, D)),
                  _full((N_GROUPS, CHUNK, CHUNK)), _full((CHUNK, D))],
        out_specs=[tile(D)] * 8 + [pl.BlockSpec((D, tm), lambda i: (0, i)), _full((8, D))],
        out_shape=[f32_out, f32_out, f32_out, bf_out, bf_out, bf_out, f32_out, bf_out,
                   jax.ShapeDtypeStruct((D, t), BF16), jax.ShapeDtypeStruct((8, D), F32)],
        scratch_shapes=[pltpu.VMEM((tm + HALO, D), F32), pltpu.VMEM((SUBLANES - 1, tm + SHIFT_ROWS, GROUP_W), F32)],
        compiler_params=_params("arbitrary"),
    )(p, p, x, tgt, wpa, wpb, wo, convw, vecs, ws, bsp)


def _backward_tiles(p, h1, ya, yb, drb, wpa, wpb, wo, vecs, ws, wst, bsp):
    t = h1.shape[0]
    tm = TOKEN_TILE

    def body(p_ref, h1_ref, ya_ref, yb_ref, drb_ref, wpa_ref, wpb_ref, wo_ref, vec_ref, ws_ref, wst_ref, bsp_ref,
             dh1_ref, dp_ref, dya_ref, dyb_ref, acc_ref, dbin_ref, dws_ref, dbsp_ref):
        i = pl.program_id(0)
        _, gn_g, gn_b, lnv_g, lnv_b = [vec_ref[j:j + 1, :] for j in range(5)]

        @pl.when(i == 0)
        def _():
            acc_ref[...] = jnp.zeros_like(acc_ref)
            dbin_ref[...] = jnp.zeros_like(dbin_ref)
            dws_ref[...] = jnp.zeros_like(dws_ref)
            dbsp_ref[...] = jnp.zeros_like(dbsp_ref)

        def emit(block, val):
            dbin_ref[0:1, block * D:(block + 1) * D] += _colsum(val)
            dp_ref[:, block * D:(block + 1) * D] = val.astype(BF16)

        dp_ref[:, 0:2 * D] = jnp.zeros((tm, 2 * D), BF16)
        dmixed = _dot_nt(drb_ref[...], wo_ref[...])
        ga = _sigmoid(p_ref[:, 6 * D:7 * D])
        gb = _sigmoid(p_ref[:, 7 * D:8 * D])
        dya = (dmixed * ga).astype(BF16)
        dyb = (dmixed * gb).astype(BF16)
        dya_ref[...] = dya
        dyb_ref[...] = dyb
        emit(6, dmixed * ya_ref[...] * (ga * (1.0 - ga)))
        emit(7, dmixed * yb_ref[...] * (gb * (1.0 - gb)))

        dh3 = _dot_nt(dya, wpa_ref[...])
        h2, xhat, rstd = _group_norm_fwd(h1_ref[...], gn_g, gn_b)
        sg = _sigmoid(h2)
        a_gate = p_ref[:, 2 * D:3 * D]
        sa = _sigmoid(a_gate)
        dh2 = dh3 * (a_gate * sa) * (sg * (1.0 + h2 * (1.0 - sg)))
        emit(2, dh3 * (h2 * sg) * (sa * (1.0 + a_gate * (1.0 - sa))))
        acc_ref[0:1, :] += _colsum(dh2 * xhat)
        acc_ref[1:2, :] += _colsum(dh2)
        dxhat = dh2 * gn_g
        for g in range(N_GROUPS):
            cs = slice(g * GROUP_W, (g + 1) * GROUP_W)
            dh1_ref[:, cs] = _norm_bwd(dxhat[:, cs], xhat[:, cs], rstd[g])

        ds = _dot_nt(dyb, wpb_ref[...])
        u_pre = p_ref[:, 3 * D:4 * D]
        u, du_dpre = _gelu_and_grad(u_pre)
        v0, dv_dpre = _gelu_and_grad(p_ref[:, 4 * D:5 * D])
        vhat, vrstd = _norm_stats(v0)
        v1 = (vhat * lnv_g + lnv_b).astype(BF16)
        vmix = _spatial_mix(ws_ref, v1, tm) + jnp.concatenate([bsp_ref[...]] * (tm // CHUNK), axis=0)
        b_gate = p_ref[:, 5 * D:6 * D]
        sb = _sigmoid(b_gate)
        silu_b = b_gate * sb
        emit(3, ds * vmix * silu_b * du_dpre)
        emit(5, ds * u * vmix * (sb * (1.0 + b_gate * (1.0 - sb))))
        dvmix = ds * u * silu_b
        dvmix_bf = dvmix.astype(BF16)
        for q in range(tm // CHUNK):
            dbsp_ref[...] += dvmix[q * CHUNK:(q + 1) * CHUNK, :]
            for h in range(N_GROUPS):
                blk = (slice(q * CHUNK, (q + 1) * CHUNK), slice(h * GROUP_W, (h + 1) * GROUP_W))
                dws_ref[:, h * GROUP_W:(h + 1) * GROUP_W] += _dot_nt(dvmix_bf[blk], v1[blk])
        dv1 = _spatial_mix(wst_ref, dvmix_bf, tm)
        acc_ref[2:3, :] += _colsum(dv1 * vhat)
        acc_ref[3:4, :] += _colsum(dv1)
        emit(4, _norm_bwd(dv1 * lnv_g, vhat, vrstd) * dv_dpre)

    tile = lambda w: pl.BlockSpec((tm, w), lambda i: (i, 0))
    return pl.pallas_call(
        body, name="backward_tiles", grid=(t // tm,),
        in_specs=[tile(D_IN), tile(D), tile(D), tile(D), tile(D), _resident((D, D)), _resident((D, D)), _resident((D, D)),
                  _full((8, D)), _full((N_GROUPS, CHUNK, CHUNK)), _full((N_GROUPS, CHUNK, CHUNK)), _full((CHUNK, D))],
        out_specs=[tile(D), tile(D_IN), tile(D), tile(D), _full((8, D)), _full((8, D_IN)),
                   _full((CHUNK, D)), _full((CHUNK, D))],
        out_shape=[jax.ShapeDtypeStruct((t, D), F32), jax.ShapeDtypeStruct((t, D_IN), BF16),
                   jax.ShapeDtypeStruct((t, D), BF16), jax.ShapeDtypeStruct((t, D), BF16),
                   jax.ShapeDtypeStruct((8, D), F32), jax.ShapeDtypeStruct((8, D_IN), F32),
                   jax.ShapeDtypeStruct((CHUNK, D), F32), jax.ShapeDtypeStruct((CHUNK, D), F32)],
        compiler_params=_params("arbitrary"),
    )(p, h1, ya, yb, drb, wpa, wpb, wo, vecs, ws, wst, bsp)


def _conv_backward(dh1, p, dp, convw, pairs, tiles_per_seq):
    t = dh1.shape[0]
    tm = TOKEN_TILE
    hb = tm // HALO
    last = t // HALO - 1
    n_sq = len(pairs)
    span = 2
    rows = D // 8

    def body(dh1_ref, dnext_ref, p_ref, ph_ref, cw_ref, dp_in_ref, *refs):
        del dp_in_ref
        sq_in = refs[:2 * n_sq]
        dp_ref, dcw_ref, dbin_ref = refs[2 * n_sq:2 * n_sq + 3]
        sq_out = refs[2 * n_sq + 3:3 * n_sq + 3]
        sq_wire = refs[3 * n_sq + 3:4 * n_sq + 3]
        de_ref, he_ref, dh0_ref, sh_ref, acc_ref, wire_ref, sq_sem, wire_sem = refs[4 * n_sq + 3:]
        i = pl.program_id(0)

        @pl.when(i == 0)
        def _():
            dcw_ref[...] = jnp.zeros_like(dcw_ref)
            dbin_ref[...] = jnp.zeros_like(dbin_ref)
            acc_ref[...] = jnp.zeros_like(acc_ref)

        @pl.when(i % span == span - 1)
        def _():
            for a in range(n_sq):
                acc_ref[a] += _dot_tn(sq_in[2 * a][...], sq_in[2 * a + 1][...])

        keep_next = jnp.where(i % tiles_per_seq == tiles_per_seq - 1, 0.0, 1.0)
        de_ref[0:tm, :] = dh1_ref[...]
        de_ref[tm:, :] = dnext_ref[...] * keep_next
        _conv_taps(de_ref, sh_ref, cw_ref, CONV_K - 1, -1, None, dh0_ref, tm)

        keep_prev = jnp.where(i % tiles_per_seq == 0, 0.0, 1.0)
        sg = _sigmoid(p_ref[:, D:2 * D])
        val = p_ref[:, 0:D]
        he_ref[0:HALO, :] = ph_ref[:, 0:D] * _sigmoid(ph_ref[:, D:2 * D]) * keep_prev
        he_ref[HALO:, :] = val * sg
        _conv_weight_grad(dh1_ref, he_ref, sh_ref, HALO - (CONV_K - 1), dcw_ref, tm)
        dcw_ref[SUBLANES * CONV_K:, :] += jnp.sum(dh1_ref[...].reshape(tm // SUBLANES, SUBLANES, D), axis=0)

        dh0 = dh0_ref[...]
        dval = dh0 * sg
        dglu = dh0 * val * (sg * (1.0 - sg))
        dbin_ref[0:1, 0:D] += _colsum(dval)
        dbin_ref[0:1, D:2 * D] += _colsum(dglu)
        dp_ref[:, 0:D] = dval.astype(BF16)
        dp_ref[:, D:2 * D] = dglu.astype(BF16)

        @pl.when(i == t // tm - 1)
        def _():
            cps = [pltpu.make_async_copy(acc_ref.at[a, pl.ds((2 * j + h) * rows, rows)], sq_out[a].at[h, j],
                                         sq_sem.at[(a * N_CHIPS + j) * 2 + h])
                   for a in range(n_sq) for j in range(N_CHIPS) for h in range(2)]
            for cp in cps:
                cp.start()
            for a in range(n_sq):
                wire_ref[...] = acc_ref[a].astype(BF16)
                narrow = [pltpu.make_async_copy(wire_ref.at[pl.ds((2 * j + h) * rows, rows)], sq_wire[a].at[h, j],
                                                wire_sem.at[2 * j + h]) for j in range(N_CHIPS) for h in range(2)]
                for cp in narrow:
                    cp.start()
                for cp in narrow:
                    cp.wait()
            for cp in cps:
                cp.wait()

    any_spec = pl.BlockSpec(memory_space=pl.ANY)
    wide = pl.BlockSpec((span * tm, D), lambda i: (i // span, 0))
    return pl.pallas_call(
        body, name="conv_backward", grid=(t // tm,),
        in_specs=[pl.BlockSpec((tm, D), lambda i: (i, 0)),
                  pl.BlockSpec((HALO, D), lambda i: (jnp.minimum((i + 1) * hb, last), 0)),
                  pl.BlockSpec((tm, 2 * D), lambda i: (i, 0)),
                  pl.BlockSpec((HALO, 2 * D), lambda i: (jnp.maximum(i * hb - 1, 0), 0)),
                  _full((HALO, D)), any_spec] + [wide] * (2 * n_sq),
        out_specs=[pl.BlockSpec((tm, 2 * D), lambda i: (i, 0)), _full((SUBLANES * HALO, D)), _full((8, 2 * D))]
        + [any_spec] * (2 * n_sq),
        out_shape=[jax.ShapeDtypeStruct(dp.shape, BF16), jax.ShapeDtypeStruct((SUBLANES * HALO, D), F32),
                   jax.ShapeDtypeStruct((8, 2 * D), F32)]
        + [jax.ShapeDtypeStruct((2, N_CHIPS, rows, D), F32)] * n_sq
        + [jax.ShapeDtypeStruct((2, N_CHIPS, rows, D), BF16)] * n_sq,
        scratch_shapes=[pltpu.VMEM((tm + HALO, D), F32), pltpu.VMEM((tm + HALO, D), F32), pltpu.VMEM((tm, D), F32),
                        pltpu.VMEM((SUBLANES - 1, tm + SHIFT_ROWS, GROUP_W), F32), pltpu.VMEM((n_sq, D, D), F32),
                        pltpu.VMEM((D, D), BF16), pltpu.SemaphoreType.DMA((n_sq * N_CHIPS * 2,)),
                        pltpu.SemaphoreType.DMA((N_CHIPS * 2,))],
        input_output_aliases={5: 0},
        compiler_params=_params("arbitrary"),
    )(dh1, dh1, p, p, convw, dp, *[a for pair in pairs for a in pair])


def _grad_in_and_x(xt, dp, w4, dr, wires, grads):
    t = dr.shape[0]
    tm = TOKEN_TILE
    half, tn = D // 2, 512
    nb = W_BLOCK // tn
    n_w, n_x = 2 * N_CHIPS * nb, t // tm
    ns = len(grads)
    xi, yi, ci = lax.axis_index("x"), lax.axis_index("y"), lax.axis_index("c")
    others = [2 * (1 - xi) + yi, 2 * xi + (1 - yi), 2 * (1 - xi) + (1 - yi)]
    blocks = others + others + [2 * xi + yi] * 2
    halves = [1 - ci] * 3 + [ci] * 3 + [1 - ci, ci]
    table = jnp.stack([jnp.stack([b * nb + n for b in blocks for n in range(nb)]),
                       jnp.stack([h for h in halves for _ in range(nb)])]).astype(jnp.int32)

    def body(tab_ref, xt_ref, dpc_ref, dpr_ref, w_ref, dr_ref, *refs):
        parts, fulls = refs[:ns], refs[ns:2 * ns]
        dx_ref, qk_ref, b2_ref, b1_ref, wire_ref = refs[2 * ns:2 * ns + 5]
        lands, sibs = refs[2 * ns + 5:3 * ns + 5], refs[3 * ns + 5:4 * ns + 5]
        (g_ref, st_ref, sb_ref, tmp_ref, d2d_send, d2d_recv, ici_send, ici_recv, own_sem, tmp_sem, wire_sem,
         p_send, p_recv, s_send, s_recv) = refs[4 * ns + 5:]
        s = pl.program_id(0)
        x_, y_, c, k = _position()
        chips = _other_chips(x_, y_)
        n = s % nb
        grp = s // nb
        cols = pl.ds(pl.multiple_of(n * tn, tn), tn)

        def part(a, r, core):
            cx, cy = chips[r]
            return pltpu.make_async_remote_copy(
                src_ref=parts[a].at[core, 2 * cx + cy], dst_ref=lands[a].at[2 * r + c],
                send_sem=p_send.at[6 * a + 2 * r + core], recv_sem=p_recv.at[6 * a + 2 * r + c],
                device_id=(cx, cy, core), device_id_type=MESH)

        def landed(a, r, core):
            cx, cy = chips[r]
            return pltpu.make_async_remote_copy(
                src_ref=lands[a].at[2 * r + core], dst_ref=lands[a].at[2 * r + core],
                send_sem=p_send.at[6 * a + 2 * r + core], recv_sem=p_recv.at[6 * a + 2 * r + core],
                device_id=(cx, cy, core), device_id_type=MESH)

        def to_sibling_whole(a):
            return pltpu.make_async_remote_copy(
                src_ref=fulls[a].at[1 - c, k], dst_ref=sibs[a], send_sem=s_send.at[a], recv_sem=s_recv.at[a],
                device_id=(x_, y_, 1 - c), device_id_type=MESH)

        def to_sibling(slot, land):
            return pltpu.make_async_remote_copy(
                src_ref=st_ref.at[slot], dst_ref=b1_ref.at[land, :, cols], send_sem=d2d_send.at[slot],
                recv_sem=d2d_recv.at[land * nb + n], device_id=(x_, y_, 1 - c), device_id_type=MESH)

        def to_chip(r):
            cx, cy = chips[r]
            return pltpu.make_async_remote_copy(
                src_ref=wire_ref.at[r, :, cols], dst_ref=b2_ref.at[r, :, cols], send_sem=ici_send.at[r],
                recv_sem=ici_recv.at[r], device_id=(cx, cy, c), device_id_type=MESH)

        def all_of_chip(r):
            cx, cy = chips[r]
            return pltpu.make_async_remote_copy(
                src_ref=wire_ref.at[r], dst_ref=b2_ref.at[r], send_sem=ici_send.at[r],
                recv_sem=ici_recv.at[r], device_id=(cx, cy, c), device_id_type=MESH)

        def to_result(slot):
            return pltpu.make_async_copy(st_ref.at[slot], qk_ref.at[:, cols], own_sem.at[slot])

        def sibling_piece(land):
            return pltpu.make_async_copy(b1_ref.at[land, :, cols], tmp_ref, tmp_sem)

        @pl.when(s == 0)
        def _():
            for a in range(ns):
                to_sibling_whole(a).start()
                for r in range(3):
                    for core in range(2):
                        part(a, r, core).start()

        own_half = ((grp >= 3) & (grp <= 5)) | (grp == 7)
        land = jnp.where(grp == 7, 3, grp - 3)

        @pl.when(own_half)
        def _():
            to_sibling(0, land).wait_recv()
            sibling_piece(land).start()

        @pl.when(s < n_w)
        def _():
            g_ref[...] = _dot(xt_ref[tab_ref[1, s]], dpc_ref[...])

        @pl.when(own_half)
        def _():
            sibling_piece(land).wait()

        for g in range(2 * N_CHIPS):
            @pl.when(grp == g)
            def _(g=g):
                if g in (0, 1, 2, 6):
                    use = s if g < 3 else 3 * nb + n
                    slot = use % 2

                    @pl.when(use >= 2)
                    def _():
                        to_sibling(slot, 0).wait_send()

                    st_ref[slot] = g_ref[...]
                    to_sibling(slot, min(g, 3)).start()
                elif g in (3, 4, 5):
                    sb_ref[...] = (g_ref[...] + tmp_ref[...]).astype(BF16)
                    stage = pltpu.make_async_copy(sb_ref, wire_ref.at[g - 3, :, cols], wire_sem)
                    stage.start()
                    stage.wait()
                    to_chip(g - 3).start()
                else:
                    slot = n % 2
                    piece = g_ref[...] + tmp_ref[...]

                    @pl.when(n < 2)
                    def _():
                        to_sibling(slot, 0).wait_send()

                    @pl.when(n >= 2)
                    def _():
                        to_result(slot).wait()

                    st_ref[slot] = piece
                    to_result(slot).start()

        @pl.when(s >= n_w)
        def _():
            acc = ALPHA * dr_ref[...]
            for j in range(N_CHIPS):
                acc = acc + _dot_nt(dpr_ref[:, j * W_BLOCK:(j + 1) * W_BLOCK], w_ref[j])
            dx_ref[...] = acc

        @pl.when(s == n_w + n_x - 1)
        def _():
            for slot in range(2):
                to_result(slot).wait()
            for r in range(3):
                all_of_chip(r).wait_recv()
                all_of_chip(r).wait_send()
            for a in range(ns):
                to_sibling_whole(a).wait_recv()
                to_sibling_whole(a).wait_send()
                for r in range(3):
                    for core in range(2):
                        landed(a, r, core).wait_recv()
                        part(a, r, core).wait_send()

    any_spec = pl.BlockSpec(memory_space=pl.ANY)
    tile = lambda s, tab: (jnp.maximum(s - n_w, 0), 0)
    return pl.pallas_call(
        body, name="grad_in_and_x",
        grid_spec=pltpu.PrefetchScalarGridSpec(
            num_scalar_prefetch=1, grid=(n_w + n_x,),
            in_specs=[pl.BlockSpec((2, half, t), lambda s, tab: (0, 0, 0), pipeline_mode=pl.Buffered(1)),
                      pl.BlockSpec((t, tn), lambda s, tab: (0, tab[0, jnp.minimum(s, n_w - 1)])),
                      pl.BlockSpec((tm, D_IN), tile),
                      pl.BlockSpec((N_CHIPS, D, W_BLOCK), lambda s, tab: (0, 0, 0), pipeline_mode=pl.Buffered(1)),
                      pl.BlockSpec((tm, D), tile)] + [any_spec] * (2 * ns),
            out_specs=[pl.BlockSpec((tm, D), tile)] + [any_spec] * (4 + 2 * ns),
            scratch_shapes=[pltpu.VMEM((half, tn), F32), pltpu.VMEM((2, half, tn), F32), pltpu.VMEM((half, tn), BF16),
                            pltpu.VMEM((half, tn), F32),
                            pltpu.SemaphoreType.DMA((2,)), pltpu.SemaphoreType.DMA((N_CHIPS * nb,)),
                            pltpu.SemaphoreType.DMA((3,)), pltpu.SemaphoreType.DMA((3,)),
                            pltpu.SemaphoreType.DMA((2,)), pltpu.SemaphoreType.DMA, pltpu.SemaphoreType.DMA,
                            pltpu.SemaphoreType.DMA((6 * ns,)), pltpu.SemaphoreType.DMA((6 * ns,)),
                            pltpu.SemaphoreType.DMA((ns,)), pltpu.SemaphoreType.DMA((ns,))]),
        out_shape=[jax.ShapeDtypeStruct((t, D), F32), jax.ShapeDtypeStruct((half, W_BLOCK), F32),
                   jax.ShapeDtypeStruct((3, half, W_BLOCK), BF16), jax.ShapeDtypeStruct((N_CHIPS, half, W_BLOCK), F32),
                   jax.ShapeDtypeStruct((3, half, W_BLOCK), BF16)]
        + [jax.ShapeDtypeStruct((6,) + w.shape[2:], w.dtype) for w in wires]
        + [jax.ShapeDtypeStruct(g.shape[2:], F32) for g in grads],
        compiler_params=_params("arbitrary"),
    )(table, xt, dp, dp, w4, dr, *wires, *grads)


def kernel(x, w_in, b_in, conv_w, conv_b, gn_g, gn_b, ln_v_g, ln_v_b, w_spatial, b_spatial, w_pa, w_pb, w_o, b_o, ln_out_g, ln_out_b, loss_target, m_w_in, m_b_in, m_conv_w, m_conv_b, m_gn_g, m_gn_b, m_ln_v_g, m_ln_v_b, m_w_spatial, m_b_spatial, m_w_pa, m_w_pb, m_w_o, m_b_o, m_ln_out_g, m_ln_out_b, v_w_in, v_b_in, v_conv_w, v_conv_b, v_gn_g, v_gn_b, v_ln_v_g, v_ln_v_b, v_w_spatial, v_b_spatial, v_w_pa, v_w_pb, v_w_o, v_b_o, v_ln_out_g, v_ln_out_b):
    n_seq, seq, _ = x.shape
    t = n_seq * seq
    tiles_per_seq = seq // TOKEN_TILE
    x2 = x.reshape(t, D)
    tgt = loss_target.reshape(t, D)

    conv_shard = jnp.pad(conv_w, ((0, HALO - CONV_K), (0, 0)))
    p, win4, wpa4, wpb4, wo4, conv4 = _proj_gather(
        x2, b_in,
        _place_shards([w_in, w_pa, w_pb, w_o, conv_shard], [BF16, BF16, BF16, BF16, F32]))
    win4 = win4.reshape(N_CHIPS, D, W_BLOCK)
    wpa, wpb, wo = wpa4.reshape(D, D), wpb4.reshape(D, D), wo4.reshape(D, D)
    convw = conv4.reshape(N_CHIPS, HALO, D // N_CHIPS).transpose(1, 0, 2).reshape(HALO, D)

    vecs = jnp.stack([conv_b, gn_g, gn_b, ln_v_g, ln_v_b, b_o, ln_out_g, ln_out_b])
    causal = jnp.tril(jnp.ones((CHUNK, CHUNK), bool))
    ws = jnp.where(causal[None], w_spatial, 0.0)
    ws_bf, wst_bf = ws.astype(BF16), ws.transpose(0, 2, 1).astype(BF16)
    bsp = jnp.repeat(b_spatial.T, GROUP_W, axis=1)

    h1, ya, yb, h3, s, mixed, dr, drb, xt, acc_f = _forward_tiles(p, x2, tgt, wpa, wpb, wo, convw, vecs, ws_bf, bsp, tiles_per_seq)
    dh1, dp, dya, dyb, acc_b, dbin_b, dws, dbsp_acc = _backward_tiles(p, h1, ya, yb, drb, wpa, wpb, wo, vecs, ws_bf, wst_bf, bsp)
    dp, dcw8, dbin_a, *square = _conv_backward(dh1, p, dp, convw, [(h3, dya), (s, dyb), (mixed, drb)], tiles_per_seq)

    small, g_conv = _pack_small(acc_f, acc_b, dbin_a, dbin_b, dcw8, dws, dbsp_acc)
    small = small.reshape(2, N_CHIPS, SMALL_ROWS // 8, D)

    grads = square[:3] + [g_conv, small]
    wires = square[3:] + [g_conv, small]
    grad_x, q_in, chips_in, _, _, *landed = _grad_in_and_x(xt.reshape(2, D // 2, t), dp, win4, dr, wires, grads)
    grad_x = grad_x.reshape(x.shape)
    mine = [_add_chips(q_in, chips_in)]
    mine += _add_devices(grads, landed[:5], landed[5:])
    *full, small_parts = _share_results(mine[:5], mine[5])
    grad_w_in, grad_w_pa, grad_w_pb, grad_w_o = [f.reshape(w.shape) for f, w in zip(full[:4], (w_in, w_pa, w_pb, w_o))]
    grad_conv_w = full[4].reshape(HALO, D // N_CHIPS)[:CONV_K]

    big = {"w_in": (grad_w_in,) + tuple(_adamw(w_in, grad_w_in, m_w_in, v_w_in))}
    group = _adamw_group([(w_pa, grad_w_pa, m_w_pa, v_w_pa), (w_pb, grad_w_pb, m_w_pb, v_w_pb),
                          (w_o, grad_w_o, m_w_o, v_w_o)], (conv_w, grad_conv_w, m_conv_w, v_conv_w))
    for name, g, res in zip(["w_pa", "w_pb", "w_o", "conv_w"], [grad_w_pa, grad_w_pb, grad_w_o, grad_conv_w], group):
        big[name] = (g,) + tuple(res)
    vec_names = ["conv_b", "gn_g", "gn_b", "ln_v_g", "ln_v_b", "b_o", "ln_out_g", "ln_out_b"]
    vec_triples = [(conv_b, m_conv_b, v_conv_b), (gn_g, m_gn_g, v_gn_g), (gn_b, m_gn_b, v_gn_b),
                   (ln_v_g, m_ln_v_g, v_ln_v_g), (ln_v_b, m_ln_v_b, v_ln_v_b), (b_o, m_b_o, v_b_o),
                   (ln_out_g, m_ln_out_g, v_ln_out_g), (ln_out_b, m_ln_out_b, v_ln_out_b)]
    small_res, loss8 = _adamw_small(
        small_parts.reshape(8, SMALL_ROWS // 8, D), vec_triples, (b_in, m_b_in, v_b_in),
        (w_spatial, m_w_spatial, v_w_spatial), (b_spatial, m_b_spatial, v_b_spatial))
    per_name = dict(zip(vec_names + ["b_in", "w_spatial", "b_spatial"], small_res))

    order = ["w_in", "b_in", "conv_w", "conv_b", "gn_g", "gn_b", "ln_v_g", "ln_v_b", "w_spatial", "b_spatial",
             "w_pa", "w_pb", "w_o", "b_o", "ln_out_g", "ln_out_b"]
    outs = [loss8[0, 0], grad_x]
    for kind in range(4):
        outs += [big[n][kind] if n in big else per_name[n][kind] for n in order]
    return tuple(outs)
```

```python
import math

import jax
import jax.numpy as jnp
from jax import lax
from jax.experimental import pallas as pl
from jax.experimental.pallas import tpu as pltpu

D = 1024
N_GROUPS = 8
GROUP_W = D // N_GROUPS
CHUNK = 128
CONV_K = 31
HALO = 32
D_IN = 8 * D
N_CHIPS = 4
W_BLOCK = D_IN // N_CHIPS
ALPHA = 2.0 ** 0.25
LN_EPS = 1e-5
ADAM_LR, ADAM_B1, ADAM_B2, ADAM_EPS, ADAM_WD, ADAM_STEP = 0.001, 0.9, 0.999, 1e-08, 0.01, 10

TOKEN_TILE = 256
LAYER_TILE = 128
VMEM_LIMIT = 56 * 1024 * 1024
MESH = pl.DeviceIdType.MESH
F32, BF16 = jnp.float32, jnp.bfloat16


def _sigmoid(x):
    return 1.0 / (1.0 + jnp.exp(-x))


def _gelu_and_grad(x):
    c = math.sqrt(2.0 / math.pi)
    x2 = x * x
    t = jnp.tanh(c * (x + 0.044715 * (x2 * x)))
    cdf = 0.5 * (1.0 + t)
    return x * cdf, cdf + 0.5 * x * (1.0 - t * t) * (c * (1.0 + 3.0 * 0.044715 * x2))


def _norm_stats(v):
    mu = jnp.mean(v, axis=-1, keepdims=True)
    vc = v - mu
    var = jnp.mean(vc * vc, axis=-1, keepdims=True)
    rstd = lax.rsqrt(var + LN_EPS)
    return vc * rstd, rstd


def _norm_bwd(dxhat, xhat, rstd):
    m1 = jnp.mean(dxhat, axis=-1, keepdims=True)
    m2 = jnp.mean(dxhat * xhat, axis=-1, keepdims=True)
    return rstd * (dxhat - m1 - xhat * m2)


def _dot(a, b):
    return jnp.dot(a, b, preferred_element_type=F32)


def _dot_nt(a, b):
    return lax.dot_general(a, b, (((1,), (1,)), ((), ())), preferred_element_type=F32)


def _dot_tn(a, b):
    return lax.dot_general(a, b, (((0,), (0,)), ((), ())), preferred_element_type=F32)


def _colsum(v):
    return jnp.sum(v, axis=0, keepdims=True)


def _full(shape):
    return pl.BlockSpec(shape, lambda *_: (0,) * len(shape))


def _resident(shape):
    return pl.BlockSpec(shape, lambda *_: (0,) * len(shape), pipeline_mode=pl.Buffered(1))


def _params(*sem):
    return pltpu.CompilerParams(dimension_semantics=sem, vmem_limit_bytes=VMEM_LIMIT)


def _chip_index():
    return (2 * lax.axis_index("x") + lax.axis_index("y")).astype(jnp.int32).reshape(1)


def _core_index():
    return lax.axis_index("c").astype(jnp.int32).reshape(1)


def _place_shards(ws, dtypes):
    n = len(ws)

    def body(k_ref, *refs):
        for w_ref, o_ref, dtype in zip(refs[:n], refs[n:], dtypes):
            rows = w_ref.shape[0] // 2
            for h in range(2):
                o_ref[h] = w_ref[h * rows:(h + 1) * rows, :].astype(dtype)

    return pl.pallas_call(
        body, name="place_shards",
        grid_spec=pltpu.PrefetchScalarGridSpec(
            num_scalar_prefetch=1, grid=(1,),
            in_specs=[pl.BlockSpec(w.shape, lambda i, k: (0, 0)) for w in ws],
            out_specs=[pl.BlockSpec((None, 2, w.shape[0] // 2, w.shape[1]), lambda i, k: (k[0], 0, 0, 0)) for w in ws]),
        out_shape=[jax.ShapeDtypeStruct((N_CHIPS, 2, w.shape[0] // 2, w.shape[1]), dt) for w, dt in zip(ws, dtypes)],
        compiler_params=_params("arbitrary"),
    )(_chip_index(), *ws)


def _position():
    x, y, c = lax.axis_index("x"), lax.axis_index("y"), lax.axis_index("c")
    return x, y, c, 2 * x + y


def _other_chips(x, y):
    return [(1 - x, y), (x, 1 - y), (1 - x, 1 - y)]


def _any_specs(n):
    return [pl.BlockSpec(memory_space=pl.ANY)] * n


def _proj_gather(x, b_in, bufs):
    t = x.shape[0]
    tm = 1024
    steps = t // tm
    ahead = steps // 2
    half = D // 2
    chunk = W_BLOCK // 2
    n = len(bufs)
    xi, yi = lax.axis_index("x"), lax.axis_index("y")
    chips = [2 * xi + yi, 2 * (1 - xi) + yi, 2 * xi + (1 - yi), 2 * (1 - xi) + (1 - yi)]
    plan = [(0, 0), (0, 1), (1, 0), (2, 1), (1, 1), (2, 0), (3, 0), (3, 1)]
    order = jnp.stack([2 * chips[ch] + q for ch, q in plan]).astype(jnp.int32)

    def body(order_ref, x_ref, b_ref, *refs):
        p_ref, outs = refs[n], refs[n + 1:2 * n + 1]
        xb_ref, w_ref, lsem, send, recv, hop_send, hop_recv, fsend, frecv, qsend, qrecv = refs[2 * n + 1:]
        jj, i = pl.program_id(0), pl.program_id(1)
        x_, y_, c, k = _position()
        nbrs = [(1 - x_, y_), (x_, 1 - y_)]
        blocks = [2 * (1 - x_) + y_, 2 * x_ + (1 - y_), 2 * (1 - x_) + (1 - y_)]

        def quarter(a, block, q, h):
            if a == 0:
                return outs[0].at[block, h, :, pl.ds(q * chunk, chunk)]
            rows = outs[a].shape[2] // 2
            return outs[a].at[block, h, pl.ds(q * rows, rows)]

        def copy(ref, to, send_sem, recv_sem):
            return pltpu.make_async_remote_copy(src_ref=ref, dst_ref=ref, send_sem=send_sem, recv_sem=recv_sem,
                                                device_id=(to[0], to[1], c), device_id_type=MESH)

        def sent(a, nb, q):
            return copy(quarter(a, k, q, c), nbrs[nb], send.at[4 * a + 2 * nb + q], recv.at[4 * a + 2 * nb + q])

        def landed(a, nb, q):
            return copy(quarter(a, blocks[nb], q, c), nbrs[nb], send.at[4 * a + 2 * nb + q], recv.at[4 * a + 2 * nb + q])

        def hopped(a, nb):
            return copy(quarter(a, blocks[nb], nb, c), nbrs[1 - nb], hop_send.at[2 * a + nb], hop_recv.at[2 * a + 1 - nb])

        def from_diagonal(a, via):
            return copy(quarter(a, blocks[2], 1 - via, c), nbrs[via], hop_send.at[2 * a + via], hop_recv.at[2 * a + via])

        def passed(a, r, h):
            return pltpu.make_async_remote_copy(
                src_ref=outs[a].at[blocks[r], h], dst_ref=outs[a].at[blocks[r], h], send_sem=fsend.at[3 * a + r],
                recv_sem=frecv.at[3 * a + r], device_id=(x_, y_, 1 - c), device_id_type=MESH)

        def passed_quarter(r, q, h):
            ref = quarter(0, blocks[r], q, h)
            return pltpu.make_async_remote_copy(
                src_ref=ref, dst_ref=ref, send_sem=qsend.at[2 * r + q], recv_sem=qrecv.at[2 * r + q],
                device_id=(x_, y_, 1 - c), device_id_type=MESH)

        def load(block, q, slot):
            return [pltpu.make_async_copy(quarter(0, block, q, h), w_ref.at[slot, pl.ds(h * half, half)],
                                          lsem.at[2 * slot + h]) for h in range(2)]

        def pass_on(arrays):
            for a in arrays:
                for nb in range(2):
                    landed(a, nb, nb).wait_recv()
                    hopped(a, nb).start()

        @pl.when((jj == 0) & (i == 0))
        def _():
            for a in range(n):
                for nb, q in ((0, 0), (1, 1), (0, 1), (1, 0)):
                    sent(a, nb, q).start()
            for cp in load(k, 0, 0):
                cp.start()

        for nxt in range(1, len(plan)):
            @pl.when((jj == nxt - 1) & (i == ahead))
            def _(nxt=nxt):
                ch, q = plan[nxt]
                if ch in (1, 2):
                    landed(0, ch - 1, q).wait_recv()
                    if q == ch - 1:
                        hopped(0, ch - 1).start()
                elif ch == 3:
                    from_diagonal(0, 1 - q).wait_recv()
                if ch:
                    passed_quarter(ch - 1, q, c).start()
                    passed_quarter(ch - 1, q, 1 - c).wait_recv()
                for cp in load(k if ch == 0 else blocks[ch - 1], q, nxt % 2):
                    cp.start()
                if nxt == 5:
                    pass_on(range(1, n))
                    for a in range(1, n):
                        landed(a, 0, 1).wait_recv()
                        passed(a, 0, c).start()
                        landed(a, 1, 0).wait_recv()
                        passed(a, 1, c).start()

        slot = jj % 2

        @pl.when(i == 0)
        def _():
            for cp in load(k, 0, slot):
                cp.wait()

        rows = pl.ds(pl.multiple_of(i * tm, tm), tm)

        @pl.when(jj == 0)
        def _():
            xb_ref[rows, :] = x_ref[...].astype(BF16)

        p_ref[...] = _dot(xb_ref[rows, :], w_ref[slot]) + b_ref[...]

        @pl.when((jj == len(plan) - 1) & (i == steps - 1))
        def _():
            for a in range(1, n):
                from_diagonal(a, 0).wait_recv()
                from_diagonal(a, 1).wait_recv()
                passed(a, 2, c).start()
            for a in range(1, n):
                for r in range(3):
                    passed(a, r, 1 - c).wait_recv()
                    passed(a, r, c).wait_send()
            for r in range(3):
                for q in range(2):
                    passed_quarter(r, q, c).wait_send()
            for a in range(n):
                for nb in range(2):
                    for q in range(2):
                        sent(a, nb, q).wait_send()
                    hopped(a, nb).wait_send()

    any_spec = pl.BlockSpec(memory_space=pl.ANY)
    return pl.pallas_call(
        body, name="proj_gather",
        grid_spec=pltpu.PrefetchScalarGridSpec(
            num_scalar_prefetch=1, grid=(len(plan), steps),
            in_specs=[pl.BlockSpec((tm, D), lambda jj, i, o: (jnp.where(jj == 0, i, steps - 1), 0)),
                      pl.BlockSpec((None, 1, chunk), lambda jj, i, o: (o[jj], 0, 0))] + [any_spec] * n,
            out_specs=[pl.BlockSpec((tm, chunk), lambda jj, i, o: (i, o[jj]))] + [any_spec] * n,
            scratch_shapes=[pltpu.VMEM((t, D), BF16), pltpu.VMEM((2, D, chunk), BF16), pltpu.SemaphoreType.DMA((4,)),
                            pltpu.SemaphoreType.DMA((4 * n,)), pltpu.SemaphoreType.DMA((4 * n,)),
                            pltpu.SemaphoreType.DMA((2 * n,)), pltpu.SemaphoreType.DMA((2 * n,)),
                            pltpu.SemaphoreType.DMA((3 * n,)), pltpu.SemaphoreType.DMA((3 * n,)),
                            pltpu.SemaphoreType.DMA((6,)), pltpu.SemaphoreType.DMA((6,))]),
        out_shape=[jax.ShapeDtypeStruct((t, D_IN), F32)] + [jax.ShapeDtypeStruct(b.shape, b.dtype) for b in bufs],
        input_output_aliases={3 + a: 1 + a for a in range(n)},
        compiler_params=_params("arbitrary", "arbitrary"),
    )(order, x, b_in.reshape(D_IN // chunk, 1, chunk), *bufs)


def _share_results(bufs, small):
    n = len(bufs)

    def body(*refs):
        outs, small_out = refs[n + 1:2 * n + 1], refs[2 * n + 1]
        send, recv, ssend, srecv = refs[2 * n + 2:]
        x, y, c, k = _position()
        cps = []
        for a in range(n):
            cp = pltpu.make_async_remote_copy(
                src_ref=outs[a].at[c], dst_ref=outs[a].at[c], send_sem=send.at[a], recv_sem=recv.at[a],
                device_id=(x, y, 1 - c), device_id_type=MESH)
            cp.start()
            cps.append(cp)
        waits = []
        for p in range(1, 8):
            px, py, pc = x ^ (p >> 2), y ^ ((p >> 1) & 1), c ^ (p & 1)
            cp = pltpu.make_async_remote_copy(
                src_ref=small_out.at[k, c], dst_ref=small_out.at[k, c], send_sem=ssend.at[p - 1],
                recv_sem=srecv.at[p - 1], device_id=(px, py, pc), device_id_type=MESH)
            cp.start()
            cps.append(cp)
            waits.append(pltpu.make_async_remote_copy(
                src_ref=small_out.at[2 * px + py, pc], dst_ref=small_out.at[2 * px + py, pc], send_sem=ssend.at[p - 1],
                recv_sem=srecv.at[p - 1], device_id=(px, py, pc), device_id_type=MESH))
        for a in range(n):
            pltpu.make_async_remote_copy(
                src_ref=outs[a].at[1 - c], dst_ref=outs[a].at[1 - c], send_sem=send.at[a], recv_sem=recv.at[a],
                device_id=(x, y, 1 - c), device_id_type=MESH).wait_recv()
        for w in waits:
            w.wait_recv()
        for cp in cps:
            cp.wait_send()

    return pl.pallas_call(
        body, name="rs_share_results",
        in_specs=_any_specs(n + 1), out_specs=_any_specs(n + 1),
        out_shape=[jax.ShapeDtypeStruct(b.shape, b.dtype) for b in bufs + [small]],
        scratch_shapes=[pltpu.SemaphoreType.DMA((n,)), pltpu.SemaphoreType.DMA((n,)),
                        pltpu.SemaphoreType.DMA((7,)), pltpu.SemaphoreType.DMA((7,))],
        input_output_aliases={a: a for a in range(n + 1)},
    )(*bufs, small)


def _row_tile(r, c):
    t = max(8, min(r, (1 << 18) // c))
    while r % t:
        t //= 2
    return t


def _add_devices(grads, lands, sibs):
    n = len(grads)

    def body(kc_ref, *refs):
        for g_ref, l_ref, s_ref, f_ref in zip(refs[:n], refs[n:2 * n], refs[2 * n:3 * n], refs[3 * n:]):
            f = g_ref[...] + s_ref[...]
            for i in range(l_ref.shape[0]):
                f = f + l_ref[i].astype(F32)
            f_ref[...] = f

    shapes = [g.shape[2:] for g in grads]
    out_specs = [pl.BlockSpec((None,) + sh, lambda i, kc: (kc[1], 0, 0)) for sh in shapes[:-1]]
    out_specs.append(pl.BlockSpec((None, None) + shapes[-1], lambda i, kc: (kc[0], kc[1], 0, 0)))
    out_shape = [jax.ShapeDtypeStruct((2,) + sh, F32) for sh in shapes[:-1]]
    out_shape.append(jax.ShapeDtypeStruct((N_CHIPS, 2) + shapes[-1], F32))
    return pl.pallas_call(
        body, name="rs_add_devices",
        grid_spec=pltpu.PrefetchScalarGridSpec(
            num_scalar_prefetch=1, grid=(1,),
            in_specs=[pl.BlockSpec((None, None) + sh, lambda i, kc: (kc[1], kc[0], 0, 0)) for sh in shapes]
            + [pl.BlockSpec(l.shape, lambda i, kc: (0, 0, 0)) for l in lands]
            + [pl.BlockSpec(sh, lambda i, kc: (0, 0)) for sh in shapes],
            out_specs=out_specs),
        out_shape=out_shape,
        compiler_params=_params("arbitrary"),
    )(jnp.concatenate([_chip_index(), _core_index()]), *grads, *lands, *sibs)


def _add_chips(q, b2):
    r, c = q.shape
    t = _row_tile(r, c)

    def body(c_ref, q_ref, b_ref, f_ref):
        f_ref[...] = ((q_ref[...] + b_ref[0].astype(F32)) + b_ref[1].astype(F32)) + b_ref[2].astype(F32)

    return pl.pallas_call(
        body, name="rs_add_chips",
        grid_spec=pltpu.PrefetchScalarGridSpec(
            num_scalar_prefetch=1, grid=(r // t,),
            in_specs=[pl.BlockSpec((t, c), lambda i, cr: (i, 0)), pl.BlockSpec((3, t, c), lambda i, cr: (0, i, 0))],
            out_specs=pl.BlockSpec((None, t, c), lambda i, cr: (cr[0], i, 0))),
        out_shape=jax.ShapeDtypeStruct((2, r, c), F32),
        compiler_params=_params("parallel"),
    )(_core_index(), q, b2)


def _adamw_math(w, g, m, v):
    m = ADAM_B1 * m + (1.0 - ADAM_B1) * g
    v = ADAM_B2 * v + (1.0 - ADAM_B2) * (g * g)
    m_hat = m / (1.0 - ADAM_B1 ** ADAM_STEP)
    v_hat = v / (1.0 - ADAM_B2 ** ADAM_STEP)
    delta = -ADAM_LR * (m_hat / (jnp.sqrt(v_hat) + ADAM_EPS) + ADAM_WD * w)
    return delta, m, v


def _adamw_group(quads, conv):
    n = len(quads)
    r, c = quads[0][0].shape

    def body(*refs):
        ins, outs = refs[:4 * (n + 1)], refs[4 * (n + 1):]
        for i in range(n + 1):
            w_ref, g_ref, m_ref, v_ref = ins[4 * i:4 * i + 4]
            res = _adamw_math(w_ref[...], g_ref[...], m_ref[...], v_ref[...])
            for o_ref, val in zip(outs[3 * i:3 * i + 3], res):
                o_ref[...] = val

    half = pl.BlockSpec((r // 2, c), lambda i: (i, 0))
    whole = pl.BlockSpec(conv[0].shape, lambda i: (0, 0))
    res = pl.pallas_call(
        body, name="adamw_group", grid=(2,),
        in_specs=[half] * (4 * n) + [whole] * 4, out_specs=[half] * (3 * n) + [whole] * 3,
        out_shape=[jax.ShapeDtypeStruct((r, c), F32)] * (3 * n) + [jax.ShapeDtypeStruct(conv[0].shape, F32)] * 3,
        compiler_params=_params("arbitrary"),
    )(*[a for quad in quads for a in quad], *conv)
    return [res[3 * i:3 * i + 3] for i in range(n + 1)]


def _adamw(w, g, m, v):
    r, c = w.shape
    t = _row_tile(r, c) if r % 8 == 0 else r

    def body(w_ref, g_ref, m_ref, v_ref, d_ref, nm_ref, nv_ref):
        d_ref[...], nm_ref[...], nv_ref[...] = _adamw_math(w_ref[...], g_ref[...], m_ref[...], v_ref[...])

    spec = pl.BlockSpec((t, c), lambda i: (i, 0))
    return pl.pallas_call(
        body, name="adamw", grid=(r // t,), in_specs=[spec] * 4, out_specs=[spec] * 3,
        out_shape=[jax.ShapeDtypeStruct((r, c), F32)] * 3, compiler_params=_params("parallel"),
    )(w, g, m, v)


ROW_B_IN = 0
ROW_VECS = 8
ROW_LOSS = 16
ROW_B_SPATIAL = 24
ROW_W_SPATIAL = 32
SMALL_ROWS = 192
N_VECS = 8


def _pack_small(acc_f, acc_b, dbin_a, dbin_b, dcw8, dws, dbsp):
    cols = D // N_CHIPS

    def body(af_ref, ab_ref, da_ref, db_ref, cw_ref, ws_ref, bs_ref, o_ref, gc_ref):
        o_ref[...] = jnp.zeros_like(o_ref)
        for j in range(D_IN // D):
            src = da_ref if j < 2 else db_ref
            o_ref[ROW_B_IN + j:ROW_B_IN + j + 1, :] = src[0:1, j * D:(j + 1) * D]
        dcw = jnp.sum(cw_ref[...].reshape(HALO, SUBLANES, D), axis=1)
        o_ref[ROW_VECS:ROW_VECS + 1, :] = dcw[CONV_K:CONV_K + 1]
        o_ref[ROW_VECS + 1:ROW_VECS + 5, :] = ab_ref[0:4, :]
        o_ref[ROW_VECS + 5:ROW_VECS + 6, :] = af_ref[2:3, :]
        o_ref[ROW_VECS + 6:ROW_VECS + 8, :] = af_ref[0:2, :]
        o_ref[ROW_LOSS:ROW_LOSS + 1, :] = af_ref[3:4, :]
        head = lax.broadcasted_iota(jnp.int32, (N_GROUPS, D), 0)
        lane = lax.broadcasted_iota(jnp.int32, (N_GROUPS, D), 1)
        indicator = jnp.where(lane // GROUP_W == head, 1.0, 0.0)
        o_ref[ROW_B_SPATIAL:ROW_B_SPATIAL + N_GROUPS, 0:CHUNK] = lax.dot_general(
            indicator, bs_ref[...], (((1,), (1,)), ((), ())), precision=lax.Precision.HIGHEST, preferred_element_type=F32)
        t_idx = lax.broadcasted_iota(jnp.int32, (CHUNK, D), 0)
        s_idx = lax.broadcasted_iota(jnp.int32, (CHUNK, D), 1) % CHUNK
        o_ref[ROW_W_SPATIAL:ROW_W_SPATIAL + CHUNK, :] = jnp.where(s_idx <= t_idx, ws_ref[...], 0.0)
        for h in range(2):
            for j in range(N_CHIPS):
                gc_ref[h, j] = dcw[h * (HALO // 2):(h + 1) * (HALO // 2), j * cols:(j + 1) * cols]

    ins = [acc_f, acc_b, dbin_a, dbin_b, dcw8, dws, dbsp]
    return pl.pallas_call(
        body, name="pack_small",
        in_specs=[_full(a.shape) for a in ins],
        out_specs=[_full((SMALL_ROWS, D)), _full((2, N_CHIPS, HALO // 2, cols))],
        out_shape=[jax.ShapeDtypeStruct((SMALL_ROWS, D), F32), jax.ShapeDtypeStruct((2, N_CHIPS, HALO // 2, cols), F32)],
        compiler_params=_params(),
    )(*ins)


def _adamw_small(parts, vecs, b_in, w_spatial, b_spatial):
    triples = list(vecs) + [b_in, w_spatial, b_spatial]
    n_in = 3 * len(triples)

    def body(p_ref, *refs):
        ins = [refs[3 * i:3 * i + 3] for i in range(len(triples))]
        outs = [refs[n_in + 4 * i:n_in + 4 * i + 4] for i in range(len(triples))]
        loss_ref, g_ref = refs[n_in + 4 * len(triples):]
        rows = SMALL_ROWS // 8
        for k in range(N_CHIPS):
            for core in range(2):
                g_ref[(core * N_CHIPS + k) * rows:(core * N_CHIPS + k + 1) * rows, :] = p_ref[2 * k + core]

        def step(g, wmv, out, get, put):
            d, nm, nv = _adamw_math(get(wmv[0]), g, get(wmv[1]), get(wmv[2]))
            for o, val in zip(out, (g, d, nm, nv)):
                put(o, val)

        for i in range(N_VECS):
            step(g_ref[ROW_VECS + i:ROW_VECS + i + 1, :], ins[i], outs[i],
                 lambda r: r[...].reshape(1, D), lambda o, val: o.__setitem__(Ellipsis, val.reshape(D)))
        for j in range(D_IN // D):
            piece = pl.ds(j * D, D)
            step(g_ref[ROW_B_IN + j:ROW_B_IN + j + 1, :], ins[N_VECS], outs[N_VECS],
                 lambda r: r[piece].reshape(1, D), lambda o, val: o.__setitem__(piece, val.reshape(D)))
        for h in range(N_GROUPS):
            step(g_ref[ROW_W_SPATIAL:ROW_W_SPATIAL + CHUNK, h * CHUNK:(h + 1) * CHUNK], ins[N_VECS + 1], outs[N_VECS + 1],
                 lambda r: r[h], lambda o, val: o.__setitem__(h, val))
        step(g_ref[ROW_B_SPATIAL:ROW_B_SPATIAL + N_GROUPS, 0:CHUNK], ins[N_VECS + 2], outs[N_VECS + 2],
             lambda r: r[...], lambda o, val: o.__setitem__(Ellipsis, val))
        lanes = g_ref[ROW_LOSS:ROW_LOSS + 1, :]
        loss_ref[...] = jnp.broadcast_to(jnp.sum(lanes, axis=1, keepdims=True), (8, 128))

    flat = [a for tr in triples for a in tr]
    out_shape = [jax.ShapeDtypeStruct(tr[0].shape, F32) for tr in triples for _ in range(4)]
    out_shape.append(jax.ShapeDtypeStruct((8, 128), F32))
    res = pl.pallas_call(
        body, name="adamw_small",
        in_specs=[_full(parts.shape)] + [_full(a.shape) for a in flat],
        out_specs=[_full(o.shape) for o in out_shape],
        out_shape=out_shape,
        scratch_shapes=[pltpu.VMEM((SMALL_ROWS, D), F32)],
        compiler_params=_params(),
    )(parts, *flat)
    return [res[4 * i:4 * i + 4] for i in range(len(triples))], res[-1]


SUBLANES = 8
SHIFT_ROWS = HALO - SUBLANES


def _shifted_copies(src_ref, sh_ref, cs, tm):
    for p in range(1, SUBLANES):
        sh_ref[p - 1] = src_ref[pl.ds(p, tm + SHIFT_ROWS), cs]


def _tap(src_ref, sh_ref, cs, offset, start, rows):
    p, q = offset % SUBLANES, offset // SUBLANES
    if p == 0:
        return src_ref[pl.ds(start + SUBLANES * q, rows), cs]
    return sh_ref[p - 1, pl.ds(start + SUBLANES * q, rows), :]


def _conv_taps(src_ref, sh_ref, w_ref, first_offset, step, bias, dst_ref, tm):
    rows = 64
    for g in range(N_GROUPS):
        cs = slice(g * GROUP_W, (g + 1) * GROUP_W)
        _shifted_copies(src_ref, sh_ref, cs, tm)
        for rb in range(tm // rows):
            acc = jnp.zeros((rows, GROUP_W), F32) + (bias[:, cs] if bias is not None else 0.0)
            for k in range(CONV_K):
                acc = acc + w_ref[k:k + 1, cs] * _tap(src_ref, sh_ref, cs, first_offset + step * k, rb * rows, rows)
            dst_ref[rb * rows:(rb + 1) * rows, cs] = acc


def _conv_weight_grad(d_ref, src_ref, sh_ref, first_offset, acc_ref, tm):
    rows = 64
    for g in range(N_GROUPS):
        cs = slice(g * GROUP_W, (g + 1) * GROUP_W)
        _shifted_copies(src_ref, sh_ref, cs, tm)
        for rb in range(tm // rows):
            d = d_ref[rb * rows:(rb + 1) * rows, cs]
            for k in range(CONV_K):
                prod = d * _tap(src_ref, sh_ref, cs, first_offset + k, rb * rows, rows)
                acc_ref[SUBLANES * k:SUBLANES * (k + 1), cs] += jnp.sum(
                    prod.reshape(rows // SUBLANES, SUBLANES, GROUP_W), axis=0)


def _spatial_mix(w_ref, v_bf, tm):
    rows = []
    for q in range(tm // CHUNK):
        cols = [_dot(w_ref[h], v_bf[q * CHUNK:(q + 1) * CHUNK, h * GROUP_W:(h + 1) * GROUP_W])
                for h in range(N_GROUPS)]
        rows.append(jnp.concatenate(cols, axis=1))
    return jnp.concatenate(rows, axis=0)


def _group_norm_fwd(h1, gn_g, gn_b):
    xhat, rstd = [], []
    for g in range(N_GROUPS):
        xh, rs = _norm_stats(h1[:, g * GROUP_W:(g + 1) * GROUP_W])
        xhat.append(xh)
        rstd.append(rs)
    xhat = jnp.concatenate(xhat, axis=1)
    return xhat * gn_g + gn_b, xhat, rstd


def _layer_tiles(p, x, tgt, wpa, wpb, wo, convw, vecs, ws, wst, bsp, tiles_per_seq):
    t = x.shape[0]
    tm = LAYER_TILE
    hb = tm // HALO

    def body(p_ref, ph_ref, x_ref, t_ref, wpa_ref, wpb_ref, wo_ref, cw_ref, vec_ref, ws_ref, wst_ref, bsp_ref,
             h3_ref, s_ref, mx_ref, dr_ref, drb_ref, xt_ref, dh1_ref, dp_ref, dya_ref, dyb_ref,
             accf_ref, accb_ref, dbin_ref, dws_ref, dbsp_ref, he_ref, sh_ref, h1_ref):
        i = pl.program_id(0)
        xt_ref[...] = x_ref[...].T.astype(BF16)
        conv_b, gn_g, gn_b, lnv_g, lnv_b, b_o, lno_g, lno_b = [vec_ref[j:j + 1, :] for j in range(8)]

        @pl.when(i == 0)
        def _():
            for ref in (accf_ref, accb_ref, dbin_ref, dws_ref, dbsp_ref):
                ref[...] = jnp.zeros_like(ref)

        keep = jnp.where(i % tiles_per_seq == 0, 0.0, 1.0)
        he_ref[0:HALO, :] = ph_ref[:, 0:D] * _sigmoid(ph_ref[:, D:2 * D]) * keep
        he_ref[HALO:, :] = p_ref[:, 0:D] * _sigmoid(p_ref[:, D:2 * D])
        _conv_taps(he_ref, sh_ref, cw_ref, HALO - (CONV_K - 1), 1, conv_b, h1_ref, tm)
        h2, xhat_a, rstd_a = _group_norm_fwd(h1_ref[...], gn_g, gn_b)
        sg = _sigmoid(h2)
        a_gate = p_ref[:, 2 * D:3 * D]
        sa = _sigmoid(a_gate)
        silu_h2, silu_a = h2 * sg, a_gate * sa
        h3 = (silu_h2 * silu_a).astype(BF16)
        h3_ref[...] = h3
        ya = _dot(h3, wpa_ref[...])

        u, du_dpre = _gelu_and_grad(p_ref[:, 3 * D:4 * D])
        v0, dv_dpre = _gelu_and_grad(p_ref[:, 4 * D:5 * D])
        vhat, vrstd = _norm_stats(v0)
        v1 = (vhat * lnv_g + lnv_b).astype(BF16)
        b_gate = p_ref[:, 5 * D:6 * D]
        sb = _sigmoid(b_gate)
        silu_b = b_gate * sb
        vmix = _spatial_mix(ws_ref, v1, tm) + jnp.concatenate([bsp_ref[...]] * (tm // CHUNK), axis=0)
        s = (u * vmix * silu_b).astype(BF16)
        s_ref[...] = s
        yb = _dot(s, wpb_ref[...])

        ga = _sigmoid(p_ref[:, 6 * D:7 * D])
        gb = _sigmoid(p_ref[:, 7 * D:8 * D])
        mixed = (ga * ya + gb * yb).astype(BF16)
        mx_ref[...] = mixed
        r = ALPHA * x_ref[...] + (_dot(mixed, wo_ref[...]) + b_o)
        xhat_o, rstd_o = _norm_stats(r)
        err = (xhat_o * lno_g + lno_b) - t_ref[...]
        dout = err * (1.0 / D)
        dr = _norm_bwd(dout * lno_g, xhat_o, rstd_o)
        drb = dr.astype(BF16)
        dr_ref[...] = dr
        drb_ref[...] = drb
        accf_ref[0:1, :] += _colsum(dout * xhat_o)
        accf_ref[1:2, :] += _colsum(dout)
        accf_ref[2:3, :] += _colsum(dr)
        accf_ref[3:4, :] += _colsum(err * err) * (0.5 / D)

        def emit(block, val):
            dbin_ref[0:1, block * D:(block + 1) * D] += _colsum(val)
            dp_ref[:, block * D:(block + 1) * D] = val.astype(BF16)

        dp_ref[:, 0:2 * D] = jnp.zeros((tm, 2 * D), BF16)
        dmixed = _dot_nt(drb, wo_ref[...])
        dya = (dmixed * ga).astype(BF16)
        dyb = (dmixed * gb).astype(BF16)
        dya_ref[...] = dya
        dyb_ref[...] = dyb
        emit(6, dmixed * ya * (ga * (1.0 - ga)))
        emit(7, dmixed * yb * (gb * (1.0 - gb)))

        dh3 = _dot_nt(dya, wpa_ref[...])
        dh2 = dh3 * silu_a * (sg * (1.0 + h2 * (1.0 - sg)))
        emit(2, dh3 * silu_h2 * (sa * (1.0 + a_gate * (1.0 - sa))))
        accb_ref[0:1, :] += _colsum(dh2 * xhat_a)
        accb_ref[1:2, :] += _colsum(dh2)
        dxhat = dh2 * gn_g
        for g in range(N_GROUPS):
            cs = slice(g * GROUP_W, (g + 1) * GROUP_W)
            dh1_ref[:, cs] = _norm_bwd(dxhat[:, cs], xhat_a[:, cs], rstd_a[g])

        ds = _dot_nt(dyb, wpb_ref[...])
        emit(3, ds * vmix * silu_b * du_dpre)
        emit(5, ds * u * vmix * (sb * (1.0 + b_gate * (1.0 - sb))))
        dvmix = ds * u * silu_b
        dvmix_bf = dvmix.astype(BF16)
        for q in range(tm // CHUNK):
            dbsp_ref[...] += dvmix[q * CHUNK:(q + 1) * CHUNK, :]
            for h in range(N_GROUPS):
                blk = (slice(q * CHUNK, (q + 1) * CHUNK), slice(h * GROUP_W, (h + 1) * GROUP_W))
                dws_ref[:, h * GROUP_W:(h + 1) * GROUP_W] += _dot_nt(dvmix_bf[blk], v1[blk])
        dv1 = _spatial_mix(wst_ref, dvmix_bf, tm)
        accb_ref[2:3, :] += _colsum(dv1 * vhat)
        accb_ref[3:4, :] += _colsum(dv1)
        emit(4, _norm_bwd(dv1 * lnv_g, vhat, vrstd) * dv_dpre)

    tile = lambda w: pl.BlockSpec((tm, w), lambda i: (i, 0))
    f32_out = jax.ShapeDtypeStruct((t, D), F32)
    bf_out = jax.ShapeDtypeStruct((t, D), BF16)
    return pl.pallas_call(
        body, name="layer_tiles", grid=(t // tm,),
        in_specs=[tile(D_IN),
                  pl.BlockSpec((HALO, 2 * D), lambda i: (jnp.maximum(i * hb - 1, 0), 0)),
                  tile(D), tile(D), _resident((D, D)), _resident((D, D)), _resident((D, D)), _full((HALO, D)), _full((8, D)),
                  _full((N_GROUPS, CHUNK, CHUNK)), _full((N_GROUPS, CHUNK, CHUNK)), _full((CHUNK, D))],
        out_specs=[tile(D)] * 5 + [pl.BlockSpec((D, tm), lambda i: (0, i)), tile(D), tile(D_IN), tile(D), tile(D),
                   _full((8, D)), _full((8, D)), _full((8, D_IN)), _full((CHUNK, D)), _full((CHUNK, D))],
        out_shape=[bf_out, bf_out, bf_out, f32_out, bf_out, jax.ShapeDtypeStruct((D, t), BF16), f32_out,
                   jax.ShapeDtypeStruct((t, D_IN), BF16), bf_out, bf_out,
                   jax.ShapeDtypeStruct((8, D), F32), jax.ShapeDtypeStruct((8, D), F32),
                   jax.ShapeDtypeStruct((8, D_IN), F32), jax.ShapeDtypeStruct((CHUNK, D), F32),
                   jax.ShapeDtypeStruct((CHUNK, D), F32)],
        scratch_shapes=[pltpu.VMEM((tm + HALO, D), F32), pltpu.VMEM((SUBLANES - 1, tm + SHIFT_ROWS, GROUP_W), F32),
                        pltpu.VMEM((tm, D), F32)],
        compiler_params=_params("arbitrary"),
    )(p, p, x, tgt, wpa, wpb, wo, convw, vecs, ws, wst, bsp)


def _conv_backward(dh1, p, dp, convw, pairs, tiles_per_seq):
    t = dh1.shape[0]
    tm = TOKEN_TILE
    hb = tm // HALO
    last = t // HALO - 1
    n_sq = len(pairs)
    span = 2
    rows = D // 8

    def body(dh1_ref, dnext_ref, p_ref, ph_ref, cw_ref, dp_in_ref, *refs):
        del dp_in_ref
        sq_in = refs[:2 * n_sq]
        dp_ref, dcw_ref, dbin_ref = refs[2 * n_sq:2 * n_sq + 3]
        sq_out = refs[2 * n_sq + 3:3 * n_sq + 3]
        sq_wire = refs[3 * n_sq + 3:4 * n_sq + 3]
        de_ref, he_ref, dh0_ref, sh_ref, acc_ref, wire_ref, sq_sem, wire_sem = refs[4 * n_sq + 3:]
        i = pl.program_id(0)

        @pl.when(i == 0)
        def _():
            dcw_ref[...] = jnp.zeros_like(dcw_ref)
            dbin_ref[...] = jnp.zeros_like(dbin_ref)
            acc_ref[...] = jnp.zeros_like(acc_ref)

        @pl.when(i % span == span - 1)
        def _():
            for a in range(n_sq):
                acc_ref[a] += _dot_tn(sq_in[2 * a][...], sq_in[2 * a + 1][...])

        keep_next = jnp.where(i % tiles_per_seq == tiles_per_seq - 1, 0.0, 1.0)
        de_ref[0:tm, :] = dh1_ref[...]
        de_ref[tm:, :] = dnext_ref[...] * keep_next
        _conv_taps(de_ref, sh_ref, cw_ref, CONV_K - 1, -1, None, dh0_ref, tm)

        keep_prev = jnp.where(i % tiles_per_seq == 0, 0.0, 1.0)
        sg = _sigmoid(p_ref[:, D:2 * D])
        val = p_ref[:, 0:D]
        he_ref[0:HALO, :] = ph_ref[:, 0:D] * _sigmoid(ph_ref[:, D:2 * D]) * keep_prev
        he_ref[HALO:, :] = val * sg
        _conv_weight_grad(dh1_ref, he_ref, sh_ref, HALO - (CONV_K - 1), dcw_ref, tm)
        dcw_ref[SUBLANES * CONV_K:, :] += jnp.sum(dh1_ref[...].reshape(tm // SUBLANES, SUBLANES, D), axis=0)

        dh0 = dh0_ref[...]
        dval = dh0 * sg
        dglu = dh0 * val * (sg * (1.0 - sg))
        dbin_ref[0:1, 0:D] += _colsum(dval)
        dbin_ref[0:1, D:2 * D] += _colsum(dglu)
        dp_ref[:, 0:D] = dval.astype(BF16)
        dp_ref[:, D:2 * D] = dglu.astype(BF16)

        @pl.when(i == t // tm - 1)
        def _():
            cps = [pltpu.make_async_copy(acc_ref.at[a, pl.ds((2 * j + h) * rows, rows)], sq_out[a].at[h, j],
                                         sq_sem.at[(a * N_CHIPS + j) * 2 + h])
                   for a in range(n_sq) for j in range(N_CHIPS) for h in range(2)]
            for cp in cps:
                cp.start()
            for a in range(n_sq):
                wire_ref[...] = acc_ref[a].astype(BF16)
                narrow = [pltpu.make_async_copy(wire_ref.at[pl.ds((2 * j + h) * rows, rows)], sq_wire[a].at[h, j],
                                                wire_sem.at[2 * j + h]) for j in range(N_CHIPS) for h in range(2)]
                for cp in narrow:
                    cp.start()
                for cp in narrow:
                    cp.wait()
            for cp in cps:
                cp.wait()

    any_spec = pl.BlockSpec(memory_space=pl.ANY)
    wide = pl.BlockSpec((span * tm, D), lambda i: (i // span, 0))
    return pl.pallas_call(
        body, name="conv_backward", grid=(t // tm,),
        in_specs=[pl.BlockSpec((tm, D), lambda i: (i, 0)),
                  pl.BlockSpec((HALO, D), lambda i: (jnp.minimum((i + 1) * hb, last), 0)),
                  pl.BlockSpec((tm, 2 * D), lambda i: (i, 0)),
                  pl.BlockSpec((HALO, 2 * D), lambda i: (jnp.maximum(i * hb - 1, 0), 0)),
                  _full((HALO, D)), any_spec] + [wide] * (2 * n_sq),
        out_specs=[pl.BlockSpec((tm, 2 * D), lambda i: (i, 0)), _full((SUBLANES * HALO, D)), _full((8, 2 * D))]
        + [any_spec] * (2 * n_sq),
        out_shape=[jax.ShapeDtypeStruct(dp.shape, BF16), jax.ShapeDtypeStruct((SUBLANES * HALO, D), F32),
                   jax.ShapeDtypeStruct((8, 2 * D), F32)]
        + [jax.ShapeDtypeStruct((2, N_CHIPS, rows, D), F32)] * n_sq
        + [jax.ShapeDtypeStruct((2, N_CHIPS, rows, D), BF16)] * n_sq,
        scratch_shapes=[pltpu.VMEM((tm + HALO, D), F32), pltpu.VMEM((tm + HALO, D), F32), pltpu.VMEM((tm, D), F32),
                        pltpu.VMEM((SUBLANES - 1, tm + SHIFT_ROWS, GROUP_W), F32), pltpu.VMEM((n_sq, D, D), F32),
                        pltpu.VMEM((D, D), BF16), pltpu.SemaphoreType.DMA((n_sq * N_CHIPS * 2,)),
                        pltpu.SemaphoreType.DMA((N_CHIPS * 2,))],
        input_output_aliases={5: 0},
        compiler_params=_params("arbitrary"),
    )(dh1, dh1, p, p, convw, dp, *[a for pair in pairs for a in pair])


def _grad_in_and_x(xt, dp, w4, dr, wires, grads):
    t = dr.shape[0]
    tm = TOKEN_TILE
    half, tn = D // 2, 512
    nb = W_BLOCK // tn
    n_w, n_x = 2 * N_CHIPS * nb, t // tm
    ns = len(grads)
    xi, yi, ci = lax.axis_index("x"), lax.axis_index("y"), lax.axis_index("c")
    others = [2 * (1 - xi) + yi, 2 * xi + (1 - yi), 2 * (1 - xi) + (1 - yi)]
    blocks = others + others + [2 * xi + yi] * 2
    halves = [1 - ci] * 3 + [ci] * 3 + [1 - ci, ci]
    table = jnp.stack([jnp.stack([b * nb + n for b in blocks for n in range(nb)]),
                       jnp.stack([h for h in halves for _ in range(nb)])]).astype(jnp.int32)

    def body(tab_ref, xt_ref, dpc_ref, dpr_ref, w_ref, dr_ref, *refs):
        parts, fulls = refs[:ns], refs[ns:2 * ns]
        dx_ref, qk_ref, b2_ref, b1_ref, wire_ref = refs[2 * ns:2 * ns + 5]
        lands, sibs = refs[2 * ns + 5:3 * ns + 5], refs[3 * ns + 5:4 * ns + 5]
        (g_ref, st_ref, sb_ref, tmp_ref, d2d_send, d2d_recv, ici_send, ici_recv, own_sem, tmp_sem, wire_sem,
         p_send, p_recv, s_send, s_recv) = refs[4 * ns + 5:]
        s = pl.program_id(0)
        x_, y_, c, k = _position()
        chips = _other_chips(x_, y_)
        n = s % nb
        grp = s // nb
        cols = pl.ds(pl.multiple_of(n * tn, tn), tn)

        def part(a, r, core):
            cx, cy = chips[r]
            return pltpu.make_async_remote_copy(
                src_ref=parts[a].at[core, 2 * cx + cy], dst_ref=lands[a].at[2 * r + c],
                send_sem=p_send.at[6 * a + 2 * r + core], recv_sem=p_recv.at[6 * a + 2 * r + c],
                device_id=(cx, cy, core), device_id_type=MESH)

        def landed(a, r, core):
            cx, cy = chips[r]
            return pltpu.make_async_remote_copy(
                src_ref=lands[a].at[2 * r + core], dst_ref=lands[a].at[2 * r + core],
                send_sem=p_send.at[6 * a + 2 * r + core], recv_sem=p_recv.at[6 * a + 2 * r + core],
                device_id=(cx, cy, core), device_id_type=MESH)

        def to_sibling_whole(a):
            return pltpu.make_async_remote_copy(
                src_ref=fulls[a].at[1 - c, k], dst_ref=sibs[a], send_sem=s_send.at[a], recv_sem=s_recv.at[a],
                device_id=(x_, y_, 1 - c), device_id_type=MESH)

        def to_sibling(slot, land):
            return pltpu.make_async_remote_copy(
                src_ref=st_ref.at[slot], dst_ref=b1_ref.at[land, :, cols], send_sem=d2d_send.at[slot],
                recv_sem=d2d_recv.at[land * nb + n], device_id=(x_, y_, 1 - c), device_id_type=MESH)

        def to_chip(r):
            cx, cy = chips[r]
            return pltpu.make_async_remote_copy(
                src_ref=wire_ref.at[r, :, cols], dst_ref=b2_ref.at[r, :, cols], send_sem=ici_send.at[r],
                recv_sem=ici_recv.at[r], device_id=(cx, cy, c), device_id_type=MESH)

        def all_of_chip(r):
            cx, cy = chips[r]
            return pltpu.make_async_remote_copy(
                src_ref=wire_ref.at[r], dst_ref=b2_ref.at[r], send_sem=ici_send.at[r],
                recv_sem=ici_recv.at[r], device_id=(cx, cy, c), device_id_type=MESH)

        def to_result(slot):
            return pltpu.make_async_copy(st_ref.at[slot], qk_ref.at[:, cols], own_sem.at[slot])

        def sibling_piece(land):
            return pltpu.make_async_copy(b1_ref.at[land, :, cols], tmp_ref, tmp_sem)

        @pl.when(s == 0)
        def _():
            for a in range(ns):
                to_sibling_whole(a).start()
                for r in range(3):
                    for core in range(2):
                        part(a, r, core).start()

        own_half = ((grp >= 3) & (grp <= 5)) | (grp == 7)
        land = jnp.where(grp == 7, 3, grp - 3)

        @pl.when(own_half)
        def _():
            to_sibling(0, land).wait_recv()
            sibling_piece(land).start()

        @pl.when(s < n_w)
        def _():
            g_ref[...] = _dot(xt_ref[tab_ref[1, s]], dpc_ref[...])

        @pl.when(own_half)
        def _():
            sibling_piece(land).wait()

        for g in range(2 * N_CHIPS):
            @pl.when(grp == g)
            def _(g=g):
                if g in (0, 1, 2, 6):
                    use = s if g < 3 else 3 * nb + n
                    slot = use % 2

                    @pl.when(use >= 2)
                    def _():
                        to_sibling(slot, 0).wait_send()

                    st_ref[slot] = g_ref[...]
                    to_sibling(slot, min(g, 3)).start()
                elif g in (3, 4, 5):
                    sb_ref[...] = (g_ref[...] + tmp_ref[...]).astype(BF16)
                    stage = pltpu.make_async_copy(sb_ref, wire_ref.at[g - 3, :, cols], wire_sem)
                    stage.start()
                    stage.wait()
                    to_chip(g - 3).start()
                else:
                    slot = n % 2
                    piece = g_ref[...] + tmp_ref[...]

                    @pl.when(n < 2)
                    def _():
                        to_sibling(slot, 0).wait_send()

                    @pl.when(n >= 2)
                    def _():
                        to_result(slot).wait()

                    st_ref[slot] = piece
                    to_result(slot).start()

        @pl.when(s >= n_w)
        def _():
            acc = ALPHA * dr_ref[...]
            for j in range(N_CHIPS):
                acc = acc + _dot_nt(dpr_ref[:, j * W_BLOCK:(j + 1) * W_BLOCK], w_ref[j])
            dx_ref[...] = acc

        @pl.when(s == n_w + n_x - 1)
        def _():
            for slot in range(2):
                to_result(slot).wait()
            for r in range(3):
                all_of_chip(r).wait_recv()
                all_of_chip(r).wait_send()
            for a in range(ns):
                to_sibling_whole(a).wait_recv()
                to_sibling_whole(a).wait_send()
                for r in range(3):
                    for core in range(2):
                        landed(a, r, core).wait_recv()
                        part(a, r, core).wait_send()

    any_spec = pl.BlockSpec(memory_space=pl.ANY)
    tile = lambda s, tab: (jnp.maximum(s - n_w, 0), 0)
    return pl.pallas_call(
        body, name="grad_in_and_x",
        grid_spec=pltpu.PrefetchScalarGridSpec(
            num_scalar_prefetch=1, grid=(n_w + n_x,),
            in_specs=[pl.BlockSpec((2, half, t), lambda s, tab: (0, 0, 0), pipeline_mode=pl.Buffered(1)),
                      pl.BlockSpec((t, tn), lambda s, tab: (0, tab[0, jnp.minimum(s, n_w - 1)])),
                      pl.BlockSpec((tm, D_IN), tile),
                      pl.BlockSpec((N_CHIPS, D, W_BLOCK), lambda s, tab: (0, 0, 0), pipeline_mode=pl.Buffered(1)),
                      pl.BlockSpec((tm, D), tile)] + [any_spec] * (2 * ns),
            out_specs=[pl.BlockSpec((tm, D), tile)] + [any_spec] * (4 + 2 * ns),
            scratch_shapes=[pltpu.VMEM((half, tn), F32), pltpu.VMEM((2, half, tn), F32), pltpu.VMEM((half, tn), BF16),
                            pltpu.VMEM((half, tn), F32),
                            pltpu.SemaphoreType.DMA((2,)), pltpu.SemaphoreType.DMA((N_CHIPS * nb,)),
                            pltpu.SemaphoreType.DMA((3,)), pltpu.SemaphoreType.DMA((3,)),
                            pltpu.SemaphoreType.DMA((2,)), pltpu.SemaphoreType.DMA, pltpu.SemaphoreType.DMA,
                            pltpu.SemaphoreType.DMA((6 * ns,)), pltpu.SemaphoreType.DMA((6 * ns,)),
                            pltpu.SemaphoreType.DMA((ns,)), pltpu.SemaphoreType.DMA((ns,))]),
        out_shape=[jax.ShapeDtypeStruct((t, D), F32), jax.ShapeDtypeStruct((half, W_BLOCK), F32),
                   jax.ShapeDtypeStruct((3, half, W_BLOCK), BF16), jax.ShapeDtypeStruct((N_CHIPS, half, W_BLOCK), F32),
                   jax.ShapeDtypeStruct((3, half, W_BLOCK), BF16)]
        + [jax.ShapeDtypeStruct((6,) + w.shape[2:], w.dtype) for w in wires]
        + [jax.ShapeDtypeStruct(g.shape[2:], F32) for g in grads],
        compiler_params=_params("arbitrary"),
    )(table, xt, dp, dp, w4, dr, *wires, *grads)


def kernel(x, w_in, b_in, conv_w, conv_b, gn_g, gn_b, ln_v_g, ln_v_b, w_spatial, b_spatial, w_pa, w_pb, w_o, b_o, ln_out_g, ln_out_b, loss_target, m_w_in, m_b_in, m_conv_w, m_conv_b, m_gn_g, m_gn_b, m_ln_v_g, m_ln_v_b, m_w_spatial, m_b_spatial, m_w_pa, m_w_pb, m_w_o, m_b_o, m_ln_out_g, m_ln_out_b, v_w_in, v_b_in, v_conv_w, v_conv_b, v_gn_g, v_gn_b, v_ln_v_g, v_ln_v_b, v_w_spatial, v_b_spatial, v_w_pa, v_w_pb, v_w_o, v_b_o, v_ln_out_g, v_ln_out_b):
    n_seq, seq, _ = x.shape
    t = n_seq * seq
    tiles_per_seq = seq // TOKEN_TILE
    x2 = x.reshape(t, D)
    tgt = loss_target.reshape(t, D)

    conv_shard = jnp.pad(conv_w, ((0, HALO - CONV_K), (0, 0)))
    p, win4, wpa4, wpb4, wo4, conv4 = _proj_gather(
        x2, b_in,
        _place_shards([w_in, w_pa, w_pb, w_o, conv_shard], [BF16, BF16, BF16, BF16, F32]))
    win4 = win4.reshape(N_CHIPS, D, W_BLOCK)
    wpa, wpb, wo = wpa4.reshape(D, D), wpb4.reshape(D, D), wo4.reshape(D, D)
    convw = conv4.reshape(N_CHIPS, HALO, D // N_CHIPS).transpose(1, 0, 2).reshape(HALO, D)

    vecs = jnp.stack([conv_b, gn_g, gn_b, ln_v_g, ln_v_b, b_o, ln_out_g, ln_out_b])
    causal = jnp.tril(jnp.ones((CHUNK, CHUNK), bool))
    ws = jnp.where(causal[None], w_spatial, 0.0)
    ws_bf, wst_bf = ws.astype(BF16), ws.transpose(0, 2, 1).astype(BF16)
    bsp = jnp.repeat(b_spatial.T, GROUP_W, axis=1)

    h3, s, mixed, dr, drb, xt, dh1, dp, dya, dyb, acc_f, acc_b, dbin_b, dws, dbsp_acc = _layer_tiles(
        p, x2, tgt, wpa, wpb, wo, convw, vecs, ws_bf, wst_bf, bsp, seq // LAYER_TILE)
    dp, dcw8, dbin_a, *square = _conv_backward(dh1, p, dp, convw, [(h3, dya), (s, dyb), (mixed, drb)], tiles_per_seq)

    small, g_conv = _pack_small(acc_f, acc_b, dbin_a, dbin_b, dcw8, dws, dbsp_acc)
    small = small.reshape(2, N_CHIPS, SMALL_ROWS // 8, D)

    grads = square[:3] + [g_conv, small]
    wires = square[3:] + [g_conv, small]
    grad_x, q_in, chips_in, _, _, *landed = _grad_in_and_x(xt.reshape(2, D // 2, t), dp, win4, dr, wires, grads)
    grad_x = grad_x.reshape(x.shape)
    mine = [_add_chips(q_in, chips_in)]
    mine += _add_devices(grads, landed[:5], landed[5:])
    *full, small_parts = _share_results(mine[:5], mine[5])
    grad_w_in, grad_w_pa, grad_w_pb, grad_w_o = [f.reshape(w.shape) for f, w in zip(full[:4], (w_in, w_pa, w_pb, w_o))]
    grad_conv_w = full[4].reshape(HALO, D // N_CHIPS)[:CONV_K]

    big = {"w_in": (grad_w_in,) + tuple(_adamw(w_in, grad_w_in, m_w_in, v_w_in))}
    group = _adamw_group([(w_pa, grad_w_pa, m_w_pa, v_w_pa), (w_pb, grad_w_pb, m_w_pb, v_w_pb),
                          (w_o, grad_w_o, m_w_o, v_w_o)], (conv_w, grad_conv_w, m_conv_w, v_conv_w))
    for name, g, res in zip(["w_pa", "w_pb", "w_o", "conv_w"], [grad_w_pa, grad_w_pb, grad_w_o, grad_conv_w], group):
        big[name] = (g,) + tuple(res)
    vec_names = ["conv_b", "gn_g", "gn_b", "ln_v_g", "ln_v_b", "b_o", "ln_out_g", "ln_out_b"]
    vec_triples = [(conv_b, m_conv_b, v_conv_b), (gn_g, m_gn_g, v_gn_g), (gn_b, m_gn_b, v_gn_b),
                   (ln_v_g, m_ln_v_g, v_ln_v_g), (ln_v_b, m_ln_v_b, v_ln_v_b), (b_o, m_b_o, v_b_o),
                   (ln_out_g, m_ln_out_g, v_ln_out_g), (ln_out_b, m_ln_out_b, v_ln_out_b)]
    small_res, loss8 = _adamw_small(
        small_parts.reshape(8, SMALL_ROWS // 8, D), vec_triples, (b_in, m_b_in, v_b_in),
        (w_spatial, m_w_spatial, v_w_spatial), (b_spatial, m_b_spatial, v_b_spatial))
    per_name = dict(zip(vec_names + ["b_in", "w_spatial", "b_spatial"], small_res))

    order = ["w_in", "b_in", "conv_w", "conv_b", "gn_g", "gn_b", "ln_v_g", "ln_v_b", "w_spatial", "b_spatial",
             "w_pa", "w_pb", "w_o", "b_o", "ln_out_g", "ln_out_b"]
    outs = [loss8[0, 0], grad_x]
    for kind in range(4):
        outs += [big[n][kind] if n in big else per_name[n][kind] for n in order]
    return tuple(outs)
```

```python
import functools
import math

import jax
import jax.numpy as jnp
from jax import lax
from jax.experimental import pallas as pl
from jax.experimental.pallas import tpu as pltpu

D = 1024
N_GROUPS = 8
GROUP_W = D // N_GROUPS
CHUNK = 128
CONV_K = 31
HALO = 32
D_IN = 8 * D
N_CHIPS = 4
W_BLOCK = D_IN // N_CHIPS
ALPHA = 2.0 ** 0.25
LN_EPS = 1e-5
ADAM_LR, ADAM_B1, ADAM_B2, ADAM_EPS, ADAM_WD, ADAM_STEP = 0.001, 0.9, 0.999, 1e-08, 0.01, 10

TOKEN_TILE = 256
VMEM_LIMIT = 62 * 1024 * 1024
MESH = pl.DeviceIdType.MESH
F32, BF16 = jnp.float32, jnp.bfloat16


def _sigmoid(x):
    return 1.0 / (1.0 + jnp.exp(-x))


def _gelu(x):
    c = math.sqrt(2.0 / math.pi)
    t = jnp.tanh(c * (x + 0.044715 * (x * x * x)))
    return x * (0.5 * (1.0 + t))


def _gelu_and_grad(x):
    c = math.sqrt(2.0 / math.pi)
    x2 = x * x
    t = jnp.tanh(c * (x + 0.044715 * (x2 * x)))
    cdf = 0.5 * (1.0 + t)
    return x * cdf, cdf + 0.5 * x * (1.0 - t * t) * (c * (1.0 + 3.0 * 0.044715 * x2))


def _norm_stats(v):
    mu = jnp.mean(v, axis=-1, keepdims=True)
    vc = v - mu
    var = jnp.mean(vc * vc, axis=-1, keepdims=True)
    rstd = lax.rsqrt(var + LN_EPS)
    return vc * rstd, rstd


def _norm_bwd(dxhat, xhat, rstd):
    m1 = jnp.mean(dxhat, axis=-1, keepdims=True)
    m2 = jnp.mean(dxhat * xhat, axis=-1, keepdims=True)
    return rstd * (dxhat - m1 - xhat * m2)


def _dot(a, b):
    return jnp.dot(a, b, preferred_element_type=F32)


def _dot_nt(a, b):
    return lax.dot_general(a, b, (((1,), (1,)), ((), ())), preferred_element_type=F32)


def _dot_tn(a, b):
    return lax.dot_general(a, b, (((0,), (0,)), ((), ())), preferred_element_type=F32)


def _colsum(v):
    return jnp.sum(v, axis=0, keepdims=True)


def _full(shape):
    return pl.BlockSpec(shape, lambda *_: (0,) * len(shape))


def _resident(shape):
    return pl.BlockSpec(shape, lambda *_: (0,) * len(shape), pipeline_mode=pl.Buffered(1))


def _params(*sem):
    return pltpu.CompilerParams(dimension_semantics=sem, vmem_limit_bytes=VMEM_LIMIT)


def _chip_index():
    return (2 * lax.axis_index("x") + lax.axis_index("y")).astype(jnp.int32).reshape(1)


def _core_index():
    return lax.axis_index("c").astype(jnp.int32).reshape(1)


def _place_shards(ws, dtypes):
    n = len(ws)

    def body(k_ref, *refs):
        for w_ref, o_ref, dtype in zip(refs[:n], refs[n:], dtypes):
            rows = w_ref.shape[0] // 2
            for h in range(2):
                o_ref[h] = w_ref[h * rows:(h + 1) * rows, :].astype(dtype)

    return pl.pallas_call(
        body, name="place_shards",
        grid_spec=pltpu.PrefetchScalarGridSpec(
            num_scalar_prefetch=1, grid=(1,),
            in_specs=[pl.BlockSpec(w.shape, lambda i, k: (0, 0)) for w in ws],
            out_specs=[pl.BlockSpec((None, 2, w.shape[0] // 2, w.shape[1]), lambda i, k: (k[0], 0, 0, 0)) for w in ws]),
        out_shape=[jax.ShapeDtypeStruct((N_CHIPS, 2, w.shape[0] // 2, w.shape[1]), dt) for w, dt in zip(ws, dtypes)],
        compiler_params=_params("arbitrary"),
    )(_chip_index(), *ws)


def _position():
    x, y, c = lax.axis_index("x"), lax.axis_index("y"), lax.axis_index("c")
    return x, y, c, 2 * x + y


def _other_chips(x, y):
    return [(1 - x, y), (x, 1 - y), (1 - x, 1 - y)]


def _any_specs(n):
    return [pl.BlockSpec(memory_space=pl.ANY)] * n


def _proj_gather(x, b_in, bufs):
    t = x.shape[0]
    tm = 1024
    steps = t // tm
    ahead = steps // 2
    half = D // 2
    chunk = W_BLOCK // 2
    n = len(bufs)
    xi, yi = lax.axis_index("x"), lax.axis_index("y")
    chips = [2 * xi + yi, 2 * (1 - xi) + yi, 2 * xi + (1 - yi), 2 * (1 - xi) + (1 - yi)]
    plan = [(0, 0), (0, 1), (1, 0), (2, 1), (1, 1), (2, 0), (3, 0), (3, 1)]
    order = jnp.stack([2 * chips[ch] + q for ch, q in plan]).astype(jnp.int32)

    def body(order_ref, x_ref, b_ref, *refs):
        p_ref, outs = refs[n], refs[n + 1:2 * n + 1]
        xb_ref, w_ref, lsem, send, recv, hop_send, hop_recv, fsend, frecv, qsend, qrecv = refs[2 * n + 1:]
        jj, i = pl.program_id(0), pl.program_id(1)
        x_, y_, c, k = _position()
        nbrs = [(1 - x_, y_), (x_, 1 - y_)]
        blocks = [2 * (1 - x_) + y_, 2 * x_ + (1 - y_), 2 * (1 - x_) + (1 - y_)]

        def quarter(a, block, q, h):
            if a == 0:
                return outs[0].at[block, h, :, pl.ds(q * chunk, chunk)]
            rows = outs[a].shape[2] // 2
            return outs[a].at[block, h, pl.ds(q * rows, rows)]

        def copy(ref, to, send_sem, recv_sem):
            return pltpu.make_async_remote_copy(src_ref=ref, dst_ref=ref, send_sem=send_sem, recv_sem=recv_sem,
                                                device_id=(to[0], to[1], c), device_id_type=MESH)

        def sent(a, nb, q):
            return copy(quarter(a, k, q, c), nbrs[nb], send.at[4 * a + 2 * nb + q], recv.at[4 * a + 2 * nb + q])

        def landed(a, nb, q):
            return copy(quarter(a, blocks[nb], q, c), nbrs[nb], send.at[4 * a + 2 * nb + q], recv.at[4 * a + 2 * nb + q])

        def hopped(a, nb):
            return copy(quarter(a, blocks[nb], nb, c), nbrs[1 - nb], hop_send.at[2 * a + nb], hop_recv.at[2 * a + 1 - nb])

        def from_diagonal(a, via):
            return copy(quarter(a, blocks[2], 1 - via, c), nbrs[via], hop_send.at[2 * a + via], hop_recv.at[2 * a + via])

        def passed(a, r, h):
            return pltpu.make_async_remote_copy(
                src_ref=outs[a].at[blocks[r], h], dst_ref=outs[a].at[blocks[r], h], send_sem=fsend.at[3 * a + r],
                recv_sem=frecv.at[3 * a + r], device_id=(x_, y_, 1 - c), device_id_type=MESH)

        def passed_quarter(r, q, h):
            ref = quarter(0, blocks[r], q, h)
            return pltpu.make_async_remote_copy(
                src_ref=ref, dst_ref=ref, send_sem=qsend.at[2 * r + q], recv_sem=qrecv.at[2 * r + q],
                device_id=(x_, y_, 1 - c), device_id_type=MESH)

        def load(block, q, slot):
            return [pltpu.make_async_copy(quarter(0, block, q, h), w_ref.at[slot, pl.ds(h * half, half)],
                                          lsem.at[2 * slot + h]) for h in range(2)]

        def pass_on(arrays):
            for a in arrays:
                for nb in range(2):
                    landed(a, nb, nb).wait_recv()
                    hopped(a, nb).start()

        @pl.when((jj == 0) & (i == 0))
        def _():
            for a in range(n):
                for nb, q in ((0, 0), (1, 1), (0, 1), (1, 0)):
                    sent(a, nb, q).start()
            for cp in load(k, 0, 0):
                cp.start()

        for nxt in range(1, len(plan)):
            @pl.when((jj == nxt - 1) & (i == ahead))
            def _(nxt=nxt):
                ch, q = plan[nxt]
                if ch in (1, 2):
                    landed(0, ch - 1, q).wait_recv()
                    if q == ch - 1:
                        hopped(0, ch - 1).start()
                elif ch == 3:
                    from_diagonal(0, 1 - q).wait_recv()
                if ch:
                    passed_quarter(ch - 1, q, c).start()
                    passed_quarter(ch - 1, q, 1 - c).wait_recv()
                for cp in load(k if ch == 0 else blocks[ch - 1], q, nxt % 2):
                    cp.start()
                if nxt == 5:
                    pass_on(range(1, n))
                    for a in range(1, n):
                        landed(a, 0, 1).wait_recv()
                        passed(a, 0, c).start()
                        landed(a, 1, 0).wait_recv()
                        passed(a, 1, c).start()

        slot = jj % 2

        @pl.when(i == 0)
        def _():
            for cp in load(k, 0, slot):
                cp.wait()

        rows = pl.ds(pl.multiple_of(i * tm, tm), tm)

        @pl.when(jj == 0)
        def _():
            xb_ref[rows, :] = x_ref[...].astype(BF16)

        p_ref[...] = _dot(xb_ref[rows, :], w_ref[slot]) + b_ref[...]

        @pl.when((jj == len(plan) - 1) & (i == steps - 1))
        def _():
            for a in range(1, n):
                from_diagonal(a, 0).wait_recv()
                from_diagonal(a, 1).wait_recv()
                passed(a, 2, c).start()
            for a in range(1, n):
                for r in range(3):
                    passed(a, r, 1 - c).wait_recv()
                    passed(a, r, c).wait_send()
            for r in range(3):
                for q in range(2):
                    passed_quarter(r, q, c).wait_send()
            for a in range(n):
                for nb in range(2):
                    for q in range(2):
                        sent(a, nb, q).wait_send()
                    hopped(a, nb).wait_send()

    any_spec = pl.BlockSpec(memory_space=pl.ANY)
    return pl.pallas_call(
        body, name="proj_gather",
        grid_spec=pltpu.PrefetchScalarGridSpec(
            num_scalar_prefetch=1, grid=(len(plan), steps),
            in_specs=[pl.BlockSpec((tm, D), lambda jj, i, o: (jnp.where(jj == 0, i, steps - 1), 0)),
                      pl.BlockSpec((None, 1, chunk), lambda jj, i, o: (o[jj], 0, 0))] + [any_spec] * n,
            out_specs=[pl.BlockSpec((tm, chunk), lambda jj, i, o: (i, o[jj]))] + [any_spec] * n,
            scratch_shapes=[pltpu.VMEM((t, D), BF16), pltpu.VMEM((2, D, chunk), BF16), pltpu.SemaphoreType.DMA((4,)),
                            pltpu.SemaphoreType.DMA((4 * n,)), pltpu.SemaphoreType.DMA((4 * n,)),
                            pltpu.SemaphoreType.DMA((2 * n,)), pltpu.SemaphoreType.DMA((2 * n,)),
                            pltpu.SemaphoreType.DMA((3 * n,)), pltpu.SemaphoreType.DMA((3 * n,)),
                            pltpu.SemaphoreType.DMA((6,)), pltpu.SemaphoreType.DMA((6,))]),
        out_shape=[jax.ShapeDtypeStruct((t, D_IN), F32)] + [jax.ShapeDtypeStruct(b.shape, b.dtype) for b in bufs],
        input_output_aliases={3 + a: 1 + a for a in range(n)},
        compiler_params=_params("arbitrary", "arbitrary"),
    )(order, x, b_in.reshape(D_IN // chunk, 1, chunk), *bufs)


def _share_results(bufs, small):
    n = len(bufs)

    def body(*refs):
        outs, small_out = refs[n + 1:2 * n + 1], refs[2 * n + 1]
        send, recv, ssend, srecv = refs[2 * n + 2:]
        x, y, c, k = _position()
        cps = []
        for a in range(n):
            cp = pltpu.make_async_remote_copy(
                src_ref=outs[a].at[c], dst_ref=outs[a].at[c], send_sem=send.at[a], recv_sem=recv.at[a],
                device_id=(x, y, 1 - c), device_id_type=MESH)
            cp.start()
            cps.append(cp)
        waits = []
        for p in range(1, 8):
            px, py, pc = x ^ (p >> 2), y ^ ((p >> 1) & 1), c ^ (p & 1)
            cp = pltpu.make_async_remote_copy(
                src_ref=small_out.at[k, c], dst_ref=small_out.at[k, c], send_sem=ssend.at[p - 1],
                recv_sem=srecv.at[p - 1], device_id=(px, py, pc), device_id_type=MESH)
            cp.start()
            cps.append(cp)
            waits.append(pltpu.make_async_remote_copy(
                src_ref=small_out.at[2 * px + py, pc], dst_ref=small_out.at[2 * px + py, pc], send_sem=ssend.at[p - 1],
                recv_sem=srecv.at[p - 1], device_id=(px, py, pc), device_id_type=MESH))
        for a in range(n):
            pltpu.make_async_remote_copy(
                src_ref=outs[a].at[1 - c], dst_ref=outs[a].at[1 - c], send_sem=send.at[a], recv_sem=recv.at[a],
                device_id=(x, y, 1 - c), device_id_type=MESH).wait_recv()
        for w in waits:
            w.wait_recv()
        for cp in cps:
            cp.wait_send()

    return pl.pallas_call(
        body, name="rs_share_results",
        in_specs=_any_specs(n + 1), out_specs=_any_specs(n + 1),
        out_shape=[jax.ShapeDtypeStruct(b.shape, b.dtype) for b in bufs + [small]],
        scratch_shapes=[pltpu.SemaphoreType.DMA((n,)), pltpu.SemaphoreType.DMA((n,)),
                        pltpu.SemaphoreType.DMA((7,)), pltpu.SemaphoreType.DMA((7,))],
        input_output_aliases={a: a for a in range(n + 1)},
    )(*bufs, small)


def _row_tile(r, c):
    t = max(8, min(r, (1 << 18) // c))
    while r % t:
        t //= 2
    return t


def _add_devices(grads, lands, sibs):
    n = len(grads)

    def body(kc_ref, *refs):
        for g_ref, l_ref, s_ref, f_ref in zip(refs[:n], refs[n:2 * n], refs[2 * n:3 * n], refs[3 * n:]):
            f = g_ref[...] + s_ref[...]
            for i in range(l_ref.shape[0]):
                f = f + l_ref[i].astype(F32)
            f_ref[...] = f

    shapes = [g.shape[2:] for g in grads]
    out_specs = [pl.BlockSpec((None,) + sh, lambda i, kc: (kc[1], 0, 0)) for sh in shapes[:-1]]
    out_specs.append(pl.BlockSpec((None, None) + shapes[-1], lambda i, kc: (kc[0], kc[1], 0, 0)))
    out_shape = [jax.ShapeDtypeStruct((2,) + sh, F32) for sh in shapes[:-1]]
    out_shape.append(jax.ShapeDtypeStruct((N_CHIPS, 2) + shapes[-1], F32))
    return pl.pallas_call(
        body, name="rs_add_devices",
        grid_spec=pltpu.PrefetchScalarGridSpec(
            num_scalar_prefetch=1, grid=(1,),
            in_specs=[pl.BlockSpec((None, None) + sh, lambda i, kc: (kc[1], kc[0], 0, 0)) for sh in shapes]
            + [pl.BlockSpec(l.shape, lambda i, kc: (0, 0, 0)) for l in lands]
            + [pl.BlockSpec(sh, lambda i, kc: (0, 0)) for sh in shapes],
            out_specs=out_specs),
        out_shape=out_shape,
        compiler_params=_params("arbitrary"),
    )(jnp.concatenate([_chip_index(), _core_index()]), *grads, *lands, *sibs)


def _add_chips(q, b2):
    r, c = q.shape
    t = _row_tile(r, c)

    def body(c_ref, q_ref, b_ref, f_ref):
        f_ref[...] = ((q_ref[...] + b_ref[0].astype(F32)) + b_ref[1].astype(F32)) + b_ref[2].astype(F32)

    return pl.pallas_call(
        body, name="rs_add_chips",
        grid_spec=pltpu.PrefetchScalarGridSpec(
            num_scalar_prefetch=1, grid=(r // t,),
            in_specs=[pl.BlockSpec((t, c), lambda i, cr: (i, 0)), pl.BlockSpec((3, t, c), lambda i, cr: (0, i, 0))],
            out_specs=pl.BlockSpec((None, t, c), lambda i, cr: (cr[0], i, 0))),
        out_shape=jax.ShapeDtypeStruct((2, r, c), F32),
        compiler_params=_params("parallel"),
    )(_core_index(), q, b2)


def _adamw_math(w, g, m, v):
    m = ADAM_B1 * m + (1.0 - ADAM_B1) * g
    v = ADAM_B2 * v + (1.0 - ADAM_B2) * (g * g)
    m_hat = m / (1.0 - ADAM_B1 ** ADAM_STEP)
    v_hat = v / (1.0 - ADAM_B2 ** ADAM_STEP)
    delta = -ADAM_LR * (m_hat / (jnp.sqrt(v_hat) + ADAM_EPS) + ADAM_WD * w)
    return delta, m, v


def _adamw_group(quads, conv):
    n = len(quads)
    r, c = quads[0][0].shape

    def body(*refs):
        ins, outs = refs[:4 * (n + 1)], refs[4 * (n + 1):]
        for i in range(n + 1):
            w_ref, g_ref, m_ref, v_ref = ins[4 * i:4 * i + 4]
            res = _adamw_math(w_ref[...], g_ref[...], m_ref[...], v_ref[...])
            for o_ref, val in zip(outs[3 * i:3 * i + 3], res):
                o_ref[...] = val

    half = pl.BlockSpec((r // 2, c), lambda i: (i, 0))
    whole = pl.BlockSpec(conv[0].shape, lambda i: (0, 0))
    res = pl.pallas_call(
        body, name="adamw_group", grid=(2,),
        in_specs=[half] * (4 * n) + [whole] * 4, out_specs=[half] * (3 * n) + [whole] * 3,
        out_shape=[jax.ShapeDtypeStruct((r, c), F32)] * (3 * n) + [jax.ShapeDtypeStruct(conv[0].shape, F32)] * 3,
        compiler_params=_params("arbitrary"),
    )(*[a for quad in quads for a in quad], *conv)
    return [res[3 * i:3 * i + 3] for i in range(n + 1)]


def _adamw(w, g, m, v):
    r, c = w.shape
    t = _row_tile(r, c) if r % 8 == 0 else r

    def body(w_ref, g_ref, m_ref, v_ref, d_ref, nm_ref, nv_ref):
        d_ref[...], nm_ref[...], nv_ref[...] = _adamw_math(w_ref[...], g_ref[...], m_ref[...], v_ref[...])

    spec = pl.BlockSpec((t, c), lambda i: (i, 0))
    return pl.pallas_call(
        body, name="adamw", grid=(r // t,), in_specs=[spec] * 4, out_specs=[spec] * 3,
        out_shape=[jax.ShapeDtypeStruct((r, c), F32)] * 3, compiler_params=_params("parallel"),
    )(w, g, m, v)


ROW_B_IN = 0
ROW_VECS = 8
ROW_LOSS = 16
ROW_B_SPATIAL = 24
ROW_W_SPATIAL = 32
SMALL_ROWS = 192
N_VECS = 8


def _pack_small(acc_f, acc_b, dbin_a, dbin_b, dcw8, dws, dbsp):
    cols = D // N_CHIPS

    def body(af_ref, ab_ref, da_ref, db_ref, cw_ref, ws_ref, bs_ref, o_ref, gc_ref):
        o_ref[...] = jnp.zeros_like(o_ref)
        for j in range(D_IN // D):
            src = da_ref if j < 2 else db_ref
            o_ref[ROW_B_IN + j:ROW_B_IN + j + 1, :] = src[0:1, j * D:(j + 1) * D]
        dcw = jnp.sum(cw_ref[...].reshape(HALO, SUBLANES, D), axis=1)
        o_ref[ROW_VECS:ROW_VECS + 1, :] = dcw[CONV_K:CONV_K + 1]
        o_ref[ROW_VECS + 1:ROW_VECS + 5, :] = ab_ref[0:4, :]
        o_ref[ROW_VECS + 5:ROW_VECS + 6, :] = af_ref[2:3, :]
        o_ref[ROW_VECS + 6:ROW_VECS + 8, :] = af_ref[0:2, :]
        o_ref[ROW_LOSS:ROW_LOSS + 1, :] = af_ref[3:4, :]
        head = lax.broadcasted_iota(jnp.int32, (N_GROUPS, D), 0)
        lane = lax.broadcasted_iota(jnp.int32, (N_GROUPS, D), 1)
        indicator = jnp.where(lane // GROUP_W == head, 1.0, 0.0)
        o_ref[ROW_B_SPATIAL:ROW_B_SPATIAL + N_GROUPS, 0:CHUNK] = lax.dot_general(
            indicator, bs_ref[...], (((1,), (1,)), ((), ())), precision=lax.Precision.HIGHEST, preferred_element_type=F32)
        t_idx = lax.broadcasted_iota(jnp.int32, (CHUNK, D), 0)
        s_idx = lax.broadcasted_iota(jnp.int32, (CHUNK, D), 1) % CHUNK
        o_ref[ROW_W_SPATIAL:ROW_W_SPATIAL + CHUNK, :] = jnp.where(s_idx <= t_idx, ws_ref[...], 0.0)
        for h in range(2):
            for j in range(N_CHIPS):
                gc_ref[h, j] = dcw[h * (HALO // 2):(h + 1) * (HALO // 2), j * cols:(j + 1) * cols]

    ins = [acc_f, acc_b, dbin_a, dbin_b, dcw8, dws, dbsp]
    return pl.pallas_call(
        body, name="pack_small",
        in_specs=[_full(a.shape) for a in ins],
        out_specs=[_full((SMALL_ROWS, D)), _full((2, N_CHIPS, HALO // 2, cols))],
        out_shape=[jax.ShapeDtypeStruct((SMALL_ROWS, D), F32), jax.ShapeDtypeStruct((2, N_CHIPS, HALO // 2, cols), F32)],
        compiler_params=_params(),
    )(*ins)


def _adamw_small(parts, vecs, b_in, w_spatial, b_spatial):
    triples = list(vecs) + [b_in, w_spatial, b_spatial]
    n_in = 3 * len(triples)

    def body(p_ref, *refs):
        ins = [refs[3 * i:3 * i + 3] for i in range(len(triples))]
        outs = [refs[n_in + 4 * i:n_in + 4 * i + 4] for i in range(len(triples))]
        loss_ref, g_ref = refs[n_in + 4 * len(triples):]
        rows = SMALL_ROWS // 8
        for k in range(N_CHIPS):
            for core in range(2):
                g_ref[(core * N_CHIPS + k) * rows:(core * N_CHIPS + k + 1) * rows, :] = p_ref[2 * k + core]

        def step(g, wmv, out, get, put):
            d, nm, nv = _adamw_math(get(wmv[0]), g, get(wmv[1]), get(wmv[2]))
            for o, val in zip(out, (g, d, nm, nv)):
                put(o, val)

        for i in range(N_VECS):
            step(g_ref[ROW_VECS + i:ROW_VECS + i + 1, :], ins[i], outs[i],
                 lambda r: r[...].reshape(1, D), lambda o, val: o.__setitem__(Ellipsis, val.reshape(D)))
        for j in range(D_IN // D):
            piece = pl.ds(j * D, D)
            step(g_ref[ROW_B_IN + j:ROW_B_IN + j + 1, :], ins[N_VECS], outs[N_VECS],
                 lambda r: r[piece].reshape(1, D), lambda o, val: o.__setitem__(piece, val.reshape(D)))
        for h in range(N_GROUPS):
            step(g_ref[ROW_W_SPATIAL:ROW_W_SPATIAL + CHUNK, h * CHUNK:(h + 1) * CHUNK], ins[N_VECS + 1], outs[N_VECS + 1],
                 lambda r: r[h], lambda o, val: o.__setitem__(h, val))
        step(g_ref[ROW_B_SPATIAL:ROW_B_SPATIAL + N_GROUPS, 0:CHUNK], ins[N_VECS + 2], outs[N_VECS + 2],
             lambda r: r[...], lambda o, val: o.__setitem__(Ellipsis, val))
        lanes = g_ref[ROW_LOSS:ROW_LOSS + 1, :]
        loss_ref[...] = jnp.broadcast_to(jnp.sum(lanes, axis=1, keepdims=True), (8, 128))

    flat = [a for tr in triples for a in tr]
    out_shape = [jax.ShapeDtypeStruct(tr[0].shape, F32) for tr in triples for _ in range(4)]
    out_shape.append(jax.ShapeDtypeStruct((8, 128), F32))
    res = pl.pallas_call(
        body, name="adamw_small",
        in_specs=[_full(parts.shape)] + [_full(a.shape) for a in flat],
        out_specs=[_full(o.shape) for o in out_shape],
        out_shape=out_shape,
        scratch_shapes=[pltpu.VMEM((SMALL_ROWS, D), F32)],
        compiler_params=_params(),
    )(parts, *flat)
    return [res[4 * i:4 * i + 4] for i in range(len(triples))], res[-1]


SUBLANES = 8
SHIFT_ROWS = HALO - SUBLANES


def _shifted_copies(src_ref, sh_ref, cs, tm):
    for p in range(1, SUBLANES):
        sh_ref[p - 1] = src_ref[pl.ds(p, tm + SHIFT_ROWS), cs]


def _tap(src_ref, sh_ref, cs, offset, start, rows):
    p, q = offset % SUBLANES, offset // SUBLANES
    if p == 0:
        return src_ref[pl.ds(start + SUBLANES * q, rows), cs]
    return sh_ref[p - 1, pl.ds(start + SUBLANES * q, rows), :]


def _conv_taps(src_ref, sh_ref, w_ref, first_offset, step, bias, dst_ref, tm):
    rows = 64
    for g in range(N_GROUPS):
        cs = slice(g * GROUP_W, (g + 1) * GROUP_W)
        _shifted_copies(src_ref, sh_ref, cs, tm)
        for rb in range(tm // rows):
            acc = jnp.zeros((rows, GROUP_W), F32) + (bias[:, cs] if bias is not None else 0.0)
            for k in range(CONV_K):
                acc = acc + w_ref[k:k + 1, cs] * _tap(src_ref, sh_ref, cs, first_offset + step * k, rb * rows, rows)
            dst_ref[rb * rows:(rb + 1) * rows, cs] = acc


def _conv_weight_grad(d_ref, src_ref, sh_ref, first_offset, acc_ref, tm):
    rows = 64
    for g in range(N_GROUPS):
        cs = slice(g * GROUP_W, (g + 1) * GROUP_W)
        _shifted_copies(src_ref, sh_ref, cs, tm)
        for rb in range(tm // rows):
            d = d_ref[rb * rows:(rb + 1) * rows, cs]
            for k in range(CONV_K):
                prod = d * _tap(src_ref, sh_ref, cs, first_offset + k, rb * rows, rows)
                acc_ref[SUBLANES * k:SUBLANES * (k + 1), cs] += jnp.sum(
                    prod.reshape(rows // SUBLANES, SUBLANES, GROUP_W), axis=0)


def _spatial_mix(w_ref, v_bf, tm):
    rows = []
    for q in range(tm // CHUNK):
        cols = [_dot(w_ref[h], v_bf[q * CHUNK:(q + 1) * CHUNK, h * GROUP_W:(h + 1) * GROUP_W])
                for h in range(N_GROUPS)]
        rows.append(jnp.concatenate(cols, axis=1))
    return jnp.concatenate(rows, axis=0)


def _group_norm_fwd(h1, gn_g, gn_b):
    xhat, rstd = [], []
    for g in range(N_GROUPS):
        xh, rs = _norm_stats(h1[:, g * GROUP_W:(g + 1) * GROUP_W])
        xhat.append(xh)
        rstd.append(rs)
    xhat = jnp.concatenate(xhat, axis=1)
    return xhat * gn_g + gn_b, xhat, rstd


def _forward_tiles(p, x, tgt, wpa, wpb, wo, convw, vecs, ws, bsp, tiles_per_seq):
    t = x.shape[0]
    tm = TOKEN_TILE
    hb = tm // HALO

    def body(p_ref, ph_ref, x_ref, t_ref, wpa_ref, wpb_ref, wo_ref, cw_ref, vec_ref, ws_ref, bsp_ref,
             h1_ref, ya_ref, yb_ref, h3_ref, s_ref, mx_ref, dr_ref, drb_ref, xt_ref, acc_ref, he_ref, sh_ref):
        i = pl.program_id(0)
        xt_ref[...] = x_ref[...].T.astype(BF16)
        conv_b, gn_g, gn_b, lnv_g, lnv_b, b_o, lno_g, lno_b = [vec_ref[j:j + 1, :] for j in range(8)]

        keep = jnp.where(i % tiles_per_seq == 0, 0.0, 1.0)
        he_ref[0:HALO, :] = ph_ref[:, 0:D] * _sigmoid(ph_ref[:, D:2 * D]) * keep
        he_ref[HALO:, :] = p_ref[:, 0:D] * _sigmoid(p_ref[:, D:2 * D])
        _conv_taps(he_ref, sh_ref, cw_ref, HALO - (CONV_K - 1), 1, conv_b, h1_ref, tm)
        h2, _, _ = _group_norm_fwd(h1_ref[...], gn_g, gn_b)
        a_gate = p_ref[:, 2 * D:3 * D]
        h3 = ((h2 * _sigmoid(h2)) * (a_gate * _sigmoid(a_gate))).astype(BF16)
        h3_ref[...] = h3
        ya = _dot(h3, wpa_ref[...])
        ya_ref[...] = ya

        u = _gelu(p_ref[:, 3 * D:4 * D])
        vhat, _ = _norm_stats(_gelu(p_ref[:, 4 * D:5 * D]))
        v1 = (vhat * lnv_g + lnv_b).astype(BF16)
        b_gate = p_ref[:, 5 * D:6 * D]
        vmix = _spatial_mix(ws_ref, v1, tm) + jnp.concatenate([bsp_ref[...]] * (tm // CHUNK), axis=0)
        s = (u * vmix * (b_gate * _sigmoid(b_gate))).astype(BF16)
        s_ref[...] = s
        yb = _dot(s, wpb_ref[...])
        yb_ref[...] = yb

        mixed = (_sigmoid(p_ref[:, 6 * D:7 * D]) * ya + _sigmoid(p_ref[:, 7 * D:8 * D]) * yb).astype(BF16)
        mx_ref[...] = mixed
        r = ALPHA * x_ref[...] + (_dot(mixed, wo_ref[...]) + b_o)
        xhat, rstd = _norm_stats(r)
        err = (xhat * lno_g + lno_b) - t_ref[...]
        dout = err * (1.0 / D)
        dr = _norm_bwd(dout * lno_g, xhat, rstd)
        dr_ref[...] = dr
        drb_ref[...] = dr.astype(BF16)

        @pl.when(i == 0)
        def _():
            acc_ref[...] = jnp.zeros_like(acc_ref)

        acc_ref[0:1, :] += _colsum(dout * xhat)
        acc_ref[1:2, :] += _colsum(dout)
        acc_ref[2:3, :] += _colsum(dr)
        acc_ref[3:4, :] += _colsum(err * err) * (0.5 / D)

    tile = lambda w: pl.BlockSpec((tm, w), lambda i: (i, 0))
    f32_out = jax.ShapeDtypeStruct((t, D), F32)
    bf_out = jax.ShapeDtypeStruct((t, D), BF16)
    return pl.pallas_call(
        body, name="forward_tiles", grid=(t // tm,),
        in_specs=[tile(D_IN),
                  pl.BlockSpec((HALO, 2 * D), lambda i: (jnp.maximum(i * hb - 1, 0), 0)),
                  tile(D), tile(D), _resident((D, D)), _resident((D, D)), _resident((D, D)), _full((HALO, D)), _full((8, D)),
                  _full((N_GROUPS, CHUNK, CHUNK)), _full((CHUNK, D))],
        out_specs=[tile(D)] * 8 + [pl.BlockSpec((D, tm), lambda i: (0, i)), _full((8, D))],
        out_shape=[f32_out, f32_out, f32_out, bf_out, bf_out, bf_out, f32_out, bf_out,
                   jax.ShapeDtypeStruct((D, t), BF16), jax.ShapeDtypeStruct((8, D), F32)],
        scratch_shapes=[pltpu.VMEM((tm + HALO, D), F32), pltpu.VMEM((SUBLANES - 1, tm + SHIFT_ROWS, GROUP_W), F32)],
        compiler_params=_params("arbitrary"),
    )(p, p, x, tgt, wpa, wpb, wo, convw, vecs, ws, bsp)


def _backward_tiles(p, h1, ya, yb, drb, wpa, wpb, wo, vecs, ws, wst, bsp):
    t = h1.shape[0]
    tm = TOKEN_TILE

    def body(p_ref, h1_ref, ya_ref, yb_ref, drb_ref, wpa_ref, wpb_ref, wo_ref, vec_ref, ws_ref, wst_ref, bsp_ref,
             dh1_ref, dp_ref, dya_ref, dyb_ref, acc_ref, dbin_ref, dws_ref, dbsp_ref):
        i = pl.program_id(0)
        _, gn_g, gn_b, lnv_g, lnv_b = [vec_ref[j:j + 1, :] for j in range(5)]

        @pl.when(i == 0)
        def _():
            acc_ref[...] = jnp.zeros_like(acc_ref)
            dbin_ref[...] = jnp.zeros_like(dbin_ref)
            dws_ref[...] = jnp.zeros_like(dws_ref)
            dbsp_ref[...] = jnp.zeros_like(dbsp_ref)

        def emit(block, val):
            dbin_ref[0:1, block * D:(block + 1) * D] += _colsum(val)
            dp_ref[:, block * D:(block + 1) * D] = val.astype(BF16)

        dp_ref[:, 0:2 * D] = jnp.zeros((tm, 2 * D), BF16)
        dmixed = _dot_nt(drb_ref[...], wo_ref[...])
        ga = _sigmoid(p_ref[:, 6 * D:7 * D])
        gb = _sigmoid(p_ref[:, 7 * D:8 * D])
        dya = (dmixed * ga).astype(BF16)
        dyb = (dmixed * gb).astype(BF16)
        dya_ref[...] = dya
        dyb_ref[...] = dyb
        emit(6, dmixed * ya_ref[...] * (ga * (1.0 - ga)))
        emit(7, dmixed * yb_ref[...] * (gb * (1.0 - gb)))

        dh3 = _dot_nt(dya, wpa_ref[...])
        h2, xhat, rstd = _group_norm_fwd(h1_ref[...], gn_g, gn_b)
        sg = _sigmoid(h2)
        a_gate = p_ref[:, 2 * D:3 * D]
        sa = _sigmoid(a_gate)
        dh2 = dh3 * (a_gate * sa) * (sg * (1.0 + h2 * (1.0 - sg)))
        emit(2, dh3 * (h2 * sg) * (sa * (1.0 + a_gate * (1.0 - sa))))
        acc_ref[0:1, :] += _colsum(dh2 * xhat)
        acc_ref[1:2, :] += _colsum(dh2)
        dxhat = dh2 * gn_g
        for g in range(N_GROUPS):
            cs = slice(g * GROUP_W, (g + 1) * GROUP_W)
            dh1_ref[:, cs] = _norm_bwd(dxhat[:, cs], xhat[:, cs], rstd[g])

        ds = _dot_nt(dyb, wpb_ref[...])
        u_pre = p_ref[:, 3 * D:4 * D]
        u, du_dpre = _gelu_and_grad(u_pre)
        v0, dv_dpre = _gelu_and_grad(p_ref[:, 4 * D:5 * D])
        vhat, vrstd = _norm_stats(v0)
        v1 = (vhat * lnv_g + lnv_b).astype(BF16)
        vmix = _spatial_mix(ws_ref, v1, tm) + jnp.concatenate([bsp_ref[...]] * (tm // CHUNK), axis=0)
        b_gate = p_ref[:, 5 * D:6 * D]
        sb = _sigmoid(b_gate)
        silu_b = b_gate * sb
        emit(3, ds * vmix * silu_b * du_dpre)
        emit(5, ds * u * vmix * (sb * (1.0 + b_gate * (1.0 - sb))))
        dvmix = ds * u * silu_b
        dvmix_bf = dvmix.astype(BF16)
        for q in range(tm // CHUNK):
            dbsp_ref[...] += dvmix[q * CHUNK:(q + 1) * CHUNK, :]
            for h in range(N_GROUPS):
                blk = (slice(q * CHUNK, (q + 1) * CHUNK), slice(h * GROUP_W, (h + 1) * GROUP_W))
                dws_ref[:, h * GROUP_W:(h + 1) * GROUP_W] += _dot_nt(dvmix_bf[blk], v1[blk])
        dv1 = _spatial_mix(wst_ref, dvmix_bf, tm)
        acc_ref[2:3, :] += _colsum(dv1 * vhat)
        acc_ref[3:4, :] += _colsum(dv1)
        emit(4, _norm_bwd(dv1 * lnv_g, vhat, vrstd) * dv_dpre)

    tile = lambda w: pl.BlockSpec((tm, w), lambda i: (i, 0))
    return pl.pallas_call(
        body, name="backward_tiles", grid=(t // tm,),
        in_specs=[tile(D_IN), tile(D), tile(D), tile(D), tile(D), _resident((D, D)), _resident((D, D)), _resident((D, D)),
                  _full((8, D)), _full((N_GROUPS, CHUNK, CHUNK)), _full((N_GROUPS, CHUNK, CHUNK)), _full((CHUNK, D))],
        out_specs=[tile(D), tile(D_IN), tile(D), tile(D), _full((8, D)), _full((8, D_IN)),
                   _full((CHUNK, D)), _full((CHUNK, D))],
        out_shape=[jax.ShapeDtypeStruct((t, D), F32), jax.ShapeDtypeStruct((t, D_IN), BF16),
                   jax.ShapeDtypeStruct((t, D), BF16), jax.ShapeDtypeStruct((t, D), BF16),
                   jax.ShapeDtypeStruct((8, D), F32), jax.ShapeDtypeStruct((8, D_IN), F32),
                   jax.ShapeDtypeStruct((CHUNK, D), F32), jax.ShapeDtypeStruct((CHUNK, D), F32)],
        compiler_params=_params("arbitrary"),
    )(p, h1, ya, yb, drb, wpa, wpb, wo, vecs, ws, wst, bsp)


def _conv_backward(dh1, p, dp, convw, pairs, tiles_per_seq):
    t = dh1.shape[0]
    tm = TOKEN_TILE
    hb = tm // HALO
    last = t // HALO - 1
    n_sq = len(pairs)
    span = 4
    rows = D // 8

    def body(dh1_ref, dnext_ref, p_ref, ph_ref, cw_ref, dp_in_ref, *refs):
        del dp_in_ref
        sq_in = refs[:2 * n_sq]
        dp_ref, dcw_ref, dbin_ref = refs[2 * n_sq:2 * n_sq + 3]
        sq_out = refs[2 * n_sq + 3:3 * n_sq + 3]
        sq_wire = refs[3 * n_sq + 3:4 * n_sq + 3]
        de_ref, he_ref, dh0_ref, sh_ref, acc_ref, wire_ref, sq_sem, wire_sem = refs[4 * n_sq + 3:]
        i = pl.program_id(0)

        @pl.when(i == 0)
        def _():
            dcw_ref[...] = jnp.zeros_like(dcw_ref)
            dbin_ref[...] = jnp.zeros_like(dbin_ref)
            acc_ref[...] = jnp.zeros_like(acc_ref)

        @pl.when(i % span == span - 1)
        def _():
            for a in range(n_sq):
                acc_ref[a] += _dot_tn(sq_in[2 * a][...], sq_in[2 * a + 1][...])

        keep_next = jnp.where(i % tiles_per_seq == tiles_per_seq - 1, 0.0, 1.0)
        de_ref[0:tm, :] = dh1_ref[...]
        de_ref[tm:, :] = dnext_ref[...] * keep_next
        _conv_taps(de_ref, sh_ref, cw_ref, CONV_K - 1, -1, None, dh0_ref, tm)

        keep_prev = jnp.where(i % tiles_per_seq == 0, 0.0, 1.0)
        sg = _sigmoid(p_ref[:, D:2 * D])
        val = p_ref[:, 0:D]
        he_ref[0:HALO, :] = ph_ref[:, 0:D] * _sigmoid(ph_ref[:, D:2 * D]) * keep_prev
        he_ref[HALO:, :] = val * sg
        _conv_weight_grad(dh1_ref, he_ref, sh_ref, HALO - (CONV_K - 1), dcw_ref, tm)
        dcw_ref[SUBLANES * CONV_K:, :] += jnp.sum(dh1_ref[...].reshape(tm // SUBLANES, SUBLANES, D), axis=0)

        dh0 = dh0_ref[...]
        dval = dh0 * sg
        dglu = dh0 * val * (sg * (1.0 - sg))
        dbin_ref[0:1, 0:D] += _colsum(dval)
        dbin_ref[0:1, D:2 * D] += _colsum(dglu)
        dp_ref[:, 0:D] = dval.astype(BF16)
        dp_ref[:, D:2 * D] = dglu.astype(BF16)

        @pl.when(i == t // tm - 1)
        def _():
            cps = [pltpu.make_async_copy(acc_ref.at[a, pl.ds((2 * j + h) * rows, rows)], sq_out[a].at[h, j],
                                         sq_sem.at[(a * N_CHIPS + j) * 2 + h])
                   for a in range(n_sq) for j in range(N_CHIPS) for h in range(2)]
            for cp in cps:
                cp.start()
            for a in range(n_sq):
                wire_ref[...] = acc_ref[a].astype(BF16)
                narrow = [pltpu.make_async_copy(wire_ref.at[pl.ds((2 * j + h) * rows, rows)], sq_wire[a].at[h, j],
                                                wire_sem.at[2 * j + h]) for j in range(N_CHIPS) for h in range(2)]
                for cp in narrow:
                    cp.start()
                for cp in narrow:
                    cp.wait()
            for cp in cps:
                cp.wait()

    any_spec = pl.BlockSpec(memory_space=pl.ANY)
    wide = pl.BlockSpec((span * tm, D), lambda i: (i // span, 0))
    return pl.pallas_call(
        body, name="conv_backward", grid=(t // tm,),
        in_specs=[pl.BlockSpec((tm, D), lambda i: (i, 0)),
                  pl.BlockSpec((HALO, D), lambda i: (jnp.minimum((i + 1) * hb, last), 0)),
                  pl.BlockSpec((tm, 2 * D), lambda i: (i, 0)),
                  pl.BlockSpec((HALO, 2 * D), lambda i: (jnp.maximum(i * hb - 1, 0), 0)),
                  _full((HALO, D)), any_spec] + [wide] * (2 * n_sq),
        out_specs=[pl.BlockSpec((tm, 2 * D), lambda i: (i, 0)), _full((SUBLANES * HALO, D)), _full((8, 2 * D))]
        + [any_spec] * (2 * n_sq),
        out_shape=[jax.ShapeDtypeStruct(dp.shape, BF16), jax.ShapeDtypeStruct((SUBLANES * HALO, D), F32),
                   jax.ShapeDtypeStruct((8, 2 * D), F32)]
        + [jax.ShapeDtypeStruct((2, N_CHIPS, rows, D), F32)] * n_sq
        + [jax.ShapeDtypeStruct((2, N_CHIPS, rows, D), BF16)] * n_sq,
        scratch_shapes=[pltpu.VMEM((tm + HALO, D), F32), pltpu.VMEM((tm + HALO, D), F32), pltpu.VMEM((tm, D), F32),
                        pltpu.VMEM((SUBLANES - 1, tm + SHIFT_ROWS, GROUP_W), F32), pltpu.VMEM((n_sq, D, D), F32),
                        pltpu.VMEM((D, D), BF16), pltpu.SemaphoreType.DMA((n_sq * N_CHIPS * 2,)),
                        pltpu.SemaphoreType.DMA((N_CHIPS * 2,))],
        input_output_aliases={5: 0},
        compiler_params=_params("arbitrary"),
    )(dh1, dh1, p, p, convw, dp, *[a for pair in pairs for a in pair])


def _grad_in_and_x(xt, dp, w4, dr, wires, grads):
    t = dr.shape[0]
    tm = TOKEN_TILE
    half, tn = D // 2, 512
    nb = W_BLOCK // tn
    n_w, n_x = 2 * N_CHIPS * nb, t // tm
    ns = len(grads)
    xi, yi, ci = lax.axis_index("x"), lax.axis_index("y"), lax.axis_index("c")
    others = [2 * (1 - xi) + yi, 2 * xi + (1 - yi), 2 * (1 - xi) + (1 - yi)]
    blocks = others + others + [2 * xi + yi] * 2
    halves = [1 - ci] * 3 + [ci] * 3 + [1 - ci, ci]
    table = jnp.stack([jnp.stack([b * nb + n for b in blocks for n in range(nb)]),
                       jnp.stack([h for h in halves for _ in range(nb)])]).astype(jnp.int32)

    def body(tab_ref, xt_ref, dpc_ref, dpr_ref, w_ref, dr_ref, *refs):
        parts, fulls = refs[:ns], refs[ns:2 * ns]
        dx_ref, qk_ref, b2_ref, b1_ref, wire_ref = refs[2 * ns:2 * ns + 5]
        lands, sibs = refs[2 * ns + 5:3 * ns + 5], refs[3 * ns + 5:4 * ns + 5]
        (g_ref, st_ref, sb_ref, tmp_ref, d2d_send, d2d_recv, ici_send, ici_recv, own_sem, tmp_sem, wire_sem,
         p_send, p_recv, s_send, s_recv) = refs[4 * ns + 5:]
        s = pl.program_id(0)
        x_, y_, c, k = _position()
        chips = _other_chips(x_, y_)
        n = s % nb
        grp = s // nb
        cols = pl.ds(pl.multiple_of(n * tn, tn), tn)

        def part(a, r, core):
            cx, cy = chips[r]
            return pltpu.make_async_remote_copy(
                src_ref=parts[a].at[core, 2 * cx + cy], dst_ref=lands[a].at[2 * r + c],
                send_sem=p_send.at[6 * a + 2 * r + core], recv_sem=p_recv.at[6 * a + 2 * r + c],
                device_id=(cx, cy, core), device_id_type=MESH)

        def landed(a, r, core):
            cx, cy = chips[r]
            return pltpu.make_async_remote_copy(
                src_ref=lands[a].at[2 * r + core], dst_ref=lands[a].at[2 * r + core],
                send_sem=p_send.at[6 * a + 2 * r + core], recv_sem=p_recv.at[6 * a + 2 * r + core],
                device_id=(cx, cy, core), device_id_type=MESH)

        def to_sibling_whole(a):
            return pltpu.make_async_remote_copy(
                src_ref=fulls[a].at[1 - c, k], dst_ref=sibs[a], send_sem=s_send.at[a], recv_sem=s_recv.at[a],
                device_id=(x_, y_, 1 - c), device_id_type=MESH)

        def to_sibling(slot, land):
            return pltpu.make_async_remote_copy(
                src_ref=st_ref.at[slot], dst_ref=b1_ref.at[land, :, cols], send_sem=d2d_send.at[slot],
                recv_sem=d2d_recv.at[land * nb + n], device_id=(x_, y_, 1 - c), device_id_type=MESH)

        def to_chip(r):
            cx, cy = chips[r]
            return pltpu.make_async_remote_copy(
                src_ref=wire_ref.at[r, :, cols], dst_ref=b2_ref.at[r, :, cols], send_sem=ici_send.at[r],
                recv_sem=ici_recv.at[r], device_id=(cx, cy, c), device_id_type=MESH)

        def all_of_chip(r):
            cx, cy = chips[r]
            return pltpu.make_async_remote_copy(
                src_ref=wire_ref.at[r], dst_ref=b2_ref.at[r], send_sem=ici_send.at[r],
                recv_sem=ici_recv.at[r], device_id=(cx, cy, c), device_id_type=MESH)

        def to_result(slot):
            return pltpu.make_async_copy(st_ref.at[slot], qk_ref.at[:, cols], own_sem.at[slot])

        def sibling_piece(land):
            return pltpu.make_async_copy(b1_ref.at[land, :, cols], tmp_ref, tmp_sem)

        @pl.when(s == 0)
        def _():
            for a in range(ns):
                to_sibling_whole(a).start()
                for r in range(3):
                    for core in range(2):
                        part(a, r, core).start()

        own_half = ((grp >= 3) & (grp <= 5)) | (grp == 7)
        land = jnp.where(grp == 7, 3, grp - 3)

        @pl.when(own_half)
        def _():
            to_sibling(0, land).wait_recv()
            sibling_piece(land).start()

        @pl.when(s < n_w)
        def _():
            g_ref[...] = _dot(xt_ref[tab_ref[1, s]], dpc_ref[...])

        @pl.when(own_half)
        def _():
            sibling_piece(land).wait()

        for g in range(2 * N_CHIPS):
            @pl.when(grp == g)
            def _(g=g):
                if g in (0, 1, 2, 6):
                    use = s if g < 3 else 3 * nb + n
                    slot = use % 2

                    @pl.when(use >= 2)
                    def _():
                        to_sibling(slot, 0).wait_send()

                    st_ref[slot] = g_ref[...]
                    to_sibling(slot, min(g, 3)).start()
                elif g in (3, 4, 5):
                    sb_ref[...] = (g_ref[...] + tmp_ref[...]).astype(BF16)
                    stage = pltpu.make_async_copy(sb_ref, wire_ref.at[g - 3, :, cols], wire_sem)
                    stage.start()
                    stage.wait()
                    to_chip(g - 3).start()
                else:
                    slot = n % 2
                    piece = g_ref[...] + tmp_ref[...]

                    @pl.when(n < 2)
                    def _():
                        to_sibling(slot, 0).wait_send()

                    @pl.when(n >= 2)
                    def _():
                        to_result(slot).wait()

                    st_ref[slot] = piece
                    to_result(slot).start()

        @pl.when(s >= n_w)
        def _():
            acc = ALPHA * dr_ref[...]
            for j in range(N_CHIPS):
                acc = acc + _dot_nt(dpr_ref[:, j * W_BLOCK:(j + 1) * W_BLOCK], w_ref[j])
            dx_ref[...] = acc

        @pl.when(s == n_w + n_x - 1)
        def _():
            for slot in range(2):
                to_result(slot).wait()
            for r in range(3):
                all_of_chip(r).wait_recv()
                all_of_chip(r).wait_send()
            for a in range(ns):
                to_sibling_whole(a).wait_recv()
                to_sibling_whole(a).wait_send()
                for r in range(3):
                    for core in range(2):
                        landed(a, r, core).wait_recv()
                        part(a, r, core).wait_send()

    any_spec = pl.BlockSpec(memory_space=pl.ANY)
    tile = lambda s, tab: (jnp.maximum(s - n_w, 0), 0)
    return pl.pallas_call(
        body, name="grad_in_and_x",
        grid_spec=pltpu.PrefetchScalarGridSpec(
            num_scalar_prefetch=1, grid=(n_w + n_x,),
            in_specs=[pl.BlockSpec((2, half, t), lambda s, tab: (0, 0, 0), pipeline_mode=pl.Buffered(1)),
                      pl.BlockSpec((t, tn), lambda s, tab: (0, tab[0, jnp.minimum(s, n_w - 1)])),
                      pl.BlockSpec((tm, D_IN), tile),
                      pl.BlockSpec((N_CHIPS, D, W_BLOCK), lambda s, tab: (0, 0, 0), pipeline_mode=pl.Buffered(1)),
                      pl.BlockSpec((tm, D), tile)] + [any_spec] * (2 * ns),
            out_specs=[pl.BlockSpec((tm, D), tile)] + [any_spec] * (4 + 2 * ns),
            scratch_shapes=[pltpu.VMEM((half, tn), F32), pltpu.VMEM((2, half, tn), F32), pltpu.VMEM((half, tn), BF16),
                            pltpu.VMEM((half, tn), F32),
                            pltpu.SemaphoreType.DMA((2,)), pltpu.SemaphoreType.DMA((N_CHIPS * nb,)),
                            pltpu.SemaphoreType.DMA((3,)), pltpu.SemaphoreType.DMA((3,)),
                            pltpu.SemaphoreType.DMA((2,)), pltpu.SemaphoreType.DMA, pltpu.SemaphoreType.DMA,
                            pltpu.SemaphoreType.DMA((6 * ns,)), pltpu.SemaphoreType.DMA((6 * ns,)),
                            pltpu.SemaphoreType.DMA((ns,)), pltpu.SemaphoreType.DMA((ns,))]),
        out_shape=[jax.ShapeDtypeStruct((t, D), F32), jax.ShapeDtypeStruct((half, W_BLOCK), F32),
                   jax.ShapeDtypeStruct((3, half, W_BLOCK), BF16), jax.ShapeDtypeStruct((N_CHIPS, half, W_BLOCK), F32),
                   jax.ShapeDtypeStruct((3, half, W_BLOCK), BF16)]
        + [jax.ShapeDtypeStruct((6,) + w.shape[2:], w.dtype) for w in wires]
        + [jax.ShapeDtypeStruct(g.shape[2:], F32) for g in grads],
        compiler_params=_params("arbitrary"),
    )(table, xt, dp, dp, w4, dr, *wires, *grads)


def kernel(x, w_in, b_in, conv_w, conv_b, gn_g, gn_b, ln_v_g, ln_v_b, w_spatial, b_spatial, w_pa, w_pb, w_o, b_o, ln_out_g, ln_out_b, loss_target, m_w_in, m_b_in, m_conv_w, m_conv_b, m_gn_g, m_gn_b, m_ln_v_g, m_ln_v_b, m_w_spatial, m_b_spatial, m_w_pa, m_w_pb, m_w_o, m_b_o, m_ln_out_g, m_ln_out_b, v_w_in, v_b_in, v_conv_w, v_conv_b, v_gn_g, v_gn_b, v_ln_v_g, v_ln_v_b, v_w_spatial, v_b_spatial, v_w_pa, v_w_pb, v_w_o, v_b_o, v_ln_out_g, v_ln_out_b):
    n_seq, seq, _ = x.shape
    t = n_seq * seq
    tiles_per_seq = seq // TOKEN_TILE
    x2 = x.reshape(t, D)
    tgt = loss_target.reshape(t, D)

    conv_shard = jnp.pad(conv_w, ((0, HALO - CONV_K), (0, 0)))
    p, win4, wpa4, wpb4, wo4, conv4 = _proj_gather(
        x2, b_in,
        _place_shards([w_in, w_pa, w_pb, w_o, conv_shard], [BF16, BF16, BF16, BF16, F32]))
    win4 = win4.reshape(N_CHIPS, D, W_BLOCK)
    wpa, wpb, wo = wpa4.reshape(D, D), wpb4.reshape(D, D), wo4.reshape(D, D)
    convw = conv4.reshape(N_CHIPS, HALO, D // N_CHIPS).transpose(1, 0, 2).reshape(HALO, D)

    vecs = jnp.stack([conv_b, gn_g, gn_b, ln_v_g, ln_v_b, b_o, ln_out_g, ln_out_b])
    causal = jnp.tril(jnp.ones((CHUNK, CHUNK), bool))
    ws = jnp.where(causal[None], w_spatial, 0.0)
    ws_bf, wst_bf = ws.astype(BF16), ws.transpose(0, 2, 1).astype(BF16)
    bsp = jnp.repeat(b_spatial.T, GROUP_W, axis=1)

    h1, ya, yb, h3, s, mixed, dr, drb, xt, acc_f = _forward_tiles(p, x2, tgt, wpa, wpb, wo, convw, vecs, ws_bf, bsp, tiles_per_seq)
    dh1, dp, dya, dyb, acc_b, dbin_b, dws, dbsp_acc = _backward_tiles(p, h1, ya, yb, drb, wpa, wpb, wo, vecs, ws_bf, wst_bf, bsp)
    dp, dcw8, dbin_a, *square = _conv_backward(dh1, p, dp, convw, [(h3, dya), (s, dyb), (mixed, drb)], tiles_per_seq)

    small, g_conv = _pack_small(acc_f, acc_b, dbin_a, dbin_b, dcw8, dws, dbsp_acc)
    small = small.reshape(2, N_CHIPS, SMALL_ROWS // 8, D)

    grads = square[:3] + [g_conv, small]
    wires = square[3:] + [g_conv, small]
    grad_x, q_in, chips_in, _, _, *landed = _grad_in_and_x(xt.reshape(2, D // 2, t), dp, win4, dr, wires, grads)
    grad_x = grad_x.reshape(x.shape)
    mine = [_add_chips(q_in, chips_in)]
    mine += _add_devices(grads, landed[:5], landed[5:])
    *full, small_parts = _share_results(mine[:5], mine[5])
    grad_w_in, grad_w_pa, grad_w_pb, grad_w_o = [f.reshape(w.shape) for f, w in zip(full[:4], (w_in, w_pa, w_pb, w_o))]
    grad_conv_w = full[4].reshape(HALO, D // N_CHIPS)[:CONV_K]

    big = {"w_in": (grad_w_in,) + tuple(_adamw(w_in, grad_w_in, m_w_in, v_w_in))}
    group = _adamw_group([(w_pa, grad_w_pa, m_w_pa, v_w_pa), (w_pb, grad_w_pb, m_w_pb, v_w_pb),
                          (w_o, grad_w_o, m_w_o, v_w_o)], (conv_w, grad_conv_w, m_conv_w, v_conv_w))
    for name, g, res in zip(["w_pa", "w_pb", "w_o", "conv_w"], [grad_w_pa, grad_w_pb, grad_w_o, grad_conv_w], group):
        big[name] = (g,) + tuple(res)
    vec_names = ["conv_b", "gn_g", "gn_b", "ln_v_g", "ln_v_b", "b_o", "ln_out_g", "ln_out_b"]
    vec_triples = [(conv_b, m_conv_b, v_conv_b), (gn_g, m_gn_g, v_gn_g), (gn_b, m_gn_b, v_gn_b),
                   (ln_v_g, m_ln_v_g, v_ln_v_g), (ln_v_b, m_ln_v_b, v_ln_v_b), (b_o, m_b_o, v_b_o),
                   (ln_out_g, m_ln_out_g, v_ln_out_g), (ln_out_b, m_ln_out_b, v_ln_out_b)]
    small_res, loss8 = _adamw_small(
        small_parts.reshape(8, SMALL_ROWS // 8, D), vec_triples, (b_in, m_b_in, v_b_in),
        (w_spatial, m_w_spatial, v_w_spatial), (b_spatial, m_b_spatial, v_b_spatial))
    per_name = dict(zip(vec_names + ["b_in", "w_spatial", "b_spatial"], small_res))

    order = ["w_in", "b_in", "conv_w", "conv_b", "gn_g", "gn_b", "ln_v_g", "ln_v_b", "w_spatial", "b_spatial",
             "w_pa", "w_pb", "w_o", "b_o", "ln_out_g", "ln_out_b"]
    outs = [loss8[0, 0], grad_x]
    for kind in range(4):
        outs += [big[n][kind] if n in big else per_name[n][kind] for n in order]
    return tuple(outs)
```

```python
import math

import jax
import jax.numpy as jnp
from jax import lax
from jax.experimental import pallas as pl
from jax.experimental.pallas import tpu as pltpu

D = 1024
N_GROUPS = 8
GROUP_W = D // N_GROUPS
CHUNK = 128
CONV_K = 31
HALO = 32
D_IN = 8 * D
N_CHIPS = 4
W_BLOCK = D_IN // N_CHIPS
ALPHA = 2.0 ** 0.25
LN_EPS = 1e-5
ADAM_LR, ADAM_B1, ADAM_B2, ADAM_EPS, ADAM_WD, ADAM_STEP = 0.001, 0.9, 0.999, 1e-08, 0.01, 10

TOKEN_TILE = 256
VMEM_LIMIT = 56 * 1024 * 1024
MESH = pl.DeviceIdType.MESH
F32, BF16 = jnp.float32, jnp.bfloat16


def _sigmoid(x):
    return jax.nn.sigmoid(x)


def _gelu(x):
    c = math.sqrt(2.0 / math.pi)
    t = jnp.tanh(c * (x + 0.044715 * (x * x * x)))
    return x * (0.5 * (1.0 + t))


def _gelu_and_grad(x):
    c = math.sqrt(2.0 / math.pi)
    x2 = x * x
    t = jnp.tanh(c * (x + 0.044715 * (x2 * x)))
    cdf = 0.5 * (1.0 + t)
    return x * cdf, cdf + 0.5 * x * (1.0 - t * t) * (c * (1.0 + 3.0 * 0.044715 * x2))


def _norm_stats(v):
    mu = jnp.mean(v, axis=-1, keepdims=True)
    vc = v - mu
    var = jnp.mean(vc * vc, axis=-1, keepdims=True)
    rstd = lax.rsqrt(var + LN_EPS)
    return vc * rstd, rstd


def _norm_bwd(dxhat, xhat, rstd):
    m1 = jnp.mean(dxhat, axis=-1, keepdims=True)
    m2 = jnp.mean(dxhat * xhat, axis=-1, keepdims=True)
    return rstd * (dxhat - m1 - xhat * m2)


def _dot(a, b):
    return jnp.dot(a, b, preferred_element_type=F32)


def _dot_nt(a, b):
    return lax.dot_general(a, b, (((1,), (1,)), ((), ())), preferred_element_type=F32)


def _dot_tn(a, b):
    return lax.dot_general(a, b, (((0,), (0,)), ((), ())), preferred_element_type=F32)


def _colsum(v):
    return jnp.sum(v, axis=0, keepdims=True)


def _full(shape):
    return pl.BlockSpec(shape, lambda *_: (0,) * len(shape))


def _resident(shape):
    return pl.BlockSpec(shape, lambda *_: (0,) * len(shape), pipeline_mode=pl.Buffered(1))


def _params(*sem):
    return pltpu.CompilerParams(dimension_semantics=sem, vmem_limit_bytes=VMEM_LIMIT)


def _chip_index():
    return (2 * lax.axis_index("x") + lax.axis_index("y")).astype(jnp.int32).reshape(1)


def _core_index():
    return lax.axis_index("c").astype(jnp.int32).reshape(1)


def _place_shards(ws, dtypes):
    n = len(ws)

    def body(k_ref, *refs):
        for w_ref, o_ref, dtype in zip(refs[:n], refs[n:], dtypes):
            rows = w_ref.shape[0] // 2
            for h in range(2):
                o_ref[h] = w_ref[h * rows:(h + 1) * rows, :].astype(dtype)

    return pl.pallas_call(
        body, name="place_shards",
        grid_spec=pltpu.PrefetchScalarGridSpec(
            num_scalar_prefetch=1, grid=(1,),
            in_specs=[pl.BlockSpec(w.shape, lambda i, k: (0, 0)) for w in ws],
            out_specs=[pl.BlockSpec((None, 2, w.shape[0] // 2, w.shape[1]), lambda i, k: (k[0], 0, 0, 0)) for w in ws]),
        out_shape=[jax.ShapeDtypeStruct((N_CHIPS, 2, w.shape[0] // 2, w.shape[1]), dt) for w, dt in zip(ws, dtypes)],
        compiler_params=_params("arbitrary"),
    )(_chip_index(), *ws)


def _position():
    x, y, c = lax.axis_index("x"), lax.axis_index("y"), lax.axis_index("c")
    return x, y, c, 2 * x + y


def _other_chips(x, y):
    return [(1 - x, y), (x, 1 - y), (1 - x, 1 - y)]


def _any_specs(n):
    return [pl.BlockSpec(memory_space=pl.ANY)] * n


def _proj_gather(x, b_in, bufs):
    t = x.shape[0]
    tm = 1024
    steps = t // tm
    ahead = steps // 2
    half = D // 2
    chunk = W_BLOCK // 2
    n = len(bufs)
    xi, yi = lax.axis_index("x"), lax.axis_index("y")
    chips = [2 * xi + yi, 2 * (1 - xi) + yi, 2 * xi + (1 - yi), 2 * (1 - xi) + (1 - yi)]
    plan = [(0, 0), (0, 1), (1, 0), (2, 1), (1, 1), (2, 0), (3, 0), (3, 1)]
    order = jnp.stack([2 * chips[ch] + q for ch, q in plan]).astype(jnp.int32)

    def body(order_ref, x_ref, b_ref, *refs):
        p_ref, outs = refs[n], refs[n + 1:2 * n + 1]
        xb_ref, w_ref, lsem, send, recv, hop_send, hop_recv, fsend, frecv, qsend, qrecv = refs[2 * n + 1:]
        jj, i = pl.program_id(0), pl.program_id(1)
        x_, y_, c, k = _position()
        nbrs = [(1 - x_, y_), (x_, 1 - y_)]
        blocks = [2 * (1 - x_) + y_, 2 * x_ + (1 - y_), 2 * (1 - x_) + (1 - y_)]

        def quarter(a, block, q, h):
            if a == 0:
                return outs[0].at[block, h, :, pl.ds(q * chunk, chunk)]
            rows = outs[a].shape[2] // 2
            return outs[a].at[block, h, pl.ds(q * rows, rows)]

        def copy(ref, to, send_sem, recv_sem):
            return pltpu.make_async_remote_copy(src_ref=ref, dst_ref=ref, send_sem=send_sem, recv_sem=recv_sem,
                                                device_id=(to[0], to[1], c), device_id_type=MESH)

        def sent(a, nb, q):
            return copy(quarter(a, k, q, c), nbrs[nb], send.at[4 * a + 2 * nb + q], recv.at[4 * a + 2 * nb + q])

        def landed(a, nb, q):
            return copy(quarter(a, blocks[nb], q, c), nbrs[nb], send.at[4 * a + 2 * nb + q], recv.at[4 * a + 2 * nb + q])

        def hopped(a, nb):
            return copy(quarter(a, blocks[nb], nb, c), nbrs[1 - nb], hop_send.at[2 * a + nb], hop_recv.at[2 * a + 1 - nb])

        def from_diagonal(a, via):
            return copy(quarter(a, blocks[2], 1 - via, c), nbrs[via], hop_send.at[2 * a + via], hop_recv.at[2 * a + via])

        def passed(a, r, h):
            return pltpu.make_async_remote_copy(
                src_ref=outs[a].at[blocks[r], h], dst_ref=outs[a].at[blocks[r], h], send_sem=fsend.at[3 * a + r],
                recv_sem=frecv.at[3 * a + r], device_id=(x_, y_, 1 - c), device_id_type=MESH)

        def passed_quarter(r, q, h):
            ref = quarter(0, blocks[r], q, h)
            return pltpu.make_async_remote_copy(
                src_ref=ref, dst_ref=ref, send_sem=qsend.at[2 * r + q], recv_sem=qrecv.at[2 * r + q],
                device_id=(x_, y_, 1 - c), device_id_type=MESH)

        def load(block, q, slot):
            return [pltpu.make_async_copy(quarter(0, block, q, h), w_ref.at[slot, pl.ds(h * half, half)],
                                          lsem.at[2 * slot + h]) for h in range(2)]

        def pass_on(arrays):
            for a in arrays:
                for nb in range(2):
                    landed(a, nb, nb).wait_recv()
                    hopped(a, nb).start()

        @pl.when((jj == 0) & (i == 0))
        def _():
            for a in range(n):
                for nb, q in ((0, 0), (1, 1), (0, 1), (1, 0)):
                    sent(a, nb, q).start()
            for cp in load(k, 0, 0):
                cp.start()

        for nxt in range(1, len(plan)):
            @pl.when((jj == nxt - 1) & (i == ahead))
            def _(nxt=nxt):
                ch, q = plan[nxt]
                if ch in (1, 2):
                    landed(0, ch - 1, q).wait_recv()
                    if q == ch - 1:
                        hopped(0, ch - 1).start()
                elif ch == 3:
                    from_diagonal(0, 1 - q).wait_recv()
                if ch:
                    passed_quarter(ch - 1, q, c).start()
                    passed_quarter(ch - 1, q, 1 - c).wait_recv()
                for cp in load(k if ch == 0 else blocks[ch - 1], q, nxt % 2):
                    cp.start()
                if nxt == 5:
                    pass_on(range(1, n))
                    for a in range(1, n):
                        landed(a, 0, 1).wait_recv()
                        passed(a, 0, c).start()
                        landed(a, 1, 0).wait_recv()
                        passed(a, 1, c).start()

        slot = jj % 2

        @pl.when(i == 0)
        def _():
            for cp in load(k, 0, slot):
                cp.wait()

        rows = pl.ds(pl.multiple_of(i * tm, tm), tm)

        @pl.when(jj == 0)
        def _():
            xb_ref[rows, :] = x_ref[...].astype(BF16)

        p_ref[...] = _dot(xb_ref[rows, :], w_ref[slot]) + b_ref[...]

        @pl.when((jj == len(plan) - 1) & (i == steps - 1))
        def _():
            for a in range(1, n):
                from_diagonal(a, 0).wait_recv()
                from_diagonal(a, 1).wait_recv()
                passed(a, 2, c).start()
            for a in range(1, n):
                for r in range(3):
                    passed(a, r, 1 - c).wait_recv()
                    passed(a, r, c).wait_send()
            for r in range(3):
                for q in range(2):
                    passed_quarter(r, q, c).wait_send()
            for a in range(n):
                for nb in range(2):
                    for q in range(2):
                        sent(a, nb, q).wait_send()
                    hopped(a, nb).wait_send()

    any_spec = pl.BlockSpec(memory_space=pl.ANY)
    return pl.pallas_call(
        body, name="proj_gather",
        grid_spec=pltpu.PrefetchScalarGridSpec(
            num_scalar_prefetch=1, grid=(len(plan), steps),
            in_specs=[pl.BlockSpec((tm, D), lambda jj, i, o: (jnp.where(jj == 0, i, steps - 1), 0)),
                      pl.BlockSpec((None, 1, chunk), lambda jj, i, o: (o[jj], 0, 0))] + [any_spec] * n,
            out_specs=[pl.BlockSpec((tm, chunk), lambda jj, i, o: (i, o[jj]))] + [any_spec] * n,
            scratch_shapes=[pltpu.VMEM((t, D), BF16), pltpu.VMEM((2, D, chunk), BF16), pltpu.SemaphoreType.DMA((4,)),
                            pltpu.SemaphoreType.DMA((4 * n,)), pltpu.SemaphoreType.DMA((4 * n,)),
                            pltpu.SemaphoreType.DMA((2 * n,)), pltpu.SemaphoreType.DMA((2 * n,)),
                            pltpu.SemaphoreType.DMA((3 * n,)), pltpu.SemaphoreType.DMA((3 * n,)),
                            pltpu.SemaphoreType.DMA((6,)), pltpu.SemaphoreType.DMA((6,))]),
        out_shape=[jax.ShapeDtypeStruct((t, D_IN), F32)] + [jax.ShapeDtypeStruct(b.shape, b.dtype) for b in bufs],
        input_output_aliases={3 + a: 1 + a for a in range(n)},
        compiler_params=_params("arbitrary", "arbitrary"),
    )(order, x, b_in.reshape(D_IN // chunk, 1, chunk), *bufs)


def _share_results(bufs, small):
    n = len(bufs)

    def body(*refs):
        outs, small_out = refs[n + 1:2 * n + 1], refs[2 * n + 1]
        send, recv, ssend, srecv = refs[2 * n + 2:]
        x, y, c, k = _position()
        cps = []
        for a in range(n):
            cp = pltpu.make_async_remote_copy(
                src_ref=outs[a].at[c], dst_ref=outs[a].at[c], send_sem=send.at[a], recv_sem=recv.at[a],
                device_id=(x, y, 1 - c), device_id_type=MESH)
            cp.start()
            cps.append(cp)
        waits = []
        for p in range(1, 8):
            px, py, pc = x ^ (p >> 2), y ^ ((p >> 1) & 1), c ^ (p & 1)
            cp = pltpu.make_async_remote_copy(
                src_ref=small_out.at[k, c], dst_ref=small_out.at[k, c], send_sem=ssend.at[p - 1],
                recv_sem=srecv.at[p - 1], device_id=(px, py, pc), device_id_type=MESH)
            cp.start()
            cps.append(cp)
            waits.append(pltpu.make_async_remote_copy(
                src_ref=small_out.at[2 * px + py, pc], dst_ref=small_out.at[2 * px + py, pc], send_sem=ssend.at[p - 1],
                recv_sem=srecv.at[p - 1], device_id=(px, py, pc), device_id_type=MESH))
        for a in range(n):
            pltpu.make_async_remote_copy(
                src_ref=outs[a].at[1 - c], dst_ref=outs[a].at[1 - c], send_sem=send.at[a], recv_sem=recv.at[a],
                device_id=(x, y, 1 - c), device_id_type=MESH).wait_recv()
        for w in waits:
            w.wait_recv()
        for cp in cps:
            cp.wait_send()

    return pl.pallas_call(
        body, name="rs_share_results",
        in_specs=_any_specs(n + 1), out_specs=_any_specs(n + 1),
        out_shape=[jax.ShapeDtypeStruct(b.shape, b.dtype) for b in bufs + [small]],
        scratch_shapes=[pltpu.SemaphoreType.DMA((n,)), pltpu.SemaphoreType.DMA((n,)),
                        pltpu.SemaphoreType.DMA((7,)), pltpu.SemaphoreType.DMA((7,))],
        input_output_aliases={a: a for a in range(n + 1)},
    )(*bufs, small)


def _row_tile(r, c):
    t = max(8, min(r, (1 << 18) // c))
    while r % t:
        t //= 2
    return t


def _add_devices(grads, lands, sibs):
    n = len(grads)

    def body(kc_ref, *refs):
        for g_ref, l_ref, s_ref, f_ref in zip(refs[:n], refs[n:2 * n], refs[2 * n:3 * n], refs[3 * n:]):
            f = g_ref[...] + s_ref[...]
            for i in range(l_ref.shape[0]):
                f = f + l_ref[i].astype(F32)
            f_ref[...] = f

    shapes = [g.shape[2:] for g in grads]
    out_specs = [pl.BlockSpec((None,) + sh, lambda i, kc: (kc[1], 0, 0)) for sh in shapes[:-1]]
    out_specs.append(pl.BlockSpec((None, None) + shapes[-1], lambda i, kc: (kc[0], kc[1], 0, 0)))
    out_shape = [jax.ShapeDtypeStruct((2,) + sh, F32) for sh in shapes[:-1]]
    out_shape.append(jax.ShapeDtypeStruct((N_CHIPS, 2) + shapes[-1], F32))
    return pl.pallas_call(
        body, name="rs_add_devices",
        grid_spec=pltpu.PrefetchScalarGridSpec(
            num_scalar_prefetch=1, grid=(1,),
            in_specs=[pl.BlockSpec((None, None) + sh, lambda i, kc: (kc[1], kc[0], 0, 0)) for sh in shapes]
            + [pl.BlockSpec(l.shape, lambda i, kc: (0, 0, 0)) for l in lands]
            + [pl.BlockSpec(sh, lambda i, kc: (0, 0)) for sh in shapes],
            out_specs=out_specs),
        out_shape=out_shape,
        compiler_params=_params("arbitrary"),
    )(jnp.concatenate([_chip_index(), _core_index()]), *grads, *lands, *sibs)


def _add_chips(q, b2):
    r, c = q.shape
    t = _row_tile(r, c)

    def body(c_ref, q_ref, b_ref, f_ref):
        f_ref[...] = ((q_ref[...] + b_ref[0].astype(F32)) + b_ref[1].astype(F32)) + b_ref[2].astype(F32)

    return pl.pallas_call(
        body, name="rs_add_chips",
        grid_spec=pltpu.PrefetchScalarGridSpec(
            num_scalar_prefetch=1, grid=(r // t,),
            in_specs=[pl.BlockSpec((t, c), lambda i, cr: (i, 0)), pl.BlockSpec((3, t, c), lambda i, cr: (0, i, 0))],
            out_specs=pl.BlockSpec((None, t, c), lambda i, cr: (cr[0], i, 0))),
        out_shape=jax.ShapeDtypeStruct((2, r, c), F32),
        compiler_params=_params("parallel"),
    )(_core_index(), q, b2)


def _adamw_math(w, g, m, v):
    m = ADAM_B1 * m + (1.0 - ADAM_B1) * g
    v = ADAM_B2 * v + (1.0 - ADAM_B2) * (g * g)
    m_hat = m / (1.0 - ADAM_B1 ** ADAM_STEP)
    v_hat = v / (1.0 - ADAM_B2 ** ADAM_STEP)
    delta = -ADAM_LR * (m_hat / (jnp.sqrt(v_hat) + ADAM_EPS) + ADAM_WD * w)
    return delta, m, v


def _adamw_group(quads, conv):
    n = len(quads)
    r, c = quads[0][0].shape

    def body(*refs):
        ins, outs = refs[:4 * (n + 1)], refs[4 * (n + 1):]
        for i in range(n + 1):
            w_ref, g_ref, m_ref, v_ref = ins[4 * i:4 * i + 4]
            res = _adamw_math(w_ref[...], g_ref[...], m_ref[...], v_ref[...])
            for o_ref, val in zip(outs[3 * i:3 * i + 3], res):
                o_ref[...] = val

    half = pl.BlockSpec((r // 2, c), lambda i: (i, 0))
    whole = pl.BlockSpec(conv[0].shape, lambda i: (0, 0))
    res = pl.pallas_call(
        body, name="adamw_group", grid=(2,),
        in_specs=[half] * (4 * n) + [whole] * 4, out_specs=[half] * (3 * n) + [whole] * 3,
        out_shape=[jax.ShapeDtypeStruct((r, c), F32)] * (3 * n) + [jax.ShapeDtypeStruct(conv[0].shape, F32)] * 3,
        compiler_params=_params("arbitrary"),
    )(*[a for quad in quads for a in quad], *conv)
    return [res[3 * i:3 * i + 3] for i in range(n + 1)]


def _adamw(w, g, m, v):
    r, c = w.shape
    t = _row_tile(r, c) if r % 8 == 0 else r

    def body(w_ref, g_ref, m_ref, v_ref, d_ref, nm_ref, nv_ref):
        d_ref[...], nm_ref[...], nv_ref[...] = _adamw_math(w_ref[...], g_ref[...], m_ref[...], v_ref[...])

    spec = pl.BlockSpec((t, c), lambda i: (i, 0))
    return pl.pallas_call(
        body, name="adamw", grid=(r // t,), in_specs=[spec] * 4, out_specs=[spec] * 3,
        out_shape=[jax.ShapeDtypeStruct((r, c), F32)] * 3, compiler_params=_params("parallel"),
    )(w, g, m, v)


ROW_B_IN = 0
ROW_VECS = 8
ROW_LOSS = 16
ROW_B_SPATIAL = 24
ROW_W_SPATIAL = 32
SMALL_ROWS = 192
N_VECS = 8


def _pack_small(acc_f, acc_b, dbin_a, dbin_b, dcw8, dws, dbsp):
    cols = D // N_CHIPS

    def body(af_ref, ab_ref, da_ref, db_ref, cw_ref, ws_ref, bs_ref, o_ref, gc_ref):
        o_ref[...] = jnp.zeros_like(o_ref)
        for j in range(D_IN // D):
            src = da_ref if j < 2 else db_ref
            o_ref[ROW_B_IN + j:ROW_B_IN + j + 1, :] = src[0:1, j * D:(j + 1) * D]
        dcw = jnp.sum(cw_ref[...].reshape(HALO, SUBLANES, D), axis=1)
        o_ref[ROW_VECS:ROW_VECS + 1, :] = dcw[CONV_K:CONV_K + 1]
        o_ref[ROW_VECS + 1:ROW_VECS + 5, :] = ab_ref[0:4, :]
        o_ref[ROW_VECS + 5:ROW_VECS + 6, :] = af_ref[2:3, :]
        o_ref[ROW_VECS + 6:ROW_VECS + 8, :] = af_ref[0:2, :]
        o_ref[ROW_LOSS:ROW_LOSS + 1, :] = af_ref[3:4, :]
        head = lax.broadcasted_iota(jnp.int32, (N_GROUPS, D), 0)
        lane = lax.broadcasted_iota(jnp.int32, (N_GROUPS, D), 1)
        indicator = jnp.where(lane // GROUP_W == head, 1.0, 0.0)
        o_ref[ROW_B_SPATIAL:ROW_B_SPATIAL + N_GROUPS, 0:CHUNK] = lax.dot_general(
            indicator, bs_ref[...], (((1,), (1,)), ((), ())), precision=lax.Precision.HIGHEST, preferred_element_type=F32)
        t_idx = lax.broadcasted_iota(jnp.int32, (CHUNK, D), 0)
        s_idx = lax.broadcasted_iota(jnp.int32, (CHUNK, D), 1) % CHUNK
        o_ref[ROW_W_SPATIAL:ROW_W_SPATIAL + CHUNK, :] = jnp.where(s_idx <= t_idx, ws_ref[...], 0.0)
        for h in range(2):
            for j in range(N_CHIPS):
                gc_ref[h, j] = dcw[h * (HALO // 2):(h + 1) * (HALO // 2), j * cols:(j + 1) * cols]

    ins = [acc_f, acc_b, dbin_a, dbin_b, dcw8, dws, dbsp]
    return pl.pallas_call(
        body, name="pack_small",
        in_specs=[_full(a.shape) for a in ins],
        out_specs=[_full((SMALL_ROWS, D)), _full((2, N_CHIPS, HALO // 2, cols))],
        out_shape=[jax.ShapeDtypeStruct((SMALL_ROWS, D), F32), jax.ShapeDtypeStruct((2, N_CHIPS, HALO // 2, cols), F32)],
        compiler_params=_params(),
    )(*ins)


def _adamw_small(parts, vecs, b_in, w_spatial, b_spatial):
    triples = list(vecs) + [b_in, w_spatial, b_spatial]
    n_in = 3 * len(triples)

    def body(p_ref, *refs):
        ins = [refs[3 * i:3 * i + 3] for i in range(len(triples))]
        outs = [refs[n_in + 4 * i:n_in + 4 * i + 4] for i in range(len(triples))]
        loss_ref, g_ref = refs[n_in + 4 * len(triples):]
        rows = SMALL_ROWS // 8
        for k in range(N_CHIPS):
            for core in range(2):
                g_ref[(core * N_CHIPS + k) * rows:(core * N_CHIPS + k + 1) * rows, :] = p_ref[2 * k + core]

        def step(g, wmv, out, get, put):
            d, nm, nv = _adamw_math(get(wmv[0]), g, get(wmv[1]), get(wmv[2]))
            for o, val in zip(out, (g, d, nm, nv)):
                put(o, val)

        for i in range(N_VECS):
            step(g_ref[ROW_VECS + i:ROW_VECS + i + 1, :], ins[i], outs[i],
                 lambda r: r[...].reshape(1, D), lambda o, val: o.__setitem__(Ellipsis, val.reshape(D)))
        for j in range(D_IN // D):
            piece = pl.ds(j * D, D)
            step(g_ref[ROW_B_IN + j:ROW_B_IN + j + 1, :], ins[N_VECS], outs[N_VECS],
                 lambda r: r[piece].reshape(1, D), lambda o, val: o.__setitem__(piece, val.reshape(D)))
        for h in range(N_GROUPS):
            step(g_ref[ROW_W_SPATIAL:ROW_W_SPATIAL + CHUNK, h * CHUNK:(h + 1) * CHUNK], ins[N_VECS + 1], outs[N_VECS + 1],
                 lambda r: r[h], lambda o, val: o.__setitem__(h, val))
        step(g_ref[ROW_B_SPATIAL:ROW_B_SPATIAL + N_GROUPS, 0:CHUNK], ins[N_VECS + 2], outs[N_VECS + 2],
             lambda r: r[...], lambda o, val: o.__setitem__(Ellipsis, val))
        lanes = g_ref[ROW_LOSS:ROW_LOSS + 1, :]
        loss_ref[...] = jnp.broadcast_to(jnp.sum(lanes, axis=1, keepdims=True), (8, 128))

    flat = [a for tr in triples for a in tr]
    out_shape = [jax.ShapeDtypeStruct(tr[0].shape, F32) for tr in triples for _ in range(4)]
    out_shape.append(jax.ShapeDtypeStruct((8, 128), F32))
    res = pl.pallas_call(
        body, name="adamw_small",
        in_specs=[_full(parts.shape)] + [_full(a.shape) for a in flat],
        out_specs=[_full(o.shape) for o in out_shape],
        out_shape=out_shape,
        scratch_shapes=[pltpu.VMEM((SMALL_ROWS, D), F32)],
        compiler_params=_params(),
    )(parts, *flat)
    return [res[4 * i:4 * i + 4] for i in range(len(triples))], res[-1]


SUBLANES = 8
SHIFT_ROWS = HALO - SUBLANES


def _shifted_copies(src_ref, sh_ref, cs, tm):
    for p in range(1, SUBLANES):
        sh_ref[p - 1] = src_ref[pl.ds(p, tm + SHIFT_ROWS), cs]


def _tap(src_ref, sh_ref, cs, offset, start, rows):
    p, q = offset % SUBLANES, offset // SUBLANES
    if p == 0:
        return src_ref[pl.ds(start + SUBLANES * q, rows), cs]
    return sh_ref[p - 1, pl.ds(start + SUBLANES * q, rows), :]


def _conv_taps(src_ref, sh_ref, w_ref, first_offset, step, bias, dst_ref, tm):
    rows = 64
    for g in range(N_GROUPS):
        cs = slice(g * GROUP_W, (g + 1) * GROUP_W)
        _shifted_copies(src_ref, sh_ref, cs, tm)
        for rb in range(tm // rows):
            acc = jnp.zeros((rows, GROUP_W), F32) + (bias[:, cs] if bias is not None else 0.0)
            for k in range(CONV_K):
                acc = acc + w_ref[k:k + 1, cs] * _tap(src_ref, sh_ref, cs, first_offset + step * k, rb * rows, rows)
            dst_ref[rb * rows:(rb + 1) * rows, cs] = acc


def _conv_weight_grad(d_ref, src_ref, sh_ref, first_offset, acc_ref, tm):
    rows = 64
    for g in range(N_GROUPS):
        cs = slice(g * GROUP_W, (g + 1) * GROUP_W)
        _shifted_copies(src_ref, sh_ref, cs, tm)
        for rb in range(tm // rows):
            d = d_ref[rb * rows:(rb + 1) * rows, cs]
            for k in range(CONV_K):
                prod = d * _tap(src_ref, sh_ref, cs, first_offset + k, rb * rows, rows)
                acc_ref[SUBLANES * k:SUBLANES * (k + 1), cs] += jnp.sum(
                    prod.reshape(rows // SUBLANES, SUBLANES, GROUP_W), axis=0)


def _spatial_mix(w_ref, v_bf, tm):
    rows = []
    for q in range(tm // CHUNK):
        cols = [_dot(w_ref[h], v_bf[q * CHUNK:(q + 1) * CHUNK, h * GROUP_W:(h + 1) * GROUP_W])
                for h in range(N_GROUPS)]
        rows.append(jnp.concatenate(cols, axis=1))
    return jnp.concatenate(rows, axis=0)


def _group_norm_fwd(h1, gn_g, gn_b):
    xhat, rstd = [], []
    for g in range(N_GROUPS):
        xh, rs = _norm_stats(h1[:, g * GROUP_W:(g + 1) * GROUP_W])
        xhat.append(xh)
        rstd.append(rs)
    xhat = jnp.concatenate(xhat, axis=1)
    return xhat * gn_g + gn_b, xhat, rstd


def _forward_tiles(p, x, tgt, wpa, wpb, wo, convw, vecs, ws, bsp, tiles_per_seq):
    t = x.shape[0]
    tm = TOKEN_TILE
    hb = tm // HALO

    def body(p_ref, ph_ref, x_ref, t_ref, wpa_ref, wpb_ref, wo_ref, cw_ref, vec_ref, ws_ref, bsp_ref,
             h1_ref, ya_ref, yb_ref, h3_ref, s_ref, mx_ref, dr_ref, drb_ref, xt_ref, acc_ref, he_ref, sh_ref):
        i = pl.program_id(0)
        xt_ref[...] = x_ref[...].T.astype(BF16)
        conv_b, gn_g, gn_b, lnv_g, lnv_b, b_o, lno_g, lno_b = [vec_ref[j:j + 1, :] for j in range(8)]

        keep = jnp.where(i % tiles_per_seq == 0, 0.0, 1.0)
        he_ref[0:HALO, :] = ph_ref[:, 0:D] * _sigmoid(ph_ref[:, D:2 * D]) * keep
        he_ref[HALO:, :] = p_ref[:, 0:D] * _sigmoid(p_ref[:, D:2 * D])
        _conv_taps(he_ref, sh_ref, cw_ref, HALO - (CONV_K - 1), 1, conv_b, h1_ref, tm)
        h2, _, _ = _group_norm_fwd(h1_ref[...], gn_g, gn_b)
        a_gate = p_ref[:, 2 * D:3 * D]
        h3 = ((h2 * _sigmoid(h2)) * (a_gate * _sigmoid(a_gate))).astype(BF16)
        h3_ref[...] = h3
        ya = _dot(h3, wpa_ref[...])
        ya_ref[...] = ya

        u = _gelu(p_ref[:, 3 * D:4 * D])
        vhat, _ = _norm_stats(_gelu(p_ref[:, 4 * D:5 * D]))
        v1 = (vhat * lnv_g + lnv_b).astype(BF16)
        b_gate = p_ref[:, 5 * D:6 * D]
        vmix = _spatial_mix(ws_ref, v1, tm) + jnp.concatenate([bsp_ref[...]] * (tm // CHUNK), axis=0)
        s = (u * vmix * (b_gate * _sigmoid(b_gate))).astype(BF16)
        s_ref[...] = s
        yb = _dot(s, wpb_ref[...])
        yb_ref[...] = yb

        mixed = (_sigmoid(p_ref[:, 6 * D:7 * D]) * ya + _sigmoid(p_ref[:, 7 * D:8 * D]) * yb).astype(BF16)
        mx_ref[...] = mixed
        r = ALPHA * x_ref[...] + (_dot(mixed, wo_ref[...]) + b_o)
        xhat, rstd = _norm_stats(r)
        err = (xhat * lno_g + lno_b) - t_ref[...]
        dout = err * (1.0 / D)
        dr = _norm_bwd(dout * lno_g, xhat, rstd)
        dr_ref[...] = dr
        drb_ref[...] = dr.astype(BF16)

        @pl.when(i == 0)
        def _():
            acc_ref[...] = jnp.zeros_like(acc_ref)

        acc_ref[0:1, :] += _colsum(dout * xhat)
        acc_ref[1:2, :] += _colsum(dout)
        acc_ref[2:3, :] += _colsum(dr)
        acc_ref[3:4, :] += _colsum(err * err) * (0.5 / D)

    tile = lambda w: pl.BlockSpec((tm, w), lambda i: (i, 0))
    f32_out = jax.ShapeDtypeStruct((t, D), F32)
    bf_out = jax.ShapeDtypeStruct((t, D), BF16)
    return pl.pallas_call(
        body, name="forward_tiles", grid=(t // tm,),
        in_specs=[tile(D_IN),
                  pl.BlockSpec((HALO, 2 * D), lambda i: (jnp.maximum(i * hb - 1, 0), 0)),
                  tile(D), tile(D), _resident((D, D)), _resident((D, D)), _resident((D, D)), _full((HALO, D)), _full((8, D)),
                  _full((N_GROUPS, CHUNK, CHUNK)), _full((CHUNK, D))],
        out_specs=[tile(D)] * 8 + [pl.BlockSpec((D, tm), lambda i: (0, i)), _full((8, D))],
        out_shape=[f32_out, f32_out, f32_out, bf_out, bf_out, bf_out, f32_out, bf_out,
                   jax.ShapeDtypeStruct((D, t), BF16), jax.ShapeDtypeStruct((8, D), F32)],
        scratch_shapes=[pltpu.VMEM((tm + HALO, D), F32), pltpu.VMEM((SUBLANES - 1, tm + SHIFT_ROWS, GROUP_W), F32)],
        compiler_params=_params("arbitrary"),
    )(p, p, x, tgt, wpa, wpb, wo, convw, vecs, ws, bsp)


def _backward_tiles(p, h1, ya, yb, drb, wpa, wpb, wo, vecs, ws, wst, bsp):
    t = h1.shape[0]
    tm = TOKEN_TILE

    def body(p_ref, h1_ref, ya_ref, yb_ref, drb_ref, wpa_ref, wpb_ref, wo_ref, vec_ref, ws_ref, wst_ref, bsp_ref,
             dh1_ref, dp_ref, dya_ref, dyb_ref, acc_ref, dbin_ref, dws_ref, dbsp_ref):
        i = pl.program_id(0)
        _, gn_g, gn_b, lnv_g, lnv_b = [vec_ref[j:j + 1, :] for j in range(5)]

        @pl.when(i == 0)
        def _():
            acc_ref[...] = jnp.zeros_like(acc_ref)
            dbin_ref[...] = jnp.zeros_like(dbin_ref)
            dws_ref[...] = jnp.zeros_like(dws_ref)
            dbsp_ref[...] = jnp.zeros_like(dbsp_ref)

        def emit(block, val):
            dbin_ref[0:1, block * D:(block + 1) * D] += _colsum(val)
            dp_ref[:, block * D:(block + 1) * D] = val.astype(BF16)

        dp_ref[:, 0:2 * D] = jnp.zeros((tm, 2 * D), BF16)
        dmixed = _dot_nt(drb_ref[...], wo_ref[...])
        ga = _sigmoid(p_ref[:, 6 * D:7 * D])
        gb = _sigmoid(p_ref[:, 7 * D:8 * D])
        dya = (dmixed * ga).astype(BF16)
        dyb = (dmixed * gb).astype(BF16)
        dya_ref[...] = dya
        dyb_ref[...] = dyb
        emit(6, dmixed * ya_ref[...] * (ga * (1.0 - ga)))
        emit(7, dmixed * yb_ref[...] * (gb * (1.0 - gb)))

        dh3 = _dot_nt(dya, wpa_ref[...])
        h2, xhat, rstd = _group_norm_fwd(h1_ref[...], gn_g, gn_b)
        sg = _sigmoid(h2)
        a_gate = p_ref[:, 2 * D:3 * D]
        sa = _sigmoid(a_gate)
        dh2 = dh3 * (a_gate * sa) * (sg * (1.0 + h2 * (1.0 - sg)))
        emit(2, dh3 * (h2 * sg) * (sa * (1.0 + a_gate * (1.0 - sa))))
        acc_ref[0:1, :] += _colsum(dh2 * xhat)
        acc_ref[1:2, :] += _colsum(dh2)
        dxhat = dh2 * gn_g
        for g in range(N_GROUPS):
            cs = slice(g * GROUP_W, (g + 1) * GROUP_W)
            dh1_ref[:, cs] = _norm_bwd(dxhat[:, cs], xhat[:, cs], rstd[g])

        ds = _dot_nt(dyb, wpb_ref[...])
        u_pre = p_ref[:, 3 * D:4 * D]
        u, du_dpre = _gelu_and_grad(u_pre)
        v0, dv_dpre = _gelu_and_grad(p_ref[:, 4 * D:5 * D])
        vhat, vrstd = _norm_stats(v0)
        v1 = (vhat * lnv_g + lnv_b).astype(BF16)
        vmix = _spatial_mix(ws_ref, v1, tm) + jnp.concatenate([bsp_ref[...]] * (tm // CHUNK), axis=0)
        b_gate = p_ref[:, 5 * D:6 * D]
        sb = _sigmoid(b_gate)
        silu_b = b_gate * sb
        emit(3, ds * vmix * silu_b * du_dpre)
        emit(5, ds * u * vmix * (sb * (1.0 + b_gate * (1.0 - sb))))
        dvmix = ds * u * silu_b
        dvmix_bf = dvmix.astype(BF16)
        for q in range(tm // CHUNK):
            dbsp_ref[...] += dvmix[q * CHUNK:(q + 1) * CHUNK, :]
            for h in range(N_GROUPS):
                blk = (slice(q * CHUNK, (q + 1) * CHUNK), slice(h * GROUP_W, (h + 1) * GROUP_W))
                dws_ref[:, h * GROUP_W:(h + 1) * GROUP_W] += _dot_nt(dvmix_bf[blk], v1[blk])
        dv1 = _spatial_mix(wst_ref, dvmix_bf, tm)
        acc_ref[2:3, :] += _colsum(dv1 * vhat)
        acc_ref[3:4, :] += _colsum(dv1)
        emit(4, _norm_bwd(dv1 * lnv_g, vhat, vrstd) * dv_dpre)

    tile = lambda w: pl.BlockSpec((tm, w), lambda i: (i, 0))
    return pl.pallas_call(
        body, name="backward_tiles", grid=(t // tm,),
        in_specs=[tile(D_IN), tile(D), tile(D), tile(D), tile(D), _resident((D, D)), _resident((D, D)), _resident((D, D)),
                  _full((8, D)), _full((N_GROUPS, CHUNK, CHUNK)), _full((N_GROUPS, CHUNK, CHUNK)), _full((CHUNK, D))],
        out_specs=[tile(D), tile(D_IN), tile(D), tile(D), _full((8, D)), _full((8, D_IN)),
                   _full((CHUNK, D)), _full((CHUNK, D))],
        out_shape=[jax.ShapeDtypeStruct((t, D), F32), jax.ShapeDtypeStruct((t, D_IN), BF16),
                   jax.ShapeDtypeStruct((t, D), BF16), jax.ShapeDtypeStruct((t, D), BF16),
                   jax.ShapeDtypeStruct((8, D), F32), jax.ShapeDtypeStruct((8, D_IN), F32),
                   jax.ShapeDtypeStruct((CHUNK, D), F32), jax.ShapeDtypeStruct((CHUNK, D), F32)],
        compiler_params=_params("arbitrary"),
    )(p, h1, ya, yb, drb, wpa, wpb, wo, vecs, ws, wst, bsp)


def _conv_backward(dh1, p, dp, convw, pairs, tiles_per_seq):
    t = dh1.shape[0]
    tm = TOKEN_TILE
    hb = tm // HALO
    last = t // HALO - 1
    n_sq = len(pairs)
    span = 2
    rows = D // 8

    def body(dh1_ref, dnext_ref, p_ref, ph_ref, cw_ref, dp_in_ref, *refs):
        del dp_in_ref
        sq_in = refs[:2 * n_sq]
        dp_ref, dcw_ref, dbin_ref = refs[2 * n_sq:2 * n_sq + 3]
        sq_out = refs[2 * n_sq + 3:3 * n_sq + 3]
        sq_wire = refs[3 * n_sq + 3:4 * n_sq + 3]
        de_ref, he_ref, dh0_ref, sh_ref, acc_ref, wire_ref, sq_sem, wire_sem = refs[4 * n_sq + 3:]
        i = pl.program_id(0)

        @pl.when(i == 0)
        def _():
            dcw_ref[...] = jnp.zeros_like(dcw_ref)
            dbin_ref[...] = jnp.zeros_like(dbin_ref)
            acc_ref[...] = jnp.zeros_like(acc_ref)

        @pl.when(i % span == span - 1)
        def _():
            for a in range(n_sq):
                acc_ref[a] += _dot_tn(sq_in[2 * a][...], sq_in[2 * a + 1][...])

        keep_next = jnp.where(i % tiles_per_seq == tiles_per_seq - 1, 0.0, 1.0)
        de_ref[0:tm, :] = dh1_ref[...]
        de_ref[tm:, :] = dnext_ref[...] * keep_next
        _conv_taps(de_ref, sh_ref, cw_ref, CONV_K - 1, -1, None, dh0_ref, tm)

        keep_prev = jnp.where(i % tiles_per_seq == 0, 0.0, 1.0)
        sg = _sigmoid(p_ref[:, D:2 * D])
        val = p_ref[:, 0:D]
        he_ref[0:HALO, :] = ph_ref[:, 0:D] * _sigmoid(ph_ref[:, D:2 * D]) * keep_prev
        he_ref[HALO:, :] = val * sg
        _conv_weight_grad(dh1_ref, he_ref, sh_ref, HALO - (CONV_K - 1), dcw_ref, tm)
        dcw_ref[SUBLANES * CONV_K:, :] += jnp.sum(dh1_ref[...].reshape(tm // SUBLANES, SUBLANES, D), axis=0)

        dh0 = dh0_ref[...]
        dval = dh0 * sg
        dglu = dh0 * val * (sg * (1.0 - sg))
        dbin_ref[0:1, 0:D] += _colsum(dval)
        dbin_ref[0:1, D:2 * D] += _colsum(dglu)
        dp_ref[:, 0:D] = dval.astype(BF16)
        dp_ref[:, D:2 * D] = dglu.astype(BF16)

        @pl.when(i == t // tm - 1)
        def _():
            cps = [pltpu.make_async_copy(acc_ref.at[a, pl.ds((2 * j + h) * rows, rows)], sq_out[a].at[h, j],
                                         sq_sem.at[(a * N_CHIPS + j) * 2 + h])
                   for a in range(n_sq) for j in range(N_CHIPS) for h in range(2)]
            for cp in cps:
                cp.start()
            for a in range(n_sq):
                wire_ref[...] = acc_ref[a].astype(BF16)
                narrow = [pltpu.make_async_copy(wire_ref.at[pl.ds((2 * j + h) * rows, rows)], sq_wire[a].at[h, j],
                                                wire_sem.at[2 * j + h]) for j in range(N_CHIPS) for h in range(2)]
                for cp in narrow:
                    cp.start()
                for cp in narrow:
                    cp.wait()
            for cp in cps:
                cp.wait()

    any_spec = pl.BlockSpec(memory_space=pl.ANY)
    wide = pl.BlockSpec((span * tm, D), lambda i: (i // span, 0))
    return pl.pallas_call(
        body, name="conv_backward", grid=(t // tm,),
        in_specs=[pl.BlockSpec((tm, D), lambda i: (i, 0)),
                  pl.BlockSpec((HALO, D), lambda i: (jnp.minimum((i + 1) * hb, last), 0)),
                  pl.BlockSpec((tm, 2 * D), lambda i: (i, 0)),
                  pl.BlockSpec((HALO, 2 * D), lambda i: (jnp.maximum(i * hb - 1, 0), 0)),
                  _full((HALO, D)), any_spec] + [wide] * (2 * n_sq),
        out_specs=[pl.BlockSpec((tm, 2 * D), lambda i: (i, 0)), _full((SUBLANES * HALO, D)), _full((8, 2 * D))]
        + [any_spec] * (2 * n_sq),
        out_shape=[jax.ShapeDtypeStruct(dp.shape, BF16), jax.ShapeDtypeStruct((SUBLANES * HALO, D), F32),
                   jax.ShapeDtypeStruct((8, 2 * D), F32)]
        + [jax.ShapeDtypeStruct((2, N_CHIPS, rows, D), F32)] * n_sq
        + [jax.ShapeDtypeStruct((2, N_CHIPS, rows, D), BF16)] * n_sq,
        scratch_shapes=[pltpu.VMEM((tm + HALO, D), F32), pltpu.VMEM((tm + HALO, D), F32), pltpu.VMEM((tm, D), F32),
                        pltpu.VMEM((SUBLANES - 1, tm + SHIFT_ROWS, GROUP_W), F32), pltpu.VMEM((n_sq, D, D), F32),
                        pltpu.VMEM((D, D), BF16), pltpu.SemaphoreType.DMA((n_sq * N_CHIPS * 2,)),
                        pltpu.SemaphoreType.DMA((N_CHIPS * 2,))],
        input_output_aliases={5: 0},
        compiler_params=_params("arbitrary"),
    )(dh1, dh1, p, p, convw, dp, *[a for pair in pairs for a in pair])


def _grad_in_and_x(xt, dp, w4, dr, wires, grads):
    t = dr.shape[0]
    tm = TOKEN_TILE
    half, tn = D // 2, 512
    nb = W_BLOCK // tn
    n_w, n_x = 2 * N_CHIPS * nb, t // tm
    ns = len(grads)
    xi, yi, ci = lax.axis_index("x"), lax.axis_index("y"), lax.axis_index("c")
    others = [2 * (1 - xi) + yi, 2 * xi + (1 - yi), 2 * (1 - xi) + (1 - yi)]
    blocks = others + others + [2 * xi + yi] * 2
    halves = [1 - ci] * 3 + [ci] * 3 + [1 - ci, ci]
    table = jnp.stack([jnp.stack([b * nb + n for b in blocks for n in range(nb)]),
                       jnp.stack([h for h in halves for _ in range(nb)])]).astype(jnp.int32)

    def body(tab_ref, xt_ref, dpc_ref, dpr_ref, w_ref, dr_ref, *refs):
        parts, fulls = refs[:ns], refs[ns:2 * ns]
        dx_ref, qk_ref, b2_ref, b1_ref, wire_ref = refs[2 * ns:2 * ns + 5]
        lands, sibs = refs[2 * ns + 5:3 * ns + 5], refs[3 * ns + 5:4 * ns + 5]
        (g_ref, st_ref, sb_ref, tmp_ref, d2d_send, d2d_recv, ici_send, ici_recv, own_sem, tmp_sem, wire_sem,
         p_send, p_recv, s_send, s_recv) = refs[4 * ns + 5:]
        s = pl.program_id(0)
        x_, y_, c, k = _position()
        chips = _other_chips(x_, y_)
        n = s % nb
        grp = s // nb
        cols = pl.ds(pl.multiple_of(n * tn, tn), tn)

        def part(a, r, core):
            cx, cy = chips[r]
            return pltpu.make_async_remote_copy(
                src_ref=parts[a].at[core, 2 * cx + cy], dst_ref=lands[a].at[2 * r + c],
                send_sem=p_send.at[6 * a + 2 * r + core], recv_sem=p_recv.at[6 * a + 2 * r + c],
                device_id=(cx, cy, core), device_id_type=MESH)

        def landed(a, r, core):
            cx, cy = chips[r]
            return pltpu.make_async_remote_copy(
                src_ref=lands[a].at[2 * r + core], dst_ref=lands[a].at[2 * r + core],
                send_sem=p_send.at[6 * a + 2 * r + core], recv_sem=p_recv.at[6 * a + 2 * r + core],
                device_id=(cx, cy, core), device_id_type=MESH)

        def to_sibling_whole(a):
            return pltpu.make_async_remote_copy(
                src_ref=fulls[a].at[1 - c, k], dst_ref=sibs[a], send_sem=s_send.at[a], recv_sem=s_recv.at[a],
                device_id=(x_, y_, 1 - c), device_id_type=MESH)

        def to_sibling(slot, land):
            return pltpu.make_async_remote_copy(
                src_ref=st_ref.at[slot], dst_ref=b1_ref.at[land, :, cols], send_sem=d2d_send.at[slot],
                recv_sem=d2d_recv.at[land * nb + n], device_id=(x_, y_, 1 - c), device_id_type=MESH)

        def to_chip(r):
            cx, cy = chips[r]
            return pltpu.make_async_remote_copy(
                src_ref=wire_ref.at[r, :, cols], dst_ref=b2_ref.at[r, :, cols], send_sem=ici_send.at[r],
                recv_sem=ici_recv.at[r], device_id=(cx, cy, c), device_id_type=MESH)

        def all_of_chip(r):
            cx, cy = chips[r]
            return pltpu.make_async_remote_copy(
                src_ref=wire_ref.at[r], dst_ref=b2_ref.at[r], send_sem=ici_send.at[r],
                recv_sem=ici_recv.at[r], device_id=(cx, cy, c), device_id_type=MESH)

        def to_result(slot):
            return pltpu.make_async_copy(st_ref.at[slot], qk_ref.at[:, cols], own_sem.at[slot])

        def sibling_piece(land):
            return pltpu.make_async_copy(b1_ref.at[land, :, cols], tmp_ref, tmp_sem)

        @pl.when(s == 0)
        def _():
            for a in range(ns):
                to_sibling_whole(a).start()
                for r in range(3):
                    for core in range(2):
                        part(a, r, core).start()

        own_half = ((grp >= 3) & (grp <= 5)) | (grp == 7)
        land = jnp.where(grp == 7, 3, grp - 3)

        @pl.when(own_half)
        def _():
            to_sibling(0, land).wait_recv()
            sibling_piece(land).start()

        @pl.when(s < n_w)
        def _():
            g_ref[...] = _dot(xt_ref[tab_ref[1, s]], dpc_ref[...])

        @pl.when(own_half)
        def _():
            sibling_piece(land).wait()

        for g in range(2 * N_CHIPS):
            @pl.when(grp == g)
            def _(g=g):
                if g in (0, 1, 2, 6):
                    use = s if g < 3 else 3 * nb + n
                    slot = use % 2

                    @pl.when(use >= 2)
                    def _():
                        to_sibling(slot, 0).wait_send()

                    st_ref[slot] = g_ref[...]
                    to_sibling(slot, min(g, 3)).start()
                elif g in (3, 4, 5):
                    sb_ref[...] = (g_ref[...] + tmp_ref[...]).astype(BF16)
                    stage = pltpu.make_async_copy(sb_ref, wire_ref.at[g - 3, :, cols], wire_sem)
                    stage.start()
                    stage.wait()
                    to_chip(g - 3).start()
                else:
                    slot = n % 2
                    piece = g_ref[...] + tmp_ref[...]

                    @pl.when(n < 2)
                    def _():
                        to_sibling(slot, 0).wait_send()

                    @pl.when(n >= 2)
                    def _():
                        to_result(slot).wait()

                    st_ref[slot] = piece
                    to_result(slot).start()

        @pl.when(s >= n_w)
        def _():
            acc = ALPHA * dr_ref[...]
            for j in range(N_CHIPS):
                acc = acc + _dot_nt(dpr_ref[:, j * W_BLOCK:(j + 1) * W_BLOCK], w_ref[j])
            dx_ref[...] = acc

        @pl.when(s == n_w + n_x - 1)
        def _():
            for slot in range(2):
                to_result(slot).wait()
            for r in range(3):
                all_of_chip(r).wait_recv()
                all_of_chip(r).wait_send()
            for a in range(ns):
                to_sibling_whole(a).wait_recv()
                to_sibling_whole(a).wait_send()
                for r in range(3):
                    for core in range(2):
                        landed(a, r, core).wait_recv()
                        part(a, r, core).wait_send()

    any_spec = pl.BlockSpec(memory_space=pl.ANY)
    tile = lambda s, tab: (jnp.maximum(s - n_w, 0), 0)
    return pl.pallas_call(
        body, name="grad_in_and_x",
        grid_spec=pltpu.PrefetchScalarGridSpec(
            num_scalar_prefetch=1, grid=(n_w + n_x,),
            in_specs=[pl.BlockSpec((2, half, t), lambda s, tab: (0, 0, 0), pipeline_mode=pl.Buffered(1)),
                      pl.BlockSpec((t, tn), lambda s, tab: (0, tab[0, jnp.minimum(s, n_w - 1)])),
                      pl.BlockSpec((tm, D_IN), tile),
                      pl.BlockSpec((N_CHIPS, D, W_BLOCK), lambda s, tab: (0, 0, 0), pipeline_mode=pl.Buffered(1)),
                      pl.BlockSpec((tm, D), tile)] + [any_spec] * (2 * ns),
            out_specs=[pl.BlockSpec((tm, D), tile)] + [any_spec] * (4 + 2 * ns),
            scratch_shapes=[pltpu.VMEM((half, tn), F32), pltpu.VMEM((2, half, tn), F32), pltpu.VMEM((half, tn), BF16),
                            pltpu.VMEM((half, tn), F32),
                            pltpu.SemaphoreType.DMA((2,)), pltpu.SemaphoreType.DMA((N_CHIPS * nb,)),
                            pltpu.SemaphoreType.DMA((3,)), pltpu.SemaphoreType.DMA((3,)),
                            pltpu.SemaphoreType.DMA((2,)), pltpu.SemaphoreType.DMA, pltpu.SemaphoreType.DMA,
                            pltpu.SemaphoreType.DMA((6 * ns,)), pltpu.SemaphoreType.DMA((6 * ns,)),
                            pltpu.SemaphoreType.DMA((ns,)), pltpu.SemaphoreType.DMA((ns,))]),
        out_shape=[jax.ShapeDtypeStruct((t, D), F32), jax.ShapeDtypeStruct((half, W_BLOCK), F32),
                   jax.ShapeDtypeStruct((3, half, W_BLOCK), BF16), jax.ShapeDtypeStruct((N_CHIPS, half, W_BLOCK), F32),
                   jax.ShapeDtypeStruct((3, half, W_BLOCK), BF16)]
        + [jax.ShapeDtypeStruct((6,) + w.shape[2:], w.dtype) for w in wires]
        + [jax.ShapeDtypeStruct(g.shape[2:], F32) for g in grads],
        compiler_params=_params("arbitrary"),
    )(table, xt, dp, dp, w4, dr, *wires, *grads)


def kernel(x, w_in, b_in, conv_w, conv_b, gn_g, gn_b, ln_v_g, ln_v_b, w_spatial, b_spatial, w_pa, w_pb, w_o, b_o, ln_out_g, ln_out_b, loss_target, m_w_in, m_b_in, m_conv_w, m_conv_b, m_gn_g, m_gn_b, m_ln_v_g, m_ln_v_b, m_w_spatial, m_b_spatial, m_w_pa, m_w_pb, m_w_o, m_b_o, m_ln_out_g, m_ln_out_b, v_w_in, v_b_in, v_conv_w, v_conv_b, v_gn_g, v_gn_b, v_ln_v_g, v_ln_v_b, v_w_spatial, v_b_spatial, v_w_pa, v_w_pb, v_w_o, v_b_o, v_ln_out_g, v_ln_out_b):
    n_seq, seq, _ = x.shape
    t = n_seq * seq
    tiles_per_seq = seq // TOKEN_TILE
    x2 = x.reshape(t, D)
    tgt = loss_target.reshape(t, D)

    conv_shard = jnp.pad(conv_w, ((0, HALO - CONV_K), (0, 0)))
    p, win4, wpa4, wpb4, wo4, conv4 = _proj_gather(
        x2, b_in,
        _place_shards([w_in, w_pa, w_pb, w_o, conv_shard], [BF16, BF16, BF16, BF16, F32]))
    win4 = win4.reshape(N_CHIPS, D, W_BLOCK)
    wpa, wpb, wo = wpa4.reshape(D, D), wpb4.reshape(D, D), wo4.reshape(D, D)
    convw = conv4.reshape(N_CHIPS, HALO, D // N_CHIPS).transpose(1, 0, 2).reshape(HALO, D)

    vecs = jnp.stack([conv_b, gn_g, gn_b, ln_v_g, ln_v_b, b_o, ln_out_g, ln_out_b])
    causal = jnp.tril(jnp.ones((CHUNK, CHUNK), bool))
    ws = jnp.where(causal[None], w_spatial, 0.0)
    ws_bf, wst_bf = ws.astype(BF16), ws.transpose(0, 2, 1).astype(BF16)
    bsp = jnp.repeat(b_spatial.T, GROUP_W, axis=1)

    h1, ya, yb, h3, s, mixed, dr, drb, xt, acc_f = _forward_tiles(p, x2, tgt, wpa, wpb, wo, convw, vecs, ws_bf, bsp, tiles_per_seq)
    dh1, dp, dya, dyb, acc_b, dbin_b, dws, dbsp_acc = _backward_tiles(p, h1, ya, yb, drb, wpa, wpb, wo, vecs, ws_bf, wst_bf, bsp)
    dp, dcw8, dbin_a, *square = _conv_backward(dh1, p, dp, convw, [(h3, dya), (s, dyb), (mixed, drb)], tiles_per_seq)

    small, g_conv = _pack_small(acc_f, acc_b, dbin_a, dbin_b, dcw8, dws, dbsp_acc)
    small = small.reshape(2, N_CHIPS, SMALL_ROWS // 8, D)

    grads = square[:3] + [g_conv, small]
    wires = square[3:] + [g_conv, small]
    grad_x, q_in, chips_in, _, _, *landed = _grad_in_and_x(xt.reshape(2, D // 2, t), dp, win4, dr, wires, grads)
    grad_x = grad_x.reshape(x.shape)
    mine = [_add_chips(q_in, chips_in)]
    mine += _add_devices(grads, landed[:5], landed[5:])
    *full, small_parts = _share_results(mine[:5], mine[5])
    grad_w_in, grad_w_pa, grad_w_pb, grad_w_o = [f.reshape(w.shape) for f, w in zip(full[:4], (w_in, w_pa, w_pb, w_o))]
    grad_conv_w = full[4].reshape(HALO, D // N_CHIPS)[:CONV_K]

    big = {"w_in": (grad_w_in,) + tuple(_adamw(w_in, grad_w_in, m_w_in, v_w_in))}
    group = _adamw_group([(w_pa, grad_w_pa, m_w_pa, v_w_pa), (w_pb, grad_w_pb, m_w_pb, v_w_pb),
                          (w_o, grad_w_o, m_w_o, v_w_o)], (conv_w, grad_conv_w, m_conv_w, v_conv_w))
    for name, g, res in zip(["w_pa", "w_pb", "w_o", "conv_w"], [grad_w_pa, grad_w_pb, grad_w_o, grad_conv_w], group):
        big[name] = (g,) + tuple(res)
    vec_names = ["conv_b", "gn_g", "gn_b", "ln_v_g", "ln_v_b", "b_o", "ln_out_g", "ln_out_b"]
    vec_triples = [(conv_b, m_conv_b, v_conv_b), (gn_g, m_gn_g, v_gn_g), (gn_b, m_gn_b, v_gn_b),
                   (ln_v_g, m_ln_v_g, v_ln_v_g), (ln_v_b, m_ln_v_b, v_ln_v_b), (b_o, m_b_o, v_b_o),
                   (ln_out_g, m_ln_out_g, v_ln_out_g), (ln_out_b, m_ln_out_b, v_ln_out_b)]
    small_res, loss8 = _adamw_small(
        small_parts.reshape(8, SMALL_ROWS // 8, D), vec_triples, (b_in, m_b_in, v_b_in),
        (w_spatial, m_w_spatial, v_w_spatial), (b_spatial, m_b_spatial, v_b_spatial))
    per_name = dict(zip(vec_names + ["b_in", "w_spatial", "b_spatial"], small_res))

    order = ["w_in", "b_in", "conv_w", "conv_b", "gn_g", "gn_b", "ln_v_g", "ln_v_b", "w_spatial", "b_spatial",
             "w_pa", "w_pb", "w_o", "b_o", "ln_out_g", "ln_out_b"]
    outs = [loss8[0, 0], grad_x]
    for kind in range(4):
        outs += [big[n][kind] if n in big else per_name[n][kind] for n in order]
    return tuple(outs)
```

```python
import math

import jax
import jax.numpy as jnp
from jax import lax
from jax.experimental import pallas as pl
from jax.experimental.pallas import tpu as pltpu

D = 1024
N_GROUPS = 8
GROUP_W = D // N_GROUPS
CHUNK = 128
CONV_K = 31
HALO = 32
D_IN = 8 * D
N_CHIPS = 4
W_BLOCK = D_IN // N_CHIPS
ALPHA = 2.0 ** 0.25
LN_EPS = 1e-5
ADAM_LR, ADAM_B1, ADAM_B2, ADAM_EPS, ADAM_WD, ADAM_STEP = 0.001, 0.9, 0.999, 1e-08, 0.01, 10

TOKEN_TILE = 256
VMEM_LIMIT = 56 * 1024 * 1024
MESH = pl.DeviceIdType.MESH
F32, BF16 = jnp.float32, jnp.bfloat16


def _sigmoid(x):
    return jax.nn.sigmoid(x)


GELU_C = math.sqrt(2.0 / math.pi)
GELU_CA = GELU_C * 0.044715


def _gelu(x):
    t = jnp.tanh(x * (GELU_C + GELU_CA * (x * x)))
    return x * (0.5 + 0.5 * t)


def _gelu_and_grad(x):
    x2 = x * x
    t = jnp.tanh(x * (GELU_C + GELU_CA * x2))
    cdf = 0.5 + 0.5 * t
    return x * cdf, cdf + (0.5 * x) * (1.0 - t * t) * (GELU_C + (3.0 * GELU_CA) * x2)


def _norm_stats(v):
    mu = jnp.mean(v, axis=-1, keepdims=True)
    vc = v - mu
    var = jnp.mean(vc * vc, axis=-1, keepdims=True)
    rstd = lax.rsqrt(var + LN_EPS)
    return vc * rstd, rstd


def _norm_bwd(dxhat, xhat, rstd):
    m1 = jnp.mean(dxhat, axis=-1, keepdims=True)
    m2 = jnp.mean(dxhat * xhat, axis=-1, keepdims=True)
    return rstd * (dxhat - m1 - xhat * m2)


def _dot(a, b):
    return jnp.dot(a, b, preferred_element_type=F32)


def _dot_nt(a, b):
    return lax.dot_general(a, b, (((1,), (1,)), ((), ())), preferred_element_type=F32)


def _dot_tn(a, b):
    return lax.dot_general(a, b, (((0,), (0,)), ((), ())), preferred_element_type=F32)


def _colsum(v):
    return jnp.sum(v, axis=0, keepdims=True)


def _full(shape):
    return pl.BlockSpec(shape, lambda *_: (0,) * len(shape))


def _resident(shape):
    return pl.BlockSpec(shape, lambda *_: (0,) * len(shape), pipeline_mode=pl.Buffered(1))


def _params(*sem):
    return pltpu.CompilerParams(dimension_semantics=sem, vmem_limit_bytes=VMEM_LIMIT)


def _chip_index():
    return (2 * lax.axis_index("x") + lax.axis_index("y")).astype(jnp.int32).reshape(1)


def _core_index():
    return lax.axis_index("c").astype(jnp.int32).reshape(1)


def _place_shards(ws, dtypes):
    n = len(ws)

    def body(k_ref, *refs):
        for w_ref, o_ref, dtype in zip(refs[:n], refs[n:], dtypes):
            rows = w_ref.shape[0] // 2
            for h in range(2):
                o_ref[h] = w_ref[h * rows:(h + 1) * rows, :].astype(dtype)

    return pl.pallas_call(
        body, name="place_shards",
        grid_spec=pltpu.PrefetchScalarGridSpec(
            num_scalar_prefetch=1, grid=(1,),
            in_specs=[pl.BlockSpec(w.shape, lambda i, k: (0, 0)) for w in ws],
            out_specs=[pl.BlockSpec((None, 2, w.shape[0] // 2, w.shape[1]), lambda i, k: (k[0], 0, 0, 0)) for w in ws]),
        out_shape=[jax.ShapeDtypeStruct((N_CHIPS, 2, w.shape[0] // 2, w.shape[1]), dt) for w, dt in zip(ws, dtypes)],
        compiler_params=_params("arbitrary"),
    )(_chip_index(), *ws)


def _position():
    x, y, c = lax.axis_index("x"), lax.axis_index("y"), lax.axis_index("c")
    return x, y, c, 2 * x + y


def _other_chips(x, y):
    return [(1 - x, y), (x, 1 - y), (1 - x, 1 - y)]


def _any_specs(n):
    return [pl.BlockSpec(memory_space=pl.ANY)] * n


def _proj_gather(x, b_in, bufs):
    t = x.shape[0]
    tm = 1024
    steps = t // tm
    ahead = steps // 2
    half = D // 2
    chunk = W_BLOCK // 2
    n = len(bufs)
    xi, yi = lax.axis_index("x"), lax.axis_index("y")
    chips = [2 * xi + yi, 2 * (1 - xi) + yi, 2 * xi + (1 - yi), 2 * (1 - xi) + (1 - yi)]
    plan = [(0, 0), (0, 1), (1, 0), (2, 1), (1, 1), (2, 0), (3, 0), (3, 1)]
    order = jnp.stack([2 * chips[ch] + q for ch, q in plan]).astype(jnp.int32)

    def body(order_ref, x_ref, b_ref, *refs):
        p_ref, outs = refs[n], refs[n + 1:2 * n + 1]
        xb_ref, w_ref, lsem, send, recv, hop_send, hop_recv, fsend, frecv, qsend, qrecv = refs[2 * n + 1:]
        jj, i = pl.program_id(0), pl.program_id(1)
        x_, y_, c, k = _position()
        nbrs = [(1 - x_, y_), (x_, 1 - y_)]
        blocks = [2 * (1 - x_) + y_, 2 * x_ + (1 - y_), 2 * (1 - x_) + (1 - y_)]

        def quarter(a, block, q, h):
            if a == 0:
                return outs[0].at[block, h, :, pl.ds(q * chunk, chunk)]
            rows = outs[a].shape[2] // 2
            return outs[a].at[block, h, pl.ds(q * rows, rows)]

        def copy(ref, to, send_sem, recv_sem):
            return pltpu.make_async_remote_copy(src_ref=ref, dst_ref=ref, send_sem=send_sem, recv_sem=recv_sem,
                                                device_id=(to[0], to[1], c), device_id_type=MESH)

        def sent(a, nb, q):
            return copy(quarter(a, k, q, c), nbrs[nb], send.at[4 * a + 2 * nb + q], recv.at[4 * a + 2 * nb + q])

        def landed(a, nb, q):
            return copy(quarter(a, blocks[nb], q, c), nbrs[nb], send.at[4 * a + 2 * nb + q], recv.at[4 * a + 2 * nb + q])

        def hopped(a, nb):
            return copy(quarter(a, blocks[nb], nb, c), nbrs[1 - nb], hop_send.at[2 * a + nb], hop_recv.at[2 * a + 1 - nb])

        def from_diagonal(a, via):
            return copy(quarter(a, blocks[2], 1 - via, c), nbrs[via], hop_send.at[2 * a + via], hop_recv.at[2 * a + via])

        def passed(a, r, h):
            return pltpu.make_async_remote_copy(
                src_ref=outs[a].at[blocks[r], h], dst_ref=outs[a].at[blocks[r], h], send_sem=fsend.at[3 * a + r],
                recv_sem=frecv.at[3 * a + r], device_id=(x_, y_, 1 - c), device_id_type=MESH)

        def passed_quarter(r, q, h):
            ref = quarter(0, blocks[r], q, h)
            return pltpu.make_async_remote_copy(
                src_ref=ref, dst_ref=ref, send_sem=qsend.at[2 * r + q], recv_sem=qrecv.at[2 * r + q],
                device_id=(x_, y_, 1 - c), device_id_type=MESH)

        def load(block, q, slot):
            return [pltpu.make_async_copy(quarter(0, block, q, h), w_ref.at[slot, pl.ds(h * half, half)],
                                          lsem.at[2 * slot + h]) for h in range(2)]

        def pass_on(arrays):
            for a in arrays:
                for nb in range(2):
                    landed(a, nb, nb).wait_recv()
                    hopped(a, nb).start()

        @pl.when((jj == 0) & (i == 0))
        def _():
            for a in range(n):
                for nb, q in ((0, 0), (1, 1), (0, 1), (1, 0)):
                    sent(a, nb, q).start()
            for cp in load(k, 0, 0):
                cp.start()

        for nxt in range(1, len(plan)):
            @pl.when((jj == nxt - 1) & (i == ahead))
            def _(nxt=nxt):
                ch, q = plan[nxt]
                if ch in (1, 2):
                    landed(0, ch - 1, q).wait_recv()
                    if q == ch - 1:
                        hopped(0, ch - 1).start()
                elif ch == 3:
                    from_diagonal(0, 1 - q).wait_recv()
                if ch:
                    passed_quarter(ch - 1, q, c).start()
                    passed_quarter(ch - 1, q, 1 - c).wait_recv()
                for cp in load(k if ch == 0 else blocks[ch - 1], q, nxt % 2):
                    cp.start()
                if nxt == 5:
                    pass_on(range(1, n))
                    for a in range(1, n):
                        landed(a, 0, 1).wait_recv()
                        passed(a, 0, c).start()
                        landed(a, 1, 0).wait_recv()
                        passed(a, 1, c).start()

        slot = jj % 2

        @pl.when(i == 0)
        def _():
            for cp in load(k, 0, slot):
                cp.wait()

        rows = pl.ds(pl.multiple_of(i * tm, tm), tm)

        @pl.when(jj == 0)
        def _():
            xb_ref[rows, :] = x_ref[...].astype(BF16)

        p_ref[...] = _dot(xb_ref[rows, :], w_ref[slot]) + b_ref[...]

        @pl.when((jj == len(plan) - 1) & (i == steps - 1))
        def _():
            for a in range(1, n):
                from_diagonal(a, 0).wait_recv()
                from_diagonal(a, 1).wait_recv()
                passed(a, 2, c).start()
            for a in range(1, n):
                for r in range(3):
                    passed(a, r, 1 - c).wait_recv()
                    passed(a, r, c).wait_send()
            for r in range(3):
                for q in range(2):
                    passed_quarter(r, q, c).wait_send()
            for a in range(n):
                for nb in range(2):
                    for q in range(2):
                        sent(a, nb, q).wait_send()
                    hopped(a, nb).wait_send()

    any_spec = pl.BlockSpec(memory_space=pl.ANY)
    return pl.pallas_call(
        body, name="proj_gather",
        grid_spec=pltpu.PrefetchScalarGridSpec(
            num_scalar_prefetch=1, grid=(len(plan), steps),
            in_specs=[pl.BlockSpec((tm, D), lambda jj, i, o: (jnp.where(jj == 0, i, steps - 1), 0)),
                      pl.BlockSpec((None, 1, chunk), lambda jj, i, o: (o[jj], 0, 0))] + [any_spec] * n,
            out_specs=[pl.BlockSpec((tm, chunk), lambda jj, i, o: (i, o[jj]))] + [any_spec] * n,
            scratch_shapes=[pltpu.VMEM((t, D), BF16), pltpu.VMEM((2, D, chunk), BF16), pltpu.SemaphoreType.DMA((4,)),
                            pltpu.SemaphoreType.DMA((4 * n,)), pltpu.SemaphoreType.DMA((4 * n,)),
                            pltpu.SemaphoreType.DMA((2 * n,)), pltpu.SemaphoreType.DMA((2 * n,)),
                            pltpu.SemaphoreType.DMA((3 * n,)), pltpu.SemaphoreType.DMA((3 * n,)),
                            pltpu.SemaphoreType.DMA((6,)), pltpu.SemaphoreType.DMA((6,))]),
        out_shape=[jax.ShapeDtypeStruct((t, D_IN), F32)] + [jax.ShapeDtypeStruct(b.shape, b.dtype) for b in bufs],
        input_output_aliases={3 + a: 1 + a for a in range(n)},
        compiler_params=_params("arbitrary", "arbitrary"),
    )(order, x, b_in.reshape(D_IN // chunk, 1, chunk), *bufs)


def _share_results(bufs, small):
    n = len(bufs)

    def body(*refs):
        outs, small_out = refs[n + 1:2 * n + 1], refs[2 * n + 1]
        send, recv, ssend, srecv = refs[2 * n + 2:]
        x, y, c, k = _position()
        cps = []
        for a in range(n):
            cp = pltpu.make_async_remote_copy(
                src_ref=outs[a].at[c], dst_ref=outs[a].at[c], send_sem=send.at[a], recv_sem=recv.at[a],
                device_id=(x, y, 1 - c), device_id_type=MESH)
            cp.start()
            cps.append(cp)
        waits = []
        for p in range(1, 8):
            px, py, pc = x ^ (p >> 2), y ^ ((p >> 1) & 1), c ^ (p & 1)
            cp = pltpu.make_async_remote_copy(
                src_ref=small_out.at[k, c], dst_ref=small_out.at[k, c], send_sem=ssend.at[p - 1],
                recv_sem=srecv.at[p - 1], device_id=(px, py, pc), device_id_type=MESH)
            cp.start()
            cps.append(cp)
            waits.append(pltpu.make_async_remote_copy(
                src_ref=small_out.at[2 * px + py, pc], dst_ref=small_out.at[2 * px + py, pc], send_sem=ssend.at[p - 1],
                recv_sem=srecv.at[p - 1], device_id=(px, py, pc), device_id_type=MESH))
        for a in range(n):
            pltpu.make_async_remote_copy(
                src_ref=outs[a].at[1 - c], dst_ref=outs[a].at[1 - c], send_sem=send.at[a], recv_sem=recv.at[a],
                device_id=(x, y, 1 - c), device_id_type=MESH).wait_recv()
        for w in waits:
            w.wait_recv()
        for cp in cps:
            cp.wait_send()

    return pl.pallas_call(
        body, name="rs_share_results",
        in_specs=_any_specs(n + 1), out_specs=_any_specs(n + 1),
        out_shape=[jax.ShapeDtypeStruct(b.shape, b.dtype) for b in bufs + [small]],
        scratch_shapes=[pltpu.SemaphoreType.DMA((n,)), pltpu.SemaphoreType.DMA((n,)),
                        pltpu.SemaphoreType.DMA((7,)), pltpu.SemaphoreType.DMA((7,))],
        input_output_aliases={a: a for a in range(n + 1)},
    )(*bufs, small)


def _row_tile(r, c):
    t = max(8, min(r, (1 << 18) // c))
    while r % t:
        t //= 2
    return t


def _add_devices(grads, lands, sibs):
    n = len(grads)

    def body(kc_ref, *refs):
        for g_ref, l_ref, s_ref, f_ref in zip(refs[:n], refs[n:2 * n], refs[2 * n:3 * n], refs[3 * n:]):
            f = g_ref[...] + s_ref[...]
            for i in range(l_ref.shape[0]):
                f = f + l_ref[i].astype(F32)
            f_ref[...] = f

    shapes = [g.shape[2:] for g in grads]
    out_specs = [pl.BlockSpec((None,) + sh, lambda i, kc: (kc[1], 0, 0)) for sh in shapes[:-1]]
    out_specs.append(pl.BlockSpec((None, None) + shapes[-1], lambda i, kc: (kc[0], kc[1], 0, 0)))
    out_shape = [jax.ShapeDtypeStruct((2,) + sh, F32) for sh in shapes[:-1]]
    out_shape.append(jax.ShapeDtypeStruct((N_CHIPS, 2) + shapes[-1], F32))
    return pl.pallas_call(
        body, name="rs_add_devices",
        grid_spec=pltpu.PrefetchScalarGridSpec(
            num_scalar_prefetch=1, grid=(1,),
            in_specs=[pl.BlockSpec((None, None) + sh, lambda i, kc: (kc[1], kc[0], 0, 0)) for sh in shapes]
            + [pl.BlockSpec(l.shape, lambda i, kc: (0, 0, 0)) for l in lands]
            + [pl.BlockSpec(sh, lambda i, kc: (0, 0)) for sh in shapes],
            out_specs=out_specs),
        out_shape=out_shape,
        compiler_params=_params("arbitrary"),
    )(jnp.concatenate([_chip_index(), _core_index()]), *grads, *lands, *sibs)


def _add_chips(q, b2):
    r, c = q.shape
    t = _row_tile(r, c)

    def body(c_ref, q_ref, b_ref, f_ref):
        f_ref[...] = ((q_ref[...] + b_ref[0].astype(F32)) + b_ref[1].astype(F32)) + b_ref[2].astype(F32)

    return pl.pallas_call(
        body, name="rs_add_chips",
        grid_spec=pltpu.PrefetchScalarGridSpec(
            num_scalar_prefetch=1, grid=(r // t,),
            in_specs=[pl.BlockSpec((t, c), lambda i, cr: (i, 0)), pl.BlockSpec((3, t, c), lambda i, cr: (0, i, 0))],
            out_specs=pl.BlockSpec((None, t, c), lambda i, cr: (cr[0], i, 0))),
        out_shape=jax.ShapeDtypeStruct((2, r, c), F32),
        compiler_params=_params("parallel"),
    )(_core_index(), q, b2)


def _adamw_math(w, g, m, v):
    m = ADAM_B1 * m + (1.0 - ADAM_B1) * g
    v = ADAM_B2 * v + (1.0 - ADAM_B2) * (g * g)
    m_hat = m / (1.0 - ADAM_B1 ** ADAM_STEP)
    v_hat = v / (1.0 - ADAM_B2 ** ADAM_STEP)
    delta = -ADAM_LR * (m_hat / (jnp.sqrt(v_hat) + ADAM_EPS) + ADAM_WD * w)
    return delta, m, v


def _adamw_group(quads, conv):
    n = len(quads)
    r, c = quads[0][0].shape

    def body(*refs):
        ins, outs = refs[:4 * (n + 1)], refs[4 * (n + 1):]
        for i in range(n + 1):
            w_ref, g_ref, m_ref, v_ref = ins[4 * i:4 * i + 4]
            res = _adamw_math(w_ref[...], g_ref[...], m_ref[...], v_ref[...])
            for o_ref, val in zip(outs[3 * i:3 * i + 3], res):
                o_ref[...] = val

    half = pl.BlockSpec((r // 2, c), lambda i: (i, 0))
    whole = pl.BlockSpec(conv[0].shape, lambda i: (0, 0))
    res = pl.pallas_call(
        body, name="adamw_group", grid=(2,),
        in_specs=[half] * (4 * n) + [whole] * 4, out_specs=[half] * (3 * n) + [whole] * 3,
        out_shape=[jax.ShapeDtypeStruct((r, c), F32)] * (3 * n) + [jax.ShapeDtypeStruct(conv[0].shape, F32)] * 3,
        compiler_params=_params("arbitrary"),
    )(*[a for quad in quads for a in quad], *conv)
    return [res[3 * i:3 * i + 3] for i in range(n + 1)]


def _adamw(w, g, m, v):
    r, c = w.shape
    t = _row_tile(r, c) if r % 8 == 0 else r

    def body(w_ref, g_ref, m_ref, v_ref, d_ref, nm_ref, nv_ref):
        d_ref[...], nm_ref[...], nv_ref[...] = _adamw_math(w_ref[...], g_ref[...], m_ref[...], v_ref[...])

    spec = pl.BlockSpec((t, c), lambda i: (i, 0))
    return pl.pallas_call(
        body, name="adamw", grid=(r // t,), in_specs=[spec] * 4, out_specs=[spec] * 3,
        out_shape=[jax.ShapeDtypeStruct((r, c), F32)] * 3, compiler_params=_params("parallel"),
    )(w, g, m, v)


ROW_B_IN = 0
ROW_VECS = 8
ROW_LOSS = 16
ROW_B_SPATIAL = 24
ROW_W_SPATIAL = 32
SMALL_ROWS = 192
N_VECS = 8


def _pack_small(acc_f, acc_b, dbin_a, dbin_b, dcw8, dws, dbsp):
    cols = D // N_CHIPS

    def body(af_ref, ab_ref, da_ref, db_ref, cw_ref, ws_ref, bs_ref, o_ref, gc_ref):
        o_ref[...] = jnp.zeros_like(o_ref)
        for j in range(D_IN // D):
            src = da_ref if j < 2 else db_ref
            o_ref[ROW_B_IN + j:ROW_B_IN + j + 1, :] = src[0:1, j * D:(j + 1) * D]
        dcw = jnp.sum(cw_ref[...].reshape(HALO, SUBLANES, D), axis=1)
        o_ref[ROW_VECS:ROW_VECS + 1, :] = dcw[CONV_K:CONV_K + 1]
        o_ref[ROW_VECS + 1:ROW_VECS + 5, :] = ab_ref[0:4, :]
        o_ref[ROW_VECS + 5:ROW_VECS + 6, :] = af_ref[2:3, :]
        o_ref[ROW_VECS + 6:ROW_VECS + 8, :] = af_ref[0:2, :]
        o_ref[ROW_LOSS:ROW_LOSS + 1, :] = af_ref[3:4, :]
        head = lax.broadcasted_iota(jnp.int32, (N_GROUPS, D), 0)
        lane = lax.broadcasted_iota(jnp.int32, (N_GROUPS, D), 1)
        indicator = jnp.where(lane // GROUP_W == head, 1.0, 0.0)
        o_ref[ROW_B_SPATIAL:ROW_B_SPATIAL + N_GROUPS, 0:CHUNK] = lax.dot_general(
            indicator, bs_ref[...], (((1,), (1,)), ((), ())), precision=lax.Precision.HIGHEST, preferred_element_type=F32)
        t_idx = lax.broadcasted_iota(jnp.int32, (CHUNK, D), 0)
        s_idx = lax.broadcasted_iota(jnp.int32, (CHUNK, D), 1) % CHUNK
        o_ref[ROW_W_SPATIAL:ROW_W_SPATIAL + CHUNK, :] = jnp.where(s_idx <= t_idx, ws_ref[...], 0.0)
        for h in range(2):
            for j in range(N_CHIPS):
                gc_ref[h, j] = dcw[h * (HALO // 2):(h + 1) * (HALO // 2), j * cols:(j + 1) * cols]

    ins = [acc_f, acc_b, dbin_a, dbin_b, dcw8, dws, dbsp]
    return pl.pallas_call(
        body, name="pack_small",
        in_specs=[_full(a.shape) for a in ins],
        out_specs=[_full((SMALL_ROWS, D)), _full((2, N_CHIPS, HALO // 2, cols))],
        out_shape=[jax.ShapeDtypeStruct((SMALL_ROWS, D), F32), jax.ShapeDtypeStruct((2, N_CHIPS, HALO // 2, cols), F32)],
        compiler_params=_params(),
    )(*ins)


def _adamw_small(parts, vecs, b_in, w_spatial, b_spatial):
    triples = list(vecs) + [b_in, w_spatial, b_spatial]
    n_in = 3 * len(triples)

    def body(p_ref, *refs):
        ins = [refs[3 * i:3 * i + 3] for i in range(len(triples))]
        outs = [refs[n_in + 4 * i:n_in + 4 * i + 4] for i in range(len(triples))]
        loss_ref, g_ref = refs[n_in + 4 * len(triples):]
        rows = SMALL_ROWS // 8
        for k in range(N_CHIPS):
            for core in range(2):
                g_ref[(core * N_CHIPS + k) * rows:(core * N_CHIPS + k + 1) * rows, :] = p_ref[2 * k + core]

        def step(g, wmv, out, get, put):
            d, nm, nv = _adamw_math(get(wmv[0]), g, get(wmv[1]), get(wmv[2]))
            for o, val in zip(out, (g, d, nm, nv)):
                put(o, val)

        for i in range(N_VECS):
            step(g_ref[ROW_VECS + i:ROW_VECS + i + 1, :], ins[i], outs[i],
                 lambda r: r[...].reshape(1, D), lambda o, val: o.__setitem__(Ellipsis, val.reshape(D)))
        for j in range(D_IN // D):
            piece = pl.ds(j * D, D)
            step(g_ref[ROW_B_IN + j:ROW_B_IN + j + 1, :], ins[N_VECS], outs[N_VECS],
                 lambda r: r[piece].reshape(1, D), lambda o, val: o.__setitem__(piece, val.reshape(D)))
        for h in range(N_GROUPS):
            step(g_ref[ROW_W_SPATIAL:ROW_W_SPATIAL + CHUNK, h * CHUNK:(h + 1) * CHUNK], ins[N_VECS + 1], outs[N_VECS + 1],
                 lambda r: r[h], lambda o, val: o.__setitem__(h, val))
        step(g_ref[ROW_B_SPATIAL:ROW_B_SPATIAL + N_GROUPS, 0:CHUNK], ins[N_VECS + 2], outs[N_VECS + 2],
             lambda r: r[...], lambda o, val: o.__setitem__(Ellipsis, val))
        lanes = g_ref[ROW_LOSS:ROW_LOSS + 1, :]
        loss_ref[...] = jnp.broadcast_to(jnp.sum(lanes, axis=1, keepdims=True), (8, 128))

    flat = [a for tr in triples for a in tr]
    out_shape = [jax.ShapeDtypeStruct(tr[0].shape, F32) for tr in triples for _ in range(4)]
    out_shape.append(jax.ShapeDtypeStruct((8, 128), F32))
    res = pl.pallas_call(
        body, name="adamw_small",
        in_specs=[_full(parts.shape)] + [_full(a.shape) for a in flat],
        out_specs=[_full(o.shape) for o in out_shape],
        out_shape=out_shape,
        scratch_shapes=[pltpu.VMEM((SMALL_ROWS, D), F32)],
        compiler_params=_params(),
    )(parts, *flat)
    return [res[4 * i:4 * i + 4] for i in range(len(triples))], res[-1]


SUBLANES = 8
SHIFT_ROWS = HALO - SUBLANES


def _shifted_copies(src_ref, sh_ref, cs, tm):
    for p in range(1, SUBLANES):
        sh_ref[p - 1] = src_ref[pl.ds(p, tm + SHIFT_ROWS), cs]


def _tap(src_ref, sh_ref, cs, offset, start, rows):
    p, q = offset % SUBLANES, offset // SUBLANES
    if p == 0:
        return src_ref[pl.ds(start + SUBLANES * q, rows), cs]
    return sh_ref[p - 1, pl.ds(start + SUBLANES * q, rows), :]


def _conv_taps(src_ref, sh_ref, w_ref, first_offset, step, bias, dst_ref, tm):
    rows = 64
    for g in range(N_GROUPS):
        cs = slice(g * GROUP_W, (g + 1) * GROUP_W)
        _shifted_copies(src_ref, sh_ref, cs, tm)
        for rb in range(tm // rows):
            acc = jnp.zeros((rows, GROUP_W), F32) + (bias[:, cs] if bias is not None else 0.0)
            for k in range(CONV_K):
                acc = acc + w_ref[k:k + 1, cs] * _tap(src_ref, sh_ref, cs, first_offset + step * k, rb * rows, rows)
            dst_ref[rb * rows:(rb + 1) * rows, cs] = acc


def _conv_weight_grad(d_ref, src_ref, sh_ref, first_offset, acc_ref, tm):
    rows = 64
    for g in range(N_GROUPS):
        cs = slice(g * GROUP_W, (g + 1) * GROUP_W)
        _shifted_copies(src_ref, sh_ref, cs, tm)
        for rb in range(tm // rows):
            d = d_ref[rb * rows:(rb + 1) * rows, cs]
            for k in range(CONV_K):
                prod = d * _tap(src_ref, sh_ref, cs, first_offset + k, rb * rows, rows)
                acc_ref[SUBLANES * k:SUBLANES * (k + 1), cs] += jnp.sum(
                    prod.reshape(rows // SUBLANES, SUBLANES, GROUP_W), axis=0)


def _spatial_mix(w_ref, v_bf, tm):
    rows = []
    for q in range(tm // CHUNK):
        cols = [_dot(w_ref[h], v_bf[q * CHUNK:(q + 1) * CHUNK, h * GROUP_W:(h + 1) * GROUP_W])
                for h in range(N_GROUPS)]
        rows.append(jnp.concatenate(cols, axis=1))
    return jnp.concatenate(rows, axis=0)


def _group_norm_fwd(h1, gn_g, gn_b):
    xhat, rstd = [], []
    for g in range(N_GROUPS):
        xh, rs = _norm_stats(h1[:, g * GROUP_W:(g + 1) * GROUP_W])
        xhat.append(xh)
        rstd.append(rs)
    xhat = jnp.concatenate(xhat, axis=1)
    return xhat * gn_g + gn_b, xhat, rstd


def _forward_tiles(p, x, tgt, wpa, wpb, wo, convw, vecs, ws, bsp, tiles_per_seq):
    t = x.shape[0]
    tm = TOKEN_TILE
    hb = tm // HALO

    def body(p_ref, ph_ref, x_ref, t_ref, wpa_ref, wpb_ref, wo_ref, cw_ref, vec_ref, ws_ref, bsp_ref,
             h1_ref, ya_ref, yb_ref, h3_ref, s_ref, mx_ref, dr_ref, drb_ref, xt_ref, acc_ref, he_ref, sh_ref):
        i = pl.program_id(0)
        xt_ref[...] = x_ref[...].T.astype(BF16)
        conv_b, gn_g, gn_b, lnv_g, lnv_b, b_o, lno_g, lno_b = [vec_ref[j:j + 1, :] for j in range(8)]

        keep = jnp.where(i % tiles_per_seq == 0, 0.0, 1.0)
        he_ref[0:HALO, :] = ph_ref[:, 0:D] * _sigmoid(ph_ref[:, D:2 * D]) * keep
        he_ref[HALO:, :] = p_ref[:, 0:D] * _sigmoid(p_ref[:, D:2 * D])
        _conv_taps(he_ref, sh_ref, cw_ref, HALO - (CONV_K - 1), 1, conv_b, h1_ref, tm)
        h2, _, _ = _group_norm_fwd(h1_ref[...], gn_g, gn_b)
        a_gate = p_ref[:, 2 * D:3 * D]
        h3 = ((h2 * _sigmoid(h2)) * (a_gate * _sigmoid(a_gate))).astype(BF16)
        h3_ref[...] = h3
        ya = _dot(h3, wpa_ref[...])
        ya_ref[...] = ya

        u = _gelu(p_ref[:, 3 * D:4 * D])
        vhat, _ = _norm_stats(_gelu(p_ref[:, 4 * D:5 * D]))
        v1 = (vhat * lnv_g + lnv_b).astype(BF16)
        b_gate = p_ref[:, 5 * D:6 * D]
        vmix = _spatial_mix(ws_ref, v1, tm) + jnp.concatenate([bsp_ref[...]] * (tm // CHUNK), axis=0)
        s = (u * vmix * (b_gate * _sigmoid(b_gate))).astype(BF16)
        s_ref[...] = s
        yb = _dot(s, wpb_ref[...])
        yb_ref[...] = yb

        mixed = (_sigmoid(p_ref[:, 6 * D:7 * D]) * ya + _sigmoid(p_ref[:, 7 * D:8 * D]) * yb).astype(BF16)
        mx_ref[...] = mixed
        r = ALPHA * x_ref[...] + (_dot(mixed, wo_ref[...]) + b_o)
        xhat, rstd = _norm_stats(r)
        err = (xhat * lno_g + lno_b) - t_ref[...]
        dout = err * (1.0 / D)
        dr = _norm_bwd(dout * lno_g, xhat, rstd)
        dr_ref[...] = dr
        drb_ref[...] = dr.astype(BF16)

        @pl.when(i == 0)
        def _():
            acc_ref[...] = jnp.zeros_like(acc_ref)

        acc_ref[0:1, :] += _colsum(dout * xhat)
        acc_ref[1:2, :] += _colsum(dout)
        acc_ref[2:3, :] += _colsum(dr)
        acc_ref[3:4, :] += _colsum(err * err) * (0.5 / D)

    tile = lambda w: pl.BlockSpec((tm, w), lambda i: (i, 0))
    f32_out = jax.ShapeDtypeStruct((t, D), F32)
    bf_out = jax.ShapeDtypeStruct((t, D), BF16)
    return pl.pallas_call(
        body, name="forward_tiles", grid=(t // tm,),
        in_specs=[tile(D_IN),
                  pl.BlockSpec((HALO, 2 * D), lambda i: (jnp.maximum(i * hb - 1, 0), 0)),
                  tile(D), tile(D), _resident((D, D)), _resident((D, D)), _resident((D, D)), _full((HALO, D)), _full((8, D)),
                  _full((N_GROUPS, CHUNK, CHUNK)), _full((CHUNK, D))],
        out_specs=[tile(D)] * 8 + [pl.BlockSpec((D, tm), lambda i: (0, i)), _full((8, D))],
        out_shape=[f32_out, f32_out, f32_out, bf_out, bf_out, bf_out, f32_out, bf_out,
                   jax.ShapeDtypeStruct((D, t), BF16), jax.ShapeDtypeStruct((8, D), F32)],
        scratch_shapes=[pltpu.VMEM((tm + HALO, D), F32), pltpu.VMEM((SUBLANES - 1, tm + SHIFT_ROWS, GROUP_W), F32)],
        compiler_params=_params("arbitrary"),
    )(p, p, x, tgt, wpa, wpb, wo, convw, vecs, ws, bsp)


def _backward_tiles(p, h1, ya, yb, drb, wpa, wpb, wo, vecs, ws, wst, bsp):
    t = h1.shape[0]
    tm = TOKEN_TILE

    def body(p_ref, h1_ref, ya_ref, yb_ref, drb_ref, wpa_ref, wpb_ref, wo_ref, vec_ref, ws_ref, wst_ref, bsp_ref,
             dh1_ref, dp_ref, dya_ref, dyb_ref, acc_ref, dbin_ref, dws_ref, dbsp_ref):
        i = pl.program_id(0)
        _, gn_g, gn_b, lnv_g, lnv_b = [vec_ref[j:j + 1, :] for j in range(5)]

        @pl.when(i == 0)
        def _():
            acc_ref[...] = jnp.zeros_like(acc_ref)
            dbin_ref[...] = jnp.zeros_like(dbin_ref)
            dws_ref[...] = jnp.zeros_like(dws_ref)
            dbsp_ref[...] = jnp.zeros_like(dbsp_ref)

        def emit(block, val):
            dbin_ref[0:1, block * D:(block + 1) * D] += _colsum(val)
            dp_ref[:, block * D:(block + 1) * D] = val.astype(BF16)

        dp_ref[:, 0:2 * D] = jnp.zeros((tm, 2 * D), BF16)
        dmixed = _dot_nt(drb_ref[...], wo_ref[...])
        ga = _sigmoid(p_ref[:, 6 * D:7 * D])
        gb = _sigmoid(p_ref[:, 7 * D:8 * D])
        dya = (dmixed * ga).astype(BF16)
        dyb = (dmixed * gb).astype(BF16)
        dya_ref[...] = dya
        dyb_ref[...] = dyb
        emit(6, dmixed * ya_ref[...] * (ga * (1.0 - ga)))
        emit(7, dmixed * yb_ref[...] * (gb * (1.0 - gb)))

        dh3 = _dot_nt(dya, wpa_ref[...])
        h2, xhat, rstd = _group_norm_fwd(h1_ref[...], gn_g, gn_b)
        sg = _sigmoid(h2)
        a_gate = p_ref[:, 2 * D:3 * D]
        sa = _sigmoid(a_gate)
        silu_h2, silu_a = h2 * sg, a_gate * sa
        dh2 = dh3 * silu_a * (sg + silu_h2 * (1.0 - sg))
        emit(2, dh3 * silu_h2 * (sa + silu_a * (1.0 - sa)))
        acc_ref[0:1, :] += _colsum(dh2 * xhat)
        acc_ref[1:2, :] += _colsum(dh2)
        dxhat = dh2 * gn_g
        for g in range(N_GROUPS):
            cs = slice(g * GROUP_W, (g + 1) * GROUP_W)
            dh1_ref[:, cs] = _norm_bwd(dxhat[:, cs], xhat[:, cs], rstd[g])

        ds = _dot_nt(dyb, wpb_ref[...])
        u_pre = p_ref[:, 3 * D:4 * D]
        u, du_dpre = _gelu_and_grad(u_pre)
        v0, dv_dpre = _gelu_and_grad(p_ref[:, 4 * D:5 * D])
        vhat, vrstd = _norm_stats(v0)
        v1 = (vhat * lnv_g + lnv_b).astype(BF16)
        vmix = _spatial_mix(ws_ref, v1, tm) + jnp.concatenate([bsp_ref[...]] * (tm // CHUNK), axis=0)
        b_gate = p_ref[:, 5 * D:6 * D]
        sb = _sigmoid(b_gate)
        silu_b = b_gate * sb
        emit(3, ds * vmix * silu_b * du_dpre)
        emit(5, ds * u * vmix * (sb + silu_b * (1.0 - sb)))
        dvmix = ds * u * silu_b
        dvmix_bf = dvmix.astype(BF16)
        for q in range(tm // CHUNK):
            dbsp_ref[...] += dvmix[q * CHUNK:(q + 1) * CHUNK, :]
            for h in range(N_GROUPS):
                blk = (slice(q * CHUNK, (q + 1) * CHUNK), slice(h * GROUP_W, (h + 1) * GROUP_W))
                dws_ref[:, h * GROUP_W:(h + 1) * GROUP_W] += _dot_nt(dvmix_bf[blk], v1[blk])
        dv1 = _spatial_mix(wst_ref, dvmix_bf, tm)
        acc_ref[2:3, :] += _colsum(dv1 * vhat)
        acc_ref[3:4, :] += _colsum(dv1)
        emit(4, _norm_bwd(dv1 * lnv_g, vhat, vrstd) * dv_dpre)

    tile = lambda w: pl.BlockSpec((tm, w), lambda i: (i, 0))
    return pl.pallas_call(
        body, name="backward_tiles", grid=(t // tm,),
        in_specs=[tile(D_IN), tile(D), tile(D), tile(D), tile(D), _resident((D, D)), _resident((D, D)), _resident((D, D)),
                  _full((8, D)), _full((N_GROUPS, CHUNK, CHUNK)), _full((N_GROUPS, CHUNK, CHUNK)), _full((CHUNK, D))],
        out_specs=[tile(D), tile(D_IN), tile(D), tile(D), _full((8, D)), _full((8, D_IN)),
                   _full((CHUNK, D)), _full((CHUNK, D))],
        out_shape=[jax.ShapeDtypeStruct((t, D), F32), jax.ShapeDtypeStruct((t, D_IN), BF16),
                   jax.ShapeDtypeStruct((t, D), BF16), jax.ShapeDtypeStruct((t, D), BF16),
                   jax.ShapeDtypeStruct((8, D), F32), jax.ShapeDtypeStruct((8, D_IN), F32),
                   jax.ShapeDtypeStruct((CHUNK, D), F32), jax.ShapeDtypeStruct((CHUNK, D), F32)],
        compiler_params=_params("arbitrary"),
    )(p, h1, ya, yb, drb, wpa, wpb, wo, vecs, ws, wst, bsp)


def _conv_backward(dh1, p, dp, convw, pairs, tiles_per_seq):
    t = dh1.shape[0]
    tm = TOKEN_TILE
    hb = tm // HALO
    last = t // HALO - 1
    n_sq = len(pairs)
    span = 2
    rows = D // 8

    def body(dh1_ref, dnext_ref, p_ref, ph_ref, cw_ref, dp_in_ref, *refs):
        del dp_in_ref
        sq_in = refs[:2 * n_sq]
        dp_ref, dcw_ref, dbin_ref = refs[2 * n_sq:2 * n_sq + 3]
        sq_out = refs[2 * n_sq + 3:3 * n_sq + 3]
        sq_wire = refs[3 * n_sq + 3:4 * n_sq + 3]
        de_ref, he_ref, dh0_ref, sh_ref, acc_ref, wire_ref, sq_sem, wire_sem = refs[4 * n_sq + 3:]
        i = pl.program_id(0)

        @pl.when(i == 0)
        def _():
            dcw_ref[...] = jnp.zeros_like(dcw_ref)
            dbin_ref[...] = jnp.zeros_like(dbin_ref)
            acc_ref[...] = jnp.zeros_like(acc_ref)

        @pl.when(i % span == span - 1)
        def _():
            for a in range(n_sq):
                acc_ref[a] += _dot_tn(sq_in[2 * a][...], sq_in[2 * a + 1][...])

        keep_next = jnp.where(i % tiles_per_seq == tiles_per_seq - 1, 0.0, 1.0)
        de_ref[0:tm, :] = dh1_ref[...]
        de_ref[tm:, :] = dnext_ref[...] * keep_next
        _conv_taps(de_ref, sh_ref, cw_ref, CONV_K - 1, -1, None, dh0_ref, tm)

        keep_prev = jnp.where(i % tiles_per_seq == 0, 0.0, 1.0)
        sg = _sigmoid(p_ref[:, D:2 * D])
        val = p_ref[:, 0:D]
        he_ref[0:HALO, :] = ph_ref[:, 0:D] * _sigmoid(ph_ref[:, D:2 * D]) * keep_prev
        he_ref[HALO:, :] = val * sg
        _conv_weight_grad(dh1_ref, he_ref, sh_ref, HALO - (CONV_K - 1), dcw_ref, tm)
        dcw_ref[SUBLANES * CONV_K:, :] += jnp.sum(dh1_ref[...].reshape(tm // SUBLANES, SUBLANES, D), axis=0)

        dh0 = dh0_ref[...]
        dval = dh0 * sg
        dglu = dh0 * val * (sg * (1.0 - sg))
        dbin_ref[0:1, 0:D] += _colsum(dval)
        dbin_ref[0:1, D:2 * D] += _colsum(dglu)
        dp_ref[:, 0:D] = dval.astype(BF16)
        dp_ref[:, D:2 * D] = dglu.astype(BF16)

        @pl.when(i == t // tm - 1)
        def _():
            cps = [pltpu.make_async_copy(acc_ref.at[a, pl.ds((2 * j + h) * rows, rows)], sq_out[a].at[h, j],
                                         sq_sem.at[(a * N_CHIPS + j) * 2 + h])
                   for a in range(n_sq) for j in range(N_CHIPS) for h in range(2)]
            for cp in cps:
                cp.start()
            for a in range(n_sq):
                wire_ref[...] = acc_ref[a].astype(BF16)
                narrow = [pltpu.make_async_copy(wire_ref.at[pl.ds((2 * j + h) * rows, rows)], sq_wire[a].at[h, j],
                                                wire_sem.at[2 * j + h]) for j in range(N_CHIPS) for h in range(2)]
                for cp in narrow:
                    cp.start()
                for cp in narrow:
                    cp.wait()
            for cp in cps:
                cp.wait()

    any_spec = pl.BlockSpec(memory_space=pl.ANY)
    wide = pl.BlockSpec((span * tm, D), lambda i: (i // span, 0))
    return pl.pallas_call(
        body, name="conv_backward", grid=(t // tm,),
        in_specs=[pl.BlockSpec((tm, D), lambda i: (i, 0)),
                  pl.BlockSpec((HALO, D), lambda i: (jnp.minimum((i + 1) * hb, last), 0)),
                  pl.BlockSpec((tm, 2 * D), lambda i: (i, 0)),
                  pl.BlockSpec((HALO, 2 * D), lambda i: (jnp.maximum(i * hb - 1, 0), 0)),
                  _full((HALO, D)), any_spec] + [wide] * (2 * n_sq),
        out_specs=[pl.BlockSpec((tm, 2 * D), lambda i: (i, 0)), _full((SUBLANES * HALO, D)), _full((8, 2 * D))]
        + [any_spec] * (2 * n_sq),
        out_shape=[jax.ShapeDtypeStruct(dp.shape, BF16), jax.ShapeDtypeStruct((SUBLANES * HALO, D), F32),
                   jax.ShapeDtypeStruct((8, 2 * D), F32)]
        + [jax.ShapeDtypeStruct((2, N_CHIPS, rows, D), F32)] * n_sq
        + [jax.ShapeDtypeStruct((2, N_CHIPS, rows, D), BF16)] * n_sq,
        scratch_shapes=[pltpu.VMEM((tm + HALO, D), F32), pltpu.VMEM((tm + HALO, D), F32), pltpu.VMEM((tm, D), F32),
                        pltpu.VMEM((SUBLANES - 1, tm + SHIFT_ROWS, GROUP_W), F32), pltpu.VMEM((n_sq, D, D), F32),
                        pltpu.VMEM((D, D), BF16), pltpu.SemaphoreType.DMA((n_sq * N_CHIPS * 2,)),
                        pltpu.SemaphoreType.DMA((N_CHIPS * 2,))],
        input_output_aliases={5: 0},
        compiler_params=_params("arbitrary"),
    )(dh1, dh1, p, p, convw, dp, *[a for pair in pairs for a in pair])


def _grad_in_and_x(xt, dp, w4, dr, wires, grads):
    t = dr.shape[0]
    tm = TOKEN_TILE
    half, tn = D // 2, 512
    nb = W_BLOCK // tn
    n_w, n_x = 2 * N_CHIPS * nb, t // tm
    ns = len(grads)
    xi, yi, ci = lax.axis_index("x"), lax.axis_index("y"), lax.axis_index("c")
    others = [2 * (1 - xi) + yi, 2 * xi + (1 - yi), 2 * (1 - xi) + (1 - yi)]
    blocks = others + others + [2 * xi + yi] * 2
    halves = [1 - ci] * 3 + [ci] * 3 + [1 - ci, ci]
    table = jnp.stack([jnp.stack([b * nb + n for b in blocks for n in range(nb)]),
                       jnp.stack([h for h in halves for _ in range(nb)])]).astype(jnp.int32)

    def body(tab_ref, xt_ref, dpc_ref, dpr_ref, w_ref, dr_ref, *refs):
        parts, fulls = refs[:ns], refs[ns:2 * ns]
        dx_ref, qk_ref, b2_ref, b1_ref, wire_ref = refs[2 * ns:2 * ns + 5]
        lands, sibs = refs[2 * ns + 5:3 * ns + 5], refs[3 * ns + 5:4 * ns + 5]
        (g_ref, st_ref, sb_ref, tmp_ref, d2d_send, d2d_recv, ici_send, ici_recv, own_sem, tmp_sem, wire_sem,
         p_send, p_recv, s_send, s_recv) = refs[4 * ns + 5:]
        s = pl.program_id(0)
        x_, y_, c, k = _position()
        chips = _other_chips(x_, y_)
        n = s % nb
        grp = s // nb
        cols = pl.ds(pl.multiple_of(n * tn, tn), tn)

        def part(a, r, core):
            cx, cy = chips[r]
            return pltpu.make_async_remote_copy(
                src_ref=parts[a].at[core, 2 * cx + cy], dst_ref=lands[a].at[2 * r + c],
                send_sem=p_send.at[6 * a + 2 * r + core], recv_sem=p_recv.at[6 * a + 2 * r + c],
                device_id=(cx, cy, core), device_id_type=MESH)

        def landed(a, r, core):
            cx, cy = chips[r]
            return pltpu.make_async_remote_copy(
                src_ref=lands[a].at[2 * r + core], dst_ref=lands[a].at[2 * r + core],
                send_sem=p_send.at[6 * a + 2 * r + core], recv_sem=p_recv.at[6 * a + 2 * r + core],
                device_id=(cx, cy, core), device_id_type=MESH)

        def to_sibling_whole(a):
            return pltpu.make_async_remote_copy(
                src_ref=fulls[a].at[1 - c, k], dst_ref=sibs[a], send_sem=s_send.at[a], recv_sem=s_recv.at[a],
                device_id=(x_, y_, 1 - c), device_id_type=MESH)

        def to_sibling(slot, land):
            return pltpu.make_async_remote_copy(
                src_ref=st_ref.at[slot], dst_ref=b1_ref.at[land, :, cols], send_sem=d2d_send.at[slot],
                recv_sem=d2d_recv.at[land * nb + n], device_id=(x_, y_, 1 - c), device_id_type=MESH)

        def to_chip(r):
            cx, cy = chips[r]
            return pltpu.make_async_remote_copy(
                src_ref=wire_ref.at[r, :, cols], dst_ref=b2_ref.at[r, :, cols], send_sem=ici_send.at[r],
                recv_sem=ici_recv.at[r], device_id=(cx, cy, c), device_id_type=MESH)

        def all_of_chip(r):
            cx, cy = chips[r]
            return pltpu.make_async_remote_copy(
                src_ref=wire_ref.at[r], dst_ref=b2_ref.at[r], send_sem=ici_send.at[r],
                recv_sem=ici_recv.at[r], device_id=(cx, cy, c), device_id_type=MESH)

        def to_result(slot):
            return pltpu.make_async_copy(st_ref.at[slot], qk_ref.at[:, cols], own_sem.at[slot])

        def sibling_piece(land):
            return pltpu.make_async_copy(b1_ref.at[land, :, cols], tmp_ref, tmp_sem)

        @pl.when(s == 0)
        def _():
            for a in range(ns):
                to_sibling_whole(a).start()
                for r in range(3):
                    for core in range(2):
                        part(a, r, core).start()

        own_half = ((grp >= 3) & (grp <= 5)) | (grp == 7)
        land = jnp.where(grp == 7, 3, grp - 3)

        @pl.when(own_half)
        def _():
            to_sibling(0, land).wait_recv()
            sibling_piece(land).start()

        @pl.when(s < n_w)
        def _():
            g_ref[...] = _dot(xt_ref[tab_ref[1, s]], dpc_ref[...])

        @pl.when(own_half)
        def _():
            sibling_piece(land).wait()

        for g in range(2 * N_CHIPS):
            @pl.when(grp == g)
            def _(g=g):
                if g in (0, 1, 2, 6):
                    use = s if g < 3 else 3 * nb + n
                    slot = use % 2

                    @pl.when(use >= 2)
                    def _():
                        to_sibling(slot, 0).wait_send()

                    st_ref[slot] = g_ref[...]
                    to_sibling(slot, min(g, 3)).start()
                elif g in (3, 4, 5):
                    sb_ref[...] = (g_ref[...] + tmp_ref[...]).astype(BF16)
                    stage = pltpu.make_async_copy(sb_ref, wire_ref.at[g - 3, :, cols], wire_sem)
                    stage.start()
                    stage.wait()
                    to_chip(g - 3).start()
                else:
                    slot = n % 2
                    piece = g_ref[...] + tmp_ref[...]

                    @pl.when(n < 2)
                    def _():
                        to_sibling(slot, 0).wait_send()

                    @pl.when(n >= 2)
                    def _():
                        to_result(slot).wait()

                    st_ref[slot] = piece
                    to_result(slot).start()

        @pl.when(s >= n_w)
        def _():
            acc = ALPHA * dr_ref[...]
            for j in range(N_CHIPS):
                acc = acc + _dot_nt(dpr_ref[:, j * W_BLOCK:(j + 1) * W_BLOCK], w_ref[j])
            dx_ref[...] = acc

        @pl.when(s == n_w + n_x - 1)
        def _():
            for slot in range(2):
                to_result(slot).wait()
            for r in range(3):
                all_of_chip(r).wait_recv()
                all_of_chip(r).wait_send()
            for a in range(ns):
                to_sibling_whole(a).wait_recv()
                to_sibling_whole(a).wait_send()
                for r in range(3):
                    for core in range(2):
                        landed(a, r, core).wait_recv()
                        part(a, r, core).wait_send()

    any_spec = pl.BlockSpec(memory_space=pl.ANY)
    tile = lambda s, tab: (jnp.maximum(s - n_w, 0), 0)
    return pl.pallas_call(
        body, name="grad_in_and_x",
        grid_spec=pltpu.PrefetchScalarGridSpec(
            num_scalar_prefetch=1, grid=(n_w + n_x,),
            in_specs=[pl.BlockSpec((2, half, t), lambda s, tab: (0, 0, 0), pipeline_mode=pl.Buffered(1)),
                      pl.BlockSpec((t, tn), lambda s, tab: (0, tab[0, jnp.minimum(s, n_w - 1)])),
                      pl.BlockSpec((tm, D_IN), tile),
                      pl.BlockSpec((N_CHIPS, D, W_BLOCK), lambda s, tab: (0, 0, 0), pipeline_mode=pl.Buffered(1)),
                      pl.BlockSpec((tm, D), tile)] + [any_spec] * (2 * ns),
            out_specs=[pl.BlockSpec((tm, D), tile)] + [any_spec] * (4 + 2 * ns),
            scratch_shapes=[pltpu.VMEM((half, tn), F32), pltpu.VMEM((2, half, tn), F32), pltpu.VMEM((half, tn), BF16),
                            pltpu.VMEM((half, tn), F32),
                            pltpu.SemaphoreType.DMA((2,)), pltpu.SemaphoreType.DMA((N_CHIPS * nb,)),
                            pltpu.SemaphoreType.DMA((3,)), pltpu.SemaphoreType.DMA((3,)),
                            pltpu.SemaphoreType.DMA((2,)), pltpu.SemaphoreType.DMA, pltpu.SemaphoreType.DMA,
                            pltpu.SemaphoreType.DMA((6 * ns,)), pltpu.SemaphoreType.DMA((6 * ns,)),
                            pltpu.SemaphoreType.DMA((ns,)), pltpu.SemaphoreType.DMA((ns,))]),
        out_shape=[jax.ShapeDtypeStruct((t, D), F32), jax.ShapeDtypeStruct((half, W_BLOCK), F32),
                   jax.ShapeDtypeStruct((3, half, W_BLOCK), BF16), jax.ShapeDtypeStruct((N_CHIPS, half, W_BLOCK), F32),
                   jax.ShapeDtypeStruct((3, half, W_BLOCK), BF16)]
        + [jax.ShapeDtypeStruct((6,) + w.shape[2:], w.dtype) for w in wires]
        + [jax.ShapeDtypeStruct(g.shape[2:], F32) for g in grads],
        compiler_params=_params("arbitrary"),
    )(table, xt, dp, dp, w4, dr, *wires, *grads)


def kernel(x, w_in, b_in, conv_w, conv_b, gn_g, gn_b, ln_v_g, ln_v_b, w_spatial, b_spatial, w_pa, w_pb, w_o, b_o, ln_out_g, ln_out_b, loss_target, m_w_in, m_b_in, m_conv_w, m_conv_b, m_gn_g, m_gn_b, m_ln_v_g, m_ln_v_b, m_w_spatial, m_b_spatial, m_w_pa, m_w_pb, m_w_o, m_b_o, m_ln_out_g, m_ln_out_b, v_w_in, v_b_in, v_conv_w, v_conv_b, v_gn_g, v_gn_b, v_ln_v_g, v_ln_v_b, v_w_spatial, v_b_spatial, v_w_pa, v_w_pb, v_w_o, v_b_o, v_ln_out_g, v_ln_out_b):
    n_seq, seq, _ = x.shape
    t = n_seq * seq
    tiles_per_seq = seq // TOKEN_TILE
    x2 = x.reshape(t, D)
    tgt = loss_target.reshape(t, D)

    conv_shard = jnp.pad(conv_w, ((0, HALO - CONV_K), (0, 0)))
    p, win4, wpa4, wpb4, wo4, conv4 = _proj_gather(
        x2, b_in,
        _place_shards([w_in, w_pa, w_pb, w_o, conv_shard], [BF16, BF16, BF16, BF16, F32]))
    win4 = win4.reshape(N_CHIPS, D, W_BLOCK)
    wpa, wpb, wo = wpa4.reshape(D, D), wpb4.reshape(D, D), wo4.reshape(D, D)
    convw = conv4.reshape(N_CHIPS, HALO, D // N_CHIPS).transpose(1, 0, 2).reshape(HALO, D)

    vecs = jnp.stack([conv_b, gn_g, gn_b, ln_v_g, ln_v_b, b_o, ln_out_g, ln_out_b])
    causal = jnp.tril(jnp.ones((CHUNK, CHUNK), bool))
    ws = jnp.where(causal[None], w_spatial, 0.0)
    ws_bf, wst_bf = ws.astype(BF16), ws.transpose(0, 2, 1).astype(BF16)
    bsp = jnp.repeat(b_spatial.T, GROUP_W, axis=1)

    h1, ya, yb, h3, s, mixed, dr, drb, xt, acc_f = _forward_tiles(p, x2, tgt, wpa, wpb, wo, convw, vecs, ws_bf, bsp, tiles_per_seq)
    dh1, dp, dya, dyb, acc_b, dbin_b, dws, dbsp_acc = _backward_tiles(p, h1, ya, yb, drb, wpa, wpb, wo, vecs, ws_bf, wst_bf, bsp)
    dp, dcw8, dbin_a, *square = _conv_backward(dh1, p, dp, convw, [(h3, dya), (s, dyb), (mixed, drb)], tiles_per_seq)

    small, g_conv = _pack_small(acc_f, acc_b, dbin_a, dbin_b, dcw8, dws, dbsp_acc)
    small = small.reshape(2, N_CHIPS, SMALL_ROWS // 8, D)

    grads = square[:3] + [g_conv, small]
    wires = square[3:] + [g_conv, small]
    grad_x, q_in, chips_in, _, _, *landed = _grad_in_and_x(xt.reshape(2, D // 2, t), dp, win4, dr, wires, grads)
    grad_x = grad_x.reshape(x.shape)
    mine = [_add_chips(q_in, chips_in)]
    mine += _add_devices(grads, landed[:5], landed[5:])
    *full, small_parts = _share_results(mine[:5], mine[5])
    grad_w_in, grad_w_pa, grad_w_pb, grad_w_o = [f.reshape(w.shape) for f, w in zip(full[:4], (w_in, w_pa, w_pb, w_o))]
    grad_conv_w = full[4].reshape(HALO, D // N_CHIPS)[:CONV_K]

    big = {"w_in": (grad_w_in,) + tuple(_adamw(w_in, grad_w_in, m_w_in, v_w_in))}
    group = _adamw_group([(w_pa, grad_w_pa, m_w_pa, v_w_pa), (w_pb, grad_w_pb, m_w_pb, v_w_pb),
                          (w_o, grad_w_o, m_w_o, v_w_o)], (conv_w, grad_conv_w, m_conv_w, v_conv_w))
    for name, g, res in zip(["w_pa", "w_pb", "w_o", "conv_w"], [grad_w_pa, grad_w_pb, grad_w_o, grad_conv_w], group):
        big[name] = (g,) + tuple(res)
    vec_names = ["conv_b", "gn_g", "gn_b", "ln_v_g", "ln_v_b", "b_o", "ln_out_g", "ln_out_b"]
    vec_triples = [(conv_b, m_conv_b, v_conv_b), (gn_g, m_gn_g, v_gn_g), (gn_b, m_gn_b, v_gn_b),
                   (ln_v_g, m_ln_v_g, v_ln_v_g), (ln_v_b, m_ln_v_b, v_ln_v_b), (b_o, m_b_o, v_b_o),
                   (ln_out_g, m_ln_out_g, v_ln_out_g), (ln_out_b, m_ln_out_b, v_ln_out_b)]
    small_res, loss8 = _adamw_small(
        small_parts.reshape(8, SMALL_ROWS // 8, D), vec_triples, (b_in, m_b_in, v_b_in),
        (w_spatial, m_w_spatial, v_w_spatial), (b_spatial, m_b_spatial, v_b_spatial))
    per_name = dict(zip(vec_names + ["b_in", "w_spatial", "b_spatial"], small_res))

    order = ["w_in", "b_in", "conv_w", "conv_b", "gn_g", "gn_b", "ln_v_g", "ln_v_b", "w_spatial", "b_spatial",
             "w_pa", "w_pb", "w_o", "b_o", "ln_out_g", "ln_out_b"]
    outs = [loss8[0, 0], grad_x]
    for kind in range(4):
        outs += [big[n][kind] if n in big else per_name[n][kind] for n in order]
    return tuple(outs)
```

```python
import math

import jax
import jax.numpy as jnp
from jax import lax
from jax.experimental import pallas as pl
from jax.experimental.pallas import tpu as pltpu

D = 1024
N_GROUPS = 8
GROUP_W = D // N_GROUPS
CHUNK = 128
CONV_K = 31
HALO = 32
D_IN = 8 * D
N_CHIPS = 4
W_BLOCK = D_IN // N_CHIPS
ALPHA = 2.0 ** 0.25
LN_EPS = 1e-5
ADAM_LR, ADAM_B1, ADAM_B2, ADAM_EPS, ADAM_WD, ADAM_STEP = 0.001, 0.9, 0.999, 1e-08, 0.01, 10

TOKEN_TILE = 256
VMEM_LIMIT = 56 * 1024 * 1024
MESH = pl.DeviceIdType.MESH
F32, BF16 = jnp.float32, jnp.bfloat16


def _sigmoid(x):
    return jax.nn.sigmoid(x)


GELU_C = math.sqrt(2.0 / math.pi)
GELU_CA = GELU_C * 0.044715


def _gelu(x):
    t = jnp.tanh(x * (GELU_C + GELU_CA * (x * x)))
    return x * (0.5 + 0.5 * t)


def _gelu_and_grad(x):
    x2 = x * x
    t = jnp.tanh(x * (GELU_C + GELU_CA * x2))
    cdf = 0.5 + 0.5 * t
    return x * cdf, cdf + (0.5 * x) * (1.0 - t * t) * (GELU_C + (3.0 * GELU_CA) * x2)


def _norm_stats(v):
    mu = jnp.mean(v, axis=-1, keepdims=True)
    vc = v - mu
    var = jnp.mean(vc * vc, axis=-1, keepdims=True)
    rstd = lax.rsqrt(var + LN_EPS)
    return vc * rstd, rstd


def _norm_bwd(dxhat, xhat, rstd):
    m1 = jnp.mean(dxhat, axis=-1, keepdims=True)
    m2 = jnp.mean(dxhat * xhat, axis=-1, keepdims=True)
    return rstd * (dxhat - m1 - xhat * m2)


def _dot(a, b):
    return jnp.dot(a, b, preferred_element_type=F32)


def _dot_nt(a, b):
    return lax.dot_general(a, b, (((1,), (1,)), ((), ())), preferred_element_type=F32)


def _dot_tn(a, b):
    return lax.dot_general(a, b, (((0,), (0,)), ((), ())), preferred_element_type=F32)


def _colsum(v):
    return jnp.sum(v, axis=0, keepdims=True)


def _full(shape):
    return pl.BlockSpec(shape, lambda *_: (0,) * len(shape))


def _resident(shape):
    return pl.BlockSpec(shape, lambda *_: (0,) * len(shape), pipeline_mode=pl.Buffered(1))


def _params(*sem):
    return pltpu.CompilerParams(dimension_semantics=sem, vmem_limit_bytes=VMEM_LIMIT)


def _chip_index():
    return (2 * lax.axis_index("x") + lax.axis_index("y")).astype(jnp.int32).reshape(1)


def _core_index():
    return lax.axis_index("c").astype(jnp.int32).reshape(1)


def _place_shards(ws, dtypes):
    n = len(ws)

    def body(k_ref, *refs):
        for w_ref, o_ref, dtype in zip(refs[:n], refs[n:], dtypes):
            rows = w_ref.shape[0] // 2
            for h in range(2):
                o_ref[h] = w_ref[h * rows:(h + 1) * rows, :].astype(dtype)

    return pl.pallas_call(
        body, name="place_shards",
        grid_spec=pltpu.PrefetchScalarGridSpec(
            num_scalar_prefetch=1, grid=(1,),
            in_specs=[pl.BlockSpec(w.shape, lambda i, k: (0, 0)) for w in ws],
            out_specs=[pl.BlockSpec((None, 2, w.shape[0] // 2, w.shape[1]), lambda i, k: (k[0], 0, 0, 0)) for w in ws]),
        out_shape=[jax.ShapeDtypeStruct((N_CHIPS, 2, w.shape[0] // 2, w.shape[1]), dt) for w, dt in zip(ws, dtypes)],
        compiler_params=_params("arbitrary"),
    )(_chip_index(), *ws)


def _position():
    x, y, c = lax.axis_index("x"), lax.axis_index("y"), lax.axis_index("c")
    return x, y, c, 2 * x + y


def _other_chips(x, y):
    return [(1 - x, y), (x, 1 - y), (1 - x, 1 - y)]


def _any_specs(n):
    return [pl.BlockSpec(memory_space=pl.ANY)] * n


def _proj_gather(x, b_in, bufs):
    t = x.shape[0]
    tm = 1024
    steps = t // tm
    ahead = steps // 2
    half = D // 2
    chunk = W_BLOCK // 2
    n = len(bufs)
    xi, yi = lax.axis_index("x"), lax.axis_index("y")
    chips = [2 * xi + yi, 2 * (1 - xi) + yi, 2 * xi + (1 - yi), 2 * (1 - xi) + (1 - yi)]
    plan = [(0, 0), (0, 1), (1, 0), (2, 1), (1, 1), (2, 0), (3, 0), (3, 1)]
    order = jnp.stack([2 * chips[ch] + q for ch, q in plan]).astype(jnp.int32)

    def body(order_ref, x_ref, b_ref, *refs):
        p_ref, outs = refs[n], refs[n + 1:2 * n + 1]
        xb_ref, w_ref, lsem, send, recv, hop_send, hop_recv, fsend, frecv, qsend, qrecv = refs[2 * n + 1:]
        jj, i = pl.program_id(0), pl.program_id(1)
        x_, y_, c, k = _position()
        nbrs = [(1 - x_, y_), (x_, 1 - y_)]
        blocks = [2 * (1 - x_) + y_, 2 * x_ + (1 - y_), 2 * (1 - x_) + (1 - y_)]

        def quarter(a, block, q, h):
            if a == 0:
                return outs[0].at[block, h, :, pl.ds(q * chunk, chunk)]
            rows = outs[a].shape[2] // 2
            return outs[a].at[block, h, pl.ds(q * rows, rows)]

        def copy(ref, to, send_sem, recv_sem):
            return pltpu.make_async_remote_copy(src_ref=ref, dst_ref=ref, send_sem=send_sem, recv_sem=recv_sem,
                                                device_id=(to[0], to[1], c), device_id_type=MESH)

        def sent(a, nb, q):
            return copy(quarter(a, k, q, c), nbrs[nb], send.at[4 * a + 2 * nb + q], recv.at[4 * a + 2 * nb + q])

        def landed(a, nb, q):
            return copy(quarter(a, blocks[nb], q, c), nbrs[nb], send.at[4 * a + 2 * nb + q], recv.at[4 * a + 2 * nb + q])

        def hopped(a, nb):
            return copy(quarter(a, blocks[nb], nb, c), nbrs[1 - nb], hop_send.at[2 * a + nb], hop_recv.at[2 * a + 1 - nb])

        def from_diagonal(a, via):
            return copy(quarter(a, blocks[2], 1 - via, c), nbrs[via], hop_send.at[2 * a + via], hop_recv.at[2 * a + via])

        def passed(a, r, h):
            return pltpu.make_async_remote_copy(
                src_ref=outs[a].at[blocks[r], h], dst_ref=outs[a].at[blocks[r], h], send_sem=fsend.at[3 * a + r],
                recv_sem=frecv.at[3 * a + r], device_id=(x_, y_, 1 - c), device_id_type=MESH)

        def passed_quarter(r, q, h):
            ref = quarter(0, blocks[r], q, h)
            return pltpu.make_async_remote_copy(
                src_ref=ref, dst_ref=ref, send_sem=qsend.at[2 * r + q], recv_sem=qrecv.at[2 * r + q],
                device_id=(x_, y_, 1 - c), device_id_type=MESH)

        def load(block, q, slot):
            return [pltpu.make_async_copy(quarter(0, block, q, h), w_ref.at[slot, pl.ds(h * half, half)],
                                          lsem.at[2 * slot + h]) for h in range(2)]

        def pass_on(arrays):
            for a in arrays:
                for nb in range(2):
                    landed(a, nb, nb).wait_recv()
                    hopped(a, nb).start()

        @pl.when((jj == 0) & (i == 0))
        def _():
            for a in range(n):
                for nb, q in ((0, 0), (1, 1), (0, 1), (1, 0)):
                    sent(a, nb, q).start()
            for cp in load(k, 0, 0):
                cp.start()

        for nxt in range(1, len(plan)):
            @pl.when((jj == nxt - 1) & (i == ahead))
            def _(nxt=nxt):
                ch, q = plan[nxt]
                if ch in (1, 2):
                    landed(0, ch - 1, q).wait_recv()
                    if q == ch - 1:
                        hopped(0, ch - 1).start()
                elif ch == 3:
                    from_diagonal(0, 1 - q).wait_recv()
                if ch:
                    passed_quarter(ch - 1, q, c).start()
                    passed_quarter(ch - 1, q, 1 - c).wait_recv()
                for cp in load(k if ch == 0 else blocks[ch - 1], q, nxt % 2):
                    cp.start()
                if nxt == 5:
                    pass_on(range(1, n))
                    for a in range(1, n):
                        landed(a, 0, 1).wait_recv()
                        passed(a, 0, c).start()
                        landed(a, 1, 0).wait_recv()
                        passed(a, 1, c).start()

        slot = jj % 2

        @pl.when(i == 0)
        def _():
            for cp in load(k, 0, slot):
                cp.wait()

        rows = pl.ds(pl.multiple_of(i * tm, tm), tm)

        @pl.when(jj == 0)
        def _():
            xb_ref[rows, :] = x_ref[...].astype(BF16)

        p_ref[...] = _dot(xb_ref[rows, :], w_ref[slot]) + b_ref[...]

        @pl.when((jj == len(plan) - 1) & (i == steps - 1))
        def _():
            for a in range(1, n):
                from_diagonal(a, 0).wait_recv()
                from_diagonal(a, 1).wait_recv()
                passed(a, 2, c).start()
            for a in range(1, n):
                for r in range(3):
                    passed(a, r, 1 - c).wait_recv()
                    passed(a, r, c).wait_send()
            for r in range(3):
                for q in range(2):
                    passed_quarter(r, q, c).wait_send()
            for a in range(n):
                for nb in range(2):
                    for q in range(2):
                        sent(a, nb, q).wait_send()
                    hopped(a, nb).wait_send()

    any_spec = pl.BlockSpec(memory_space=pl.ANY)
    return pl.pallas_call(
        body, name="proj_gather",
        grid_spec=pltpu.PrefetchScalarGridSpec(
            num_scalar_prefetch=1, grid=(len(plan), steps),
            in_specs=[pl.BlockSpec((tm, D), lambda jj, i, o: (jnp.where(jj == 0, i, steps - 1), 0)),
                      pl.BlockSpec((None, 1, chunk), lambda jj, i, o: (o[jj], 0, 0))] + [any_spec] * n,
            out_specs=[pl.BlockSpec((tm, chunk), lambda jj, i, o: (i, o[jj]))] + [any_spec] * n,
            scratch_shapes=[pltpu.VMEM((t, D), BF16), pltpu.VMEM((2, D, chunk), BF16), pltpu.SemaphoreType.DMA((4,)),
                            pltpu.SemaphoreType.DMA((4 * n,)), pltpu.SemaphoreType.DMA((4 * n,)),
                            pltpu.SemaphoreType.DMA((2 * n,)), pltpu.SemaphoreType.DMA((2 * n,)),
                            pltpu.SemaphoreType.DMA((3 * n,)), pltpu.SemaphoreType.DMA((3 * n,)),
                            pltpu.SemaphoreType.DMA((6,)), pltpu.SemaphoreType.DMA((6,))]),
        out_shape=[jax.ShapeDtypeStruct((t, D_IN), F32)] + [jax.ShapeDtypeStruct(b.shape, b.dtype) for b in bufs],
        input_output_aliases={3 + a: 1 + a for a in range(n)},
        compiler_params=_params("arbitrary", "arbitrary"),
    )(order, x, b_in.reshape(D_IN // chunk, 1, chunk), *bufs)


def _share_results(bufs, small):
    n = len(bufs)

    def body(*refs):
        outs, small_out = refs[n + 1:2 * n + 1], refs[2 * n + 1]
        send, recv, ssend, srecv = refs[2 * n + 2:]
        x, y, c, k = _position()
        cps = []
        for a in range(n):
            cp = pltpu.make_async_remote_copy(
                src_ref=outs[a].at[c], dst_ref=outs[a].at[c], send_sem=send.at[a], recv_sem=recv.at[a],
                device_id=(x, y, 1 - c), device_id_type=MESH)
            cp.start()
            cps.append(cp)
        waits = []
        for p in range(1, 8):
            px, py, pc = x ^ (p >> 2), y ^ ((p >> 1) & 1), c ^ (p & 1)
            cp = pltpu.make_async_remote_copy(
                src_ref=small_out.at[k, c], dst_ref=small_out.at[k, c], send_sem=ssend.at[p - 1],
                recv_sem=srecv.at[p - 1], device_id=(px, py, pc), device_id_type=MESH)
            cp.start()
            cps.append(cp)
            waits.append(pltpu.make_async_remote_copy(
                src_ref=small_out.at[2 * px + py, pc], dst_ref=small_out.at[2 * px + py, pc], send_sem=ssend.at[p - 1],
                recv_sem=srecv.at[p - 1], device_id=(px, py, pc), device_id_type=MESH))
        for a in range(n):
            pltpu.make_async_remote_copy(
                src_ref=outs[a].at[1 - c], dst_ref=outs[a].at[1 - c], send_sem=send.at[a], recv_sem=recv.at[a],
                device_id=(x, y, 1 - c), device_id_type=MESH).wait_recv()
        for w in waits:
            w.wait_recv()
        for cp in cps:
            cp.wait_send()

    return pl.pallas_call(
        body, name="rs_share_results",
        in_specs=_any_specs(n + 1), out_specs=_any_specs(n + 1),
        out_shape=[jax.ShapeDtypeStruct(b.shape, b.dtype) for b in bufs + [small]],
        scratch_shapes=[pltpu.SemaphoreType.DMA((n,)), pltpu.SemaphoreType.DMA((n,)),
                        pltpu.SemaphoreType.DMA((7,)), pltpu.SemaphoreType.DMA((7,))],
        input_output_aliases={a: a for a in range(n + 1)},
    )(*bufs, small)


def _row_tile(r, c):
    t = max(8, min(r, (1 << 18) // c))
    while r % t:
        t //= 2
    return t


def _add_devices(grads, lands, sibs):
    n = len(grads)

    def body(kc_ref, *refs):
        for g_ref, l_ref, s_ref, f_ref in zip(refs[:n], refs[n:2 * n], refs[2 * n:3 * n], refs[3 * n:]):
            f = g_ref[...] + s_ref[...]
            for i in range(l_ref.shape[0]):
                f = f + l_ref[i].astype(F32)
            f_ref[...] = f

    shapes = [g.shape[2:] for g in grads]
    out_specs = [pl.BlockSpec((None,) + sh, lambda i, kc: (kc[1], 0, 0)) for sh in shapes[:-1]]
    out_specs.append(pl.BlockSpec((None, None) + shapes[-1], lambda i, kc: (kc[0], kc[1], 0, 0)))
    out_shape = [jax.ShapeDtypeStruct((2,) + sh, F32) for sh in shapes[:-1]]
    out_shape.append(jax.ShapeDtypeStruct((N_CHIPS, 2) + shapes[-1], F32))
    return pl.pallas_call(
        body, name="rs_add_devices",
        grid_spec=pltpu.PrefetchScalarGridSpec(
            num_scalar_prefetch=1, grid=(1,),
            in_specs=[pl.BlockSpec((None, None) + sh, lambda i, kc: (kc[1], kc[0], 0, 0)) for sh in shapes]
            + [pl.BlockSpec(l.shape, lambda i, kc: (0, 0, 0)) for l in lands]
            + [pl.BlockSpec(sh, lambda i, kc: (0, 0)) for sh in shapes],
            out_specs=out_specs),
        out_shape=out_shape,
        compiler_params=_params("arbitrary"),
    )(jnp.concatenate([_chip_index(), _core_index()]), *grads, *lands, *sibs)


def _add_chips(q, b2):
    r, c = q.shape
    t = _row_tile(r, c)

    def body(c_ref, q_ref, b_ref, f_ref):
        f_ref[...] = ((q_ref[...] + b_ref[0].astype(F32)) + b_ref[1].astype(F32)) + b_ref[2].astype(F32)

    return pl.pallas_call(
        body, name="rs_add_chips",
        grid_spec=pltpu.PrefetchScalarGridSpec(
            num_scalar_prefetch=1, grid=(r // t,),
            in_specs=[pl.BlockSpec((t, c), lambda i, cr: (i, 0)), pl.BlockSpec((3, t, c), lambda i, cr: (0, i, 0))],
            out_specs=pl.BlockSpec((None, t, c), lambda i, cr: (cr[0], i, 0))),
        out_shape=jax.ShapeDtypeStruct((2, r, c), F32),
        compiler_params=_params("parallel"),
    )(_core_index(), q, b2)


def _adamw_math(w, g, m, v):
    m = ADAM_B1 * m + (1.0 - ADAM_B1) * g
    v = ADAM_B2 * v + (1.0 - ADAM_B2) * (g * g)
    m_hat = m / (1.0 - ADAM_B1 ** ADAM_STEP)
    v_hat = v / (1.0 - ADAM_B2 ** ADAM_STEP)
    delta = -ADAM_LR * (m_hat / (jnp.sqrt(v_hat) + ADAM_EPS) + ADAM_WD * w)
    return delta, m, v


def _adamw_group(quads, conv):
    n = len(quads)
    r, c = quads[0][0].shape

    def body(*refs):
        ins, outs = refs[:4 * (n + 1)], refs[4 * (n + 1):]
        for i in range(n + 1):
            w_ref, g_ref, m_ref, v_ref = ins[4 * i:4 * i + 4]
            res = _adamw_math(w_ref[...], g_ref[...], m_ref[...], v_ref[...])
            for o_ref, val in zip(outs[3 * i:3 * i + 3], res):
                o_ref[...] = val

    half = pl.BlockSpec((r // 2, c), lambda i: (i, 0))
    whole = pl.BlockSpec(conv[0].shape, lambda i: (0, 0))
    res = pl.pallas_call(
        body, name="adamw_group", grid=(2,),
        in_specs=[half] * (4 * n) + [whole] * 4, out_specs=[half] * (3 * n) + [whole] * 3,
        out_shape=[jax.ShapeDtypeStruct((r, c), F32)] * (3 * n) + [jax.ShapeDtypeStruct(conv[0].shape, F32)] * 3,
        compiler_params=_params("arbitrary"),
    )(*[a for quad in quads for a in quad], *conv)
    return [res[3 * i:3 * i + 3] for i in range(n + 1)]


def _adamw(w, g, m, v):
    r, c = w.shape
    t = _row_tile(r, c) if r % 8 == 0 else r

    def body(w_ref, g_ref, m_ref, v_ref, d_ref, nm_ref, nv_ref):
        d_ref[...], nm_ref[...], nv_ref[...] = _adamw_math(w_ref[...], g_ref[...], m_ref[...], v_ref[...])

    spec = pl.BlockSpec((t, c), lambda i: (i, 0))
    return pl.pallas_call(
        body, name="adamw", grid=(r // t,), in_specs=[spec] * 4, out_specs=[spec] * 3,
        out_shape=[jax.ShapeDtypeStruct((r, c), F32)] * 3, compiler_params=_params("parallel"),
    )(w, g, m, v)


ROW_B_IN = 0
ROW_VECS = 8
ROW_LOSS = 16
ROW_B_SPATIAL = 24
ROW_W_SPATIAL = 32
SMALL_ROWS = 192
N_VECS = 8


def _pack_small(acc_f, acc_b, dbin_a, dbin_b, dcw8, dws, dbsp):
    cols = D // N_CHIPS

    def body(af_ref, ab_ref, da_ref, db_ref, cw_ref, ws_ref, bs_ref, o_ref, gc_ref):
        o_ref[...] = jnp.zeros_like(o_ref)
        for j in range(D_IN // D):
            src = da_ref if j < 2 else db_ref
            o_ref[ROW_B_IN + j:ROW_B_IN + j + 1, :] = src[0:1, j * D:(j + 1) * D]
        dcw = jnp.sum(cw_ref[...].reshape(HALO, SUBLANES, D), axis=1)
        o_ref[ROW_VECS:ROW_VECS + 1, :] = dcw[CONV_K:CONV_K + 1]
        o_ref[ROW_VECS + 1:ROW_VECS + 5, :] = ab_ref[0:4, :]
        o_ref[ROW_VECS + 5:ROW_VECS + 6, :] = af_ref[2:3, :]
        o_ref[ROW_VECS + 6:ROW_VECS + 8, :] = af_ref[0:2, :]
        o_ref[ROW_LOSS:ROW_LOSS + 1, :] = af_ref[3:4, :]
        head = lax.broadcasted_iota(jnp.int32, (N_GROUPS, D), 0)
        lane = lax.broadcasted_iota(jnp.int32, (N_GROUPS, D), 1)
        indicator = jnp.where(lane // GROUP_W == head, 1.0, 0.0)
        o_ref[ROW_B_SPATIAL:ROW_B_SPATIAL + N_GROUPS, 0:CHUNK] = lax.dot_general(
            indicator, bs_ref[...], (((1,), (1,)), ((), ())), precision=lax.Precision.HIGHEST, preferred_element_type=F32)
        t_idx = lax.broadcasted_iota(jnp.int32, (CHUNK, D), 0)
        s_idx = lax.broadcasted_iota(jnp.int32, (CHUNK, D), 1) % CHUNK
        o_ref[ROW_W_SPATIAL:ROW_W_SPATIAL + CHUNK, :] = jnp.where(s_idx <= t_idx, ws_ref[...], 0.0)
        for h in range(2):
            for j in range(N_CHIPS):
                gc_ref[h, j] = dcw[h * (HALO // 2):(h + 1) * (HALO // 2), j * cols:(j + 1) * cols]

    ins = [acc_f, acc_b, dbin_a, dbin_b, dcw8, dws, dbsp]
    return pl.pallas_call(
        body, name="pack_small",
        in_specs=[_full(a.shape) for a in ins],
        out_specs=[_full((SMALL_ROWS, D)), _full((2, N_CHIPS, HALO // 2, cols))],
        out_shape=[jax.ShapeDtypeStruct((SMALL_ROWS, D), F32), jax.ShapeDtypeStruct((2, N_CHIPS, HALO // 2, cols), F32)],
        compiler_params=_params(),
    )(*ins)


def _adamw_small(parts, vecs, b_in, w_spatial, b_spatial):
    triples = list(vecs) + [b_in, w_spatial, b_spatial]
    n_in = 3 * len(triples)

    def body(p_ref, *refs):
        ins = [refs[3 * i:3 * i + 3] for i in range(len(triples))]
        outs = [refs[n_in + 4 * i:n_in + 4 * i + 4] for i in range(len(triples))]
        loss_ref, g_ref = refs[n_in + 4 * len(triples):]
        rows = SMALL_ROWS // 8
        for k in range(N_CHIPS):
            for core in range(2):
                g_ref[(core * N_CHIPS + k) * rows:(core * N_CHIPS + k + 1) * rows, :] = p_ref[2 * k + core]

        def step(g, wmv, out, get, put):
            d, nm, nv = _adamw_math(get(wmv[0]), g, get(wmv[1]), get(wmv[2]))
            for o, val in zip(out, (g, d, nm, nv)):
                put(o, val)

        for i in range(N_VECS):
            step(g_ref[ROW_VECS + i:ROW_VECS + i + 1, :], ins[i], outs[i],
                 lambda r: r[...].reshape(1, D), lambda o, val: o.__setitem__(Ellipsis, val.reshape(D)))
        for j in range(D_IN // D):
            piece = pl.ds(j * D, D)
            step(g_ref[ROW_B_IN + j:ROW_B_IN + j + 1, :], ins[N_VECS], outs[N_VECS],
                 lambda r: r[piece].reshape(1, D), lambda o, val: o.__setitem__(piece, val.reshape(D)))
        for h in range(N_GROUPS):
            step(g_ref[ROW_W_SPATIAL:ROW_W_SPATIAL + CHUNK, h * CHUNK:(h + 1) * CHUNK], ins[N_VECS + 1], outs[N_VECS + 1],
                 lambda r: r[h], lambda o, val: o.__setitem__(h, val))
        step(g_ref[ROW_B_SPATIAL:ROW_B_SPATIAL + N_GROUPS, 0:CHUNK], ins[N_VECS + 2], outs[N_VECS + 2],
             lambda r: r[...], lambda o, val: o.__setitem__(Ellipsis, val))
        lanes = g_ref[ROW_LOSS:ROW_LOSS + 1, :]
        loss_ref[...] = jnp.broadcast_to(jnp.sum(lanes, axis=1, keepdims=True), (8, 128))

    flat = [a for tr in triples for a in tr]
    out_shape = [jax.ShapeDtypeStruct(tr[0].shape, F32) for tr in triples for _ in range(4)]
    out_shape.append(jax.ShapeDtypeStruct((8, 128), F32))
    res = pl.pallas_call(
        body, name="adamw_small",
        in_specs=[_full(parts.shape)] + [_full(a.shape) for a in flat],
        out_specs=[_full(o.shape) for o in out_shape],
        out_shape=out_shape,
        scratch_shapes=[pltpu.VMEM((SMALL_ROWS, D), F32)],
        compiler_params=_params(),
    )(parts, *flat)
    return [res[4 * i:4 * i + 4] for i in range(len(triples))], res[-1]


SUBLANES = 8
SHIFT_ROWS = HALO - SUBLANES


def _shifted_copies(src_ref, sh_ref, cs, tm):
    for p in range(1, SUBLANES):
        sh_ref[p - 1] = src_ref[pl.ds(p, tm + SHIFT_ROWS), cs]


def _tap(src_ref, sh_ref, cs, offset, start, rows):
    p, q = offset % SUBLANES, offset // SUBLANES
    if p == 0:
        return src_ref[pl.ds(start + SUBLANES * q, rows), cs]
    return sh_ref[p - 1, pl.ds(start + SUBLANES * q, rows), :]


def _conv_taps(src_ref, sh_ref, w_ref, first_offset, step, bias, dst_ref, tm):
    rows = 64
    for g in range(N_GROUPS):
        cs = slice(g * GROUP_W, (g + 1) * GROUP_W)
        _shifted_copies(src_ref, sh_ref, cs, tm)
        for rb in range(tm // rows):
            acc = jnp.zeros((rows, GROUP_W), F32) + (bias[:, cs] if bias is not None else 0.0)
            for k in range(CONV_K):
                acc = acc + w_ref[k:k + 1, cs] * _tap(src_ref, sh_ref, cs, first_offset + step * k, rb * rows, rows)
            dst_ref[rb * rows:(rb + 1) * rows, cs] = acc


def _conv_weight_grad(d_ref, src_ref, sh_ref, first_offset, acc_ref, tm):
    rows = 64
    for g in range(N_GROUPS):
        cs = slice(g * GROUP_W, (g + 1) * GROUP_W)
        _shifted_copies(src_ref, sh_ref, cs, tm)
        for rb in range(tm // rows):
            d = d_ref[rb * rows:(rb + 1) * rows, cs]
            for k in range(CONV_K):
                prod = d * _tap(src_ref, sh_ref, cs, first_offset + k, rb * rows, rows)
                acc_ref[SUBLANES * k:SUBLANES * (k + 1), cs] += jnp.sum(
                    prod.reshape(rows // SUBLANES, SUBLANES, GROUP_W), axis=0)


def _spatial_mix(w_ref, v_bf, tm):
    rows = []
    for q in range(tm // CHUNK):
        cols = [_dot(w_ref[h], v_bf[q * CHUNK:(q + 1) * CHUNK, h * GROUP_W:(h + 1) * GROUP_W])
                for h in range(N_GROUPS)]
        rows.append(jnp.concatenate(cols, axis=1))
    return jnp.concatenate(rows, axis=0)


def _group_norm_fwd(h1, gn_g, gn_b):
    xhat, rstd = [], []
    for g in range(N_GROUPS):
        xh, rs = _norm_stats(h1[:, g * GROUP_W:(g + 1) * GROUP_W])
        xhat.append(xh)
        rstd.append(rs)
    xhat = jnp.concatenate(xhat, axis=1)
    return xhat * gn_g + gn_b, xhat, rstd


def _forward_tiles(p, x, tgt, wpa, wpb, wo, convw, vecs, ws, bsp, tiles_per_seq):
    t = x.shape[0]
    tm = TOKEN_TILE
    hb = tm // HALO

    def body(p_ref, ph_ref, x_ref, t_ref, wpa_ref, wpb_ref, wo_ref, cw_ref, vec_ref, ws_ref, bsp_ref,
             h1_ref, ya_ref, yb_ref, h3_ref, s_ref, mx_ref, dr_ref, drb_ref, xt_ref, acc_ref, he_ref, sh_ref):
        i = pl.program_id(0)
        xt_ref[...] = x_ref[...].T.astype(BF16)
        conv_b, gn_g, gn_b, lnv_g, lnv_b, b_o, lno_g, lno_b = [vec_ref[j:j + 1, :] for j in range(8)]

        keep = jnp.where(i % tiles_per_seq == 0, 0.0, 1.0)
        he_ref[0:HALO, :] = ph_ref[:, 0:D] * _sigmoid(ph_ref[:, D:2 * D]) * keep
        he_ref[HALO:, :] = p_ref[:, 0:D] * _sigmoid(p_ref[:, D:2 * D])
        _conv_taps(he_ref, sh_ref, cw_ref, HALO - (CONV_K - 1), 1, conv_b, h1_ref, tm)
        h2, _, _ = _group_norm_fwd(h1_ref[...], gn_g, gn_b)
        a_gate = p_ref[:, 2 * D:3 * D]
        h3 = ((h2 * _sigmoid(h2)) * (a_gate * _sigmoid(a_gate))).astype(BF16)
        h3_ref[...] = h3
        ya = _dot(h3, wpa_ref[...])
        ya_ref[...] = ya

        u = _gelu(p_ref[:, 3 * D:4 * D])
        vhat, _ = _norm_stats(_gelu(p_ref[:, 4 * D:5 * D]))
        v1 = (vhat * lnv_g + lnv_b).astype(BF16)
        b_gate = p_ref[:, 5 * D:6 * D]
        vmix = _spatial_mix(ws_ref, v1, tm) + jnp.concatenate([bsp_ref[...]] * (tm // CHUNK), axis=0)
        s = (u * vmix * (b_gate * _sigmoid(b_gate))).astype(BF16)
        s_ref[...] = s
        yb = _dot(s, wpb_ref[...])
        yb_ref[...] = yb

        mixed = (_sigmoid(p_ref[:, 6 * D:7 * D]) * ya + _sigmoid(p_ref[:, 7 * D:8 * D]) * yb).astype(BF16)
        mx_ref[...] = mixed
        r = ALPHA * x_ref[...] + (_dot(mixed, wo_ref[...]) + b_o)
        xhat, rstd = _norm_stats(r)
        err = (xhat * lno_g + lno_b) - t_ref[...]
        dout = err * (1.0 / D)
        dr = _norm_bwd(dout * lno_g, xhat, rstd)
        dr_ref[...] = dr
        drb_ref[...] = dr.astype(BF16)

        @pl.when(i == 0)
        def _():
            acc_ref[...] = jnp.zeros_like(acc_ref)

        acc_ref[0:1, :] += _colsum(dout * xhat)
        acc_ref[1:2, :] += _colsum(dout)
        acc_ref[2:3, :] += _colsum(dr)
        acc_ref[3:4, :] += _colsum(err * err) * (0.5 / D)

    tile = lambda w: pl.BlockSpec((tm, w), lambda i: (i, 0))
    f32_out = jax.ShapeDtypeStruct((t, D), F32)
    bf_out = jax.ShapeDtypeStruct((t, D), BF16)
    return pl.pallas_call(
        body, name="forward_tiles", grid=(t // tm,),
        in_specs=[tile(D_IN),
                  pl.BlockSpec((HALO, 2 * D), lambda i: (jnp.maximum(i * hb - 1, 0), 0)),
                  tile(D), tile(D), _resident((D, D)), _resident((D, D)), _resident((D, D)), _full((HALO, D)), _full((8, D)),
                  _full((N_GROUPS, CHUNK, CHUNK)), _full((CHUNK, D))],
        out_specs=[tile(D)] * 8 + [pl.BlockSpec((D, tm), lambda i: (0, i)), _full((8, D))],
        out_shape=[f32_out, f32_out, f32_out, bf_out, bf_out, bf_out, f32_out, bf_out,
                   jax.ShapeDtypeStruct((D, t), BF16), jax.ShapeDtypeStruct((8, D), F32)],
        scratch_shapes=[pltpu.VMEM((tm + HALO, D), F32), pltpu.VMEM((SUBLANES - 1, tm + SHIFT_ROWS, GROUP_W), F32)],
        compiler_params=_params("arbitrary"),
    )(p, p, x, tgt, wpa, wpb, wo, convw, vecs, ws, bsp)


def _backward_tiles(p, h1, ya, yb, drb, wpa, wpb, wo, vecs, ws, wst, bsp):
    t = h1.shape[0]
    tm = TOKEN_TILE

    def body(p_ref, h1_ref, ya_ref, yb_ref, drb_ref, wpa_ref, wpb_ref, wo_ref, vec_ref, ws_ref, wst_ref, bsp_ref,
             dh1_ref, dp_ref, dya_ref, dyb_ref, acc_ref, dbin_ref, dws_ref, dbsp_ref):
        i = pl.program_id(0)
        _, gn_g, gn_b, lnv_g, lnv_b = [vec_ref[j:j + 1, :] for j in range(5)]

        @pl.when(i == 0)
        def _():
            acc_ref[...] = jnp.zeros_like(acc_ref)
            dbin_ref[...] = jnp.zeros_like(dbin_ref)
            dws_ref[...] = jnp.zeros_like(dws_ref)
            dbsp_ref[...] = jnp.zeros_like(dbsp_ref)

        def emit(block, val):
            dbin_ref[0:1, block * D:(block + 1) * D] += _colsum(val)
            dp_ref[:, block * D:(block + 1) * D] = val.astype(BF16)

        dp_ref[:, 0:2 * D] = jnp.zeros((tm, 2 * D), BF16)
        dmixed = _dot_nt(drb_ref[...], wo_ref[...])
        ga = _sigmoid(p_ref[:, 6 * D:7 * D])
        gb = _sigmoid(p_ref[:, 7 * D:8 * D])
        dya32, dyb32 = dmixed * ga, dmixed * gb
        dya = dya32.astype(BF16)
        dyb = dyb32.astype(BF16)
        dya_ref[...] = dya
        dyb_ref[...] = dyb
        emit(6, dya32 * ya_ref[...] * (1.0 - ga))
        emit(7, dyb32 * yb_ref[...] * (1.0 - gb))

        dh3 = _dot_nt(dya, wpa_ref[...])
        h2, xhat, rstd = _group_norm_fwd(h1_ref[...], gn_g, gn_b)
        sg = _sigmoid(h2)
        a_gate = p_ref[:, 2 * D:3 * D]
        sa = _sigmoid(a_gate)
        silu_h2, silu_a = h2 * sg, a_gate * sa
        dh2 = dh3 * silu_a * (sg + silu_h2 * (1.0 - sg))
        emit(2, dh3 * silu_h2 * (sa + silu_a * (1.0 - sa)))
        acc_ref[0:1, :] += _colsum(dh2 * xhat)
        acc_ref[1:2, :] += _colsum(dh2)
        dxhat = dh2 * gn_g
        for g in range(N_GROUPS):
            cs = slice(g * GROUP_W, (g + 1) * GROUP_W)
            dh1_ref[:, cs] = _norm_bwd(dxhat[:, cs], xhat[:, cs], rstd[g])

        ds = _dot_nt(dyb, wpb_ref[...])
        u_pre = p_ref[:, 3 * D:4 * D]
        u, du_dpre = _gelu_and_grad(u_pre)
        v0, dv_dpre = _gelu_and_grad(p_ref[:, 4 * D:5 * D])
        vhat, vrstd = _norm_stats(v0)
        v1 = (vhat * lnv_g + lnv_b).astype(BF16)
        vmix = _spatial_mix(ws_ref, v1, tm) + jnp.concatenate([bsp_ref[...]] * (tm // CHUNK), axis=0)
        b_gate = p_ref[:, 5 * D:6 * D]
        sb = _sigmoid(b_gate)
        silu_b = b_gate * sb
        emit(3, ds * vmix * silu_b * du_dpre)
        emit(5, ds * u * vmix * (sb + silu_b * (1.0 - sb)))
        dvmix = ds * u * silu_b
        dvmix_bf = dvmix.astype(BF16)
        for q in range(tm // CHUNK):
            dbsp_ref[...] += dvmix[q * CHUNK:(q + 1) * CHUNK, :]
            for h in range(N_GROUPS):
                blk = (slice(q * CHUNK, (q + 1) * CHUNK), slice(h * GROUP_W, (h + 1) * GROUP_W))
                dws_ref[:, h * GROUP_W:(h + 1) * GROUP_W] += _dot_nt(dvmix_bf[blk], v1[blk])
        dv1 = _spatial_mix(wst_ref, dvmix_bf, tm)
        acc_ref[2:3, :] += _colsum(dv1 * vhat)
        acc_ref[3:4, :] += _colsum(dv1)
        emit(4, _norm_bwd(dv1 * lnv_g, vhat, vrstd) * dv_dpre)

    tile = lambda w: pl.BlockSpec((tm, w), lambda i: (i, 0))
    return pl.pallas_call(
        body, name="backward_tiles", grid=(t // tm,),
        in_specs=[tile(D_IN), tile(D), tile(D), tile(D), tile(D), _resident((D, D)), _resident((D, D)), _resident((D, D)),
                  _full((8, D)), _full((N_GROUPS, CHUNK, CHUNK)), _full((N_GROUPS, CHUNK, CHUNK)), _full((CHUNK, D))],
        out_specs=[tile(D), tile(D_IN), tile(D), tile(D), _full((8, D)), _full((8, D_IN)),
                   _full((CHUNK, D)), _full((CHUNK, D))],
        out_shape=[jax.ShapeDtypeStruct((t, D), F32), jax.ShapeDtypeStruct((t, D_IN), BF16),
                   jax.ShapeDtypeStruct((t, D), BF16), jax.ShapeDtypeStruct((t, D), BF16),
                   jax.ShapeDtypeStruct((8, D), F32), jax.ShapeDtypeStruct((8, D_IN), F32),
                   jax.ShapeDtypeStruct((CHUNK, D), F32), jax.ShapeDtypeStruct((CHUNK, D), F32)],
        compiler_params=_params("arbitrary"),
    )(p, h1, ya, yb, drb, wpa, wpb, wo, vecs, ws, wst, bsp)


def _conv_backward(dh1, p, dp, convw, pairs, tiles_per_seq):
    t = dh1.shape[0]
    tm = TOKEN_TILE
    hb = tm // HALO
    last = t // HALO - 1
    n_sq = len(pairs)
    span = 2
    rows = D // 8

    def body(dh1_ref, dnext_ref, p_ref, ph_ref, cw_ref, dp_in_ref, *refs):
        del dp_in_ref
        sq_in = refs[:2 * n_sq]
        dp_ref, dcw_ref, dbin_ref = refs[2 * n_sq:2 * n_sq + 3]
        sq_out = refs[2 * n_sq + 3:3 * n_sq + 3]
        sq_wire = refs[3 * n_sq + 3:4 * n_sq + 3]
        de_ref, he_ref, dh0_ref, sh_ref, acc_ref, wire_ref, sq_sem, wire_sem = refs[4 * n_sq + 3:]
        i = pl.program_id(0)

        @pl.when(i == 0)
        def _():
            dcw_ref[...] = jnp.zeros_like(dcw_ref)
            dbin_ref[...] = jnp.zeros_like(dbin_ref)
            acc_ref[...] = jnp.zeros_like(acc_ref)

        @pl.when(i % span == span - 1)
        def _():
            for a in range(n_sq):
                acc_ref[a] += _dot_tn(sq_in[2 * a][...], sq_in[2 * a + 1][...])

        keep_next = jnp.where(i % tiles_per_seq == tiles_per_seq - 1, 0.0, 1.0)
        de_ref[0:tm, :] = dh1_ref[...]
        de_ref[tm:, :] = dnext_ref[...] * keep_next
        _conv_taps(de_ref, sh_ref, cw_ref, CONV_K - 1, -1, None, dh0_ref, tm)

        keep_prev = jnp.where(i % tiles_per_seq == 0, 0.0, 1.0)
        sg = _sigmoid(p_ref[:, D:2 * D])
        val = p_ref[:, 0:D]
        he_ref[0:HALO, :] = ph_ref[:, 0:D] * _sigmoid(ph_ref[:, D:2 * D]) * keep_prev
        he_ref[HALO:, :] = val * sg
        _conv_weight_grad(dh1_ref, he_ref, sh_ref, HALO - (CONV_K - 1), dcw_ref, tm)
        dcw_ref[SUBLANES * CONV_K:, :] += jnp.sum(dh1_ref[...].reshape(tm // SUBLANES, SUBLANES, D), axis=0)

        dh0 = dh0_ref[...]
        dval = dh0 * sg
        dglu = dval * val * (1.0 - sg)
        dbin_ref[0:1, 0:D] += _colsum(dval)
        dbin_ref[0:1, D:2 * D] += _colsum(dglu)
        dp_ref[:, 0:D] = dval.astype(BF16)
        dp_ref[:, D:2 * D] = dglu.astype(BF16)

        @pl.when(i == t // tm - 1)
        def _():
            cps = [pltpu.make_async_copy(acc_ref.at[a, pl.ds((2 * j + h) * rows, rows)], sq_out[a].at[h, j],
                                         sq_sem.at[(a * N_CHIPS + j) * 2 + h])
                   for a in range(n_sq) for j in range(N_CHIPS) for h in range(2)]
            for cp in cps:
                cp.start()
            for a in range(n_sq):
                wire_ref[...] = acc_ref[a].astype(BF16)
                narrow = [pltpu.make_async_copy(wire_ref.at[pl.ds((2 * j + h) * rows, rows)], sq_wire[a].at[h, j],
                                                wire_sem.at[2 * j + h]) for j in range(N_CHIPS) for h in range(2)]
                for cp in narrow:
                    cp.start()
                for cp in narrow:
                    cp.wait()
            for cp in cps:
                cp.wait()

    any_spec = pl.BlockSpec(memory_space=pl.ANY)
    wide = pl.BlockSpec((span * tm, D), lambda i: (i // span, 0))
    return pl.pallas_call(
        body, name="conv_backward", grid=(t // tm,),
        in_specs=[pl.BlockSpec((tm, D), lambda i: (i, 0)),
                  pl.BlockSpec((HALO, D), lambda i: (jnp.minimum((i + 1) * hb, last), 0)),
                  pl.BlockSpec((tm, 2 * D), lambda i: (i, 0)),
                  pl.BlockSpec((HALO, 2 * D), lambda i: (jnp.maximum(i * hb - 1, 0), 0)),
                  _full((HALO, D)), any_spec] + [wide] * (2 * n_sq),
        out_specs=[pl.BlockSpec((tm, 2 * D), lambda i: (i, 0)), _full((SUBLANES * HALO, D)), _full((8, 2 * D))]
        + [any_spec] * (2 * n_sq),
        out_shape=[jax.ShapeDtypeStruct(dp.shape, BF16), jax.ShapeDtypeStruct((SUBLANES * HALO, D), F32),
                   jax.ShapeDtypeStruct((8, 2 * D), F32)]
        + [jax.ShapeDtypeStruct((2, N_CHIPS, rows, D), F32)] * n_sq
        + [jax.ShapeDtypeStruct((2, N_CHIPS, rows, D), BF16)] * n_sq,
        scratch_shapes=[pltpu.VMEM((tm + HALO, D), F32), pltpu.VMEM((tm + HALO, D), F32), pltpu.VMEM((tm, D), F32),
                        pltpu.VMEM((SUBLANES - 1, tm + SHIFT_ROWS, GROUP_W), F32), pltpu.VMEM((n_sq, D, D), F32),
                        pltpu.VMEM((D, D), BF16), pltpu.SemaphoreType.DMA((n_sq * N_CHIPS * 2,)),
                        pltpu.SemaphoreType.DMA((N_CHIPS * 2,))],
        input_output_aliases={5: 0},
        compiler_params=_params("arbitrary"),
    )(dh1, dh1, p, p, convw, dp, *[a for pair in pairs for a in pair])


def _grad_in_and_x(xt, dp, w4, dr, wires, grads):
    t = dr.shape[0]
    tm = TOKEN_TILE
    half, tn = D // 2, 512
    nb = W_BLOCK // tn
    n_w, n_x = 2 * N_CHIPS * nb, t // tm
    ns = len(grads)
    xi, yi, ci = lax.axis_index("x"), lax.axis_index("y"), lax.axis_index("c")
    others = [2 * (1 - xi) + yi, 2 * xi + (1 - yi), 2 * (1 - xi) + (1 - yi)]
    blocks = others + others + [2 * xi + yi] * 2
    halves = [1 - ci] * 3 + [ci] * 3 + [1 - ci, ci]
    table = jnp.stack([jnp.stack([b * nb + n for b in blocks for n in range(nb)]),
                       jnp.stack([h for h in halves for _ in range(nb)])]).astype(jnp.int32)

    def body(tab_ref, xt_ref, dpc_ref, dpr_ref, w_ref, dr_ref, *refs):
        parts, fulls = refs[:ns], refs[ns:2 * ns]
        dx_ref, qk_ref, b2_ref, b1_ref, wire_ref = refs[2 * ns:2 * ns + 5]
        lands, sibs = refs[2 * ns + 5:3 * ns + 5], refs[3 * ns + 5:4 * ns + 5]
        (g_ref, st_ref, sb_ref, tmp_ref, d2d_send, d2d_recv, ici_send, ici_recv, own_sem, tmp_sem, wire_sem,
         p_send, p_recv, s_send, s_recv) = refs[4 * ns + 5:]
        s = pl.program_id(0)
        x_, y_, c, k = _position()
        chips = _other_chips(x_, y_)
        n = s % nb
        grp = s // nb
        cols = pl.ds(pl.multiple_of(n * tn, tn), tn)

        def part(a, r, core):
            cx, cy = chips[r]
            return pltpu.make_async_remote_copy(
                src_ref=parts[a].at[core, 2 * cx + cy], dst_ref=lands[a].at[2 * r + c],
                send_sem=p_send.at[6 * a + 2 * r + core], recv_sem=p_recv.at[6 * a + 2 * r + c],
                device_id=(cx, cy, core), device_id_type=MESH)

        def landed(a, r, core):
            cx, cy = chips[r]
            return pltpu.make_async_remote_copy(
                src_ref=lands[a].at[2 * r + core], dst_ref=lands[a].at[2 * r + core],
                send_sem=p_send.at[6 * a + 2 * r + core], recv_sem=p_recv.at[6 * a + 2 * r + core],
                device_id=(cx, cy, core), device_id_type=MESH)

        def to_sibling_whole(a):
            return pltpu.make_async_remote_copy(
                src_ref=fulls[a].at[1 - c, k], dst_ref=sibs[a], send_sem=s_send.at[a], recv_sem=s_recv.at[a],
                device_id=(x_, y_, 1 - c), device_id_type=MESH)

        def to_sibling(slot, land):
            return pltpu.make_async_remote_copy(
                src_ref=st_ref.at[slot], dst_ref=b1_ref.at[land, :, cols], send_sem=d2d_send.at[slot],
                recv_sem=d2d_recv.at[land * nb + n], device_id=(x_, y_, 1 - c), device_id_type=MESH)

        def to_chip(r):
            cx, cy = chips[r]
            return pltpu.make_async_remote_copy(
                src_ref=wire_ref.at[r, :, cols], dst_ref=b2_ref.at[r, :, cols], send_sem=ici_send.at[r],
                recv_sem=ici_recv.at[r], device_id=(cx, cy, c), device_id_type=MESH)

        def all_of_chip(r):
            cx, cy = chips[r]
            return pltpu.make_async_remote_copy(
                src_ref=wire_ref.at[r], dst_ref=b2_ref.at[r], send_sem=ici_send.at[r],
                recv_sem=ici_recv.at[r], device_id=(cx, cy, c), device_id_type=MESH)

        def to_result(slot):
            return pltpu.make_async_copy(st_ref.at[slot], qk_ref.at[:, cols], own_sem.at[slot])

        def sibling_piece(land):
            return pltpu.make_async_copy(b1_ref.at[land, :, cols], tmp_ref, tmp_sem)

        @pl.when(s == 0)
        def _():
            for a in range(ns):
                to_sibling_whole(a).start()
                for r in range(3):
                    for core in range(2):
                        part(a, r, core).start()

        own_half = ((grp >= 3) & (grp <= 5)) | (grp == 7)
        land = jnp.where(grp == 7, 3, grp - 3)

        @pl.when(own_half)
        def _():
            to_sibling(0, land).wait_recv()
            sibling_piece(land).start()

        @pl.when(s < n_w)
        def _():
            g_ref[...] = _dot(xt_ref[tab_ref[1, s]], dpc_ref[...])

        @pl.when(own_half)
        def _():
            sibling_piece(land).wait()

        for g in range(2 * N_CHIPS):
            @pl.when(grp == g)
            def _(g=g):
                if g in (0, 1, 2, 6):
                    use = s if g < 3 else 3 * nb + n
                    slot = use % 2

                    @pl.when(use >= 2)
                    def _():
                        to_sibling(slot, 0).wait_send()

                    st_ref[slot] = g_ref[...]
                    to_sibling(slot, min(g, 3)).start()
                elif g in (3, 4, 5):
                    sb_ref[...] = (g_ref[...] + tmp_ref[...]).astype(BF16)
                    stage = pltpu.make_async_copy(sb_ref, wire_ref.at[g - 3, :, cols], wire_sem)
                    stage.start()
                    stage.wait()
                    to_chip(g - 3).start()
                else:
                    slot = n % 2
                    piece = g_ref[...] + tmp_ref[...]

                    @pl.when(n < 2)
                    def _():
                        to_sibling(slot, 0).wait_send()

                    @pl.when(n >= 2)
                    def _():
                        to_result(slot).wait()

                    st_ref[slot] = piece
                    to_result(slot).start()

        @pl.when(s >= n_w)
        def _():
            acc = ALPHA * dr_ref[...]
            for j in range(N_CHIPS):
                acc = acc + _dot_nt(dpr_ref[:, j * W_BLOCK:(j + 1) * W_BLOCK], w_ref[j])
            dx_ref[...] = acc

        @pl.when(s == n_w + n_x - 1)
        def _():
            for slot in range(2):
                to_result(slot).wait()
            for r in range(3):
                all_of_chip(r).wait_recv()
                all_of_chip(r).wait_send()
            for a in range(ns):
                to_sibling_whole(a).wait_recv()
                to_sibling_whole(a).wait_send()
                for r in range(3):
                    for core in range(2):
                        landed(a, r, core).wait_recv()
                        part(a, r, core).wait_send()

    any_spec = pl.BlockSpec(memory_space=pl.ANY)
    tile = lambda s, tab: (jnp.maximum(s - n_w, 0), 0)
    return pl.pallas_call(
        body, name="grad_in_and_x",
        grid_spec=pltpu.PrefetchScalarGridSpec(
            num_scalar_prefetch=1, grid=(n_w + n_x,),
            in_specs=[pl.BlockSpec((2, half, t), lambda s, tab: (0, 0, 0), pipeline_mode=pl.Buffered(1)),
                      pl.BlockSpec((t, tn), lambda s, tab: (0, tab[0, jnp.minimum(s, n_w - 1)])),
                      pl.BlockSpec((tm, D_IN), tile),
                      pl.BlockSpec((N_CHIPS, D, W_BLOCK), lambda s, tab: (0, 0, 0), pipeline_mode=pl.Buffered(1)),
                      pl.BlockSpec((tm, D), tile)] + [any_spec] * (2 * ns),
            out_specs=[pl.BlockSpec((tm, D), tile)] + [any_spec] * (4 + 2 * ns),
            scratch_shapes=[pltpu.VMEM((half, tn), F32), pltpu.VMEM((2, half, tn), F32), pltpu.VMEM((half, tn), BF16),
                            pltpu.VMEM((half, tn), F32),
                            pltpu.SemaphoreType.DMA((2,)), pltpu.SemaphoreType.DMA((N_CHIPS * nb,)),
                            pltpu.SemaphoreType.DMA((3,)), pltpu.SemaphoreType.DMA((3,)),
                            pltpu.SemaphoreType.DMA((2,)), pltpu.SemaphoreType.DMA, pltpu.SemaphoreType.DMA,
                            pltpu.SemaphoreType.DMA((6 * ns,)), pltpu.SemaphoreType.DMA((6 * ns,)),
                            pltpu.SemaphoreType.DMA((ns,)), pltpu.SemaphoreType.DMA((ns,))]),
        out_shape=[jax.ShapeDtypeStruct((t, D), F32), jax.ShapeDtypeStruct((half, W_BLOCK), F32),
                   jax.ShapeDtypeStruct((3, half, W_BLOCK), BF16), jax.ShapeDtypeStruct((N_CHIPS, half, W_BLOCK), F32),
                   jax.ShapeDtypeStruct((3, half, W_BLOCK), BF16)]
        + [jax.ShapeDtypeStruct((6,) + w.shape[2:], w.dtype) for w in wires]
        + [jax.ShapeDtypeStruct(g.shape[2:], F32) for g in grads],
        compiler_params=_params("arbitrary"),
    )(table, xt, dp, dp, w4, dr, *wires, *grads)


def kernel(x, w_in, b_in, conv_w, conv_b, gn_g, gn_b, ln_v_g, ln_v_b, w_spatial, b_spatial, w_pa, w_pb, w_o, b_o, ln_out_g, ln_out_b, loss_target, m_w_in, m_b_in, m_conv_w, m_conv_b, m_gn_g, m_gn_b, m_ln_v_g, m_ln_v_b, m_w_spatial, m_b_spatial, m_w_pa, m_w_pb, m_w_o, m_b_o, m_ln_out_g, m_ln_out_b, v_w_in, v_b_in, v_conv_w, v_conv_b, v_gn_g, v_gn_b, v_ln_v_g, v_ln_v_b, v_w_spatial, v_b_spatial, v_w_pa, v_w_pb, v_w_o, v_b_o, v_ln_out_g, v_ln_out_b):
    n_seq, seq, _ = x.shape
    t = n_seq * seq
    tiles_per_seq = seq // TOKEN_TILE
    x2 = x.reshape(t, D)
    tgt = loss_target.reshape(t, D)

    conv_shard = jnp.pad(conv_w, ((0, HALO - CONV_K), (0, 0)))
    p, win4, wpa4, wpb4, wo4, conv4 = _proj_gather(
        x2, b_in,
        _place_shards([w_in, w_pa, w_pb, w_o, conv_shard], [BF16, BF16, BF16, BF16, F32]))
    win4 = win4.reshape(N_CHIPS, D, W_BLOCK)
    wpa, wpb, wo = wpa4.reshape(D, D), wpb4.reshape(D, D), wo4.reshape(D, D)
    convw = conv4.reshape(N_CHIPS, HALO, D // N_CHIPS).transpose(1, 0, 2).reshape(HALO, D)

    vecs = jnp.stack([conv_b, gn_g, gn_b, ln_v_g, ln_v_b, b_o, ln_out_g, ln_out_b])
    causal = jnp.tril(jnp.ones((CHUNK, CHUNK), bool))
    ws = jnp.where(causal[None], w_spatial, 0.0)
    ws_bf, wst_bf = ws.astype(BF16), ws.transpose(0, 2, 1).astype(BF16)
    bsp = jnp.repeat(b_spatial.T, GROUP_W, axis=1)

    h1, ya, yb, h3, s, mixed, dr, drb, xt, acc_f = _forward_tiles(p, x2, tgt, wpa, wpb, wo, convw, vecs, ws_bf, bsp, tiles_per_seq)
    dh1, dp, dya, dyb, acc_b, dbin_b, dws, dbsp_acc = _backward_tiles(p, h1, ya, yb, drb, wpa, wpb, wo, vecs, ws_bf, wst_bf, bsp)
    dp, dcw8, dbin_a, *square = _conv_backward(dh1, p, dp, convw, [(h3, dya), (s, dyb), (mixed, drb)], tiles_per_seq)

    small, g_conv = _pack_small(acc_f, acc_b, dbin_a, dbin_b, dcw8, dws, dbsp_acc)
    small = small.reshape(2, N_CHIPS, SMALL_ROWS // 8, D)

    grads = square[:3] + [g_conv, small]
    wires = square[3:] + [g_conv, small]
    grad_x, q_in, chips_in, _, _, *landed = _grad_in_and_x(xt.reshape(2, D // 2, t), dp, win4, dr, wires, grads)
    grad_x = grad_x.reshape(x.shape)
    mine = [_add_chips(q_in, chips_in)]
    mine += _add_devices(grads, landed[:5], landed[5:])
    *full, small_parts = _share_results(mine[:5], mine[5])
    grad_w_in, grad_w_pa, grad_w_pb, grad_w_o = [f.reshape(w.shape) for f, w in zip(full[:4], (w_in, w_pa, w_pb, w_o))]
    grad_conv_w = full[4].reshape(HALO, D // N_CHIPS)[:CONV_K]

    big = {"w_in": (grad_w_in,) + tuple(_adamw(w_in, grad_w_in, m_w_in, v_w_in))}
    group = _adamw_group([(w_pa, grad_w_pa, m_w_pa, v_w_pa), (w_pb, grad_w_pb, m_w_pb, v_w_pb),
                          (w_o, grad_w_o, m_w_o, v_w_o)], (conv_w, grad_conv_w, m_conv_w, v_conv_w))
    for name, g, res in zip(["w_pa", "w_pb", "w_o", "conv_w"], [grad_w_pa, grad_w_pb, grad_w_o, grad_conv_w], group):
        big[name] = (g,) + tuple(res)
    vec_names = ["conv_b", "gn_g", "gn_b", "ln_v_g", "ln_v_b", "b_o", "ln_out_g", "ln_out_b"]
    vec_triples = [(conv_b, m_conv_b, v_conv_b), (gn_g, m_gn_g, v_gn_g), (gn_b, m_gn_b, v_gn_b),
                   (ln_v_g, m_ln_v_g, v_ln_v_g), (ln_v_b, m_ln_v_b, v_ln_v_b), (b_o, m_b_o, v_b_o),
                   (ln_out_g, m_ln_out_g, v_ln_out_g), (ln_out_b, m_ln_out_b, v_ln_out_b)]
    small_res, loss8 = _adamw_small(
        small_parts.reshape(8, SMALL_ROWS // 8, D), vec_triples, (b_in, m_b_in, v_b_in),
        (w_spatial, m_w_spatial, v_w_spatial), (b_spatial, m_b_spatial, v_b_spatial))
    per_name = dict(zip(vec_names + ["b_in", "w_spatial", "b_spatial"], small_res))

    order = ["w_in", "b_in", "conv_w", "conv_b", "gn_g", "gn_b", "ln_v_g", "ln_v_b", "w_spatial", "b_spatial",
             "w_pa", "w_pb", "w_o", "b_o", "ln_out_g", "ln_out_b"]
    outs = [loss8[0, 0], grad_x]
    for kind in range(4):
        outs += [big[n][kind] if n in big else per_name[n][kind] for n in order]
    return tuple(outs)
```

```python
import math

import jax
import jax.numpy as jnp
from jax import lax
from jax.experimental import pallas as pl
from jax.experimental.pallas import tpu as pltpu

D = 1024
N_GROUPS = 8
GROUP_W = D // N_GROUPS
CHUNK = 128
CONV_K = 31
HALO = 32
D_IN = 8 * D
N_CHIPS = 4
W_BLOCK = D_IN // N_CHIPS
ALPHA = 2.0 ** 0.25
LN_EPS = 1e-5
ADAM_LR, ADAM_B1, ADAM_B2, ADAM_EPS, ADAM_WD, ADAM_STEP = 0.001, 0.9, 0.999, 1e-08, 0.01, 10

TOKEN_TILE = 256
VMEM_LIMIT = 56 * 1024 * 1024
MESH = pl.DeviceIdType.MESH
F32, BF16 = jnp.float32, jnp.bfloat16


def _sigmoid(x):
    return jax.nn.sigmoid(x)


GELU_C = math.sqrt(2.0 / math.pi)
GELU_CA = GELU_C * 0.044715


def _gelu(x):
    t = jnp.tanh(x * (GELU_C + GELU_CA * (x * x)))
    return x * (0.5 + 0.5 * t)


def _gelu_and_grad(x):
    x2 = x * x
    t = jnp.tanh(x * (GELU_C + GELU_CA * x2))
    cdf = 0.5 + 0.5 * t
    return x * cdf, cdf + (0.5 * x) * (1.0 - t * t) * (GELU_C + (3.0 * GELU_CA) * x2)


def _norm_stats(v):
    mu = jnp.mean(v, axis=-1, keepdims=True)
    vc = v - mu
    var = jnp.mean(vc * vc, axis=-1, keepdims=True)
    rstd = lax.rsqrt(var + LN_EPS)
    return vc * rstd, rstd


def _norm_bwd(dxhat, xhat, rstd):
    m1 = jnp.mean(dxhat, axis=-1, keepdims=True)
    m2 = jnp.mean(dxhat * xhat, axis=-1, keepdims=True)
    return rstd * (dxhat - m1 - xhat * m2)


def _dot(a, b):
    return jnp.dot(a, b, preferred_element_type=F32)


def _dot_nt(a, b):
    return lax.dot_general(a, b, (((1,), (1,)), ((), ())), preferred_element_type=F32)


def _dot_tn(a, b):
    return lax.dot_general(a, b, (((0,), (0,)), ((), ())), preferred_element_type=F32)


def _colsum(v):
    return jnp.sum(v, axis=0, keepdims=True)


def _full(shape):
    return pl.BlockSpec(shape, lambda *_: (0,) * len(shape))


def _resident(shape):
    return pl.BlockSpec(shape, lambda *_: (0,) * len(shape), pipeline_mode=pl.Buffered(1))


def _params(*sem):
    return pltpu.CompilerParams(dimension_semantics=sem, vmem_limit_bytes=VMEM_LIMIT)


def _chip_index():
    return (2 * lax.axis_index("x") + lax.axis_index("y")).astype(jnp.int32).reshape(1)


def _core_index():
    return lax.axis_index("c").astype(jnp.int32).reshape(1)


def _place_shards(ws, dtypes):
    n = len(ws)

    def body(k_ref, *refs):
        for w_ref, o_ref, dtype in zip(refs[:n], refs[n:], dtypes):
            rows = w_ref.shape[0] // 2
            for h in range(2):
                o_ref[h] = w_ref[h * rows:(h + 1) * rows, :].astype(dtype)

    return pl.pallas_call(
        body, name="place_shards",
        grid_spec=pltpu.PrefetchScalarGridSpec(
            num_scalar_prefetch=1, grid=(1,),
            in_specs=[pl.BlockSpec(w.shape, lambda i, k: (0, 0)) for w in ws],
            out_specs=[pl.BlockSpec((None, 2, w.shape[0] // 2, w.shape[1]), lambda i, k: (k[0], 0, 0, 0)) for w in ws]),
        out_shape=[jax.ShapeDtypeStruct((N_CHIPS, 2, w.shape[0] // 2, w.shape[1]), dt) for w, dt in zip(ws, dtypes)],
        compiler_params=_params("arbitrary"),
    )(_chip_index(), *ws)


def _position():
    x, y, c = lax.axis_index("x"), lax.axis_index("y"), lax.axis_index("c")
    return x, y, c, 2 * x + y


def _other_chips(x, y):
    return [(1 - x, y), (x, 1 - y), (1 - x, 1 - y)]


def _any_specs(n):
    return [pl.BlockSpec(memory_space=pl.ANY)] * n


def _proj_gather(x, b_in, bufs):
    t = x.shape[0]
    tm = 1024
    steps = t // tm
    ahead = steps // 4
    half = D // 2
    chunk = W_BLOCK // 2
    n = len(bufs)
    xi, yi = lax.axis_index("x"), lax.axis_index("y")
    chips = [2 * xi + yi, 2 * (1 - xi) + yi, 2 * xi + (1 - yi), 2 * (1 - xi) + (1 - yi)]
    plan = [(0, 0), (0, 1), (1, 0), (2, 1), (1, 1), (2, 0), (3, 0), (3, 1)]
    order = jnp.stack([2 * chips[ch] + q for ch, q in plan]).astype(jnp.int32)

    def body(order_ref, x_ref, b_ref, *refs):
        p_ref, outs = refs[n], refs[n + 1:2 * n + 1]
        xb_ref, w_ref, lsem, send, recv, hop_send, hop_recv, fsend, frecv, qsend, qrecv = refs[2 * n + 1:]
        jj, i = pl.program_id(0), pl.program_id(1)
        x_, y_, c, k = _position()
        nbrs = [(1 - x_, y_), (x_, 1 - y_)]
        blocks = [2 * (1 - x_) + y_, 2 * x_ + (1 - y_), 2 * (1 - x_) + (1 - y_)]

        def quarter(a, block, q, h):
            if a == 0:
                return outs[0].at[block, h, :, pl.ds(q * chunk, chunk)]
            rows = outs[a].shape[2] // 2
            return outs[a].at[block, h, pl.ds(q * rows, rows)]

        def copy(ref, to, send_sem, recv_sem):
            return pltpu.make_async_remote_copy(src_ref=ref, dst_ref=ref, send_sem=send_sem, recv_sem=recv_sem,
                                                device_id=(to[0], to[1], c), device_id_type=MESH)

        def sent(a, nb, q):
            return copy(quarter(a, k, q, c), nbrs[nb], send.at[4 * a + 2 * nb + q], recv.at[4 * a + 2 * nb + q])

        def landed(a, nb, q):
            return copy(quarter(a, blocks[nb], q, c), nbrs[nb], send.at[4 * a + 2 * nb + q], recv.at[4 * a + 2 * nb + q])

        def hopped(a, nb):
            return copy(quarter(a, blocks[nb], nb, c), nbrs[1 - nb], hop_send.at[2 * a + nb], hop_recv.at[2 * a + 1 - nb])

        def from_diagonal(a, via):
            return copy(quarter(a, blocks[2], 1 - via, c), nbrs[via], hop_send.at[2 * a + via], hop_recv.at[2 * a + via])

        def passed(a, r, h):
            return pltpu.make_async_remote_copy(
                src_ref=outs[a].at[blocks[r], h], dst_ref=outs[a].at[blocks[r], h], send_sem=fsend.at[3 * a + r],
                recv_sem=frecv.at[3 * a + r], device_id=(x_, y_, 1 - c), device_id_type=MESH)

        def passed_quarter(r, q, h):
            ref = quarter(0, blocks[r], q, h)
            return pltpu.make_async_remote_copy(
                src_ref=ref, dst_ref=ref, send_sem=qsend.at[2 * r + q], recv_sem=qrecv.at[2 * r + q],
                device_id=(x_, y_, 1 - c), device_id_type=MESH)

        def load(block, q, slot):
            return [pltpu.make_async_copy(quarter(0, block, q, h), w_ref.at[slot, pl.ds(h * half, half)],
                                          lsem.at[2 * slot + h]) for h in range(2)]

        def pass_on(arrays):
            for a in arrays:
                for nb in range(2):
                    landed(a, nb, nb).wait_recv()
                    hopped(a, nb).start()

        @pl.when((jj == 0) & (i == 0))
        def _():
            for a in range(n):
                for nb, q in ((0, 0), (1, 1), (0, 1), (1, 0)):
                    sent(a, nb, q).start()
            for cp in load(k, 0, 0):
                cp.start()

        for nxt in range(1, len(plan)):
            @pl.when((jj == nxt - 1) & (i == ahead))
            def _(nxt=nxt):
                ch, q = plan[nxt]
                if ch in (1, 2):
                    landed(0, ch - 1, q).wait_recv()
                    if q == ch - 1:
                        hopped(0, ch - 1).start()
                elif ch == 3:
                    from_diagonal(0, 1 - q).wait_recv()
                if ch:
                    passed_quarter(ch - 1, q, c).start()
                    passed_quarter(ch - 1, q, 1 - c).wait_recv()
                for cp in load(k if ch == 0 else blocks[ch - 1], q, nxt % 2):
                    cp.start()
                if nxt == 5:
                    pass_on(range(1, n))
                    for a in range(1, n):
                        landed(a, 0, 1).wait_recv()
                        passed(a, 0, c).start()
                        landed(a, 1, 0).wait_recv()
                        passed(a, 1, c).start()

        slot = jj % 2

        @pl.when(i == 0)
        def _():
            for cp in load(k, 0, slot):
                cp.wait()

        rows = pl.ds(pl.multiple_of(i * tm, tm), tm)

        @pl.when(jj == 0)
        def _():
            xb_ref[rows, :] = x_ref[...].astype(BF16)

        p_ref[...] = _dot(xb_ref[rows, :], w_ref[slot]) + b_ref[...]

        @pl.when((jj == len(plan) - 1) & (i == steps - 1))
        def _():
            for a in range(1, n):
                from_diagonal(a, 0).wait_recv()
                from_diagonal(a, 1).wait_recv()
                passed(a, 2, c).start()
            for a in range(1, n):
                for r in range(3):
                    passed(a, r, 1 - c).wait_recv()
                    passed(a, r, c).wait_send()
            for r in range(3):
                for q in range(2):
                    passed_quarter(r, q, c).wait_send()
            for a in range(n):
                for nb in range(2):
                    for q in range(2):
                        sent(a, nb, q).wait_send()
                    hopped(a, nb).wait_send()

    any_spec = pl.BlockSpec(memory_space=pl.ANY)
    return pl.pallas_call(
        body, name="proj_gather",
        grid_spec=pltpu.PrefetchScalarGridSpec(
            num_scalar_prefetch=1, grid=(len(plan), steps),
            in_specs=[pl.BlockSpec((tm, D), lambda jj, i, o: (jnp.where(jj == 0, i, steps - 1), 0)),
                      pl.BlockSpec((None, 1, chunk), lambda jj, i, o: (o[jj], 0, 0))] + [any_spec] * n,
            out_specs=[pl.BlockSpec((tm, chunk), lambda jj, i, o: (i, o[jj]))] + [any_spec] * n,
            scratch_shapes=[pltpu.VMEM((t, D), BF16), pltpu.VMEM((2, D, chunk), BF16), pltpu.SemaphoreType.DMA((4,)),
                            pltpu.SemaphoreType.DMA((4 * n,)), pltpu.SemaphoreType.DMA((4 * n,)),
                            pltpu.SemaphoreType.DMA((2 * n,)), pltpu.SemaphoreType.DMA((2 * n,)),
                            pltpu.SemaphoreType.DMA((3 * n,)), pltpu.SemaphoreType.DMA((3 * n,)),
                            pltpu.SemaphoreType.DMA((6,)), pltpu.SemaphoreType.DMA((6,))]),
        out_shape=[jax.ShapeDtypeStruct((t, D_IN), F32)] + [jax.ShapeDtypeStruct(b.shape, b.dtype) for b in bufs],
        input_output_aliases={3 + a: 1 + a for a in range(n)},
        compiler_params=_params("arbitrary", "arbitrary"),
    )(order, x, b_in.reshape(D_IN // chunk, 1, chunk), *bufs)


def _share_results(bufs, small):
    n = len(bufs)

    def body(*refs):
        outs, small_out = refs[n + 1:2 * n + 1], refs[2 * n + 1]
        send, recv, ssend, srecv = refs[2 * n + 2:]
        x, y, c, k = _position()
        cps = []
        for a in range(n):
            cp = pltpu.make_async_remote_copy(
                src_ref=outs[a].at[c], dst_ref=outs[a].at[c], send_sem=send.at[a], recv_sem=recv.at[a],
                device_id=(x, y, 1 - c), device_id_type=MESH)
            cp.start()
            cps.append(cp)
        waits = []
        for p in range(1, 8):
            px, py, pc = x ^ (p >> 2), y ^ ((p >> 1) & 1), c ^ (p & 1)
            cp = pltpu.make_async_remote_copy(
                src_ref=small_out.at[k, c], dst_ref=small_out.at[k, c], send_sem=ssend.at[p - 1],
                recv_sem=srecv.at[p - 1], device_id=(px, py, pc), device_id_type=MESH)
            cp.start()
            cps.append(cp)
            waits.append(pltpu.make_async_remote_copy(
                src_ref=small_out.at[2 * px + py, pc], dst_ref=small_out.at[2 * px + py, pc], send_sem=ssend.at[p - 1],
                recv_sem=srecv.at[p - 1], device_id=(px, py, pc), device_id_type=MESH))
        for a in range(n):
            pltpu.make_async_remote_copy(
                src_ref=outs[a].at[1 - c], dst_ref=outs[a].at[1 - c], send_sem=send.at[a], recv_sem=recv.at[a],
                device_id=(x, y, 1 - c), device_id_type=MESH).wait_recv()
        for w in waits:
            w.wait_recv()
        for cp in cps:
            cp.wait_send()

    return pl.pallas_call(
        body, name="rs_share_results",
        in_specs=_any_specs(n + 1), out_specs=_any_specs(n + 1),
        out_shape=[jax.ShapeDtypeStruct(b.shape, b.dtype) for b in bufs + [small]],
        scratch_shapes=[pltpu.SemaphoreType.DMA((n,)), pltpu.SemaphoreType.DMA((n,)),
                        pltpu.SemaphoreType.DMA((7,)), pltpu.SemaphoreType.DMA((7,))],
        input_output_aliases={a: a for a in range(n + 1)},
    )(*bufs, small)


def _row_tile(r, c):
    t = max(8, min(r, (1 << 18) // c))
    while r % t:
        t //= 2
    return t


def _add_devices(grads, lands, sibs):
    n = len(grads)

    def body(kc_ref, *refs):
        for g_ref, l_ref, s_ref, f_ref in zip(refs[:n], refs[n:2 * n], refs[2 * n:3 * n], refs[3 * n:]):
            f = g_ref[...] + s_ref[...]
            for i in range(l_ref.shape[0]):
                f = f + l_ref[i].astype(F32)
            f_ref[...] = f

    shapes = [g.shape[2:] for g in grads]
    out_specs = [pl.BlockSpec((None,) + sh, lambda i, kc: (kc[1], 0, 0)) for sh in shapes[:-1]]
    out_specs.append(pl.BlockSpec((None, None) + shapes[-1], lambda i, kc: (kc[0], kc[1], 0, 0)))
    out_shape = [jax.ShapeDtypeStruct((2,) + sh, F32) for sh in shapes[:-1]]
    out_shape.append(jax.ShapeDtypeStruct((N_CHIPS, 2) + shapes[-1], F32))
    return pl.pallas_call(
        body, name="rs_add_devices",
        grid_spec=pltpu.PrefetchScalarGridSpec(
            num_scalar_prefetch=1, grid=(1,),
            in_specs=[pl.BlockSpec((None, None) + sh, lambda i, kc: (kc[1], kc[0], 0, 0)) for sh in shapes]
            + [pl.BlockSpec(l.shape, lambda i, kc: (0, 0, 0)) for l in lands]
            + [pl.BlockSpec(sh, lambda i, kc: (0, 0)) for sh in shapes],
            out_specs=out_specs),
        out_shape=out_shape,
        compiler_params=_params("arbitrary"),
    )(jnp.concatenate([_chip_index(), _core_index()]), *grads, *lands, *sibs)


def _add_chips(q, b2):
    r, c = q.shape
    t = _row_tile(r, c)

    def body(c_ref, q_ref, b_ref, f_ref):
        f_ref[...] = ((q_ref[...] + b_ref[0].astype(F32)) + b_ref[1].astype(F32)) + b_ref[2].astype(F32)

    return pl.pallas_call(
        body, name="rs_add_chips",
        grid_spec=pltpu.PrefetchScalarGridSpec(
            num_scalar_prefetch=1, grid=(r // t,),
            in_specs=[pl.BlockSpec((t, c), lambda i, cr: (i, 0)), pl.BlockSpec((3, t, c), lambda i, cr: (0, i, 0))],
            out_specs=pl.BlockSpec((None, t, c), lambda i, cr: (cr[0], i, 0))),
        out_shape=jax.ShapeDtypeStruct((2, r, c), F32),
        compiler_params=_params("parallel"),
    )(_core_index(), q, b2)


def _adamw_math(w, g, m, v):
    m = ADAM_B1 * m + (1.0 - ADAM_B1) * g
    v = ADAM_B2 * v + (1.0 - ADAM_B2) * (g * g)
    m_hat = m / (1.0 - ADAM_B1 ** ADAM_STEP)
    v_hat = v / (1.0 - ADAM_B2 ** ADAM_STEP)
    delta = -ADAM_LR * (m_hat / (jnp.sqrt(v_hat) + ADAM_EPS) + ADAM_WD * w)
    return delta, m, v


def _adamw_group(quads, conv):
    n = len(quads)
    r, c = quads[0][0].shape

    def body(*refs):
        ins, outs = refs[:4 * (n + 1)], refs[4 * (n + 1):]
        for i in range(n + 1):
            w_ref, g_ref, m_ref, v_ref = ins[4 * i:4 * i + 4]
            res = _adamw_math(w_ref[...], g_ref[...], m_ref[...], v_ref[...])
            for o_ref, val in zip(outs[3 * i:3 * i + 3], res):
                o_ref[...] = val

    half = pl.BlockSpec((r // 2, c), lambda i: (i, 0))
    whole = pl.BlockSpec(conv[0].shape, lambda i: (0, 0))
    res = pl.pallas_call(
        body, name="adamw_group", grid=(2,),
        in_specs=[half] * (4 * n) + [whole] * 4, out_specs=[half] * (3 * n) + [whole] * 3,
        out_shape=[jax.ShapeDtypeStruct((r, c), F32)] * (3 * n) + [jax.ShapeDtypeStruct(conv[0].shape, F32)] * 3,
        compiler_params=_params("arbitrary"),
    )(*[a for quad in quads for a in quad], *conv)
    return [res[3 * i:3 * i + 3] for i in range(n + 1)]


def _adamw(w, g, m, v):
    r, c = w.shape
    t = _row_tile(r, c) if r % 8 == 0 else r

    def body(w_ref, g_ref, m_ref, v_ref, d_ref, nm_ref, nv_ref):
        d_ref[...], nm_ref[...], nv_ref[...] = _adamw_math(w_ref[...], g_ref[...], m_ref[...], v_ref[...])

    spec = pl.BlockSpec((t, c), lambda i: (i, 0))
    return pl.pallas_call(
        body, name="adamw", grid=(r // t,), in_specs=[spec] * 4, out_specs=[spec] * 3,
        out_shape=[jax.ShapeDtypeStruct((r, c), F32)] * 3, compiler_params=_params("parallel"),
    )(w, g, m, v)


ROW_B_IN = 0
ROW_VECS = 8
ROW_LOSS = 16
ROW_B_SPATIAL = 24
ROW_W_SPATIAL = 32
SMALL_ROWS = 192
N_VECS = 8


def _pack_small(acc_f, acc_b, dbin_a, dbin_b, dcw8, dws, dbsp):
    cols = D // N_CHIPS

    def body(af_ref, ab_ref, da_ref, db_ref, cw_ref, ws_ref, bs_ref, o_ref, gc_ref):
        o_ref[...] = jnp.zeros_like(o_ref)
        for j in range(D_IN // D):
            src = da_ref if j < 2 else db_ref
            o_ref[ROW_B_IN + j:ROW_B_IN + j + 1, :] = src[0:1, j * D:(j + 1) * D]
        dcw = jnp.sum(cw_ref[...].reshape(HALO, SUBLANES, D), axis=1)
        o_ref[ROW_VECS:ROW_VECS + 1, :] = dcw[CONV_K:CONV_K + 1]
        o_ref[ROW_VECS + 1:ROW_VECS + 5, :] = ab_ref[0:4, :]
        o_ref[ROW_VECS + 5:ROW_VECS + 6, :] = af_ref[2:3, :]
        o_ref[ROW_VECS + 6:ROW_VECS + 8, :] = af_ref[0:2, :]
        o_ref[ROW_LOSS:ROW_LOSS + 1, :] = af_ref[3:4, :]
        head = lax.broadcasted_iota(jnp.int32, (N_GROUPS, D), 0)
        lane = lax.broadcasted_iota(jnp.int32, (N_GROUPS, D), 1)
        indicator = jnp.where(lane // GROUP_W == head, 1.0, 0.0)
        o_ref[ROW_B_SPATIAL:ROW_B_SPATIAL + N_GROUPS, 0:CHUNK] = lax.dot_general(
            indicator, bs_ref[...], (((1,), (1,)), ((), ())), precision=lax.Precision.HIGHEST, preferred_element_type=F32)
        t_idx = lax.broadcasted_iota(jnp.int32, (CHUNK, D), 0)
        s_idx = lax.broadcasted_iota(jnp.int32, (CHUNK, D), 1) % CHUNK
        o_ref[ROW_W_SPATIAL:ROW_W_SPATIAL + CHUNK, :] = jnp.where(s_idx <= t_idx, ws_ref[...], 0.0)
        for h in range(2):
            for j in range(N_CHIPS):
                gc_ref[h, j] = dcw[h * (HALO // 2):(h + 1) * (HALO // 2), j * cols:(j + 1) * cols]

    ins = [acc_f, acc_b, dbin_a, dbin_b, dcw8, dws, dbsp]
    return pl.pallas_call(
        body, name="pack_small",
        in_specs=[_full(a.shape) for a in ins],
        out_specs=[_full((SMALL_ROWS, D)), _full((2, N_CHIPS, HALO // 2, cols))],
        out_shape=[jax.ShapeDtypeStruct((SMALL_ROWS, D), F32), jax.ShapeDtypeStruct((2, N_CHIPS, HALO // 2, cols), F32)],
        compiler_params=_params(),
    )(*ins)


def _adamw_small(parts, vecs, b_in, w_spatial, b_spatial):
    triples = list(vecs) + [b_in, w_spatial, b_spatial]
    n_in = 3 * len(triples)

    def body(p_ref, *refs):
        ins = [refs[3 * i:3 * i + 3] for i in range(len(triples))]
        outs = [refs[n_in + 4 * i:n_in + 4 * i + 4] for i in range(len(triples))]
        loss_ref, g_ref = refs[n_in + 4 * len(triples):]
        rows = SMALL_ROWS // 8
        for k in range(N_CHIPS):
            for core in range(2):
                g_ref[(core * N_CHIPS + k) * rows:(core * N_CHIPS + k + 1) * rows, :] = p_ref[2 * k + core]

        def step(g, wmv, out, get, put):
            d, nm, nv = _adamw_math(get(wmv[0]), g, get(wmv[1]), get(wmv[2]))
            for o, val in zip(out, (g, d, nm, nv)):
                put(o, val)

        for i in range(N_VECS):
            step(g_ref[ROW_VECS + i:ROW_VECS + i + 1, :], ins[i], outs[i],
                 lambda r: r[...].reshape(1, D), lambda o, val: o.__setitem__(Ellipsis, val.reshape(D)))
        for j in range(D_IN // D):
            piece = pl.ds(j * D, D)
            step(g_ref[ROW_B_IN + j:ROW_B_IN + j + 1, :], ins[N_VECS], outs[N_VECS],
                 lambda r: r[piece].reshape(1, D), lambda o, val: o.__setitem__(piece, val.reshape(D)))
        for h in range(N_GROUPS):
            step(g_ref[ROW_W_SPATIAL:ROW_W_SPATIAL + CHUNK, h * CHUNK:(h + 1) * CHUNK], ins[N_VECS + 1], outs[N_VECS + 1],
                 lambda r: r[h], lambda o, val: o.__setitem__(h, val))
        step(g_ref[ROW_B_SPATIAL:ROW_B_SPATIAL + N_GROUPS, 0:CHUNK], ins[N_VECS + 2], outs[N_VECS + 2],
             lambda r: r[...], lambda o, val: o.__setitem__(Ellipsis, val))
        lanes = g_ref[ROW_LOSS:ROW_LOSS + 1, :]
        loss_ref[...] = jnp.broadcast_to(jnp.sum(lanes, axis=1, keepdims=True), (8, 128))

    flat = [a for tr in triples for a in tr]
    out_shape = [jax.ShapeDtypeStruct(tr[0].shape, F32) for tr in triples for _ in range(4)]
    out_shape.append(jax.ShapeDtypeStruct((8, 128), F32))
    res = pl.pallas_call(
        body, name="adamw_small",
        in_specs=[_full(parts.shape)] + [_full(a.shape) for a in flat],
        out_specs=[_full(o.shape) for o in out_shape],
        out_shape=out_shape,
        scratch_shapes=[pltpu.VMEM((SMALL_ROWS, D), F32)],
        compiler_params=_params(),
    )(parts, *flat)
    return [res[4 * i:4 * i + 4] for i in range(len(triples))], res[-1]


SUBLANES = 8
SHIFT_ROWS = HALO - SUBLANES


def _shifted_copies(src_ref, sh_ref, cs, tm):
    for p in range(1, SUBLANES):
        sh_ref[p - 1] = src_ref[pl.ds(p, tm + SHIFT_ROWS), cs]


def _tap(src_ref, sh_ref, cs, offset, start, rows):
    p, q = offset % SUBLANES, offset // SUBLANES
    if p == 0:
        return src_ref[pl.ds(start + SUBLANES * q, rows), cs]
    return sh_ref[p - 1, pl.ds(start + SUBLANES * q, rows), :]


def _conv_taps(src_ref, sh_ref, w_ref, first_offset, step, bias, dst_ref, tm):
    rows = 64
    for g in range(N_GROUPS):
        cs = slice(g * GROUP_W, (g + 1) * GROUP_W)
        _shifted_copies(src_ref, sh_ref, cs, tm)
        for rb in range(tm // rows):
            acc = jnp.zeros((rows, GROUP_W), F32) + (bias[:, cs] if bias is not None else 0.0)
            for k in range(CONV_K):
                acc = acc + w_ref[k:k + 1, cs] * _tap(src_ref, sh_ref, cs, first_offset + step * k, rb * rows, rows)
            dst_ref[rb * rows:(rb + 1) * rows, cs] = acc


def _conv_weight_grad(d_ref, src_ref, sh_ref, first_offset, acc_ref, tm):
    rows = 64
    for g in range(N_GROUPS):
        cs = slice(g * GROUP_W, (g + 1) * GROUP_W)
        _shifted_copies(src_ref, sh_ref, cs, tm)
        for rb in range(tm // rows):
            d = d_ref[rb * rows:(rb + 1) * rows, cs]
            for k in range(CONV_K):
                prod = d * _tap(src_ref, sh_ref, cs, first_offset + k, rb * rows, rows)
                acc_ref[SUBLANES * k:SUBLANES * (k + 1), cs] += jnp.sum(
                    prod.reshape(rows // SUBLANES, SUBLANES, GROUP_W), axis=0)


def _spatial_mix(w_ref, v_bf, tm):
    rows = []
    for q in range(tm // CHUNK):
        cols = [_dot(w_ref[h], v_bf[q * CHUNK:(q + 1) * CHUNK, h * GROUP_W:(h + 1) * GROUP_W])
                for h in range(N_GROUPS)]
        rows.append(jnp.concatenate(cols, axis=1))
    return jnp.concatenate(rows, axis=0)


def _group_norm_fwd(h1, gn_g, gn_b):
    xhat, rstd = [], []
    for g in range(N_GROUPS):
        xh, rs = _norm_stats(h1[:, g * GROUP_W:(g + 1) * GROUP_W])
        xhat.append(xh)
        rstd.append(rs)
    xhat = jnp.concatenate(xhat, axis=1)
    return xhat * gn_g + gn_b, xhat, rstd


def _forward_tiles(p, x, tgt, wpa, wpb, wo, convw, vecs, ws, bsp, tiles_per_seq):
    t = x.shape[0]
    tm = TOKEN_TILE
    hb = tm // HALO

    def body(p_ref, ph_ref, x_ref, t_ref, wpa_ref, wpb_ref, wo_ref, cw_ref, vec_ref, ws_ref, bsp_ref,
             h1_ref, ya_ref, yb_ref, h3_ref, s_ref, mx_ref, dr_ref, drb_ref, xt_ref, acc_ref, he_ref, sh_ref):
        i = pl.program_id(0)
        xt_ref[...] = x_ref[...].T.astype(BF16)
        conv_b, gn_g, gn_b, lnv_g, lnv_b, b_o, lno_g, lno_b = [vec_ref[j:j + 1, :] for j in range(8)]

        keep = jnp.where(i % tiles_per_seq == 0, 0.0, 1.0)
        he_ref[0:HALO, :] = ph_ref[:, 0:D] * _sigmoid(ph_ref[:, D:2 * D]) * keep
        he_ref[HALO:, :] = p_ref[:, 0:D] * _sigmoid(p_ref[:, D:2 * D])
        _conv_taps(he_ref, sh_ref, cw_ref, HALO - (CONV_K - 1), 1, conv_b, h1_ref, tm)
        h2, _, _ = _group_norm_fwd(h1_ref[...], gn_g, gn_b)
        a_gate = p_ref[:, 2 * D:3 * D]
        h3 = ((h2 * _sigmoid(h2)) * (a_gate * _sigmoid(a_gate))).astype(BF16)
        h3_ref[...] = h3
        ya = _dot(h3, wpa_ref[...])
        ya_ref[...] = ya

        u = _gelu(p_ref[:, 3 * D:4 * D])
        vhat, _ = _norm_stats(_gelu(p_ref[:, 4 * D:5 * D]))
        v1 = (vhat * lnv_g + lnv_b).astype(BF16)
        b_gate = p_ref[:, 5 * D:6 * D]
        vmix = _spatial_mix(ws_ref, v1, tm) + jnp.concatenate([bsp_ref[...]] * (tm // CHUNK), axis=0)
        s = (u * vmix * (b_gate * _sigmoid(b_gate))).astype(BF16)
        s_ref[...] = s
        yb = _dot(s, wpb_ref[...])
        yb_ref[...] = yb

        mixed = (_sigmoid(p_ref[:, 6 * D:7 * D]) * ya + _sigmoid(p_ref[:, 7 * D:8 * D]) * yb).astype(BF16)
        mx_ref[...] = mixed
        r = ALPHA * x_ref[...] + (_dot(mixed, wo_ref[...]) + b_o)
        xhat, rstd = _norm_stats(r)
        err = (xhat * lno_g + lno_b) - t_ref[...]
        dout = err * (1.0 / D)
        dr = _norm_bwd(dout * lno_g, xhat, rstd)
        dr_ref[...] = dr
        drb_ref[...] = dr.astype(BF16)

        @pl.when(i == 0)
        def _():
            acc_ref[...] = jnp.zeros_like(acc_ref)

        acc_ref[0:1, :] += _colsum(dout * xhat)
        acc_ref[1:2, :] += _colsum(dout)
        acc_ref[2:3, :] += _colsum(dr)
        acc_ref[3:4, :] += _colsum(err * err) * (0.5 / D)

    tile = lambda w: pl.BlockSpec((tm, w), lambda i: (i, 0))
    f32_out = jax.ShapeDtypeStruct((t, D), F32)
    bf_out = jax.ShapeDtypeStruct((t, D), BF16)
    return pl.pallas_call(
        body, name="forward_tiles", grid=(t // tm,),
        in_specs=[tile(D_IN),
                  pl.BlockSpec((HALO, 2 * D), lambda i: (jnp.maximum(i * hb - 1, 0), 0)),
                  tile(D), tile(D), _resident((D, D)), _resident((D, D)), _resident((D, D)), _full((HALO, D)), _full((8, D)),
                  _full((N_GROUPS, CHUNK, CHUNK)), _full((CHUNK, D))],
        out_specs=[tile(D)] * 8 + [pl.BlockSpec((D, tm), lambda i: (0, i)), _full((8, D))],
        out_shape=[f32_out, f32_out, f32_out, bf_out, bf_out, bf_out, f32_out, bf_out,
                   jax.ShapeDtypeStruct((D, t), BF16), jax.ShapeDtypeStruct((8, D), F32)],
        scratch_shapes=[pltpu.VMEM((tm + HALO, D), F32), pltpu.VMEM((SUBLANES - 1, tm + SHIFT_ROWS, GROUP_W), F32)],
        compiler_params=_params("arbitrary"),
    )(p, p, x, tgt, wpa, wpb, wo, convw, vecs, ws, bsp)


def _backward_tiles(p, h1, ya, yb, drb, wpa, wpb, wo, vecs, ws, wst, bsp):
    t = h1.shape[0]
    tm = TOKEN_TILE

    def body(p_ref, h1_ref, ya_ref, yb_ref, drb_ref, wpa_ref, wpb_ref, wo_ref, vec_ref, ws_ref, wst_ref, bsp_ref,
             dh1_ref, dp_ref, dya_ref, dyb_ref, acc_ref, dbin_ref, dws_ref, dbsp_ref):
        i = pl.program_id(0)
        _, gn_g, gn_b, lnv_g, lnv_b = [vec_ref[j:j + 1, :] for j in range(5)]

        @pl.when(i == 0)
        def _():
            acc_ref[...] = jnp.zeros_like(acc_ref)
            dbin_ref[...] = jnp.zeros_like(dbin_ref)
            dws_ref[...] = jnp.zeros_like(dws_ref)
            dbsp_ref[...] = jnp.zeros_like(dbsp_ref)

        def emit(block, val):
            dbin_ref[0:1, block * D:(block + 1) * D] += _colsum(val)
            dp_ref[:, block * D:(block + 1) * D] = val.astype(BF16)

        dp_ref[:, 0:2 * D] = jnp.zeros((tm, 2 * D), BF16)
        dmixed = _dot_nt(drb_ref[...], wo_ref[...])
        ga = _sigmoid(p_ref[:, 6 * D:7 * D])
        gb = _sigmoid(p_ref[:, 7 * D:8 * D])
        dya32, dyb32 = dmixed * ga, dmixed * gb
        dya = dya32.astype(BF16)
        dyb = dyb32.astype(BF16)
        dya_ref[...] = dya
        dyb_ref[...] = dyb
        emit(6, dya32 * ya_ref[...] * (1.0 - ga))
        emit(7, dyb32 * yb_ref[...] * (1.0 - gb))

        dh3 = _dot_nt(dya, wpa_ref[...])
        h2, xhat, rstd = _group_norm_fwd(h1_ref[...], gn_g, gn_b)
        sg = _sigmoid(h2)
        a_gate = p_ref[:, 2 * D:3 * D]
        sa = _sigmoid(a_gate)
        silu_h2, silu_a = h2 * sg, a_gate * sa
        dh2 = dh3 * silu_a * (sg + silu_h2 * (1.0 - sg))
        emit(2, dh3 * silu_h2 * (sa + silu_a * (1.0 - sa)))
        acc_ref[0:1, :] += _colsum(dh2 * xhat)
        acc_ref[1:2, :] += _colsum(dh2)
        dxhat = dh2 * gn_g
        for g in range(N_GROUPS):
            cs = slice(g * GROUP_W, (g + 1) * GROUP_W)
            dh1_ref[:, cs] = _norm_bwd(dxhat[:, cs], xhat[:, cs], rstd[g])

        ds = _dot_nt(dyb, wpb_ref[...])
        u_pre = p_ref[:, 3 * D:4 * D]
        u, du_dpre = _gelu_and_grad(u_pre)
        v0, dv_dpre = _gelu_and_grad(p_ref[:, 4 * D:5 * D])
        vhat, vrstd = _norm_stats(v0)
        v1 = (vhat * lnv_g + lnv_b).astype(BF16)
        vmix = _spatial_mix(ws_ref, v1, tm) + jnp.concatenate([bsp_ref[...]] * (tm // CHUNK), axis=0)
        b_gate = p_ref[:, 5 * D:6 * D]
        sb = _sigmoid(b_gate)
        silu_b = b_gate * sb
        emit(3, ds * vmix * silu_b * du_dpre)
        emit(5, ds * u * vmix * (sb + silu_b * (1.0 - sb)))
        dvmix = ds * u * silu_b
        dvmix_bf = dvmix.astype(BF16)
        for q in range(tm // CHUNK):
            dbsp_ref[...] += dvmix[q * CHUNK:(q + 1) * CHUNK, :]
            for h in range(N_GROUPS):
                blk = (slice(q * CHUNK, (q + 1) * CHUNK), slice(h * GROUP_W, (h + 1) * GROUP_W))
                dws_ref[:, h * GROUP_W:(h + 1) * GROUP_W] += _dot_nt(dvmix_bf[blk], v1[blk])
        dv1 = _spatial_mix(wst_ref, dvmix_bf, tm)
        acc_ref[2:3, :] += _colsum(dv1 * vhat)
        acc_ref[3:4, :] += _colsum(dv1)
        emit(4, _norm_bwd(dv1 * lnv_g, vhat, vrstd) * dv_dpre)

    tile = lambda w: pl.BlockSpec((tm, w), lambda i: (i, 0))
    return pl.pallas_call(
        body, name="backward_tiles", grid=(t // tm,),
        in_specs=[tile(D_IN), tile(D), tile(D), tile(D), tile(D), _resident((D, D)), _resident((D, D)), _resident((D, D)),
                  _full((8, D)), _full((N_GROUPS, CHUNK, CHUNK)), _full((N_GROUPS, CHUNK, CHUNK)), _full((CHUNK, D))],
        out_specs=[tile(D), tile(D_IN), tile(D), tile(D), _full((8, D)), _full((8, D_IN)),
                   _full((CHUNK, D)), _full((CHUNK, D))],
        out_shape=[jax.ShapeDtypeStruct((t, D), F32), jax.ShapeDtypeStruct((t, D_IN), BF16),
                   jax.ShapeDtypeStruct((t, D), BF16), jax.ShapeDtypeStruct((t, D), BF16),
                   jax.ShapeDtypeStruct((8, D), F32), jax.ShapeDtypeStruct((8, D_IN), F32),
                   jax.ShapeDtypeStruct((CHUNK, D), F32), jax.ShapeDtypeStruct((CHUNK, D), F32)],
        compiler_params=_params("arbitrary"),
    )(p, h1, ya, yb, drb, wpa, wpb, wo, vecs, ws, wst, bsp)


def _conv_backward(dh1, p, dp, convw, pairs, tiles_per_seq):
    t = dh1.shape[0]
    tm = TOKEN_TILE
    hb = tm // HALO
    last = t // HALO - 1
    n_sq = len(pairs)
    span = 2
    rows = D // 8

    def body(dh1_ref, dnext_ref, p_ref, ph_ref, cw_ref, dp_in_ref, *refs):
        del dp_in_ref
        sq_in = refs[:2 * n_sq]
        dp_ref, dcw_ref, dbin_ref = refs[2 * n_sq:2 * n_sq + 3]
        sq_out = refs[2 * n_sq + 3:3 * n_sq + 3]
        sq_wire = refs[3 * n_sq + 3:4 * n_sq + 3]
        de_ref, he_ref, dh0_ref, sh_ref, acc_ref, wire_ref, sq_sem, wire_sem = refs[4 * n_sq + 3:]
        i = pl.program_id(0)

        @pl.when(i == 0)
        def _():
            dcw_ref[...] = jnp.zeros_like(dcw_ref)
            dbin_ref[...] = jnp.zeros_like(dbin_ref)
            acc_ref[...] = jnp.zeros_like(acc_ref)

        @pl.when(i % span == span - 1)
        def _():
            for a in range(n_sq):
                acc_ref[a] += _dot_tn(sq_in[2 * a][...], sq_in[2 * a + 1][...])

        keep_next = jnp.where(i % tiles_per_seq == tiles_per_seq - 1, 0.0, 1.0)
        de_ref[0:tm, :] = dh1_ref[...]
        de_ref[tm:, :] = dnext_ref[...] * keep_next
        _conv_taps(de_ref, sh_ref, cw_ref, CONV_K - 1, -1, None, dh0_ref, tm)

        keep_prev = jnp.where(i % tiles_per_seq == 0, 0.0, 1.0)
        sg = _sigmoid(p_ref[:, D:2 * D])
        val = p_ref[:, 0:D]
        he_ref[0:HALO, :] = ph_ref[:, 0:D] * _sigmoid(ph_ref[:, D:2 * D]) * keep_prev
        he_ref[HALO:, :] = val * sg
        _conv_weight_grad(dh1_ref, he_ref, sh_ref, HALO - (CONV_K - 1), dcw_ref, tm)
        dcw_ref[SUBLANES * CONV_K:, :] += jnp.sum(dh1_ref[...].reshape(tm // SUBLANES, SUBLANES, D), axis=0)

        dh0 = dh0_ref[...]
        dval = dh0 * sg
        dglu = dval * val * (1.0 - sg)
        dbin_ref[0:1, 0:D] += _colsum(dval)
        dbin_ref[0:1, D:2 * D] += _colsum(dglu)
        dp_ref[:, 0:D] = dval.astype(BF16)
        dp_ref[:, D:2 * D] = dglu.astype(BF16)

        @pl.when(i == t // tm - 1)
        def _():
            cps = [pltpu.make_async_copy(acc_ref.at[a, pl.ds((2 * j + h) * rows, rows)], sq_out[a].at[h, j],
                                         sq_sem.at[(a * N_CHIPS + j) * 2 + h])
                   for a in range(n_sq) for j in range(N_CHIPS) for h in range(2)]
            for cp in cps:
                cp.start()
            for a in range(n_sq):
                wire_ref[...] = acc_ref[a].astype(BF16)
                narrow = [pltpu.make_async_copy(wire_ref.at[pl.ds((2 * j + h) * rows, rows)], sq_wire[a].at[h, j],
                                                wire_sem.at[2 * j + h]) for j in range(N_CHIPS) for h in range(2)]
                for cp in narrow:
                    cp.start()
                for cp in narrow:
                    cp.wait()
            for cp in cps:
                cp.wait()

    any_spec = pl.BlockSpec(memory_space=pl.ANY)
    wide = pl.BlockSpec((span * tm, D), lambda i: (i // span, 0))
    return pl.pallas_call(
        body, name="conv_backward", grid=(t // tm,),
        in_specs=[pl.BlockSpec((tm, D), lambda i: (i, 0)),
                  pl.BlockSpec((HALO, D), lambda i: (jnp.minimum((i + 1) * hb, last), 0)),
                  pl.BlockSpec((tm, 2 * D), lambda i: (i, 0)),
                  pl.BlockSpec((HALO, 2 * D), lambda i: (jnp.maximum(i * hb - 1, 0), 0)),
                  _full((HALO, D)), any_spec] + [wide] * (2 * n_sq),
        out_specs=[pl.BlockSpec((tm, 2 * D), lambda i: (i, 0)), _full((SUBLANES * HALO, D)), _full((8, 2 * D))]
        + [any_spec] * (2 * n_sq),
        out_shape=[jax.ShapeDtypeStruct(dp.shape, BF16), jax.ShapeDtypeStruct((SUBLANES * HALO, D), F32),
                   jax.ShapeDtypeStruct((8, 2 * D), F32)]
        + [jax.ShapeDtypeStruct((2, N_CHIPS, rows, D), F32)] * n_sq
        + [jax.ShapeDtypeStruct((2, N_CHIPS, rows, D), BF16)] * n_sq,
        scratch_shapes=[pltpu.VMEM((tm + HALO, D), F32), pltpu.VMEM((tm + HALO, D), F32), pltpu.VMEM((tm, D), F32),
                        pltpu.VMEM((SUBLANES - 1, tm + SHIFT_ROWS, GROUP_W), F32), pltpu.VMEM((n_sq, D, D), F32),
                        pltpu.VMEM((D, D), BF16), pltpu.SemaphoreType.DMA((n_sq * N_CHIPS * 2,)),
                        pltpu.SemaphoreType.DMA((N_CHIPS * 2,))],
        input_output_aliases={5: 0},
        compiler_params=_params("arbitrary"),
    )(dh1, dh1, p, p, convw, dp, *[a for pair in pairs for a in pair])


def _grad_in_and_x(xt, dp, w4, dr, wires, grads):
    t = dr.shape[0]
    tm = TOKEN_TILE
    half, tn = D // 2, 512
    nb = W_BLOCK // tn
    n_w, n_x = 2 * N_CHIPS * nb, t // tm
    ns = len(grads)
    xi, yi, ci = lax.axis_index("x"), lax.axis_index("y"), lax.axis_index("c")
    others = [2 * (1 - xi) + yi, 2 * xi + (1 - yi), 2 * (1 - xi) + (1 - yi)]
    blocks = others + others + [2 * xi + yi] * 2
    halves = [1 - ci] * 3 + [ci] * 3 + [1 - ci, ci]
    table = jnp.stack([jnp.stack([b * nb + n for b in blocks for n in range(nb)]),
                       jnp.stack([h for h in halves for _ in range(nb)])]).astype(jnp.int32)

    def body(tab_ref, xt_ref, dpc_ref, dpr_ref, w_ref, dr_ref, *refs):
        parts, fulls = refs[:ns], refs[ns:2 * ns]
        dx_ref, qk_ref, b2_ref, b1_ref, wire_ref = refs[2 * ns:2 * ns + 5]
        lands, sibs = refs[2 * ns + 5:3 * ns + 5], refs[3 * ns + 5:4 * ns + 5]
        (g_ref, st_ref, sb_ref, tmp_ref, d2d_send, d2d_recv, ici_send, ici_recv, own_sem, tmp_sem, wire_sem,
         p_send, p_recv, s_send, s_recv) = refs[4 * ns + 5:]
        s = pl.program_id(0)
        x_, y_, c, k = _position()
        chips = _other_chips(x_, y_)
        n = s % nb
        grp = s // nb
        cols = pl.ds(pl.multiple_of(n * tn, tn), tn)

        def part(a, r, core):
            cx, cy = chips[r]
            return pltpu.make_async_remote_copy(
                src_ref=parts[a].at[core, 2 * cx + cy], dst_ref=lands[a].at[2 * r + c],
                send_sem=p_send.at[6 * a + 2 * r + core], recv_sem=p_recv.at[6 * a + 2 * r + c],
                device_id=(cx, cy, core), device_id_type=MESH)

        def landed(a, r, core):
            cx, cy = chips[r]
            return pltpu.make_async_remote_copy(
                src_ref=lands[a].at[2 * r + core], dst_ref=lands[a].at[2 * r + core],
                send_sem=p_send.at[6 * a + 2 * r + core], recv_sem=p_recv.at[6 * a + 2 * r + core],
                device_id=(cx, cy, core), device_id_type=MESH)

        def to_sibling_whole(a):
            return pltpu.make_async_remote_copy(
                src_ref=fulls[a].at[1 - c, k], dst_ref=sibs[a], send_sem=s_send.at[a], recv_sem=s_recv.at[a],
                device_id=(x_, y_, 1 - c), device_id_type=MESH)

        def to_sibling(slot, land):
            return pltpu.make_async_remote_copy(
                src_ref=st_ref.at[slot], dst_ref=b1_ref.at[land, :, cols], send_sem=d2d_send.at[slot],
                recv_sem=d2d_recv.at[land * nb + n], device_id=(x_, y_, 1 - c), device_id_type=MESH)

        def to_chip(r):
            cx, cy = chips[r]
            return pltpu.make_async_remote_copy(
                src_ref=wire_ref.at[r, :, cols], dst_ref=b2_ref.at[r, :, cols], send_sem=ici_send.at[r],
                recv_sem=ici_recv.at[r], device_id=(cx, cy, c), device_id_type=MESH)

        def all_of_chip(r):
            cx, cy = chips[r]
            return pltpu.make_async_remote_copy(
                src_ref=wire_ref.at[r], dst_ref=b2_ref.at[r], send_sem=ici_send.at[r],
                recv_sem=ici_recv.at[r], device_id=(cx, cy, c), device_id_type=MESH)

        def to_result(slot):
            return pltpu.make_async_copy(st_ref.at[slot], qk_ref.at[:, cols], own_sem.at[slot])

        def sibling_piece(land):
            return pltpu.make_async_copy(b1_ref.at[land, :, cols], tmp_ref, tmp_sem)

        @pl.when(s == 0)
        def _():
            for a in range(ns):
                to_sibling_whole(a).start()
                for r in range(3):
                    for core in range(2):
                        part(a, r, core).start()

        own_half = ((grp >= 3) & (grp <= 5)) | (grp == 7)
        land = jnp.where(grp == 7, 3, grp - 3)

        @pl.when(own_half)
        def _():
            to_sibling(0, land).wait_recv()
            sibling_piece(land).start()

        @pl.when(s < n_w)
        def _():
            g_ref[...] = _dot(xt_ref[tab_ref[1, s]], dpc_ref[...])

        @pl.when(own_half)
        def _():
            sibling_piece(land).wait()

        for g in range(2 * N_CHIPS):
            @pl.when(grp == g)
            def _(g=g):
                if g in (0, 1, 2, 6):
                    use = s if g < 3 else 3 * nb + n
                    slot = use % 2

                    @pl.when(use >= 2)
                    def _():
                        to_sibling(slot, 0).wait_send()

                    st_ref[slot] = g_ref[...]
                    to_sibling(slot, min(g, 3)).start()
                elif g in (3, 4, 5):
                    sb_ref[...] = (g_ref[...] + tmp_ref[...]).astype(BF16)
                    stage = pltpu.make_async_copy(sb_ref, wire_ref.at[g - 3, :, cols], wire_sem)
                    stage.start()
                    stage.wait()
                    to_chip(g - 3).start()
                else:
                    slot = n % 2
                    piece = g_ref[...] + tmp_ref[...]

                    @pl.when(n < 2)
                    def _():
                        to_sibling(slot, 0).wait_send()

                    @pl.when(n >= 2)
                    def _():
                        to_result(slot).wait()

                    st_ref[slot] = piece
                    to_result(slot).start()

        @pl.when(s >= n_w)
        def _():
            acc = ALPHA * dr_ref[...]
            for j in range(N_CHIPS):
                acc = acc + _dot_nt(dpr_ref[:, j * W_BLOCK:(j + 1) * W_BLOCK], w_ref[j])
            dx_ref[...] = acc

        @pl.when(s == n_w + n_x - 1)
        def _():
            for slot in range(2):
                to_result(slot).wait()
            for r in range(3):
                all_of_chip(r).wait_recv()
                all_of_chip(r).wait_send()
            for a in range(ns):
                to_sibling_whole(a).wait_recv()
                to_sibling_whole(a).wait_send()
                for r in range(3):
                    for core in range(2):
                        landed(a, r, core).wait_recv()
                        part(a, r, core).wait_send()

    any_spec = pl.BlockSpec(memory_space=pl.ANY)
    tile = lambda s, tab: (jnp.maximum(s - n_w, 0), 0)
    return pl.pallas_call(
        body, name="grad_in_and_x",
        grid_spec=pltpu.PrefetchScalarGridSpec(
            num_scalar_prefetch=1, grid=(n_w + n_x,),
            in_specs=[pl.BlockSpec((2, half, t), lambda s, tab: (0, 0, 0), pipeline_mode=pl.Buffered(1)),
                      pl.BlockSpec((t, tn), lambda s, tab: (0, tab[0, jnp.minimum(s, n_w - 1)])),
                      pl.BlockSpec((tm, D_IN), tile),
                      pl.BlockSpec((N_CHIPS, D, W_BLOCK), lambda s, tab: (0, 0, 0), pipeline_mode=pl.Buffered(1)),
                      pl.BlockSpec((tm, D), tile)] + [any_spec] * (2 * ns),
            out_specs=[pl.BlockSpec((tm, D), tile)] + [any_spec] * (4 + 2 * ns),
            scratch_shapes=[pltpu.VMEM((half, tn), F32), pltpu.VMEM((2, half, tn), F32), pltpu.VMEM((half, tn), BF16),
                            pltpu.VMEM((half, tn), F32),
                            pltpu.SemaphoreType.DMA((2,)), pltpu.SemaphoreType.DMA((N_CHIPS * nb,)),
                            pltpu.SemaphoreType.DMA((3,)), pltpu.SemaphoreType.DMA((3,)),
                            pltpu.SemaphoreType.DMA((2,)), pltpu.SemaphoreType.DMA, pltpu.SemaphoreType.DMA,
                            pltpu.SemaphoreType.DMA((6 * ns,)), pltpu.SemaphoreType.DMA((6 * ns,)),
                            pltpu.SemaphoreType.DMA((ns,)), pltpu.SemaphoreType.DMA((ns,))]),
        out_shape=[jax.ShapeDtypeStruct((t, D), F32), jax.ShapeDtypeStruct((half, W_BLOCK), F32),
                   jax.ShapeDtypeStruct((3, half, W_BLOCK), BF16), jax.ShapeDtypeStruct((N_CHIPS, half, W_BLOCK), F32),
                   jax.ShapeDtypeStruct((3, half, W_BLOCK), BF16)]
        + [jax.ShapeDtypeStruct((6,) + w.shape[2:], w.dtype) for w in wires]
        + [jax.ShapeDtypeStruct(g.shape[2:], F32) for g in grads],
        compiler_params=_params("arbitrary"),
    )(table, xt, dp, dp, w4, dr, *wires, *grads)


def kernel(x, w_in, b_in, conv_w, conv_b, gn_g, gn_b, ln_v_g, ln_v_b, w_spatial, b_spatial, w_pa, w_pb, w_o, b_o, ln_out_g, ln_out_b, loss_target, m_w_in, m_b_in, m_conv_w, m_conv_b, m_gn_g, m_gn_b, m_ln_v_g, m_ln_v_b, m_w_spatial, m_b_spatial, m_w_pa, m_w_pb, m_w_o, m_b_o, m_ln_out_g, m_ln_out_b, v_w_in, v_b_in, v_conv_w, v_conv_b, v_gn_g, v_gn_b, v_ln_v_g, v_ln_v_b, v_w_spatial, v_b_spatial, v_w_pa, v_w_pb, v_w_o, v_b_o, v_ln_out_g, v_ln_out_b):
    n_seq, seq, _ = x.shape
    t = n_seq * seq
    tiles_per_seq = seq // TOKEN_TILE
    x2 = x.reshape(t, D)
    tgt = loss_target.reshape(t, D)

    conv_shard = jnp.pad(conv_w, ((0, HALO - CONV_K), (0, 0)))
    p, win4, wpa4, wpb4, wo4, conv4 = _proj_gather(
        x2, b_in,
        _place_shards([w_in, w_pa, w_pb, w_o, conv_shard], [BF16, BF16, BF16, BF16, F32]))
    win4 = win4.reshape(N_CHIPS, D, W_BLOCK)
    wpa, wpb, wo = wpa4.reshape(D, D), wpb4.reshape(D, D), wo4.reshape(D, D)
    convw = conv4.reshape(N_CHIPS, HALO, D // N_CHIPS).transpose(1, 0, 2).reshape(HALO, D)

    vecs = jnp.stack([conv_b, gn_g, gn_b, ln_v_g, ln_v_b, b_o, ln_out_g, ln_out_b])
    causal = jnp.tril(jnp.ones((CHUNK, CHUNK), bool))
    ws = jnp.where(causal[None], w_spatial, 0.0)
    ws_bf, wst_bf = ws.astype(BF16), ws.transpose(0, 2, 1).astype(BF16)
    bsp = jnp.repeat(b_spatial.T, GROUP_W, axis=1)

    h1, ya, yb, h3, s, mixed, dr, drb, xt, acc_f = _forward_tiles(p, x2, tgt, wpa, wpb, wo, convw, vecs, ws_bf, bsp, tiles_per_seq)
    dh1, dp, dya, dyb, acc_b, dbin_b, dws, dbsp_acc = _backward_tiles(p, h1, ya, yb, drb, wpa, wpb, wo, vecs, ws_bf, wst_bf, bsp)
    dp, dcw8, dbin_a, *square = _conv_backward(dh1, p, dp, convw, [(h3, dya), (s, dyb), (mixed, drb)], tiles_per_seq)

    small, g_conv = _pack_small(acc_f, acc_b, dbin_a, dbin_b, dcw8, dws, dbsp_acc)
    small = small.reshape(2, N_CHIPS, SMALL_ROWS // 8, D)

    grads = square[:3] + [g_conv, small]
    wires = square[3:] + [g_conv, small]
    grad_x, q_in, chips_in, _, _, *landed = _grad_in_and_x(xt.reshape(2, D // 2, t), dp, win4, dr, wires, grads)
    grad_x = grad_x.reshape(x.shape)
    mine = [_add_chips(q_in, chips_in)]
    mine += _add_devices(grads, landed[:5], landed[5:])
    *full, small_parts = _share_results(mine[:5], mine[5])
    grad_w_in, grad_w_pa, grad_w_pb, grad_w_o = [f.reshape(w.shape) for f, w in zip(full[:4], (w_in, w_pa, w_pb, w_o))]
    grad_conv_w = full[4].reshape(HALO, D // N_CHIPS)[:CONV_K]

    big = {"w_in": (grad_w_in,) + tuple(_adamw(w_in, grad_w_in, m_w_in, v_w_in))}
    group = _adamw_group([(w_pa, grad_w_pa, m_w_pa, v_w_pa), (w_pb, grad_w_pb, m_w_pb, v_w_pb),
                          (w_o, grad_w_o, m_w_o, v_w_o)], (conv_w, grad_conv_w, m_conv_w, v_conv_w))
    for name, g, res in zip(["w_pa", "w_pb", "w_o", "conv_w"], [grad_w_pa, grad_w_pb, grad_w_o, grad_conv_w], group):
        big[name] = (g,) + tuple(res)
    vec_names = ["conv_b", "gn_g", "gn_b", "ln_v_g", "ln_v_b", "b_o", "ln_out_g", "ln_out_b"]
    vec_triples = [(conv_b, m_conv_b, v_conv_b), (gn_g, m_gn_g, v_gn_g), (gn_b, m_gn_b, v_gn_b),
                   (ln_v_g, m_ln_v_g, v_ln_v_g), (ln_v_b, m_ln_v_b, v_ln_v_b), (b_o, m_b_o, v_b_o),
                   (ln_out_g, m_ln_out_g, v_ln_out_g), (ln_out_b, m_ln_out_b, v_ln_out_b)]
    small_res, loss8 = _adamw_small(
        small_parts.reshape(8, SMALL_ROWS // 8, D), vec_triples, (b_in, m_b_in, v_b_in),
        (w_spatial, m_w_spatial, v_w_spatial), (b_spatial, m_b_spatial, v_b_spatial))
    per_name = dict(zip(vec_names + ["b_in", "w_spatial", "b_spatial"], small_res))

    order = ["w_in", "b_in", "conv_w", "conv_b", "gn_g", "gn_b", "ln_v_g", "ln_v_b", "w_spatial", "b_spatial",
             "w_pa", "w_pb", "w_o", "b_o", "ln_out_g", "ln_out_b"]
    outs = [loss8[0, 0], grad_x]
    for kind in range(4):
        outs += [big[n][kind] if n in big else per_name[n][kind] for n in order]
    return tuple(outs)
```

```python
import math

import jax
import jax.numpy as jnp
from jax import lax
from jax.experimental import pallas as pl
from jax.experimental.pallas import tpu as pltpu

D = 1024
N_GROUPS = 8
GROUP_W = D // N_GROUPS
CHUNK = 128
CONV_K = 31
HALO = 32
D_IN = 8 * D
N_CHIPS = 4
W_BLOCK = D_IN // N_CHIPS
ALPHA = 2.0 ** 0.25
LN_EPS = 1e-5
ADAM_LR, ADAM_B1, ADAM_B2, ADAM_EPS, ADAM_WD, ADAM_STEP = 0.001, 0.9, 0.999, 1e-08, 0.01, 10

TOKEN_TILE = 256
VMEM_LIMIT = 56 * 1024 * 1024
MESH = pl.DeviceIdType.MESH
F32, BF16 = jnp.float32, jnp.bfloat16


def _sigmoid(x):
    return jax.nn.sigmoid(x)


GELU_C = math.sqrt(2.0 / math.pi)
GELU_CA = GELU_C * 0.044715


def _gelu(x):
    t = jnp.tanh(x * (GELU_C + GELU_CA * (x * x)))
    return x * (0.5 + 0.5 * t)


def _gelu_and_grad(x):
    x2 = x * x
    t = jnp.tanh(x * (GELU_C + GELU_CA * x2))
    cdf = 0.5 + 0.5 * t
    return x * cdf, cdf + (0.5 * x) * (1.0 - t * t) * (GELU_C + (3.0 * GELU_CA) * x2)


def _norm_stats(v):
    mu = jnp.mean(v, axis=-1, keepdims=True)
    vc = v - mu
    var = jnp.mean(vc * vc, axis=-1, keepdims=True)
    rstd = lax.rsqrt(var + LN_EPS)
    return vc * rstd, rstd


def _norm_bwd(dxhat, xhat, rstd):
    m1 = jnp.mean(dxhat, axis=-1, keepdims=True)
    m2 = jnp.mean(dxhat * xhat, axis=-1, keepdims=True)
    return rstd * (dxhat - m1 - xhat * m2)


def _dot(a, b):
    return jnp.dot(a, b, preferred_element_type=F32)


def _dot_nt(a, b):
    return lax.dot_general(a, b, (((1,), (1,)), ((), ())), preferred_element_type=F32)


def _dot_tn(a, b):
    return lax.dot_general(a, b, (((0,), (0,)), ((), ())), preferred_element_type=F32)


def _colsum(v):
    return jnp.sum(v, axis=0, keepdims=True)


def _full(shape):
    return pl.BlockSpec(shape, lambda *_: (0,) * len(shape))


def _resident(shape):
    return pl.BlockSpec(shape, lambda *_: (0,) * len(shape), pipeline_mode=pl.Buffered(1))


def _params(*sem):
    return pltpu.CompilerParams(dimension_semantics=sem, vmem_limit_bytes=VMEM_LIMIT)


def _chip_index():
    return (2 * lax.axis_index("x") + lax.axis_index("y")).astype(jnp.int32).reshape(1)


def _core_index():
    return lax.axis_index("c").astype(jnp.int32).reshape(1)


def _place_shards(ws, dtypes):
    n = len(ws)

    def body(k_ref, *refs):
        for w_ref, o_ref, dtype in zip(refs[:n], refs[n:], dtypes):
            rows = w_ref.shape[0] // 2
            for h in range(2):
                o_ref[h] = w_ref[h * rows:(h + 1) * rows, :].astype(dtype)

    return pl.pallas_call(
        body, name="place_shards",
        grid_spec=pltpu.PrefetchScalarGridSpec(
            num_scalar_prefetch=1, grid=(1,),
            in_specs=[pl.BlockSpec(w.shape, lambda i, k: (0, 0)) for w in ws],
            out_specs=[pl.BlockSpec((None, 2, w.shape[0] // 2, w.shape[1]), lambda i, k: (k[0], 0, 0, 0)) for w in ws]),
        out_shape=[jax.ShapeDtypeStruct((N_CHIPS, 2, w.shape[0] // 2, w.shape[1]), dt) for w, dt in zip(ws, dtypes)],
        compiler_params=_params("arbitrary"),
    )(_chip_index(), *ws)


def _position():
    x, y, c = lax.axis_index("x"), lax.axis_index("y"), lax.axis_index("c")
    return x, y, c, 2 * x + y


def _other_chips(x, y):
    return [(1 - x, y), (x, 1 - y), (1 - x, 1 - y)]


def _any_specs(n):
    return [pl.BlockSpec(memory_space=pl.ANY)] * n


def _proj_gather(x, b_in, bufs):
    t = x.shape[0]
    tm = 1024
    steps = t // tm
    ahead = 3 * steps // 4
    half = D // 2
    chunk = W_BLOCK // 2
    n = len(bufs)
    xi, yi = lax.axis_index("x"), lax.axis_index("y")
    chips = [2 * xi + yi, 2 * (1 - xi) + yi, 2 * xi + (1 - yi), 2 * (1 - xi) + (1 - yi)]
    plan = [(0, 0), (0, 1), (1, 0), (2, 1), (1, 1), (2, 0), (3, 0), (3, 1)]
    order = jnp.stack([2 * chips[ch] + q for ch, q in plan]).astype(jnp.int32)

    def body(order_ref, x_ref, b_ref, *refs):
        p_ref, outs = refs[n], refs[n + 1:2 * n + 1]
        xb_ref, w_ref, lsem, send, recv, hop_send, hop_recv, fsend, frecv, qsend, qrecv = refs[2 * n + 1:]
        jj, i = pl.program_id(0), pl.program_id(1)
        x_, y_, c, k = _position()
        nbrs = [(1 - x_, y_), (x_, 1 - y_)]
        blocks = [2 * (1 - x_) + y_, 2 * x_ + (1 - y_), 2 * (1 - x_) + (1 - y_)]

        def quarter(a, block, q, h):
            if a == 0:
                return outs[0].at[block, h, :, pl.ds(q * chunk, chunk)]
            rows = outs[a].shape[2] // 2
            return outs[a].at[block, h, pl.ds(q * rows, rows)]

        def copy(ref, to, send_sem, recv_sem):
            return pltpu.make_async_remote_copy(src_ref=ref, dst_ref=ref, send_sem=send_sem, recv_sem=recv_sem,
                                                device_id=(to[0], to[1], c), device_id_type=MESH)

        def sent(a, nb, q):
            return copy(quarter(a, k, q, c), nbrs[nb], send.at[4 * a + 2 * nb + q], recv.at[4 * a + 2 * nb + q])

        def landed(a, nb, q):
            return copy(quarter(a, blocks[nb], q, c), nbrs[nb], send.at[4 * a + 2 * nb + q], recv.at[4 * a + 2 * nb + q])

        def hopped(a, nb):
            return copy(quarter(a, blocks[nb], nb, c), nbrs[1 - nb], hop_send.at[2 * a + nb], hop_recv.at[2 * a + 1 - nb])

        def from_diagonal(a, via):
            return copy(quarter(a, blocks[2], 1 - via, c), nbrs[via], hop_send.at[2 * a + via], hop_recv.at[2 * a + via])

        def passed(a, r, h):
            return pltpu.make_async_remote_copy(
                src_ref=outs[a].at[blocks[r], h], dst_ref=outs[a].at[blocks[r], h], send_sem=fsend.at[3 * a + r],
                recv_sem=frecv.at[3 * a + r], device_id=(x_, y_, 1 - c), device_id_type=MESH)

        def passed_quarter(r, q, h):
            ref = quarter(0, blocks[r], q, h)
            return pltpu.make_async_remote_copy(
                src_ref=ref, dst_ref=ref, send_sem=qsend.at[2 * r + q], recv_sem=qrecv.at[2 * r + q],
                device_id=(x_, y_, 1 - c), device_id_type=MESH)

        def load(block, q, slot):
            return [pltpu.make_async_copy(quarter(0, block, q, h), w_ref.at[slot, pl.ds(h * half, half)],
                                          lsem.at[2 * slot + h]) for h in range(2)]

        def pass_on(arrays):
            for a in arrays:
                for nb in range(2):
                    landed(a, nb, nb).wait_recv()
                    hopped(a, nb).start()

        @pl.when((jj == 0) & (i == 0))
        def _():
            for a in range(n):
                for nb, q in ((0, 0), (1, 1), (0, 1), (1, 0)):
                    sent(a, nb, q).start()
            for cp in load(k, 0, 0):
                cp.start()

        for nxt in range(1, len(plan)):
            @pl.when((jj == nxt - 1) & (i == ahead))
            def _(nxt=nxt):
                ch, q = plan[nxt]
                if ch in (1, 2):
                    landed(0, ch - 1, q).wait_recv()
                    if q == ch - 1:
                        hopped(0, ch - 1).start()
                elif ch == 3:
                    from_diagonal(0, 1 - q).wait_recv()
                if ch:
                    passed_quarter(ch - 1, q, c).start()
                    passed_quarter(ch - 1, q, 1 - c).wait_recv()
                for cp in load(k if ch == 0 else blocks[ch - 1], q, nxt % 2):
                    cp.start()
                if nxt == 5:
                    pass_on(range(1, n))
                    for a in range(1, n):
                        landed(a, 0, 1).wait_recv()
                        passed(a, 0, c).start()
                        landed(a, 1, 0).wait_recv()
                        passed(a, 1, c).start()

        slot = jj % 2

        @pl.when(i == 0)
        def _():
            for cp in load(k, 0, slot):
                cp.wait()

        rows = pl.ds(pl.multiple_of(i * tm, tm), tm)

        @pl.when(jj == 0)
        def _():
            xb_ref[rows, :] = x_ref[...].astype(BF16)

        p_ref[...] = _dot(xb_ref[rows, :], w_ref[slot]) + b_ref[...]

        @pl.when((jj == len(plan) - 1) & (i == steps - 1))
        def _():
            for a in range(1, n):
                from_diagonal(a, 0).wait_recv()
                from_diagonal(a, 1).wait_recv()
                passed(a, 2, c).start()
            for a in range(1, n):
                for r in range(3):
                    passed(a, r, 1 - c).wait_recv()
                    passed(a, r, c).wait_send()
            for r in range(3):
                for q in range(2):
                    passed_quarter(r, q, c).wait_send()
            for a in range(n):
                for nb in range(2):
                    for q in range(2):
                        sent(a, nb, q).wait_send()
                    hopped(a, nb).wait_send()

    any_spec = pl.BlockSpec(memory_space=pl.ANY)
    return pl.pallas_call(
        body, name="proj_gather",
        grid_spec=pltpu.PrefetchScalarGridSpec(
            num_scalar_prefetch=1, grid=(len(plan), steps),
            in_specs=[pl.BlockSpec((tm, D), lambda jj, i, o: (jnp.where(jj == 0, i, steps - 1), 0)),
                      pl.BlockSpec((None, 1, chunk), lambda jj, i, o: (o[jj], 0, 0))] + [any_spec] * n,
            out_specs=[pl.BlockSpec((tm, chunk), lambda jj, i, o: (i, o[jj]))] + [any_spec] * n,
            scratch_shapes=[pltpu.VMEM((t, D), BF16), pltpu.VMEM((2, D, chunk), BF16), pltpu.SemaphoreType.DMA((4,)),
                            pltpu.SemaphoreType.DMA((4 * n,)), pltpu.SemaphoreType.DMA((4 * n,)),
                            pltpu.SemaphoreType.DMA((2 * n,)), pltpu.SemaphoreType.DMA((2 * n,)),
                            pltpu.SemaphoreType.DMA((3 * n,)), pltpu.SemaphoreType.DMA((3 * n,)),
                            pltpu.SemaphoreType.DMA((6,)), pltpu.SemaphoreType.DMA((6,))]),
        out_shape=[jax.ShapeDtypeStruct((t, D_IN), F32)] + [jax.ShapeDtypeStruct(b.shape, b.dtype) for b in bufs],
        input_output_aliases={3 + a: 1 + a for a in range(n)},
        compiler_params=_params("arbitrary", "arbitrary"),
    )(order, x, b_in.reshape(D_IN // chunk, 1, chunk), *bufs)


def _share_results(bufs, small):
    n = len(bufs)

    def body(*refs):
        outs, small_out = refs[n + 1:2 * n + 1], refs[2 * n + 1]
        send, recv, ssend, srecv = refs[2 * n + 2:]
        x, y, c, k = _position()
        cps = []
        for a in range(n):
            cp = pltpu.make_async_remote_copy(
                src_ref=outs[a].at[c], dst_ref=outs[a].at[c], send_sem=send.at[a], recv_sem=recv.at[a],
                device_id=(x, y, 1 - c), device_id_type=MESH)
            cp.start()
            cps.append(cp)
        waits = []
        for p in range(1, 8):
            px, py, pc = x ^ (p >> 2), y ^ ((p >> 1) & 1), c ^ (p & 1)
            cp = pltpu.make_async_remote_copy(
                src_ref=small_out.at[k, c], dst_ref=small_out.at[k, c], send_sem=ssend.at[p - 1],
                recv_sem=srecv.at[p - 1], device_id=(px, py, pc), device_id_type=MESH)
            cp.start()
            cps.append(cp)
            waits.append(pltpu.make_async_remote_copy(
                src_ref=small_out.at[2 * px + py, pc], dst_ref=small_out.at[2 * px + py, pc], send_sem=ssend.at[p - 1],
                recv_sem=srecv.at[p - 1], device_id=(px, py, pc), device_id_type=MESH))
        for a in range(n):
            pltpu.make_async_remote_copy(
                src_ref=outs[a].at[1 - c], dst_ref=outs[a].at[1 - c], send_sem=send.at[a], recv_sem=recv.at[a],
                device_id=(x, y, 1 - c), device_id_type=MESH).wait_recv()
        for w in waits:
            w.wait_recv()
        for cp in cps:
            cp.wait_send()

    return pl.pallas_call(
        body, name="rs_share_results",
        in_specs=_any_specs(n + 1), out_specs=_any_specs(n + 1),
        out_shape=[jax.ShapeDtypeStruct(b.shape, b.dtype) for b in bufs + [small]],
        scratch_shapes=[pltpu.SemaphoreType.DMA((n,)), pltpu.SemaphoreType.DMA((n,)),
                        pltpu.SemaphoreType.DMA((7,)), pltpu.SemaphoreType.DMA((7,))],
        input_output_aliases={a: a for a in range(n + 1)},
    )(*bufs, small)


def _row_tile(r, c):
    t = max(8, min(r, (1 << 18) // c))
    while r % t:
        t //= 2
    return t


def _add_devices(grads, lands, sibs):
    n = len(grads)

    def body(kc_ref, *refs):
        for g_ref, l_ref, s_ref, f_ref in zip(refs[:n], refs[n:2 * n], refs[2 * n:3 * n], refs[3 * n:]):
            f = g_ref[...] + s_ref[...]
            for i in range(l_ref.shape[0]):
                f = f + l_ref[i].astype(F32)
            f_ref[...] = f

    shapes = [g.shape[2:] for g in grads]
    out_specs = [pl.BlockSpec((None,) + sh, lambda i, kc: (kc[1], 0, 0)) for sh in shapes[:-1]]
    out_specs.append(pl.BlockSpec((None, None) + shapes[-1], lambda i, kc: (kc[0], kc[1], 0, 0)))
    out_shape = [jax.ShapeDtypeStruct((2,) + sh, F32) for sh in shapes[:-1]]
    out_shape.append(jax.ShapeDtypeStruct((N_CHIPS, 2) + shapes[-1], F32))
    return pl.pallas_call(
        body, name="rs_add_devices",
        grid_spec=pltpu.PrefetchScalarGridSpec(
            num_scalar_prefetch=1, grid=(1,),
            in_specs=[pl.BlockSpec((None, None) + sh, lambda i, kc: (kc[1], kc[0], 0, 0)) for sh in shapes]
            + [pl.BlockSpec(l.shape, lambda i, kc: (0, 0, 0)) for l in lands]
            + [pl.BlockSpec(sh, lambda i, kc: (0, 0)) for sh in shapes],
            out_specs=out_specs),
        out_shape=out_shape,
        compiler_params=_params("arbitrary"),
    )(jnp.concatenate([_chip_index(), _core_index()]), *grads, *lands, *sibs)


def _add_chips(q, b2):
    r, c = q.shape
    t = _row_tile(r, c)

    def body(c_ref, q_ref, b_ref, f_ref):
        f_ref[...] = ((q_ref[...] + b_ref[0].astype(F32)) + b_ref[1].astype(F32)) + b_ref[2].astype(F32)

    return pl.pallas_call(
        body, name="rs_add_chips",
        grid_spec=pltpu.PrefetchScalarGridSpec(
            num_scalar_prefetch=1, grid=(r // t,),
            in_specs=[pl.BlockSpec((t, c), lambda i, cr: (i, 0)), pl.BlockSpec((3, t, c), lambda i, cr: (0, i, 0))],
            out_specs=pl.BlockSpec((None, t, c), lambda i, cr: (cr[0], i, 0))),
        out_shape=jax.ShapeDtypeStruct((2, r, c), F32),
        compiler_params=_params("parallel"),
    )(_core_index(), q, b2)


def _adamw_math(w, g, m, v):
    m = ADAM_B1 * m + (1.0 - ADAM_B1) * g
    v = ADAM_B2 * v + (1.0 - ADAM_B2) * (g * g)
    m_hat = m / (1.0 - ADAM_B1 ** ADAM_STEP)
    v_hat = v / (1.0 - ADAM_B2 ** ADAM_STEP)
    delta = -ADAM_LR * (m_hat / (jnp.sqrt(v_hat) + ADAM_EPS) + ADAM_WD * w)
    return delta, m, v


def _adamw_group(quads, conv):
    n = len(quads)
    r, c = quads[0][0].shape

    def body(*refs):
        ins, outs = refs[:4 * (n + 1)], refs[4 * (n + 1):]
        for i in range(n + 1):
            w_ref, g_ref, m_ref, v_ref = ins[4 * i:4 * i + 4]
            res = _adamw_math(w_ref[...], g_ref[...], m_ref[...], v_ref[...])
            for o_ref, val in zip(outs[3 * i:3 * i + 3], res):
                o_ref[...] = val

    half = pl.BlockSpec((r // 2, c), lambda i: (i, 0))
    whole = pl.BlockSpec(conv[0].shape, lambda i: (0, 0))
    res = pl.pallas_call(
        body, name="adamw_group", grid=(2,),
        in_specs=[half] * (4 * n) + [whole] * 4, out_specs=[half] * (3 * n) + [whole] * 3,
        out_shape=[jax.ShapeDtypeStruct((r, c), F32)] * (3 * n) + [jax.ShapeDtypeStruct(conv[0].shape, F32)] * 3,
        compiler_params=_params("arbitrary"),
    )(*[a for quad in quads for a in quad], *conv)
    return [res[3 * i:3 * i + 3] for i in range(n + 1)]


def _adamw(w, g, m, v):
    r, c = w.shape
    t = _row_tile(r, c) if r % 8 == 0 else r

    def body(w_ref, g_ref, m_ref, v_ref, d_ref, nm_ref, nv_ref):
        d_ref[...], nm_ref[...], nv_ref[...] = _adamw_math(w_ref[...], g_ref[...], m_ref[...], v_ref[...])

    spec = pl.BlockSpec((t, c), lambda i: (i, 0))
    return pl.pallas_call(
        body, name="adamw", grid=(r // t,), in_specs=[spec] * 4, out_specs=[spec] * 3,
        out_shape=[jax.ShapeDtypeStruct((r, c), F32)] * 3, compiler_params=_params("parallel"),
    )(w, g, m, v)


ROW_B_IN = 0
ROW_VECS = 8
ROW_LOSS = 16
ROW_B_SPATIAL = 24
ROW_W_SPATIAL = 32
SMALL_ROWS = 192
N_VECS = 8


def _pack_small(acc_f, acc_b, dbin_a, dbin_b, dcw8, dws, dbsp):
    cols = D // N_CHIPS

    def body(af_ref, ab_ref, da_ref, db_ref, cw_ref, ws_ref, bs_ref, o_ref, gc_ref):
        o_ref[...] = jnp.zeros_like(o_ref)
        for j in range(D_IN // D):
            src = da_ref if j < 2 else db_ref
            o_ref[ROW_B_IN + j:ROW_B_IN + j + 1, :] = src[0:1, j * D:(j + 1) * D]
        dcw = jnp.sum(cw_ref[...].reshape(HALO, SUBLANES, D), axis=1)
        o_ref[ROW_VECS:ROW_VECS + 1, :] = dcw[CONV_K:CONV_K + 1]
        o_ref[ROW_VECS + 1:ROW_VECS + 5, :] = ab_ref[0:4, :]
        o_ref[ROW_VECS + 5:ROW_VECS + 6, :] = af_ref[2:3, :]
        o_ref[ROW_VECS + 6:ROW_VECS + 8, :] = af_ref[0:2, :]
        o_ref[ROW_LOSS:ROW_LOSS + 1, :] = af_ref[3:4, :]
        head = lax.broadcasted_iota(jnp.int32, (N_GROUPS, D), 0)
        lane = lax.broadcasted_iota(jnp.int32, (N_GROUPS, D), 1)
        indicator = jnp.where(lane // GROUP_W == head, 1.0, 0.0)
        o_ref[ROW_B_SPATIAL:ROW_B_SPATIAL + N_GROUPS, 0:CHUNK] = lax.dot_general(
            indicator, bs_ref[...], (((1,), (1,)), ((), ())), precision=lax.Precision.HIGHEST, preferred_element_type=F32)
        t_idx = lax.broadcasted_iota(jnp.int32, (CHUNK, D), 0)
        s_idx = lax.broadcasted_iota(jnp.int32, (CHUNK, D), 1) % CHUNK
        o_ref[ROW_W_SPATIAL:ROW_W_SPATIAL + CHUNK, :] = jnp.where(s_idx <= t_idx, ws_ref[...], 0.0)
        for h in range(2):
            for j in range(N_CHIPS):
                gc_ref[h, j] = dcw[h * (HALO // 2):(h + 1) * (HALO // 2), j * cols:(j + 1) * cols]

    ins = [acc_f, acc_b, dbin_a, dbin_b, dcw8, dws, dbsp]
    return pl.pallas_call(
        body, name="pack_small",
        in_specs=[_full(a.shape) for a in ins],
        out_specs=[_full((SMALL_ROWS, D)), _full((2, N_CHIPS, HALO // 2, cols))],
        out_shape=[jax.ShapeDtypeStruct((SMALL_ROWS, D), F32), jax.ShapeDtypeStruct((2, N_CHIPS, HALO // 2, cols), F32)],
        compiler_params=_params(),
    )(*ins)


def _adamw_small(parts, vecs, b_in, w_spatial, b_spatial):
    triples = list(vecs) + [b_in, w_spatial, b_spatial]
    n_in = 3 * len(triples)

    def body(p_ref, *refs):
        ins = [refs[3 * i:3 * i + 3] for i in range(len(triples))]
        outs = [refs[n_in + 4 * i:n_in + 4 * i + 4] for i in range(len(triples))]
        loss_ref, g_ref = refs[n_in + 4 * len(triples):]
        rows = SMALL_ROWS // 8
        for k in range(N_CHIPS):
            for core in range(2):
                g_ref[(core * N_CHIPS + k) * rows:(core * N_CHIPS + k + 1) * rows, :] = p_ref[2 * k + core]

        def step(g, wmv, out, get, put):
            d, nm, nv = _adamw_math(get(wmv[0]), g, get(wmv[1]), get(wmv[2]))
            for o, val in zip(out, (g, d, nm, nv)):
                put(o, val)

        for i in range(N_VECS):
            step(g_ref[ROW_VECS + i:ROW_VECS + i + 1, :], ins[i], outs[i],
                 lambda r: r[...].reshape(1, D), lambda o, val: o.__setitem__(Ellipsis, val.reshape(D)))
        for j in range(D_IN // D):
            piece = pl.ds(j * D, D)
            step(g_ref[ROW_B_IN + j:ROW_B_IN + j + 1, :], ins[N_VECS], outs[N_VECS],
                 lambda r: r[piece].reshape(1, D), lambda o, val: o.__setitem__(piece, val.reshape(D)))
        for h in range(N_GROUPS):
            step(g_ref[ROW_W_SPATIAL:ROW_W_SPATIAL + CHUNK, h * CHUNK:(h + 1) * CHUNK], ins[N_VECS + 1], outs[N_VECS + 1],
                 lambda r: r[h], lambda o, val: o.__setitem__(h, val))
        step(g_ref[ROW_B_SPATIAL:ROW_B_SPATIAL + N_GROUPS, 0:CHUNK], ins[N_VECS + 2], outs[N_VECS + 2],
             lambda r: r[...], lambda o, val: o.__setitem__(Ellipsis, val))
        lanes = g_ref[ROW_LOSS:ROW_LOSS + 1, :]
        loss_ref[...] = jnp.broadcast_to(jnp.sum(lanes, axis=1, keepdims=True), (8, 128))

    flat = [a for tr in triples for a in tr]
    out_shape = [jax.ShapeDtypeStruct(tr[0].shape, F32) for tr in triples for _ in range(4)]
    out_shape.append(jax.ShapeDtypeStruct((8, 128), F32))
    res = pl.pallas_call(
        body, name="adamw_small",
        in_specs=[_full(parts.shape)] + [_full(a.shape) for a in flat],
        out_specs=[_full(o.shape) for o in out_shape],
        out_shape=out_shape,
        scratch_shapes=[pltpu.VMEM((SMALL_ROWS, D), F32)],
        compiler_params=_params(),
    )(parts, *flat)
    return [res[4 * i:4 * i + 4] for i in range(len(triples))], res[-1]


SUBLANES = 8
SHIFT_ROWS = HALO - SUBLANES


def _shifted_copies(src_ref, sh_ref, cs, tm):
    for p in range(1, SUBLANES):
        sh_ref[p - 1] = src_ref[pl.ds(p, tm + SHIFT_ROWS), cs]


def _tap(src_ref, sh_ref, cs, offset, start, rows):
    p, q = offset % SUBLANES, offset // SUBLANES
    if p == 0:
        return src_ref[pl.ds(start + SUBLANES * q, rows), cs]
    return sh_ref[p - 1, pl.ds(start + SUBLANES * q, rows), :]


def _conv_taps(src_ref, sh_ref, w_ref, first_offset, step, bias, dst_ref, tm):
    rows = 64
    for g in range(N_GROUPS):
        cs = slice(g * GROUP_W, (g + 1) * GROUP_W)
        _shifted_copies(src_ref, sh_ref, cs, tm)
        for rb in range(tm // rows):
            acc = jnp.zeros((rows, GROUP_W), F32) + (bias[:, cs] if bias is not None else 0.0)
            for k in range(CONV_K):
                acc = acc + w_ref[k:k + 1, cs] * _tap(src_ref, sh_ref, cs, first_offset + step * k, rb * rows, rows)
            dst_ref[rb * rows:(rb + 1) * rows, cs] = acc


def _conv_weight_grad(d_ref, src_ref, sh_ref, first_offset, acc_ref, tm):
    rows = 64
    for g in range(N_GROUPS):
        cs = slice(g * GROUP_W, (g + 1) * GROUP_W)
        _shifted_copies(src_ref, sh_ref, cs, tm)
        for rb in range(tm // rows):
            d = d_ref[rb * rows:(rb + 1) * rows, cs]
            for k in range(CONV_K):
                prod = d * _tap(src_ref, sh_ref, cs, first_offset + k, rb * rows, rows)
                acc_ref[SUBLANES * k:SUBLANES * (k + 1), cs] += jnp.sum(
                    prod.reshape(rows // SUBLANES, SUBLANES, GROUP_W), axis=0)


def _spatial_mix(w_ref, v_bf, tm):
    rows = []
    for q in range(tm // CHUNK):
        cols = [_dot(w_ref[h], v_bf[q * CHUNK:(q + 1) * CHUNK, h * GROUP_W:(h + 1) * GROUP_W])
                for h in range(N_GROUPS)]
        rows.append(jnp.concatenate(cols, axis=1))
    return jnp.concatenate(rows, axis=0)


def _group_norm_fwd(h1, gn_g, gn_b):
    xhat, rstd = [], []
    for g in range(N_GROUPS):
        xh, rs = _norm_stats(h1[:, g * GROUP_W:(g + 1) * GROUP_W])
        xhat.append(xh)
        rstd.append(rs)
    xhat = jnp.concatenate(xhat, axis=1)
    return xhat * gn_g + gn_b, xhat, rstd


def _forward_tiles(p, x, tgt, wpa, wpb, wo, convw, vecs, ws, bsp, tiles_per_seq):
    t = x.shape[0]
    tm = TOKEN_TILE
    hb = tm // HALO

    def body(p_ref, ph_ref, x_ref, t_ref, wpa_ref, wpb_ref, wo_ref, cw_ref, vec_ref, ws_ref, bsp_ref,
             h1_ref, ya_ref, yb_ref, h3_ref, s_ref, mx_ref, dr_ref, drb_ref, xt_ref, acc_ref, he_ref, sh_ref):
        i = pl.program_id(0)
        xt_ref[...] = x_ref[...].T.astype(BF16)
        conv_b, gn_g, gn_b, lnv_g, lnv_b, b_o, lno_g, lno_b = [vec_ref[j:j + 1, :] for j in range(8)]

        keep = jnp.where(i % tiles_per_seq == 0, 0.0, 1.0)
        he_ref[0:HALO, :] = ph_ref[:, 0:D] * _sigmoid(ph_ref[:, D:2 * D]) * keep
        he_ref[HALO:, :] = p_ref[:, 0:D] * _sigmoid(p_ref[:, D:2 * D])
        _conv_taps(he_ref, sh_ref, cw_ref, HALO - (CONV_K - 1), 1, conv_b, h1_ref, tm)
        h2, _, _ = _group_norm_fwd(h1_ref[...], gn_g, gn_b)
        a_gate = p_ref[:, 2 * D:3 * D]
        h3 = ((h2 * _sigmoid(h2)) * (a_gate * _sigmoid(a_gate))).astype(BF16)
        h3_ref[...] = h3
        ya = _dot(h3, wpa_ref[...])
        ya_ref[...] = ya

        u = _gelu(p_ref[:, 3 * D:4 * D])
        vhat, _ = _norm_stats(_gelu(p_ref[:, 4 * D:5 * D]))
        v1 = (vhat * lnv_g + lnv_b).astype(BF16)
        b_gate = p_ref[:, 5 * D:6 * D]
        vmix = _spatial_mix(ws_ref, v1, tm) + jnp.concatenate([bsp_ref[...]] * (tm // CHUNK), axis=0)
        s = (u * vmix * (b_gate * _sigmoid(b_gate))).astype(BF16)
        s_ref[...] = s
        yb = _dot(s, wpb_ref[...])
        yb_ref[...] = yb

        mixed = (_sigmoid(p_ref[:, 6 * D:7 * D]) * ya + _sigmoid(p_ref[:, 7 * D:8 * D]) * yb).astype(BF16)
        mx_ref[...] = mixed
        r = ALPHA * x_ref[...] + (_dot(mixed, wo_ref[...]) + b_o)
        xhat, rstd = _norm_stats(r)
        err = (xhat * lno_g + lno_b) - t_ref[...]
        dout = err * (1.0 / D)
        dr = _norm_bwd(dout * lno_g, xhat, rstd)
        dr_ref[...] = dr
        drb_ref[...] = dr.astype(BF16)

        @pl.when(i == 0)
        def _():
            acc_ref[...] = jnp.zeros_like(acc_ref)

        acc_ref[0:1, :] += _colsum(dout * xhat)
        acc_ref[1:2, :] += _colsum(dout)
        acc_ref[2:3, :] += _colsum(dr)
        acc_ref[3:4, :] += _colsum(err * err) * (0.5 / D)

    tile = lambda w: pl.BlockSpec((tm, w), lambda i: (i, 0))
    f32_out = jax.ShapeDtypeStruct((t, D), F32)
    bf_out = jax.ShapeDtypeStruct((t, D), BF16)
    return pl.pallas_call(
        body, name="forward_tiles", grid=(t // tm,),
        in_specs=[tile(D_IN),
                  pl.BlockSpec((HALO, 2 * D), lambda i: (jnp.maximum(i * hb - 1, 0), 0)),
                  tile(D), tile(D), _resident((D, D)), _resident((D, D)), _resident((D, D)), _full((HALO, D)), _full((8, D)),
                  _full((N_GROUPS, CHUNK, CHUNK)), _full((CHUNK, D))],
        out_specs=[tile(D)] * 8 + [pl.BlockSpec((D, tm), lambda i: (0, i)), _full((8, D))],
        out_shape=[f32_out, f32_out, f32_out, bf_out, bf_out, bf_out, f32_out, bf_out,
                   jax.ShapeDtypeStruct((D, t), BF16), jax.ShapeDtypeStruct((8, D), F32)],
        scratch_shapes=[pltpu.VMEM((tm + HALO, D), F32), pltpu.VMEM((SUBLANES - 1, tm + SHIFT_ROWS, GROUP_W), F32)],
        compiler_params=_params("arbitrary"),
    )(p, p, x, tgt, wpa, wpb, wo, convw, vecs, ws, bsp)


def _backward_tiles(p, h1, ya, yb, drb, wpa, wpb, wo, vecs, ws, wst, bsp):
    t = h1.shape[0]
    tm = TOKEN_TILE

    def body(p_ref, h1_ref, ya_ref, yb_ref, drb_ref, wpa_ref, wpb_ref, wo_ref, vec_ref, ws_ref, wst_ref, bsp_ref,
             dh1_ref, dp_ref, dya_ref, dyb_ref, acc_ref, dbin_ref, dws_ref, dbsp_ref):
        i = pl.program_id(0)
        _, gn_g, gn_b, lnv_g, lnv_b = [vec_ref[j:j + 1, :] for j in range(5)]

        @pl.when(i == 0)
        def _():
            acc_ref[...] = jnp.zeros_like(acc_ref)
            dbin_ref[...] = jnp.zeros_like(dbin_ref)
            dws_ref[...] = jnp.zeros_like(dws_ref)
            dbsp_ref[...] = jnp.zeros_like(dbsp_ref)

        def emit(block, val):
            dbin_ref[0:1, block * D:(block + 1) * D] += _colsum(val)
            dp_ref[:, block * D:(block + 1) * D] = val.astype(BF16)

        dp_ref[:, 0:2 * D] = jnp.zeros((tm, 2 * D), BF16)
        dmixed = _dot_nt(drb_ref[...], wo_ref[...])
        ga = _sigmoid(p_ref[:, 6 * D:7 * D])
        gb = _sigmoid(p_ref[:, 7 * D:8 * D])
        dya32, dyb32 = dmixed * ga, dmixed * gb
        dya = dya32.astype(BF16)
        dyb = dyb32.astype(BF16)
        dya_ref[...] = dya
        dyb_ref[...] = dyb
        emit(6, dya32 * ya_ref[...] * (1.0 - ga))
        emit(7, dyb32 * yb_ref[...] * (1.0 - gb))

        dh3 = _dot_nt(dya, wpa_ref[...])
        h2, xhat, rstd = _group_norm_fwd(h1_ref[...], gn_g, gn_b)
        sg = _sigmoid(h2)
        a_gate = p_ref[:, 2 * D:3 * D]
        sa = _sigmoid(a_gate)
        silu_h2, silu_a = h2 * sg, a_gate * sa
        dh2 = dh3 * silu_a * (sg + silu_h2 * (1.0 - sg))
        emit(2, dh3 * silu_h2 * (sa + silu_a * (1.0 - sa)))
        acc_ref[0:1, :] += _colsum(dh2 * xhat)
        acc_ref[1:2, :] += _colsum(dh2)
        dxhat = dh2 * gn_g
        for g in range(N_GROUPS):
            cs = slice(g * GROUP_W, (g + 1) * GROUP_W)
            dh1_ref[:, cs] = _norm_bwd(dxhat[:, cs], xhat[:, cs], rstd[g])

        ds = _dot_nt(dyb, wpb_ref[...])
        u_pre = p_ref[:, 3 * D:4 * D]
        u, du_dpre = _gelu_and_grad(u_pre)
        v0, dv_dpre = _gelu_and_grad(p_ref[:, 4 * D:5 * D])
        vhat, vrstd = _norm_stats(v0)
        v1 = (vhat * lnv_g + lnv_b).astype(BF16)
        vmix = _spatial_mix(ws_ref, v1, tm) + jnp.concatenate([bsp_ref[...]] * (tm // CHUNK), axis=0)
        b_gate = p_ref[:, 5 * D:6 * D]
        sb = _sigmoid(b_gate)
        silu_b = b_gate * sb
        emit(3, ds * vmix * silu_b * du_dpre)
        emit(5, ds * u * vmix * (sb + silu_b * (1.0 - sb)))
        dvmix = ds * u * silu_b
        dvmix_bf = dvmix.astype(BF16)
        for q in range(tm // CHUNK):
            dbsp_ref[...] += dvmix[q * CHUNK:(q + 1) * CHUNK, :]
            for h in range(N_GROUPS):
                blk = (slice(q * CHUNK, (q + 1) * CHUNK), slice(h * GROUP_W, (h + 1) * GROUP_W))
                dws_ref[:, h * GROUP_W:(h + 1) * GROUP_W] += _dot_nt(dvmix_bf[blk], v1[blk])
        dv1 = _spatial_mix(wst_ref, dvmix_bf, tm)
        acc_ref[2:3, :] += _colsum(dv1 * vhat)
        acc_ref[3:4, :] += _colsum(dv1)
        emit(4, _norm_bwd(dv1 * lnv_g, vhat, vrstd) * dv_dpre)

    tile = lambda w: pl.BlockSpec((tm, w), lambda i: (i, 0))
    return pl.pallas_call(
        body, name="backward_tiles", grid=(t // tm,),
        in_specs=[tile(D_IN), tile(D), tile(D), tile(D), tile(D), _resident((D, D)), _resident((D, D)), _resident((D, D)),
                  _full((8, D)), _full((N_GROUPS, CHUNK, CHUNK)), _full((N_GROUPS, CHUNK, CHUNK)), _full((CHUNK, D))],
        out_specs=[tile(D), tile(D_IN), tile(D), tile(D), _full((8, D)), _full((8, D_IN)),
                   _full((CHUNK, D)), _full((CHUNK, D))],
        out_shape=[jax.ShapeDtypeStruct((t, D), F32), jax.ShapeDtypeStruct((t, D_IN), BF16),
                   jax.ShapeDtypeStruct((t, D), BF16), jax.ShapeDtypeStruct((t, D), BF16),
                   jax.ShapeDtypeStruct((8, D), F32), jax.ShapeDtypeStruct((8, D_IN), F32),
                   jax.ShapeDtypeStruct((CHUNK, D), F32), jax.ShapeDtypeStruct((CHUNK, D), F32)],
        compiler_params=_params("arbitrary"),
    )(p, h1, ya, yb, drb, wpa, wpb, wo, vecs, ws, wst, bsp)


def _conv_backward(dh1, p, dp, convw, pairs, tiles_per_seq):
    t = dh1.shape[0]
    tm = TOKEN_TILE
    hb = tm // HALO
    last = t // HALO - 1
    n_sq = len(pairs)
    span = 2
    rows = D // 8

    def body(dh1_ref, dnext_ref, p_ref, ph_ref, cw_ref, dp_in_ref, *refs):
        del dp_in_ref
        sq_in = refs[:2 * n_sq]
        dp_ref, dcw_ref, dbin_ref = refs[2 * n_sq:2 * n_sq + 3]
        sq_out = refs[2 * n_sq + 3:3 * n_sq + 3]
        sq_wire = refs[3 * n_sq + 3:4 * n_sq + 3]
        de_ref, he_ref, dh0_ref, sh_ref, acc_ref, wire_ref, sq_sem, wire_sem = refs[4 * n_sq + 3:]
        i = pl.program_id(0)

        @pl.when(i == 0)
        def _():
            dcw_ref[...] = jnp.zeros_like(dcw_ref)
            dbin_ref[...] = jnp.zeros_like(dbin_ref)
            acc_ref[...] = jnp.zeros_like(acc_ref)

        @pl.when(i % span == span - 1)
        def _():
            for a in range(n_sq):
                acc_ref[a] += _dot_tn(sq_in[2 * a][...], sq_in[2 * a + 1][...])

        keep_next = jnp.where(i % tiles_per_seq == tiles_per_seq - 1, 0.0, 1.0)
        de_ref[0:tm, :] = dh1_ref[...]
        de_ref[tm:, :] = dnext_ref[...] * keep_next
        _conv_taps(de_ref, sh_ref, cw_ref, CONV_K - 1, -1, None, dh0_ref, tm)

        keep_prev = jnp.where(i % tiles_per_seq == 0, 0.0, 1.0)
        sg = _sigmoid(p_ref[:, D:2 * D])
        val = p_ref[:, 0:D]
        he_ref[0:HALO, :] = ph_ref[:, 0:D] * _sigmoid(ph_ref[:, D:2 * D]) * keep_prev
        he_ref[HALO:, :] = val * sg
        _conv_weight_grad(dh1_ref, he_ref, sh_ref, HALO - (CONV_K - 1), dcw_ref, tm)
        dcw_ref[SUBLANES * CONV_K:, :] += jnp.sum(dh1_ref[...].reshape(tm // SUBLANES, SUBLANES, D), axis=0)

        dh0 = dh0_ref[...]
        dval = dh0 * sg
        dglu = dval * val * (1.0 - sg)
        dbin_ref[0:1, 0:D] += _colsum(dval)
        dbin_ref[0:1, D:2 * D] += _colsum(dglu)
        dp_ref[:, 0:D] = dval.astype(BF16)
        dp_ref[:, D:2 * D] = dglu.astype(BF16)

        @pl.when(i == t // tm - 1)
        def _():
            cps = [pltpu.make_async_copy(acc_ref.at[a, pl.ds((2 * j + h) * rows, rows)], sq_out[a].at[h, j],
                                         sq_sem.at[(a * N_CHIPS + j) * 2 + h])
                   for a in range(n_sq) for j in range(N_CHIPS) for h in range(2)]
            for cp in cps:
                cp.start()
            for a in range(n_sq):
                wire_ref[...] = acc_ref[a].astype(BF16)
                narrow = [pltpu.make_async_copy(wire_ref.at[pl.ds((2 * j + h) * rows, rows)], sq_wire[a].at[h, j],
                                                wire_sem.at[2 * j + h]) for j in range(N_CHIPS) for h in range(2)]
                for cp in narrow:
                    cp.start()
                for cp in narrow:
                    cp.wait()
            for cp in cps:
                cp.wait()

    any_spec = pl.BlockSpec(memory_space=pl.ANY)
    wide = pl.BlockSpec((span * tm, D), lambda i: (i // span, 0))
    return pl.pallas_call(
        body, name="conv_backward", grid=(t // tm,),
        in_specs=[pl.BlockSpec((tm, D), lambda i: (i, 0)),
                  pl.BlockSpec((HALO, D), lambda i: (jnp.minimum((i + 1) * hb, last), 0)),
                  pl.BlockSpec((tm, 2 * D), lambda i: (i, 0)),
                  pl.BlockSpec((HALO, 2 * D), lambda i: (jnp.maximum(i * hb - 1, 0), 0)),
                  _full((HALO, D)), any_spec] + [wide] * (2 * n_sq),
        out_specs=[pl.BlockSpec((tm, 2 * D), lambda i: (i, 0)), _full((SUBLANES * HALO, D)), _full((8, 2 * D))]
        + [any_spec] * (2 * n_sq),
        out_shape=[jax.ShapeDtypeStruct(dp.shape, BF16), jax.ShapeDtypeStruct((SUBLANES * HALO, D), F32),
                   jax.ShapeDtypeStruct((8, 2 * D), F32)]
        + [jax.ShapeDtypeStruct((2, N_CHIPS, rows, D), F32)] * n_sq
        + [jax.ShapeDtypeStruct((2, N_CHIPS, rows, D), BF16)] * n_sq,
        scratch_shapes=[pltpu.VMEM((tm + HALO, D), F32), pltpu.VMEM((tm + HALO, D), F32), pltpu.VMEM((tm, D), F32),
                        pltpu.VMEM((SUBLANES - 1, tm + SHIFT_ROWS, GROUP_W), F32), pltpu.VMEM((n_sq, D, D), F32),
                        pltpu.VMEM((D, D), BF16), pltpu.SemaphoreType.DMA((n_sq * N_CHIPS * 2,)),
                        pltpu.SemaphoreType.DMA((N_CHIPS * 2,))],
        input_output_aliases={5: 0},
        compiler_params=_params("arbitrary"),
    )(dh1, dh1, p, p, convw, dp, *[a for pair in pairs for a in pair])


def _grad_in_and_x(xt, dp, w4, dr, wires, grads):
    t = dr.shape[0]
    tm = TOKEN_TILE
    half, tn = D // 2, 512
    nb = W_BLOCK // tn
    n_w, n_x = 2 * N_CHIPS * nb, t // tm
    ns = len(grads)
    xi, yi, ci = lax.axis_index("x"), lax.axis_index("y"), lax.axis_index("c")
    others = [2 * (1 - xi) + yi, 2 * xi + (1 - yi), 2 * (1 - xi) + (1 - yi)]
    blocks = others + others + [2 * xi + yi] * 2
    halves = [1 - ci] * 3 + [ci] * 3 + [1 - ci, ci]
    table = jnp.stack([jnp.stack([b * nb + n for b in blocks for n in range(nb)]),
                       jnp.stack([h for h in halves for _ in range(nb)])]).astype(jnp.int32)

    def body(tab_ref, xt_ref, dpc_ref, dpr_ref, w_ref, dr_ref, *refs):
        parts, fulls = refs[:ns], refs[ns:2 * ns]
        dx_ref, qk_ref, b2_ref, b1_ref, wire_ref = refs[2 * ns:2 * ns + 5]
        lands, sibs = refs[2 * ns + 5:3 * ns + 5], refs[3 * ns + 5:4 * ns + 5]
        (g_ref, st_ref, sb_ref, tmp_ref, d2d_send, d2d_recv, ici_send, ici_recv, own_sem, tmp_sem, wire_sem,
         p_send, p_recv, s_send, s_recv) = refs[4 * ns + 5:]
        s = pl.program_id(0)
        x_, y_, c, k = _position()
        chips = _other_chips(x_, y_)
        n = s % nb
        grp = s // nb
        cols = pl.ds(pl.multiple_of(n * tn, tn), tn)

        def part(a, r, core):
            cx, cy = chips[r]
            return pltpu.make_async_remote_copy(
                src_ref=parts[a].at[core, 2 * cx + cy], dst_ref=lands[a].at[2 * r + c],
                send_sem=p_send.at[6 * a + 2 * r + core], recv_sem=p_recv.at[6 * a + 2 * r + c],
                device_id=(cx, cy, core), device_id_type=MESH)

        def landed(a, r, core):
            cx, cy = chips[r]
            return pltpu.make_async_remote_copy(
                src_ref=lands[a].at[2 * r + core], dst_ref=lands[a].at[2 * r + core],
                send_sem=p_send.at[6 * a + 2 * r + core], recv_sem=p_recv.at[6 * a + 2 * r + core],
                device_id=(cx, cy, core), device_id_type=MESH)

        def to_sibling_whole(a):
            return pltpu.make_async_remote_copy(
                src_ref=fulls[a].at[1 - c, k], dst_ref=sibs[a], send_sem=s_send.at[a], recv_sem=s_recv.at[a],
                device_id=(x_, y_, 1 - c), device_id_type=MESH)

        def to_sibling(slot, land):
            return pltpu.make_async_remote_copy(
                src_ref=st_ref.at[slot], dst_ref=b1_ref.at[land, :, cols], send_sem=d2d_send.at[slot],
                recv_sem=d2d_recv.at[land * nb + n], device_id=(x_, y_, 1 - c), device_id_type=MESH)

        def to_chip(r):
            cx, cy = chips[r]
            return pltpu.make_async_remote_copy(
                src_ref=wire_ref.at[r, :, cols], dst_ref=b2_ref.at[r, :, cols], send_sem=ici_send.at[r],
                recv_sem=ici_recv.at[r], device_id=(cx, cy, c), device_id_type=MESH)

        def all_of_chip(r):
            cx, cy = chips[r]
            return pltpu.make_async_remote_copy(
                src_ref=wire_ref.at[r], dst_ref=b2_ref.at[r], send_sem=ici_send.at[r],
                recv_sem=ici_recv.at[r], device_id=(cx, cy, c), device_id_type=MESH)

        def to_result(slot):
            return pltpu.make_async_copy(st_ref.at[slot], qk_ref.at[:, cols], own_sem.at[slot])

        def sibling_piece(land):
            return pltpu.make_async_copy(b1_ref.at[land, :, cols], tmp_ref, tmp_sem)

        @pl.when(s == 0)
        def _():
            for a in range(ns):
                to_sibling_whole(a).start()
                for r in range(3):
                    for core in range(2):
                        part(a, r, core).start()

        own_half = ((grp >= 3) & (grp <= 5)) | (grp == 7)
        land = jnp.where(grp == 7, 3, grp - 3)

        @pl.when(own_half)
        def _():
            to_sibling(0, land).wait_recv()
            sibling_piece(land).start()

        @pl.when(s < n_w)
        def _():
            g_ref[...] = _dot(xt_ref[tab_ref[1, s]], dpc_ref[...])

        @pl.when(own_half)
        def _():
            sibling_piece(land).wait()

        for g in range(2 * N_CHIPS):
            @pl.when(grp == g)
            def _(g=g):
                if g in (0, 1, 2, 6):
                    use = s if g < 3 else 3 * nb + n
                    slot = use % 2

                    @pl.when(use >= 2)
                    def _():
                        to_sibling(slot, 0).wait_send()

                    st_ref[slot] = g_ref[...]
                    to_sibling(slot, min(g, 3)).start()
                elif g in (3, 4, 5):
                    sb_ref[...] = (g_ref[...] + tmp_ref[...]).astype(BF16)
                    stage = pltpu.make_async_copy(sb_ref, wire_ref.at[g - 3, :, cols], wire_sem)
                    stage.start()
                    stage.wait()
                    to_chip(g - 3).start()
                else:
                    slot = n % 2
                    piece = g_ref[...] + tmp_ref[...]

                    @pl.when(n < 2)
                    def _():
                        to_sibling(slot, 0).wait_send()

                    @pl.when(n >= 2)
                    def _():
                        to_result(slot).wait()

                    st_ref[slot] = piece
                    to_result(slot).start()

        @pl.when(s >= n_w)
        def _():
            acc = ALPHA * dr_ref[...]
            for j in range(N_CHIPS):
                acc = acc + _dot_nt(dpr_ref[:, j * W_BLOCK:(j + 1) * W_BLOCK], w_ref[j])
            dx_ref[...] = acc

        @pl.when(s == n_w + n_x - 1)
        def _():
            for slot in range(2):
                to_result(slot).wait()
            for r in range(3):
                all_of_chip(r).wait_recv()
                all_of_chip(r).wait_send()
            for a in range(ns):
                to_sibling_whole(a).wait_recv()
                to_sibling_whole(a).wait_send()
                for r in range(3):
                    for core in range(2):
                        landed(a, r, core).wait_recv()
                        part(a, r, core).wait_send()

    any_spec = pl.BlockSpec(memory_space=pl.ANY)
    tile = lambda s, tab: (jnp.maximum(s - n_w, 0), 0)
    return pl.pallas_call(
        body, name="grad_in_and_x",
        grid_spec=pltpu.PrefetchScalarGridSpec(
            num_scalar_prefetch=1, grid=(n_w + n_x,),
            in_specs=[pl.BlockSpec((2, half, t), lambda s, tab: (0, 0, 0), pipeline_mode=pl.Buffered(1)),
                      pl.BlockSpec((t, tn), lambda s, tab: (0, tab[0, jnp.minimum(s, n_w - 1)])),
                      pl.BlockSpec((tm, D_IN), tile),
                      pl.BlockSpec((N_CHIPS, D, W_BLOCK), lambda s, tab: (0, 0, 0), pipeline_mode=pl.Buffered(1)),
                      pl.BlockSpec((tm, D), tile)] + [any_spec] * (2 * ns),
            out_specs=[pl.BlockSpec((tm, D), tile)] + [any_spec] * (4 + 2 * ns),
            scratch_shapes=[pltpu.VMEM((half, tn), F32), pltpu.VMEM((2, half, tn), F32), pltpu.VMEM((half, tn), BF16),
                            pltpu.VMEM((half, tn), F32),
                            pltpu.SemaphoreType.DMA((2,)), pltpu.SemaphoreType.DMA((N_CHIPS * nb,)),
                            pltpu.SemaphoreType.DMA((3,)), pltpu.SemaphoreType.DMA((3,)),
                            pltpu.SemaphoreType.DMA((2,)), pltpu.SemaphoreType.DMA, pltpu.SemaphoreType.DMA,
                            pltpu.SemaphoreType.DMA((6 * ns,)), pltpu.SemaphoreType.DMA((6 * ns,)),
                            pltpu.SemaphoreType.DMA((ns,)), pltpu.SemaphoreType.DMA((ns,))]),
        out_shape=[jax.ShapeDtypeStruct((t, D), F32), jax.ShapeDtypeStruct((half, W_BLOCK), F32),
                   jax.ShapeDtypeStruct((3, half, W_BLOCK), BF16), jax.ShapeDtypeStruct((N_CHIPS, half, W_BLOCK), F32),
                   jax.ShapeDtypeStruct((3, half, W_BLOCK), BF16)]
        + [jax.ShapeDtypeStruct((6,) + w.shape[2:], w.dtype) for w in wires]
        + [jax.ShapeDtypeStruct(g.shape[2:], F32) for g in grads],
        compiler_params=_params("arbitrary"),
    )(table, xt, dp, dp, w4, dr, *wires, *grads)


def kernel(x, w_in, b_in, conv_w, conv_b, gn_g, gn_b, ln_v_g, ln_v_b, w_spatial, b_spatial, w_pa, w_pb, w_o, b_o, ln_out_g, ln_out_b, loss_target, m_w_in, m_b_in, m_conv_w, m_conv_b, m_gn_g, m_gn_b, m_ln_v_g, m_ln_v_b, m_w_spatial, m_b_spatial, m_w_pa, m_w_pb, m_w_o, m_b_o, m_ln_out_g, m_ln_out_b, v_w_in, v_b_in, v_conv_w, v_conv_b, v_gn_g, v_gn_b, v_ln_v_g, v_ln_v_b, v_w_spatial, v_b_spatial, v_w_pa, v_w_pb, v_w_o, v_b_o, v_ln_out_g, v_ln_out_b):
    n_seq, seq, _ = x.shape
    t = n_seq * seq
    tiles_per_seq = seq // TOKEN_TILE
    x2 = x.reshape(t, D)
    tgt = loss_target.reshape(t, D)

    conv_shard = jnp.pad(conv_w, ((0, HALO - CONV_K), (0, 0)))
    p, win4, wpa4, wpb4, wo4, conv4 = _proj_gather(
        x2, b_in,
        _place_shards([w_in, w_pa, w_pb, w_o, conv_shard], [BF16, BF16, BF16, BF16, F32]))
    win4 = win4.reshape(N_CHIPS, D, W_BLOCK)
    wpa, wpb, wo = wpa4.reshape(D, D), wpb4.reshape(D, D), wo4.reshape(D, D)
    convw = conv4.reshape(N_CHIPS, HALO, D // N_CHIPS).transpose(1, 0, 2).reshape(HALO, D)

    vecs = jnp.stack([conv_b, gn_g, gn_b, ln_v_g, ln_v_b, b_o, ln_out_g, ln_out_b])
    causal = jnp.tril(jnp.ones((CHUNK, CHUNK), bool))
    ws = jnp.where(causal[None], w_spatial, 0.0)
    ws_bf, wst_bf = ws.astype(BF16), ws.transpose(0, 2, 1).astype(BF16)
    bsp = jnp.repeat(b_spatial.T, GROUP_W, axis=1)

    h1, ya, yb, h3, s, mixed, dr, drb, xt, acc_f = _forward_tiles(p, x2, tgt, wpa, wpb, wo, convw, vecs, ws_bf, bsp, tiles_per_seq)
    dh1, dp, dya, dyb, acc_b, dbin_b, dws, dbsp_acc = _backward_tiles(p, h1, ya, yb, drb, wpa, wpb, wo, vecs, ws_bf, wst_bf, bsp)
    dp, dcw8, dbin_a, *square = _conv_backward(dh1, p, dp, convw, [(h3, dya), (s, dyb), (mixed, drb)], tiles_per_seq)

    small, g_conv = _pack_small(acc_f, acc_b, dbin_a, dbin_b, dcw8, dws, dbsp_acc)
    small = small.reshape(2, N_CHIPS, SMALL_ROWS // 8, D)

    grads = square[:3] + [g_conv, small]
    wires = square[3:] + [g_conv, small]
    grad_x, q_in, chips_in, _, _, *landed = _grad_in_and_x(xt.reshape(2, D // 2, t), dp, win4, dr, wires, grads)
    grad_x = grad_x.reshape(x.shape)
    mine = [_add_chips(q_in, chips_in)]
    mine += _add_devices(grads, landed[:5], landed[5:])
    *full, small_parts = _share_results(mine[:5], mine[5])
    grad_w_in, grad_w_pa, grad_w_pb, grad_w_o = [f.reshape(w.shape) for f, w in zip(full[:4], (w_in, w_pa, w_pb, w_o))]
    grad_conv_w = full[4].reshape(HALO, D // N_CHIPS)[:CONV_K]

    big = {"w_in": (grad_w_in,) + tuple(_adamw(w_in, grad_w_in, m_w_in, v_w_in))}
    group = _adamw_group([(w_pa, grad_w_pa, m_w_pa, v_w_pa), (w_pb, grad_w_pb, m_w_pb, v_w_pb),
                          (w_o, grad_w_o, m_w_o, v_w_o)], (conv_w, grad_conv_w, m_conv_w, v_conv_w))
    for name, g, res in zip(["w_pa", "w_pb", "w_o", "conv_w"], [grad_w_pa, grad_w_pb, grad_w_o, grad_conv_w], group):
        big[name] = (g,) + tuple(res)
    vec_names = ["conv_b", "gn_g", "gn_b", "ln_v_g", "ln_v_b", "b_o", "ln_out_g", "ln_out_b"]
    vec_triples = [(conv_b, m_conv_b, v_conv_b), (gn_g, m_gn_g, v_gn_g), (gn_b, m_gn_b, v_gn_b),
                   (ln_v_g, m_ln_v_g, v_ln_v_g), (ln_v_b, m_ln_v_b, v_ln_v_b), (b_o, m_b_o, v_b_o),
                   (ln_out_g, m_ln_out_g, v_ln_out_g), (ln_out_b, m_ln_out_b, v_ln_out_b)]
    small_res, loss8 = _adamw_small(
        small_parts.reshape(8, SMALL_ROWS // 8, D), vec_triples, (b_in, m_b_in, v_b_in),
        (w_spatial, m_w_spatial, v_w_spatial), (b_spatial, m_b_spatial, v_b_spatial))
    per_name = dict(zip(vec_names + ["b_in", "w_spatial", "b_spatial"], small_res))

    order = ["w_in", "b_in", "conv_w", "conv_b", "gn_g", "gn_b", "ln_v_g", "ln_v_b", "w_spatial", "b_spatial",
             "w_pa", "w_pb", "w_o", "b_o", "ln_out_g", "ln_out_b"]
    outs = [loss8[0, 0], grad_x]
    for kind in range(4):
        outs += [big[n][kind] if n in big else per_name[n][kind] for n in order]
    return tuple(outs)
```

```python
import math

import jax
import jax.numpy as jnp
from jax import lax
from jax.experimental import pallas as pl
from jax.experimental.pallas import tpu as pltpu

D = 1024
N_GROUPS = 8
GROUP_W = D // N_GROUPS
CHUNK = 128
CONV_K = 31
HALO = 32
D_IN = 8 * D
N_CHIPS = 4
W_BLOCK = D_IN // N_CHIPS
ALPHA = 2.0 ** 0.25
LN_EPS = 1e-5
ADAM_LR, ADAM_B1, ADAM_B2, ADAM_EPS, ADAM_WD, ADAM_STEP = 0.001, 0.9, 0.999, 1e-08, 0.01, 10

TOKEN_TILE = 256
VMEM_LIMIT = 56 * 1024 * 1024
MESH = pl.DeviceIdType.MESH
F32, BF16 = jnp.float32, jnp.bfloat16


def _sigmoid(x):
    return jax.nn.sigmoid(x)


GELU_C = math.sqrt(2.0 / math.pi)
GELU_CA = GELU_C * 0.044715


def _gelu(x):
    t = jnp.tanh(x * (GELU_C + GELU_CA * (x * x)))
    return x * (0.5 + 0.5 * t)


def _gelu_and_grad(x):
    x2 = x * x
    t = jnp.tanh(x * (GELU_C + GELU_CA * x2))
    cdf = 0.5 + 0.5 * t
    return x * cdf, cdf + (0.5 * x) * (1.0 - t * t) * (GELU_C + (3.0 * GELU_CA) * x2)


def _norm_stats(v):
    mu = jnp.mean(v, axis=-1, keepdims=True)
    vc = v - mu
    var = jnp.mean(vc * vc, axis=-1, keepdims=True)
    rstd = lax.rsqrt(var + LN_EPS)
    return vc * rstd, rstd


def _norm_bwd(dxhat, xhat, rstd):
    m1 = jnp.mean(dxhat, axis=-1, keepdims=True)
    m2 = jnp.mean(dxhat * xhat, axis=-1, keepdims=True)
    return rstd * (dxhat - m1 - xhat * m2)


def _dot(a, b):
    return jnp.dot(a, b, preferred_element_type=F32)


def _dot_nt(a, b):
    return lax.dot_general(a, b, (((1,), (1,)), ((), ())), preferred_element_type=F32)


def _dot_tn(a, b):
    return lax.dot_general(a, b, (((0,), (0,)), ((), ())), preferred_element_type=F32)


def _colsum(v):
    return jnp.sum(v, axis=0, keepdims=True)


def _full(shape):
    return pl.BlockSpec(shape, lambda *_: (0,) * len(shape))


def _resident(shape):
    return pl.BlockSpec(shape, lambda *_: (0,) * len(shape), pipeline_mode=pl.Buffered(1))


def _params(*sem):
    return pltpu.CompilerParams(dimension_semantics=sem, vmem_limit_bytes=VMEM_LIMIT)


def _chip_index():
    return (2 * lax.axis_index("x") + lax.axis_index("y")).astype(jnp.int32).reshape(1)


def _core_index():
    return lax.axis_index("c").astype(jnp.int32).reshape(1)


def _place_shards(ws, dtypes):
    n = len(ws)

    def body(k_ref, *refs):
        for w_ref, o_ref, dtype in zip(refs[:n], refs[n:], dtypes):
            rows = w_ref.shape[0] // 2
            for h in range(2):
                o_ref[h] = w_ref[h * rows:(h + 1) * rows, :].astype(dtype)

    return pl.pallas_call(
        body, name="place_shards",
        grid_spec=pltpu.PrefetchScalarGridSpec(
            num_scalar_prefetch=1, grid=(1,),
            in_specs=[pl.BlockSpec(w.shape, lambda i, k: (0, 0)) for w in ws],
            out_specs=[pl.BlockSpec((None, 2, w.shape[0] // 2, w.shape[1]), lambda i, k: (k[0], 0, 0, 0)) for w in ws]),
        out_shape=[jax.ShapeDtypeStruct((N_CHIPS, 2, w.shape[0] // 2, w.shape[1]), dt) for w, dt in zip(ws, dtypes)],
        compiler_params=_params("arbitrary"),
    )(_chip_index(), *ws)


def _position():
    x, y, c = lax.axis_index("x"), lax.axis_index("y"), lax.axis_index("c")
    return x, y, c, 2 * x + y


def _other_chips(x, y):
    return [(1 - x, y), (x, 1 - y), (1 - x, 1 - y)]


def _any_specs(n):
    return [pl.BlockSpec(memory_space=pl.ANY)] * n


def _proj_gather(x, b_in, bufs):
    t = x.shape[0]
    tm = 1024
    steps = t // tm
    ahead = 3 * steps // 4
    half = D // 2
    chunk = W_BLOCK // 2
    n = len(bufs)
    xi, yi = lax.axis_index("x"), lax.axis_index("y")
    chips = [2 * xi + yi, 2 * (1 - xi) + yi, 2 * xi + (1 - yi), 2 * (1 - xi) + (1 - yi)]
    plan = [(0, 0), (0, 1), (1, 0), (2, 1), (1, 1), (2, 0), (3, 0), (3, 1)]
    order = jnp.stack([2 * chips[ch] + q for ch, q in plan]).astype(jnp.int32)

    def body(order_ref, x_ref, b_ref, *refs):
        p_ref, outs = refs[n], refs[n + 1:2 * n + 1]
        xb_ref, w_ref, lsem, send, recv, hop_send, hop_recv, fsend, frecv, qsend, qrecv = refs[2 * n + 1:]
        jj, i = pl.program_id(0), pl.program_id(1)
        x_, y_, c, k = _position()
        nbrs = [(1 - x_, y_), (x_, 1 - y_)]
        blocks = [2 * (1 - x_) + y_, 2 * x_ + (1 - y_), 2 * (1 - x_) + (1 - y_)]

        def quarter(a, block, q, h):
            if a == 0:
                return outs[0].at[block, h, :, pl.ds(q * chunk, chunk)]
            rows = outs[a].shape[2] // 2
            return outs[a].at[block, h, pl.ds(q * rows, rows)]

        def copy(ref, to, send_sem, recv_sem):
            return pltpu.make_async_remote_copy(src_ref=ref, dst_ref=ref, send_sem=send_sem, recv_sem=recv_sem,
                                                device_id=(to[0], to[1], c), device_id_type=MESH)

        def sent(a, nb, q):
            return copy(quarter(a, k, q, c), nbrs[nb], send.at[4 * a + 2 * nb + q], recv.at[4 * a + 2 * nb + q])

        def landed(a, nb, q):
            return copy(quarter(a, blocks[nb], q, c), nbrs[nb], send.at[4 * a + 2 * nb + q], recv.at[4 * a + 2 * nb + q])

        def hopped(a, nb):
            return copy(quarter(a, blocks[nb], nb, c), nbrs[1 - nb], hop_send.at[2 * a + nb], hop_recv.at[2 * a + 1 - nb])

        def from_diagonal(a, via):
            return copy(quarter(a, blocks[2], 1 - via, c), nbrs[via], hop_send.at[2 * a + via], hop_recv.at[2 * a + via])

        def passed(a, r, h):
            return pltpu.make_async_remote_copy(
                src_ref=outs[a].at[blocks[r], h], dst_ref=outs[a].at[blocks[r], h], send_sem=fsend.at[3 * a + r],
                recv_sem=frecv.at[3 * a + r], device_id=(x_, y_, 1 - c), device_id_type=MESH)

        def passed_quarter(r, q, h):
            ref = quarter(0, blocks[r], q, h)
            return pltpu.make_async_remote_copy(
                src_ref=ref, dst_ref=ref, send_sem=qsend.at[2 * r + q], recv_sem=qrecv.at[2 * r + q],
                device_id=(x_, y_, 1 - c), device_id_type=MESH)

        def load(block, q, slot):
            return [pltpu.make_async_copy(quarter(0, block, q, h), w_ref.at[slot, pl.ds(h * half, half)],
                                          lsem.at[2 * slot + h]) for h in range(2)]

        def pass_on(arrays):
            for a in arrays:
                for nb in range(2):
                    landed(a, nb, nb).wait_recv()
                    hopped(a, nb).start()

        @pl.when((jj == 0) & (i == 0))
        def _():
            for a in range(n):
                for nb, q in ((0, 0), (1, 1), (0, 1), (1, 0)):
                    sent(a, nb, q).start()
            for cp in load(k, 0, 0):
                cp.start()

        for nxt in range(1, len(plan)):
            @pl.when((jj == nxt - 1) & (i == ahead))
            def _(nxt=nxt):
                ch, q = plan[nxt]
                if ch in (1, 2):
                    landed(0, ch - 1, q).wait_recv()
                    if q == ch - 1:
                        hopped(0, ch - 1).start()
                elif ch == 3:
                    from_diagonal(0, 1 - q).wait_recv()
                if ch:
                    passed_quarter(ch - 1, q, c).start()
                    passed_quarter(ch - 1, q, 1 - c).wait_recv()
                for cp in load(k if ch == 0 else blocks[ch - 1], q, nxt % 2):
                    cp.start()
                if nxt == 5:
                    pass_on(range(1, n))
                    for a in range(1, n):
                        landed(a, 0, 1).wait_recv()
                        passed(a, 0, c).start()
                        landed(a, 1, 0).wait_recv()
                        passed(a, 1, c).start()

        slot = jj % 2

        @pl.when(i == 0)
        def _():
            for cp in load(k, 0, slot):
                cp.wait()

        rows = pl.ds(pl.multiple_of(i * tm, tm), tm)

        @pl.when(jj == 0)
        def _():
            xb_ref[rows, :] = x_ref[...].astype(BF16)

        p_ref[...] = _dot(xb_ref[rows, :], w_ref[slot]) + b_ref[...]

        @pl.when((jj == len(plan) - 1) & (i == steps - 1))
        def _():
            for a in range(1, n):
                from_diagonal(a, 0).wait_recv()
                from_diagonal(a, 1).wait_recv()
                passed(a, 2, c).start()
            for a in range(1, n):
                for r in range(3):
                    passed(a, r, 1 - c).wait_recv()
                    passed(a, r, c).wait_send()
            for r in range(3):
                for q in range(2):
                    passed_quarter(r, q, c).wait_send()
            for a in range(n):
                for nb in range(2):
                    for q in range(2):
                        sent(a, nb, q).wait_send()
                    hopped(a, nb).wait_send()

    any_spec = pl.BlockSpec(memory_space=pl.ANY)
    return pl.pallas_call(
        body, name="proj_gather",
        grid_spec=pltpu.PrefetchScalarGridSpec(
            num_scalar_prefetch=1, grid=(len(plan), steps),
            in_specs=[pl.BlockSpec((tm, D), lambda jj, i, o: (jnp.where(jj == 0, i, steps - 1), 0)),
                      pl.BlockSpec((None, 1, chunk), lambda jj, i, o: (o[jj], 0, 0))] + [any_spec] * n,
            out_specs=[pl.BlockSpec((tm, chunk), lambda jj, i, o: (i, o[jj]))] + [any_spec] * n,
            scratch_shapes=[pltpu.VMEM((t, D), BF16), pltpu.VMEM((2, D, chunk), BF16), pltpu.SemaphoreType.DMA((4,)),
                            pltpu.SemaphoreType.DMA((4 * n,)), pltpu.SemaphoreType.DMA((4 * n,)),
                            pltpu.SemaphoreType.DMA((2 * n,)), pltpu.SemaphoreType.DMA((2 * n,)),
                            pltpu.SemaphoreType.DMA((3 * n,)), pltpu.SemaphoreType.DMA((3 * n,)),
                            pltpu.SemaphoreType.DMA((6,)), pltpu.SemaphoreType.DMA((6,))]),
        out_shape=[jax.ShapeDtypeStruct((t, D_IN), F32)] + [jax.ShapeDtypeStruct(b.shape, b.dtype) for b in bufs],
        input_output_aliases={3 + a: 1 + a for a in range(n)},
        compiler_params=_params("arbitrary", "arbitrary"),
    )(order, x, b_in.reshape(D_IN // chunk, 1, chunk), *bufs)


def _share_results(bufs, small):
    n = len(bufs)

    def body(*refs):
        outs, small_out = refs[n + 1:2 * n + 1], refs[2 * n + 1]
        send, recv, ssend, srecv = refs[2 * n + 2:]
        x, y, c, k = _position()
        cps = []
        for a in range(n):
            cp = pltpu.make_async_remote_copy(
                src_ref=outs[a].at[c], dst_ref=outs[a].at[c], send_sem=send.at[a], recv_sem=recv.at[a],
                device_id=(x, y, 1 - c), device_id_type=MESH)
            cp.start()
            cps.append(cp)
        waits = []
        for p in range(1, 8):
            px, py, pc = x ^ (p >> 2), y ^ ((p >> 1) & 1), c ^ (p & 1)
            cp = pltpu.make_async_remote_copy(
                src_ref=small_out.at[k, c], dst_ref=small_out.at[k, c], send_sem=ssend.at[p - 1],
                recv_sem=srecv.at[p - 1], device_id=(px, py, pc), device_id_type=MESH)
            cp.start()
            cps.append(cp)
            waits.append(pltpu.make_async_remote_copy(
                src_ref=small_out.at[2 * px + py, pc], dst_ref=small_out.at[2 * px + py, pc], send_sem=ssend.at[p - 1],
                recv_sem=srecv.at[p - 1], device_id=(px, py, pc), device_id_type=MESH))
        for a in range(n):
            pltpu.make_async_remote_copy(
                src_ref=outs[a].at[1 - c], dst_ref=outs[a].at[1 - c], send_sem=send.at[a], recv_sem=recv.at[a],
                device_id=(x, y, 1 - c), device_id_type=MESH).wait_recv()
        for w in waits:
            w.wait_recv()
        for cp in cps:
            cp.wait_send()

    return pl.pallas_call(
        body, name="rs_share_results",
        in_specs=_any_specs(n + 1), out_specs=_any_specs(n + 1),
        out_shape=[jax.ShapeDtypeStruct(b.shape, b.dtype) for b in bufs + [small]],
        scratch_shapes=[pltpu.SemaphoreType.DMA((n,)), pltpu.SemaphoreType.DMA((n,)),
                        pltpu.SemaphoreType.DMA((7,)), pltpu.SemaphoreType.DMA((7,))],
        input_output_aliases={a: a for a in range(n + 1)},
    )(*bufs, small)


def _row_tile(r, c):
    t = max(8, min(r, (1 << 19) // c))
    while r % t:
        t //= 2
    return t


def _add_devices(grads, lands, sibs):
    n = len(grads)

    def body(kc_ref, *refs):
        for g_ref, l_ref, s_ref, f_ref in zip(refs[:n], refs[n:2 * n], refs[2 * n:3 * n], refs[3 * n:]):
            f = g_ref[...] + s_ref[...]
            for i in range(l_ref.shape[0]):
                f = f + l_ref[i].astype(F32)
            f_ref[...] = f

    shapes = [g.shape[2:] for g in grads]
    out_specs = [pl.BlockSpec((None,) + sh, lambda i, kc: (kc[1], 0, 0)) for sh in shapes[:-1]]
    out_specs.append(pl.BlockSpec((None, None) + shapes[-1], lambda i, kc: (kc[0], kc[1], 0, 0)))
    out_shape = [jax.ShapeDtypeStruct((2,) + sh, F32) for sh in shapes[:-1]]
    out_shape.append(jax.ShapeDtypeStruct((N_CHIPS, 2) + shapes[-1], F32))
    return pl.pallas_call(
        body, name="rs_add_devices",
        grid_spec=pltpu.PrefetchScalarGridSpec(
            num_scalar_prefetch=1, grid=(1,),
            in_specs=[pl.BlockSpec((None, None) + sh, lambda i, kc: (kc[1], kc[0], 0, 0)) for sh in shapes]
            + [pl.BlockSpec(l.shape, lambda i, kc: (0, 0, 0)) for l in lands]
            + [pl.BlockSpec(sh, lambda i, kc: (0, 0)) for sh in shapes],
            out_specs=out_specs),
        out_shape=out_shape,
        compiler_params=_params("arbitrary"),
    )(jnp.concatenate([_chip_index(), _core_index()]), *grads, *lands, *sibs)


def _add_chips(q, b2):
    r, c = q.shape
    t = _row_tile(r, c)

    def body(c_ref, q_ref, b_ref, f_ref):
        f_ref[...] = ((q_ref[...] + b_ref[0].astype(F32)) + b_ref[1].astype(F32)) + b_ref[2].astype(F32)

    return pl.pallas_call(
        body, name="rs_add_chips",
        grid_spec=pltpu.PrefetchScalarGridSpec(
            num_scalar_prefetch=1, grid=(r // t,),
            in_specs=[pl.BlockSpec((t, c), lambda i, cr: (i, 0)), pl.BlockSpec((3, t, c), lambda i, cr: (0, i, 0))],
            out_specs=pl.BlockSpec((None, t, c), lambda i, cr: (cr[0], i, 0))),
        out_shape=jax.ShapeDtypeStruct((2, r, c), F32),
        compiler_params=_params("parallel"),
    )(_core_index(), q, b2)


def _adamw_math(w, g, m, v):
    m = ADAM_B1 * m + (1.0 - ADAM_B1) * g
    v = ADAM_B2 * v + (1.0 - ADAM_B2) * (g * g)
    m_hat = m / (1.0 - ADAM_B1 ** ADAM_STEP)
    v_hat = v / (1.0 - ADAM_B2 ** ADAM_STEP)
    delta = -ADAM_LR * (m_hat / (jnp.sqrt(v_hat) + ADAM_EPS) + ADAM_WD * w)
    return delta, m, v


def _adamw_group(quads, conv):
    n = len(quads)
    r, c = quads[0][0].shape

    def body(*refs):
        ins, outs = refs[:4 * (n + 1)], refs[4 * (n + 1):]
        for i in range(n + 1):
            w_ref, g_ref, m_ref, v_ref = ins[4 * i:4 * i + 4]
            res = _adamw_math(w_ref[...], g_ref[...], m_ref[...], v_ref[...])
            for o_ref, val in zip(outs[3 * i:3 * i + 3], res):
                o_ref[...] = val

    half = pl.BlockSpec((r // 2, c), lambda i: (i, 0))
    whole = pl.BlockSpec(conv[0].shape, lambda i: (0, 0))
    res = pl.pallas_call(
        body, name="adamw_group", grid=(2,),
        in_specs=[half] * (4 * n) + [whole] * 4, out_specs=[half] * (3 * n) + [whole] * 3,
        out_shape=[jax.ShapeDtypeStruct((r, c), F32)] * (3 * n) + [jax.ShapeDtypeStruct(conv[0].shape, F32)] * 3,
        compiler_params=_params("arbitrary"),
    )(*[a for quad in quads for a in quad], *conv)
    return [res[3 * i:3 * i + 3] for i in range(n + 1)]


def _adamw(w, g, m, v):
    r, c = w.shape
    t = _row_tile(r, c) if r % 8 == 0 else r

    def body(w_ref, g_ref, m_ref, v_ref, d_ref, nm_ref, nv_ref):
        d_ref[...], nm_ref[...], nv_ref[...] = _adamw_math(w_ref[...], g_ref[...], m_ref[...], v_ref[...])

    spec = pl.BlockSpec((t, c), lambda i: (i, 0))
    return pl.pallas_call(
        body, name="adamw", grid=(r // t,), in_specs=[spec] * 4, out_specs=[spec] * 3,
        out_shape=[jax.ShapeDtypeStruct((r, c), F32)] * 3, compiler_params=_params("parallel"),
    )(w, g, m, v)


ROW_B_IN = 0
ROW_VECS = 8
ROW_LOSS = 16
ROW_B_SPATIAL = 24
ROW_W_SPATIAL = 32
SMALL_ROWS = 192
N_VECS = 8


def _pack_small(acc_f, acc_b, dbin_a, dbin_b, dcw8, dws, dbsp):
    cols = D // N_CHIPS

    def body(af_ref, ab_ref, da_ref, db_ref, cw_ref, ws_ref, bs_ref, o_ref, gc_ref):
        o_ref[...] = jnp.zeros_like(o_ref)
        for j in range(D_IN // D):
            src = da_ref if j < 2 else db_ref
            o_ref[ROW_B_IN + j:ROW_B_IN + j + 1, :] = src[0:1, j * D:(j + 1) * D]
        dcw = jnp.sum(cw_ref[...].reshape(HALO, SUBLANES, D), axis=1)
        o_ref[ROW_VECS:ROW_VECS + 1, :] = dcw[CONV_K:CONV_K + 1]
        o_ref[ROW_VECS + 1:ROW_VECS + 5, :] = ab_ref[0:4, :]
        o_ref[ROW_VECS + 5:ROW_VECS + 6, :] = af_ref[2:3, :]
        o_ref[ROW_VECS + 6:ROW_VECS + 8, :] = af_ref[0:2, :]
        o_ref[ROW_LOSS:ROW_LOSS + 1, :] = af_ref[3:4, :]
        head = lax.broadcasted_iota(jnp.int32, (N_GROUPS, D), 0)
        lane = lax.broadcasted_iota(jnp.int32, (N_GROUPS, D), 1)
        indicator = jnp.where(lane // GROUP_W == head, 1.0, 0.0)
        o_ref[ROW_B_SPATIAL:ROW_B_SPATIAL + N_GROUPS, 0:CHUNK] = lax.dot_general(
            indicator, bs_ref[...], (((1,), (1,)), ((), ())), precision=lax.Precision.HIGHEST, preferred_element_type=F32)
        t_idx = lax.broadcasted_iota(jnp.int32, (CHUNK, D), 0)
        s_idx = lax.broadcasted_iota(jnp.int32, (CHUNK, D), 1) % CHUNK
        o_ref[ROW_W_SPATIAL:ROW_W_SPATIAL + CHUNK, :] = jnp.where(s_idx <= t_idx, ws_ref[...], 0.0)
        for h in range(2):
            for j in range(N_CHIPS):
                gc_ref[h, j] = dcw[h * (HALO // 2):(h + 1) * (HALO // 2), j * cols:(j + 1) * cols]

    ins = [acc_f, acc_b, dbin_a, dbin_b, dcw8, dws, dbsp]
    return pl.pallas_call(
        body, name="pack_small",
        in_specs=[_full(a.shape) for a in ins],
        out_specs=[_full((SMALL_ROWS, D)), _full((2, N_CHIPS, HALO // 2, cols))],
        out_shape=[jax.ShapeDtypeStruct((SMALL_ROWS, D), F32), jax.ShapeDtypeStruct((2, N_CHIPS, HALO // 2, cols), F32)],
        compiler_params=_params(),
    )(*ins)


def _adamw_small(parts, vecs, b_in, w_spatial, b_spatial):
    triples = list(vecs) + [b_in, w_spatial, b_spatial]
    n_in = 3 * len(triples)

    def body(p_ref, *refs):
        ins = [refs[3 * i:3 * i + 3] for i in range(len(triples))]
        outs = [refs[n_in + 4 * i:n_in + 4 * i + 4] for i in range(len(triples))]
        loss_ref, g_ref = refs[n_in + 4 * len(triples):]
        rows = SMALL_ROWS // 8
        for k in range(N_CHIPS):
            for core in range(2):
                g_ref[(core * N_CHIPS + k) * rows:(core * N_CHIPS + k + 1) * rows, :] = p_ref[2 * k + core]

        def step(g, wmv, out, get, put):
            d, nm, nv = _adamw_math(get(wmv[0]), g, get(wmv[1]), get(wmv[2]))
            for o, val in zip(out, (g, d, nm, nv)):
                put(o, val)

        for i in range(N_VECS):
            step(g_ref[ROW_VECS + i:ROW_VECS + i + 1, :], ins[i], outs[i],
                 lambda r: r[...].reshape(1, D), lambda o, val: o.__setitem__(Ellipsis, val.reshape(D)))
        for j in range(D_IN // D):
            piece = pl.ds(j * D, D)
            step(g_ref[ROW_B_IN + j:ROW_B_IN + j + 1, :], ins[N_VECS], outs[N_VECS],
                 lambda r: r[piece].reshape(1, D), lambda o, val: o.__setitem__(piece, val.reshape(D)))
        for h in range(N_GROUPS):
            step(g_ref[ROW_W_SPATIAL:ROW_W_SPATIAL + CHUNK, h * CHUNK:(h + 1) * CHUNK], ins[N_VECS + 1], outs[N_VECS + 1],
                 lambda r: r[h], lambda o, val: o.__setitem__(h, val))
        step(g_ref[ROW_B_SPATIAL:ROW_B_SPATIAL + N_GROUPS, 0:CHUNK], ins[N_VECS + 2], outs[N_VECS + 2],
             lambda r: r[...], lambda o, val: o.__setitem__(Ellipsis, val))
        lanes = g_ref[ROW_LOSS:ROW_LOSS + 1, :]
        loss_ref[...] = jnp.broadcast_to(jnp.sum(lanes, axis=1, keepdims=True), (8, 128))

    flat = [a for tr in triples for a in tr]
    out_shape = [jax.ShapeDtypeStruct(tr[0].shape, F32) for tr in triples for _ in range(4)]
    out_shape.append(jax.ShapeDtypeStruct((8, 128), F32))
    res = pl.pallas_call(
        body, name="adamw_small",
        in_specs=[_full(parts.shape)] + [_full(a.shape) for a in flat],
        out_specs=[_full(o.shape) for o in out_shape],
        out_shape=out_shape,
        scratch_shapes=[pltpu.VMEM((SMALL_ROWS, D), F32)],
        compiler_params=_params(),
    )(parts, *flat)
    return [res[4 * i:4 * i + 4] for i in range(len(triples))], res[-1]


SUBLANES = 8
SHIFT_ROWS = HALO - SUBLANES


def _shifted_copies(src_ref, sh_ref, cs, tm):
    for p in range(1, SUBLANES):
        sh_ref[p - 1] = src_ref[pl.ds(p, tm + SHIFT_ROWS), cs]


def _tap(src_ref, sh_ref, cs, offset, start, rows):
    p, q = offset % SUBLANES, offset // SUBLANES
    if p == 0:
        return src_ref[pl.ds(start + SUBLANES * q, rows), cs]
    return sh_ref[p - 1, pl.ds(start + SUBLANES * q, rows), :]


def _conv_taps(src_ref, sh_ref, w_ref, first_offset, step, bias, dst_ref, tm):
    rows = 64
    for g in range(N_GROUPS):
        cs = slice(g * GROUP_W, (g + 1) * GROUP_W)
        _shifted_copies(src_ref, sh_ref, cs, tm)
        for rb in range(tm // rows):
            acc = jnp.zeros((rows, GROUP_W), F32) + (bias[:, cs] if bias is not None else 0.0)
            for k in range(CONV_K):
                acc = acc + w_ref[k:k + 1, cs] * _tap(src_ref, sh_ref, cs, first_offset + step * k, rb * rows, rows)
            dst_ref[rb * rows:(rb + 1) * rows, cs] = acc


def _conv_weight_grad(d_ref, src_ref, sh_ref, first_offset, acc_ref, tm):
    rows = 64
    for g in range(N_GROUPS):
        cs = slice(g * GROUP_W, (g + 1) * GROUP_W)
        _shifted_copies(src_ref, sh_ref, cs, tm)
        for rb in range(tm // rows):
            d = d_ref[rb * rows:(rb + 1) * rows, cs]
            for k in range(CONV_K):
                prod = d * _tap(src_ref, sh_ref, cs, first_offset + k, rb * rows, rows)
                acc_ref[SUBLANES * k:SUBLANES * (k + 1), cs] += jnp.sum(
                    prod.reshape(rows // SUBLANES, SUBLANES, GROUP_W), axis=0)


def _spatial_mix(w_ref, v_bf, tm):
    rows = []
    for q in range(tm // CHUNK):
        cols = [_dot(w_ref[h], v_bf[q * CHUNK:(q + 1) * CHUNK, h * GROUP_W:(h + 1) * GROUP_W])
                for h in range(N_GROUPS)]
        rows.append(jnp.concatenate(cols, axis=1))
    return jnp.concatenate(rows, axis=0)


def _group_norm_fwd(h1, gn_g, gn_b):
    xhat, rstd = [], []
    for g in range(N_GROUPS):
        xh, rs = _norm_stats(h1[:, g * GROUP_W:(g + 1) * GROUP_W])
        xhat.append(xh)
        rstd.append(rs)
    xhat = jnp.concatenate(xhat, axis=1)
    return xhat * gn_g + gn_b, xhat, rstd


def _forward_tiles(p, x, tgt, wpa, wpb, wo, convw, vecs, ws, bsp, tiles_per_seq):
    t = x.shape[0]
    tm = TOKEN_TILE
    hb = tm // HALO

    def body(p_ref, ph_ref, x_ref, t_ref, wpa_ref, wpb_ref, wo_ref, cw_ref, vec_ref, ws_ref, bsp_ref,
             h1_ref, ya_ref, yb_ref, h3_ref, s_ref, mx_ref, dr_ref, drb_ref, xt_ref, acc_ref, he_ref, sh_ref):
        i = pl.program_id(0)
        xt_ref[...] = x_ref[...].T.astype(BF16)
        conv_b, gn_g, gn_b, lnv_g, lnv_b, b_o, lno_g, lno_b = [vec_ref[j:j + 1, :] for j in range(8)]

        keep = jnp.where(i % tiles_per_seq == 0, 0.0, 1.0)
        he_ref[0:HALO, :] = ph_ref[:, 0:D] * _sigmoid(ph_ref[:, D:2 * D]) * keep
        he_ref[HALO:, :] = p_ref[:, 0:D] * _sigmoid(p_ref[:, D:2 * D])
        _conv_taps(he_ref, sh_ref, cw_ref, HALO - (CONV_K - 1), 1, conv_b, h1_ref, tm)
        h2, _, _ = _group_norm_fwd(h1_ref[...], gn_g, gn_b)
        a_gate = p_ref[:, 2 * D:3 * D]
        h3 = ((h2 * _sigmoid(h2)) * (a_gate * _sigmoid(a_gate))).astype(BF16)
        h3_ref[...] = h3
        ya = _dot(h3, wpa_ref[...])
        ya_ref[...] = ya

        u = _gelu(p_ref[:, 3 * D:4 * D])
        vhat, _ = _norm_stats(_gelu(p_ref[:, 4 * D:5 * D]))
        v1 = (vhat * lnv_g + lnv_b).astype(BF16)
        b_gate = p_ref[:, 5 * D:6 * D]
        vmix = _spatial_mix(ws_ref, v1, tm) + jnp.concatenate([bsp_ref[...]] * (tm // CHUNK), axis=0)
        s = (u * vmix * (b_gate * _sigmoid(b_gate))).astype(BF16)
        s_ref[...] = s
        yb = _dot(s, wpb_ref[...])
        yb_ref[...] = yb

        mixed = (_sigmoid(p_ref[:, 6 * D:7 * D]) * ya + _sigmoid(p_ref[:, 7 * D:8 * D]) * yb).astype(BF16)
        mx_ref[...] = mixed
        r = ALPHA * x_ref[...] + (_dot(mixed, wo_ref[...]) + b_o)
        xhat, rstd = _norm_stats(r)
        err = (xhat * lno_g + lno_b) - t_ref[...]
        dout = err * (1.0 / D)
        dr = _norm_bwd(dout * lno_g, xhat, rstd)
        dr_ref[...] = dr
        drb_ref[...] = dr.astype(BF16)

        @pl.when(i == 0)
        def _():
            acc_ref[...] = jnp.zeros_like(acc_ref)

        acc_ref[0:1, :] += _colsum(dout * xhat)
        acc_ref[1:2, :] += _colsum(dout)
        acc_ref[2:3, :] += _colsum(dr)
        acc_ref[3:4, :] += _colsum(err * err) * (0.5 / D)

    tile = lambda w: pl.BlockSpec((tm, w), lambda i: (i, 0))
    f32_out = jax.ShapeDtypeStruct((t, D), F32)
    bf_out = jax.ShapeDtypeStruct((t, D), BF16)
    return pl.pallas_call(
        body, name="forward_tiles", grid=(t // tm,),
        in_specs=[tile(D_IN),
                  pl.BlockSpec((HALO, 2 * D), lambda i: (jnp.maximum(i * hb - 1, 0), 0)),
                  tile(D), tile(D), _resident((D, D)), _resident((D, D)), _resident((D, D)), _full((HALO, D)), _full((8, D)),
                  _full((N_GROUPS, CHUNK, CHUNK)), _full((CHUNK, D))],
        out_specs=[tile(D)] * 8 + [pl.BlockSpec((D, tm), lambda i: (0, i)), _full((8, D))],
        out_shape=[f32_out, f32_out, f32_out, bf_out, bf_out, bf_out, f32_out, bf_out,
                   jax.ShapeDtypeStruct((D, t), BF16), jax.ShapeDtypeStruct((8, D), F32)],
        scratch_shapes=[pltpu.VMEM((tm + HALO, D), F32), pltpu.VMEM((SUBLANES - 1, tm + SHIFT_ROWS, GROUP_W), F32)],
        compiler_params=_params("arbitrary"),
    )(p, p, x, tgt, wpa, wpb, wo, convw, vecs, ws, bsp)


def _backward_tiles(p, h1, ya, yb, drb, wpa, wpb, wo, vecs, ws, wst, bsp):
    t = h1.shape[0]
    tm = TOKEN_TILE

    def body(p_ref, h1_ref, ya_ref, yb_ref, drb_ref, wpa_ref, wpb_ref, wo_ref, vec_ref, ws_ref, wst_ref, bsp_ref,
             dh1_ref, dp_ref, dya_ref, dyb_ref, acc_ref, dbin_ref, dws_ref, dbsp_ref):
        i = pl.program_id(0)
        _, gn_g, gn_b, lnv_g, lnv_b = [vec_ref[j:j + 1, :] for j in range(5)]

        @pl.when(i == 0)
        def _():
            acc_ref[...] = jnp.zeros_like(acc_ref)
            dbin_ref[...] = jnp.zeros_like(dbin_ref)
            dws_ref[...] = jnp.zeros_like(dws_ref)
            dbsp_ref[...] = jnp.zeros_like(dbsp_ref)

        def emit(block, val):
            dbin_ref[0:1, block * D:(block + 1) * D] += _colsum(val)
            dp_ref[:, block * D:(block + 1) * D] = val.astype(BF16)

        dp_ref[:, 0:2 * D] = jnp.zeros((tm, 2 * D), BF16)
        dmixed = _dot_nt(drb_ref[...], wo_ref[...])
        ga = _sigmoid(p_ref[:, 6 * D:7 * D])
        gb = _sigmoid(p_ref[:, 7 * D:8 * D])
        dya32, dyb32 = dmixed * ga, dmixed * gb
        dya = dya32.astype(BF16)
        dyb = dyb32.astype(BF16)
        dya_ref[...] = dya
        dyb_ref[...] = dyb
        emit(6, dya32 * ya_ref[...] * (1.0 - ga))
        emit(7, dyb32 * yb_ref[...] * (1.0 - gb))

        dh3 = _dot_nt(dya, wpa_ref[...])
        h2, xhat, rstd = _group_norm_fwd(h1_ref[...], gn_g, gn_b)
        sg = _sigmoid(h2)
        a_gate = p_ref[:, 2 * D:3 * D]
        sa = _sigmoid(a_gate)
        silu_h2, silu_a = h2 * sg, a_gate * sa
        dh2 = dh3 * silu_a * (sg + silu_h2 * (1.0 - sg))
        emit(2, dh3 * silu_h2 * (sa + silu_a * (1.0 - sa)))
        acc_ref[0:1, :] += _colsum(dh2 * xhat)
        acc_ref[1:2, :] += _colsum(dh2)
        dxhat = dh2 * gn_g
        for g in range(N_GROUPS):
            cs = slice(g * GROUP_W, (g + 1) * GROUP_W)
            dh1_ref[:, cs] = _norm_bwd(dxhat[:, cs], xhat[:, cs], rstd[g])

        ds = _dot_nt(dyb, wpb_ref[...])
        u_pre = p_ref[:, 3 * D:4 * D]
        u, du_dpre = _gelu_and_grad(u_pre)
        v0, dv_dpre = _gelu_and_grad(p_ref[:, 4 * D:5 * D])
        vhat, vrstd = _norm_stats(v0)
        v1 = (vhat * lnv_g + lnv_b).astype(BF16)
        vmix = _spatial_mix(ws_ref, v1, tm) + jnp.concatenate([bsp_ref[...]] * (tm // CHUNK), axis=0)
        b_gate = p_ref[:, 5 * D:6 * D]
        sb = _sigmoid(b_gate)
        silu_b = b_gate * sb
        emit(3, ds * vmix * silu_b * du_dpre)
        emit(5, ds * u * vmix * (sb + silu_b * (1.0 - sb)))
        dvmix = ds * u * silu_b
        dvmix_bf = dvmix.astype(BF16)
        for q in range(tm // CHUNK):
            dbsp_ref[...] += dvmix[q * CHUNK:(q + 1) * CHUNK, :]
            for h in range(N_GROUPS):
                blk = (slice(q * CHUNK, (q + 1) * CHUNK), slice(h * GROUP_W, (h + 1) * GROUP_W))
                dws_ref[:, h * GROUP_W:(h + 1) * GROUP_W] += _dot_nt(dvmix_bf[blk], v1[blk])
        dv1 = _spatial_mix(wst_ref, dvmix_bf, tm)
        acc_ref[2:3, :] += _colsum(dv1 * vhat)
        acc_ref[3:4, :] += _colsum(dv1)
        emit(4, _norm_bwd(dv1 * lnv_g, vhat, vrstd) * dv_dpre)

    tile = lambda w: pl.BlockSpec((tm, w), lambda i: (i, 0))
    return pl.pallas_call(
        body, name="backward_tiles", grid=(t // tm,),
        in_specs=[tile(D_IN), tile(D), tile(D), tile(D), tile(D), _resident((D, D)), _resident((D, D)), _resident((D, D)),
                  _full((8, D)), _full((N_GROUPS, CHUNK, CHUNK)), _full((N_GROUPS, CHUNK, CHUNK)), _full((CHUNK, D))],
        out_specs=[tile(D), tile(D_IN), tile(D), tile(D), _full((8, D)), _full((8, D_IN)),
                   _full((CHUNK, D)), _full((CHUNK, D))],
        out_shape=[jax.ShapeDtypeStruct((t, D), F32), jax.ShapeDtypeStruct((t, D_IN), BF16),
                   jax.ShapeDtypeStruct((t, D), BF16), jax.ShapeDtypeStruct((t, D), BF16),
                   jax.ShapeDtypeStruct((8, D), F32), jax.ShapeDtypeStruct((8, D_IN), F32),
                   jax.ShapeDtypeStruct((CHUNK, D), F32), jax.ShapeDtypeStruct((CHUNK, D), F32)],
        compiler_params=_params("arbitrary"),
    )(p, h1, ya, yb, drb, wpa, wpb, wo, vecs, ws, wst, bsp)


def _conv_backward(dh1, p, dp, convw, pairs, tiles_per_seq):
    t = dh1.shape[0]
    tm = TOKEN_TILE
    hb = tm // HALO
    last = t // HALO - 1
    n_sq = len(pairs)
    span = 2
    rows = D // 8

    def body(dh1_ref, dnext_ref, p_ref, ph_ref, cw_ref, dp_in_ref, *refs):
        del dp_in_ref
        sq_in = refs[:2 * n_sq]
        dp_ref, dcw_ref, dbin_ref = refs[2 * n_sq:2 * n_sq + 3]
        sq_out = refs[2 * n_sq + 3:3 * n_sq + 3]
        sq_wire = refs[3 * n_sq + 3:4 * n_sq + 3]
        de_ref, he_ref, dh0_ref, sh_ref, acc_ref, wire_ref, sq_sem, wire_sem = refs[4 * n_sq + 3:]
        i = pl.program_id(0)

        @pl.when(i == 0)
        def _():
            dcw_ref[...] = jnp.zeros_like(dcw_ref)
            dbin_ref[...] = jnp.zeros_like(dbin_ref)
            acc_ref[...] = jnp.zeros_like(acc_ref)

        @pl.when(i % span == span - 1)
        def _():
            for a in range(n_sq):
                acc_ref[a] += _dot_tn(sq_in[2 * a][...], sq_in[2 * a + 1][...])

        keep_next = jnp.where(i % tiles_per_seq == tiles_per_seq - 1, 0.0, 1.0)
        de_ref[0:tm, :] = dh1_ref[...]
        de_ref[tm:, :] = dnext_ref[...] * keep_next
        _conv_taps(de_ref, sh_ref, cw_ref, CONV_K - 1, -1, None, dh0_ref, tm)

        keep_prev = jnp.where(i % tiles_per_seq == 0, 0.0, 1.0)
        sg = _sigmoid(p_ref[:, D:2 * D])
        val = p_ref[:, 0:D]
        he_ref[0:HALO, :] = ph_ref[:, 0:D] * _sigmoid(ph_ref[:, D:2 * D]) * keep_prev
        he_ref[HALO:, :] = val * sg
        _conv_weight_grad(dh1_ref, he_ref, sh_ref, HALO - (CONV_K - 1), dcw_ref, tm)
        dcw_ref[SUBLANES * CONV_K:, :] += jnp.sum(dh1_ref[...].reshape(tm // SUBLANES, SUBLANES, D), axis=0)

        dh0 = dh0_ref[...]
        dval = dh0 * sg
        dglu = dval * val * (1.0 - sg)
        dbin_ref[0:1, 0:D] += _colsum(dval)
        dbin_ref[0:1, D:2 * D] += _colsum(dglu)
        dp_ref[:, 0:D] = dval.astype(BF16)
        dp_ref[:, D:2 * D] = dglu.astype(BF16)

        @pl.when(i == t // tm - 1)
        def _():
            cps = [pltpu.make_async_copy(acc_ref.at[a, pl.ds((2 * j + h) * rows, rows)], sq_out[a].at[h, j],
                                         sq_sem.at[(a * N_CHIPS + j) * 2 + h])
                   for a in range(n_sq) for j in range(N_CHIPS) for h in range(2)]
            for cp in cps:
                cp.start()
            for a in range(n_sq):
                wire_ref[...] = acc_ref[a].astype(BF16)
                narrow = [pltpu.make_async_copy(wire_ref.at[pl.ds((2 * j + h) * rows, rows)], sq_wire[a].at[h, j],
                                                wire_sem.at[2 * j + h]) for j in range(N_CHIPS) for h in range(2)]
                for cp in narrow:
                    cp.start()
                for cp in narrow:
                    cp.wait()
            for cp in cps:
                cp.wait()

    any_spec = pl.BlockSpec(memory_space=pl.ANY)
    wide = pl.BlockSpec((span * tm, D), lambda i: (i // span, 0))
    return pl.pallas_call(
        body, name="conv_backward", grid=(t // tm,),
        in_specs=[pl.BlockSpec((tm, D), lambda i: (i, 0)),
                  pl.BlockSpec((HALO, D), lambda i: (jnp.minimum((i + 1) * hb, last), 0)),
                  pl.BlockSpec((tm, 2 * D), lambda i: (i, 0)),
                  pl.BlockSpec((HALO, 2 * D), lambda i: (jnp.maximum(i * hb - 1, 0), 0)),
                  _full((HALO, D)), any_spec] + [wide] * (2 * n_sq),
        out_specs=[pl.BlockSpec((tm, 2 * D), lambda i: (i, 0)), _full((SUBLANES * HALO, D)), _full((8, 2 * D))]
        + [any_spec] * (2 * n_sq),
        out_shape=[jax.ShapeDtypeStruct(dp.shape, BF16), jax.ShapeDtypeStruct((SUBLANES * HALO, D), F32),
                   jax.ShapeDtypeStruct((8, 2 * D), F32)]
        + [jax.ShapeDtypeStruct((2, N_CHIPS, rows, D), F32)] * n_sq
        + [jax.ShapeDtypeStruct((2, N_CHIPS, rows, D), BF16)] * n_sq,
        scratch_shapes=[pltpu.VMEM((tm + HALO, D), F32), pltpu.VMEM((tm + HALO, D), F32), pltpu.VMEM((tm, D), F32),
                        pltpu.VMEM((SUBLANES - 1, tm + SHIFT_ROWS, GROUP_W), F32), pltpu.VMEM((n_sq, D, D), F32),
                        pltpu.VMEM((D, D), BF16), pltpu.SemaphoreType.DMA((n_sq * N_CHIPS * 2,)),
                        pltpu.SemaphoreType.DMA((N_CHIPS * 2,))],
        input_output_aliases={5: 0},
        compiler_params=_params("arbitrary"),
    )(dh1, dh1, p, p, convw, dp, *[a for pair in pairs for a in pair])


def _grad_in_and_x(xt, dp, w4, dr, wires, grads):
    t = dr.shape[0]
    tm = TOKEN_TILE
    half, tn = D // 2, 512
    nb = W_BLOCK // tn
    n_w, n_x = 2 * N_CHIPS * nb, t // tm
    ns = len(grads)
    xi, yi, ci = lax.axis_index("x"), lax.axis_index("y"), lax.axis_index("c")
    others = [2 * (1 - xi) + yi, 2 * xi + (1 - yi), 2 * (1 - xi) + (1 - yi)]
    blocks = others + others + [2 * xi + yi] * 2
    halves = [1 - ci] * 3 + [ci] * 3 + [1 - ci, ci]
    table = jnp.stack([jnp.stack([b * nb + n for b in blocks for n in range(nb)]),
                       jnp.stack([h for h in halves for _ in range(nb)])]).astype(jnp.int32)

    def body(tab_ref, xt_ref, dpc_ref, dpr_ref, w_ref, dr_ref, *refs):
        parts, fulls = refs[:ns], refs[ns:2 * ns]
        dx_ref, qk_ref, b2_ref, b1_ref, wire_ref = refs[2 * ns:2 * ns + 5]
        lands, sibs = refs[2 * ns + 5:3 * ns + 5], refs[3 * ns + 5:4 * ns + 5]
        (g_ref, st_ref, sb_ref, tmp_ref, d2d_send, d2d_recv, ici_send, ici_recv, own_sem, tmp_sem, wire_sem,
         p_send, p_recv, s_send, s_recv) = refs[4 * ns + 5:]
        s = pl.program_id(0)
        x_, y_, c, k = _position()
        chips = _other_chips(x_, y_)
        n = s % nb
        grp = s // nb
        cols = pl.ds(pl.multiple_of(n * tn, tn), tn)

        def part(a, r, core):
            cx, cy = chips[r]
            return pltpu.make_async_remote_copy(
                src_ref=parts[a].at[core, 2 * cx + cy], dst_ref=lands[a].at[2 * r + c],
                send_sem=p_send.at[6 * a + 2 * r + core], recv_sem=p_recv.at[6 * a + 2 * r + c],
                device_id=(cx, cy, core), device_id_type=MESH)

        def landed(a, r, core):
            cx, cy = chips[r]
            return pltpu.make_async_remote_copy(
                src_ref=lands[a].at[2 * r + core], dst_ref=lands[a].at[2 * r + core],
                send_sem=p_send.at[6 * a + 2 * r + core], recv_sem=p_recv.at[6 * a + 2 * r + core],
                device_id=(cx, cy, core), device_id_type=MESH)

        def to_sibling_whole(a):
            return pltpu.make_async_remote_copy(
                src_ref=fulls[a].at[1 - c, k], dst_ref=sibs[a], send_sem=s_send.at[a], recv_sem=s_recv.at[a],
                device_id=(x_, y_, 1 - c), device_id_type=MESH)

        def to_sibling(slot, land):
            return pltpu.make_async_remote_copy(
                src_ref=st_ref.at[slot], dst_ref=b1_ref.at[land, :, cols], send_sem=d2d_send.at[slot],
                recv_sem=d2d_recv.at[land * nb + n], device_id=(x_, y_, 1 - c), device_id_type=MESH)

        def to_chip(r):
            cx, cy = chips[r]
            return pltpu.make_async_remote_copy(
                src_ref=wire_ref.at[r, :, cols], dst_ref=b2_ref.at[r, :, cols], send_sem=ici_send.at[r],
                recv_sem=ici_recv.at[r], device_id=(cx, cy, c), device_id_type=MESH)

        def all_of_chip(r):
            cx, cy = chips[r]
            return pltpu.make_async_remote_copy(
                src_ref=wire_ref.at[r], dst_ref=b2_ref.at[r], send_sem=ici_send.at[r],
                recv_sem=ici_recv.at[r], device_id=(cx, cy, c), device_id_type=MESH)

        def to_result(slot):
            return pltpu.make_async_copy(st_ref.at[slot], qk_ref.at[:, cols], own_sem.at[slot])

        def sibling_piece(land):
            return pltpu.make_async_copy(b1_ref.at[land, :, cols], tmp_ref, tmp_sem)

        @pl.when(s == 0)
        def _():
            for a in range(ns):
                to_sibling_whole(a).start()
                for r in range(3):
                    for core in range(2):
                        part(a, r, core).start()

        own_half = ((grp >= 3) & (grp <= 5)) | (grp == 7)
        land = jnp.where(grp == 7, 3, grp - 3)

        @pl.when(own_half)
        def _():
            to_sibling(0, land).wait_recv()
            sibling_piece(land).start()

        @pl.when(s < n_w)
        def _():
            g_ref[...] = _dot(xt_ref[tab_ref[1, s]], dpc_ref[...])

        @pl.when(own_half)
        def _():
            sibling_piece(land).wait()

        for g in range(2 * N_CHIPS):
            @pl.when(grp == g)
            def _(g=g):
                if g in (0, 1, 2, 6):
                    use = s if g < 3 else 3 * nb + n
                    slot = use % 2

                    @pl.when(use >= 2)
                    def _():
                        to_sibling(slot, 0).wait_send()

                    st_ref[slot] = g_ref[...]
                    to_sibling(slot, min(g, 3)).start()
                elif g in (3, 4, 5):
                    sb_ref[...] = (g_ref[...] + tmp_ref[...]).astype(BF16)
                    stage = pltpu.make_async_copy(sb_ref, wire_ref.at[g - 3, :, cols], wire_sem)
                    stage.start()
                    stage.wait()
                    to_chip(g - 3).start()
                else:
                    slot = n % 2
                    piece = g_ref[...] + tmp_ref[...]

                    @pl.when(n < 2)
                    def _():
                        to_sibling(slot, 0).wait_send()

                    @pl.when(n >= 2)
                    def _():
                        to_result(slot).wait()

                    st_ref[slot] = piece
                    to_result(slot).start()

        @pl.when(s >= n_w)
        def _():
            acc = ALPHA * dr_ref[...]
            for j in range(N_CHIPS):
                acc = acc + _dot_nt(dpr_ref[:, j * W_BLOCK:(j + 1) * W_BLOCK], w_ref[j])
            dx_ref[...] = acc

        @pl.when(s == n_w + n_x - 1)
        def _():
            for slot in range(2):
                to_result(slot).wait()
            for r in range(3):
                all_of_chip(r).wait_recv()
                all_of_chip(r).wait_send()
            for a in range(ns):
                to_sibling_whole(a).wait_recv()
                to_sibling_whole(a).wait_send()
                for r in range(3):
                    for core in range(2):
                        landed(a, r, core).wait_recv()
                        part(a, r, core).wait_send()

    any_spec = pl.BlockSpec(memory_space=pl.ANY)
    tile = lambda s, tab: (jnp.maximum(s - n_w, 0), 0)
    return pl.pallas_call(
        body, name="grad_in_and_x",
        grid_spec=pltpu.PrefetchScalarGridSpec(
            num_scalar_prefetch=1, grid=(n_w + n_x,),
            in_specs=[pl.BlockSpec((2, half, t), lambda s, tab: (0, 0, 0), pipeline_mode=pl.Buffered(1)),
                      pl.BlockSpec((t, tn), lambda s, tab: (0, tab[0, jnp.minimum(s, n_w - 1)])),
                      pl.BlockSpec((tm, D_IN), tile),
                      pl.BlockSpec((N_CHIPS, D, W_BLOCK), lambda s, tab: (0, 0, 0), pipeline_mode=pl.Buffered(1)),
                      pl.BlockSpec((tm, D), tile)] + [any_spec] * (2 * ns),
            out_specs=[pl.BlockSpec((tm, D), tile)] + [any_spec] * (4 + 2 * ns),
            scratch_shapes=[pltpu.VMEM((half, tn), F32), pltpu.VMEM((2, half, tn), F32), pltpu.VMEM((half, tn), BF16),
                            pltpu.VMEM((half, tn), F32),
                            pltpu.SemaphoreType.DMA((2,)), pltpu.SemaphoreType.DMA((N_CHIPS * nb,)),
                            pltpu.SemaphoreType.DMA((3,)), pltpu.SemaphoreType.DMA((3,)),
                            pltpu.SemaphoreType.DMA((2,)), pltpu.SemaphoreType.DMA, pltpu.SemaphoreType.DMA,
                            pltpu.SemaphoreType.DMA((6 * ns,)), pltpu.SemaphoreType.DMA((6 * ns,)),
                            pltpu.SemaphoreType.DMA((ns,)), pltpu.SemaphoreType.DMA((ns,))]),
        out_shape=[jax.ShapeDtypeStruct((t, D), F32), jax.ShapeDtypeStruct((half, W_BLOCK), F32),
                   jax.ShapeDtypeStruct((3, half, W_BLOCK), BF16), jax.ShapeDtypeStruct((N_CHIPS, half, W_BLOCK), F32),
                   jax.ShapeDtypeStruct((3, half, W_BLOCK), BF16)]
        + [jax.ShapeDtypeStruct((6,) + w.shape[2:], w.dtype) for w in wires]
        + [jax.ShapeDtypeStruct(g.shape[2:], F32) for g in grads],
        compiler_params=_params("arbitrary"),
    )(table, xt, dp, dp, w4, dr, *wires, *grads)


def kernel(x, w_in, b_in, conv_w, conv_b, gn_g, gn_b, ln_v_g, ln_v_b, w_spatial, b_spatial, w_pa, w_pb, w_o, b_o, ln_out_g, ln_out_b, loss_target, m_w_in, m_b_in, m_conv_w, m_conv_b, m_gn_g, m_gn_b, m_ln_v_g, m_ln_v_b, m_w_spatial, m_b_spatial, m_w_pa, m_w_pb, m_w_o, m_b_o, m_ln_out_g, m_ln_out_b, v_w_in, v_b_in, v_conv_w, v_conv_b, v_gn_g, v_gn_b, v_ln_v_g, v_ln_v_b, v_w_spatial, v_b_spatial, v_w_pa, v_w_pb, v_w_o, v_b_o, v_ln_out_g, v_ln_out_b):
    n_seq, seq, _ = x.shape
    t = n_seq * seq
    tiles_per_seq = seq // TOKEN_TILE
    x2 = x.reshape(t, D)
    tgt = loss_target.reshape(t, D)

    conv_shard = jnp.pad(conv_w, ((0, HALO - CONV_K), (0, 0)))
    p, win4, wpa4, wpb4, wo4, conv4 = _proj_gather(
        x2, b_in,
        _place_shards([w_in, w_pa, w_pb, w_o, conv_shard], [BF16, BF16, BF16, BF16, F32]))
    win4 = win4.reshape(N_CHIPS, D, W_BLOCK)
    wpa, wpb, wo = wpa4.reshape(D, D), wpb4.reshape(D, D), wo4.reshape(D, D)
    convw = conv4.reshape(N_CHIPS, HALO, D // N_CHIPS).transpose(1, 0, 2).reshape(HALO, D)

    vecs = jnp.stack([conv_b, gn_g, gn_b, ln_v_g, ln_v_b, b_o, ln_out_g, ln_out_b])
    causal = jnp.tril(jnp.ones((CHUNK, CHUNK), bool))
    ws = jnp.where(causal[None], w_spatial, 0.0)
    ws_bf, wst_bf = ws.astype(BF16), ws.transpose(0, 2, 1).astype(BF16)
    bsp = jnp.repeat(b_spatial.T, GROUP_W, axis=1)

    h1, ya, yb, h3, s, mixed, dr, drb, xt, acc_f = _forward_tiles(p, x2, tgt, wpa, wpb, wo, convw, vecs, ws_bf, bsp, tiles_per_seq)
    dh1, dp, dya, dyb, acc_b, dbin_b, dws, dbsp_acc = _backward_tiles(p, h1, ya, yb, drb, wpa, wpb, wo, vecs, ws_bf, wst_bf, bsp)
    dp, dcw8, dbin_a, *square = _conv_backward(dh1, p, dp, convw, [(h3, dya), (s, dyb), (mixed, drb)], tiles_per_seq)

    small, g_conv = _pack_small(acc_f, acc_b, dbin_a, dbin_b, dcw8, dws, dbsp_acc)
    small = small.reshape(2, N_CHIPS, SMALL_ROWS // 8, D)

    grads = square[:3] + [g_conv, small]
    wires = square[3:] + [g_conv, small]
    grad_x, q_in, chips_in, _, _, *landed = _grad_in_and_x(xt.reshape(2, D // 2, t), dp, win4, dr, wires, grads)
    grad_x = grad_x.reshape(x.shape)
    mine = [_add_chips(q_in, chips_in)]
    mine += _add_devices(grads, landed[:5], landed[5:])
    *full, small_parts = _share_results(mine[:5], mine[5])
    grad_w_in, grad_w_pa, grad_w_pb, grad_w_o = [f.reshape(w.shape) for f, w in zip(full[:4], (w_in, w_pa, w_pb, w_o))]
    grad_conv_w = full[4].reshape(HALO, D // N_CHIPS)[:CONV_K]

    big = {"w_in": (grad_w_in,) + tuple(_adamw(w_in, grad_w_in, m_w_in, v_w_in))}
    group = _adamw_group([(w_pa, grad_w_pa, m_w_pa, v_w_pa), (w_pb, grad_w_pb, m_w_pb, v_w_pb),
                          (w_o, grad_w_o, m_w_o, v_w_o)], (conv_w, grad_conv_w, m_conv_w, v_conv_w))
    for name, g, res in zip(["w_pa", "w_pb", "w_o", "conv_w"], [grad_w_pa, grad_w_pb, grad_w_o, grad_conv_w], group):
        big[name] = (g,) + tuple(res)
    vec_names = ["conv_b", "gn_g", "gn_b", "ln_v_g", "ln_v_b", "b_o", "ln_out_g", "ln_out_b"]
    vec_triples = [(conv_b, m_conv_b, v_conv_b), (gn_g, m_gn_g, v_gn_g), (gn_b, m_gn_b, v_gn_b),
                   (ln_v_g, m_ln_v_g, v_ln_v_g), (ln_v_b, m_ln_v_b, v_ln_v_b), (b_o, m_b_o, v_b_o),
                   (ln_out_g, m_ln_out_g, v_ln_out_g), (ln_out_b, m_ln_out_b, v_ln_out_b)]
    small_res, loss8 = _adamw_small(
        small_parts.reshape(8, SMALL_ROWS // 8, D), vec_triples, (b_in, m_b_in, v_b_in),
        (w_spatial, m_w_spatial, v_w_spatial), (b_spatial, m_b_spatial, v_b_spatial))
    per_name = dict(zip(vec_names + ["b_in", "w_spatial", "b_spatial"], small_res))

    order = ["w_in", "b_in", "conv_w", "conv_b", "gn_g", "gn_b", "ln_v_g", "ln_v_b", "w_spatial", "b_spatial",
             "w_pa", "w_pb", "w_o", "b_o", "ln_out_g", "ln_out_b"]
    outs = [loss8[0, 0], grad_x]
    for kind in range(4):
        outs += [big[n][kind] if n in big else per_name[n][kind] for n in order]
    return tuple(outs)
```

```python
import math

import jax
import jax.numpy as jnp
from jax import lax
from jax.experimental import pallas as pl
from jax.experimental.pallas import tpu as pltpu

D = 1024
N_GROUPS = 8
GROUP_W = D // N_GROUPS
CHUNK = 128
CONV_K = 31
HALO = 32
D_IN = 8 * D
N_CHIPS = 4
W_BLOCK = D_IN // N_CHIPS
ALPHA = 2.0 ** 0.25
LN_EPS = 1e-5
ADAM_LR, ADAM_B1, ADAM_B2, ADAM_EPS, ADAM_WD, ADAM_STEP = 0.001, 0.9, 0.999, 1e-08, 0.01, 10

TOKEN_TILE = 256
VMEM_LIMIT = 56 * 1024 * 1024
MESH = pl.DeviceIdType.MESH
F32, BF16 = jnp.float32, jnp.bfloat16


def _sigmoid(x):
    return jax.nn.sigmoid(x)


GELU_C = math.sqrt(2.0 / math.pi)
GELU_CA = GELU_C * 0.044715


def _gelu(x):
    t = jnp.tanh(x * (GELU_C + GELU_CA * (x * x)))
    return x * (0.5 + 0.5 * t)


def _gelu_and_grad(x):
    x2 = x * x
    t = jnp.tanh(x * (GELU_C + GELU_CA * x2))
    cdf = 0.5 + 0.5 * t
    return x * cdf, cdf + (0.5 * x) * (1.0 - t * t) * (GELU_C + (3.0 * GELU_CA) * x2)


def _norm_stats(v):
    mu = jnp.mean(v, axis=-1, keepdims=True)
    vc = v - mu
    var = jnp.mean(vc * vc, axis=-1, keepdims=True)
    rstd = lax.rsqrt(var + LN_EPS)
    return vc * rstd, rstd


def _norm_bwd(dxhat, xhat, rstd):
    m1 = jnp.mean(dxhat, axis=-1, keepdims=True)
    m2 = jnp.mean(dxhat * xhat, axis=-1, keepdims=True)
    return rstd * (dxhat - m1 - xhat * m2)


def _dot(a, b):
    return jnp.dot(a, b, preferred_element_type=F32)


def _dot_nt(a, b):
    return lax.dot_general(a, b, (((1,), (1,)), ((), ())), preferred_element_type=F32)


def _dot_tn(a, b):
    return lax.dot_general(a, b, (((0,), (0,)), ((), ())), preferred_element_type=F32)


def _colsum(v):
    return jnp.sum(v, axis=0, keepdims=True)


def _full(shape):
    return pl.BlockSpec(shape, lambda *_: (0,) * len(shape))


def _resident(shape):
    return pl.BlockSpec(shape, lambda *_: (0,) * len(shape), pipeline_mode=pl.Buffered(1))


def _params(*sem):
    return pltpu.CompilerParams(dimension_semantics=sem, vmem_limit_bytes=VMEM_LIMIT)


def _chip_index():
    return (2 * lax.axis_index("x") + lax.axis_index("y")).astype(jnp.int32).reshape(1)


def _core_index():
    return lax.axis_index("c").astype(jnp.int32).reshape(1)


def _place_shards(ws, dtypes):
    n = len(ws)

    def body(k_ref, *refs):
        for w_ref, o_ref, dtype in zip(refs[:n], refs[n:], dtypes):
            rows = w_ref.shape[0] // 2
            for h in range(2):
                o_ref[h] = w_ref[h * rows:(h + 1) * rows, :].astype(dtype)

    return pl.pallas_call(
        body, name="place_shards",
        grid_spec=pltpu.PrefetchScalarGridSpec(
            num_scalar_prefetch=1, grid=(1,),
            in_specs=[pl.BlockSpec(w.shape, lambda i, k: (0, 0)) for w in ws],
            out_specs=[pl.BlockSpec((None, 2, w.shape[0] // 2, w.shape[1]), lambda i, k: (k[0], 0, 0, 0)) for w in ws]),
        out_shape=[jax.ShapeDtypeStruct((N_CHIPS, 2, w.shape[0] // 2, w.shape[1]), dt) for w, dt in zip(ws, dtypes)],
        compiler_params=_params("arbitrary"),
    )(_chip_index(), *ws)


def _position():
    x, y, c = lax.axis_index("x"), lax.axis_index("y"), lax.axis_index("c")
    return x, y, c, 2 * x + y


def _other_chips(x, y):
    return [(1 - x, y), (x, 1 - y), (1 - x, 1 - y)]


def _any_specs(n):
    return [pl.BlockSpec(memory_space=pl.ANY)] * n


def _proj_gather(x, b_in, bufs):
    t = x.shape[0]
    tm = 1024
    steps = t // tm
    ahead = 3 * steps // 4
    half = D // 2
    chunk = W_BLOCK // 2
    n = len(bufs)
    xi, yi = lax.axis_index("x"), lax.axis_index("y")
    chips = [2 * xi + yi, 2 * (1 - xi) + yi, 2 * xi + (1 - yi), 2 * (1 - xi) + (1 - yi)]
    plan = [(0, 0), (0, 1), (1, 0), (2, 1), (1, 1), (2, 0), (3, 0), (3, 1)]
    order = jnp.stack([2 * chips[ch] + q for ch, q in plan]).astype(jnp.int32)

    def body(order_ref, x_ref, b_ref, *refs):
        p_ref, outs = refs[n], refs[n + 1:2 * n + 1]
        xb_ref, w_ref, lsem, send, recv, hop_send, hop_recv, fsend, frecv, qsend, qrecv = refs[2 * n + 1:]
        jj, i = pl.program_id(0), pl.program_id(1)
        x_, y_, c, k = _position()
        nbrs = [(1 - x_, y_), (x_, 1 - y_)]
        blocks = [2 * (1 - x_) + y_, 2 * x_ + (1 - y_), 2 * (1 - x_) + (1 - y_)]

        def quarter(a, block, q, h):
            if a == 0:
                return outs[0].at[block, h, :, pl.ds(q * chunk, chunk)]
            rows = outs[a].shape[2] // 2
            return outs[a].at[block, h, pl.ds(q * rows, rows)]

        def copy(ref, to, send_sem, recv_sem):
            return pltpu.make_async_remote_copy(src_ref=ref, dst_ref=ref, send_sem=send_sem, recv_sem=recv_sem,
                                                device_id=(to[0], to[1], c), device_id_type=MESH)

        def sent(a, nb, q):
            return copy(quarter(a, k, q, c), nbrs[nb], send.at[4 * a + 2 * nb + q], recv.at[4 * a + 2 * nb + q])

        def landed(a, nb, q):
            return copy(quarter(a, blocks[nb], q, c), nbrs[nb], send.at[4 * a + 2 * nb + q], recv.at[4 * a + 2 * nb + q])

        def hopped(a, nb):
            return copy(quarter(a, blocks[nb], nb, c), nbrs[1 - nb], hop_send.at[2 * a + nb], hop_recv.at[2 * a + 1 - nb])

        def from_diagonal(a, via):
            return copy(quarter(a, blocks[2], 1 - via, c), nbrs[via], hop_send.at[2 * a + via], hop_recv.at[2 * a + via])

        def passed(a, r, h):
            return pltpu.make_async_remote_copy(
                src_ref=outs[a].at[blocks[r], h], dst_ref=outs[a].at[blocks[r], h], send_sem=fsend.at[3 * a + r],
                recv_sem=frecv.at[3 * a + r], device_id=(x_, y_, 1 - c), device_id_type=MESH)

        def passed_quarter(r, q, h):
            ref = quarter(0, blocks[r], q, h)
            return pltpu.make_async_remote_copy(
                src_ref=ref, dst_ref=ref, send_sem=qsend.at[2 * r + q], recv_sem=qrecv.at[2 * r + q],
                device_id=(x_, y_, 1 - c), device_id_type=MESH)

        def load(block, q, slot):
            return [pltpu.make_async_copy(quarter(0, block, q, h), w_ref.at[slot, pl.ds(h * half, half)],
                                          lsem.at[2 * slot + h]) for h in range(2)]

        def pass_on(arrays):
            for a in arrays:
                for nb in range(2):
                    landed(a, nb, nb).wait_recv()
                    hopped(a, nb).start()

        @pl.when((jj == 0) & (i == 0))
        def _():
            for a in range(n):
                for nb, q in ((0, 0), (1, 1), (0, 1), (1, 0)):
                    sent(a, nb, q).start()
            for cp in load(k, 0, 0):
                cp.start()

        for nxt in range(1, len(plan)):
            @pl.when((jj == nxt - 1) & (i == ahead))
            def _(nxt=nxt):
                ch, q = plan[nxt]
                if ch in (1, 2):
                    landed(0, ch - 1, q).wait_recv()
                    if q == ch - 1:
                        hopped(0, ch - 1).start()
                elif ch == 3:
                    from_diagonal(0, 1 - q).wait_recv()
                if ch:
                    passed_quarter(ch - 1, q, c).start()
                    passed_quarter(ch - 1, q, 1 - c).wait_recv()
                for cp in load(k if ch == 0 else blocks[ch - 1], q, nxt % 2):
                    cp.start()
                if nxt == 5:
                    pass_on(range(1, n))
                    for a in range(1, n):
                        landed(a, 0, 1).wait_recv()
                        passed(a, 0, c).start()
                        landed(a, 1, 0).wait_recv()
                        passed(a, 1, c).start()

        slot = jj % 2

        @pl.when(i == 0)
        def _():
            for cp in load(k, 0, slot):
                cp.wait()

        rows = pl.ds(pl.multiple_of(i * tm, tm), tm)

        @pl.when(jj == 0)
        def _():
            xb_ref[rows, :] = x_ref[...].astype(BF16)

        p_ref[...] = _dot(xb_ref[rows, :], w_ref[slot]) + b_ref[...]

        @pl.when((jj == len(plan) - 1) & (i == steps - 1))
        def _():
            for a in range(1, n):
                from_diagonal(a, 0).wait_recv()
                from_diagonal(a, 1).wait_recv()
                passed(a, 2, c).start()
            for a in range(1, n):
                for r in range(3):
                    passed(a, r, 1 - c).wait_recv()
                    passed(a, r, c).wait_send()
            for r in range(3):
                for q in range(2):
                    passed_quarter(r, q, c).wait_send()
            for a in range(n):
                for nb in range(2):
                    for q in range(2):
                        sent(a, nb, q).wait_send()
                    hopped(a, nb).wait_send()

    any_spec = pl.BlockSpec(memory_space=pl.ANY)
    return pl.pallas_call(
        body, name="proj_gather",
        grid_spec=pltpu.PrefetchScalarGridSpec(
            num_scalar_prefetch=1, grid=(len(plan), steps),
            in_specs=[pl.BlockSpec((tm, D), lambda jj, i, o: (jnp.where(jj == 0, i, steps - 1), 0)),
                      pl.BlockSpec((None, 1, chunk), lambda jj, i, o: (o[jj], 0, 0))] + [any_spec] * n,
            out_specs=[pl.BlockSpec((tm, chunk), lambda jj, i, o: (i, o[jj]))] + [any_spec] * n,
            scratch_shapes=[pltpu.VMEM((t, D), BF16), pltpu.VMEM((2, D, chunk), BF16), pltpu.SemaphoreType.DMA((4,)),
                            pltpu.SemaphoreType.DMA((4 * n,)), pltpu.SemaphoreType.DMA((4 * n,)),
                            pltpu.SemaphoreType.DMA((2 * n,)), pltpu.SemaphoreType.DMA((2 * n,)),
                            pltpu.SemaphoreType.DMA((3 * n,)), pltpu.SemaphoreType.DMA((3 * n,)),
                            pltpu.SemaphoreType.DMA((6,)), pltpu.SemaphoreType.DMA((6,))]),
        out_shape=[jax.ShapeDtypeStruct((t, D_IN), F32)] + [jax.ShapeDtypeStruct(b.shape, b.dtype) for b in bufs],
        input_output_aliases={3 + a: 1 + a for a in range(n)},
        compiler_params=_params("arbitrary", "arbitrary"),
    )(order, x, b_in.reshape(D_IN // chunk, 1, chunk), *bufs)


def _share_results(bufs, small):
    n = len(bufs)

    def body(*refs):
        outs, small_out = refs[n + 1:2 * n + 1], refs[2 * n + 1]
        send, recv, ssend, srecv = refs[2 * n + 2:]
        x, y, c, k = _position()
        cps = []
        for a in range(n):
            cp = pltpu.make_async_remote_copy(
                src_ref=outs[a].at[c], dst_ref=outs[a].at[c], send_sem=send.at[a], recv_sem=recv.at[a],
                device_id=(x, y, 1 - c), device_id_type=MESH)
            cp.start()
            cps.append(cp)
        waits = []
        for p in range(1, 8):
            px, py, pc = x ^ (p >> 2), y ^ ((p >> 1) & 1), c ^ (p & 1)
            cp = pltpu.make_async_remote_copy(
                src_ref=small_out.at[k, c], dst_ref=small_out.at[k, c], send_sem=ssend.at[p - 1],
                recv_sem=srecv.at[p - 1], device_id=(px, py, pc), device_id_type=MESH)
            cp.start()
            cps.append(cp)
            waits.append(pltpu.make_async_remote_copy(
                src_ref=small_out.at[2 * px + py, pc], dst_ref=small_out.at[2 * px + py, pc], send_sem=ssend.at[p - 1],
                recv_sem=srecv.at[p - 1], device_id=(px, py, pc), device_id_type=MESH))
        for a in range(n):
            pltpu.make_async_remote_copy(
                src_ref=outs[a].at[1 - c], dst_ref=outs[a].at[1 - c], send_sem=send.at[a], recv_sem=recv.at[a],
                device_id=(x, y, 1 - c), device_id_type=MESH).wait_recv()
        for w in waits:
            w.wait_recv()
        for cp in cps:
            cp.wait_send()

    return pl.pallas_call(
        body, name="rs_share_results",
        in_specs=_any_specs(n + 1), out_specs=_any_specs(n + 1),
        out_shape=[jax.ShapeDtypeStruct(b.shape, b.dtype) for b in bufs + [small]],
        scratch_shapes=[pltpu.SemaphoreType.DMA((n,)), pltpu.SemaphoreType.DMA((n,)),
                        pltpu.SemaphoreType.DMA((7,)), pltpu.SemaphoreType.DMA((7,))],
        input_output_aliases={a: a for a in range(n + 1)},
    )(*bufs, small)


def _row_tile(r, c):
    t = max(8, min(r, (1 << 19) // c))
    while r % t:
        t //= 2
    return t


def _add_devices(grads, lands, sibs):
    n = len(grads)

    def body(kc_ref, *refs):
        for g_ref, l_ref, s_ref, f_ref in zip(refs[:n], refs[n:2 * n], refs[2 * n:3 * n], refs[3 * n:]):
            f = g_ref[...] + s_ref[...]
            for i in range(l_ref.shape[0]):
                f = f + l_ref[i].astype(F32)
            f_ref[...] = f

    shapes = [g.shape[2:] for g in grads]
    out_specs = [pl.BlockSpec((None,) + sh, lambda i, kc: (kc[1], 0, 0)) for sh in shapes[:-1]]
    out_specs.append(pl.BlockSpec((None, None) + shapes[-1], lambda i, kc: (kc[0], kc[1], 0, 0)))
    out_shape = [jax.ShapeDtypeStruct((2,) + sh, F32) for sh in shapes[:-1]]
    out_shape.append(jax.ShapeDtypeStruct((N_CHIPS, 2) + shapes[-1], F32))
    return pl.pallas_call(
        body, name="rs_add_devices",
        grid_spec=pltpu.PrefetchScalarGridSpec(
            num_scalar_prefetch=1, grid=(1,),
            in_specs=[pl.BlockSpec((None, None) + sh, lambda i, kc: (kc[1], kc[0], 0, 0)) for sh in shapes]
            + [pl.BlockSpec(l.shape, lambda i, kc: (0, 0, 0)) for l in lands]
            + [pl.BlockSpec(sh, lambda i, kc: (0, 0)) for sh in shapes],
            out_specs=out_specs),
        out_shape=out_shape,
        compiler_params=_params("arbitrary"),
    )(jnp.concatenate([_chip_index(), _core_index()]), *grads, *lands, *sibs)


def _add_chips(q, b2):
    r, c = q.shape
    t = _row_tile(r, c)

    def body(c_ref, q_ref, b_ref, f_ref):
        f_ref[...] = ((q_ref[...] + b_ref[0].astype(F32)) + b_ref[1].astype(F32)) + b_ref[2].astype(F32)

    return pl.pallas_call(
        body, name="rs_add_chips",
        grid_spec=pltpu.PrefetchScalarGridSpec(
            num_scalar_prefetch=1, grid=(r // t,),
            in_specs=[pl.BlockSpec((t, c), lambda i, cr: (i, 0)), pl.BlockSpec((3, t, c), lambda i, cr: (0, i, 0))],
            out_specs=pl.BlockSpec((None, t, c), lambda i, cr: (cr[0], i, 0))),
        out_shape=jax.ShapeDtypeStruct((2, r, c), F32),
        compiler_params=_params("parallel"),
    )(_core_index(), q, b2)


def _adamw_math(w, g, m, v):
    m = ADAM_B1 * m + (1.0 - ADAM_B1) * g
    v = ADAM_B2 * v + (1.0 - ADAM_B2) * (g * g)
    m_hat = m / (1.0 - ADAM_B1 ** ADAM_STEP)
    v_hat = v / (1.0 - ADAM_B2 ** ADAM_STEP)
    delta = -ADAM_LR * (m_hat / (jnp.sqrt(v_hat) + ADAM_EPS) + ADAM_WD * w)
    return delta, m, v


def _adamw_group(quads, conv):
    n = len(quads)
    r, c = quads[0][0].shape

    def body(*refs):
        ins, outs = refs[:4 * (n + 1)], refs[4 * (n + 1):]
        for i in range(n + 1):
            w_ref, g_ref, m_ref, v_ref = ins[4 * i:4 * i + 4]
            res = _adamw_math(w_ref[...], g_ref[...], m_ref[...], v_ref[...])
            for o_ref, val in zip(outs[3 * i:3 * i + 3], res):
                o_ref[...] = val

    half = pl.BlockSpec((r // 2, c), lambda i: (i, 0))
    whole = pl.BlockSpec(conv[0].shape, lambda i: (0, 0))
    res = pl.pallas_call(
        body, name="adamw_group", grid=(2,),
        in_specs=[half] * (4 * n) + [whole] * 4, out_specs=[half] * (3 * n) + [whole] * 3,
        out_shape=[jax.ShapeDtypeStruct((r, c), F32)] * (3 * n) + [jax.ShapeDtypeStruct(conv[0].shape, F32)] * 3,
        compiler_params=_params("arbitrary"),
    )(*[a for quad in quads for a in quad], *conv)
    return [res[3 * i:3 * i + 3] for i in range(n + 1)]


def _adamw(w, g, m, v):
    r, c = w.shape
    t = _row_tile(r, c) if r % 8 == 0 else r

    def body(w_ref, g_ref, m_ref, v_ref, d_ref, nm_ref, nv_ref):
        d_ref[...], nm_ref[...], nv_ref[...] = _adamw_math(w_ref[...], g_ref[...], m_ref[...], v_ref[...])

    spec = pl.BlockSpec((t, c), lambda i: (i, 0))
    return pl.pallas_call(
        body, name="adamw", grid=(r // t,), in_specs=[spec] * 4, out_specs=[spec] * 3,
        out_shape=[jax.ShapeDtypeStruct((r, c), F32)] * 3, compiler_params=_params("parallel"),
    )(w, g, m, v)


ROW_B_IN = 0
ROW_VECS = 8
ROW_LOSS = 16
ROW_B_SPATIAL = 24
ROW_W_SPATIAL = 32
SMALL_ROWS = 192
N_VECS = 8


def _pack_small(acc_f, acc_b, dbin_a, dbin_b, dcw8, dws, dbsp):
    cols = D // N_CHIPS

    def body(af_ref, ab_ref, da_ref, db_ref, cw_ref, ws_ref, bs_ref, o_ref, gc_ref):
        o_ref[...] = jnp.zeros_like(o_ref)
        for j in range(D_IN // D):
            src = da_ref if j < 2 else db_ref
            o_ref[ROW_B_IN + j:ROW_B_IN + j + 1, :] = src[0:1, j * D:(j + 1) * D]
        dcw = jnp.sum(cw_ref[...].reshape(HALO, SUBLANES, D), axis=1)
        o_ref[ROW_VECS:ROW_VECS + 1, :] = dcw[CONV_K:CONV_K + 1]
        o_ref[ROW_VECS + 1:ROW_VECS + 5, :] = ab_ref[0:4, :]
        o_ref[ROW_VECS + 5:ROW_VECS + 6, :] = af_ref[2:3, :]
        o_ref[ROW_VECS + 6:ROW_VECS + 8, :] = af_ref[0:2, :]
        o_ref[ROW_LOSS:ROW_LOSS + 1, :] = af_ref[3:4, :]
        head = lax.broadcasted_iota(jnp.int32, (N_GROUPS, D), 0)
        lane = lax.broadcasted_iota(jnp.int32, (N_GROUPS, D), 1)
        indicator = jnp.where(lane // GROUP_W == head, 1.0, 0.0)
        o_ref[ROW_B_SPATIAL:ROW_B_SPATIAL + N_GROUPS, 0:CHUNK] = lax.dot_general(
            indicator, bs_ref[...], (((1,), (1,)), ((), ())), precision=lax.Precision.HIGHEST, preferred_element_type=F32)
        t_idx = lax.broadcasted_iota(jnp.int32, (CHUNK, D), 0)
        s_idx = lax.broadcasted_iota(jnp.int32, (CHUNK, D), 1) % CHUNK
        o_ref[ROW_W_SPATIAL:ROW_W_SPATIAL + CHUNK, :] = jnp.where(s_idx <= t_idx, ws_ref[...], 0.0)
        for h in range(2):
            for j in range(N_CHIPS):
                gc_ref[h, j] = dcw[h * (HALO // 2):(h + 1) * (HALO // 2), j * cols:(j + 1) * cols]

    ins = [acc_f, acc_b, dbin_a, dbin_b, dcw8, dws, dbsp]
    return pl.pallas_call(
        body, name="pack_small",
        in_specs=[_full(a.shape) for a in ins],
        out_specs=[_full((SMALL_ROWS, D)), _full((2, N_CHIPS, HALO // 2, cols))],
        out_shape=[jax.ShapeDtypeStruct((SMALL_ROWS, D), F32), jax.ShapeDtypeStruct((2, N_CHIPS, HALO // 2, cols), F32)],
        compiler_params=_params(),
    )(*ins)


def _adamw_small(parts, vecs, b_in, w_spatial, b_spatial):
    triples = list(vecs) + [b_in, w_spatial, b_spatial]
    n_in = 3 * len(triples)

    def body(p_ref, *refs):
        ins = [refs[3 * i:3 * i + 3] for i in range(len(triples))]
        outs = [refs[n_in + 4 * i:n_in + 4 * i + 4] for i in range(len(triples))]
        loss_ref, g_ref = refs[n_in + 4 * len(triples):]
        rows = SMALL_ROWS // 8
        for k in range(N_CHIPS):
            for core in range(2):
                g_ref[(core * N_CHIPS + k) * rows:(core * N_CHIPS + k + 1) * rows, :] = p_ref[2 * k + core]

        def step(g, wmv, out, get, put):
            d, nm, nv = _adamw_math(get(wmv[0]), g, get(wmv[1]), get(wmv[2]))
            for o, val in zip(out, (g, d, nm, nv)):
                put(o, val)

        for i in range(N_VECS):
            step(g_ref[ROW_VECS + i:ROW_VECS + i + 1, :], ins[i], outs[i],
                 lambda r: r[...].reshape(1, D), lambda o, val: o.__setitem__(Ellipsis, val.reshape(D)))
        for j in range(D_IN // D):
            piece = pl.ds(j * D, D)
            step(g_ref[ROW_B_IN + j:ROW_B_IN + j + 1, :], ins[N_VECS], outs[N_VECS],
                 lambda r: r[piece].reshape(1, D), lambda o, val: o.__setitem__(piece, val.reshape(D)))
        for h in range(N_GROUPS):
            step(g_ref[ROW_W_SPATIAL:ROW_W_SPATIAL + CHUNK, h * CHUNK:(h + 1) * CHUNK], ins[N_VECS + 1], outs[N_VECS + 1],
                 lambda r: r[h], lambda o, val: o.__setitem__(h, val))
        step(g_ref[ROW_B_SPATIAL:ROW_B_SPATIAL + N_GROUPS, 0:CHUNK], ins[N_VECS + 2], outs[N_VECS + 2],
             lambda r: r[...], lambda o, val: o.__setitem__(Ellipsis, val))
        lanes = g_ref[ROW_LOSS:ROW_LOSS + 1, :]
        loss_ref[...] = jnp.broadcast_to(jnp.sum(lanes, axis=1, keepdims=True), (8, 128))

    flat = [a for tr in triples for a in tr]
    out_shape = [jax.ShapeDtypeStruct(tr[0].shape, F32) for tr in triples for _ in range(4)]
    out_shape.append(jax.ShapeDtypeStruct((8, 128), F32))
    res = pl.pallas_call(
        body, name="adamw_small",
        in_specs=[_full(parts.shape)] + [_full(a.shape) for a in flat],
        out_specs=[_full(o.shape) for o in out_shape],
        out_shape=out_shape,
        scratch_shapes=[pltpu.VMEM((SMALL_ROWS, D), F32)],
        compiler_params=_params(),
    )(parts, *flat)
    return [res[4 * i:4 * i + 4] for i in range(len(triples))], res[-1]


SUBLANES = 8
SHIFT_ROWS = HALO - SUBLANES


def _shifted_copies(src_ref, sh_ref, cs, tm):
    for p in range(1, SUBLANES):
        sh_ref[p - 1] = src_ref[pl.ds(p, tm + SHIFT_ROWS), cs]


def _tap(src_ref, sh_ref, cs, offset, start, rows):
    p, q = offset % SUBLANES, offset // SUBLANES
    if p == 0:
        return src_ref[pl.ds(start + SUBLANES * q, rows), cs]
    return sh_ref[p - 1, pl.ds(start + SUBLANES * q, rows), :]


def _conv_taps(src_ref, sh_ref, w_ref, first_offset, step, bias, dst_ref, tm):
    rows = 64
    for g in range(N_GROUPS):
        cs = slice(g * GROUP_W, (g + 1) * GROUP_W)
        _shifted_copies(src_ref, sh_ref, cs, tm)
        for rb in range(tm // rows):
            acc = jnp.zeros((rows, GROUP_W), F32) + (bias[:, cs] if bias is not None else 0.0)
            for k in range(CONV_K):
                acc = acc + w_ref[k:k + 1, cs] * _tap(src_ref, sh_ref, cs, first_offset + step * k, rb * rows, rows)
            dst_ref[rb * rows:(rb + 1) * rows, cs] = acc


def _conv_weight_grad(d_ref, src_ref, sh_ref, first_offset, acc_ref, tm):
    rows = 64
    for g in range(N_GROUPS):
        cs = slice(g * GROUP_W, (g + 1) * GROUP_W)
        _shifted_copies(src_ref, sh_ref, cs, tm)
        for rb in range(tm // rows):
            d = d_ref[rb * rows:(rb + 1) * rows, cs]
            for k in range(CONV_K):
                prod = d * _tap(src_ref, sh_ref, cs, first_offset + k, rb * rows, rows)
                acc_ref[SUBLANES * k:SUBLANES * (k + 1), cs] += jnp.sum(
                    prod.reshape(rows // SUBLANES, SUBLANES, GROUP_W), axis=0)


def _spatial_mix(w_ref, v_bf, tm):
    rows = []
    for q in range(tm // CHUNK):
        cols = [_dot(w_ref[h], v_bf[q * CHUNK:(q + 1) * CHUNK, h * GROUP_W:(h + 1) * GROUP_W])
                for h in range(N_GROUPS)]
        rows.append(jnp.concatenate(cols, axis=1))
    return jnp.concatenate(rows, axis=0)


def _group_norm_fwd(h1, gn_g, gn_b):
    xhat, rstd = [], []
    for g in range(N_GROUPS):
        xh, rs = _norm_stats(h1[:, g * GROUP_W:(g + 1) * GROUP_W])
        xhat.append(xh)
        rstd.append(rs)
    xhat = jnp.concatenate(xhat, axis=1)
    return xhat * gn_g + gn_b, xhat, rstd


def _forward_tiles(p, x, tgt, wpa, wpb, wo, convw, vecs, ws, bsp, tiles_per_seq):
    t = x.shape[0]
    tm = TOKEN_TILE
    hb = tm // HALO

    def body(p_ref, ph_ref, x_ref, t_ref, wpa_ref, wpb_ref, wo_ref, cw_ref, vec_ref, ws_ref, bsp_ref,
             h1_ref, ya_ref, yb_ref, h3_ref, s_ref, mx_ref, dr_ref, drb_ref, xt_ref, acc_ref, he_ref, sh_ref):
        i = pl.program_id(0)
        xt_ref[...] = x_ref[...].T.astype(BF16)
        conv_b, gn_g, gn_b, lnv_g, lnv_b, b_o, lno_g, lno_b = [vec_ref[j:j + 1, :] for j in range(8)]

        keep = jnp.where(i % tiles_per_seq == 0, 0.0, 1.0)
        he_ref[0:HALO, :] = ph_ref[:, 0:D] * _sigmoid(ph_ref[:, D:2 * D]) * keep
        he_ref[HALO:, :] = p_ref[:, 0:D] * _sigmoid(p_ref[:, D:2 * D])
        _conv_taps(he_ref, sh_ref, cw_ref, HALO - (CONV_K - 1), 1, conv_b, h1_ref, tm)
        h2, _, _ = _group_norm_fwd(h1_ref[...], gn_g, gn_b)
        a_gate = p_ref[:, 2 * D:3 * D]
        h3 = ((h2 * _sigmoid(h2)) * (a_gate * _sigmoid(a_gate))).astype(BF16)
        h3_ref[...] = h3
        ya = _dot(h3, wpa_ref[...])
        ya_ref[...] = ya

        u = _gelu(p_ref[:, 3 * D:4 * D])
        vhat, _ = _norm_stats(_gelu(p_ref[:, 4 * D:5 * D]))
        v1 = (vhat * lnv_g + lnv_b).astype(BF16)
        b_gate = p_ref[:, 5 * D:6 * D]
        vmix = _spatial_mix(ws_ref, v1, tm) + jnp.concatenate([bsp_ref[...]] * (tm // CHUNK), axis=0)
        s = (u * vmix * (b_gate * _sigmoid(b_gate))).astype(BF16)
        s_ref[...] = s
        yb = _dot(s, wpb_ref[...])
        yb_ref[...] = yb

        mixed = (_sigmoid(p_ref[:, 6 * D:7 * D]) * ya + _sigmoid(p_ref[:, 7 * D:8 * D]) * yb).astype(BF16)
        mx_ref[...] = mixed
        r = ALPHA * x_ref[...] + (_dot(mixed, wo_ref[...]) + b_o)
        xhat, rstd = _norm_stats(r)
        err = (xhat * lno_g + lno_b) - t_ref[...]
        dr = _norm_bwd(err * (lno_g * (1.0 / D)), xhat, rstd)
        dr_ref[...] = dr
        drb_ref[...] = dr.astype(BF16)

        @pl.when(i == 0)
        def _():
            acc_ref[...] = jnp.zeros_like(acc_ref)

        acc_ref[0:1, :] += _colsum(err * xhat) * (1.0 / D)
        acc_ref[1:2, :] += _colsum(err) * (1.0 / D)
        acc_ref[2:3, :] += _colsum(dr)
        acc_ref[3:4, :] += _colsum(err * err) * (0.5 / D)

    tile = lambda w: pl.BlockSpec((tm, w), lambda i: (i, 0))
    f32_out = jax.ShapeDtypeStruct((t, D), F32)
    bf_out = jax.ShapeDtypeStruct((t, D), BF16)
    return pl.pallas_call(
        body, name="forward_tiles", grid=(t // tm,),
        in_specs=[tile(D_IN),
                  pl.BlockSpec((HALO, 2 * D), lambda i: (jnp.maximum(i * hb - 1, 0), 0)),
                  tile(D), tile(D), _resident((D, D)), _resident((D, D)), _resident((D, D)), _full((HALO, D)), _full((8, D)),
                  _full((N_GROUPS, CHUNK, CHUNK)), _full((CHUNK, D))],
        out_specs=[tile(D)] * 8 + [pl.BlockSpec((D, tm), lambda i: (0, i)), _full((8, D))],
        out_shape=[f32_out, f32_out, f32_out, bf_out, bf_out, bf_out, f32_out, bf_out,
                   jax.ShapeDtypeStruct((D, t), BF16), jax.ShapeDtypeStruct((8, D), F32)],
        scratch_shapes=[pltpu.VMEM((tm + HALO, D), F32), pltpu.VMEM((SUBLANES - 1, tm + SHIFT_ROWS, GROUP_W), F32)],
        compiler_params=_params("arbitrary"),
    )(p, p, x, tgt, wpa, wpb, wo, convw, vecs, ws, bsp)


def _backward_tiles(p, h1, ya, yb, drb, wpa, wpb, wo, vecs, ws, wst, bsp):
    t = h1.shape[0]
    tm = TOKEN_TILE

    def body(p_ref, h1_ref, ya_ref, yb_ref, drb_ref, wpa_ref, wpb_ref, wo_ref, vec_ref, ws_ref, wst_ref, bsp_ref,
             dh1_ref, dp_ref, dya_ref, dyb_ref, acc_ref, dbin_ref, dws_ref, dbsp_ref):
        i = pl.program_id(0)
        _, gn_g, gn_b, lnv_g, lnv_b = [vec_ref[j:j + 1, :] for j in range(5)]

        @pl.when(i == 0)
        def _():
            acc_ref[...] = jnp.zeros_like(acc_ref)
            dbin_ref[...] = jnp.zeros_like(dbin_ref)
            dws_ref[...] = jnp.zeros_like(dws_ref)
            dbsp_ref[...] = jnp.zeros_like(dbsp_ref)

        def emit(block, val):
            dbin_ref[0:1, block * D:(block + 1) * D] += _colsum(val)
            dp_ref[:, block * D:(block + 1) * D] = val.astype(BF16)

        dp_ref[:, 0:2 * D] = jnp.zeros((tm, 2 * D), BF16)
        dmixed = _dot_nt(drb_ref[...], wo_ref[...])
        ga = _sigmoid(p_ref[:, 6 * D:7 * D])
        gb = _sigmoid(p_ref[:, 7 * D:8 * D])
        dya32, dyb32 = dmixed * ga, dmixed * gb
        dya = dya32.astype(BF16)
        dyb = dyb32.astype(BF16)
        dya_ref[...] = dya
        dyb_ref[...] = dyb
        emit(6, dya32 * ya_ref[...] * (1.0 - ga))
        emit(7, dyb32 * yb_ref[...] * (1.0 - gb))

        dh3 = _dot_nt(dya, wpa_ref[...])
        h2, xhat, rstd = _group_norm_fwd(h1_ref[...], gn_g, gn_b)
        sg = _sigmoid(h2)
        a_gate = p_ref[:, 2 * D:3 * D]
        sa = _sigmoid(a_gate)
        silu_h2, silu_a = h2 * sg, a_gate * sa
        dh2 = dh3 * silu_a * (sg + silu_h2 * (1.0 - sg))
        emit(2, dh3 * silu_h2 * (sa + silu_a * (1.0 - sa)))
        acc_ref[0:1, :] += _colsum(dh2 * xhat)
        acc_ref[1:2, :] += _colsum(dh2)
        dxhat = dh2 * gn_g
        for g in range(N_GROUPS):
            cs = slice(g * GROUP_W, (g + 1) * GROUP_W)
            dh1_ref[:, cs] = _norm_bwd(dxhat[:, cs], xhat[:, cs], rstd[g])

        ds = _dot_nt(dyb, wpb_ref[...])
        u_pre = p_ref[:, 3 * D:4 * D]
        u, du_dpre = _gelu_and_grad(u_pre)
        v0, dv_dpre = _gelu_and_grad(p_ref[:, 4 * D:5 * D])
        vhat, vrstd = _norm_stats(v0)
        v1 = (vhat * lnv_g + lnv_b).astype(BF16)
        vmix = _spatial_mix(ws_ref, v1, tm) + jnp.concatenate([bsp_ref[...]] * (tm // CHUNK), axis=0)
        b_gate = p_ref[:, 5 * D:6 * D]
        sb = _sigmoid(b_gate)
        silu_b = b_gate * sb
        emit(3, ds * vmix * silu_b * du_dpre)
        emit(5, ds * u * vmix * (sb + silu_b * (1.0 - sb)))
        dvmix = ds * u * silu_b
        dvmix_bf = dvmix.astype(BF16)
        for q in range(tm // CHUNK):
            dbsp_ref[...] += dvmix[q * CHUNK:(q + 1) * CHUNK, :]
            for h in range(N_GROUPS):
                blk = (slice(q * CHUNK, (q + 1) * CHUNK), slice(h * GROUP_W, (h + 1) * GROUP_W))
                dws_ref[:, h * GROUP_W:(h + 1) * GROUP_W] += _dot_nt(dvmix_bf[blk], v1[blk])
        dv1 = _spatial_mix(wst_ref, dvmix_bf, tm)
        acc_ref[2:3, :] += _colsum(dv1 * vhat)
        acc_ref[3:4, :] += _colsum(dv1)
        emit(4, _norm_bwd(dv1 * lnv_g, vhat, vrstd) * dv_dpre)

    tile = lambda w: pl.BlockSpec((tm, w), lambda i: (i, 0))
    return pl.pallas_call(
        body, name="backward_tiles", grid=(t // tm,),
        in_specs=[tile(D_IN), tile(D), tile(D), tile(D), tile(D), _resident((D, D)), _resident((D, D)), _resident((D, D)),
                  _full((8, D)), _full((N_GROUPS, CHUNK, CHUNK)), _full((N_GROUPS, CHUNK, CHUNK)), _full((CHUNK, D))],
        out_specs=[tile(D), tile(D_IN), tile(D), tile(D), _full((8, D)), _full((8, D_IN)),
                   _full((CHUNK, D)), _full((CHUNK, D))],
        out_shape=[jax.ShapeDtypeStruct((t, D), F32), jax.ShapeDtypeStruct((t, D_IN), BF16),
                   jax.ShapeDtypeStruct((t, D), BF16), jax.ShapeDtypeStruct((t, D), BF16),
                   jax.ShapeDtypeStruct((8, D), F32), jax.ShapeDtypeStruct((8, D_IN), F32),
                   jax.ShapeDtypeStruct((CHUNK, D), F32), jax.ShapeDtypeStruct((CHUNK, D), F32)],
        compiler_params=_params("arbitrary"),
    )(p, h1, ya, yb, drb, wpa, wpb, wo, vecs, ws, wst, bsp)


def _conv_backward(dh1, p, dp, convw, pairs, tiles_per_seq):
    t = dh1.shape[0]
    tm = TOKEN_TILE
    hb = tm // HALO
    last = t // HALO - 1
    n_sq = len(pairs)
    span = 2
    rows = D // 8

    def body(dh1_ref, dnext_ref, p_ref, ph_ref, cw_ref, dp_in_ref, *refs):
        del dp_in_ref
        sq_in = refs[:2 * n_sq]
        dp_ref, dcw_ref, dbin_ref = refs[2 * n_sq:2 * n_sq + 3]
        sq_out = refs[2 * n_sq + 3:3 * n_sq + 3]
        sq_wire = refs[3 * n_sq + 3:4 * n_sq + 3]
        de_ref, he_ref, dh0_ref, sh_ref, acc_ref, wire_ref, sq_sem, wire_sem = refs[4 * n_sq + 3:]
        i = pl.program_id(0)

        @pl.when(i == 0)
        def _():
            dcw_ref[...] = jnp.zeros_like(dcw_ref)
            dbin_ref[...] = jnp.zeros_like(dbin_ref)
            acc_ref[...] = jnp.zeros_like(acc_ref)

        @pl.when(i % span == span - 1)
        def _():
            for a in range(n_sq):
                acc_ref[a] += _dot_tn(sq_in[2 * a][...], sq_in[2 * a + 1][...])

        keep_next = jnp.where(i % tiles_per_seq == tiles_per_seq - 1, 0.0, 1.0)
        de_ref[0:tm, :] = dh1_ref[...]
        de_ref[tm:, :] = dnext_ref[...] * keep_next
        _conv_taps(de_ref, sh_ref, cw_ref, CONV_K - 1, -1, None, dh0_ref, tm)

        keep_prev = jnp.where(i % tiles_per_seq == 0, 0.0, 1.0)
        sg = _sigmoid(p_ref[:, D:2 * D])
        val = p_ref[:, 0:D]
        he_ref[0:HALO, :] = ph_ref[:, 0:D] * _sigmoid(ph_ref[:, D:2 * D]) * keep_prev
        he_ref[HALO:, :] = val * sg
        _conv_weight_grad(dh1_ref, he_ref, sh_ref, HALO - (CONV_K - 1), dcw_ref, tm)
        dcw_ref[SUBLANES * CONV_K:, :] += jnp.sum(dh1_ref[...].reshape(tm // SUBLANES, SUBLANES, D), axis=0)

        dh0 = dh0_ref[...]
        dval = dh0 * sg
        dglu = dval * val * (1.0 - sg)
        dbin_ref[0:1, 0:D] += _colsum(dval)
        dbin_ref[0:1, D:2 * D] += _colsum(dglu)
        dp_ref[:, 0:D] = dval.astype(BF16)
        dp_ref[:, D:2 * D] = dglu.astype(BF16)

        @pl.when(i == t // tm - 1)
        def _():
            cps = [pltpu.make_async_copy(acc_ref.at[a, pl.ds((2 * j + h) * rows, rows)], sq_out[a].at[h, j],
                                         sq_sem.at[(a * N_CHIPS + j) * 2 + h])
                   for a in range(n_sq) for j in range(N_CHIPS) for h in range(2)]
            for cp in cps:
                cp.start()
            for a in range(n_sq):
                wire_ref[...] = acc_ref[a].astype(BF16)
                narrow = [pltpu.make_async_copy(wire_ref.at[pl.ds((2 * j + h) * rows, rows)], sq_wire[a].at[h, j],
                                                wire_sem.at[2 * j + h]) for j in range(N_CHIPS) for h in range(2)]
                for cp in narrow:
                    cp.start()
                for cp in narrow:
                    cp.wait()
            for cp in cps:
                cp.wait()

    any_spec = pl.BlockSpec(memory_space=pl.ANY)
    wide = pl.BlockSpec((span * tm, D), lambda i: (i // span, 0))
    return pl.pallas_call(
        body, name="conv_backward", grid=(t // tm,),
        in_specs=[pl.BlockSpec((tm, D), lambda i: (i, 0)),
                  pl.BlockSpec((HALO, D), lambda i: (jnp.minimum((i + 1) * hb, last), 0)),
                  pl.BlockSpec((tm, 2 * D), lambda i: (i, 0)),
                  pl.BlockSpec((HALO, 2 * D), lambda i: (jnp.maximum(i * hb - 1, 0), 0)),
                  _full((HALO, D)), any_spec] + [wide] * (2 * n_sq),
        out_specs=[pl.BlockSpec((tm, 2 * D), lambda i: (i, 0)), _full((SUBLANES * HALO, D)), _full((8, 2 * D))]
        + [any_spec] * (2 * n_sq),
        out_shape=[jax.ShapeDtypeStruct(dp.shape, BF16), jax.ShapeDtypeStruct((SUBLANES * HALO, D), F32),
                   jax.ShapeDtypeStruct((8, 2 * D), F32)]
        + [jax.ShapeDtypeStruct((2, N_CHIPS, rows, D), F32)] * n_sq
        + [jax.ShapeDtypeStruct((2, N_CHIPS, rows, D), BF16)] * n_sq,
        scratch_shapes=[pltpu.VMEM((tm + HALO, D), F32), pltpu.VMEM((tm + HALO, D), F32), pltpu.VMEM((tm, D), F32),
                        pltpu.VMEM((SUBLANES - 1, tm + SHIFT_ROWS, GROUP_W), F32), pltpu.VMEM((n_sq, D, D), F32),
                        pltpu.VMEM((D, D), BF16), pltpu.SemaphoreType.DMA((n_sq * N_CHIPS * 2,)),
                        pltpu.SemaphoreType.DMA((N_CHIPS * 2,))],
        input_output_aliases={5: 0},
        compiler_params=_params("arbitrary"),
    )(dh1, dh1, p, p, convw, dp, *[a for pair in pairs for a in pair])


def _grad_in_and_x(xt, dp, w4, dr, wires, grads):
    t = dr.shape[0]
    tm = TOKEN_TILE
    half, tn = D // 2, 512
    nb = W_BLOCK // tn
    n_w, n_x = 2 * N_CHIPS * nb, t // tm
    ns = len(grads)
    xi, yi, ci = lax.axis_index("x"), lax.axis_index("y"), lax.axis_index("c")
    others = [2 * (1 - xi) + yi, 2 * xi + (1 - yi), 2 * (1 - xi) + (1 - yi)]
    blocks = others + others + [2 * xi + yi] * 2
    halves = [1 - ci] * 3 + [ci] * 3 + [1 - ci, ci]
    table = jnp.stack([jnp.stack([b * nb + n for b in blocks for n in range(nb)]),
                       jnp.stack([h for h in halves for _ in range(nb)])]).astype(jnp.int32)

    def body(tab_ref, xt_ref, dpc_ref, dpr_ref, w_ref, dr_ref, *refs):
        parts, fulls = refs[:ns], refs[ns:2 * ns]
        dx_ref, qk_ref, b2_ref, b1_ref, wire_ref = refs[2 * ns:2 * ns + 5]
        lands, sibs = refs[2 * ns + 5:3 * ns + 5], refs[3 * ns + 5:4 * ns + 5]
        (g_ref, st_ref, sb_ref, tmp_ref, d2d_send, d2d_recv, ici_send, ici_recv, own_sem, tmp_sem, wire_sem,
         p_send, p_recv, s_send, s_recv) = refs[4 * ns + 5:]
        s = pl.program_id(0)
        x_, y_, c, k = _position()
        chips = _other_chips(x_, y_)
        n = s % nb
        grp = s // nb
        cols = pl.ds(pl.multiple_of(n * tn, tn), tn)

        def part(a, r, core):
            cx, cy = chips[r]
            return pltpu.make_async_remote_copy(
                src_ref=parts[a].at[core, 2 * cx + cy], dst_ref=lands[a].at[2 * r + c],
                send_sem=p_send.at[6 * a + 2 * r + core], recv_sem=p_recv.at[6 * a + 2 * r + c],
                device_id=(cx, cy, core), device_id_type=MESH)

        def landed(a, r, core):
            cx, cy = chips[r]
            return pltpu.make_async_remote_copy(
                src_ref=lands[a].at[2 * r + core], dst_ref=lands[a].at[2 * r + core],
                send_sem=p_send.at[6 * a + 2 * r + core], recv_sem=p_recv.at[6 * a + 2 * r + core],
                device_id=(cx, cy, core), device_id_type=MESH)

        def to_sibling_whole(a):
            return pltpu.make_async_remote_copy(
                src_ref=fulls[a].at[1 - c, k], dst_ref=sibs[a], send_sem=s_send.at[a], recv_sem=s_recv.at[a],
                device_id=(x_, y_, 1 - c), device_id_type=MESH)

        def to_sibling(slot, land):
            return pltpu.make_async_remote_copy(
                src_ref=st_ref.at[slot], dst_ref=b1_ref.at[land, :, cols], send_sem=d2d_send.at[slot],
                recv_sem=d2d_recv.at[land * nb + n], device_id=(x_, y_, 1 - c), device_id_type=MESH)

        def to_chip(r):
            cx, cy = chips[r]
            return pltpu.make_async_remote_copy(
                src_ref=wire_ref.at[r, :, cols], dst_ref=b2_ref.at[r, :, cols], send_sem=ici_send.at[r],
                recv_sem=ici_recv.at[r], device_id=(cx, cy, c), device_id_type=MESH)

        def all_of_chip(r):
            cx, cy = chips[r]
            return pltpu.make_async_remote_copy(
                src_ref=wire_ref.at[r], dst_ref=b2_ref.at[r], send_sem=ici_send.at[r],
                recv_sem=ici_recv.at[r], device_id=(cx, cy, c), device_id_type=MESH)

        def to_result(slot):
            return pltpu.make_async_copy(st_ref.at[slot], qk_ref.at[:, cols], own_sem.at[slot])

        def sibling_piece(land):
            return pltpu.make_async_copy(b1_ref.at[land, :, cols], tmp_ref, tmp_sem)

        @pl.when(s == 0)
        def _():
            for a in range(ns):
                to_sibling_whole(a).start()
                for r in range(3):
                    for core in range(2):
                        part(a, r, core).start()

        own_half = ((grp >= 3) & (grp <= 5)) | (grp == 7)
        land = jnp.where(grp == 7, 3, grp - 3)

        @pl.when(own_half)
        def _():
            to_sibling(0, land).wait_recv()
            sibling_piece(land).start()

        @pl.when(s < n_w)
        def _():
            g_ref[...] = _dot(xt_ref[tab_ref[1, s]], dpc_ref[...])

        @pl.when(own_half)
        def _():
            sibling_piece(land).wait()

        for g in range(2 * N_CHIPS):
            @pl.when(grp == g)
            def _(g=g):
                if g in (0, 1, 2, 6):
                    use = s if g < 3 else 3 * nb + n
                    slot = use % 2

                    @pl.when(use >= 2)
                    def _():
                        to_sibling(slot, 0).wait_send()

                    st_ref[slot] = g_ref[...]
                    to_sibling(slot, min(g, 3)).start()
                elif g in (3, 4, 5):
                    sb_ref[...] = (g_ref[...] + tmp_ref[...]).astype(BF16)
                    stage = pltpu.make_async_copy(sb_ref, wire_ref.at[g - 3, :, cols], wire_sem)
                    stage.start()
                    stage.wait()
                    to_chip(g - 3).start()
                else:
                    slot = n % 2
                    piece = g_ref[...] + tmp_ref[...]

                    @pl.when(n < 2)
                    def _():
                        to_sibling(slot, 0).wait_send()

                    @pl.when(n >= 2)
                    def _():
                        to_result(slot).wait()

                    st_ref[slot] = piece
                    to_result(slot).start()

        @pl.when(s >= n_w)
        def _():
            acc = ALPHA * dr_ref[...]
            for j in range(N_CHIPS):
                acc = acc + _dot_nt(dpr_ref[:, j * W_BLOCK:(j + 1) * W_BLOCK], w_ref[j])
            dx_ref[...] = acc

        @pl.when(s == n_w + n_x - 1)
        def _():
            for slot in range(2):
                to_result(slot).wait()
            for r in range(3):
                all_of_chip(r).wait_recv()
                all_of_chip(r).wait_send()
            for a in range(ns):
                to_sibling_whole(a).wait_recv()
                to_sibling_whole(a).wait_send()
                for r in range(3):
                    for core in range(2):
                        landed(a, r, core).wait_recv()
                        part(a, r, core).wait_send()

    any_spec = pl.BlockSpec(memory_space=pl.ANY)
    tile = lambda s, tab: (jnp.maximum(s - n_w, 0), 0)
    return pl.pallas_call(
        body, name="grad_in_and_x",
        grid_spec=pltpu.PrefetchScalarGridSpec(
            num_scalar_prefetch=1, grid=(n_w + n_x,),
            in_specs=[pl.BlockSpec((2, half, t), lambda s, tab: (0, 0, 0), pipeline_mode=pl.Buffered(1)),
                      pl.BlockSpec((t, tn), lambda s, tab: (0, tab[0, jnp.minimum(s, n_w - 1)])),
                      pl.BlockSpec((tm, D_IN), tile),
                      pl.BlockSpec((N_CHIPS, D, W_BLOCK), lambda s, tab: (0, 0, 0), pipeline_mode=pl.Buffered(1)),
                      pl.BlockSpec((tm, D), tile)] + [any_spec] * (2 * ns),
            out_specs=[pl.BlockSpec((tm, D), tile)] + [any_spec] * (4 + 2 * ns),
            scratch_shapes=[pltpu.VMEM((half, tn), F32), pltpu.VMEM((2, half, tn), F32), pltpu.VMEM((half, tn), BF16),
                            pltpu.VMEM((half, tn), F32),
                            pltpu.SemaphoreType.DMA((2,)), pltpu.SemaphoreType.DMA((N_CHIPS * nb,)),
                            pltpu.SemaphoreType.DMA((3,)), pltpu.SemaphoreType.DMA((3,)),
                            pltpu.SemaphoreType.DMA((2,)), pltpu.SemaphoreType.DMA, pltpu.SemaphoreType.DMA,
                            pltpu.SemaphoreType.DMA((6 * ns,)), pltpu.SemaphoreType.DMA((6 * ns,)),
                            pltpu.SemaphoreType.DMA((ns,)), pltpu.SemaphoreType.DMA((ns,))]),
        out_shape=[jax.ShapeDtypeStruct((t, D), F32), jax.ShapeDtypeStruct((half, W_BLOCK), F32),
                   jax.ShapeDtypeStruct((3, half, W_BLOCK), BF16), jax.ShapeDtypeStruct((N_CHIPS, half, W_BLOCK), F32),
                   jax.ShapeDtypeStruct((3, half, W_BLOCK), BF16)]
        + [jax.ShapeDtypeStruct((6,) + w.shape[2:], w.dtype) for w in wires]
        + [jax.ShapeDtypeStruct(g.shape[2:], F32) for g in grads],
        compiler_params=_params("arbitrary"),
    )(table, xt, dp, dp, w4, dr, *wires, *grads)


def kernel(x, w_in, b_in, conv_w, conv_b, gn_g, gn_b, ln_v_g, ln_v_b, w_spatial, b_spatial, w_pa, w_pb, w_o, b_o, ln_out_g, ln_out_b, loss_target, m_w_in, m_b_in, m_conv_w, m_conv_b, m_gn_g, m_gn_b, m_ln_v_g, m_ln_v_b, m_w_spatial, m_b_spatial, m_w_pa, m_w_pb, m_w_o, m_b_o, m_ln_out_g, m_ln_out_b, v_w_in, v_b_in, v_conv_w, v_conv_b, v_gn_g, v_gn_b, v_ln_v_g, v_ln_v_b, v_w_spatial, v_b_spatial, v_w_pa, v_w_pb, v_w_o, v_b_o, v_ln_out_g, v_ln_out_b):
    n_seq, seq, _ = x.shape
    t = n_seq * seq
    tiles_per_seq = seq // TOKEN_TILE
    x2 = x.reshape(t, D)
    tgt = loss_target.reshape(t, D)

    conv_shard = jnp.pad(conv_w, ((0, HALO - CONV_K), (0, 0)))
    p, win4, wpa4, wpb4, wo4, conv4 = _proj_gather(
        x2, b_in,
        _place_shards([w_in, w_pa, w_pb, w_o, conv_shard], [BF16, BF16, BF16, BF16, F32]))
    win4 = win4.reshape(N_CHIPS, D, W_BLOCK)
    wpa, wpb, wo = wpa4.reshape(D, D), wpb4.reshape(D, D), wo4.reshape(D, D)
    convw = conv4.reshape(N_CHIPS, HALO, D // N_CHIPS).transpose(1, 0, 2).reshape(HALO, D)

    vecs = jnp.stack([conv_b, gn_g, gn_b, ln_v_g, ln_v_b, b_o, ln_out_g, ln_out_b])
    causal = jnp.tril(jnp.ones((CHUNK, CHUNK), bool))
    ws = jnp.where(causal[None], w_spatial, 0.0)
    ws_bf, wst_bf = ws.astype(BF16), ws.transpose(0, 2, 1).astype(BF16)
    bsp = jnp.repeat(b_spatial.T, GROUP_W, axis=1)

    h1, ya, yb, h3, s, mixed, dr, drb, xt, acc_f = _forward_tiles(p, x2, tgt, wpa, wpb, wo, convw, vecs, ws_bf, bsp, tiles_per_seq)
    dh1, dp, dya, dyb, acc_b, dbin_b, dws, dbsp_acc = _backward_tiles(p, h1, ya, yb, drb, wpa, wpb, wo, vecs, ws_bf, wst_bf, bsp)
    dp, dcw8, dbin_a, *square = _conv_backward(dh1, p, dp, convw, [(h3, dya), (s, dyb), (mixed, drb)], tiles_per_seq)

    small, g_conv = _pack_small(acc_f, acc_b, dbin_a, dbin_b, dcw8, dws, dbsp_acc)
    small = small.reshape(2, N_CHIPS, SMALL_ROWS // 8, D)

    grads = square[:3] + [g_conv, small]
    wires = square[3:] + [g_conv, small]
    grad_x, q_in, chips_in, _, _, *landed = _grad_in_and_x(xt.reshape(2, D // 2, t), dp, win4, dr, wires, grads)
    grad_x = grad_x.reshape(x.shape)
    mine = [_add_chips(q_in, chips_in)]
    mine += _add_devices(grads, landed[:5], landed[5:])
    *full, small_parts = _share_results(mine[:5], mine[5])
    grad_w_in, grad_w_pa, grad_w_pb, grad_w_o = [f.reshape(w.shape) for f, w in zip(full[:4], (w_in, w_pa, w_pb, w_o))]
    grad_conv_w = full[4].reshape(HALO, D // N_CHIPS)[:CONV_K]

    big = {"w_in": (grad_w_in,) + tuple(_adamw(w_in, grad_w_in, m_w_in, v_w_in))}
    group = _adamw_group([(w_pa, grad_w_pa, m_w_pa, v_w_pa), (w_pb, grad_w_pb, m_w_pb, v_w_pb),
                          (w_o, grad_w_o, m_w_o, v_w_o)], (conv_w, grad_conv_w, m_conv_w, v_conv_w))
    for name, g, res in zip(["w_pa", "w_pb", "w_o", "conv_w"], [grad_w_pa, grad_w_pb, grad_w_o, grad_conv_w], group):
        big[name] = (g,) + tuple(res)
    vec_names = ["conv_b", "gn_g", "gn_b", "ln_v_g", "ln_v_b", "b_o", "ln_out_g", "ln_out_b"]
    vec_triples = [(conv_b, m_conv_b, v_conv_b), (gn_g, m_gn_g, v_gn_g), (gn_b, m_gn_b, v_gn_b),
                   (ln_v_g, m_ln_v_g, v_ln_v_g), (ln_v_b, m_ln_v_b, v_ln_v_b), (b_o, m_b_o, v_b_o),
                   (ln_out_g, m_ln_out_g, v_ln_out_g), (ln_out_b, m_ln_out_b, v_ln_out_b)]
    small_res, loss8 = _adamw_small(
        small_parts.reshape(8, SMALL_ROWS // 8, D), vec_triples, (b_in, m_b_in, v_b_in),
        (w_spatial, m_w_spatial, v_w_spatial), (b_spatial, m_b_spatial, v_b_spatial))
    per_name = dict(zip(vec_names + ["b_in", "w_spatial", "b_spatial"], small_res))

    order = ["w_in", "b_in", "conv_w", "conv_b", "gn_g", "gn_b", "ln_v_g", "ln_v_b", "w_spatial", "b_spatial",
             "w_pa", "w_pb", "w_o", "b_o", "ln_out_g", "ln_out_b"]
    outs = [loss8[0, 0], grad_x]
    for kind in range(4):
        outs += [big[n][kind] if n in big else per_name[n][kind] for n in order]
    return tuple(outs)
```
